```python
import math
import jax, jax.numpy as jnp
from jax import lax
import numpy as np

D_MODEL = 1024
BATCH = 16
SEQ = 256
DEPTH = 2
DEC_BATCH = 4
DEC_SEQ = 1024
PAST_LEN = 512

GRID_W = 64
N_EVEN = (DEPTH + 1) // 2
N_ODD = DEPTH // 2
N_MOD = 6
EPS = 1e-6
NEG_INF = -1e30
ROPE_BASE = 10000.0
Q_BLOCK = 128
HEAD_DIM = 64
A_HEADS = 8
A_KV_HEADS = 2
A_WINDOW = 128
A_WIDTH = A_HEADS * HEAD_DIM
A_KV_WIDTH = A_KV_HEADS * HEAD_DIM
S5_WIDTH = D_MODEL // 2
S5_GROUP = 16
S5_GROUPS = S5_WIDTH // S5_GROUP
S5_STATE = 64
MLA_HEADS = 8
MLA_Q_RANK = D_MODEL // 4
MLA_KV_RANK = D_MODEL // 8
MLA_NOPE = 64
MLA_ROPE = 32
MLA_V = 64
MLA_WIDTH = MLA_HEADS * MLA_V
HG_HEADS = 8
HG_KEY = 64
HG_VAL = 64
HG_KD = HG_HEADS * HG_KEY
HG_WIDTH = HG_HEADS * HG_VAL
HG_CHUNK = 16
N_EXPERTS = 16
N_GROUPS = 4
EXPERTS_PER_GROUP = N_EXPERTS // N_GROUPS
TOPK_GROUP = 1
TOP_K = 2
D_FF = D_MODEL // 2
AB_SPLITS = (A_WIDTH, A_KV_WIDTH, A_KV_WIDTH, S5_WIDTH)
CD_SPLITS = (MLA_Q_RANK, MLA_KV_RANK, MLA_ROPE, HG_KD, HG_KD, HG_KD, HG_WIDTH, HG_WIDTH)
D_IN_AB = sum(AB_SPLITS)
D_IN_CD = sum(CD_SPLITS)
F32 = jnp.float32

kernel_name = 'hybrid_prefix_diffusion_step'


def rmsnorm(x, g):
    xf = x.astype(F32)
    y = xf * lax.rsqrt(jnp.mean(xf * xf, axis=-1, keepdims=True) + EPS)
    return (y * g.astype(F32)).astype(x.dtype)


def split_cols(z, sizes):
    cuts = [sum(sizes[:i + 1]) for i in range(len(sizes) - 1)]
    return jnp.split(z, cuts, axis=-1)


def axial_rope(length, rot_dim):
    n_rows = length // GRID_W
    rows = jnp.repeat(jnp.arange(n_rows), GRID_W).astype(F32)
    cols = jnp.tile(jnp.arange(GRID_W), n_rows).astype(F32)
    n_freq = rot_dim // 4
    inv = ROPE_BASE ** (-jnp.arange(n_freq, dtype=F32) / n_freq)
    ang_r = rows[:, None] * inv[None, :]
    ang_c = cols[:, None] * inv[None, :]
    ang = jnp.concatenate([ang_r, ang_r, ang_c, ang_c], axis=-1)
    return jnp.cos(ang), jnp.sin(ang)


def apply_axial_rope(x, cos, sin):
    x1, x2, x3, x4 = jnp.split(x, 4, axis=-1)
    rot = jnp.concatenate([-x2, x1, -x4, x3], axis=-1)
    return (x * cos[:, None, :] + rot * sin[:, None, :]).astype(x.dtype)


def block_attention(q, k, v, sink):
    B, Lq, H, Dk = q.shape
    Hk = k.shape[2]
    G = H // Hk
    Dv = v.shape[-1]
    scale = Dk ** -0.5
    nq = Lq // Q_BLOCK
    qb = jnp.moveaxis(q.reshape(B, nq, Q_BLOCK, Hk, G, Dk), 1, 0)

    def attend(qblk):
        s = jnp.einsum('bqkgd,bskd->bkgqs', qblk, k).astype(F32) * scale
        if sink is not None:
            sk = jnp.broadcast_to(sink.astype(F32).reshape(1, Hk, G, 1, 1), s.shape[:-1] + (1,))
            p = jax.nn.softmax(jnp.concatenate([s, sk], axis=-1), axis=-1)[..., :-1]
        else:
            p = jax.nn.softmax(s, axis=-1)
        return jnp.einsum('bkgqs,bskd->bqkgd', p.astype(v.dtype), v)

    o = lax.map(attend, qb)
    return jnp.moveaxis(o, 0, 1).reshape(B, Lq, H, Dv)


def window_attention(q, k, v, k_ctx, v_ctx, sink):
    B, L, H, Dh = q.shape
    Hk = k.shape[2]
    G = H // Hk
    W = A_WINDOW
    nb = L // W
    Lc = k_ctx.shape[1]
    scale = Dh ** -0.5
    pad = ((0, 0), (W, W), (0, 0), (0, 0))
    win = jnp.arange(nb)[:, None] * W + jnp.arange(3 * W)[None, :]
    kb = jnp.pad(k, pad)[:, win]
    vb = jnp.pad(v, pad)[:, win]
    qb = q.reshape(B, nb, W, Hk, G, Dh)
    kpos = win - W
    qpos = jnp.arange(nb)[:, None] * W + jnp.arange(W)[None, :]
    valid = ((jnp.abs(kpos[:, None, :] - qpos[:, :, None]) <= W)
             & (kpos >= 0)[:, None, :] & (kpos < L)[:, None, :])
    s_loc = jnp.einsum('bnqkgd,bnskd->bnkgqs', qb, kb).astype(F32) * scale
    s_loc = jnp.where(valid[None, :, None, None], s_loc, NEG_INF)
    s_ctx = jnp.einsum('bnqkgd,bckd->bnkgqc', qb, k_ctx).astype(F32) * scale
    s_sink = jnp.broadcast_to(sink.astype(F32).reshape(1, 1, Hk, G, 1, 1), s_loc.shape[:-1] + (1,))
    p = jax.nn.softmax(jnp.concatenate([s_loc, s_ctx, s_sink], axis=-1), axis=-1).astype(v.dtype)
    o = (jnp.einsum('bnkgqs,bnskd->bnqkgd', p[..., :3 * W], vb)
         + jnp.einsum('bnkgqc,bckd->bnqkgd', p[..., 3 * W:3 * W + Lc], v_ctx))
    return o.reshape(B, L, H, Dh)


def complex_diag_scan(a_re, a_im, b_re, b_im):
    def combine(x, y):
        a1r, a1i, b1r, b1i = x
        a2r, a2i, b2r, b2i = y
        return (a2r * a1r - a2i * a1i, a2r * a1i + a2i * a1r,
                a2r * b1r - a2i * b1i + b2r, a2r * b1i + a2i * b1r + b2i)
    _, _, h_re, h_im = lax.associative_scan(combine, (a_re, a_im, b_re, b_im), axis=1)
    return h_re, h_im


def s5_mixer(u, h0_re, h0_im, lam_re, lam_im, log_dt, b_re, b_im, c_re, c_im, d_skip, w_glu):
    Bt, L, _ = u.shape
    uf = u.astype(F32).reshape(Bt, L, S5_GROUPS, S5_GROUP)
    lr, li = lam_re.astype(F32), lam_im.astype(F32)
    dt = jnp.exp(log_dt.astype(F32))[..., None]
    mag = jnp.exp(lr * dt)
    ar, ai = mag * jnp.cos(li * dt), mag * jnp.sin(li * dt)
    den = lr * lr + li * li
    zr = ((ar - 1.0) * lr + ai * li) / den
    zi = (ai * lr - (ar - 1.0) * li) / den
    br, bi = b_re.astype(F32), b_im.astype(F32)
    bbr = zr[..., None] * br - zi[..., None] * bi
    bbi = zr[..., None] * bi + zi[..., None] * br
    y = jnp.zeros_like(uf)
    fin_re, fin_im = [], []
    for d in range(2):
        ud = uf if d == 0 else uf[:, ::-1]
        xr = jnp.einsum('blgp,gnp->blgn', ud, bbr[d])
        xi = jnp.einsum('blgp,gnp->blgn', ud, bbi[d])
        h0r, h0i = h0_re[:, d].astype(F32), h0_im[:, d].astype(F32)
        xr = xr.at[:, 0].add(ar[d] * h0r - ai[d] * h0i)
        xi = xi.at[:, 0].add(ar[d] * h0i + ai[d] * h0r)
        hr, hi = complex_diag_scan(jnp.broadcast_to(ar[d], xr.shape), jnp.broadcast_to(ai[d], xr.shape), xr, xi)
        yd = (jnp.einsum('blgn,gpn->blgp', hr, c_re[d].astype(F32))
              - jnp.einsum('blgn,gpn->blgp', hi, c_im[d].astype(F32)))
        y = y + (yd if d == 0 else yd[:, ::-1])
        fin_re.append(hr[:, -1])
        fin_im.append(hi[:, -1])
    y = y.reshape(Bt, L, S5_WIDTH) + d_skip.astype(F32) * u.astype(F32)
    g = jax.nn.gelu(y)
    out = g * jax.nn.sigmoid(jnp.matmul(g, w_glu.astype(F32)))
    return out.astype(u.dtype), jnp.stack(fin_re, 1).astype(u.dtype), jnp.stack(fin_im, 1).astype(u.dtype)


def gla_chunked(q, k, v, logf, s0):
    Bt, L, H, K = q.shape
    V = v.shape[-1]
    C = HG_CHUNK
    nc = L // C

    def chunks(t):
        return t.reshape(Bt, nc, C, H, t.shape[-1]).transpose(1, 0, 3, 2, 4)

    qc, kc, vc, gc = chunks(q), chunks(k), chunks(v), chunks(logf)
    b = jnp.cumsum(gc, axis=3)
    causal = jnp.tril(jnp.ones((C, C), dtype=bool))
    diff = b[:, :, :, :, None, :] - b[:, :, :, None, :, :]
    decay = jnp.exp(jnp.where(causal[:, :, None], diff, -jnp.inf))
    scores = jnp.einsum('nbhtk,nbhsk,nbhtsk->nbhts', qc, kc, decay)
    intra = jnp.einsum('nbhts,nbhsv->nbhtv', scores, vc)
    b_last = b[:, :, :, -1]
    ds = jnp.einsum('nbhsk,nbhsv->nbhkv', kc * jnp.exp(b_last[:, :, :, None] - b), vc)

    def step(S, inp):
        bl, dsc = inp
        return jnp.exp(bl)[..., None] * S + dsc, S

    s_fin, s_start = lax.scan(step, s0, (b_last, ds))
    inter = jnp.einsum('nbhtk,nbhkv->nbhtv', qc * jnp.exp(b), s_start)
    o = (intra + inter).transpose(1, 0, 3, 2, 4).reshape(Bt, L, H, V)
    return o, s_fin


def hgrn2_mixer(hq, hf_fwd, hf_bwd, hi, hg, s0, lb_fwd, lb_bwd, g_o):
    Bt, L, _ = hq.shape

    def heads(t, dim):
        return t.astype(F32).reshape(Bt, L, HG_HEADS, dim)

    q = jax.nn.silu(heads(hq, HG_KEY))
    v = heads(hi, HG_VAL)
    s0f = s0.astype(F32)
    o_sum = jnp.zeros((Bt, L, HG_HEADS, HG_VAL), F32)
    finals = []
    for d, (hf, lb) in enumerate(((hf_fwd, lb_fwd), (hf_bwd, lb_bwd))):
        lbh = lb.astype(F32).reshape(HG_HEADS, HG_KEY)
        f = lbh + (1.0 - lbh) * jax.nn.sigmoid(heads(hf, HG_KEY))
        args = (q, 1.0 - f, v, jnp.log(f))
        if d == 1:
            args = tuple(t[:, ::-1] for t in args)
        o, s_fin = gla_chunked(args[0], args[1], args[2], args[3], s0f[:, d])
        o_sum = o_sum + (o if d == 0 else o[:, ::-1])
        finals.append(s_fin)
    o = rmsnorm(o_sum, g_o) * jax.nn.silu(heads(hg, HG_VAL))
    return o.reshape(Bt, L, HG_WIDTH).astype(hq.dtype), jnp.stack(finals, 1).astype(hq.dtype)


def mla_expand_kv(ckv, kpe, w_kv_up):
    B, L, _ = ckv.shape
    kv = jnp.matmul(ckv, w_kv_up).reshape(B, L, MLA_HEADS, MLA_NOPE + MLA_V)
    k = jnp.concatenate([kv[..., :MLA_NOPE], jnp.broadcast_to(kpe, (B, L, MLA_HEADS, MLA_ROPE))], axis=-1)
    return k, kv[..., MLA_NOPE:]


def moe(h, router_w, router_b, w_gate, w_up, w_down):
    Bt, L, D = h.shape
    t = h.reshape(Bt * L, D)
    T = t.shape[0]
    aff = jax.nn.sigmoid(jnp.matmul(t, router_w).astype(F32))
    sel = aff + router_b.astype(F32)
    grp = sel.reshape(T, N_GROUPS, EXPERTS_PER_GROUP)
    g_score = jnp.sum(lax.top_k(grp, TOP_K)[0], axis=-1)
    _, g_idx = lax.top_k(g_score, TOPK_GROUP)
    g_mask = jnp.sum(jax.nn.one_hot(g_idx, N_GROUPS, dtype=F32), axis=1) > 0
    e_mask = jnp.repeat(g_mask, EXPERTS_PER_GROUP, axis=1)
    _, e_idx = lax.top_k(jnp.where(e_mask, sel, -jnp.inf), TOP_K)
    w = jnp.take_along_axis(aff, e_idx, axis=1)
    w = w / jnp.sum(w, axis=-1, keepdims=True)
    gates = jnp.sum(jax.nn.one_hot(e_idx, N_EXPERTS, dtype=F32) * w[..., None], axis=1)
    hid = jax.nn.silu(jnp.einsum('td,edf->tef', t, w_gate)) * jnp.einsum('td,edf->tef', t, w_up)
    y = jnp.einsum('tef,efd->td', hid * gates[..., None].astype(hid.dtype), w_down)
    return y.reshape(Bt, L, D)


def setup_inputs(seed: int = 0) -> dict:
    key = jax.random.key(seed)
    ks = iter(jax.random.split(key, 48))

    def nrm(shape, scale):
        return jax.random.normal(next(ks), shape, F32) * scale

    def gain(shape):
        return 1.0 + 0.02 * jax.random.normal(next(ks), shape, F32)

    s5_n = jnp.pi * jnp.arange(S5_STATE, dtype=F32)
    return {
        'x_prompt': nrm((BATCH, SEQ, D_MODEL), 1.0),
        'x_sample': nrm((DEC_BATCH, DEC_SEQ, D_MODEL), 1.0),
        'cache_attn_k': nrm((DEC_BATCH, N_EVEN, PAST_LEN, A_KV_HEADS, HEAD_DIM), 1.0),
        'cache_attn_v': nrm((DEC_BATCH, N_EVEN, PAST_LEN, A_KV_HEADS, HEAD_DIM), 1.0),
        'state_ssm_re': nrm((DEC_BATCH, N_EVEN, 2, S5_GROUPS, S5_STATE), 0.1),
        'state_ssm_im': nrm((DEC_BATCH, N_EVEN, 2, S5_GROUPS, S5_STATE), 0.1),
        'cache_mla_ckv': nrm((DEC_BATCH, N_ODD, PAST_LEN, MLA_KV_RANK), 1.0),
        'cache_mla_kpe': nrm((DEC_BATCH, N_ODD, PAST_LEN, MLA_ROPE), 1.0),
        'state_hgrn': nrm((DEC_BATCH, N_ODD, 2, HG_HEADS, HG_KEY, HG_VAL), 0.5),
        'c': nrm((DEC_BATCH, D_MODEL), 1.0),
        'c_ctx': nrm((D_MODEL,), 1.0),
        'w_mod': nrm((DEPTH, D_MODEL, N_MOD * D_MODEL), 0.5 * D_MODEL ** -0.5),
        'b_mod': nrm((DEPTH, N_MOD * D_MODEL), 0.02),
        'g_mix': gain((DEPTH, D_MODEL)),
        'g_ffn': gain((DEPTH, D_MODEL)),
        'g_final': gain((D_MODEL,)),
        'router_w': nrm((D_MODEL, N_EXPERTS), D_MODEL ** -0.5),
        'router_b': nrm((N_EXPERTS,), 0.01),
        'moe_w_gate': nrm((DEPTH, N_EXPERTS, D_MODEL, D_FF), D_MODEL ** -0.5),
        'moe_w_up': nrm((DEPTH, N_EXPERTS, D_MODEL, D_FF), D_MODEL ** -0.5),
        'moe_w_down': nrm((DEPTH, N_EXPERTS, D_FF, D_MODEL), D_FF ** -0.5),
        'ab_w_in': nrm((N_EVEN, D_MODEL, D_IN_AB), D_MODEL ** -0.5),
        'ab_sink': nrm((N_EVEN, A_HEADS), 0.5),
        's5_lam_re': -0.5 + nrm((N_EVEN, 2, S5_GROUPS, S5_STATE), 0.01),
        's5_lam_im': s5_n + nrm((N_EVEN, 2, S5_GROUPS, S5_STATE), 0.01),
        's5_log_dt': jax.random.uniform(next(ks), (N_EVEN, 2, S5_GROUPS), F32, math.log(1e-3), math.log(1e-1)),
        's5_b_re': nrm((N_EVEN, 2, S5_GROUPS, S5_STATE, S5_GROUP), (2 * S5_GROUP) ** -0.5),
        's5_b_im': nrm((N_EVEN, 2, S5_GROUPS, S5_STATE, S5_GROUP), (2 * S5_GROUP) ** -0.5),
        's5_c_re': nrm((N_EVEN, 2, S5_GROUPS, S5_GROUP, S5_STATE), S5_STATE ** -0.5),
        's5_c_im': nrm((N_EVEN, 2, S5_GROUPS, S5_GROUP, S5_STATE), S5_STATE ** -0.5),
        's5_d': nrm((N_EVEN, S5_WIDTH), 1.0),
        's5_w_glu': nrm((N_EVEN, S5_WIDTH, S5_WIDTH), S5_WIDTH ** -0.5),
        'ab_w_out': nrm((N_EVEN, A_WIDTH + S5_WIDTH, D_MODEL), (A_WIDTH + S5_WIDTH) ** -0.5),
        'cd_w_in': nrm((N_ODD, D_MODEL, D_IN_CD), D_MODEL ** -0.5),
        'mla_g_q': gain((N_ODD, MLA_Q_RANK)),
        'mla_w_q_up': nrm((N_ODD, MLA_Q_RANK, MLA_HEADS * (MLA_NOPE + MLA_ROPE)), MLA_Q_RANK ** -0.5),
        'mla_g_kv': gain((N_ODD, MLA_KV_RANK)),
        'mla_w_kv_up': nrm((N_ODD, MLA_KV_RANK, MLA_HEADS * (MLA_NOPE + MLA_V)), MLA_KV_RANK ** -0.5),
        'hg_lower_bounds': nrm((2, DEPTH, HG_KD), 1.0),
        'hg_g_o': gain((N_ODD, HG_VAL)),
        'cd_w_out': nrm((N_ODD, MLA_WIDTH + HG_WIDTH, D_MODEL), (MLA_WIDTH + HG_WIDTH) ** -0.5),
    }


def reference(x_prompt, x_sample, cache_attn_k, cache_attn_v, state_ssm_re, state_ssm_im,
              cache_mla_ckv, cache_mla_kpe, state_hgrn, c, c_ctx,
              w_mod, b_mod, g_mix, g_ffn, g_final, router_w, router_b,
              moe_w_gate, moe_w_up, moe_w_down,
              ab_w_in, ab_sink, s5_lam_re, s5_lam_im, s5_log_dt, s5_b_re, s5_b_im,
              s5_c_re, s5_c_im, s5_d, s5_w_glu, ab_w_out,
              cd_w_in, mla_g_q, mla_w_q_up, mla_g_kv, mla_w_kv_up, hg_lower_bounds, hg_g_o, cd_w_out):
    lbp = jax.nn.softmax(hg_lower_bounds.astype(F32), axis=1)
    lb = jnp.cumsum(lbp, axis=1) - lbp[:, :1]

    def trunk(x, cond, ctx_side):
        Bt, L, _ = x.shape
        is_ctx = ctx_side is None
        if not is_ctx:
            cos_a, sin_a = axial_rope(L, HEAD_DIM)
            cos_m, sin_m = axial_rope(L, MLA_ROPE)
            ck, cv, sre, sim, cckv, ckpe, shg = ctx_side
        keep = ([], [], [], [], [], [], [])
        for l in range(DEPTH):
            j = l // 2
            mod = jnp.matmul(jax.nn.silu(cond), w_mod[l]) + b_mod[l]
            mod = mod[None, None] if is_ctx else mod[:, None]
            sh1, sc1, gt1, sh2, sc2, gt2 = jnp.split(mod, N_MOD, axis=-1)
            h = rmsnorm(x, g_mix[l]) * (1.0 + sc1) + sh1
            if l % 2 == 0:
                q, k, v, u = split_cols(jnp.matmul(h, ab_w_in[j]), AB_SPLITS)
                q = q.reshape(Bt, L, A_HEADS, HEAD_DIM)
                k = k.reshape(Bt, L, A_KV_HEADS, HEAD_DIM)
                v = v.reshape(Bt, L, A_KV_HEADS, HEAD_DIM)
                if is_ctx:
                    o_a = block_attention(q, k, v, ab_sink[j])
                    h0r = jnp.zeros((Bt, 2, S5_GROUPS, S5_STATE), x.dtype)
                    h0i = h0r
                else:
                    o_a = window_attention(apply_axial_rope(q, cos_a, sin_a), apply_axial_rope(k, cos_a, sin_a),
                                           v, ck[:, j], cv[:, j], ab_sink[j])
                    h0r, h0i = sre[:, j], sim[:, j]
                o_b, fr, fi = s5_mixer(u, h0r, h0i, s5_lam_re[j], s5_lam_im[j], s5_log_dt[j], s5_b_re[j],
                                       s5_b_im[j], s5_c_re[j], s5_c_im[j], s5_d[j], s5_w_glu[j])
                out = jnp.matmul(jnp.concatenate([o_a.reshape(Bt, L, A_WIDTH), o_b], axis=-1), ab_w_out[j])
                if is_ctx:
                    keep[0].append(k)
                    keep[1].append(v)
                    keep[2].append(fr)
                    keep[3].append(fi)
            else:
                cq, ckv, kpe, hq, hff, hfb, hi, hg = split_cols(jnp.matmul(h, cd_w_in[j]), CD_SPLITS)
                q = jnp.matmul(rmsnorm(cq, mla_g_q[j]), mla_w_q_up[j]).reshape(Bt, L, MLA_HEADS, MLA_NOPE + MLA_ROPE)
                ckv = rmsnorm(ckv, mla_g_kv[j])
                kpe4 = kpe[:, :, None, :]
                if is_ctx:
                    k_m, v_m = mla_expand_kv(ckv, kpe4, mla_w_kv_up[j])
                    o_c = block_attention(q, k_m, v_m, None)
                    s0 = jnp.zeros((Bt, 2, HG_HEADS, HG_KEY, HG_VAL), x.dtype)
                else:
                    q = jnp.concatenate([q[..., :MLA_NOPE], apply_axial_rope(q[..., MLA_NOPE:], cos_m, sin_m)], axis=-1)
                    k_l, v_l = mla_expand_kv(ckv, apply_axial_rope(kpe4, cos_m, sin_m), mla_w_kv_up[j])
                    k_c, v_c = mla_expand_kv(cckv[:, j], ckpe[:, j][:, :, None, :], mla_w_kv_up[j])
                    o_c = block_attention(q, jnp.concatenate([k_c, k_l], axis=1),
                                          jnp.concatenate([v_c, v_l], axis=1), None)
                    s0 = shg[:, j]
                o_d, s_fin = hgrn2_mixer(hq, hff, hfb, hi, hg, s0, lb[0, l], lb[1, l], hg_g_o[j])
                out = jnp.matmul(jnp.concatenate([o_c.reshape(Bt, L, MLA_WIDTH), o_d], axis=-1), cd_w_out[j])
                if is_ctx:
                    keep[4].append(ckv)
                    keep[5].append(kpe)
                    keep[6].append(s_fin)
            x = x + gt1 * out
            h = rmsnorm(x, g_ffn[l]) * (1.0 + sc2) + sh2
            x = x + gt2 * moe(h, router_w, router_b, moe_w_gate[l], moe_w_up[l], moe_w_down[l])
        return rmsnorm(x, g_final), keep

    y_prompt, kept = trunk(x_prompt, c_ctx, None)
    y_sample, _ = trunk(x_sample, c, (cache_attn_k, cache_attn_v, state_ssm_re, state_ssm_im,
                                      cache_mla_ckv, cache_mla_kpe, state_hgrn))
    return (y_prompt, y_sample,
            jnp.stack(kept[0], 1), jnp.stack(kept[1], 1), jnp.stack(kept[2], 1), jnp.stack(kept[3], 1),
            jnp.stack(kept[4], 1), jnp.stack(kept[5], 1), jnp.stack(kept[6], 1))
```

```python
import functools
import math

import numpy as np
import jax
import jax.numpy as jnp
from jax import lax
from jax.experimental import pallas as pl
from jax.experimental.pallas import tpu as pltpu

F32 = jnp.float32
BF16 = jnp.bfloat16

D_MODEL = 1024
BATCH = 16
SEQ = 256
DEPTH = 2
DEC_BATCH = 4
DEC_SEQ = 1024
PAST_LEN = 512
GRID_W = 64
N_MOD = 6
EPS = 1e-6
NEG_INF = -1e30
ROPE_BASE = 10000.0
HEAD_DIM = 64
A_HEADS = 8
A_KV_HEADS = 2
A_WINDOW = 128
A_WIDTH = A_HEADS * HEAD_DIM
A_KV_WIDTH = A_KV_HEADS * HEAD_DIM
S5_WIDTH = D_MODEL // 2
S5_GROUP = 16
S5_GROUPS = S5_WIDTH // S5_GROUP
S5_STATE = 64
MLA_HEADS = 8
MLA_Q_RANK = D_MODEL // 4
MLA_KV_RANK = D_MODEL // 8
MLA_NOPE = 64
MLA_ROPE = 32
MLA_V = 64
MLA_WIDTH = MLA_HEADS * MLA_V
HG_HEADS = 8
HG_KEY = 64
HG_VAL = 64
HG_KD = HG_HEADS * HG_KEY
HG_WIDTH = HG_HEADS * HG_VAL
HG_CHUNK = 16
N_EXPERTS = 16
N_GROUPS = 4
EXPERTS_PER_GROUP = N_EXPERTS // N_GROUPS
D_FF = D_MODEL // 2

N_CTX_TOK = BATCH * SEQ
N_LAT_TOK = DEC_BATCH * DEC_SEQ
N_TOK = N_CTX_TOK + N_LAT_TOK
ROW_BLK = 256
N_ROW_BLK = N_TOK // ROW_BLK
N_CTX_BLK = N_CTX_TOK // ROW_BLK
LAT_BLK_PER_SEQ = DEC_SEQ // ROW_BLK
MOD_ROWS = 8
VMEM_LIMIT = 56 * 1024 * 1024


def _cparams(*sem):
    return pltpu.CompilerParams(dimension_semantics=sem, vmem_limit_bytes=VMEM_LIMIT)


def _mod_group(i):
    return jnp.where(i < N_CTX_BLK, 0, 1 + (i - N_CTX_BLK) // LAT_BLK_PER_SEQ)


def _mod_spec(layer, which):
    return pl.BlockSpec((1, 1, D_MODEL), lambda i: ((layer * MOD_ROWS + _mod_group(i)) * N_MOD + which, 0, 0))


def _rope_blk(i):
    return jnp.where(i < N_CTX_BLK, 0, 1 + (i - N_CTX_BLK) % LAT_BLK_PER_SEQ)


def _rope_tables(rot_dim):
    n_freq = rot_dim // 4
    t = np.arange(DEC_SEQ)
    rows = (t // GRID_W).astype(np.float32)
    cols = (t % GRID_W).astype(np.float32)
    inv = (np.float32(ROPE_BASE) ** (-np.arange(n_freq, dtype=np.float32) / np.float32(n_freq))).astype(np.float32)
    ang_r = rows[:, None] * inv[None, :]
    ang_c = cols[:, None] * inv[None, :]
    ang = np.concatenate([ang_r, ang_r, ang_c, ang_c], axis=-1).astype(np.float32)
    reps = 128 // rot_dim
    cos = np.tile(np.cos(ang), (1, reps)).astype(np.float32)
    sin = np.tile(np.sin(ang), (1, reps)).astype(np.float32)
    lane = np.arange(128)
    first = (lane % (2 * n_freq)) < n_freq
    sin_a = np.where(first[None, :], -sin, 0.0).astype(np.float32)
    sin_b = np.where(first[None, :], 0.0, sin).astype(np.float32)
    ident = np.zeros((ROW_BLK, 128), np.float32)
    cos = np.concatenate([ident + 1.0, cos], axis=0)
    sin_a = np.concatenate([ident, sin_a], axis=0)
    sin_b = np.concatenate([ident, sin_b], axis=0)
    return jnp.asarray(cos), jnp.asarray(sin_a), jnp.asarray(sin_b)


def _apply_rope(x, cos, sin_a, sin_b, quarter):
    outs = []
    for j in range(x.shape[1] // 128):
        xt = x[:, 128 * j:128 * (j + 1)]
        up = pltpu.roll(xt, 128 - quarter, axis=1)
        dn = pltpu.roll(xt, quarter, axis=1)
        outs.append(xt * cos + up * sin_a + dn * sin_b)
    return outs[0] if len(outs) == 1 else jnp.concatenate(outs, axis=1)


def _rms(x):
    return x * lax.rsqrt(jnp.mean(x * x, axis=-1, keepdims=True) + EPS)


def _norm_mod(x, g, sc, sh):
    return _rms(x) * g * (1.0 + sc) + sh


def _dot(a, b):
    return jnp.dot(a.astype(BF16), b.astype(BF16), preferred_element_type=F32)


def _dot_nt(a, b):
    return lax.dot_general(a.astype(BF16), b.astype(BF16), (((1,), (1,)), ((), ())), preferred_element_type=F32)


def _mod_kernel(cond_ref, w_ref, b_ref, o_ref):
    c = cond_ref[...]
    s = c * jax.nn.sigmoid(c)
    o_ref[0] = _dot(s, w_ref[0]) + b_ref[0]


def _modulation(cond, w_mod, b_mod):
    nb = 1024
    out = pl.pallas_call(
        _mod_kernel,
        grid=(DEPTH, N_MOD * D_MODEL // nb),
        in_specs=[pl.BlockSpec((MOD_ROWS, D_MODEL), lambda l, n: (0, 0)),
                  pl.BlockSpec((1, D_MODEL, nb), lambda l, n: (l, 0, n)),
                  pl.BlockSpec((1, 1, nb), lambda l, n: (l, 0, n))],
        out_specs=pl.BlockSpec((1, MOD_ROWS, nb), lambda l, n: (l, 0, n)),
        out_shape=jax.ShapeDtypeStruct((DEPTH, MOD_ROWS, N_MOD * D_MODEL), F32),
        compiler_params=_cparams("arbitrary", "arbitrary"),
        name="modulation",
    )(cond, w_mod, b_mod.reshape(DEPTH, 1, N_MOD * D_MODEL))
    return out.reshape(DEPTH * MOD_ROWS * N_MOD, 1, D_MODEL)


def _ab_in_kernel(x_ref, g_ref, sc_ref, sh_ref, w_ref, cos_ref, sa_ref, sb_ref,
                  q_ref, k_ref, v_ref, u_ref, wq_s, wk_s, wv_s, wu_s):
    @pl.when(pl.program_id(0) == 0)
    def _():
        wq_s[...] = w_ref[:, 0:A_WIDTH].astype(BF16)
        wk_s[...] = w_ref[:, A_WIDTH:A_WIDTH + A_KV_WIDTH].astype(BF16)
        wv_s[...] = w_ref[:, A_WIDTH + A_KV_WIDTH:A_WIDTH + 2 * A_KV_WIDTH].astype(BF16)
        wu_s[...] = w_ref[:, A_WIDTH + 2 * A_KV_WIDTH:].astype(BF16)

    h = _norm_mod(x_ref[...], g_ref[...], sc_ref[0], sh_ref[0]).astype(BF16)
    cos, sa, sb = cos_ref[...], sa_ref[...], sb_ref[...]
    q = jnp.dot(h, wq_s[...], preferred_element_type=F32)
    q_ref[...] = _apply_rope(q, cos, sa, sb, HEAD_DIM // 4)
    k = jnp.dot(h, wk_s[...], preferred_element_type=F32)
    k_ref[...] = _apply_rope(k, cos, sa, sb, HEAD_DIM // 4)
    v_ref[...] = jnp.dot(h, wv_s[...], preferred_element_type=F32)
    u_ref[...] = jnp.dot(h, wu_s[...], preferred_element_type=F32)


def _ab_in(x, mod, g, w, layer):
    cos, sa, sb = _rope_tables(HEAD_DIM)
    d_in = w.shape[1]
    row = lambda n: pl.BlockSpec((ROW_BLK, n), lambda i: (i, 0))
    rope = pl.BlockSpec((ROW_BLK, 128), lambda i: (_rope_blk(i), 0))
    return pl.pallas_call(
        _ab_in_kernel,
        grid=(N_ROW_BLK,),
        in_specs=[row(D_MODEL), pl.BlockSpec((1, D_MODEL), lambda i: (0, 0)),
                  _mod_spec(layer, 1), _mod_spec(layer, 0),
                  pl.BlockSpec((D_MODEL, d_in), lambda i: (0, 0)), rope, rope, rope],
        out_specs=[row(A_WIDTH), row(A_KV_WIDTH), row(A_KV_WIDTH), row(S5_WIDTH)],
        out_shape=[jax.ShapeDtypeStruct((N_TOK, A_WIDTH), F32), jax.ShapeDtypeStruct((N_TOK, A_KV_WIDTH), F32),
                   jax.ShapeDtypeStruct((N_TOK, A_KV_WIDTH), F32), jax.ShapeDtypeStruct((N_TOK, S5_WIDTH), F32)],
        scratch_shapes=[pltpu.VMEM((D_MODEL, A_WIDTH), BF16), pltpu.VMEM((D_MODEL, A_KV_WIDTH), BF16),
                        pltpu.VMEM((D_MODEL, A_KV_WIDTH), BF16), pltpu.VMEM((D_MODEL, S5_WIDTH), BF16)],
        compiler_params=_cparams("arbitrary"),
        name="ab_in",
    )(x, g.reshape(1, D_MODEL), mod, mod, w, cos, sa, sb)


def _softmax_pv(s_list, v_list, sink):
    m = functools.reduce(jnp.maximum, [jnp.max(s, axis=-1, keepdims=True) for s in s_list])
    if sink is not None:
        m = jnp.maximum(m, sink)
    ps = [jnp.exp(s - m) for s in s_list]
    l = functools.reduce(jnp.add, [jnp.sum(p, axis=-1, keepdims=True) for p in ps])
    if sink is not None:
        l = l + jnp.exp(sink - m)
    o = functools.reduce(jnp.add, [_dot(p, v) for p, v in zip(ps, v_list)])
    return o / l


def _attn_ctx_kernel(sink_ref, q_ref, k_ref, v_ref, o_ref):
    scale = HEAD_DIM ** -0.5
    g = A_HEADS // A_KV_HEADS
    outs = []
    for h in range(A_HEADS):
        kh = h // g
        q = q_ref[:, HEAD_DIM * h:HEAD_DIM * (h + 1)]
        k = k_ref[:, HEAD_DIM * kh:HEAD_DIM * (kh + 1)]
        v = v_ref[:, HEAD_DIM * kh:HEAD_DIM * (kh + 1)]
        s = _dot_nt(q, k) * scale
        outs.append(_softmax_pv([s], [v], sink_ref[h]))
    o_ref[...] = jnp.concatenate(outs, axis=1)


def _attn_lat_kernel(sink_ref, q_ref, kp_ref, kc_ref, kn_ref, vp_ref, vc_ref, vn_ref, kx_ref, vx_ref, oin_ref, o_ref):
    del oin_ref
    n = pl.program_id(1)
    nb = DEC_SEQ // A_WINDOW
    scale = HEAD_DIM ** -0.5
    g = A_HEADS // A_KV_HEADS
    i = lax.broadcasted_iota(jnp.int32, (A_WINDOW, A_WINDOW), 0)
    j = lax.broadcasted_iota(jnp.int32, (A_WINDOW, A_WINDOW), 1)
    ok_prev = (j >= i) & (n > 0)
    ok_next = (j <= i) & (n < nb - 1)
    outs = []
    for h in range(A_HEADS):
        kh = h // g
        sl = slice(HEAD_DIM * kh, HEAD_DIM * (kh + 1))
        q = q_ref[:, HEAD_DIM * h:HEAD_DIM * (h + 1)]
        s_p = jnp.where(ok_prev, _dot_nt(q, kp_ref[:, sl]) * scale, NEG_INF)
        s_c = _dot_nt(q, kc_ref[:, sl]) * scale
        s_n = jnp.where(ok_next, _dot_nt(q, kn_ref[:, sl]) * scale, NEG_INF)
        s_x = _dot_nt(q, kx_ref[0, :, sl]) * scale
        outs.append(_softmax_pv([s_p, s_c, s_n, s_x],
                                [vp_ref[:, sl], vc_ref[:, sl], vn_ref[:, sl], vx_ref[0, :, sl]], sink_ref[h]))
    o_ref[...] = jnp.concatenate(outs, axis=1)


def _attention_a(q, k, v, cache_k, cache_v, sink):
    smem = pl.BlockSpec(memory_space=pltpu.SMEM)
    o = pl.pallas_call(
        _attn_ctx_kernel,
        grid=(BATCH,),
        in_specs=[smem, pl.BlockSpec((SEQ, A_WIDTH), lambda b: (b, 0)),
                  pl.BlockSpec((SEQ, A_KV_WIDTH), lambda b: (b, 0)), pl.BlockSpec((SEQ, A_KV_WIDTH), lambda b: (b, 0))],
        out_specs=pl.BlockSpec((SEQ, A_WIDTH), lambda b: (b, 0)),
        out_shape=jax.ShapeDtypeStruct((N_TOK, A_WIDTH), F32),
        compiler_params=_cparams("arbitrary"),
        name="attn_a_ctx",
    )(sink, q, k, v)
    nb = DEC_SEQ // A_WINDOW
    base = N_CTX_TOK // A_WINDOW
    cur = lambda b, n: (base + b * nb + n, 0)
    prev = lambda b, n: (base + b * nb + jnp.maximum(n - 1, 0), 0)
    nxt = lambda b, n: (base + b * nb + jnp.minimum(n + 1, nb - 1), 0)
    kv = lambda f: pl.BlockSpec((A_WINDOW, A_KV_WIDTH), f)
    cache = pl.BlockSpec((1, PAST_LEN, A_KV_WIDTH), lambda b, n: (b, 0, 0))
    return pl.pallas_call(
        _attn_lat_kernel,
        grid=(DEC_BATCH, nb),
        in_specs=[smem, pl.BlockSpec((A_WINDOW, A_WIDTH), cur), kv(prev), kv(cur), kv(nxt), kv(prev), kv(cur), kv(nxt),
                  cache, cache, pl.BlockSpec(memory_space=pl.ANY)],
        out_specs=pl.BlockSpec((A_WINDOW, A_WIDTH), cur),
        out_shape=jax.ShapeDtypeStruct((N_TOK, A_WIDTH), F32),
        input_output_aliases={10: 0},
        compiler_params=_cparams("arbitrary", "arbitrary"),
        name="attn_a_lat",
    )(sink, q, k, k, k, v, v, v, cache_k, cache_v, o)


S5_CHUNK = 16
S5_CP = S5_CHUNK * S5_GROUP
S5_ROWS_CTX = BATCH * SEQ // S5_CHUNK
S5_ROWS_LAT = DEC_BATCH * DEC_SEQ // S5_CHUNK
HI = lax.Precision.HIGHEST


def _s5_disc_kernel(lr_ref, li_ref, ldt_ref, ar_ref, ai_ref, zr_ref, zi_ref):
    lr, li = lr_ref[...], li_ref[...]
    dt = jnp.exp(ldt_ref[...])
    mag = jnp.exp(lr * dt)
    ar, ai = mag * jnp.cos(li * dt), mag * jnp.sin(li * dt)
    den = lr * lr + li * li
    ar_ref[...] = ar
    ai_ref[...] = ai
    zr_ref[...] = ((ar - 1.0) * lr + ai * li) / den
    zi_ref[...] = (ai * lr - (ar - 1.0) * li) / den


def _cmul(xr, xi, yr, yi):
    return xr * yr - xi * yi, xr * yi + xi * yr


def _cpow_table(ar, ai, e):
    pr = jnp.ones(jnp.broadcast_shapes(ar.shape, e.shape), F32)
    pi = jnp.zeros_like(pr)
    for b in range(5):
        bit = ((e >> b) & 1) == 1
        qr, qi = _cmul(pr, pi, ar, ai)
        pr, pi = jnp.where(bit, qr, pr), jnp.where(bit, qi, pi)
        if b < 4:
            ar, ai = _cmul(ar, ai, ar, ai)
    return pr, pi


def _s5_prep_kernel(arc_ref, aic_ref, arr_ref, air_ref, zrr_ref, zir_ref, btr_ref, bti_ref, ctr_ref, cti_ref,
                    m_ref, win_ref, wout_ref, a16_ref):
    n, cp, grp = S5_STATE, S5_CP, S5_GROUP
    lane_j = lax.broadcasted_iota(jnp.int32, (n, cp), 1) // grp
    lane = lax.broadcasted_iota(jnp.int32, (grp, cp), 1)
    sub_e = lax.broadcasted_iota(jnp.int32, (S5_CHUNK, n), 0)
    m_acc = [None] * S5_CHUNK
    for d in range(2):
        arc, aic = arc_ref[0, d], aic_ref[0, d]
        arr, air = arr_ref[0, d], air_ref[0, d]
        ctr, cti = ctr_ref[0, d], cti_ref[0, d]
        bbr = zrr_ref[0, d] * btr_ref[0, d] - zir_ref[0, d] * bti_ref[0, d]
        bbi = zrr_ref[0, d] * bti_ref[0, d] + zir_ref[0, d] * btr_ref[0, d]

        def ca(e):
            pr, pi = _cpow_table(arc, aic, e)
            return pr * ctr - pi * cti, -(pi * ctr + pr * cti)

        car, cai = ca(lane_j if d == 0 else (S5_CHUNK - 1) - lane_j)
        kflat = (jnp.dot(bbr, car, precision=HI, preferred_element_type=F32)
                 + jnp.dot(bbi, cai, precision=HI, preferred_element_type=F32))
        for s in range(S5_CHUNK):
            if d == 0:
                sh = grp * s
                blk = kflat if sh == 0 else jnp.where(lane >= sh, pltpu.roll(kflat, sh, axis=1), 0.0)
                m_acc[s] = blk
            else:
                sh = grp * (S5_CHUNK - 1 - s)
                blk = kflat if sh == 0 else jnp.where(lane < cp - sh, pltpu.roll(kflat, cp - sh, axis=1), 0.0)
                m_ref[0, grp * s:grp * (s + 1), :] = (m_acc[s] + blk).astype(BF16)
        wr, wi = ca(lane_j + 1 if d == 0 else S5_CHUNK - lane_j)
        wout_ref[0, n * d:n * (d + 1), :] = wr.astype(BF16)
        wout_ref[0, 2 * n + n * d:2 * n + n * (d + 1), :] = wi.astype(BF16)
        pr, pi = _cpow_table(arr, air, (S5_CHUNK - 1) - sub_e if d == 0 else sub_e)
        for s in range(S5_CHUNK):
            xr, xi = _cmul(pr[s:s + 1], pi[s:s + 1], bbr, bbi)
            win_ref[0, grp * s:grp * (s + 1), n * d:n * (d + 1)] = xr.astype(BF16)
            win_ref[0, grp * s:grp * (s + 1), 2 * n + n * d:2 * n + n * (d + 1)] = xi.astype(BF16)
        p16r, p16i = _cpow_table(arr, air, jnp.full((1, n), S5_CHUNK, jnp.int32))
        a16_ref[0, :, n * d:n * (d + 1)] = p16r
        a16_ref[0, :, 2 * n + n * d:2 * n + n * (d + 1)] = p16i


def _s5_main_kernel(u_ref, m_ref, win_ref, wout_ref, a16_ref, h0_ref, y_ref, hfin_ref, x_s, hs_s):
    n2 = 2 * S5_STATE
    ub = u_ref[0].astype(BF16)
    x_s[...] = jnp.dot(ub, win_ref[0], preferred_element_type=F32)
    a_re, a_im = a16_ref[0, :, 0:n2], a16_ref[0, :, n2:2 * n2]
    fwd_lane = lax.broadcasted_iota(jnp.int32, (1, n2), 1) < S5_STATE

    def run(row0, nb, nc, h_re, h_im):
        for c in range(nc):
            rf = row0 + c * nb
            rb = row0 + (nc - 1 - c) * nb
            hs_s[rf:rf + nb, 0:S5_STATE] = h_re[:, 0:S5_STATE]
            hs_s[rf:rf + nb, n2:n2 + S5_STATE] = h_im[:, 0:S5_STATE]
            hs_s[rb:rb + nb, S5_STATE:n2] = h_re[:, S5_STATE:n2]
            hs_s[rb:rb + nb, n2 + S5_STATE:2 * n2] = h_im[:, S5_STATE:n2]
            x_re = jnp.where(fwd_lane, x_s[rf:rf + nb, 0:n2], x_s[rb:rb + nb, 0:n2])
            x_im = jnp.where(fwd_lane, x_s[rf:rf + nb, n2:2 * n2], x_s[rb:rb + nb, n2:2 * n2])
            h_re, h_im = a_re * h_re - a_im * h_im + x_re, a_re * h_im + a_im * h_re + x_im
        return h_re, h_im

    zero = jnp.zeros((BATCH, n2), F32)
    f_re, f_im = run(0, BATCH, SEQ // S5_CHUNK, zero, zero)
    hfin_ref[0, :, 0:n2] = f_re
    hfin_ref[0, :, n2:2 * n2] = f_im
    run(S5_ROWS_CTX, DEC_BATCH, DEC_SEQ // S5_CHUNK, h0_ref[0, :, 0:n2], h0_ref[0, :, n2:2 * n2])
    y_ref[0] = (jnp.dot(ub, m_ref[0], preferred_element_type=F32)
                + jnp.dot(hs_s[...].astype(BF16), wout_ref[0], preferred_element_type=F32))


def _s5_to_chunks(u):
    def one(x, nb, ln):
        x = x.reshape(nb, ln // S5_CHUNK, S5_CHUNK, S5_GROUPS, S5_GROUP)
        return x.transpose(3, 1, 0, 2, 4).reshape(S5_GROUPS, nb * ln // S5_CHUNK, S5_CP)
    return jnp.concatenate([one(u[:N_CTX_TOK], BATCH, SEQ), one(u[N_CTX_TOK:], DEC_BATCH, DEC_SEQ)], axis=1)


def _s5_from_chunks(y):
    def one(x, nb, ln):
        x = x.reshape(S5_GROUPS, ln // S5_CHUNK, nb, S5_CHUNK, S5_GROUP)
        return x.transpose(2, 1, 3, 0, 4).reshape(nb * ln, S5_WIDTH)
    return jnp.concatenate([one(y[:, :S5_ROWS_CTX], BATCH, SEQ), one(y[:, S5_ROWS_CTX:], DEC_BATCH, DEC_SEQ)], axis=0)


def _s5_state_to_lanes(h_re, h_im):
    parts = [h_re[:, 0], h_re[:, 1], h_im[:, 0], h_im[:, 1]]
    return jnp.concatenate(parts, axis=-1).transpose(1, 0, 2)


def _s5_state_from_lanes(h):
    n = S5_STATE
    h = h.transpose(1, 0, 2)
    re = jnp.stack([h[..., 0:n], h[..., n:2 * n]], axis=1)
    im = jnp.stack([h[..., 2 * n:3 * n], h[..., 3 * n:4 * n]], axis=1)
    return re, im


def _s5_scan(u, h0_re, h0_im, lam_re, lam_im, log_dt, b_re, b_im, c_re, c_im):
    ng, n = S5_GROUPS, S5_STATE
    rows = 2 * ng
    disc = pl.pallas_call(
        _s5_disc_kernel,
        out_shape=[jax.ShapeDtypeStruct((rows, n), F32)] * 4,
        name="s5_disc",
    )(lam_re.reshape(rows, n), lam_im.reshape(rows, n), log_dt.reshape(rows, 1))
    ar, ai, zr, zi = [t.reshape(2, ng, n).transpose(1, 0, 2) for t in disc]
    col = lambda t: t.reshape(ng, 2, n, 1)
    row = lambda t: t.reshape(ng, 2, 1, n)
    bt = lambda t: t.transpose(1, 0, 3, 2)
    ct = lambda t: jnp.tile(t.transpose(1, 0, 3, 2), (1, 1, 1, S5_CHUNK))
    spec = lambda *s: pl.BlockSpec((1,) + s, lambda g: (g,) + (0,) * len(s))
    mat = jax.ShapeDtypeStruct((ng, S5_CP, S5_CP), BF16)
    m, win, wout, a16 = pl.pallas_call(
        _s5_prep_kernel,
        grid=(ng,),
        in_specs=[spec(2, n, 1), spec(2, n, 1), spec(2, 1, n), spec(2, 1, n), spec(2, 1, n), spec(2, 1, n),
                  spec(2, S5_GROUP, n), spec(2, S5_GROUP, n), spec(2, n, S5_CP), spec(2, n, S5_CP)],
        out_specs=[spec(S5_CP, S5_CP), spec(S5_CP, S5_CP), spec(S5_CP, S5_CP), spec(1, 4 * n)],
        out_shape=[mat, mat, mat, jax.ShapeDtypeStruct((ng, 1, 4 * n), F32)],
        compiler_params=_cparams("arbitrary"),
        name="s5_prep",
    )(col(ar), col(ai), row(ar), row(ai), row(zr), row(zi), bt(b_re), bt(b_im), ct(c_re), ct(c_im))
    nrow = S5_ROWS_CTX + S5_ROWS_LAT
    y, hfin = pl.pallas_call(
        _s5_main_kernel,
        grid=(ng,),
        in_specs=[spec(nrow, S5_CP), spec(S5_CP, S5_CP), spec(S5_CP, S5_CP), spec(S5_CP, S5_CP), spec(1, 4 * n),
                  spec(DEC_BATCH, 4 * n)],
        out_specs=[spec(nrow, S5_CP), spec(BATCH, 4 * n)],
        out_shape=[jax.ShapeDtypeStruct((ng, nrow, S5_CP), F32), jax.ShapeDtypeStruct((ng, BATCH, 4 * n), F32)],
        scratch_shapes=[pltpu.VMEM((nrow, 4 * n), F32), pltpu.VMEM((nrow, 4 * n), F32)],
        compiler_params=_cparams("arbitrary"),
        name="s5_main",
    )(_s5_to_chunks(u), m, win, wout, a16, _s5_state_to_lanes(h0_re, h0_im))
    fin_re, fin_im = _s5_state_from_lanes(hfin)
    return _s5_from_chunks(y), fin_re, fin_im


def _ab_out_kernel(x_ref, oa_ref, y_ref, u_ref, d_ref, wglu_ref, wout_ref, gt_ref, o_ref, wglu_s, wa_s, wb_s):
    @pl.when(pl.program_id(0) == 0)
    def _():
        wglu_s[...] = wglu_ref[...].astype(BF16)
        wa_s[...] = wout_ref[0:A_WIDTH, :].astype(BF16)
        wb_s[...] = wout_ref[A_WIDTH:, :].astype(BF16)

    g = jax.nn.gelu(y_ref[...] + d_ref[...] * u_ref[...])
    ob = g * jax.nn.sigmoid(jnp.dot(g.astype(BF16), wglu_s[...], preferred_element_type=F32))
    out = (jnp.dot(oa_ref[...].astype(BF16), wa_s[...], preferred_element_type=F32)
           + jnp.dot(ob.astype(BF16), wb_s[...], preferred_element_type=F32))
    o_ref[...] = x_ref[...] + gt_ref[0] * out


def _ab_out(x, oa, y, u, d_skip, w_glu, w_out, mod, layer):
    row = lambda n: pl.BlockSpec((ROW_BLK, n), lambda i: (i, 0))
    full = lambda a, b: pl.BlockSpec((a, b), lambda i: (0, 0))
    return pl.pallas_call(
        _ab_out_kernel,
        grid=(N_ROW_BLK,),
        in_specs=[row(D_MODEL), row(A_WIDTH), row(S5_WIDTH), row(S5_WIDTH), full(1, S5_WIDTH),
                  full(S5_WIDTH, S5_WIDTH), full(A_WIDTH + S5_WIDTH, D_MODEL), _mod_spec(layer, 2)],
        out_specs=row(D_MODEL),
        out_shape=jax.ShapeDtypeStruct((N_TOK, D_MODEL), F32),
        scratch_shapes=[pltpu.VMEM((S5_WIDTH, S5_WIDTH), BF16), pltpu.VMEM((A_WIDTH, D_MODEL), BF16),
                        pltpu.VMEM((S5_WIDTH, D_MODEL), BF16)],
        compiler_params=_cparams("arbitrary"),
        name="ab_out",
    )(x, oa, y, u, d_skip.reshape(1, S5_WIDTH), w_glu, w_out, mod)


CD_HG0 = MLA_Q_RANK + MLA_KV_RANK + MLA_ROPE
KPE_LANES = 128


def _cd_in_kernel(x_ref, g_ref, sc_ref, sh_ref, w_ref, gq_ref, wqu_ref, gkv_ref, wkvu_ref, cos_ref, sa_ref, sb_ref,
                  qn_ref, qr_ref, ckv_ref, kpe_ref, kn_ref, vm_ref, hq_ref, hff_ref, hfb_ref, hi_ref, hg_ref,
                  wcq_s, wckv_s, wkpe_s, whg_s, wqu_s, wkvu_s):
    @pl.when(pl.program_id(0) == 0)
    def _():
        wcq_s[...] = w_ref[:, 0:MLA_Q_RANK].astype(BF16)
        wckv_s[...] = w_ref[:, MLA_Q_RANK:MLA_Q_RANK + MLA_KV_RANK].astype(BF16)
        kp = w_ref[:, MLA_Q_RANK + MLA_KV_RANK:CD_HG0].astype(BF16)
        wkpe_s[...] = jnp.concatenate([kp] * (KPE_LANES // MLA_ROPE), axis=1)
        whg_s[...] = w_ref[:, CD_HG0:].astype(BF16)
        wqu_s[...] = wqu_ref[...].astype(BF16)
        wkvu_s[...] = wkvu_ref[...].astype(BF16)

    h = _norm_mod(x_ref[...], g_ref[...], sc_ref[0], sh_ref[0]).astype(BF16)
    cos, sa, sb = cos_ref[...], sa_ref[...], sb_ref[...]
    nope = MLA_HEADS * MLA_NOPE
    cq = _rms(jnp.dot(h, wcq_s[...], preferred_element_type=F32)) * gq_ref[...]
    qq = jnp.dot(cq.astype(BF16), wqu_s[...], preferred_element_type=F32)
    qn_ref[...] = qq[:, 0:nope]
    qr_ref[...] = _apply_rope(qq[:, nope:], cos, sa, sb, MLA_ROPE // 4)
    ckv = _rms(jnp.dot(h, wckv_s[...], preferred_element_type=F32)) * gkv_ref[...]
    ckv_ref[...] = ckv
    kv = jnp.dot(ckv.astype(BF16), wkvu_s[...], preferred_element_type=F32)
    kn_ref[...] = kv[:, 0:nope]
    vm_ref[...] = kv[:, nope:]
    kpe_ref[...] = _apply_rope(jnp.dot(h, wkpe_s[...], preferred_element_type=F32), cos, sa, sb, MLA_ROPE // 4)
    hh = jnp.dot(h, whg_s[...], preferred_element_type=F32)
    for n, ref in enumerate((hq_ref, hff_ref, hfb_ref, hi_ref, hg_ref)):
        ref[...] = hh[:, HG_KD * n:HG_KD * (n + 1)]


def _mla_split_heads(w, a):
    k, n = w.shape
    w = w.reshape(k, MLA_HEADS, n // MLA_HEADS)
    return jnp.concatenate([w[:, :, :a].reshape(k, -1), w[:, :, a:].reshape(k, -1)], axis=1)


def _cd_in(x, mod, g, w, g_q, w_q_up, g_kv, w_kv_up, layer):
    cos, sa, sb = _rope_tables(MLA_ROPE)
    d_in = w.shape[1]
    row = lambda n: pl.BlockSpec((ROW_BLK, n), lambda i: (i, 0))
    full = lambda a, b: pl.BlockSpec((a, b), lambda i: (0, 0))
    rope = pl.BlockSpec((ROW_BLK, 128), lambda i: (_rope_blk(i), 0))
    widths = [MLA_HEADS * MLA_NOPE, MLA_HEADS * MLA_ROPE, MLA_KV_RANK, KPE_LANES, MLA_HEADS * MLA_NOPE, MLA_WIDTH] + [HG_KD] * 5
    nq = MLA_HEADS * (MLA_NOPE + MLA_ROPE)
    nkv = MLA_HEADS * (MLA_NOPE + MLA_V)
    return pl.pallas_call(
        _cd_in_kernel,
        grid=(N_ROW_BLK,),
        in_specs=[row(D_MODEL), full(1, D_MODEL), _mod_spec(layer, 1), _mod_spec(layer, 0), full(D_MODEL, d_in),
                  full(1, MLA_Q_RANK), full(MLA_Q_RANK, nq), full(1, MLA_KV_RANK), full(MLA_KV_RANK, nkv), rope, rope, rope],
        out_specs=[row(n) for n in widths],
        out_shape=[jax.ShapeDtypeStruct((N_TOK, n), F32) for n in widths],
        scratch_shapes=[pltpu.VMEM((D_MODEL, MLA_Q_RANK), BF16), pltpu.VMEM((D_MODEL, MLA_KV_RANK), BF16),
                        pltpu.VMEM((D_MODEL, KPE_LANES), BF16), pltpu.VMEM((D_MODEL, 5 * HG_KD), BF16),
                        pltpu.VMEM((MLA_Q_RANK, nq), BF16), pltpu.VMEM((MLA_KV_RANK, nkv), BF16)],
        compiler_params=_cparams("arbitrary"),
        name="cd_in",
    )(x, g.reshape(1, D_MODEL), mod, mod, w, g_q.reshape(1, -1), _mla_split_heads(w_q_up, MLA_NOPE),
      g_kv.reshape(1, -1), _mla_split_heads(w_kv_up, MLA_NOPE), cos, sa, sb)


def _mm_kernel(a_ref, w_ref, *o_refs):
    r = _dot(a_ref[...], w_ref[...])
    off = 0
    for o in o_refs:
        o[...] = r[:, off:off + o.shape[1]]
        off += o.shape[1]


def _mla_cache_kv(cckv, w_kv_up):
    n = MLA_HEADS * MLA_NOPE
    rows = cckv.shape[0]
    return pl.pallas_call(
        _mm_kernel,
        grid=(rows // PAST_LEN,),
        in_specs=[pl.BlockSpec((PAST_LEN, MLA_KV_RANK), lambda i: (i, 0)),
                  pl.BlockSpec((MLA_KV_RANK, 2 * n), lambda i: (0, 0))],
        out_specs=[pl.BlockSpec((PAST_LEN, n), lambda i: (i, 0))] * 2,
        out_shape=[jax.ShapeDtypeStruct((rows, n), F32)] * 2,
        compiler_params=_cparams("arbitrary"),
        name="mla_cache_kv",
    )(cckv, _mla_split_heads(w_kv_up, MLA_NOPE))


def _mla_heads(qn, qr, keys, o_ref):
    scale = (MLA_NOPE + MLA_ROPE) ** -0.5
    outs = []
    for h in range(MLA_HEADS):
        a = slice(MLA_NOPE * h, MLA_NOPE * (h + 1))
        r = slice(MLA_ROPE * h, MLA_ROPE * (h + 1))
        s_list = [(_dot_nt(qn[:, a], kn[:, a]) + _dot_nt(qr[:, r], kp[:, 0:MLA_ROPE])) * scale for kn, kp, _ in keys]
        outs.append(_softmax_pv(s_list, [v[:, a] for _, _, v in keys], None))
    o_ref[...] = jnp.concatenate(outs, axis=1)


def _mla_ctx_kernel(qn_ref, qr_ref, kn_ref, kp_ref, v_ref, o_ref):
    _mla_heads(qn_ref[...], qr_ref[...], [(kn_ref[...], kp_ref[...], v_ref[...])], o_ref)


def _mla_lat_kernel(qn_ref, qr_ref, kn_ref, kp_ref, v_ref, kcn_ref, kcp_ref, vc_ref, oin_ref, o_ref):
    del oin_ref
    _mla_heads(qn_ref[...], qr_ref[...],
               [(kcn_ref[...], kcp_ref[0], vc_ref[...]), (kn_ref[...], kp_ref[...], v_ref[...])], o_ref)


def _attention_mla(qn, qr, kn, kpe, vm, kcn, kcpe, vc):
    n = MLA_HEADS * MLA_NOPE
    nr = MLA_HEADS * MLA_ROPE
    blk = lambda w: pl.BlockSpec((SEQ, w), lambda b: (b, 0))
    o = pl.pallas_call(
        _mla_ctx_kernel,
        grid=(BATCH,),
        in_specs=[blk(n), blk(nr), blk(n), blk(KPE_LANES), blk(MLA_WIDTH)],
        out_specs=blk(MLA_WIDTH),
        out_shape=jax.ShapeDtypeStruct((N_TOK, MLA_WIDTH), F32),
        compiler_params=_cparams("arbitrary"),
        name="mla_ctx",
    )(qn, qr, kn, kpe, vm)
    nq = DEC_SEQ // ROW_BLK
    qblk = lambda w: pl.BlockSpec((ROW_BLK, w), lambda b, i: (N_CTX_BLK + b * nq + i, 0))
    seq = lambda w: pl.BlockSpec((DEC_SEQ, w), lambda b, i: (N_CTX_TOK // DEC_SEQ + b, 0))
    past = lambda w: pl.BlockSpec((PAST_LEN, w), lambda b, i: (b, 0))
    return pl.pallas_call(
        _mla_lat_kernel,
        grid=(DEC_BATCH, nq),
        in_specs=[qblk(n), qblk(nr), seq(n), seq(KPE_LANES), seq(MLA_WIDTH), past(n),
                  pl.BlockSpec((1, PAST_LEN, MLA_ROPE), lambda b, i: (b, 0, 0)), past(MLA_WIDTH),
                  pl.BlockSpec(memory_space=pl.ANY)],
        out_specs=qblk(MLA_WIDTH),
        out_shape=jax.ShapeDtypeStruct((N_TOK, MLA_WIDTH), F32),
        input_output_aliases={8: 0},
        compiler_params=_cparams("arbitrary", "arbitrary"),
        name="mla_lat",
    )(qn, qr, kn, kpe, vm, kcn, kcpe, vc, o)


HG_TILE = 128
HG_PAD = HG_CHUNK
HG_HALF = 256


def _hg_shift(ref, d, sign):
    o = HG_PAD - sign * d
    return ref[o:o + HG_TILE, :]


def _hg_direction(hq, hf, hi, lb, st_ref, kk_s, fz_s, v_s, sc_s, sign, ones_bd, diag_mask):
    t_in = lax.broadcasted_iota(jnp.int32, (HG_TILE, 1), 0) % HG_CHUNK
    first = t_in == (0 if sign > 0 else HG_CHUNK - 1)
    q = hq * jax.nn.sigmoid(hq)
    f = lb + (1.0 - lb) * jax.nn.sigmoid(hf)
    kk = 1.0 - f
    kk_s[HG_PAD:HG_PAD + HG_TILE, :] = kk
    fz_s[HG_PAD:HG_PAD + HG_TILE, :] = jnp.where(first, 0.0, f)
    v_s[HG_PAD:HG_PAD + HG_TILE, :] = hi

    def expand(p):
        pb = p.astype(BF16)
        return jnp.concatenate([jnp.dot(pb[:, 0:HG_HALF], ones_bd, preferred_element_type=F32),
                                jnp.dot(pb[:, HG_HALF:], ones_bd, preferred_element_type=F32)], axis=1)

    o = expand(q * kk) * hi
    decay = None
    for d in range(1, HG_CHUNK):
        fz = _hg_shift(fz_s, d - 1, sign)
        decay = fz if decay is None else decay * fz
        o = o + expand(q * _hg_shift(kk_s, d, sign) * decay) * _hg_shift(v_s, d, sign)

    def scan(x, sgn, start_mask_of):
        for sh in (1, 2, 4, 8):
            sc_s[HG_PAD:HG_PAD + HG_TILE, :] = x
            ok = (t_in >= sh) if sgn > 0 else (t_in < HG_CHUNK - sh)
            x = x * jnp.where(ok, _hg_shift(sc_s, sh, sgn), 1.0)
        return x

    incl = scan(f, sign, None)
    rev = scan(f, -sign, None)
    sc_s[HG_PAD:HG_PAD + HG_TILE, :] = rev
    last = t_in == (HG_CHUNK - 1 if sign > 0 else 0)
    excl = jnp.where(last, 1.0, _hg_shift(sc_s, 1, -sign))
    qt = (q * incl).astype(BF16)
    kt = (kk * excl).astype(BF16)
    vb = hi.astype(BF16)
    nchunk = HG_TILE // HG_CHUNK
    rows = [None] * nchunk
    for cc in range(nchunk):
        c = cc if sign > 0 else nchunk - 1 - cc
        r = slice(HG_CHUNK * c, HG_CHUNK * (c + 1))
        e = HG_CHUNK * c + (HG_CHUNK - 1 if sign > 0 else 0)
        parts = []
        for g in range(2):
            l = slice(HG_HALF * g, HG_HALF * (g + 1))
            st = st_ref[g]
            parts.append(_dot_nt(qt[r, l], st))
            ds = lax.dot_general(vb[r, l], kt[r, l], (((0,), (0,)), ((), ())), preferred_element_type=F32)
            st_ref[g] = st * incl[e:e + 1, l] + jnp.where(diag_mask, ds, 0.0)
        rows[c] = jnp.concatenate(parts, axis=1)
    return o + jnp.concatenate(rows, axis=0)


def _hg_lower_bound(lb_ref, d, layer):
    raw = [lb_ref[d * DEPTH + m:d * DEPTH + m + 1, :] for m in range(DEPTH)]
    mx = functools.reduce(jnp.maximum, raw)
    e = [jnp.exp(r - mx) for r in raw]
    tot = functools.reduce(jnp.add, e)
    return functools.reduce(jnp.add, e[1:layer + 1], jnp.zeros_like(tot)) / tot


def _hgrn_kernel(nt, layer, hqf_ref, hff_ref, hif_ref, hqb_ref, hfb_ref, hib_ref, lb_ref, s0_ref, *rest):
    of_ref, ob_ref, sfin_ref, st_s, kk_s, fz_s, v_s, sc_s = rest[-8:]
    i = pl.program_id(1)

    @pl.when(i == 0)
    def _():
        st_s[...] = s0_ref[0]

    zpad = jnp.zeros((HG_PAD, HG_KD), F32)
    for ref in (kk_s, fz_s, v_s, sc_s):
        ref[0:HG_PAD, :] = zpad
        ref[HG_PAD + HG_TILE:, :] = zpad
    ri = lax.broadcasted_iota(jnp.int32, (HG_HALF, HG_HALF), 0) // HG_KEY
    ci = lax.broadcasted_iota(jnp.int32, (HG_HALF, HG_HALF), 1) // HG_KEY
    diag_mask = ri == ci
    ones_bd = jnp.where(diag_mask, 1.0, 0.0).astype(BF16)
    of_ref[...] = _hg_direction(hqf_ref[...], hff_ref[...], hif_ref[...], _hg_lower_bound(lb_ref, 0, layer), st_s.at[0],
                                kk_s, fz_s, v_s, sc_s, 1, ones_bd, diag_mask)
    ob_ref[...] = _hg_direction(hqb_ref[...], hfb_ref[...], hib_ref[...], _hg_lower_bound(lb_ref, 1, layer), st_s.at[1],
                                kk_s, fz_s, v_s, sc_s, -1, ones_bd, diag_mask)

    @pl.when(i == nt - 1)
    def _():
        sfin_ref[0] = st_s[...]


def _hg_state_to_blocks(s):
    b = s.shape[0]
    st = s.transpose(0, 1, 2, 4, 3).reshape(b, 2, 2, 4, HG_VAL, HG_KEY)
    eye = jnp.eye(4, dtype=s.dtype)
    return jnp.einsum('bdgivk,ij->bdgivjk', st, eye).reshape(b, 2, 2, HG_HALF, HG_HALF)


def _hg_state_from_blocks(st):
    b = st.shape[0]
    st = st.reshape(b, 2, 2, 4, HG_VAL, 4, HG_KEY)
    diag = jnp.stack([st[:, :, :, i, :, i, :] for i in range(4)], axis=3)
    return diag.reshape(b, 2, HG_HEADS, HG_VAL, HG_KEY).transpose(0, 1, 2, 4, 3)


def _hgrn_scan(hq, hff, hfb, hi, lb, s0, layer, row0, nseq, seqlen, prev=None):
    nt = seqlen // HG_TILE
    base = row0 // HG_TILE
    fwd = pl.BlockSpec((HG_TILE, HG_KD), lambda b, i: (base + b * nt + i, 0))
    bwd = pl.BlockSpec((HG_TILE, HG_KD), lambda b, i: (base + b * nt + nt - 1 - i, 0))
    anyspec = pl.BlockSpec(memory_space=pl.ANY)
    extra = [] if prev is None else list(prev)
    st = pl.BlockSpec((1, 2, 2, HG_HALF, HG_HALF), lambda b, i: (b, 0, 0, 0, 0))
    pad = lambda: pltpu.VMEM((HG_TILE + 2 * HG_PAD, HG_KD), F32)
    return pl.pallas_call(
        functools.partial(_hgrn_kernel, nt, layer),
        grid=(nseq, nt),
        in_specs=[fwd, fwd, fwd, bwd, bwd, bwd, pl.BlockSpec((2 * DEPTH, HG_KD), lambda b, i: (0, 0)), st]
        + [anyspec] * len(extra),
        out_specs=[fwd, bwd, st],
        input_output_aliases={8 + n: n for n in range(len(extra))},
        out_shape=[jax.ShapeDtypeStruct((N_TOK, HG_WIDTH), F32)] * 2
        + [jax.ShapeDtypeStruct((nseq, 2, 2, HG_HALF, HG_HALF), F32)],
        scratch_shapes=[pltpu.VMEM((2, 2, HG_HALF, HG_HALF), F32), pad(), pad(), pad(), pad()],
        compiler_params=_cparams("arbitrary", "arbitrary"),
        name="hgrn_scan",
    )(hq, hff, hi, hq, hfb, hi, lb.reshape(2 * DEPTH, HG_KD), s0, *extra)


def _cd_out_kernel(x_ref, oc_ref, of_ref, ob_ref, hg_ref, go_ref, wout_ref, gt_ref, o_ref, wa_s, wb_s):
    @pl.when(pl.program_id(0) == 0)
    def _():
        wa_s[...] = wout_ref[0:MLA_WIDTH, :].astype(BF16)
        wb_s[...] = wout_ref[MLA_WIDTH:, :].astype(BF16)

    ri = lax.broadcasted_iota(jnp.int32, (HG_HALF, HG_HALF), 0) // HG_VAL
    ci = lax.broadcasted_iota(jnp.int32, (HG_HALF, HG_HALF), 1) // HG_VAL
    ones_bd = jnp.where(ri == ci, 1.0, 0.0).astype(BF16)
    o = of_ref[...] + ob_ref[...]
    sq = o * o
    hi = sq.astype(BF16)
    lo = (sq - hi.astype(F32)).astype(BF16)
    ms = jnp.concatenate(
        [jnp.dot(hi[:, l], ones_bd, preferred_element_type=F32) + jnp.dot(lo[:, l], ones_bd, preferred_element_type=F32)
         for l in (slice(0, HG_HALF), slice(HG_HALF, 2 * HG_HALF))], axis=1) * (1.0 / HG_VAL)
    hg = hg_ref[...]
    od = o * lax.rsqrt(ms + EPS) * go_ref[...] * (hg * jax.nn.sigmoid(hg))
    out = (jnp.dot(oc_ref[...].astype(BF16), wa_s[...], preferred_element_type=F32)
           + jnp.dot(od.astype(BF16), wb_s[...], preferred_element_type=F32))
    o_ref[...] = x_ref[...] + gt_ref[0] * out


def _cd_out(x, oc, of, ob, hg, g_o, w_out, mod, layer):
    row = lambda n: pl.BlockSpec((ROW_BLK, n), lambda i: (i, 0))
    full = lambda a, b: pl.BlockSpec((a, b), lambda i: (0, 0))
    return pl.pallas_call(
        _cd_out_kernel,
        grid=(N_ROW_BLK,),
        in_specs=[row(D_MODEL), row(MLA_WIDTH), row(HG_WIDTH), row(HG_WIDTH), row(HG_WIDTH), full(1, HG_WIDTH),
                  full(MLA_WIDTH + HG_WIDTH, D_MODEL), _mod_spec(layer, 2)],
        out_specs=row(D_MODEL),
        out_shape=jax.ShapeDtypeStruct((N_TOK, D_MODEL), F32),
        scratch_shapes=[pltpu.VMEM((MLA_WIDTH, D_MODEL), BF16), pltpu.VMEM((HG_WIDTH, D_MODEL), BF16)],
        compiler_params=_cparams("arbitrary"),
        name="cd_out",
    )(x, oc, of, ob, hg, jnp.tile(g_o, HG_HEADS).reshape(1, HG_WIDTH), w_out, mod)


N_PAIRS = 6
N_CLASSES = N_GROUPS * N_PAIRS
CLS_ROWS = 32
MOE_BM = 256
MOE_NBLK = N_TOK // MOE_BM + N_CLASSES
MOE_ROWS = MOE_NBLK * MOE_BM
PACK_W = D_MODEL // 2
PACK_ROW = PACK_W + 128


def _moe_route_kernel(x_ref, g_ref, sc_ref, sh_ref, rw_ref, rb_ref, hp_ref, ti_ref, cnt_ref, base_s):
    @pl.when(pl.program_id(0) == 0)
    def _():
        base_s[...] = jnp.zeros_like(base_s)

    h = _norm_mod(x_ref[...], g_ref[...], sc_ref[0], sh_ref[0])
    logits = lax.dot_general(rw_ref[...], h, (((1,), (1,)), ((), ())), precision=HI,
                             preferred_element_type=F32)
    aff = jax.nn.sigmoid(logits)
    sel = aff + rb_ref[...]
    s = [sel[e:e + 1, :] for e in range(N_EXPERTS)]
    a = [aff[e:e + 1, :] for e in range(N_EXPERTS)]
    gs = []
    for g in range(N_GROUPS):
        m = s[4 * g:4 * g + 4]
        pairs = [m[i] + m[j] for i in range(4) for j in range(i + 1, 4)]
        gs.append(functools.reduce(jnp.maximum, pairs))
    gmax = functools.reduce(jnp.maximum, gs)
    taken = jnp.zeros_like(gmax) > 1.0
    gsel = []
    for g in range(N_GROUPS):
        hit = (gs[g] == gmax) & jnp.logical_not(taken)
        gsel.append(hit)
        taken = taken | hit
    e_lo = jnp.zeros(gmax.shape, jnp.int32)
    e_hi = jnp.zeros(gmax.shape, jnp.int32)
    a_lo = jnp.zeros_like(gmax)
    a_hi = jnp.zeros_like(gmax)
    nsel = jnp.zeros(gmax.shape, jnp.int32)
    for g in range(N_GROUPS):
        for i in range(4):
            e = 4 * g + i
            beat = jnp.zeros(gmax.shape, jnp.int32)
            for j in range(4):
                if j != i:
                    o = 4 * g + j
                    beat = beat + jnp.where((s[o] > s[e]) | ((s[o] == s[e]) & (j < i)), 1, 0)
            pick = gsel[g] & (beat < 2)
            is_first = pick & (nsel == 0)
            is_second = pick & (nsel == 1)
            e_lo = jnp.where(is_first, e, e_lo)
            a_lo = jnp.where(is_first, a[e], a_lo)
            e_hi = jnp.where(is_second, e, e_hi)
            a_hi = jnp.where(is_second, a[e], a_hi)
            nsel = nsel + jnp.where(pick, 1, 0)
    grp = e_lo // EXPERTS_PER_GROUP
    lo = e_lo - grp * EXPERTS_PER_GROUP
    hi = e_hi - grp * EXPERTS_PER_GROUP
    cls = grp * N_PAIRS + ((lo * (7 - lo)) >> 1) + (hi - lo - 1)
    wsum = a_lo + a_hi
    w_lo, w_hi = a_lo / wsum, a_hi / wsum
    onehot = (lax.broadcasted_iota(jnp.int32, (CLS_ROWS, ROW_BLK), 0) == cls).astype(F32)
    tt = lax.broadcasted_iota(jnp.int32, (ROW_BLK, ROW_BLK), 0) < lax.broadcasted_iota(jnp.int32, (ROW_BLK, ROW_BLK), 1)
    before = _dot(onehot, jnp.where(tt, 1.0, 0.0))
    base = base_s[...]
    rank = jnp.sum(onehot * (before + base[:, 0:1]), axis=0, keepdims=True).astype(jnp.int32)
    base = base + jnp.sum(onehot, axis=1, keepdims=True)
    base_s[...] = base
    cnt_ref[...] = base.astype(jnp.int32)
    ti_ref[0] = jnp.concatenate([cls, rank, e_lo, e_hi, jnp.zeros((4, ROW_BLK), jnp.int32)], axis=0)
    bits = pltpu.bitcast(h.astype(BF16).astype(F32), jnp.uint32)
    hp_ref[:, 0:PACK_W] = (bits[:, PACK_W:] & jnp.uint32(0xFFFF0000)) | (bits[:, 0:PACK_W] >> 16)
    lane = lax.broadcasted_iota(jnp.int32, (ROW_BLK, 128), 1)
    ident = lax.broadcasted_iota(jnp.int32, (ROW_BLK, ROW_BLK), 0) == lax.broadcasted_iota(jnp.int32, (ROW_BLK, ROW_BLK), 1)
    col = lambda r: jnp.sum(jnp.where(ident, r, 0.0), axis=1, keepdims=True)
    wl = jnp.where(lane == 0, col(w_lo), jnp.where(lane == 1, col(w_hi), 0.0))
    hp_ref[:, PACK_W:] = pltpu.bitcast(wl, jnp.uint32)


def _moe_route(x, mod, g, router_w, router_b, layer):
    row = lambda n: pl.BlockSpec((ROW_BLK, n), lambda i: (i, 0))
    full = lambda a, b: pl.BlockSpec((a, b), lambda i: (0, 0))
    return pl.pallas_call(
        _moe_route_kernel,
        grid=(N_ROW_BLK,),
        in_specs=[row(D_MODEL), full(1, D_MODEL), _mod_spec(layer, 4), _mod_spec(layer, 3),
                  full(N_EXPERTS, D_MODEL), full(N_EXPERTS, 1)],
        out_specs=[row(PACK_ROW), pl.BlockSpec((1, 8, ROW_BLK), lambda i: (i, 0, 0)), full(CLS_ROWS, 128)],
        out_shape=[jax.ShapeDtypeStruct((N_TOK, PACK_ROW), jnp.uint32),
                   jax.ShapeDtypeStruct((N_ROW_BLK, 8, ROW_BLK), jnp.int32),
                   jax.ShapeDtypeStruct((CLS_ROWS, 128), jnp.int32)],
        scratch_shapes=[pltpu.VMEM((CLS_ROWS, 128), F32)],
        compiler_params=_cparams("arbitrary"),
        name="moe_route",
    )(x, g.reshape(1, D_MODEL), mod, mod, router_w.T, router_b.reshape(N_EXPERTS, 1))


def _moe_experts_kernel(layer, elo_ref, ehi_ref, nblk_ref, pos_ref, hp_ref, wgl_ref, wul_ref, wdl_ref,
                        wgh_ref, wuh_ref, wdh_ref, y_ref, perm_s, stage_s):
    del layer, elo_ref, ehi_ref
    j = pl.program_id(0)

    @pl.when(j == 0)
    def _():
        def init(r, c):
            perm_s[r] = 0
            return c

        lax.fori_loop(0, MOE_ROWS, init, 0)

        def build(t, c):
            perm_s[pos_ref[t]] = t
            return c

        lax.fori_loop(0, N_TOK, build, 0)

    @pl.when(j < nblk_ref[0])
    def _():
        def gather(r, c):
            src = perm_s[j * MOE_BM + r]
            stage_s[pl.ds(r, 1), :] = hp_ref[pl.ds(src, 1), :]
            return c

        lax.fori_loop(0, MOE_BM, gather, 0, unroll=8)
        words = stage_s[:, 0:PACK_W]
        h_a = pltpu.bitcast(words << 16, F32).astype(BF16)
        h_b = pltpu.bitcast(words & jnp.uint32(0xFFFF0000), F32).astype(BF16)
        w = pltpu.bitcast(stage_s[:, PACK_W:], F32)
        acc = None
        for wg, wu, wd, gate in ((wgl_ref, wul_ref, wdl_ref, w[:, 0:1]), (wgh_ref, wuh_ref, wdh_ref, w[:, 1:2])):
            g = _dot(h_a, wg[0, 0, 0:PACK_W, :]) + _dot(h_b, wg[0, 0, PACK_W:, :])
            u = _dot(h_a, wu[0, 0, 0:PACK_W, :]) + _dot(h_b, wu[0, 0, PACK_W:, :])
            hid = g * jax.nn.sigmoid(g) * u * gate
            y = _dot(hid, wd[0, 0])
            acc = y if acc is None else acc + y
        y_ref[...] = acc

    @pl.when(j >= nblk_ref[0])
    def _():
        y_ref[...] = jnp.zeros_like(y_ref)


def _moe_experts(hp, blk_elo, blk_ehi, nblk, pos, w_gate, w_up, w_down, layer):
    last = lambda j, nb: jnp.minimum(j, nb[0] - 1)
    wspec = lambda a, b, which: pl.BlockSpec(
        (1, 1, a, b), lambda j, elo, ehi, nb, pos: (layer, (elo, ehi)[which][last(j, nb)], 0, 0))
    grid_spec = pltpu.PrefetchScalarGridSpec(
        num_scalar_prefetch=4,
        grid=(MOE_NBLK,),
        in_specs=[pl.BlockSpec((N_TOK, PACK_ROW), lambda j, *_: (0, 0), pipeline_mode=pl.Buffered(1)),
                  wspec(D_MODEL, D_FF, 0), wspec(D_MODEL, D_FF, 0), wspec(D_FF, D_MODEL, 0),
                  wspec(D_MODEL, D_FF, 1), wspec(D_MODEL, D_FF, 1), wspec(D_FF, D_MODEL, 1)],
        out_specs=pl.BlockSpec((MOE_BM, D_MODEL), lambda j, *_: (j, 0)),
        scratch_shapes=[pltpu.SMEM((MOE_ROWS,), jnp.int32), pltpu.VMEM((MOE_BM, PACK_ROW), jnp.uint32)],
    )
    return pl.pallas_call(
        functools.partial(_moe_experts_kernel, layer),
        grid_spec=grid_spec,
        out_shape=jax.ShapeDtypeStruct((MOE_ROWS, D_MODEL), F32),
        compiler_params=_cparams("arbitrary"),
        name="moe_experts",
    )(blk_elo, blk_ehi, nblk, pos, hp, w_gate, w_up, w_down, w_gate, w_up, w_down)


def _moe_combine_kernel(pos_ref, x_ref, y_ref, gt_ref, o_ref, stage_s):
    i = pl.program_id(1)

    def gather(r, c):
        stage_s[pl.ds(r, 1), :] = y_ref[pl.ds(pos_ref[i * ROW_BLK + r], 1), :]
        return c

    lax.fori_loop(0, ROW_BLK, gather, 0, unroll=8)
    o_ref[...] = x_ref[...] + gt_ref[0] * stage_s[...]


def _moe_combine(x, y_sorted, pos, mod, layer):
    half = D_MODEL // 2
    grid_spec = pltpu.PrefetchScalarGridSpec(
        num_scalar_prefetch=1,
        grid=(2, N_ROW_BLK),
        in_specs=[pl.BlockSpec((ROW_BLK, half), lambda c, i, pos: (i, c)),
                  pl.BlockSpec((MOE_ROWS, half), lambda c, i, pos: (0, c), pipeline_mode=pl.Buffered(1)),
                  pl.BlockSpec((1, 1, half), lambda c, i, pos: ((layer * MOD_ROWS + _mod_group(i)) * N_MOD + 5, 0, c))],
        out_specs=pl.BlockSpec((ROW_BLK, half), lambda c, i, pos: (i, c)),
        scratch_shapes=[pltpu.VMEM((ROW_BLK, half), F32)],
    )
    return pl.pallas_call(
        _moe_combine_kernel,
        grid_spec=grid_spec,
        out_shape=jax.ShapeDtypeStruct((N_TOK, D_MODEL), F32),
        compiler_params=_cparams("arbitrary", "arbitrary"),
        name="moe_combine",
    )(pos, x, y_sorted, mod)


def _moe(x, mod, g, router_w, router_b, w_gate, w_up, w_down, layer):
    hp, info, counts = _moe_route(x, mod, g, router_w, router_b, layer)
    cls = info[:, 0, :].reshape(N_TOK)
    rank = info[:, 1, :].reshape(N_TOK)
    cnt = counts[:N_CLASSES, 0]
    nb = (cnt + MOE_BM - 1) // MOE_BM
    ends = jnp.cumsum(nb)
    starts = ends - nb
    pos = (starts * MOE_BM)[cls] + rank
    blk_cls = jnp.minimum(jnp.searchsorted(ends, jnp.arange(MOE_NBLK, dtype=jnp.int32), side='right'), N_CLASSES - 1)
    pair_lo = jnp.asarray([0, 0, 0, 1, 1, 2], jnp.int32)
    pair_hi = jnp.asarray([1, 2, 3, 2, 3, 3], jnp.int32)
    grp = blk_cls // N_PAIRS
    blk_elo = (grp * EXPERTS_PER_GROUP + pair_lo[blk_cls % N_PAIRS]).astype(jnp.int32)
    blk_ehi = (grp * EXPERTS_PER_GROUP + pair_hi[blk_cls % N_PAIRS]).astype(jnp.int32)
    nblk = ends[-1:].astype(jnp.int32)
    y_sorted = _moe_experts(hp, blk_elo, blk_ehi, nblk, pos.astype(jnp.int32), w_gate, w_up, w_down, layer)
    return _moe_combine(x, y_sorted, pos.astype(jnp.int32), mod, layer)


def _final_norm_kernel(x_ref, g_ref, o_ref):
    o_ref[...] = _rms(x_ref[...]) * g_ref[...]


def _final_norm(x, g):
    row = pl.BlockSpec((ROW_BLK, D_MODEL), lambda i: (i, 0))
    return pl.pallas_call(
        _final_norm_kernel,
        grid=(N_ROW_BLK,),
        in_specs=[row, pl.BlockSpec((1, D_MODEL), lambda i: (0, 0))],
        out_specs=row,
        out_shape=jax.ShapeDtypeStruct((N_TOK, D_MODEL), F32),
        compiler_params=_cparams("arbitrary"),
        name="final_norm",
    )(x, g.reshape(1, D_MODEL))


def kernel(x_prompt, x_sample, cache_attn_k, cache_attn_v, state_ssm_re, state_ssm_im, cache_mla_ckv, cache_mla_kpe,
           state_hgrn, c, c_ctx, w_mod, b_mod, g_mix, g_ffn, g_final, router_w, router_b, moe_w_gate, moe_w_up,
           moe_w_down, ab_w_in, ab_sink, s5_lam_re, s5_lam_im, s5_log_dt, s5_b_re, s5_b_im, s5_c_re, s5_c_im, s5_d,
           s5_w_glu, ab_w_out, cd_w_in, mla_g_q, mla_w_q_up, mla_g_kv, mla_w_kv_up, hg_lower_bounds, hg_g_o, cd_w_out):
    x = jnp.concatenate([x_prompt.reshape(N_CTX_TOK, D_MODEL), x_sample.reshape(N_LAT_TOK, D_MODEL)], axis=0)
    cond = jnp.zeros((MOD_ROWS, D_MODEL), F32).at[0].set(c_ctx).at[1:1 + DEC_BATCH].set(c)
    mod = _modulation(cond, w_mod, b_mod)
    keep = ([], [], [], [], [], [], [])
    for l in range(DEPTH):
        j = l // 2
        if l % 2 == 0:
            q, k, v, u = _ab_in(x, mod, g_mix[l], ab_w_in[j], l)
            o_a = _attention_a(q, k, v, cache_attn_k[:, j].reshape(DEC_BATCH, PAST_LEN, A_KV_WIDTH),
                               cache_attn_v[:, j].reshape(DEC_BATCH, PAST_LEN, A_KV_WIDTH), ab_sink[j])
            y, fin_re, fin_im = _s5_scan(u, state_ssm_re[:, j], state_ssm_im[:, j], s5_lam_re[j], s5_lam_im[j],
                                         s5_log_dt[j], s5_b_re[j], s5_b_im[j], s5_c_re[j], s5_c_im[j])
            x = _ab_out(x, o_a, y, u, s5_d[j], s5_w_glu[j], ab_w_out[j], mod, l)
            keep[0].append(k[:N_CTX_TOK].reshape(BATCH, SEQ, A_KV_HEADS, HEAD_DIM))
            keep[1].append(v[:N_CTX_TOK].reshape(BATCH, SEQ, A_KV_HEADS, HEAD_DIM))
            keep[2].append(fin_re)
            keep[3].append(fin_im)
        else:
            qn, qr, ckv, kpe, kn, vm, hq, hff, hfb, hi, hg = _cd_in(
                x, mod, g_mix[l], cd_w_in[j], mla_g_q[j], mla_w_q_up[j], mla_g_kv[j], mla_w_kv_up[j], l)
            kcn, vc = _mla_cache_kv(cache_mla_ckv[:, j].reshape(DEC_BATCH * PAST_LEN, MLA_KV_RANK), mla_w_kv_up[j])
            o_c = _attention_mla(qn, qr, kn, kpe, vm, kcn, cache_mla_kpe[:, j], vc)
            s0_ctx = jnp.zeros((BATCH, 2, 2, HG_HALF, HG_HALF), F32)
            of, ob, s_fin = _hgrn_scan(hq, hff, hfb, hi, hg_lower_bounds, s0_ctx, l, 0, BATCH, SEQ)
            of, ob, _ = _hgrn_scan(hq, hff, hfb, hi, hg_lower_bounds, _hg_state_to_blocks(state_hgrn[:, j]), l,
                                   N_CTX_TOK, DEC_BATCH, DEC_SEQ, prev=(of, ob))
            x = _cd_out(x, o_c, of, ob, hg, hg_g_o[j], cd_w_out[j], mod, l)
            keep[4].append(ckv[:N_CTX_TOK].reshape(BATCH, SEQ, MLA_KV_RANK))
            keep[5].append(kpe[:N_CTX_TOK, :MLA_ROPE].reshape(BATCH, SEQ, MLA_ROPE))
            keep[6].append(_hg_state_from_blocks(s_fin))
        x = _moe(x, mod, g_ffn[l], router_w, router_b, moe_w_gate, moe_w_up, moe_w_down, l)
    y = _final_norm(x, g_final)
    return (y[:N_CTX_TOK].reshape(BATCH, SEQ, D_MODEL), y[N_CTX_TOK:].reshape(DEC_BATCH, DEC_SEQ, D_MODEL),
            jnp.stack(keep[0], 1), jnp.stack(keep[1], 1), jnp.stack(keep[2], 1), jnp.stack(keep[3], 1),
            jnp.stack(keep[4], 1), jnp.stack(keep[5], 1), jnp.stack(keep[6], 1))
```

```python
import functools
import math

import numpy as np
import jax
import jax.numpy as jnp
from jax import lax
from jax.experimental import pallas as pl
from jax.experimental.pallas import tpu as pltpu

F32 = jnp.float32
BF16 = jnp.bfloat16

D_MODEL = 1024
BATCH = 16
SEQ = 256
DEPTH = 2
DEC_BATCH = 4
DEC_SEQ = 1024
PAST_LEN = 512
GRID_W = 64
N_MOD = 6
EPS = 1e-6
NEG_INF = -1e30
ROPE_BASE = 10000.0
HEAD_DIM = 64
A_HEADS = 8
A_KV_HEADS = 2
A_WINDOW = 128
A_WIDTH = A_HEADS * HEAD_DIM
A_KV_WIDTH = A_KV_HEADS * HEAD_DIM
S5_WIDTH = D_MODEL // 2
S5_GROUP = 16
S5_GROUPS = S5_WIDTH // S5_GROUP
S5_STATE = 64
MLA_HEADS = 8
MLA_Q_RANK = D_MODEL // 4
MLA_KV_RANK = D_MODEL // 8
MLA_NOPE = 64
MLA_ROPE = 32
MLA_V = 64
MLA_WIDTH = MLA_HEADS * MLA_V
HG_HEADS = 8
HG_KEY = 64
HG_VAL = 64
HG_KD = HG_HEADS * HG_KEY
HG_WIDTH = HG_HEADS * HG_VAL
HG_CHUNK = 16
N_EXPERTS = 16
N_GROUPS = 4
EXPERTS_PER_GROUP = N_EXPERTS // N_GROUPS
D_FF = D_MODEL // 2

N_CTX_TOK = BATCH * SEQ
N_LAT_TOK = DEC_BATCH * DEC_SEQ
N_TOK = N_CTX_TOK + N_LAT_TOK
ROW_BLK = 256
N_ROW_BLK = N_TOK // ROW_BLK
N_CTX_BLK = N_CTX_TOK // ROW_BLK
LAT_BLK_PER_SEQ = DEC_SEQ // ROW_BLK
MOD_ROWS = 8
VMEM_LIMIT = 56 * 1024 * 1024


def _cparams(*sem):
    return pltpu.CompilerParams(dimension_semantics=sem, vmem_limit_bytes=VMEM_LIMIT)


def _mod_group(i):
    return jnp.where(i < N_CTX_BLK, 0, 1 + (i - N_CTX_BLK) // LAT_BLK_PER_SEQ)


def _mod_spec(layer, which):
    return pl.BlockSpec((1, 1, D_MODEL), lambda i: ((layer * MOD_ROWS + _mod_group(i)) * N_MOD + which, 0, 0))


def _rope_blk(i):
    return jnp.where(i < N_CTX_BLK, 0, 1 + (i - N_CTX_BLK) % LAT_BLK_PER_SEQ)


def _rope_tables(rot_dim):
    n_freq = rot_dim // 4
    t = np.arange(DEC_SEQ)
    rows = (t // GRID_W).astype(np.float32)
    cols = (t % GRID_W).astype(np.float32)
    inv = (np.float32(ROPE_BASE) ** (-np.arange(n_freq, dtype=np.float32) / np.float32(n_freq))).astype(np.float32)
    ang_r = rows[:, None] * inv[None, :]
    ang_c = cols[:, None] * inv[None, :]
    ang = np.concatenate([ang_r, ang_r, ang_c, ang_c], axis=-1).astype(np.float32)
    reps = 128 // rot_dim
    cos = np.tile(np.cos(ang), (1, reps)).astype(np.float32)
    sin = np.tile(np.sin(ang), (1, reps)).astype(np.float32)
    lane = np.arange(128)
    first = (lane % (2 * n_freq)) < n_freq
    sin_a = np.where(first[None, :], -sin, 0.0).astype(np.float32)
    sin_b = np.where(first[None, :], 0.0, sin).astype(np.float32)
    ident = np.zeros((ROW_BLK, 128), np.float32)
    cos = np.concatenate([ident + 1.0, cos], axis=0)
    sin_a = np.concatenate([ident, sin_a], axis=0)
    sin_b = np.concatenate([ident, sin_b], axis=0)
    return jnp.asarray(cos), jnp.asarray(sin_a), jnp.asarray(sin_b)


def _apply_rope(x, cos, sin_a, sin_b, quarter):
    outs = []
    for j in range(x.shape[1] // 128):
        xt = x[:, 128 * j:128 * (j + 1)]
        up = pltpu.roll(xt, 128 - quarter, axis=1)
        dn = pltpu.roll(xt, quarter, axis=1)
        outs.append(xt * cos + up * sin_a + dn * sin_b)
    return outs[0] if len(outs) == 1 else jnp.concatenate(outs, axis=1)


def _rms(x):
    return x * lax.rsqrt(jnp.mean(x * x, axis=-1, keepdims=True) + EPS)


def _norm_mod(x, g, sc, sh):
    return _rms(x) * g * (1.0 + sc) + sh


def _dot(a, b):
    return jnp.dot(a.astype(BF16), b.astype(BF16), preferred_element_type=F32)


def _dot_nt(a, b):
    return lax.dot_general(a.astype(BF16), b.astype(BF16), (((1,), (1,)), ((), ())), preferred_element_type=F32)


def _mod_kernel(cond_ref, w_ref, b_ref, o_ref):
    c = cond_ref[...]
    s = c * jax.nn.sigmoid(c)
    o_ref[0] = _dot(s, w_ref[0]) + b_ref[0]


def _modulation(cond, w_mod, b_mod):
    nb = 1024
    out = pl.pallas_call(
        _mod_kernel,
        grid=(DEPTH, N_MOD * D_MODEL // nb),
        in_specs=[pl.BlockSpec((MOD_ROWS, D_MODEL), lambda l, n: (0, 0)),
                  pl.BlockSpec((1, D_MODEL, nb), lambda l, n: (l, 0, n)),
                  pl.BlockSpec((1, 1, nb), lambda l, n: (l, 0, n))],
        out_specs=pl.BlockSpec((1, MOD_ROWS, nb), lambda l, n: (l, 0, n)),
        out_shape=jax.ShapeDtypeStruct((DEPTH, MOD_ROWS, N_MOD * D_MODEL), F32),
        compiler_params=_cparams("arbitrary", "arbitrary"),
        name="modulation",
    )(cond, w_mod, b_mod.reshape(DEPTH, 1, N_MOD * D_MODEL))
    return out.reshape(DEPTH * MOD_ROWS * N_MOD, 1, D_MODEL)


def _ab_in_kernel(x_ref, g_ref, sc_ref, sh_ref, w_ref, cos_ref, sa_ref, sb_ref,
                  q_ref, k_ref, v_ref, u_ref, wq_s, wk_s, wv_s, wu_s):
    @pl.when(pl.program_id(0) == 0)
    def _():
        wq_s[...] = w_ref[:, 0:A_WIDTH].astype(BF16)
        wk_s[...] = w_ref[:, A_WIDTH:A_WIDTH + A_KV_WIDTH].astype(BF16)
        wv_s[...] = w_ref[:, A_WIDTH + A_KV_WIDTH:A_WIDTH + 2 * A_KV_WIDTH].astype(BF16)
        wu_s[...] = w_ref[:, A_WIDTH + 2 * A_KV_WIDTH:].astype(BF16)

    h = _norm_mod(x_ref[...], g_ref[...], sc_ref[0], sh_ref[0]).astype(BF16)
    cos, sa, sb = cos_ref[...], sa_ref[...], sb_ref[...]
    q = jnp.dot(h, wq_s[...], preferred_element_type=F32)
    q_ref[...] = _apply_rope(q, cos, sa, sb, HEAD_DIM // 4)
    k = jnp.dot(h, wk_s[...], preferred_element_type=F32)
    k_ref[...] = _apply_rope(k, cos, sa, sb, HEAD_DIM // 4)
    v_ref[...] = jnp.dot(h, wv_s[...], preferred_element_type=F32)
    u_ref[...] = jnp.dot(h, wu_s[...], preferred_element_type=F32)


def _ab_in(x, mod, g, w, layer):
    cos, sa, sb = _rope_tables(HEAD_DIM)
    d_in = w.shape[1]
    row = lambda n: pl.BlockSpec((ROW_BLK, n), lambda i: (i, 0))
    rope = pl.BlockSpec((ROW_BLK, 128), lambda i: (_rope_blk(i), 0))
    return pl.pallas_call(
        _ab_in_kernel,
        grid=(N_ROW_BLK,),
        in_specs=[row(D_MODEL), pl.BlockSpec((1, D_MODEL), lambda i: (0, 0)),
                  _mod_spec(layer, 1), _mod_spec(layer, 0),
                  pl.BlockSpec((D_MODEL, d_in), lambda i: (0, 0)), rope, rope, rope],
        out_specs=[row(A_WIDTH), row(A_KV_WIDTH), row(A_KV_WIDTH), row(S5_WIDTH)],
        out_shape=[jax.ShapeDtypeStruct((N_TOK, A_WIDTH), F32), jax.ShapeDtypeStruct((N_TOK, A_KV_WIDTH), F32),
                   jax.ShapeDtypeStruct((N_TOK, A_KV_WIDTH), F32), jax.ShapeDtypeStruct((N_TOK, S5_WIDTH), F32)],
        scratch_shapes=[pltpu.VMEM((D_MODEL, A_WIDTH), BF16), pltpu.VMEM((D_MODEL, A_KV_WIDTH), BF16),
                        pltpu.VMEM((D_MODEL, A_KV_WIDTH), BF16), pltpu.VMEM((D_MODEL, S5_WIDTH), BF16)],
        compiler_params=_cparams("arbitrary"),
        name="ab_in",
    )(x, g.reshape(1, D_MODEL), mod, mod, w, cos, sa, sb)


def _softmax_pv(s_list, v_list, sink):
    m = functools.reduce(jnp.maximum, [jnp.max(s, axis=-1, keepdims=True) for s in s_list])
    if sink is not None:
        m = jnp.maximum(m, sink)
    ps = [jnp.exp(s - m) for s in s_list]
    l = functools.reduce(jnp.add, [jnp.sum(p, axis=-1, keepdims=True) for p in ps])
    if sink is not None:
        l = l + jnp.exp(sink - m)
    o = functools.reduce(jnp.add, [_dot(p, v) for p, v in zip(ps, v_list)])
    return o / l


def _attn_ctx_kernel(sink_ref, q_ref, k_ref, v_ref, o_ref):
    scale = HEAD_DIM ** -0.5
    g = A_HEADS // A_KV_HEADS
    outs = []
    for h in range(A_HEADS):
        kh = h // g
        q = q_ref[:, HEAD_DIM * h:HEAD_DIM * (h + 1)]
        k = k_ref[:, HEAD_DIM * kh:HEAD_DIM * (kh + 1)]
        v = v_ref[:, HEAD_DIM * kh:HEAD_DIM * (kh + 1)]
        s = _dot_nt(q, k) * scale
        outs.append(_softmax_pv([s], [v], sink_ref[h]))
    o_ref[...] = jnp.concatenate(outs, axis=1)


def _attn_lat_kernel(sink_ref, q_ref, kp_ref, kc_ref, kn_ref, vp_ref, vc_ref, vn_ref, kx_ref, vx_ref, oin_ref, o_ref):
    del oin_ref
    n = pl.program_id(1)
    nb = DEC_SEQ // A_WINDOW
    scale = HEAD_DIM ** -0.5
    g = A_HEADS // A_KV_HEADS
    i = lax.broadcasted_iota(jnp.int32, (A_WINDOW, A_WINDOW), 0)
    j = lax.broadcasted_iota(jnp.int32, (A_WINDOW, A_WINDOW), 1)
    ok_prev = (j >= i) & (n > 0)
    ok_next = (j <= i) & (n < nb - 1)
    outs = []
    for h in range(A_HEADS):
        kh = h // g
        sl = slice(HEAD_DIM * kh, HEAD_DIM * (kh + 1))
        q = q_ref[:, HEAD_DIM * h:HEAD_DIM * (h + 1)]
        s_p = jnp.where(ok_prev, _dot_nt(q, kp_ref[:, sl]) * scale, NEG_INF)
        s_c = _dot_nt(q, kc_ref[:, sl]) * scale
        s_n = jnp.where(ok_next, _dot_nt(q, kn_ref[:, sl]) * scale, NEG_INF)
        s_x = _dot_nt(q, kx_ref[0, :, sl]) * scale
        outs.append(_softmax_pv([s_p, s_c, s_n, s_x],
                                [vp_ref[:, sl], vc_ref[:, sl], vn_ref[:, sl], vx_ref[0, :, sl]], sink_ref[h]))
    o_ref[...] = jnp.concatenate(outs, axis=1)


def _attention_a(q, k, v, cache_k, cache_v, sink):
    smem = pl.BlockSpec(memory_space=pltpu.SMEM)
    o = pl.pallas_call(
        _attn_ctx_kernel,
        grid=(BATCH,),
        in_specs=[smem, pl.BlockSpec((SEQ, A_WIDTH), lambda b: (b, 0)),
                  pl.BlockSpec((SEQ, A_KV_WIDTH), lambda b: (b, 0)), pl.BlockSpec((SEQ, A_KV_WIDTH), lambda b: (b, 0))],
        out_specs=pl.BlockSpec((SEQ, A_WIDTH), lambda b: (b, 0)),
        out_shape=jax.ShapeDtypeStruct((N_TOK, A_WIDTH), F32),
        compiler_params=_cparams("arbitrary"),
        name="attn_a_ctx",
    )(sink, q, k, v)
    nb = DEC_SEQ // A_WINDOW
    base = N_CTX_TOK // A_WINDOW
    cur = lambda b, n: (base + b * nb + n, 0)
    prev = lambda b, n: (base + b * nb + jnp.maximum(n - 1, 0), 0)
    nxt = lambda b, n: (base + b * nb + jnp.minimum(n + 1, nb - 1), 0)
    kv = lambda f: pl.BlockSpec((A_WINDOW, A_KV_WIDTH), f)
    cache = pl.BlockSpec((1, PAST_LEN, A_KV_WIDTH), lambda b, n: (b, 0, 0))
    return pl.pallas_call(
        _attn_lat_kernel,
        grid=(DEC_BATCH, nb),
        in_specs=[smem, pl.BlockSpec((A_WINDOW, A_WIDTH), cur), kv(prev), kv(cur), kv(nxt), kv(prev), kv(cur), kv(nxt),
                  cache, cache, pl.BlockSpec(memory_space=pl.ANY)],
        out_specs=pl.BlockSpec((A_WINDOW, A_WIDTH), cur),
        out_shape=jax.ShapeDtypeStruct((N_TOK, A_WIDTH), F32),
        input_output_aliases={10: 0},
        compiler_params=_cparams("arbitrary", "arbitrary"),
        name="attn_a_lat",
    )(sink, q, k, k, k, v, v, v, cache_k, cache_v, o)


S5_CHUNK = 16
S5_CP = S5_CHUNK * S5_GROUP
S5_ROWS_CTX = BATCH * SEQ // S5_CHUNK
S5_ROWS_LAT = DEC_BATCH * DEC_SEQ // S5_CHUNK
HI = lax.Precision.HIGHEST


def _s5_disc_kernel(lr_ref, li_ref, ldt_ref, ar_ref, ai_ref, zr_ref, zi_ref):
    lr, li = lr_ref[...], li_ref[...]
    dt = jnp.exp(ldt_ref[...])
    mag = jnp.exp(lr * dt)
    ar, ai = mag * jnp.cos(li * dt), mag * jnp.sin(li * dt)
    den = lr * lr + li * li
    ar_ref[...] = ar
    ai_ref[...] = ai
    zr_ref[...] = ((ar - 1.0) * lr + ai * li) / den
    zi_ref[...] = (ai * lr - (ar - 1.0) * li) / den


def _cmul(xr, xi, yr, yi):
    return xr * yr - xi * yi, xr * yi + xi * yr


def _cpow_table(ar, ai, e):
    pr = jnp.ones(jnp.broadcast_shapes(ar.shape, e.shape), F32)
    pi = jnp.zeros_like(pr)
    for b in range(5):
        bit = ((e >> b) & 1) == 1
        qr, qi = _cmul(pr, pi, ar, ai)
        pr, pi = jnp.where(bit, qr, pr), jnp.where(bit, qi, pi)
        if b < 4:
            ar, ai = _cmul(ar, ai, ar, ai)
    return pr, pi


def _s5_prep_kernel(arc_ref, aic_ref, arr_ref, air_ref, zrr_ref, zir_ref, btr_ref, bti_ref, ctr_ref, cti_ref,
                    m_ref, win_ref, wout_ref, a16_ref):
    n, cp, grp = S5_STATE, S5_CP, S5_GROUP
    lane_j = lax.broadcasted_iota(jnp.int32, (n, cp), 1) // grp
    lane = lax.broadcasted_iota(jnp.int32, (grp, cp), 1)
    sub_e = lax.broadcasted_iota(jnp.int32, (S5_CHUNK, n), 0)
    m_acc = [None] * S5_CHUNK
    for d in range(2):
        arc, aic = arc_ref[0, d], aic_ref[0, d]
        arr, air = arr_ref[0, d], air_ref[0, d]
        ctr, cti = ctr_ref[0, d], cti_ref[0, d]
        bbr = zrr_ref[0, d] * btr_ref[0, d] - zir_ref[0, d] * bti_ref[0, d]
        bbi = zrr_ref[0, d] * bti_ref[0, d] + zir_ref[0, d] * btr_ref[0, d]

        def ca(e):
            pr, pi = _cpow_table(arc, aic, e)
            return pr * ctr - pi * cti, -(pi * ctr + pr * cti)

        car, cai = ca(lane_j if d == 0 else (S5_CHUNK - 1) - lane_j)
        kflat = (jnp.dot(bbr, car, precision=HI, preferred_element_type=F32)
                 + jnp.dot(bbi, cai, precision=HI, preferred_element_type=F32))
        for s in range(S5_CHUNK):
            if d == 0:
                sh = grp * s
                blk = kflat if sh == 0 else jnp.where(lane >= sh, pltpu.roll(kflat, sh, axis=1), 0.0)
                m_acc[s] = blk
            else:
                sh = grp * (S5_CHUNK - 1 - s)
                blk = kflat if sh == 0 else jnp.where(lane < cp - sh, pltpu.roll(kflat, cp - sh, axis=1), 0.0)
                m_ref[0, grp * s:grp * (s + 1), :] = (m_acc[s] + blk).astype(BF16)
        wr, wi = ca(lane_j + 1 if d == 0 else S5_CHUNK - lane_j)
        wout_ref[0, n * d:n * (d + 1), :] = wr.astype(BF16)
        wout_ref[0, 2 * n + n * d:2 * n + n * (d + 1), :] = wi.astype(BF16)
        pr, pi = _cpow_table(arr, air, (S5_CHUNK - 1) - sub_e if d == 0 else sub_e)
        for s in range(S5_CHUNK):
            xr, xi = _cmul(pr[s:s + 1], pi[s:s + 1], bbr, bbi)
            win_ref[0, grp * s:grp * (s + 1), n * d:n * (d + 1)] = xr.astype(BF16)
            win_ref[0, grp * s:grp * (s + 1), 2 * n + n * d:2 * n + n * (d + 1)] = xi.astype(BF16)
        p16r, p16i = _cpow_table(arr, air, jnp.full((1, n), S5_CHUNK, jnp.int32))
        a16_ref[0, :, n * d:n * (d + 1)] = p16r
        a16_ref[0, :, 2 * n + n * d:2 * n + n * (d + 1)] = p16i


def _s5_main_kernel(u_ref, m_ref, win_ref, wout_ref, a16_ref, h0_ref, y_ref, hfin_ref, x_s, hs_s):
    n2 = 2 * S5_STATE
    ub = u_ref[0].astype(BF16)
    x_s[...] = jnp.dot(ub, win_ref[0], preferred_element_type=F32)
    a_re, a_im = a16_ref[0, :, 0:n2], a16_ref[0, :, n2:2 * n2]
    fwd_lane = lax.broadcasted_iota(jnp.int32, (1, n2), 1) < S5_STATE

    def run(row0, nb, nc, h_re, h_im):
        for c in range(nc):
            rf = row0 + c * nb
            rb = row0 + (nc - 1 - c) * nb
            hs_s[rf:rf + nb, 0:S5_STATE] = h_re[:, 0:S5_STATE]
            hs_s[rf:rf + nb, n2:n2 + S5_STATE] = h_im[:, 0:S5_STATE]
            hs_s[rb:rb + nb, S5_STATE:n2] = h_re[:, S5_STATE:n2]
            hs_s[rb:rb + nb, n2 + S5_STATE:2 * n2] = h_im[:, S5_STATE:n2]
            x_re = jnp.where(fwd_lane, x_s[rf:rf + nb, 0:n2], x_s[rb:rb + nb, 0:n2])
            x_im = jnp.where(fwd_lane, x_s[rf:rf + nb, n2:2 * n2], x_s[rb:rb + nb, n2:2 * n2])
            h_re, h_im = a_re * h_re - a_im * h_im + x_re, a_re * h_im + a_im * h_re + x_im
        return h_re, h_im

    zero = jnp.zeros((BATCH, n2), F32)
    f_re, f_im = run(0, BATCH, SEQ // S5_CHUNK, zero, zero)
    hfin_ref[0, :, 0:n2] = f_re
    hfin_ref[0, :, n2:2 * n2] = f_im
    run(S5_ROWS_CTX, DEC_BATCH, DEC_SEQ // S5_CHUNK, h0_ref[0, :, 0:n2], h0_ref[0, :, n2:2 * n2])
    y_ref[0] = (jnp.dot(ub, m_ref[0], preferred_element_type=F32)
                + jnp.dot(hs_s[...].astype(BF16), wout_ref[0], preferred_element_type=F32))


def _s5_to_chunks(u):
    def one(x, nb, ln):
        x = x.reshape(nb, ln // S5_CHUNK, S5_CHUNK, S5_GROUPS, S5_GROUP)
        return x.transpose(3, 1, 0, 2, 4).reshape(S5_GROUPS, nb * ln // S5_CHUNK, S5_CP)
    return jnp.concatenate([one(u[:N_CTX_TOK], BATCH, SEQ), one(u[N_CTX_TOK:], DEC_BATCH, DEC_SEQ)], axis=1)


def _s5_from_chunks(y):
    def one(x, nb, ln):
        x = x.reshape(S5_GROUPS, ln // S5_CHUNK, nb, S5_CHUNK, S5_GROUP)
        return x.transpose(2, 1, 3, 0, 4).reshape(nb * ln, S5_WIDTH)
    return jnp.concatenate([one(y[:, :S5_ROWS_CTX], BATCH, SEQ), one(y[:, S5_ROWS_CTX:], DEC_BATCH, DEC_SEQ)], axis=0)


def _s5_state_to_lanes(h_re, h_im):
    parts = [h_re[:, 0], h_re[:, 1], h_im[:, 0], h_im[:, 1]]
    return jnp.concatenate(parts, axis=-1).transpose(1, 0, 2)


def _s5_state_from_lanes(h):
    n = S5_STATE
    h = h.transpose(1, 0, 2)
    re = jnp.stack([h[..., 0:n], h[..., n:2 * n]], axis=1)
    im = jnp.stack([h[..., 2 * n:3 * n], h[..., 3 * n:4 * n]], axis=1)
    return re, im


def _s5_scan(u, h0_re, h0_im, lam_re, lam_im, log_dt, b_re, b_im, c_re, c_im):
    ng, n = S5_GROUPS, S5_STATE
    rows = 2 * ng
    disc = pl.pallas_call(
        _s5_disc_kernel,
        out_shape=[jax.ShapeDtypeStruct((rows, n), F32)] * 4,
        name="s5_disc",
    )(lam_re.reshape(rows, n), lam_im.reshape(rows, n), log_dt.reshape(rows, 1))
    ar, ai, zr, zi = [t.reshape(2, ng, n).transpose(1, 0, 2) for t in disc]
    col = lambda t: t.reshape(ng, 2, n, 1)
    row = lambda t: t.reshape(ng, 2, 1, n)
    bt = lambda t: t.transpose(1, 0, 3, 2)
    ct = lambda t: jnp.tile(t.transpose(1, 0, 3, 2), (1, 1, 1, S5_CHUNK))
    spec = lambda *s: pl.BlockSpec((1,) + s, lambda g: (g,) + (0,) * len(s))
    mat = jax.ShapeDtypeStruct((ng, S5_CP, S5_CP), BF16)
    m, win, wout, a16 = pl.pallas_call(
        _s5_prep_kernel,
        grid=(ng,),
        in_specs=[spec(2, n, 1), spec(2, n, 1), spec(2, 1, n), spec(2, 1, n), spec(2, 1, n), spec(2, 1, n),
                  spec(2, S5_GROUP, n), spec(2, S5_GROUP, n), spec(2, n, S5_CP), spec(2, n, S5_CP)],
        out_specs=[spec(S5_CP, S5_CP), spec(S5_CP, S5_CP), spec(S5_CP, S5_CP), spec(1, 4 * n)],
        out_shape=[mat, mat, mat, jax.ShapeDtypeStruct((ng, 1, 4 * n), F32)],
        compiler_params=_cparams("arbitrary"),
        name="s5_prep",
    )(col(ar), col(ai), row(ar), row(ai), row(zr), row(zi), bt(b_re), bt(b_im), ct(c_re), ct(c_im))
    nrow = S5_ROWS_CTX + S5_ROWS_LAT
    y, hfin = pl.pallas_call(
        _s5_main_kernel,
        grid=(ng,),
        in_specs=[spec(nrow, S5_CP), spec(S5_CP, S5_CP), spec(S5_CP, S5_CP), spec(S5_CP, S5_CP), spec(1, 4 * n),
                  spec(DEC_BATCH, 4 * n)],
        out_specs=[spec(nrow, S5_CP), spec(BATCH, 4 * n)],
        out_shape=[jax.ShapeDtypeStruct((ng, nrow, S5_CP), F32), jax.ShapeDtypeStruct((ng, BATCH, 4 * n), F32)],
        scratch_shapes=[pltpu.VMEM((nrow, 4 * n), F32), pltpu.VMEM((nrow, 4 * n), F32)],
        compiler_params=_cparams("arbitrary"),
        name="s5_main",
    )(_s5_to_chunks(u), m, win, wout, a16, _s5_state_to_lanes(h0_re, h0_im))
    fin_re, fin_im = _s5_state_from_lanes(hfin)
    return _s5_from_chunks(y), fin_re, fin_im


def _ab_out_kernel(x_ref, oa_ref, y_ref, u_ref, d_ref, wglu_ref, wout_ref, gt_ref, o_ref, wglu_s, wa_s, wb_s):
    @pl.when(pl.program_id(0) == 0)
    def _():
        wglu_s[...] = wglu_ref[...].astype(BF16)
        wa_s[...] = wout_ref[0:A_WIDTH, :].astype(BF16)
        wb_s[...] = wout_ref[A_WIDTH:, :].astype(BF16)

    g = jax.nn.gelu(y_ref[...] + d_ref[...] * u_ref[...])
    ob = g * jax.nn.sigmoid(jnp.dot(g.astype(BF16), wglu_s[...], preferred_element_type=F32))
    out = (jnp.dot(oa_ref[...].astype(BF16), wa_s[...], preferred_element_type=F32)
           + jnp.dot(ob.astype(BF16), wb_s[...], preferred_element_type=F32))
    o_ref[...] = x_ref[...] + gt_ref[0] * out


def _ab_out(x, oa, y, u, d_skip, w_glu, w_out, mod, layer):
    row = lambda n: pl.BlockSpec((ROW_BLK, n), lambda i: (i, 0))
    full = lambda a, b: pl.BlockSpec((a, b), lambda i: (0, 0))
    return pl.pallas_call(
        _ab_out_kernel,
        grid=(N_ROW_BLK,),
        in_specs=[row(D_MODEL), row(A_WIDTH), row(S5_WIDTH), row(S5_WIDTH), full(1, S5_WIDTH),
                  full(S5_WIDTH, S5_WIDTH), full(A_WIDTH + S5_WIDTH, D_MODEL), _mod_spec(layer, 2)],
        out_specs=row(D_MODEL),
        out_shape=jax.ShapeDtypeStruct((N_TOK, D_MODEL), F32),
        scratch_shapes=[pltpu.VMEM((S5_WIDTH, S5_WIDTH), BF16), pltpu.VMEM((A_WIDTH, D_MODEL), BF16),
                        pltpu.VMEM((S5_WIDTH, D_MODEL), BF16)],
        compiler_params=_cparams("arbitrary"),
        name="ab_out",
    )(x, oa, y, u, d_skip.reshape(1, S5_WIDTH), w_glu, w_out, mod)


CD_HG0 = MLA_Q_RANK + MLA_KV_RANK + MLA_ROPE
KPE_LANES = 128


def _cd_in_kernel(x_ref, g_ref, sc_ref, sh_ref, w_ref, gq_ref, wqu_ref, gkv_ref, wkvu_ref, cos_ref, sa_ref, sb_ref,
                  qn_ref, qr_ref, ckv_ref, kpe_ref, kn_ref, vm_ref, hq_ref, hff_ref, hfb_ref, hi_ref, hg_ref,
                  wcq_s, wckv_s, wkpe_s, whg_s, wqu_s, wkvu_s):
    @pl.when(pl.program_id(0) == 0)
    def _():
        wcq_s[...] = w_ref[:, 0:MLA_Q_RANK].astype(BF16)
        wckv_s[...] = w_ref[:, MLA_Q_RANK:MLA_Q_RANK + MLA_KV_RANK].astype(BF16)
        kp = w_ref[:, MLA_Q_RANK + MLA_KV_RANK:CD_HG0].astype(BF16)
        wkpe_s[...] = jnp.concatenate([kp] * (KPE_LANES // MLA_ROPE), axis=1)
        whg_s[...] = w_ref[:, CD_HG0:].astype(BF16)
        wqu_s[...] = wqu_ref[...].astype(BF16)
        wkvu_s[...] = wkvu_ref[...].astype(BF16)

    h = _norm_mod(x_ref[...], g_ref[...], sc_ref[0], sh_ref[0]).astype(BF16)
    cos, sa, sb = cos_ref[...], sa_ref[...], sb_ref[...]
    nope = MLA_HEADS * MLA_NOPE
    cq = _rms(jnp.dot(h, wcq_s[...], preferred_element_type=F32)) * gq_ref[...]
    qq = jnp.dot(cq.astype(BF16), wqu_s[...], preferred_element_type=F32)
    qn_ref[...] = qq[:, 0:nope]
    qr_ref[...] = _apply_rope(qq[:, nope:], cos, sa, sb, MLA_ROPE // 4)
    ckv = _rms(jnp.dot(h, wckv_s[...], preferred_element_type=F32)) * gkv_ref[...]
    ckv_ref[...] = ckv
    kv = jnp.dot(ckv.astype(BF16), wkvu_s[...], preferred_element_type=F32)
    kn_ref[...] = kv[:, 0:nope]
    vm_ref[...] = kv[:, nope:]
    kpe_ref[...] = _apply_rope(jnp.dot(h, wkpe_s[...], preferred_element_type=F32), cos, sa, sb, MLA_ROPE // 4)
    hh = jnp.dot(h, whg_s[...], preferred_element_type=F32)
    for n, ref in enumerate((hq_ref, hff_ref, hfb_ref, hi_ref, hg_ref)):
        ref[...] = hh[:, HG_KD * n:HG_KD * (n + 1)]


def _mla_split_heads(w, a):
    k, n = w.shape
    w = w.reshape(k, MLA_HEADS, n // MLA_HEADS)
    return jnp.concatenate([w[:, :, :a].reshape(k, -1), w[:, :, a:].reshape(k, -1)], axis=1)


def _cd_in(x, mod, g, w, g_q, w_q_up, g_kv, w_kv_up, layer):
    cos, sa, sb = _rope_tables(MLA_ROPE)
    d_in = w.shape[1]
    row = lambda n: pl.BlockSpec((ROW_BLK, n), lambda i: (i, 0))
    full = lambda a, b: pl.BlockSpec((a, b), lambda i: (0, 0))
    rope = pl.BlockSpec((ROW_BLK, 128), lambda i: (_rope_blk(i), 0))
    widths = [MLA_HEADS * MLA_NOPE, MLA_HEADS * MLA_ROPE, MLA_KV_RANK, KPE_LANES, MLA_HEADS * MLA_NOPE, MLA_WIDTH] + [HG_KD] * 5
    nq = MLA_HEADS * (MLA_NOPE + MLA_ROPE)
    nkv = MLA_HEADS * (MLA_NOPE + MLA_V)
    return pl.pallas_call(
        _cd_in_kernel,
        grid=(N_ROW_BLK,),
        in_specs=[row(D_MODEL), full(1, D_MODEL), _mod_spec(layer, 1), _mod_spec(layer, 0), full(D_MODEL, d_in),
                  full(1, MLA_Q_RANK), full(MLA_Q_RANK, nq), full(1, MLA_KV_RANK), full(MLA_KV_RANK, nkv), rope, rope, rope],
        out_specs=[row(n) for n in widths],
        out_shape=[jax.ShapeDtypeStruct((N_TOK, n), F32) for n in widths],
        scratch_shapes=[pltpu.VMEM((D_MODEL, MLA_Q_RANK), BF16), pltpu.VMEM((D_MODEL, MLA_KV_RANK), BF16),
                        pltpu.VMEM((D_MODEL, KPE_LANES), BF16), pltpu.VMEM((D_MODEL, 5 * HG_KD), BF16),
                        pltpu.VMEM((MLA_Q_RANK, nq), BF16), pltpu.VMEM((MLA_KV_RANK, nkv), BF16)],
        compiler_params=_cparams("arbitrary"),
        name="cd_in",
    )(x, g.reshape(1, D_MODEL), mod, mod, w, g_q.reshape(1, -1), _mla_split_heads(w_q_up, MLA_NOPE),
      g_kv.reshape(1, -1), _mla_split_heads(w_kv_up, MLA_NOPE), cos, sa, sb)


def _mm_kernel(a_ref, w_ref, *o_refs):
    r = _dot(a_ref[...], w_ref[...])
    off = 0
    for o in o_refs:
        o[...] = r[:, off:off + o.shape[1]]
        off += o.shape[1]


def _mla_cache_kv(cckv, w_kv_up):
    n = MLA_HEADS * MLA_NOPE
    rows = cckv.shape[0]
    return pl.pallas_call(
        _mm_kernel,
        grid=(rows // PAST_LEN,),
        in_specs=[pl.BlockSpec((PAST_LEN, MLA_KV_RANK), lambda i: (i, 0)),
                  pl.BlockSpec((MLA_KV_RANK, 2 * n), lambda i: (0, 0))],
        out_specs=[pl.BlockSpec((PAST_LEN, n), lambda i: (i, 0))] * 2,
        out_shape=[jax.ShapeDtypeStruct((rows, n), F32)] * 2,
        compiler_params=_cparams("arbitrary"),
        name="mla_cache_kv",
    )(cckv, _mla_split_heads(w_kv_up, MLA_NOPE))


def _mla_heads(qn, qr, keys, o_ref):
    scale = (MLA_NOPE + MLA_ROPE) ** -0.5
    outs = []
    for h in range(MLA_HEADS):
        a = slice(MLA_NOPE * h, MLA_NOPE * (h + 1))
        r = slice(MLA_ROPE * h, MLA_ROPE * (h + 1))
        s_list = [(_dot_nt(qn[:, a], kn[:, a]) + _dot_nt(qr[:, r], kp[:, 0:MLA_ROPE])) * scale for kn, kp, _ in keys]
        outs.append(_softmax_pv(s_list, [v[:, a] for _, _, v in keys], None))
    o_ref[...] = jnp.concatenate(outs, axis=1)


def _mla_ctx_kernel(qn_ref, qr_ref, kn_ref, kp_ref, v_ref, o_ref):
    _mla_heads(qn_ref[...], qr_ref[...], [(kn_ref[...], kp_ref[...], v_ref[...])], o_ref)


def _mla_lat_kernel(qn_ref, qr_ref, kn_ref, kp_ref, v_ref, kcn_ref, kcp_ref, vc_ref, oin_ref, o_ref):
    del oin_ref
    _mla_heads(qn_ref[...], qr_ref[...],
               [(kcn_ref[...], kcp_ref[0], vc_ref[...]), (kn_ref[...], kp_ref[...], v_ref[...])], o_ref)


def _attention_mla(qn, qr, kn, kpe, vm, kcn, kcpe, vc):
    n = MLA_HEADS * MLA_NOPE
    nr = MLA_HEADS * MLA_ROPE
    blk = lambda w: pl.BlockSpec((SEQ, w), lambda b: (b, 0))
    o = pl.pallas_call(
        _mla_ctx_kernel,
        grid=(BATCH,),
        in_specs=[blk(n), blk(nr), blk(n), blk(KPE_LANES), blk(MLA_WIDTH)],
        out_specs=blk(MLA_WIDTH),
        out_shape=jax.ShapeDtypeStruct((N_TOK, MLA_WIDTH), F32),
        compiler_params=_cparams("arbitrary"),
        name="mla_ctx",
    )(qn, qr, kn, kpe, vm)
    nq = DEC_SEQ // ROW_BLK
    qblk = lambda w: pl.BlockSpec((ROW_BLK, w), lambda b, i: (N_CTX_BLK + b * nq + i, 0))
    seq = lambda w: pl.BlockSpec((DEC_SEQ, w), lambda b, i: (N_CTX_TOK // DEC_SEQ + b, 0))
    past = lambda w: pl.BlockSpec((PAST_LEN, w), lambda b, i: (b, 0))
    return pl.pallas_call(
        _mla_lat_kernel,
        grid=(DEC_BATCH, nq),
        in_specs=[qblk(n), qblk(nr), seq(n), seq(KPE_LANES), seq(MLA_WIDTH), past(n),
                  pl.BlockSpec((1, PAST_LEN, MLA_ROPE), lambda b, i: (b, 0, 0)), past(MLA_WIDTH),
                  pl.BlockSpec(memory_space=pl.ANY)],
        out_specs=qblk(MLA_WIDTH),
        out_shape=jax.ShapeDtypeStruct((N_TOK, MLA_WIDTH), F32),
        input_output_aliases={8: 0},
        compiler_params=_cparams("arbitrary", "arbitrary"),
        name="mla_lat",
    )(qn, qr, kn, kpe, vm, kcn, kcpe, vc, o)


HG_TILE = 128
HG_NC = HG_TILE // HG_CHUNK
HG_HALF = 256


HG_SLABS = HG_KD // 128


def _hg_put(ref, x):
    for j in range(HG_SLABS):
        ref[j] = x[:, 128 * j:128 * (j + 1)]


def _hg_get(ref, r, l):
    return jnp.concatenate([ref[2 * l, r, :], ref[2 * l + 1, r, :]], axis=1)


def _hg_plane(ref, p):
    return jnp.concatenate([ref[j, pl.ds(p, HG_NC, stride=HG_CHUNK), :] for j in range(HG_SLABS)], axis=1)


def _hg_put_plane(ref, p, x):
    for j in range(HG_SLABS):
        ref[j, pl.ds(p, HG_NC, stride=HG_CHUNK), :] = x[:, 128 * j:128 * (j + 1)]


def _hg_direction(hq_ref, hf_ref, hi_ref, o_ref, lb, st_ref, q_s, f_s, kk_s, v_s, qt_s, kt_s, o_s, sign,
                  ones_bd, diag_mask):
    hq = hq_ref[...]
    _hg_put(q_s, hq * jax.nn.sigmoid(hq))
    f = lb + (1.0 - lb) * jax.nn.sigmoid(hf_ref[...])
    _hg_put(f_s, f)
    _hg_put(kk_s, 1.0 - f)
    _hg_put(v_s, hi_ref[...])
    pos = (lambda i: i) if sign > 0 else (lambda i: HG_CHUNK - 1 - i)
    q = [_hg_plane(q_s, pos(i)) for i in range(HG_CHUNK)]
    fp = [_hg_plane(f_s, pos(i)) for i in range(HG_CHUNK)]
    kk = [_hg_plane(kk_s, pos(i)) for i in range(HG_CHUNK)]
    v = [_hg_plane(v_s, pos(i)) for i in range(HG_CHUNK)]

    def expand(planes):
        pb = jnp.concatenate(planes, axis=0).astype(BF16)
        e = jnp.concatenate([jnp.dot(pb[:, 0:HG_HALF], ones_bd, preferred_element_type=F32),
                             jnp.dot(pb[:, HG_HALF:], ones_bd, preferred_element_type=F32)], axis=1)
        return [e[HG_NC * n:HG_NC * (n + 1)] for n in range(len(planes))]

    o = [e * v[i] for i, e in enumerate(expand([q[i] * kk[i] for i in range(HG_CHUNK)]))]
    dec = list(fp)
    for d in range(1, HG_CHUNK):
        if d > 1:
            for i in range(d, HG_CHUNK):
                dec[i] = dec[i] * fp[i - d + 1]
        es = expand([q[i] * kk[i - d] * dec[i] for i in range(d, HG_CHUNK)])
        for n, i in enumerate(range(d, HG_CHUNK)):
            o[i] = o[i] + es[n] * v[i - d]
    for i in range(HG_CHUNK):
        _hg_put_plane(o_s, pos(i), o[i])

    incl = [fp[0]]
    for i in range(1, HG_CHUNK):
        incl.append(incl[-1] * fp[i])
    excl = [None] * HG_CHUNK
    excl[HG_CHUNK - 1] = jnp.ones_like(fp[0])
    for i in range(HG_CHUNK - 2, -1, -1):
        excl[i] = excl[i + 1] * fp[i + 1]
    for i in range(HG_CHUNK):
        _hg_put_plane(qt_s, pos(i), q[i] * incl[i])
        _hg_put_plane(kt_s, pos(i), kk[i] * excl[i])
    whole = incl[HG_CHUNK - 1]
    for cc in range(HG_NC):
        c = cc if sign > 0 else HG_NC - 1 - cc
        r = slice(HG_CHUNK * c, HG_CHUNK * (c + 1))
        for g in range(2):
            l = slice(HG_HALF * g, HG_HALF * (g + 1))
            st = st_ref[g]
            o_ref[r, l] = _hg_get(o_s, r, g) + _dot_nt(_hg_get(qt_s, r, g), st)
            ds = lax.dot_general(hi_ref[r, l].astype(BF16), _hg_get(kt_s, r, g).astype(BF16),
                                 (((0,), (0,)), ((), ())), preferred_element_type=F32)
            st_ref[g] = st * whole[c:c + 1, l] + jnp.where(diag_mask, ds, 0.0)


def _hg_lower_bound(lb_ref, d, layer):
    raw = [lb_ref[d * DEPTH + m:d * DEPTH + m + 1, :] for m in range(DEPTH)]
    mx = functools.reduce(jnp.maximum, raw)
    e = [jnp.exp(r - mx) for r in raw]
    tot = functools.reduce(jnp.add, e)
    return functools.reduce(jnp.add, e[1:layer + 1], jnp.zeros_like(tot)) / tot


def _hgrn_kernel(nt, layer, hqf_ref, hff_ref, hif_ref, hqb_ref, hfb_ref, hib_ref, lb_ref, s0_ref, *rest):
    of_ref, ob_ref, sfin_ref, st_s, q_s, f_s, kk_s, v_s, qt_s, kt_s, o_s = rest[-11:]
    i = pl.program_id(1)

    @pl.when(i == 0)
    def _():
        st_s[...] = s0_ref[0]

    ri = lax.broadcasted_iota(jnp.int32, (HG_HALF, HG_HALF), 0) // HG_KEY
    ci = lax.broadcasted_iota(jnp.int32, (HG_HALF, HG_HALF), 1) // HG_KEY
    diag_mask = ri == ci
    ones_bd = jnp.where(diag_mask, 1.0, 0.0).astype(BF16)
    _hg_direction(hqf_ref, hff_ref, hif_ref, of_ref, _hg_lower_bound(lb_ref, 0, layer), st_s.at[0],
                  q_s, f_s, kk_s, v_s, qt_s, kt_s, o_s, 1, ones_bd, diag_mask)
    _hg_direction(hqb_ref, hfb_ref, hib_ref, ob_ref, _hg_lower_bound(lb_ref, 1, layer), st_s.at[1],
                  q_s, f_s, kk_s, v_s, qt_s, kt_s, o_s, -1, ones_bd, diag_mask)

    @pl.when(i == nt - 1)
    def _():
        sfin_ref[0] = st_s[...]


def _hg_state_to_blocks(s):
    b = s.shape[0]
    st = s.transpose(0, 1, 2, 4, 3).reshape(b, 2, 2, 4, HG_VAL, HG_KEY)
    eye = jnp.eye(4, dtype=s.dtype)
    return jnp.einsum('bdgivk,ij->bdgivjk', st, eye).reshape(b, 2, 2, HG_HALF, HG_HALF)


def _hg_state_from_blocks(st):
    b = st.shape[0]
    st = st.reshape(b, 2, 2, 4, HG_VAL, 4, HG_KEY)
    diag = jnp.stack([st[:, :, :, i, :, i, :] for i in range(4)], axis=3)
    return diag.reshape(b, 2, HG_HEADS, HG_VAL, HG_KEY).transpose(0, 1, 2, 4, 3)


def _hgrn_scan(hq, hff, hfb, hi, lb, s0, layer, row0, nseq, seqlen, prev=None):
    nt = seqlen // HG_TILE
    base = row0 // HG_TILE
    fwd = pl.BlockSpec((HG_TILE, HG_KD), lambda b, i: (base + b * nt + i, 0))
    bwd = pl.BlockSpec((HG_TILE, HG_KD), lambda b, i: (base + b * nt + nt - 1 - i, 0))
    anyspec = pl.BlockSpec(memory_space=pl.ANY)
    extra = [] if prev is None else list(prev)
    st = pl.BlockSpec((1, 2, 2, HG_HALF, HG_HALF), lambda b, i: (b, 0, 0, 0, 0))
    tile = lambda: pltpu.VMEM((HG_SLABS, HG_TILE, 128), F32)
    return pl.pallas_call(
        functools.partial(_hgrn_kernel, nt, layer),
        grid=(nseq, nt),
        in_specs=[fwd, fwd, fwd, bwd, bwd, bwd, pl.BlockSpec((2 * DEPTH, HG_KD), lambda b, i: (0, 0)), st]
        + [anyspec] * len(extra),
        out_specs=[fwd, bwd, st],
        input_output_aliases={8 + n: n for n in range(len(extra))},
        out_shape=[jax.ShapeDtypeStruct((N_TOK, HG_WIDTH), F32)] * 2
        + [jax.ShapeDtypeStruct((nseq, 2, 2, HG_HALF, HG_HALF), F32)],
        scratch_shapes=[pltpu.VMEM((2, 2, HG_HALF, HG_HALF), F32)] + [tile() for _ in range(7)],
        compiler_params=_cparams("arbitrary", "arbitrary"),
        name="hgrn_scan",
    )(hq, hff, hi, hq, hfb, hi, lb.reshape(2 * DEPTH, HG_KD), s0, *extra)


def _cd_out_kernel(x_ref, oc_ref, of_ref, ob_ref, hg_ref, go_ref, wout_ref, gt_ref, o_ref, wa_s, wb_s):
    @pl.when(pl.program_id(0) == 0)
    def _():
        wa_s[...] = wout_ref[0:MLA_WIDTH, :].astype(BF16)
        wb_s[...] = wout_ref[MLA_WIDTH:, :].astype(BF16)

    ri = lax.broadcasted_iota(jnp.int32, (HG_HALF, HG_HALF), 0) // HG_VAL
    ci = lax.broadcasted_iota(jnp.int32, (HG_HALF, HG_HALF), 1) // HG_VAL
    ones_bd = jnp.where(ri == ci, 1.0, 0.0).astype(BF16)
    o = of_ref[...] + ob_ref[...]
    sq = o * o
    hi = sq.astype(BF16)
    lo = (sq - hi.astype(F32)).astype(BF16)
    ms = jnp.concatenate(
        [jnp.dot(hi[:, l], ones_bd, preferred_element_type=F32) + jnp.dot(lo[:, l], ones_bd, preferred_element_type=F32)
         for l in (slice(0, HG_HALF), slice(HG_HALF, 2 * HG_HALF))], axis=1) * (1.0 / HG_VAL)
    hg = hg_ref[...]
    od = o * lax.rsqrt(ms + EPS) * go_ref[...] * (hg * jax.nn.sigmoid(hg))
    out = (jnp.dot(oc_ref[...].astype(BF16), wa_s[...], preferred_element_type=F32)
           + jnp.dot(od.astype(BF16), wb_s[...], preferred_element_type=F32))
    o_ref[...] = x_ref[...] + gt_ref[0] * out


def _cd_out(x, oc, of, ob, hg, g_o, w_out, mod, layer):
    row = lambda n: pl.BlockSpec((ROW_BLK, n), lambda i: (i, 0))
    full = lambda a, b: pl.BlockSpec((a, b), lambda i: (0, 0))
    return pl.pallas_call(
        _cd_out_kernel,
        grid=(N_ROW_BLK,),
        in_specs=[row(D_MODEL), row(MLA_WIDTH), row(HG_WIDTH), row(HG_WIDTH), row(HG_WIDTH), full(1, HG_WIDTH),
                  full(MLA_WIDTH + HG_WIDTH, D_MODEL), _mod_spec(layer, 2)],
        out_specs=row(D_MODEL),
        out_shape=jax.ShapeDtypeStruct((N_TOK, D_MODEL), F32),
        scratch_shapes=[pltpu.VMEM((MLA_WIDTH, D_MODEL), BF16), pltpu.VMEM((HG_WIDTH, D_MODEL), BF16)],
        compiler_params=_cparams("arbitrary"),
        name="cd_out",
    )(x, oc, of, ob, hg, jnp.tile(g_o, HG_HEADS).reshape(1, HG_WIDTH), w_out, mod)


N_PAIRS = 6
N_CLASSES = N_GROUPS * N_PAIRS
CLS_ROWS = 32
MOE_BM = 256
MOE_NBLK = N_TOK // MOE_BM + N_CLASSES
MOE_ROWS = MOE_NBLK * MOE_BM
PACK_W = D_MODEL // 2
PACK_ROW = PACK_W + 128


def _moe_route_kernel(x_ref, g_ref, sc_ref, sh_ref, rw_ref, rb_ref, hp_ref, ti_ref, cnt_ref, base_s):
    @pl.when(pl.program_id(0) == 0)
    def _():
        base_s[...] = jnp.zeros_like(base_s)

    h = _norm_mod(x_ref[...], g_ref[...], sc_ref[0], sh_ref[0])
    logits = lax.dot_general(rw_ref[...], h, (((1,), (1,)), ((), ())), precision=HI,
                             preferred_element_type=F32)
    aff = jax.nn.sigmoid(logits)
    sel = aff + rb_ref[...]
    s = [sel[e:e + 1, :] for e in range(N_EXPERTS)]
    a = [aff[e:e + 1, :] for e in range(N_EXPERTS)]
    gs = []
    for g in range(N_GROUPS):
        m = s[4 * g:4 * g + 4]
        pairs = [m[i] + m[j] for i in range(4) for j in range(i + 1, 4)]
        gs.append(functools.reduce(jnp.maximum, pairs))
    gmax = functools.reduce(jnp.maximum, gs)
    taken = jnp.zeros_like(gmax) > 1.0
    gsel = []
    for g in range(N_GROUPS):
        hit = (gs[g] == gmax) & jnp.logical_not(taken)
        gsel.append(hit)
        taken = taken | hit
    e_lo = jnp.zeros(gmax.shape, jnp.int32)
    e_hi = jnp.zeros(gmax.shape, jnp.int32)
    a_lo = jnp.zeros_like(gmax)
    a_hi = jnp.zeros_like(gmax)
    nsel = jnp.zeros(gmax.shape, jnp.int32)
    for g in range(N_GROUPS):
        for i in range(4):
            e = 4 * g + i
            beat = jnp.zeros(gmax.shape, jnp.int32)
            for j in range(4):
                if j != i:
                    o = 4 * g + j
                    beat = beat + jnp.where((s[o] > s[e]) | ((s[o] == s[e]) & (j < i)), 1, 0)
            pick = gsel[g] & (beat < 2)
            is_first = pick & (nsel == 0)
            is_second = pick & (nsel == 1)
            e_lo = jnp.where(is_first, e, e_lo)
            a_lo = jnp.where(is_first, a[e], a_lo)
            e_hi = jnp.where(is_second, e, e_hi)
            a_hi = jnp.where(is_second, a[e], a_hi)
            nsel = nsel + jnp.where(pick, 1, 0)
    grp = e_lo // EXPERTS_PER_GROUP
    lo = e_lo - grp * EXPERTS_PER_GROUP
    hi = e_hi - grp * EXPERTS_PER_GROUP
    cls = grp * N_PAIRS + ((lo * (7 - lo)) >> 1) + (hi - lo - 1)
    wsum = a_lo + a_hi
    w_lo, w_hi = a_lo / wsum, a_hi / wsum
    onehot = (lax.broadcasted_iota(jnp.int32, (CLS_ROWS, ROW_BLK), 0) == cls).astype(F32)
    tt = lax.broadcasted_iota(jnp.int32, (ROW_BLK, ROW_BLK), 0) < lax.broadcasted_iota(jnp.int32, (ROW_BLK, ROW_BLK), 1)
    before = _dot(onehot, jnp.where(tt, 1.0, 0.0))
    base = base_s[...]
    rank = jnp.sum(onehot * (before + base[:, 0:1]), axis=0, keepdims=True).astype(jnp.int32)
    base = base + jnp.sum(onehot, axis=1, keepdims=True)
    base_s[...] = base
    cnt_ref[...] = base.astype(jnp.int32)
    ti_ref[0] = jnp.concatenate([cls, rank, e_lo, e_hi, jnp.zeros((4, ROW_BLK), jnp.int32)], axis=0)
    bits = pltpu.bitcast(h.astype(BF16).astype(F32), jnp.uint32)
    hp_ref[:, 0:PACK_W] = (bits[:, PACK_W:] & jnp.uint32(0xFFFF0000)) | (bits[:, 0:PACK_W] >> 16)
    lane = lax.broadcasted_iota(jnp.int32, (ROW_BLK, 128), 1)
    ident = lax.broadcasted_iota(jnp.int32, (ROW_BLK, ROW_BLK), 0) == lax.broadcasted_iota(jnp.int32, (ROW_BLK, ROW_BLK), 1)
    col = lambda r: jnp.sum(jnp.where(ident, r, 0.0), axis=1, keepdims=True)
    wl = jnp.where(lane == 0, col(w_lo), jnp.where(lane == 1, col(w_hi), 0.0))
    hp_ref[:, PACK_W:] = pltpu.bitcast(wl, jnp.uint32)


def _moe_route(x, mod, g, router_w, router_b, layer):
    row = lambda n: pl.BlockSpec((ROW_BLK, n), lambda i: (i, 0))
    full = lambda a, b: pl.BlockSpec((a, b), lambda i: (0, 0))
    return pl.pallas_call(
        _moe_route_kernel,
        grid=(N_ROW_BLK,),
        in_specs=[row(D_MODEL), full(1, D_MODEL), _mod_spec(layer, 4), _mod_spec(layer, 3),
                  full(N_EXPERTS, D_MODEL), full(N_EXPERTS, 1)],
        out_specs=[row(PACK_ROW), pl.BlockSpec((1, 8, ROW_BLK), lambda i: (i, 0, 0)), full(CLS_ROWS, 128)],
        out_shape=[jax.ShapeDtypeStruct((N_TOK, PACK_ROW), jnp.uint32),
                   jax.ShapeDtypeStruct((N_ROW_BLK, 8, ROW_BLK), jnp.int32),
                   jax.ShapeDtypeStruct((CLS_ROWS, 128), jnp.int32)],
        scratch_shapes=[pltpu.VMEM((CLS_ROWS, 128), F32)],
        compiler_params=_cparams("arbitrary"),
        name="moe_route",
    )(x, g.reshape(1, D_MODEL), mod, mod, router_w.T, router_b.reshape(N_EXPERTS, 1))


def _moe_experts_kernel(layer, elo_ref, ehi_ref, nblk_ref, pos_ref, hp_ref, wgl_ref, wul_ref, wdl_ref,
                        wgh_ref, wuh_ref, wdh_ref, y_ref, perm_s, stage_s):
    del layer, elo_ref, ehi_ref
    j = pl.program_id(0)

    @pl.when(j == 0)
    def _():
        def init(r, c):
            perm_s[r] = 0
            return c

        lax.fori_loop(0, MOE_ROWS, init, 0, unroll=16)

        def build(t, c):
            perm_s[pos_ref[t]] = t
            return c

        lax.fori_loop(0, N_TOK, build, 0, unroll=16)

    @pl.when(j < nblk_ref[0])
    def _():
        def gather(r, c):
            src = perm_s[j * MOE_BM + r]
            stage_s[pl.ds(r, 1), :] = hp_ref[pl.ds(src, 1), :]
            return c

        lax.fori_loop(0, MOE_BM, gather, 0, unroll=8)
        words = stage_s[:, 0:PACK_W]
        h_a = pltpu.bitcast(words << 16, F32).astype(BF16)
        h_b = pltpu.bitcast(words & jnp.uint32(0xFFFF0000), F32).astype(BF16)
        w = pltpu.bitcast(stage_s[:, PACK_W:], F32)
        acc = None
        for wg, wu, wd, gate in ((wgl_ref, wul_ref, wdl_ref, w[:, 0:1]), (wgh_ref, wuh_ref, wdh_ref, w[:, 1:2])):
            g = _dot(h_a, wg[0, 0, 0:PACK_W, :]) + _dot(h_b, wg[0, 0, PACK_W:, :])
            u = _dot(h_a, wu[0, 0, 0:PACK_W, :]) + _dot(h_b, wu[0, 0, PACK_W:, :])
            hid = g * jax.nn.sigmoid(g) * u * gate
            y = _dot(hid, wd[0, 0])
            acc = y if acc is None else acc + y
        y_ref[...] = acc

    @pl.when(j >= nblk_ref[0])
    def _():
        y_ref[...] = jnp.zeros_like(y_ref)


def _moe_experts(hp, blk_elo, blk_ehi, nblk, pos, w_gate, w_up, w_down, layer):
    last = lambda j, nb: jnp.minimum(j, nb[0] - 1)
    wspec = lambda a, b, which: pl.BlockSpec(
        (1, 1, a, b), lambda j, elo, ehi, nb, pos: (layer, (elo, ehi)[which][last(j, nb)], 0, 0))
    grid_spec = pltpu.PrefetchScalarGridSpec(
        num_scalar_prefetch=4,
        grid=(MOE_NBLK,),
        in_specs=[pl.BlockSpec((N_TOK, PACK_ROW), lambda j, *_: (0, 0), pipeline_mode=pl.Buffered(1)),
                  wspec(D_MODEL, D_FF, 0), wspec(D_MODEL, D_FF, 0), wspec(D_FF, D_MODEL, 0),
                  wspec(D_MODEL, D_FF, 1), wspec(D_MODEL, D_FF, 1), wspec(D_FF, D_MODEL, 1)],
        out_specs=pl.BlockSpec((MOE_BM, D_MODEL), lambda j, *_: (j, 0)),
        scratch_shapes=[pltpu.SMEM((MOE_ROWS,), jnp.int32), pltpu.VMEM((MOE_BM, PACK_ROW), jnp.uint32)],
    )
    return pl.pallas_call(
        functools.partial(_moe_experts_kernel, layer),
        grid_spec=grid_spec,
        out_shape=jax.ShapeDtypeStruct((MOE_ROWS, D_MODEL), F32),
        compiler_params=_cparams("arbitrary"),
        name="moe_experts",
    )(blk_elo, blk_ehi, nblk, pos, hp, w_gate, w_up, w_down, w_gate, w_up, w_down)


def _moe_combine_kernel(pos_ref, x_ref, y_ref, gt_ref, o_ref, stage_s):
    i = pl.program_id(1)

    def gather(r, c):
        stage_s[pl.ds(r, 1), :] = y_ref[pl.ds(pos_ref[i * ROW_BLK + r], 1), :]
        return c

    lax.fori_loop(0, ROW_BLK, gather, 0, unroll=8)
    o_ref[...] = x_ref[...] + gt_ref[0] * stage_s[...]


def _moe_combine(x, y_sorted, pos, mod, layer):
    half = D_MODEL // 2
    grid_spec = pltpu.PrefetchScalarGridSpec(
        num_scalar_prefetch=1,
        grid=(2, N_ROW_BLK),
        in_specs=[pl.BlockSpec((ROW_BLK, half), lambda c, i, pos: (i, c)),
                  pl.BlockSpec((MOE_ROWS, half), lambda c, i, pos: (0, c), pipeline_mode=pl.Buffered(1)),
                  pl.BlockSpec((1, 1, half), lambda c, i, pos: ((layer * MOD_ROWS + _mod_group(i)) * N_MOD + 5, 0, c))],
        out_specs=pl.BlockSpec((ROW_BLK, half), lambda c, i, pos: (i, c)),
        scratch_shapes=[pltpu.VMEM((ROW_BLK, half), F32)],
    )
    return pl.pallas_call(
        _moe_combine_kernel,
        grid_spec=grid_spec,
        out_shape=jax.ShapeDtypeStruct((N_TOK, D_MODEL), F32),
        compiler_params=_cparams("arbitrary", "arbitrary"),
        name="moe_combine",
    )(pos, x, y_sorted, mod)


def _moe(x, mod, g, router_w, router_b, w_gate, w_up, w_down, layer):
    hp, info, counts = _moe_route(x, mod, g, router_w, router_b, layer)
    cls = info[:, 0, :].reshape(N_TOK)
    rank = info[:, 1, :].reshape(N_TOK)
    cnt = counts[:N_CLASSES, 0]
    nb = (cnt + MOE_BM - 1) // MOE_BM
    ends = jnp.cumsum(nb)
    starts = ends - nb
    pos = (starts * MOE_BM)[cls] + rank
    blk = jnp.arange(MOE_NBLK, dtype=jnp.int32)
    blk_cls = jnp.minimum(jnp.sum((blk[:, None] >= ends[None, :]).astype(jnp.int32), axis=1), N_CLASSES - 1)
    pair_lo = jnp.asarray([0, 0, 0, 1, 1, 2], jnp.int32)
    pair_hi = jnp.asarray([1, 2, 3, 2, 3, 3], jnp.int32)
    grp = blk_cls // N_PAIRS
    blk_elo = (grp * EXPERTS_PER_GROUP + pair_lo[blk_cls % N_PAIRS]).astype(jnp.int32)
    blk_ehi = (grp * EXPERTS_PER_GROUP + pair_hi[blk_cls % N_PAIRS]).astype(jnp.int32)
    nblk = ends[-1:].astype(jnp.int32)
    y_sorted = _moe_experts(hp, blk_elo, blk_ehi, nblk, pos.astype(jnp.int32), w_gate, w_up, w_down, layer)
    return _moe_combine(x, y_sorted, pos.astype(jnp.int32), mod, layer)


def _final_norm_kernel(x_ref, g_ref, o_ref):
    o_ref[...] = _rms(x_ref[...]) * g_ref[...]


def _final_norm(x, g, row0, rows):
    base = row0 // ROW_BLK
    return pl.pallas_call(
        _final_norm_kernel,
        grid=(rows // ROW_BLK,),
        in_specs=[pl.BlockSpec((ROW_BLK, D_MODEL), lambda i: (base + i, 0)), pl.BlockSpec((1, D_MODEL), lambda i: (0, 0))],
        out_specs=pl.BlockSpec((ROW_BLK, D_MODEL), lambda i: (i, 0)),
        out_shape=jax.ShapeDtypeStruct((rows, D_MODEL), F32),
        compiler_params=_cparams("arbitrary"),
        name="final_norm",
    )(x, g.reshape(1, D_MODEL))


def kernel(x_prompt, x_sample, cache_attn_k, cache_attn_v, state_ssm_re, state_ssm_im, cache_mla_ckv, cache_mla_kpe,
           state_hgrn, c, c_ctx, w_mod, b_mod, g_mix, g_ffn, g_final, router_w, router_b, moe_w_gate, moe_w_up,
           moe_w_down, ab_w_in, ab_sink, s5_lam_re, s5_lam_im, s5_log_dt, s5_b_re, s5_b_im, s5_c_re, s5_c_im, s5_d,
           s5_w_glu, ab_w_out, cd_w_in, mla_g_q, mla_w_q_up, mla_g_kv, mla_w_kv_up, hg_lower_bounds, hg_g_o, cd_w_out):
    x = jnp.concatenate([x_prompt.reshape(N_CTX_TOK, D_MODEL), x_sample.reshape(N_LAT_TOK, D_MODEL)], axis=0)
    cond = jnp.zeros((MOD_ROWS, D_MODEL), F32).at[0].set(c_ctx).at[1:1 + DEC_BATCH].set(c)
    mod = _modulation(cond, w_mod, b_mod)
    keep = ([], [], [], [], [], [], [])
    for l in range(DEPTH):
        j = l // 2
        if l % 2 == 0:
            q, k, v, u = _ab_in(x, mod, g_mix[l], ab_w_in[j], l)
            o_a = _attention_a(q, k, v, cache_attn_k[:, j].reshape(DEC_BATCH, PAST_LEN, A_KV_WIDTH),
                               cache_attn_v[:, j].reshape(DEC_BATCH, PAST_LEN, A_KV_WIDTH), ab_sink[j])
            y, fin_re, fin_im = _s5_scan(u, state_ssm_re[:, j], state_ssm_im[:, j], s5_lam_re[j], s5_lam_im[j],
                                         s5_log_dt[j], s5_b_re[j], s5_b_im[j], s5_c_re[j], s5_c_im[j])
            x = _ab_out(x, o_a, y, u, s5_d[j], s5_w_glu[j], ab_w_out[j], mod, l)
            keep[0].append(k[:N_CTX_TOK].reshape(BATCH, SEQ, A_KV_HEADS, HEAD_DIM))
            keep[1].append(v[:N_CTX_TOK].reshape(BATCH, SEQ, A_KV_HEADS, HEAD_DIM))
            keep[2].append(fin_re)
            keep[3].append(fin_im)
        else:
            qn, qr, ckv, kpe, kn, vm, hq, hff, hfb, hi, hg = _cd_in(
                x, mod, g_mix[l], cd_w_in[j], mla_g_q[j], mla_w_q_up[j], mla_g_kv[j], mla_w_kv_up[j], l)
            kcn, vc = _mla_cache_kv(cache_mla_ckv[:, j].reshape(DEC_BATCH * PAST_LEN, MLA_KV_RANK), mla_w_kv_up[j])
            o_c = _attention_mla(qn, qr, kn, kpe, vm, kcn, cache_mla_kpe[:, j], vc)
            s0_ctx = jnp.zeros((BATCH, 2, 2, HG_HALF, HG_HALF), F32)
            of, ob, s_fin = _hgrn_scan(hq, hff, hfb, hi, hg_lower_bounds, s0_ctx, l, 0, BATCH, SEQ)
            of, ob, _ = _hgrn_scan(hq, hff, hfb, hi, hg_lower_bounds, _hg_state_to_blocks(state_hgrn[:, j]), l,
                                   N_CTX_TOK, DEC_BATCH, DEC_SEQ, prev=(of, ob))
            x = _cd_out(x, o_c, of, ob, hg, hg_g_o[j], cd_w_out[j], mod, l)
            keep[4].append(ckv[:N_CTX_TOK].reshape(BATCH, SEQ, MLA_KV_RANK))
            keep[5].append(kpe[:N_CTX_TOK, :MLA_ROPE].reshape(BATCH, SEQ, MLA_ROPE))
            keep[6].append(_hg_state_from_blocks(s_fin))
        x = _moe(x, mod, g_ffn[l], router_w, router_b, moe_w_gate, moe_w_up, moe_w_down, l)
    y_ctx = _final_norm(x, g_final, 0, N_CTX_TOK)
    y_lat = _final_norm(x, g_final, N_CTX_TOK, N_LAT_TOK)
    return (y_ctx.reshape(BATCH, SEQ, D_MODEL), y_lat.reshape(DEC_BATCH, DEC_SEQ, D_MODEL),
            jnp.stack(keep[0], 1), jnp.stack(keep[1], 1), jnp.stack(keep[2], 1), jnp.stack(keep[3], 1),
            jnp.stack(keep[4], 1), jnp.stack(keep[5], 1), jnp.stack(keep[6], 1))
```

```python
import functools
import math

import numpy as np
import jax
import jax.numpy as jnp
from jax import lax
from jax.experimental import pallas as pl
from jax.experimental.pallas import tpu as pltpu

F32 = jnp.float32
BF16 = jnp.bfloat16

D_MODEL = 1024
BATCH = 16
SEQ = 256
DEPTH = 2
DEC_BATCH = 4
DEC_SEQ = 1024
PAST_LEN = 512
GRID_W = 64
N_MOD = 6
EPS = 1e-6
NEG_INF = -1e30
ROPE_BASE = 10000.0
HEAD_DIM = 64
A_HEADS = 8
A_KV_HEADS = 2
A_WINDOW = 128
A_WIDTH = A_HEADS * HEAD_DIM
A_KV_WIDTH = A_KV_HEADS * HEAD_DIM
S5_WIDTH = D_MODEL // 2
S5_GROUP = 16
S5_GROUPS = S5_WIDTH // S5_GROUP
S5_STATE = 64
MLA_HEADS = 8
MLA_Q_RANK = D_MODEL // 4
MLA_KV_RANK = D_MODEL // 8
MLA_NOPE = 64
MLA_ROPE = 32
MLA_V = 64
MLA_WIDTH = MLA_HEADS * MLA_V
HG_HEADS = 8
HG_KEY = 64
HG_VAL = 64
HG_KD = HG_HEADS * HG_KEY
HG_WIDTH = HG_HEADS * HG_VAL
HG_CHUNK = 16
N_EXPERTS = 16
N_GROUPS = 4
EXPERTS_PER_GROUP = N_EXPERTS // N_GROUPS
D_FF = D_MODEL // 2

N_CTX_TOK = BATCH * SEQ
N_LAT_TOK = DEC_BATCH * DEC_SEQ
N_TOK = N_CTX_TOK + N_LAT_TOK
ROW_BLK = 256
N_ROW_BLK = N_TOK // ROW_BLK
N_CTX_BLK = N_CTX_TOK // ROW_BLK
LAT_BLK_PER_SEQ = DEC_SEQ // ROW_BLK
MOD_ROWS = 8
VMEM_LIMIT = 56 * 1024 * 1024


def _cparams(*sem):
    return pltpu.CompilerParams(dimension_semantics=sem, vmem_limit_bytes=VMEM_LIMIT)


def _mod_group(i):
    return jnp.where(i < N_CTX_BLK, 0, 1 + (i - N_CTX_BLK) // LAT_BLK_PER_SEQ)


def _mod_spec(layer, which):
    return pl.BlockSpec((1, 1, D_MODEL), lambda i: ((layer * MOD_ROWS + _mod_group(i)) * N_MOD + which, 0, 0))


def _rope_blk(i):
    return jnp.where(i < N_CTX_BLK, 0, 1 + (i - N_CTX_BLK) % LAT_BLK_PER_SEQ)


def _rope_tables(rot_dim):
    n_freq = rot_dim // 4
    t = np.arange(DEC_SEQ)
    rows = (t // GRID_W).astype(np.float32)
    cols = (t % GRID_W).astype(np.float32)
    inv = (np.float32(ROPE_BASE) ** (-np.arange(n_freq, dtype=np.float32) / np.float32(n_freq))).astype(np.float32)
    ang_r = rows[:, None] * inv[None, :]
    ang_c = cols[:, None] * inv[None, :]
    ang = np.concatenate([ang_r, ang_r, ang_c, ang_c], axis=-1).astype(np.float32)
    reps = 128 // rot_dim
    cos = np.tile(np.cos(ang), (1, reps)).astype(np.float32)
    sin = np.tile(np.sin(ang), (1, reps)).astype(np.float32)
    lane = np.arange(128)
    first = (lane % (2 * n_freq)) < n_freq
    sin_a = np.where(first[None, :], -sin, 0.0).astype(np.float32)
    sin_b = np.where(first[None, :], 0.0, sin).astype(np.float32)
    ident = np.zeros((ROW_BLK, 128), np.float32)
    cos = np.concatenate([ident + 1.0, cos], axis=0)
    sin_a = np.concatenate([ident, sin_a], axis=0)
    sin_b = np.concatenate([ident, sin_b], axis=0)
    return jnp.asarray(cos), jnp.asarray(sin_a), jnp.asarray(sin_b)


def _apply_rope(x, cos, sin_a, sin_b, quarter):
    outs = []
    for j in range(x.shape[1] // 128):
        xt = x[:, 128 * j:128 * (j + 1)]
        up = pltpu.roll(xt, 128 - quarter, axis=1)
        dn = pltpu.roll(xt, quarter, axis=1)
        outs.append(xt * cos + up * sin_a + dn * sin_b)
    return outs[0] if len(outs) == 1 else jnp.concatenate(outs, axis=1)


def _rms(x):
    return x * lax.rsqrt(jnp.mean(x * x, axis=-1, keepdims=True) + EPS)


def _norm_mod(x, g, sc, sh):
    return _rms(x) * g * (1.0 + sc) + sh


def _dot(a, b):
    return jnp.dot(a.astype(BF16), b.astype(BF16), preferred_element_type=F32)


def _dot_nt(a, b):
    return lax.dot_general(a.astype(BF16), b.astype(BF16), (((1,), (1,)), ((), ())), preferred_element_type=F32)


def _mod_kernel(cond_ref, w_ref, b_ref, o_ref):
    c = cond_ref[...]
    s = c * jax.nn.sigmoid(c)
    o_ref[0] = _dot(s, w_ref[0]) + b_ref[0]


def _modulation(cond, w_mod, b_mod):
    nb = 1024
    out = pl.pallas_call(
        _mod_kernel,
        grid=(DEPTH, N_MOD * D_MODEL // nb),
        in_specs=[pl.BlockSpec((MOD_ROWS, D_MODEL), lambda l, n: (0, 0)),
                  pl.BlockSpec((1, D_MODEL, nb), lambda l, n: (l, 0, n)),
                  pl.BlockSpec((1, 1, nb), lambda l, n: (l, 0, n))],
        out_specs=pl.BlockSpec((1, MOD_ROWS, nb), lambda l, n: (l, 0, n)),
        out_shape=jax.ShapeDtypeStruct((DEPTH, MOD_ROWS, N_MOD * D_MODEL), F32),
        compiler_params=_cparams("arbitrary", "arbitrary"),
        name="modulation",
    )(cond, w_mod, b_mod.reshape(DEPTH, 1, N_MOD * D_MODEL))
    return out.reshape(DEPTH * MOD_ROWS * N_MOD, 1, D_MODEL)


def _ab_in_kernel(x_ref, g_ref, sc_ref, sh_ref, w_ref, cos_ref, sa_ref, sb_ref,
                  q_ref, k_ref, v_ref, u_ref, wq_s, wk_s, wv_s, wu_s):
    @pl.when(pl.program_id(0) == 0)
    def _():
        wq_s[...] = w_ref[:, 0:A_WIDTH].astype(BF16)
        wk_s[...] = w_ref[:, A_WIDTH:A_WIDTH + A_KV_WIDTH].astype(BF16)
        wv_s[...] = w_ref[:, A_WIDTH + A_KV_WIDTH:A_WIDTH + 2 * A_KV_WIDTH].astype(BF16)
        wu_s[...] = w_ref[:, A_WIDTH + 2 * A_KV_WIDTH:].astype(BF16)

    h = _norm_mod(x_ref[...], g_ref[...], sc_ref[0], sh_ref[0]).astype(BF16)
    cos, sa, sb = cos_ref[...], sa_ref[...], sb_ref[...]
    q = jnp.dot(h, wq_s[...], preferred_element_type=F32)
    q_ref[...] = _apply_rope(q, cos, sa, sb, HEAD_DIM // 4)
    k = jnp.dot(h, wk_s[...], preferred_element_type=F32)
    k_ref[...] = _apply_rope(k, cos, sa, sb, HEAD_DIM // 4)
    v_ref[...] = jnp.dot(h, wv_s[...], preferred_element_type=F32)
    u_ref[...] = jnp.dot(h, wu_s[...], preferred_element_type=F32)


def _ab_in(x, mod, g, w, layer):
    cos, sa, sb = _rope_tables(HEAD_DIM)
    d_in = w.shape[1]
    row = lambda n: pl.BlockSpec((ROW_BLK, n), lambda i: (i, 0))
    rope = pl.BlockSpec((ROW_BLK, 128), lambda i: (_rope_blk(i), 0))
    return pl.pallas_call(
        _ab_in_kernel,
        grid=(N_ROW_BLK,),
        in_specs=[row(D_MODEL), pl.BlockSpec((1, D_MODEL), lambda i: (0, 0)),
                  _mod_spec(layer, 1), _mod_spec(layer, 0),
                  pl.BlockSpec((D_MODEL, d_in), lambda i: (0, 0)), rope, rope, rope],
        out_specs=[row(A_WIDTH), row(A_KV_WIDTH), row(A_KV_WIDTH), row(S5_WIDTH)],
        out_shape=[jax.ShapeDtypeStruct((N_TOK, A_WIDTH), F32), jax.ShapeDtypeStruct((N_TOK, A_KV_WIDTH), F32),
                   jax.ShapeDtypeStruct((N_TOK, A_KV_WIDTH), F32), jax.ShapeDtypeStruct((N_TOK, S5_WIDTH), F32)],
        scratch_shapes=[pltpu.VMEM((D_MODEL, A_WIDTH), BF16), pltpu.VMEM((D_MODEL, A_KV_WIDTH), BF16),
                        pltpu.VMEM((D_MODEL, A_KV_WIDTH), BF16), pltpu.VMEM((D_MODEL, S5_WIDTH), BF16)],
        compiler_params=_cparams("arbitrary"),
        name="ab_in",
    )(x, g.reshape(1, D_MODEL), mod, mod, w, cos, sa, sb)


def _softmax_pv(s_list, v_list, sink):
    m = functools.reduce(jnp.maximum, [jnp.max(s, axis=-1, keepdims=True) for s in s_list])
    if sink is not None:
        m = jnp.maximum(m, sink)
    ps = [jnp.exp(s - m) for s in s_list]
    l = functools.reduce(jnp.add, [jnp.sum(p, axis=-1, keepdims=True) for p in ps])
    if sink is not None:
        l = l + jnp.exp(sink - m)
    o = functools.reduce(jnp.add, [_dot(p, v) for p, v in zip(ps, v_list)])
    return o / l


def _attn_ctx_kernel(sink_ref, q_ref, k_ref, v_ref, o_ref):
    scale = HEAD_DIM ** -0.5
    g = A_HEADS // A_KV_HEADS
    outs = []
    for h in range(A_HEADS):
        kh = h // g
        q = q_ref[:, HEAD_DIM * h:HEAD_DIM * (h + 1)]
        k = k_ref[:, HEAD_DIM * kh:HEAD_DIM * (kh + 1)]
        v = v_ref[:, HEAD_DIM * kh:HEAD_DIM * (kh + 1)]
        s = _dot_nt(q, k) * scale
        outs.append(_softmax_pv([s], [v], sink_ref[h]))
    o_ref[...] = jnp.concatenate(outs, axis=1)


def _attn_lat_kernel(sink_ref, q_ref, kp_ref, kc_ref, kn_ref, vp_ref, vc_ref, vn_ref, kx_ref, vx_ref, o_ref):
    n = pl.program_id(1)
    nb = DEC_SEQ // A_WINDOW
    scale = HEAD_DIM ** -0.5
    g = A_HEADS // A_KV_HEADS
    i = lax.broadcasted_iota(jnp.int32, (A_WINDOW, A_WINDOW), 0)
    j = lax.broadcasted_iota(jnp.int32, (A_WINDOW, A_WINDOW), 1)
    ok_prev = (j >= i) & (n > 0)
    ok_next = (j <= i) & (n < nb - 1)
    outs = []
    for h in range(A_HEADS):
        kh = h // g
        sl = slice(HEAD_DIM * kh, HEAD_DIM * (kh + 1))
        q = q_ref[:, HEAD_DIM * h:HEAD_DIM * (h + 1)]
        s_p = jnp.where(ok_prev, _dot_nt(q, kp_ref[:, sl]) * scale, NEG_INF)
        s_c = _dot_nt(q, kc_ref[:, sl]) * scale
        s_n = jnp.where(ok_next, _dot_nt(q, kn_ref[:, sl]) * scale, NEG_INF)
        s_x = _dot_nt(q, kx_ref[0, :, sl]) * scale
        outs.append(_softmax_pv([s_p, s_c, s_n, s_x],
                                [vp_ref[:, sl], vc_ref[:, sl], vn_ref[:, sl], vx_ref[0, :, sl]], sink_ref[h]))
    o_ref[...] = jnp.concatenate(outs, axis=1)


def _attention_a(q, k, v, cache_k, cache_v, sink):
    smem = pl.BlockSpec(memory_space=pltpu.SMEM)
    o = pl.pallas_call(
        _attn_ctx_kernel,
        grid=(BATCH,),
        in_specs=[smem, pl.BlockSpec((SEQ, A_WIDTH), lambda b: (b, 0)),
                  pl.BlockSpec((SEQ, A_KV_WIDTH), lambda b: (b, 0)), pl.BlockSpec((SEQ, A_KV_WIDTH), lambda b: (b, 0))],
        out_specs=pl.BlockSpec((SEQ, A_WIDTH), lambda b: (b, 0)),
        out_shape=jax.ShapeDtypeStruct((N_CTX_TOK, A_WIDTH), F32),
        compiler_params=_cparams("arbitrary"),
        name="attn_a_ctx",
    )(sink, q, k, v)
    nb = DEC_SEQ // A_WINDOW
    base = N_CTX_TOK // A_WINDOW
    cur = lambda b, n: (base + b * nb + n, 0)
    prev = lambda b, n: (base + b * nb + jnp.maximum(n - 1, 0), 0)
    nxt = lambda b, n: (base + b * nb + jnp.minimum(n + 1, nb - 1), 0)
    kv = lambda f: pl.BlockSpec((A_WINDOW, A_KV_WIDTH), f)
    cache = pl.BlockSpec((1, PAST_LEN, A_KV_WIDTH), lambda b, n: (b, 0, 0))
    o_lat = pl.pallas_call(
        _attn_lat_kernel,
        grid=(DEC_BATCH, nb),
        in_specs=[smem, pl.BlockSpec((A_WINDOW, A_WIDTH), cur), kv(prev), kv(cur), kv(nxt), kv(prev), kv(cur), kv(nxt),
                  cache, cache],
        out_specs=pl.BlockSpec((A_WINDOW, A_WIDTH), lambda b, n: (b * nb + n, 0)),
        out_shape=jax.ShapeDtypeStruct((N_LAT_TOK, A_WIDTH), F32),
        compiler_params=_cparams("arbitrary", "arbitrary"),
        name="attn_a_lat",
    )(sink, q, k, k, k, v, v, v, cache_k, cache_v)
    return o, o_lat


S5_CHUNK = 16
S5_OCT = 128 // S5_GROUP
S5_NOCT = S5_GROUPS // S5_OCT
S5_K = S5_CHUNK * 128
S5_PART = S5_OCT * S5_STATE
S5_SW = 4 * S5_PART
S5_ROWS_CTX = BATCH * SEQ // S5_CHUNK
S5_ROWS_LAT = DEC_BATCH * DEC_SEQ // S5_CHUNK
S5_ROWS = S5_ROWS_CTX + S5_ROWS_LAT
S5_NB = 4
HI = lax.Precision.HIGHEST


def _s5_disc_kernel(lr_ref, li_ref, ldt_ref, ar_ref, ai_ref, zr_ref, zi_ref):
    lr, li = lr_ref[...], li_ref[...]
    dt = jnp.exp(ldt_ref[...])
    mag = jnp.exp(lr * dt)
    ar, ai = mag * jnp.cos(li * dt), mag * jnp.sin(li * dt)
    den = lr * lr + li * li
    ar_ref[...] = ar
    ai_ref[...] = ai
    zr_ref[...] = ((ar - 1.0) * lr + ai * li) / den
    zi_ref[...] = (ai * lr - (ar - 1.0) * li) / den


def _cmul(xr, xi, yr, yi):
    return xr * yr - xi * yi, xr * yi + xi * yr


def _dot_nt_hi(a, b):
    return lax.dot_general(a, b, (((1,), (1,)), ((), ())), precision=HI, preferred_element_type=F32)


def _s5_prep_kernel(ar_ref, ai_ref, zr_ref, zi_ref, btr_ref, bti_ref, ctr_ref, cti_ref,
                    m_ref, win_ref, wout_ref, a16_ref, pw_s, kf_s, kb_s):
    t = pl.program_id(1)
    npw = S5_CHUNK + 1

    def bbar(d):
        return _cmul(zr_ref[0, d], zi_ref[0, d], btr_ref[0, d], bti_ref[0, d])

    @pl.when(t == 0)
    def _():
        for d in range(2):
            ar, ai = ar_ref[0, d], ai_ref[0, d]
            pr, pi = jnp.ones_like(ar), jnp.zeros_like(ar)
            bbr, bbi = bbar(d)
            for j in range(npw):
                pw_s[d, 0, j:j + 1, :] = pr
                pw_s[d, 1, j:j + 1, :] = pi
                if j < S5_CHUNK:
                    wr, wi = _cmul(pr, pi, bbr, bbi)
                    k = _dot_nt_hi(wr, ctr_ref[0, d]) - _dot_nt_hi(wi, cti_ref[0, d])
                    (kf_s if d == 0 else kb_s)[j] = k
                pr, pi = _cmul(pr, pi, ar, ai)
        a16_ref[0] = jnp.concatenate([pw_s[0, 0, S5_CHUNK:npw, :], pw_s[1, 0, S5_CHUNK:npw, :],
                                      pw_s[0, 1, S5_CHUNK:npw, :], pw_s[1, 1, S5_CHUNK:npw, :]], axis=1)

    for tp in range(S5_CHUNK):
        j = tp - t
        kf = kf_s[jnp.clip(j, 0, S5_CHUNK - 1)]
        kb = kb_s[jnp.clip(-j, 0, S5_CHUNK - 1)]
        blk = jnp.where(j > 0, kf, jnp.where(j < 0, kb, kf + kb))
        m_ref[0, :, 128 * tp:128 * (tp + 1)] = blk.astype(BF16)

    def power(d, e):
        return pw_s[d, 0, pl.ds(e, 1), :], pw_s[d, 1, pl.ds(e, 1), :]

    for d in range(2):
        bbr, bbi = bbar(d)
        xr, xi = _cmul(*power(d, (S5_CHUNK - 1 - t) if d == 0 else t), bbr, bbi)
        win_ref[0, :, S5_PART * d:S5_PART * (d + 1)] = xr.astype(BF16)
        win_ref[0, :, S5_PART * (2 + d):S5_PART * (3 + d)] = xi.astype(BF16)
        er, ei = _cmul(*power(d, (t + 1) if d == 0 else (S5_CHUNK - t)), ctr_ref[0, d], cti_ref[0, d])
        wout_ref[0, :, S5_PART * d:S5_PART * (d + 1)] = er.astype(BF16)
        wout_ref[0, :, S5_PART * (2 + d):S5_PART * (3 + d)] = (-ei).astype(BF16)


def _s5_main_kernel(u_ref, win_ref, m_ref, wout_ref, a16_ref, h0_ref, y_ref, hfin_ref, uo_s, x_s, hs_s):
    s = pl.program_id(1)
    nq = S5_PART // 128

    @pl.when(s == 0)
    def _():
        for t in range(S5_CHUNK):
            uo_s[:, 128 * t:128 * (t + 1)] = u_ref[pl.ds(t, S5_ROWS, stride=S5_CHUNK), :].astype(BF16)

    @pl.when(s < S5_NB)
    def _():
        x = jnp.dot(uo_s[...], win_ref[0], preferred_element_type=F32)
        for q in range(nq):
            x_s[s * nq + q] = x[:, 128 * q:128 * (q + 1)]

    @pl.when(s == S5_NB - 1)
    def _():
        def run(row0, nb, nc, h):
            for c in range(nc):
                for d in range(2):
                    cc = c if d == 0 else nc - 1 - c
                    rows = pl.ds(row0 + cc, nb, stride=nc)
                    for q in range(nq):
                        kr, ki = d * nq + q, (2 + d) * nq + q
                        hr, hi = h[d][0][q], h[d][1][q]
                        hs_s[kr, rows, :] = hr
                        hs_s[ki, rows, :] = hi
                        ar = a16_ref[0, :, 128 * kr:128 * (kr + 1)]
                        ai = a16_ref[0, :, 128 * ki:128 * (ki + 1)]
                        h[d][0][q] = ar * hr - ai * hi + x_s[kr, rows, :]
                        h[d][1][q] = ar * hi + ai * hr + x_s[ki, rows, :]
            return h

        zero = jnp.zeros((BATCH, 128), F32)
        fin = run(0, BATCH, SEQ // S5_CHUNK, [[[zero] * nq, [zero] * nq] for _ in range(2)])
        for d in range(2):
            for ri in range(2):
                for q in range(nq):
                    k = (2 * ri + d) * nq + q
                    hfin_ref[0, :, 128 * k:128 * (k + 1)] = fin[d][ri][q]
        h0 = [[[h0_ref[0, :, 128 * ((2 * ri + d) * nq + q):128 * ((2 * ri + d) * nq + q + 1)] for q in range(nq)]
               for ri in range(2)] for d in range(2)]
        run(S5_ROWS_CTX, DEC_BATCH, DEC_SEQ // S5_CHUNK, h0)

    @pl.when(s >= S5_NB)
    def _():
        hs = jnp.concatenate([hs_s[k] for k in range(4 * nq)], axis=1).astype(BF16)
        y = (jnp.dot(uo_s[...], m_ref[0], preferred_element_type=F32)
             + lax.dot_general(hs, wout_ref[0], (((1,), (1,)), ((), ())), preferred_element_type=F32))
        for q in range(nq):
            t = (s - S5_NB) * nq + q
            y_ref[pl.ds(t, S5_ROWS, stride=S5_CHUNK), :] = y[:, 128 * q:128 * (q + 1)]


def _s5_octets(t, lanes):
    return t.reshape(2, S5_NOCT, 1, S5_OCT * lanes).transpose(1, 0, 2, 3)


def _s5_blockdiag(t):
    a, n = t.shape[2], t.shape[3]
    t = t.reshape(2, S5_NOCT, S5_OCT, a, n)
    bd = jnp.einsum('dogan,gh->dogahn', t, jnp.eye(S5_OCT, dtype=t.dtype))
    return bd.reshape(2, S5_NOCT, S5_OCT * a, S5_OCT * n).transpose(1, 0, 2, 3)


def _s5_state_to_lanes(h_re, h_im):
    b = h_re.shape[0]
    parts = jnp.stack([h_re[:, 0], h_re[:, 1], h_im[:, 0], h_im[:, 1]], axis=1)
    parts = parts.reshape(b, 4, S5_NOCT, S5_PART).transpose(2, 0, 1, 3)
    return parts.reshape(S5_NOCT, b, S5_SW)


def _s5_state_from_lanes(h):
    b = h.shape[1]
    parts = h.reshape(S5_NOCT, b, 4, S5_OCT, S5_STATE).transpose(1, 2, 0, 3, 4).reshape(b, 4, S5_GROUPS, S5_STATE)
    return parts[:, 0:2], parts[:, 2:4]


def _s5_scan(u, h0_re, h0_im, lam_re, lam_im, log_dt, b_re, b_im, c_re, c_im):
    ng, n = S5_GROUPS, S5_STATE
    rows = 2 * ng
    disc = pl.pallas_call(
        _s5_disc_kernel,
        out_shape=[jax.ShapeDtypeStruct((rows, n), F32)] * 4,
        name="s5_disc",
    )(lam_re.reshape(rows, n), lam_im.reshape(rows, n), log_dt.reshape(rows, 1))
    ar, ai, zr, zi = [_s5_octets(t.reshape(2, ng, n), n) for t in disc]
    bt = lambda t: _s5_blockdiag(t.transpose(0, 1, 3, 2))
    vec = pl.BlockSpec((1, 2, 1, S5_PART), lambda o, t: (o, 0, 0, 0))
    mat = pl.BlockSpec((1, 2, 128, S5_PART), lambda o, t: (o, 0, 0, 0))
    rowblk = lambda w: pl.BlockSpec((1, 128, w), lambda o, t: (o, t, 0))
    m, win, wout, a16 = pl.pallas_call(
        _s5_prep_kernel,
        grid=(S5_NOCT, S5_CHUNK),
        in_specs=[vec, vec, vec, vec, mat, mat, mat, mat],
        out_specs=[rowblk(S5_K), rowblk(S5_SW), rowblk(S5_SW), pl.BlockSpec((1, 1, S5_SW), lambda o, t: (o, 0, 0))],
        out_shape=[jax.ShapeDtypeStruct((S5_NOCT, S5_K, S5_K), BF16), jax.ShapeDtypeStruct((S5_NOCT, S5_K, S5_SW), BF16),
                   jax.ShapeDtypeStruct((S5_NOCT, S5_K, S5_SW), BF16), jax.ShapeDtypeStruct((S5_NOCT, 1, S5_SW), F32)],
        scratch_shapes=[pltpu.VMEM((2, 2, 24, S5_PART), F32), pltpu.VMEM((S5_CHUNK, 128, 128), F32),
                        pltpu.VMEM((S5_CHUNK, 128, 128), F32)],
        compiler_params=_cparams("arbitrary", "arbitrary"),
        name="s5_prep",
    )(ar, ai, zr, zi, bt(b_re), bt(b_im), _s5_blockdiag(c_re), _s5_blockdiag(c_im))
    nb = S5_NB
    y, hfin = pl.pallas_call(
        _s5_main_kernel,
        grid=(S5_NOCT, 2 * nb),
        in_specs=[pl.BlockSpec((N_TOK, 128), lambda o, s: (0, o)),
                  pl.BlockSpec((1, S5_K, S5_PART), lambda o, s: (o, 0, jnp.minimum(s, nb - 1))),
                  pl.BlockSpec((1, S5_K, S5_PART), lambda o, s: (o, 0, jnp.maximum(s - nb, 0))),
                  pl.BlockSpec((1, S5_PART, S5_SW), lambda o, s: (o, jnp.maximum(s - nb, 0), 0)),
                  pl.BlockSpec((1, 1, S5_SW), lambda o, s: (o, 0, 0)),
                  pl.BlockSpec((1, DEC_BATCH, S5_SW), lambda o, s: (o, 0, 0))],
        out_specs=[pl.BlockSpec((N_TOK, 128), lambda o, s: (0, o)),
                   pl.BlockSpec((1, BATCH, S5_SW), lambda o, s: (o, 0, 0))],
        out_shape=[jax.ShapeDtypeStruct((N_TOK, S5_WIDTH), F32), jax.ShapeDtypeStruct((S5_NOCT, BATCH, S5_SW), F32)],
        scratch_shapes=[pltpu.VMEM((S5_ROWS, S5_K), BF16), pltpu.VMEM((S5_SW // 128, S5_ROWS, 128), F32),
                        pltpu.VMEM((S5_SW // 128, S5_ROWS, 128), F32)],
        compiler_params=_cparams("arbitrary", "arbitrary"),
        name="s5_main",
    )(u, win, m, wout, a16, _s5_state_to_lanes(h0_re, h0_im))
    fin_re, fin_im = _s5_state_from_lanes(hfin)
    return y, fin_re, fin_im


def _ctx_or_lat(ctx_ref, lat_ref):
    return jnp.where(pl.program_id(0) < N_CTX_BLK, ctx_ref[...], lat_ref[...])


def _split_specs(width):
    return [pl.BlockSpec((ROW_BLK, width), lambda i: (jnp.minimum(i, N_CTX_BLK - 1), 0)),
            pl.BlockSpec((ROW_BLK, width), lambda i: (jnp.maximum(i - N_CTX_BLK, 0), 0))]


def _ab_out_kernel(x_ref, oac_ref, oal_ref, y_ref, u_ref, d_ref, wglu_ref, wout_ref, gt_ref, o_ref, wglu_s, wa_s, wb_s):
    @pl.when(pl.program_id(0) == 0)
    def _():
        wglu_s[...] = wglu_ref[...].astype(BF16)
        wa_s[...] = wout_ref[0:A_WIDTH, :].astype(BF16)
        wb_s[...] = wout_ref[A_WIDTH:, :].astype(BF16)

    g = jax.nn.gelu(y_ref[...] + d_ref[...] * u_ref[...])
    ob = g * jax.nn.sigmoid(jnp.dot(g.astype(BF16), wglu_s[...], preferred_element_type=F32))
    out = (jnp.dot(_ctx_or_lat(oac_ref, oal_ref).astype(BF16), wa_s[...], preferred_element_type=F32)
           + jnp.dot(ob.astype(BF16), wb_s[...], preferred_element_type=F32))
    o_ref[...] = x_ref[...] + gt_ref[0] * out


def _ab_out(x, oa, y, u, d_skip, w_glu, w_out, mod, layer):
    row = lambda n: pl.BlockSpec((ROW_BLK, n), lambda i: (i, 0))
    full = lambda a, b: pl.BlockSpec((a, b), lambda i: (0, 0))
    return pl.pallas_call(
        _ab_out_kernel,
        grid=(N_ROW_BLK,),
        in_specs=[row(D_MODEL)] + _split_specs(A_WIDTH) + [row(S5_WIDTH), row(S5_WIDTH), full(1, S5_WIDTH),
                  full(S5_WIDTH, S5_WIDTH), full(A_WIDTH + S5_WIDTH, D_MODEL), _mod_spec(layer, 2)],
        out_specs=row(D_MODEL),
        out_shape=jax.ShapeDtypeStruct((N_TOK, D_MODEL), F32),
        scratch_shapes=[pltpu.VMEM((S5_WIDTH, S5_WIDTH), BF16), pltpu.VMEM((A_WIDTH, D_MODEL), BF16),
                        pltpu.VMEM((S5_WIDTH, D_MODEL), BF16)],
        compiler_params=_cparams("arbitrary"),
        name="ab_out",
    )(x, oa[0], oa[1], y, u, d_skip.reshape(1, S5_WIDTH), w_glu, w_out, mod)


CD_HG0 = MLA_Q_RANK + MLA_KV_RANK + MLA_ROPE
KPE_LANES = 128


def _cd_in_kernel(x_ref, g_ref, sc_ref, sh_ref, w_ref, gq_ref, wqu_ref, gkv_ref, wkvu_ref, cos_ref, sa_ref, sb_ref,
                  qn_ref, qr_ref, ckv_ref, kpe_ref, kn_ref, vm_ref, hq_ref, hff_ref, hfb_ref, hi_ref, hg_ref,
                  wcq_s, wckv_s, wkpe_s, whg_s, wqu_s, wkvu_s):
    @pl.when(pl.program_id(0) == 0)
    def _():
        wcq_s[...] = w_ref[:, 0:MLA_Q_RANK].astype(BF16)
        wckv_s[...] = w_ref[:, MLA_Q_RANK:MLA_Q_RANK + MLA_KV_RANK].astype(BF16)
        kp = w_ref[:, MLA_Q_RANK + MLA_KV_RANK:CD_HG0].astype(BF16)
        wkpe_s[...] = jnp.concatenate([kp] * (KPE_LANES // MLA_ROPE), axis=1)
        whg_s[...] = w_ref[:, CD_HG0:].astype(BF16)
        wqu_s[...] = wqu_ref[...].astype(BF16)
        wkvu_s[...] = wkvu_ref[...].astype(BF16)

    h = _norm_mod(x_ref[...], g_ref[...], sc_ref[0], sh_ref[0]).astype(BF16)
    cos, sa, sb = cos_ref[...], sa_ref[...], sb_ref[...]
    nope = MLA_HEADS * MLA_NOPE
    cq = _rms(jnp.dot(h, wcq_s[...], preferred_element_type=F32)) * gq_ref[...]
    qq = jnp.dot(cq.astype(BF16), wqu_s[...], preferred_element_type=F32)
    qn_ref[...] = qq[:, 0:nope]
    qr_ref[...] = _apply_rope(qq[:, nope:], cos, sa, sb, MLA_ROPE // 4)
    ckv = _rms(jnp.dot(h, wckv_s[...], preferred_element_type=F32)) * gkv_ref[...]
    ckv_ref[...] = ckv
    kv = jnp.dot(ckv.astype(BF16), wkvu_s[...], preferred_element_type=F32)
    kn_ref[...] = kv[:, 0:nope]
    vm_ref[...] = kv[:, nope:]
    kpe_ref[...] = _apply_rope(jnp.dot(h, wkpe_s[...], preferred_element_type=F32), cos, sa, sb, MLA_ROPE // 4)
    hh = jnp.dot(h, whg_s[...], preferred_element_type=F32)
    for n, ref in enumerate((hq_ref, hff_ref, hfb_ref, hi_ref, hg_ref)):
        ref[...] = hh[:, HG_KD * n:HG_KD * (n + 1)]


def _mla_split_heads(w, a):
    k, n = w.shape
    w = w.reshape(k, MLA_HEADS, n // MLA_HEADS)
    return jnp.concatenate([w[:, :, :a].reshape(k, -1), w[:, :, a:].reshape(k, -1)], axis=1)


def _cd_in(x, mod, g, w, g_q, w_q_up, g_kv, w_kv_up, layer):
    cos, sa, sb = _rope_tables(MLA_ROPE)
    d_in = w.shape[1]
    row = lambda n: pl.BlockSpec((ROW_BLK, n), lambda i: (i, 0))
    full = lambda a, b: pl.BlockSpec((a, b), lambda i: (0, 0))
    rope = pl.BlockSpec((ROW_BLK, 128), lambda i: (_rope_blk(i), 0))
    widths = [MLA_HEADS * MLA_NOPE, MLA_HEADS * MLA_ROPE, MLA_KV_RANK, KPE_LANES, MLA_HEADS * MLA_NOPE, MLA_WIDTH] + [HG_KD] * 5
    nq = MLA_HEADS * (MLA_NOPE + MLA_ROPE)
    nkv = MLA_HEADS * (MLA_NOPE + MLA_V)
    return pl.pallas_call(
        _cd_in_kernel,
        grid=(N_ROW_BLK,),
        in_specs=[row(D_MODEL), full(1, D_MODEL), _mod_spec(layer, 1), _mod_spec(layer, 0), full(D_MODEL, d_in),
                  full(1, MLA_Q_RANK), full(MLA_Q_RANK, nq), full(1, MLA_KV_RANK), full(MLA_KV_RANK, nkv), rope, rope, rope],
        out_specs=[row(n) for n in widths],
        out_shape=[jax.ShapeDtypeStruct((N_TOK, n), F32) for n in widths],
        scratch_shapes=[pltpu.VMEM((D_MODEL, MLA_Q_RANK), BF16), pltpu.VMEM((D_MODEL, MLA_KV_RANK), BF16),
                        pltpu.VMEM((D_MODEL, KPE_LANES), BF16), pltpu.VMEM((D_MODEL, 5 * HG_KD), BF16),
                        pltpu.VMEM((MLA_Q_RANK, nq), BF16), pltpu.VMEM((MLA_KV_RANK, nkv), BF16)],
        compiler_params=_cparams("arbitrary"),
        name="cd_in",
    )(x, g.reshape(1, D_MODEL), mod, mod, w, g_q.reshape(1, -1), _mla_split_heads(w_q_up, MLA_NOPE),
      g_kv.reshape(1, -1), _mla_split_heads(w_kv_up, MLA_NOPE), cos, sa, sb)


def _mm_kernel(a_ref, w_ref, *o_refs):
    r = _dot(a_ref[...], w_ref[...])
    off = 0
    for o in o_refs:
        o[...] = r[:, off:off + o.shape[1]]
        off += o.shape[1]


def _mla_cache_kv(cckv, w_kv_up):
    n = MLA_HEADS * MLA_NOPE
    rows = cckv.shape[0]
    return pl.pallas_call(
        _mm_kernel,
        grid=(rows // PAST_LEN,),
        in_specs=[pl.BlockSpec((PAST_LEN, MLA_KV_RANK), lambda i: (i, 0)),
                  pl.BlockSpec((MLA_KV_RANK, 2 * n), lambda i: (0, 0))],
        out_specs=[pl.BlockSpec((PAST_LEN, n), lambda i: (i, 0))] * 2,
        out_shape=[jax.ShapeDtypeStruct((rows, n), F32)] * 2,
        compiler_params=_cparams("arbitrary"),
        name="mla_cache_kv",
    )(cckv, _mla_split_heads(w_kv_up, MLA_NOPE))


def _mla_heads(qn, qr, keys, o_ref):
    scale = (MLA_NOPE + MLA_ROPE) ** -0.5
    outs = []
    for h in range(MLA_HEADS):
        a = slice(MLA_NOPE * h, MLA_NOPE * (h + 1))
        r = slice(MLA_ROPE * h, MLA_ROPE * (h + 1))
        s_list = [(_dot_nt(qn[:, a], kn[:, a]) + _dot_nt(qr[:, r], kp[:, 0:MLA_ROPE])) * scale for kn, kp, _ in keys]
        outs.append(_softmax_pv(s_list, [v[:, a] for _, _, v in keys], None))
    o_ref[...] = jnp.concatenate(outs, axis=1)


def _mla_ctx_kernel(qn_ref, qr_ref, kn_ref, kp_ref, v_ref, o_ref):
    _mla_heads(qn_ref[...], qr_ref[...], [(kn_ref[...], kp_ref[...], v_ref[...])], o_ref)


def _mla_lat_kernel(qn_ref, qr_ref, kn_ref, kp_ref, v_ref, kcn_ref, kcp_ref, vc_ref, o_ref):
    _mla_heads(qn_ref[...], qr_ref[...],
               [(kcn_ref[...], kcp_ref[0], vc_ref[...]), (kn_ref[...], kp_ref[...], v_ref[...])], o_ref)


def _attention_mla(qn, qr, kn, kpe, vm, kcn, kcpe, vc):
    n = MLA_HEADS * MLA_NOPE
    nr = MLA_HEADS * MLA_ROPE
    blk = lambda w: pl.BlockSpec((SEQ, w), lambda b: (b, 0))
    o = pl.pallas_call(
        _mla_ctx_kernel,
        grid=(BATCH,),
        in_specs=[blk(n), blk(nr), blk(n), blk(KPE_LANES), blk(MLA_WIDTH)],
        out_specs=blk(MLA_WIDTH),
        out_shape=jax.ShapeDtypeStruct((N_CTX_TOK, MLA_WIDTH), F32),
        compiler_params=_cparams("arbitrary"),
        name="mla_ctx",
    )(qn, qr, kn, kpe, vm)
    nq = DEC_SEQ // ROW_BLK
    qblk = lambda w: pl.BlockSpec((ROW_BLK, w), lambda b, i: (N_CTX_BLK + b * nq + i, 0))
    seq = lambda w: pl.BlockSpec((DEC_SEQ, w), lambda b, i: (N_CTX_TOK // DEC_SEQ + b, 0))
    past = lambda w: pl.BlockSpec((PAST_LEN, w), lambda b, i: (b, 0))
    o_lat = pl.pallas_call(
        _mla_lat_kernel,
        grid=(DEC_BATCH, nq),
        in_specs=[qblk(n), qblk(nr), seq(n), seq(KPE_LANES), seq(MLA_WIDTH), past(n),
                  pl.BlockSpec((1, PAST_LEN, MLA_ROPE), lambda b, i: (b, 0, 0)), past(MLA_WIDTH)],
        out_specs=pl.BlockSpec((ROW_BLK, MLA_WIDTH), lambda b, i: (b * nq + i, 0)),
        out_shape=jax.ShapeDtypeStruct((N_LAT_TOK, MLA_WIDTH), F32),
        compiler_params=_cparams("arbitrary", "arbitrary"),
        name="mla_lat",
    )(qn, qr, kn, kpe, vm, kcn, kcpe, vc)
    return o, o_lat


HG_TILE = 128
HG_NC = HG_TILE // HG_CHUNK
HG_HALF = 256


HG_SLABS = HG_KD // 128


def _hg_put(ref, x):
    for j in range(HG_SLABS):
        ref[j] = x[:, 128 * j:128 * (j + 1)]


def _hg_get(ref, r, l):
    return jnp.concatenate([ref[2 * l, r, :], ref[2 * l + 1, r, :]], axis=1)


def _hg_plane(ref, p):
    return jnp.concatenate([ref[j, pl.ds(p, HG_NC, stride=HG_CHUNK), :] for j in range(HG_SLABS)], axis=1)


def _hg_put_plane(ref, p, x):
    for j in range(HG_SLABS):
        ref[j, pl.ds(p, HG_NC, stride=HG_CHUNK), :] = x[:, 128 * j:128 * (j + 1)]


def _hg_direction(hq_ref, hf_ref, hi_ref, o_ref, lb, st_ref, q_s, f_s, kk_s, v_s, qt_s, kt_s, o_s, sign,
                  ones_bd, diag_mask):
    hq = hq_ref[...]
    _hg_put(q_s, hq * jax.nn.sigmoid(hq))
    f = lb + (1.0 - lb) * jax.nn.sigmoid(hf_ref[...])
    _hg_put(f_s, f)
    _hg_put(kk_s, 1.0 - f)
    _hg_put(v_s, hi_ref[...])
    pos = (lambda i: i) if sign > 0 else (lambda i: HG_CHUNK - 1 - i)
    q = [_hg_plane(q_s, pos(i)) for i in range(HG_CHUNK)]
    fp = [_hg_plane(f_s, pos(i)) for i in range(HG_CHUNK)]
    kk = [_hg_plane(kk_s, pos(i)) for i in range(HG_CHUNK)]
    v = [_hg_plane(v_s, pos(i)) for i in range(HG_CHUNK)]

    def expand(planes):
        pb = jnp.concatenate(planes, axis=0).astype(BF16)
        e = jnp.concatenate([jnp.dot(pb[:, 0:HG_HALF], ones_bd, preferred_element_type=F32),
                             jnp.dot(pb[:, HG_HALF:], ones_bd, preferred_element_type=F32)], axis=1)
        return [e[HG_NC * n:HG_NC * (n + 1)] for n in range(len(planes))]

    o = [e * v[i] for i, e in enumerate(expand([q[i] * kk[i] for i in range(HG_CHUNK)]))]
    dec = list(fp)
    for d in range(1, HG_CHUNK):
        if d > 1:
            for i in range(d, HG_CHUNK):
                dec[i] = dec[i] * fp[i - d + 1]
        es = expand([q[i] * kk[i - d] * dec[i] for i in range(d, HG_CHUNK)])
        for n, i in enumerate(range(d, HG_CHUNK)):
            o[i] = o[i] + es[n] * v[i - d]
    for i in range(HG_CHUNK):
        _hg_put_plane(o_s, pos(i), o[i])

    incl = [fp[0]]
    for i in range(1, HG_CHUNK):
        incl.append(incl[-1] * fp[i])
    excl = [None] * HG_CHUNK
    excl[HG_CHUNK - 1] = jnp.ones_like(fp[0])
    for i in range(HG_CHUNK - 2, -1, -1):
        excl[i] = excl[i + 1] * fp[i + 1]
    for i in range(HG_CHUNK):
        _hg_put_plane(qt_s, pos(i), q[i] * incl[i])
        _hg_put_plane(kt_s, pos(i), kk[i] * excl[i])
    whole = incl[HG_CHUNK - 1]
    for cc in range(HG_NC):
        c = cc if sign > 0 else HG_NC - 1 - cc
        r = slice(HG_CHUNK * c, HG_CHUNK * (c + 1))
        for g in range(2):
            l = slice(HG_HALF * g, HG_HALF * (g + 1))
            st = st_ref[g]
            o_ref[r, l] = _hg_get(o_s, r, g) + _dot_nt(_hg_get(qt_s, r, g), st)
            ds = lax.dot_general(hi_ref[r, l].astype(BF16), _hg_get(kt_s, r, g).astype(BF16),
                                 (((0,), (0,)), ((), ())), preferred_element_type=F32)
            st_ref[g] = st * whole[c:c + 1, l] + jnp.where(diag_mask, ds, 0.0)


def _hg_lower_bound(lb_ref, d, layer):
    raw = [lb_ref[d * DEPTH + m:d * DEPTH + m + 1, :] for m in range(DEPTH)]
    mx = functools.reduce(jnp.maximum, raw)
    e = [jnp.exp(r - mx) for r in raw]
    tot = functools.reduce(jnp.add, e)
    return functools.reduce(jnp.add, e[1:layer + 1], jnp.zeros_like(tot)) / tot


def _hgrn_kernel(nt, layer, hqf_ref, hff_ref, hif_ref, hqb_ref, hfb_ref, hib_ref, lb_ref, s0_ref,
                 of_ref, ob_ref, sfin_ref, st_s, q_s, f_s, kk_s, v_s, qt_s, kt_s, o_s):
    i = pl.program_id(1)

    @pl.when(i == 0)
    def _():
        st_s[...] = s0_ref[0]

    ri = lax.broadcasted_iota(jnp.int32, (HG_HALF, HG_HALF), 0) // HG_KEY
    ci = lax.broadcasted_iota(jnp.int32, (HG_HALF, HG_HALF), 1) // HG_KEY
    diag_mask = ri == ci
    ones_bd = jnp.where(diag_mask, 1.0, 0.0).astype(BF16)
    _hg_direction(hqf_ref, hff_ref, hif_ref, of_ref, _hg_lower_bound(lb_ref, 0, layer), st_s.at[0],
                  q_s, f_s, kk_s, v_s, qt_s, kt_s, o_s, 1, ones_bd, diag_mask)
    _hg_direction(hqb_ref, hfb_ref, hib_ref, ob_ref, _hg_lower_bound(lb_ref, 1, layer), st_s.at[1],
                  q_s, f_s, kk_s, v_s, qt_s, kt_s, o_s, -1, ones_bd, diag_mask)

    @pl.when(i == nt - 1)
    def _():
        sfin_ref[0] = st_s[...]


def _hg_state_to_blocks(s):
    b = s.shape[0]
    st = s.transpose(0, 1, 2, 4, 3).reshape(b, 2, 2, 4, HG_VAL, HG_KEY)
    eye = jnp.eye(4, dtype=s.dtype)
    return jnp.einsum('bdgivk,ij->bdgivjk', st, eye).reshape(b, 2, 2, HG_HALF, HG_HALF)


def _hg_state_from_blocks(st):
    b = st.shape[0]
    st = st.reshape(b, 2, 2, 4, HG_VAL, 4, HG_KEY)
    diag = jnp.stack([st[:, :, :, i, :, i, :] for i in range(4)], axis=3)
    return diag.reshape(b, 2, HG_HEADS, HG_VAL, HG_KEY).transpose(0, 1, 2, 4, 3)


def _hgrn_scan(hq, hff, hfb, hi, lb, s0, layer, row0, nseq, seqlen):
    nt = seqlen // HG_TILE
    base = row0 // HG_TILE
    fwd = pl.BlockSpec((HG_TILE, HG_KD), lambda b, i: (base + b * nt + i, 0))
    bwd = pl.BlockSpec((HG_TILE, HG_KD), lambda b, i: (base + b * nt + nt - 1 - i, 0))
    ofw = pl.BlockSpec((HG_TILE, HG_KD), lambda b, i: (b * nt + i, 0))
    obw = pl.BlockSpec((HG_TILE, HG_KD), lambda b, i: (b * nt + nt - 1 - i, 0))
    st = pl.BlockSpec((1, 2, 2, HG_HALF, HG_HALF), lambda b, i: (b, 0, 0, 0, 0))
    tile = lambda: pltpu.VMEM((HG_SLABS, HG_TILE, 128), F32)
    return pl.pallas_call(
        functools.partial(_hgrn_kernel, nt, layer),
        grid=(nseq, nt),
        in_specs=[fwd, fwd, fwd, bwd, bwd, bwd, pl.BlockSpec((2 * DEPTH, HG_KD), lambda b, i: (0, 0)), st],
        out_specs=[ofw, obw, st],
        out_shape=[jax.ShapeDtypeStruct((nseq * seqlen, HG_WIDTH), F32)] * 2
        + [jax.ShapeDtypeStruct((nseq, 2, 2, HG_HALF, HG_HALF), F32)],
        scratch_shapes=[pltpu.VMEM((2, 2, HG_HALF, HG_HALF), F32)] + [tile() for _ in range(7)],
        compiler_params=_cparams("arbitrary", "arbitrary"),
        name="hgrn_scan",
    )(hq, hff, hi, hq, hfb, hi, lb.reshape(2 * DEPTH, HG_KD), s0)


def _cd_out_kernel(x_ref, occ_ref, ocl_ref, ofc_ref, ofl_ref, obc_ref, obl_ref, hg_ref, go_ref, wout_ref, gt_ref,
                   o_ref, wa_s, wb_s):
    @pl.when(pl.program_id(0) == 0)
    def _():
        wa_s[...] = wout_ref[0:MLA_WIDTH, :].astype(BF16)
        wb_s[...] = wout_ref[MLA_WIDTH:, :].astype(BF16)

    ri = lax.broadcasted_iota(jnp.int32, (HG_HALF, HG_HALF), 0) // HG_VAL
    ci = lax.broadcasted_iota(jnp.int32, (HG_HALF, HG_HALF), 1) // HG_VAL
    ones_bd = jnp.where(ri == ci, 1.0, 0.0).astype(BF16)
    o = _ctx_or_lat(ofc_ref, ofl_ref) + _ctx_or_lat(obc_ref, obl_ref)
    sq = o * o
    hi = sq.astype(BF16)
    lo = (sq - hi.astype(F32)).astype(BF16)
    ms = jnp.concatenate(
        [jnp.dot(hi[:, l], ones_bd, preferred_element_type=F32) + jnp.dot(lo[:, l], ones_bd, preferred_element_type=F32)
         for l in (slice(0, HG_HALF), slice(HG_HALF, 2 * HG_HALF))], axis=1) * (1.0 / HG_VAL)
    hg = hg_ref[...]
    od = o * lax.rsqrt(ms + EPS) * go_ref[...] * (hg * jax.nn.sigmoid(hg))
    out = (jnp.dot(_ctx_or_lat(occ_ref, ocl_ref).astype(BF16), wa_s[...], preferred_element_type=F32)
           + jnp.dot(od.astype(BF16), wb_s[...], preferred_element_type=F32))
    o_ref[...] = x_ref[...] + gt_ref[0] * out


def _cd_out(x, oc, of, ob, hg, g_o, w_out, mod, layer):
    row = lambda n: pl.BlockSpec((ROW_BLK, n), lambda i: (i, 0))
    full = lambda a, b: pl.BlockSpec((a, b), lambda i: (0, 0))
    return pl.pallas_call(
        _cd_out_kernel,
        grid=(N_ROW_BLK,),
        in_specs=[row(D_MODEL)] + _split_specs(MLA_WIDTH) + _split_specs(HG_WIDTH) + _split_specs(HG_WIDTH)
        + [row(HG_WIDTH), full(1, HG_WIDTH), full(MLA_WIDTH + HG_WIDTH, D_MODEL), _mod_spec(layer, 2)],
        out_specs=row(D_MODEL),
        out_shape=jax.ShapeDtypeStruct((N_TOK, D_MODEL), F32),
        scratch_shapes=[pltpu.VMEM((MLA_WIDTH, D_MODEL), BF16), pltpu.VMEM((HG_WIDTH, D_MODEL), BF16)],
        compiler_params=_cparams("arbitrary"),
        name="cd_out",
    )(x, oc[0], oc[1], of[0], of[1], ob[0], ob[1], hg, jnp.tile(g_o, HG_HEADS).reshape(1, HG_WIDTH), w_out, mod)


N_PAIRS = 6
N_CLASSES = N_GROUPS * N_PAIRS
CLS_ROWS = 32
MOE_BM = 256
MOE_NBLK = N_TOK // MOE_BM + N_CLASSES
MOE_ROWS = MOE_NBLK * MOE_BM
PACK_W = D_MODEL // 2
PACK_ROW = PACK_W + 128


def _moe_route_kernel(x_ref, g_ref, sc_ref, sh_ref, rw_ref, rb_ref, hp_ref, ti_ref, cnt_ref, base_s):
    @pl.when(pl.program_id(0) == 0)
    def _():
        base_s[...] = jnp.zeros_like(base_s)

    h = _norm_mod(x_ref[...], g_ref[...], sc_ref[0], sh_ref[0])
    logits = lax.dot_general(rw_ref[...], h, (((1,), (1,)), ((), ())), precision=HI,
                             preferred_element_type=F32)
    aff = jax.nn.sigmoid(logits)
    sel = aff + rb_ref[...]
    s = [sel[e:e + 1, :] for e in range(N_EXPERTS)]
    a = [aff[e:e + 1, :] for e in range(N_EXPERTS)]
    gs = []
    for g in range(N_GROUPS):
        m = s[4 * g:4 * g + 4]
        pairs = [m[i] + m[j] for i in range(4) for j in range(i + 1, 4)]
        gs.append(functools.reduce(jnp.maximum, pairs))
    gmax = functools.reduce(jnp.maximum, gs)
    taken = jnp.zeros_like(gmax) > 1.0
    gsel = []
    for g in range(N_GROUPS):
        hit = (gs[g] == gmax) & jnp.logical_not(taken)
        gsel.append(hit)
        taken = taken | hit
    e_lo = jnp.zeros(gmax.shape, jnp.int32)
    e_hi = jnp.zeros(gmax.shape, jnp.int32)
    a_lo = jnp.zeros_like(gmax)
    a_hi = jnp.zeros_like(gmax)
    nsel = jnp.zeros(gmax.shape, jnp.int32)
    for g in range(N_GROUPS):
        for i in range(4):
            e = 4 * g + i
            beat = jnp.zeros(gmax.shape, jnp.int32)
            for j in range(4):
                if j != i:
                    o = 4 * g + j
                    beat = beat + jnp.where((s[o] > s[e]) | ((s[o] == s[e]) & (j < i)), 1, 0)
            pick = gsel[g] & (beat < 2)
            is_first = pick & (nsel == 0)
            is_second = pick & (nsel == 1)
            e_lo = jnp.where(is_first, e, e_lo)
            a_lo = jnp.where(is_first, a[e], a_lo)
            e_hi = jnp.where(is_second, e, e_hi)
            a_hi = jnp.where(is_second, a[e], a_hi)
            nsel = nsel + jnp.where(pick, 1, 0)
    grp = e_lo // EXPERTS_PER_GROUP
    lo = e_lo - grp * EXPERTS_PER_GROUP
    hi = e_hi - grp * EXPERTS_PER_GROUP
    cls = grp * N_PAIRS + ((lo * (7 - lo)) >> 1) + (hi - lo - 1)
    wsum = a_lo + a_hi
    w_lo, w_hi = a_lo / wsum, a_hi / wsum
    onehot = (lax.broadcasted_iota(jnp.int32, (CLS_ROWS, ROW_BLK), 0) == cls).astype(F32)
    tt = lax.broadcasted_iota(jnp.int32, (ROW_BLK, ROW_BLK), 0) < lax.broadcasted_iota(jnp.int32, (ROW_BLK, ROW_BLK), 1)
    before = _dot(onehot, jnp.where(tt, 1.0, 0.0))
    base = base_s[...]
    rank = jnp.sum(onehot * (before + base[:, 0:1]), axis=0, keepdims=True).astype(jnp.int32)
    base = base + jnp.sum(onehot, axis=1, keepdims=True)
    base_s[...] = base
    cnt_ref[...] = base.astype(jnp.int32)
    ti_ref[0] = jnp.concatenate([cls, rank, e_lo, e_hi, jnp.zeros((4, ROW_BLK), jnp.int32)], axis=0)
    bits = pltpu.bitcast(h.astype(BF16).astype(F32), jnp.uint32)
    hp_ref[:, 0:PACK_W] = (bits[:, PACK_W:] & jnp.uint32(0xFFFF0000)) | (bits[:, 0:PACK_W] >> 16)
    lane = lax.broadcasted_iota(jnp.int32, (ROW_BLK, 128), 1)
    ident = lax.broadcasted_iota(jnp.int32, (ROW_BLK, ROW_BLK), 0) == lax.broadcasted_iota(jnp.int32, (ROW_BLK, ROW_BLK), 1)
    col = lambda r: jnp.sum(jnp.where(ident, r, 0.0), axis=1, keepdims=True)
    wl = jnp.where(lane == 0, col(w_lo), jnp.where(lane == 1, col(w_hi), 0.0))
    hp_ref[:, PACK_W:] = pltpu.bitcast(wl, jnp.uint32)


def _moe_route(x, mod, g, router_w, router_b, layer):
    row = lambda n: pl.BlockSpec((ROW_BLK, n), lambda i: (i, 0))
    full = lambda a, b: pl.BlockSpec((a, b), lambda i: (0, 0))
    return pl.pallas_call(
        _moe_route_kernel,
        grid=(N_ROW_BLK,),
        in_specs=[row(D_MODEL), full(1, D_MODEL), _mod_spec(layer, 4), _mod_spec(layer, 3),
                  full(N_EXPERTS, D_MODEL), full(N_EXPERTS, 1)],
        out_specs=[row(PACK_ROW), pl.BlockSpec((1, 8, ROW_BLK), lambda i: (i, 0, 0)), full(CLS_ROWS, 128)],
        out_shape=[jax.ShapeDtypeStruct((N_TOK, PACK_ROW), jnp.uint32),
                   jax.ShapeDtypeStruct((N_ROW_BLK, 8, ROW_BLK), jnp.int32),
                   jax.ShapeDtypeStruct((CLS_ROWS, 128), jnp.int32)],
        scratch_shapes=[pltpu.VMEM((CLS_ROWS, 128), F32)],
        compiler_params=_cparams("arbitrary"),
        name="moe_route",
    )(x, g.reshape(1, D_MODEL), mod, mod, router_w.T, router_b.reshape(N_EXPERTS, 1))


def _moe_experts_kernel(layer, elo_ref, ehi_ref, nblk_ref, pos_ref, hp_ref, wgl_ref, wul_ref, wdl_ref,
                        wgh_ref, wuh_ref, wdh_ref, y_ref, perm_s, stage_s):
    del layer, elo_ref, ehi_ref
    j = pl.program_id(0)

    @pl.when(j == 0)
    def _():
        def init(r, c):
            perm_s[r] = 0
            return c

        lax.fori_loop(0, MOE_ROWS, init, 0, unroll=16)

        def build(t, c):
            perm_s[pos_ref[t]] = t
            return c

        lax.fori_loop(0, N_TOK, build, 0, unroll=16)

    @pl.when(j < nblk_ref[0])
    def _():
        def gather(r, c):
            src = perm_s[j * MOE_BM + r]
            stage_s[pl.ds(r, 1), :] = hp_ref[pl.ds(src, 1), :]
            return c

        lax.fori_loop(0, MOE_BM, gather, 0, unroll=8)
        words = stage_s[:, 0:PACK_W]
        h_a = pltpu.bitcast(words << 16, F32).astype(BF16)
        h_b = pltpu.bitcast(words & jnp.uint32(0xFFFF0000), F32).astype(BF16)
        w = pltpu.bitcast(stage_s[:, PACK_W:], F32)
        acc = None
        for wg, wu, wd, gate in ((wgl_ref, wul_ref, wdl_ref, w[:, 0:1]), (wgh_ref, wuh_ref, wdh_ref, w[:, 1:2])):
            g = _dot(h_a, wg[0, 0, 0:PACK_W, :]) + _dot(h_b, wg[0, 0, PACK_W:, :])
            u = _dot(h_a, wu[0, 0, 0:PACK_W, :]) + _dot(h_b, wu[0, 0, PACK_W:, :])
            hid = g * jax.nn.sigmoid(g) * u * gate
            y = _dot(hid, wd[0, 0])
            acc = y if acc is None else acc + y
        y_ref[...] = acc

    @pl.when(j >= nblk_ref[0])
    def _():
        y_ref[...] = jnp.zeros_like(y_ref)


def _moe_experts(hp, blk_elo, blk_ehi, nblk, pos, w_gate, w_up, w_down, layer):
    last = lambda j, nb: jnp.minimum(j, nb[0] - 1)
    wspec = lambda a, b, which: pl.BlockSpec(
        (1, 1, a, b), lambda j, elo, ehi, nb, pos: (layer, (elo, ehi)[which][last(j, nb)], 0, 0))
    grid_spec = pltpu.PrefetchScalarGridSpec(
        num_scalar_prefetch=4,
        grid=(MOE_NBLK,),
        in_specs=[pl.BlockSpec((N_TOK, PACK_ROW), lambda j, *_: (0, 0), pipeline_mode=pl.Buffered(1)),
                  wspec(D_MODEL, D_FF, 0), wspec(D_MODEL, D_FF, 0), wspec(D_FF, D_MODEL, 0),
                  wspec(D_MODEL, D_FF, 1), wspec(D_MODEL, D_FF, 1), wspec(D_FF, D_MODEL, 1)],
        out_specs=pl.BlockSpec((MOE_BM, D_MODEL), lambda j, *_: (j, 0)),
        scratch_shapes=[pltpu.SMEM((MOE_ROWS,), jnp.int32), pltpu.VMEM((MOE_BM, PACK_ROW), jnp.uint32)],
    )
    return pl.pallas_call(
        functools.partial(_moe_experts_kernel, layer),
        grid_spec=grid_spec,
        out_shape=jax.ShapeDtypeStruct((MOE_ROWS, D_MODEL), F32),
        compiler_params=_cparams("arbitrary"),
        name="moe_experts",
    )(blk_elo, blk_ehi, nblk, pos, hp, w_gate, w_up, w_down, w_gate, w_up, w_down)


def _moe_combine_kernel(pos_ref, x_ref, y_ref, gt_ref, o_ref, stage_s):
    i = pl.program_id(1)

    def gather(r, c):
        stage_s[pl.ds(r, 1), :] = y_ref[pl.ds(pos_ref[i * ROW_BLK + r], 1), :]
        return c

    lax.fori_loop(0, ROW_BLK, gather, 0, unroll=8)
    o_ref[...] = x_ref[...] + gt_ref[0] * stage_s[...]


def _moe_combine(x, y_sorted, pos, mod, layer):
    half = D_MODEL // 2
    grid_spec = pltpu.PrefetchScalarGridSpec(
        num_scalar_prefetch=1,
        grid=(2, N_ROW_BLK),
        in_specs=[pl.BlockSpec((ROW_BLK, half), lambda c, i, pos: (i, c)),
                  pl.BlockSpec((MOE_ROWS, half), lambda c, i, pos: (0, c), pipeline_mode=pl.Buffered(1)),
                  pl.BlockSpec((1, 1, half), lambda c, i, pos: ((layer * MOD_ROWS + _mod_group(i)) * N_MOD + 5, 0, c))],
        out_specs=pl.BlockSpec((ROW_BLK, half), lambda c, i, pos: (i, c)),
        scratch_shapes=[pltpu.VMEM((ROW_BLK, half), F32)],
    )
    return pl.pallas_call(
        _moe_combine_kernel,
        grid_spec=grid_spec,
        out_shape=jax.ShapeDtypeStruct((N_TOK, D_MODEL), F32),
        compiler_params=_cparams("arbitrary", "arbitrary"),
        name="moe_combine",
    )(pos, x, y_sorted, mod)


def _moe(x, mod, g, router_w, router_b, w_gate, w_up, w_down, layer):
    hp, info, counts = _moe_route(x, mod, g, router_w, router_b, layer)
    cls = info[:, 0, :].reshape(N_TOK)
    rank = info[:, 1, :].reshape(N_TOK)
    cnt = counts[:N_CLASSES, 0]
    nb = (cnt + MOE_BM - 1) // MOE_BM
    ends = jnp.cumsum(nb)
    starts = ends - nb
    pos = (starts * MOE_BM)[cls] + rank
    blk = jnp.arange(MOE_NBLK, dtype=jnp.int32)
    blk_cls = jnp.minimum(jnp.sum((blk[:, None] >= ends[None, :]).astype(jnp.int32), axis=1), N_CLASSES - 1)
    pair_lo = jnp.asarray([0, 0, 0, 1, 1, 2], jnp.int32)
    pair_hi = jnp.asarray([1, 2, 3, 2, 3, 3], jnp.int32)
    grp = blk_cls // N_PAIRS
    blk_elo = (grp * EXPERTS_PER_GROUP + pair_lo[blk_cls % N_PAIRS]).astype(jnp.int32)
    blk_ehi = (grp * EXPERTS_PER_GROUP + pair_hi[blk_cls % N_PAIRS]).astype(jnp.int32)
    nblk = ends[-1:].astype(jnp.int32)
    y_sorted = _moe_experts(hp, blk_elo, blk_ehi, nblk, pos.astype(jnp.int32), w_gate, w_up, w_down, layer)
    return _moe_combine(x, y_sorted, pos.astype(jnp.int32), mod, layer)


def _final_norm_kernel(x_ref, g_ref, o_ref):
    o_ref[...] = _rms(x_ref[...]) * g_ref[...]


def _final_norm(x, g, row0, rows):
    base = row0 // ROW_BLK
    return pl.pallas_call(
        _final_norm_kernel,
        grid=(rows // ROW_BLK,),
        in_specs=[pl.BlockSpec((ROW_BLK, D_MODEL), lambda i: (base + i, 0)), pl.BlockSpec((1, D_MODEL), lambda i: (0, 0))],
        out_specs=pl.BlockSpec((ROW_BLK, D_MODEL), lambda i: (i, 0)),
        out_shape=jax.ShapeDtypeStruct((rows, D_MODEL), F32),
        compiler_params=_cparams("arbitrary"),
        name="final_norm",
    )(x, g.reshape(1, D_MODEL))


def kernel(x_prompt, x_sample, cache_attn_k, cache_attn_v, state_ssm_re, state_ssm_im, cache_mla_ckv, cache_mla_kpe,
           state_hgrn, c, c_ctx, w_mod, b_mod, g_mix, g_ffn, g_final, router_w, router_b, moe_w_gate, moe_w_up,
           moe_w_down, ab_w_in, ab_sink, s5_lam_re, s5_lam_im, s5_log_dt, s5_b_re, s5_b_im, s5_c_re, s5_c_im, s5_d,
           s5_w_glu, ab_w_out, cd_w_in, mla_g_q, mla_w_q_up, mla_g_kv, mla_w_kv_up, hg_lower_bounds, hg_g_o, cd_w_out):
    x = jnp.concatenate([x_prompt.reshape(N_CTX_TOK, D_MODEL), x_sample.reshape(N_LAT_TOK, D_MODEL)], axis=0)
    cond = jnp.zeros((MOD_ROWS, D_MODEL), F32).at[0].set(c_ctx).at[1:1 + DEC_BATCH].set(c)
    mod = _modulation(cond, w_mod, b_mod)
    keep = ([], [], [], [], [], [], [])
    for l in range(DEPTH):
        j = l // 2
        if l % 2 == 0:
            q, k, v, u = _ab_in(x, mod, g_mix[l], ab_w_in[j], l)
            o_a = _attention_a(q, k, v, cache_attn_k[:, j].reshape(DEC_BATCH, PAST_LEN, A_KV_WIDTH),
                               cache_attn_v[:, j].reshape(DEC_BATCH, PAST_LEN, A_KV_WIDTH), ab_sink[j])
            y, fin_re, fin_im = _s5_scan(u, state_ssm_re[:, j], state_ssm_im[:, j], s5_lam_re[j], s5_lam_im[j],
                                         s5_log_dt[j], s5_b_re[j], s5_b_im[j], s5_c_re[j], s5_c_im[j])
            x = _ab_out(x, o_a, y, u, s5_d[j], s5_w_glu[j], ab_w_out[j], mod, l)
            keep[0].append(k[:N_CTX_TOK].reshape(BATCH, SEQ, A_KV_HEADS, HEAD_DIM))
            keep[1].append(v[:N_CTX_TOK].reshape(BATCH, SEQ, A_KV_HEADS, HEAD_DIM))
            keep[2].append(fin_re)
            keep[3].append(fin_im)
        else:
            qn, qr, ckv, kpe, kn, vm, hq, hff, hfb, hi, hg = _cd_in(
                x, mod, g_mix[l], cd_w_in[j], mla_g_q[j], mla_w_q_up[j], mla_g_kv[j], mla_w_kv_up[j], l)
            kcn, vc = _mla_cache_kv(cache_mla_ckv[:, j].reshape(DEC_BATCH * PAST_LEN, MLA_KV_RANK), mla_w_kv_up[j])
            o_c = _attention_mla(qn, qr, kn, kpe, vm, kcn, cache_mla_kpe[:, j], vc)
            s0_ctx = jnp.zeros((BATCH, 2, 2, HG_HALF, HG_HALF), F32)
            of_c, ob_c, s_fin = _hgrn_scan(hq, hff, hfb, hi, hg_lower_bounds, s0_ctx, l, 0, BATCH, SEQ)
            of_l, ob_l, _ = _hgrn_scan(hq, hff, hfb, hi, hg_lower_bounds, _hg_state_to_blocks(state_hgrn[:, j]), l,
                                       N_CTX_TOK, DEC_BATCH, DEC_SEQ)
            x = _cd_out(x, o_c, (of_c, of_l), (ob_c, ob_l), hg, hg_g_o[j], cd_w_out[j], mod, l)
            keep[4].append(ckv[:N_CTX_TOK].reshape(BATCH, SEQ, MLA_KV_RANK))
            keep[5].append(kpe[:N_CTX_TOK, :MLA_ROPE].reshape(BATCH, SEQ, MLA_ROPE))
            keep[6].append(_hg_state_from_blocks(s_fin))
        x = _moe(x, mod, g_ffn[l], router_w, router_b, moe_w_gate, moe_w_up, moe_w_down, l)
    y_ctx = _final_norm(x, g_final, 0, N_CTX_TOK)
    y_lat = _final_norm(x, g_final, N_CTX_TOK, N_LAT_TOK)
    return (y_ctx.reshape(BATCH, SEQ, D_MODEL), y_lat.reshape(DEC_BATCH, DEC_SEQ, D_MODEL),
            jnp.stack(keep[0], 1), jnp.stack(keep[1], 1), jnp.stack(keep[2], 1), jnp.stack(keep[3], 1),
            jnp.stack(keep[4], 1), jnp.stack(keep[5], 1), jnp.stack(keep[6], 1))
```

```python
import functools
import math

import numpy as np
import jax
import jax.numpy as jnp
from jax import lax
from jax.experimental import pallas as pl
from jax.experimental.pallas import tpu as pltpu

F32 = jnp.float32
BF16 = jnp.bfloat16

D_MODEL = 1024
BATCH = 16
SEQ = 256
DEPTH = 2
DEC_BATCH = 4
DEC_SEQ = 1024
PAST_LEN = 512
GRID_W = 64
N_MOD = 6
EPS = 1e-6
NEG_INF = -1e30
ROPE_BASE = 10000.0
HEAD_DIM = 64
A_HEADS = 8
A_KV_HEADS = 2
A_WINDOW = 128
A_WIDTH = A_HEADS * HEAD_DIM
A_KV_WIDTH = A_KV_HEADS * HEAD_DIM
S5_WIDTH = D_MODEL // 2
S5_GROUP = 16
S5_GROUPS = S5_WIDTH // S5_GROUP
S5_STATE = 64
MLA_HEADS = 8
MLA_Q_RANK = D_MODEL // 4
MLA_KV_RANK = D_MODEL // 8
MLA_NOPE = 64
MLA_ROPE = 32
MLA_V = 64
MLA_WIDTH = MLA_HEADS * MLA_V
HG_HEADS = 8
HG_KEY = 64
HG_VAL = 64
HG_KD = HG_HEADS * HG_KEY
HG_WIDTH = HG_HEADS * HG_VAL
HG_CHUNK = 16
N_EXPERTS = 16
N_GROUPS = 4
EXPERTS_PER_GROUP = N_EXPERTS // N_GROUPS
D_FF = D_MODEL // 2

N_CTX_TOK = BATCH * SEQ
N_LAT_TOK = DEC_BATCH * DEC_SEQ
N_TOK = N_CTX_TOK + N_LAT_TOK
ROW_BLK = 256
N_ROW_BLK = N_TOK // ROW_BLK
N_CTX_BLK = N_CTX_TOK // ROW_BLK
LAT_BLK_PER_SEQ = DEC_SEQ // ROW_BLK
MOD_ROWS = 8
VMEM_LIMIT = 56 * 1024 * 1024


def _cparams(*sem):
    return pltpu.CompilerParams(dimension_semantics=sem, vmem_limit_bytes=VMEM_LIMIT)


def _mod_group(i):
    return jnp.where(i < N_CTX_BLK, 0, 1 + (i - N_CTX_BLK) // LAT_BLK_PER_SEQ)


def _mod_spec(layer, which):
    return pl.BlockSpec((1, 1, D_MODEL), lambda i: ((layer * MOD_ROWS + _mod_group(i)) * N_MOD + which, 0, 0))


def _rope_blk(i):
    return jnp.where(i < N_CTX_BLK, 0, 1 + (i - N_CTX_BLK) % LAT_BLK_PER_SEQ)


def _rope_tables(rot_dim):
    n_freq = rot_dim // 4
    t = np.arange(DEC_SEQ)
    rows = (t // GRID_W).astype(np.float32)
    cols = (t % GRID_W).astype(np.float32)
    inv = (np.float32(ROPE_BASE) ** (-np.arange(n_freq, dtype=np.float32) / np.float32(n_freq))).astype(np.float32)
    ang_r = rows[:, None] * inv[None, :]
    ang_c = cols[:, None] * inv[None, :]
    ang = np.concatenate([ang_r, ang_r, ang_c, ang_c], axis=-1).astype(np.float32)
    reps = 128 // rot_dim
    cos = np.tile(np.cos(ang), (1, reps)).astype(np.float32)
    sin = np.tile(np.sin(ang), (1, reps)).astype(np.float32)
    lane = np.arange(128)
    first = (lane % (2 * n_freq)) < n_freq
    sin_a = np.where(first[None, :], -sin, 0.0).astype(np.float32)
    sin_b = np.where(first[None, :], 0.0, sin).astype(np.float32)
    ident = np.zeros((ROW_BLK, 128), np.float32)
    cos = np.concatenate([ident + 1.0, cos], axis=0)
    sin_a = np.concatenate([ident, sin_a], axis=0)
    sin_b = np.concatenate([ident, sin_b], axis=0)
    return jnp.asarray(cos), jnp.asarray(sin_a), jnp.asarray(sin_b)


def _apply_rope(x, cos, sin_a, sin_b, quarter):
    outs = []
    for j in range(x.shape[1] // 128):
        xt = x[:, 128 * j:128 * (j + 1)]
        up = pltpu.roll(xt, 128 - quarter, axis=1)
        dn = pltpu.roll(xt, quarter, axis=1)
        outs.append(xt * cos + up * sin_a + dn * sin_b)
    return outs[0] if len(outs) == 1 else jnp.concatenate(outs, axis=1)


def _rms(x):
    return x * lax.rsqrt(jnp.mean(x * x, axis=-1, keepdims=True) + EPS)


def _norm_mod(x, g, sc, sh):
    return _rms(x) * g * (1.0 + sc) + sh


def _dot(a, b):
    return jnp.dot(a.astype(BF16), b.astype(BF16), preferred_element_type=F32)


def _dot_nt(a, b):
    return lax.dot_general(a.astype(BF16), b.astype(BF16), (((1,), (1,)), ((), ())), preferred_element_type=F32)


def _mod_kernel(cond_ref, w_ref, b_ref, o_ref):
    c = cond_ref[...]
    s = c * jax.nn.sigmoid(c)
    o_ref[0] = _dot(s, w_ref[0]) + b_ref[0]


def _modulation(cond, w_mod, b_mod):
    nb = 1024
    out = pl.pallas_call(
        _mod_kernel,
        grid=(DEPTH, N_MOD * D_MODEL // nb),
        in_specs=[pl.BlockSpec((MOD_ROWS, D_MODEL), lambda l, n: (0, 0)),
                  pl.BlockSpec((1, D_MODEL, nb), lambda l, n: (l, 0, n)),
                  pl.BlockSpec((1, 1, nb), lambda l, n: (l, 0, n))],
        out_specs=pl.BlockSpec((1, MOD_ROWS, nb), lambda l, n: (l, 0, n)),
        out_shape=jax.ShapeDtypeStruct((DEPTH, MOD_ROWS, N_MOD * D_MODEL), F32),
        compiler_params=_cparams("arbitrary", "arbitrary"),
        name="modulation",
    )(cond, w_mod, b_mod.reshape(DEPTH, 1, N_MOD * D_MODEL))
    return out.reshape(DEPTH * MOD_ROWS * N_MOD, 1, D_MODEL)


def _ab_in_kernel(x_ref, g_ref, sc_ref, sh_ref, w_ref, cos_ref, sa_ref, sb_ref,
                  q_ref, k_ref, v_ref, u_ref, wq_s, wk_s, wv_s, wu_s):
    @pl.when(pl.program_id(0) == 0)
    def _():
        wq_s[...] = w_ref[:, 0:A_WIDTH].astype(BF16)
        wk_s[...] = w_ref[:, A_WIDTH:A_WIDTH + A_KV_WIDTH].astype(BF16)
        wv_s[...] = w_ref[:, A_WIDTH + A_KV_WIDTH:A_WIDTH + 2 * A_KV_WIDTH].astype(BF16)
        wu_s[...] = w_ref[:, A_WIDTH + 2 * A_KV_WIDTH:].astype(BF16)

    h = _norm_mod(x_ref[...], g_ref[...], sc_ref[0], sh_ref[0]).astype(BF16)
    cos, sa, sb = cos_ref[...], sa_ref[...], sb_ref[...]
    q = jnp.dot(h, wq_s[...], preferred_element_type=F32)
    q_ref[...] = _apply_rope(q, cos, sa, sb, HEAD_DIM // 4)
    k = jnp.dot(h, wk_s[...], preferred_element_type=F32)
    k_ref[...] = _apply_rope(k, cos, sa, sb, HEAD_DIM // 4)
    v_ref[...] = jnp.dot(h, wv_s[...], preferred_element_type=F32)
    u_ref[...] = jnp.dot(h, wu_s[...], preferred_element_type=F32)


def _ab_in(x, mod, g, w, layer):
    cos, sa, sb = _rope_tables(HEAD_DIM)
    d_in = w.shape[1]
    row = lambda n: pl.BlockSpec((ROW_BLK, n), lambda i: (i, 0))
    rope = pl.BlockSpec((ROW_BLK, 128), lambda i: (_rope_blk(i), 0))
    return pl.pallas_call(
        _ab_in_kernel,
        grid=(N_ROW_BLK,),
        in_specs=[row(D_MODEL), pl.BlockSpec((1, D_MODEL), lambda i: (0, 0)),
                  _mod_spec(layer, 1), _mod_spec(layer, 0),
                  pl.BlockSpec((D_MODEL, d_in), lambda i: (0, 0)), rope, rope, rope],
        out_specs=[row(A_WIDTH), row(A_KV_WIDTH), row(A_KV_WIDTH), row(S5_WIDTH)],
        out_shape=[jax.ShapeDtypeStruct((N_TOK, A_WIDTH), F32), jax.ShapeDtypeStruct((N_TOK, A_KV_WIDTH), F32),
                   jax.ShapeDtypeStruct((N_TOK, A_KV_WIDTH), F32), jax.ShapeDtypeStruct((N_TOK, S5_WIDTH), F32)],
        scratch_shapes=[pltpu.VMEM((D_MODEL, A_WIDTH), BF16), pltpu.VMEM((D_MODEL, A_KV_WIDTH), BF16),
                        pltpu.VMEM((D_MODEL, A_KV_WIDTH), BF16), pltpu.VMEM((D_MODEL, S5_WIDTH), BF16)],
        compiler_params=_cparams("arbitrary"),
        name="ab_in",
    )(x, g.reshape(1, D_MODEL), mod, mod, w, cos, sa, sb)


def _softmax_pv(s_list, v_list, sink):
    m = functools.reduce(jnp.maximum, [jnp.max(s, axis=-1, keepdims=True) for s in s_list])
    if sink is not None:
        m = jnp.maximum(m, sink)
    ps = [jnp.exp(s - m) for s in s_list]
    l = functools.reduce(jnp.add, [jnp.sum(p, axis=-1, keepdims=True) for p in ps])
    if sink is not None:
        l = l + jnp.exp(sink - m)
    o = functools.reduce(jnp.add, [_dot(p, v) for p, v in zip(ps, v_list)])
    return o / l


def _gqa_with_sink(sink_ref, q, k_of, v_of, bias, o_ref):
    g = A_HEADS // A_KV_HEADS
    r = q.shape[0]
    outs = [None] * A_HEADS
    for kh in range(A_KV_HEADS):
        heads = range(g * kh, g * (kh + 1))
        qg = jnp.concatenate([q[:, HEAD_DIM * h:HEAD_DIM * (h + 1)] for h in heads], axis=0)
        sink = jnp.concatenate([jnp.full((r, 1), sink_ref[h], F32) for h in heads], axis=0)
        s = _dot_nt(qg, k_of(kh))
        if bias is not None:
            s = s + jnp.concatenate([bias] * g, axis=0)
        m = jnp.maximum(jnp.max(s, axis=-1, keepdims=True), sink)
        p = jnp.exp(s - m)
        l = jnp.sum(p, axis=-1, keepdims=True) + jnp.exp(sink - m)
        o = _dot(p, v_of(kh)) / l
        for n, h in enumerate(heads):
            outs[h] = o[r * n:r * (n + 1)]
    o_ref[...] = jnp.concatenate(outs, axis=1)


def _attn_ctx_kernel(sink_ref, q_ref, k_ref, v_ref, o_ref):
    head = lambda ref: (lambda kh: ref[:, HEAD_DIM * kh:HEAD_DIM * (kh + 1)])
    _gqa_with_sink(sink_ref, q_ref[...] * HEAD_DIM ** -0.5, head(k_ref), head(v_ref), None, o_ref)


def _attn_lat_kernel(sink_ref, q_ref, kp_ref, kc_ref, kn_ref, vp_ref, vc_ref, vn_ref, kx_ref, vx_ref, o_ref):
    n = pl.program_id(1)
    nb = DEC_SEQ // A_WINDOW
    i = lax.broadcasted_iota(jnp.int32, (A_WINDOW, A_WINDOW), 0)
    j = lax.broadcasted_iota(jnp.int32, (A_WINDOW, A_WINDOW), 1)
    zero = jnp.zeros((A_WINDOW, A_WINDOW), F32)
    bias = jnp.concatenate([jnp.where((j >= i) & (n > 0), 0.0, NEG_INF), zero,
                            jnp.where((j <= i) & (n < nb - 1), 0.0, NEG_INF),
                            jnp.zeros((A_WINDOW, PAST_LEN), F32)], axis=1)

    def rows(p_ref, c_ref, n_ref, x_ref):
        def of(kh):
            sl = slice(HEAD_DIM * kh, HEAD_DIM * (kh + 1))
            return jnp.concatenate([p_ref[:, sl], c_ref[:, sl], n_ref[:, sl], x_ref[0, :, sl]], axis=0)
        return of

    _gqa_with_sink(sink_ref, q_ref[...] * HEAD_DIM ** -0.5, rows(kp_ref, kc_ref, kn_ref, kx_ref),
                   rows(vp_ref, vc_ref, vn_ref, vx_ref), bias, o_ref)


def _attention_a(q, k, v, cache_k, cache_v, sink):
    smem = pl.BlockSpec(memory_space=pltpu.SMEM)
    o = pl.pallas_call(
        _attn_ctx_kernel,
        grid=(BATCH,),
        in_specs=[smem, pl.BlockSpec((SEQ, A_WIDTH), lambda b: (b, 0)),
                  pl.BlockSpec((SEQ, A_KV_WIDTH), lambda b: (b, 0)), pl.BlockSpec((SEQ, A_KV_WIDTH), lambda b: (b, 0))],
        out_specs=pl.BlockSpec((SEQ, A_WIDTH), lambda b: (b, 0)),
        out_shape=jax.ShapeDtypeStruct((N_CTX_TOK, A_WIDTH), F32),
        compiler_params=_cparams("arbitrary"),
        name="attn_a_ctx",
    )(sink, q, k, v)
    nb = DEC_SEQ // A_WINDOW
    base = N_CTX_TOK // A_WINDOW
    cur = lambda b, n: (base + b * nb + n, 0)
    prev = lambda b, n: (base + b * nb + jnp.maximum(n - 1, 0), 0)
    nxt = lambda b, n: (base + b * nb + jnp.minimum(n + 1, nb - 1), 0)
    kv = lambda f: pl.BlockSpec((A_WINDOW, A_KV_WIDTH), f)
    cache = pl.BlockSpec((1, PAST_LEN, A_KV_WIDTH), lambda b, n: (b, 0, 0))
    o_lat = pl.pallas_call(
        _attn_lat_kernel,
        grid=(DEC_BATCH, nb),
        in_specs=[smem, pl.BlockSpec((A_WINDOW, A_WIDTH), cur), kv(prev), kv(cur), kv(nxt), kv(prev), kv(cur), kv(nxt),
                  cache, cache],
        out_specs=pl.BlockSpec((A_WINDOW, A_WIDTH), lambda b, n: (b * nb + n, 0)),
        out_shape=jax.ShapeDtypeStruct((N_LAT_TOK, A_WIDTH), F32),
        compiler_params=_cparams("arbitrary", "arbitrary"),
        name="attn_a_lat",
    )(sink, q, k, k, k, v, v, v, cache_k, cache_v)
    return o, o_lat


S5_CHUNK = 16
S5_OCT = 128 // S5_GROUP
S5_NOCT = S5_GROUPS // S5_OCT
S5_K = S5_CHUNK * 128
S5_PART = S5_OCT * S5_STATE
S5_SW = 4 * S5_PART
S5_ROWS_CTX = BATCH * SEQ // S5_CHUNK
S5_ROWS_LAT = DEC_BATCH * DEC_SEQ // S5_CHUNK
S5_ROWS = S5_ROWS_CTX + S5_ROWS_LAT
S5_NB = 4
HI = lax.Precision.HIGHEST


def _s5_disc_kernel(lr_ref, li_ref, ldt_ref, ar_ref, ai_ref, zr_ref, zi_ref):
    lr, li = lr_ref[...], li_ref[...]
    dt = jnp.exp(ldt_ref[...])
    mag = jnp.exp(lr * dt)
    ar, ai = mag * jnp.cos(li * dt), mag * jnp.sin(li * dt)
    den = lr * lr + li * li
    ar_ref[...] = ar
    ai_ref[...] = ai
    zr_ref[...] = ((ar - 1.0) * lr + ai * li) / den
    zi_ref[...] = (ai * lr - (ar - 1.0) * li) / den


def _cmul(xr, xi, yr, yi):
    return xr * yr - xi * yi, xr * yi + xi * yr


def _dot_nt_hi(a, b):
    return lax.dot_general(a, b, (((1,), (1,)), ((), ())), precision=HI, preferred_element_type=F32)


def _s5_prep_kernel(ar_ref, ai_ref, zr_ref, zi_ref, btr_ref, bti_ref, ctr_ref, cti_ref,
                    m_ref, win_ref, wout_ref, a16_ref, pw_s, w_s, k_s):
    t = pl.program_id(1)
    npw = S5_CHUNK + 1
    blk = lambda j: pl.ds(pl.multiple_of(j * 128, 128), 128)

    @pl.when(t == 0)
    def _():
        kd = []
        for d in range(2):
            ar, ai = ar_ref[0, d], ai_ref[0, d]
            pr, pi = jnp.ones_like(ar), jnp.zeros_like(ar)
            bbr, bbi = _cmul(zr_ref[0, d], zi_ref[0, d], btr_ref[0, d], bti_ref[0, d])
            for j in range(npw):
                pw_s[d, 0, j:j + 1, :] = pr
                pw_s[d, 1, j:j + 1, :] = pi
                if j < S5_CHUNK:
                    wr, wi = _cmul(pr, pi, bbr, bbi)
                    w_s[d, 0, 128 * j:128 * (j + 1), :] = wr.astype(BF16)
                    w_s[d, 1, 128 * j:128 * (j + 1), :] = wi.astype(BF16)
                pr, pi = _cmul(pr, pi, ar, ai)
            kd.append(_dot_nt(w_s[d, 0], ctr_ref[0, d]) - _dot_nt(w_s[d, 1], cti_ref[0, d]))
        for jj in range(2 * S5_CHUNK - 1):
            j = jj - (S5_CHUNK - 1)
            if j > 0:
                k = kd[0][128 * j:128 * (j + 1)]
            elif j < 0:
                k = kd[1][128 * -j:128 * (1 - j)]
            else:
                k = kd[0][0:128] + kd[1][0:128]
            k_s[128 * jj:128 * (jj + 1), :] = k
        a16_ref[0] = jnp.concatenate([pw_s[0, 0, S5_CHUNK:npw, :], pw_s[1, 0, S5_CHUNK:npw, :],
                                      pw_s[0, 1, S5_CHUNK:npw, :], pw_s[1, 1, S5_CHUNK:npw, :]], axis=1)

    for tp in range(S5_CHUNK):
        m_ref[0, :, 128 * tp:128 * (tp + 1)] = k_s[blk(S5_CHUNK - 1 + tp - t), :].astype(BF16)

    def power(d, e):
        return pw_s[d, 0, pl.ds(e, 1), :], pw_s[d, 1, pl.ds(e, 1), :]

    for d in range(2):
        j = (S5_CHUNK - 1 - t) if d == 0 else t
        win_ref[0, :, S5_PART * d:S5_PART * (d + 1)] = w_s[d, 0, blk(j), :]
        win_ref[0, :, S5_PART * (2 + d):S5_PART * (3 + d)] = w_s[d, 1, blk(j), :]
        er, ei = _cmul(*power(d, (t + 1) if d == 0 else (S5_CHUNK - t)), ctr_ref[0, d], cti_ref[0, d])
        wout_ref[0, :, S5_PART * d:S5_PART * (d + 1)] = er.astype(BF16)
        wout_ref[0, :, S5_PART * (2 + d):S5_PART * (3 + d)] = (-ei).astype(BF16)


def _s5_main_kernel(u_ref, win_ref, m_ref, wout_ref, a16_ref, h0_ref, y_ref, hfin_ref, uo_s, x_s, hs_s):
    s = pl.program_id(1)
    nq = S5_PART // 128

    @pl.when(s == 0)
    def _():
        for t in range(S5_CHUNK):
            uo_s[:, 128 * t:128 * (t + 1)] = u_ref[pl.ds(t, S5_ROWS, stride=S5_CHUNK), :].astype(BF16)

    @pl.when(s < S5_NB)
    def _():
        x = jnp.dot(uo_s[...], win_ref[0], preferred_element_type=F32)
        for q in range(nq):
            x_s[s * nq + q] = x[:, 128 * q:128 * (q + 1)]

    @pl.when(s == S5_NB - 1)
    def _():
        def run(row0, nb, nc, h):
            for c in range(nc):
                for d in range(2):
                    cc = c if d == 0 else nc - 1 - c
                    rows = pl.ds(row0 + cc, nb, stride=nc)
                    for q in range(nq):
                        kr, ki = d * nq + q, (2 + d) * nq + q
                        hr, hi = h[d][0][q], h[d][1][q]
                        hs_s[kr, rows, :] = hr
                        hs_s[ki, rows, :] = hi
                        ar = a16_ref[0, :, 128 * kr:128 * (kr + 1)]
                        ai = a16_ref[0, :, 128 * ki:128 * (ki + 1)]
                        h[d][0][q] = ar * hr - ai * hi + x_s[kr, rows, :]
                        h[d][1][q] = ar * hi + ai * hr + x_s[ki, rows, :]
            return h

        zero = jnp.zeros((BATCH, 128), F32)
        fin = run(0, BATCH, SEQ // S5_CHUNK, [[[zero] * nq, [zero] * nq] for _ in range(2)])
        for d in range(2):
            for ri in range(2):
                for q in range(nq):
                    k = (2 * ri + d) * nq + q
                    hfin_ref[0, :, 128 * k:128 * (k + 1)] = fin[d][ri][q]
        h0 = [[[h0_ref[0, :, 128 * ((2 * ri + d) * nq + q):128 * ((2 * ri + d) * nq + q + 1)] for q in range(nq)]
               for ri in range(2)] for d in range(2)]
        run(S5_ROWS_CTX, DEC_BATCH, DEC_SEQ // S5_CHUNK, h0)

    @pl.when(s >= S5_NB)
    def _():
        hs = jnp.concatenate([hs_s[k] for k in range(4 * nq)], axis=1).astype(BF16)
        y = (jnp.dot(uo_s[...], m_ref[0], preferred_element_type=F32)
             + lax.dot_general(hs, wout_ref[0], (((1,), (1,)), ((), ())), preferred_element_type=F32))
        for q in range(nq):
            t = (s - S5_NB) * nq + q
            y_ref[pl.ds(t, S5_ROWS, stride=S5_CHUNK), :] = y[:, 128 * q:128 * (q + 1)]


def _s5_octets(t, lanes):
    return t.reshape(2, S5_NOCT, 1, S5_OCT * lanes).transpose(1, 0, 2, 3)


def _s5_blockdiag(t):
    a, n = t.shape[2], t.shape[3]
    t = t.reshape(2, S5_NOCT, S5_OCT, a, n)
    bd = jnp.einsum('dogan,gh->dogahn', t, jnp.eye(S5_OCT, dtype=t.dtype))
    return bd.reshape(2, S5_NOCT, S5_OCT * a, S5_OCT * n).transpose(1, 0, 2, 3)


def _s5_state_to_lanes(h_re, h_im):
    b = h_re.shape[0]
    parts = jnp.stack([h_re[:, 0], h_re[:, 1], h_im[:, 0], h_im[:, 1]], axis=1)
    parts = parts.reshape(b, 4, S5_NOCT, S5_PART).transpose(2, 0, 1, 3)
    return parts.reshape(S5_NOCT, b, S5_SW)


def _s5_state_from_lanes(h):
    b = h.shape[1]
    parts = h.reshape(S5_NOCT, b, 4, S5_OCT, S5_STATE).transpose(1, 2, 0, 3, 4).reshape(b, 4, S5_GROUPS, S5_STATE)
    return parts[:, 0:2], parts[:, 2:4]


def _s5_scan(u, h0_re, h0_im, lam_re, lam_im, log_dt, b_re, b_im, c_re, c_im):
    ng, n = S5_GROUPS, S5_STATE
    rows = 2 * ng
    disc = pl.pallas_call(
        _s5_disc_kernel,
        out_shape=[jax.ShapeDtypeStruct((rows, n), F32)] * 4,
        name="s5_disc",
    )(lam_re.reshape(rows, n), lam_im.reshape(rows, n), log_dt.reshape(rows, 1))
    ar, ai, zr, zi = [_s5_octets(t.reshape(2, ng, n), n) for t in disc]
    bt = lambda t: _s5_blockdiag(t.transpose(0, 1, 3, 2))
    vec = pl.BlockSpec((1, 2, 1, S5_PART), lambda o, t: (o, 0, 0, 0))
    mat = pl.BlockSpec((1, 2, 128, S5_PART), lambda o, t: (o, 0, 0, 0))
    rowblk = lambda w: pl.BlockSpec((1, 128, w), lambda o, t: (o, t, 0))
    m, win, wout, a16 = pl.pallas_call(
        _s5_prep_kernel,
        grid=(S5_NOCT, S5_CHUNK),
        in_specs=[vec, vec, vec, vec, mat, mat, mat, mat],
        out_specs=[rowblk(S5_K), rowblk(S5_SW), rowblk(S5_SW), pl.BlockSpec((1, 1, S5_SW), lambda o, t: (o, 0, 0))],
        out_shape=[jax.ShapeDtypeStruct((S5_NOCT, S5_K, S5_K), BF16), jax.ShapeDtypeStruct((S5_NOCT, S5_K, S5_SW), BF16),
                   jax.ShapeDtypeStruct((S5_NOCT, S5_K, S5_SW), BF16), jax.ShapeDtypeStruct((S5_NOCT, 1, S5_SW), F32)],
        scratch_shapes=[pltpu.VMEM((2, 2, 24, S5_PART), F32), pltpu.VMEM((2, 2, S5_K, S5_PART), BF16),
                        pltpu.VMEM(((2 * S5_CHUNK - 1) * 128, 128), F32)],
        compiler_params=_cparams("arbitrary", "arbitrary"),
        name="s5_prep",
    )(ar, ai, zr, zi, bt(b_re), bt(b_im), _s5_blockdiag(c_re), _s5_blockdiag(c_im))
    nb = S5_NB
    y, hfin = pl.pallas_call(
        _s5_main_kernel,
        grid=(S5_NOCT, 2 * nb),
        in_specs=[pl.BlockSpec((N_TOK, 128), lambda o, s: (0, o)),
                  pl.BlockSpec((1, S5_K, S5_PART), lambda o, s: (o, 0, jnp.minimum(s, nb - 1))),
                  pl.BlockSpec((1, S5_K, S5_PART), lambda o, s: (o, 0, jnp.maximum(s - nb, 0))),
                  pl.BlockSpec((1, S5_PART, S5_SW), lambda o, s: (o, jnp.maximum(s - nb, 0), 0)),
                  pl.BlockSpec((1, 1, S5_SW), lambda o, s: (o, 0, 0)),
                  pl.BlockSpec((1, DEC_BATCH, S5_SW), lambda o, s: (o, 0, 0))],
        out_specs=[pl.BlockSpec((N_TOK, 128), lambda o, s: (0, o)),
                   pl.BlockSpec((1, BATCH, S5_SW), lambda o, s: (o, 0, 0))],
        out_shape=[jax.ShapeDtypeStruct((N_TOK, S5_WIDTH), F32), jax.ShapeDtypeStruct((S5_NOCT, BATCH, S5_SW), F32)],
        scratch_shapes=[pltpu.VMEM((S5_ROWS, S5_K), BF16), pltpu.VMEM((S5_SW // 128, S5_ROWS, 128), F32),
                        pltpu.VMEM((S5_SW // 128, S5_ROWS, 128), F32)],
        compiler_params=_cparams("arbitrary", "arbitrary"),
        name="s5_main",
    )(u, win, m, wout, a16, _s5_state_to_lanes(h0_re, h0_im))
    fin_re, fin_im = _s5_state_from_lanes(hfin)
    return y, fin_re, fin_im


def _ctx_or_lat(ctx_ref, lat_ref):
    return jnp.where(pl.program_id(0) < N_CTX_BLK, ctx_ref[...], lat_ref[...])


def _split_specs(width):
    return [pl.BlockSpec((ROW_BLK, width), lambda i: (jnp.minimum(i, N_CTX_BLK - 1), 0)),
            pl.BlockSpec((ROW_BLK, width), lambda i: (jnp.maximum(i - N_CTX_BLK, 0), 0))]


def _ab_out_kernel(x_ref, oac_ref, oal_ref, y_ref, u_ref, d_ref, wglu_ref, wout_ref, gt_ref, o_ref, wglu_s, wa_s, wb_s):
    @pl.when(pl.program_id(0) == 0)
    def _():
        wglu_s[...] = wglu_ref[...].astype(BF16)
        wa_s[...] = wout_ref[0:A_WIDTH, :].astype(BF16)
        wb_s[...] = wout_ref[A_WIDTH:, :].astype(BF16)

    g = jax.nn.gelu(y_ref[...] + d_ref[...] * u_ref[...])
    ob = g * jax.nn.sigmoid(jnp.dot(g.astype(BF16), wglu_s[...], preferred_element_type=F32))
    out = (jnp.dot(_ctx_or_lat(oac_ref, oal_ref).astype(BF16), wa_s[...], preferred_element_type=F32)
           + jnp.dot(ob.astype(BF16), wb_s[...], preferred_element_type=F32))
    o_ref[...] = x_ref[...] + gt_ref[0] * out


def _ab_out(x, oa, y, u, d_skip, w_glu, w_out, mod, layer):
    row = lambda n: pl.BlockSpec((ROW_BLK, n), lambda i: (i, 0))
    full = lambda a, b: pl.BlockSpec((a, b), lambda i: (0, 0))
    return pl.pallas_call(
        _ab_out_kernel,
        grid=(N_ROW_BLK,),
        in_specs=[row(D_MODEL)] + _split_specs(A_WIDTH) + [row(S5_WIDTH), row(S5_WIDTH), full(1, S5_WIDTH),
                  full(S5_WIDTH, S5_WIDTH), full(A_WIDTH + S5_WIDTH, D_MODEL), _mod_spec(layer, 2)],
        out_specs=row(D_MODEL),
        out_shape=jax.ShapeDtypeStruct((N_TOK, D_MODEL), F32),
        scratch_shapes=[pltpu.VMEM((S5_WIDTH, S5_WIDTH), BF16), pltpu.VMEM((A_WIDTH, D_MODEL), BF16),
                        pltpu.VMEM((S5_WIDTH, D_MODEL), BF16)],
        compiler_params=_cparams("arbitrary"),
        name="ab_out",
    )(x, oa[0], oa[1], y, u, d_skip.reshape(1, S5_WIDTH), w_glu, w_out, mod)


CD_HG0 = MLA_Q_RANK + MLA_KV_RANK + MLA_ROPE
KPE_LANES = 128


def _cd_in_kernel(x_ref, g_ref, sc_ref, sh_ref, w_ref, gq_ref, wqu_ref, gkv_ref, wkvu_ref, cos_ref, sa_ref, sb_ref,
                  qn_ref, qr_ref, ckv_ref, kpe_ref, kn_ref, vm_ref, hq_ref, hff_ref, hfb_ref, hi_ref, hg_ref,
                  wcq_s, wckv_s, wkpe_s, whg_s, wqu_s, wkvu_s):
    @pl.when(pl.program_id(0) == 0)
    def _():
        wcq_s[...] = w_ref[:, 0:MLA_Q_RANK].astype(BF16)
        wckv_s[...] = w_ref[:, MLA_Q_RANK:MLA_Q_RANK + MLA_KV_RANK].astype(BF16)
        kp = w_ref[:, MLA_Q_RANK + MLA_KV_RANK:CD_HG0].astype(BF16)
        wkpe_s[...] = jnp.concatenate([kp] * (KPE_LANES // MLA_ROPE), axis=1)
        whg_s[...] = w_ref[:, CD_HG0:].astype(BF16)
        wqu_s[...] = wqu_ref[...].astype(BF16)
        wkvu_s[...] = wkvu_ref[...].astype(BF16)

    h = _norm_mod(x_ref[...], g_ref[...], sc_ref[0], sh_ref[0]).astype(BF16)
    cos, sa, sb = cos_ref[...], sa_ref[...], sb_ref[...]
    nope = MLA_HEADS * MLA_NOPE
    cq = _rms(jnp.dot(h, wcq_s[...], preferred_element_type=F32)) * gq_ref[...]
    qq = jnp.dot(cq.astype(BF16), wqu_s[...], preferred_element_type=F32)
    qn_ref[...] = qq[:, 0:nope]
    qr_ref[...] = _apply_rope(qq[:, nope:], cos, sa, sb, MLA_ROPE // 4)
    ckv = _rms(jnp.dot(h, wckv_s[...], preferred_element_type=F32)) * gkv_ref[...]
    ckv_ref[...] = ckv
    kv = jnp.dot(ckv.astype(BF16), wkvu_s[...], preferred_element_type=F32)
    kn_ref[...] = kv[:, 0:nope]
    vm_ref[...] = kv[:, nope:]
    kpe_ref[...] = _apply_rope(jnp.dot(h, wkpe_s[...], preferred_element_type=F32), cos, sa, sb, MLA_ROPE // 4)
    hh = jnp.dot(h, whg_s[...], preferred_element_type=F32)
    for n, ref in enumerate((hq_ref, hff_ref, hfb_ref, hi_ref, hg_ref)):
        ref[...] = hh[:, HG_KD * n:HG_KD * (n + 1)]


def _mla_split_heads(w, a):
    k, n = w.shape
    w = w.reshape(k, MLA_HEADS, n // MLA_HEADS)
    return jnp.concatenate([w[:, :, :a].reshape(k, -1), w[:, :, a:].reshape(k, -1)], axis=1)


def _cd_in(x, mod, g, w, g_q, w_q_up, g_kv, w_kv_up, layer):
    cos, sa, sb = _rope_tables(MLA_ROPE)
    d_in = w.shape[1]
    row = lambda n: pl.BlockSpec((ROW_BLK, n), lambda i: (i, 0))
    full = lambda a, b: pl.BlockSpec((a, b), lambda i: (0, 0))
    rope = pl.BlockSpec((ROW_BLK, 128), lambda i: (_rope_blk(i), 0))
    widths = [MLA_HEADS * MLA_NOPE, MLA_HEADS * MLA_ROPE, MLA_KV_RANK, KPE_LANES, MLA_HEADS * MLA_NOPE, MLA_WIDTH] + [HG_KD] * 5
    nq = MLA_HEADS * (MLA_NOPE + MLA_ROPE)
    nkv = MLA_HEADS * (MLA_NOPE + MLA_V)
    return pl.pallas_call(
        _cd_in_kernel,
        grid=(N_ROW_BLK,),
        in_specs=[row(D_MODEL), full(1, D_MODEL), _mod_spec(layer, 1), _mod_spec(layer, 0), full(D_MODEL, d_in),
                  full(1, MLA_Q_RANK), full(MLA_Q_RANK, nq), full(1, MLA_KV_RANK), full(MLA_KV_RANK, nkv), rope, rope, rope],
        out_specs=[row(n) for n in widths],
        out_shape=[jax.ShapeDtypeStruct((N_TOK, n), F32) for n in widths],
        scratch_shapes=[pltpu.VMEM((D_MODEL, MLA_Q_RANK), BF16), pltpu.VMEM((D_MODEL, MLA_KV_RANK), BF16),
                        pltpu.VMEM((D_MODEL, KPE_LANES), BF16), pltpu.VMEM((D_MODEL, 5 * HG_KD), BF16),
                        pltpu.VMEM((MLA_Q_RANK, nq), BF16), pltpu.VMEM((MLA_KV_RANK, nkv), BF16)],
        compiler_params=_cparams("arbitrary"),
        name="cd_in",
    )(x, g.reshape(1, D_MODEL), mod, mod, w, g_q.reshape(1, -1), _mla_split_heads(w_q_up, MLA_NOPE),
      g_kv.reshape(1, -1), _mla_split_heads(w_kv_up, MLA_NOPE), cos, sa, sb)


def _mm_kernel(a_ref, w_ref, *o_refs):
    r = _dot(a_ref[...], w_ref[...])
    off = 0
    for o in o_refs:
        o[...] = r[:, off:off + o.shape[1]]
        off += o.shape[1]


def _mla_cache_kv(cckv, w_kv_up):
    n = MLA_HEADS * MLA_NOPE
    rows = cckv.shape[0]
    return pl.pallas_call(
        _mm_kernel,
        grid=(rows // PAST_LEN,),
        in_specs=[pl.BlockSpec((PAST_LEN, MLA_KV_RANK), lambda i: (i, 0)),
                  pl.BlockSpec((MLA_KV_RANK, 2 * n), lambda i: (0, 0))],
        out_specs=[pl.BlockSpec((PAST_LEN, n), lambda i: (i, 0))] * 2,
        out_shape=[jax.ShapeDtypeStruct((rows, n), F32)] * 2,
        compiler_params=_cparams("arbitrary"),
        name="mla_cache_kv",
    )(cckv, _mla_split_heads(w_kv_up, MLA_NOPE))


def _mla_heads(qn, qr, keys, o_ref):
    scale = (MLA_NOPE + MLA_ROPE) ** -0.5
    outs = []
    for h in range(MLA_HEADS):
        a = slice(MLA_NOPE * h, MLA_NOPE * (h + 1))
        r = slice(MLA_ROPE * h, MLA_ROPE * (h + 1))
        s_list = [(_dot_nt(qn[:, a], kn[:, a]) + _dot_nt(qr[:, r], kp[:, 0:MLA_ROPE])) * scale for kn, kp, _ in keys]
        outs.append(_softmax_pv(s_list, [v[:, a] for _, _, v in keys], None))
    o_ref[...] = jnp.concatenate(outs, axis=1)


def _mla_ctx_kernel(qn_ref, qr_ref, kn_ref, kp_ref, v_ref, o_ref):
    _mla_heads(qn_ref[...], qr_ref[...], [(kn_ref[...], kp_ref[...], v_ref[...])], o_ref)


def _mla_lat_kernel(qn_ref, qr_ref, kn_ref, kp_ref, v_ref, kcn_ref, kcp_ref, vc_ref, o_ref):
    _mla_heads(qn_ref[...], qr_ref[...],
               [(kcn_ref[...], kcp_ref[0], vc_ref[...]), (kn_ref[...], kp_ref[...], v_ref[...])], o_ref)


def _attention_mla(qn, qr, kn, kpe, vm, kcn, kcpe, vc):
    n = MLA_HEADS * MLA_NOPE
    nr = MLA_HEADS * MLA_ROPE
    blk = lambda w: pl.BlockSpec((SEQ, w), lambda b: (b, 0))
    o = pl.pallas_call(
        _mla_ctx_kernel,
        grid=(BATCH,),
        in_specs=[blk(n), blk(nr), blk(n), blk(KPE_LANES), blk(MLA_WIDTH)],
        out_specs=blk(MLA_WIDTH),
        out_shape=jax.ShapeDtypeStruct((N_CTX_TOK, MLA_WIDTH), F32),
        compiler_params=_cparams("arbitrary"),
        name="mla_ctx",
    )(qn, qr, kn, kpe, vm)
    nq = DEC_SEQ // ROW_BLK
    qblk = lambda w: pl.BlockSpec((ROW_BLK, w), lambda b, i: (N_CTX_BLK + b * nq + i, 0))
    seq = lambda w: pl.BlockSpec((DEC_SEQ, w), lambda b, i: (N_CTX_TOK // DEC_SEQ + b, 0))
    past = lambda w: pl.BlockSpec((PAST_LEN, w), lambda b, i: (b, 0))
    o_lat = pl.pallas_call(
        _mla_lat_kernel,
        grid=(DEC_BATCH, nq),
        in_specs=[qblk(n), qblk(nr), seq(n), seq(KPE_LANES), seq(MLA_WIDTH), past(n),
                  pl.BlockSpec((1, PAST_LEN, MLA_ROPE), lambda b, i: (b, 0, 0)), past(MLA_WIDTH)],
        out_specs=pl.BlockSpec((ROW_BLK, MLA_WIDTH), lambda b, i: (b * nq + i, 0)),
        out_shape=jax.ShapeDtypeStruct((N_LAT_TOK, MLA_WIDTH), F32),
        compiler_params=_cparams("arbitrary", "arbitrary"),
        name="mla_lat",
    )(qn, qr, kn, kpe, vm, kcn, kcpe, vc)
    return o, o_lat


HG_TILE = 128
HG_NC = HG_TILE // HG_CHUNK
HG_HALF = 256


HG_SLABS = HG_KD // 128


def _hg_put(ref, x):
    for j in range(HG_SLABS):
        ref[j] = x[:, 128 * j:128 * (j + 1)]


def _hg_get(ref, r, l):
    return jnp.concatenate([ref[2 * l, r, :], ref[2 * l + 1, r, :]], axis=1)


def _hg_plane(ref, p):
    return jnp.concatenate([ref[j, pl.ds(p, HG_NC, stride=HG_CHUNK), :] for j in range(HG_SLABS)], axis=1)


def _hg_put_plane(ref, p, x):
    for j in range(HG_SLABS):
        ref[j, pl.ds(p, HG_NC, stride=HG_CHUNK), :] = x[:, 128 * j:128 * (j + 1)]


def _hg_direction(hq_ref, hf_ref, hi_ref, o_ref, lb, st_ref, q_s, f_s, kk_s, v_s, qt_s, kt_s, o_s, sign,
                  ones_bd, diag_mask):
    hq = hq_ref[...]
    _hg_put(q_s, hq * jax.nn.sigmoid(hq))
    f = lb + (1.0 - lb) * jax.nn.sigmoid(hf_ref[...])
    _hg_put(f_s, f)
    _hg_put(kk_s, 1.0 - f)
    _hg_put(v_s, hi_ref[...])
    pos = (lambda i: i) if sign > 0 else (lambda i: HG_CHUNK - 1 - i)
    q = [_hg_plane(q_s, pos(i)) for i in range(HG_CHUNK)]
    fp = [_hg_plane(f_s, pos(i)) for i in range(HG_CHUNK)]
    kk = [_hg_plane(kk_s, pos(i)) for i in range(HG_CHUNK)]
    v = [_hg_plane(v_s, pos(i)) for i in range(HG_CHUNK)]

    def expand(planes):
        pb = jnp.concatenate(planes, axis=0).astype(BF16)
        e = jnp.concatenate([jnp.dot(pb[:, 0:HG_HALF], ones_bd, preferred_element_type=F32),
                             jnp.dot(pb[:, HG_HALF:], ones_bd, preferred_element_type=F32)], axis=1)
        return [e[HG_NC * n:HG_NC * (n + 1)] for n in range(len(planes))]

    o = [e * v[i] for i, e in enumerate(expand([q[i] * kk[i] for i in range(HG_CHUNK)]))]
    dec = list(fp)
    for d in range(1, HG_CHUNK):
        if d > 1:
            for i in range(d, HG_CHUNK):
                dec[i] = dec[i] * fp[i - d + 1]
        es = expand([q[i] * kk[i - d] * dec[i] for i in range(d, HG_CHUNK)])
        for n, i in enumerate(range(d, HG_CHUNK)):
            o[i] = o[i] + es[n] * v[i - d]
    for i in range(HG_CHUNK):
        _hg_put_plane(o_s, pos(i), o[i])

    incl = [fp[0]]
    for i in range(1, HG_CHUNK):
        incl.append(incl[-1] * fp[i])
    excl = [None] * HG_CHUNK
    excl[HG_CHUNK - 1] = jnp.ones_like(fp[0])
    for i in range(HG_CHUNK - 2, -1, -1):
        excl[i] = excl[i + 1] * fp[i + 1]
    for i in range(HG_CHUNK):
        _hg_put_plane(qt_s, pos(i), q[i] * incl[i])
        _hg_put_plane(kt_s, pos(i), kk[i] * excl[i])
    whole = incl[HG_CHUNK - 1]
    for cc in range(HG_NC):
        c = cc if sign > 0 else HG_NC - 1 - cc
        r = slice(HG_CHUNK * c, HG_CHUNK * (c + 1))
        for g in range(2):
            l = slice(HG_HALF * g, HG_HALF * (g + 1))
            st = st_ref[g]
            o_ref[r, l] = _hg_get(o_s, r, g) + _dot_nt(_hg_get(qt_s, r, g), st)
            ds = lax.dot_general(hi_ref[r, l].astype(BF16), _hg_get(kt_s, r, g).astype(BF16),
                                 (((0,), (0,)), ((), ())), preferred_element_type=F32)
            st_ref[g] = st * whole[c:c + 1, l] + jnp.where(diag_mask, ds, 0.0)


def _hg_lower_bound(lb_ref, d, layer):
    raw = [lb_ref[d * DEPTH + m:d * DEPTH + m + 1, :] for m in range(DEPTH)]
    mx = functools.reduce(jnp.maximum, raw)
    e = [jnp.exp(r - mx) for r in raw]
    tot = functools.reduce(jnp.add, e)
    return functools.reduce(jnp.add, e[1:layer + 1], jnp.zeros_like(tot)) / tot


def _hgrn_kernel(nt, layer, hqf_ref, hff_ref, hif_ref, hqb_ref, hfb_ref, hib_ref, lb_ref, s0_ref,
                 of_ref, ob_ref, sfin_ref, st_s, q_s, f_s, kk_s, v_s, qt_s, kt_s, o_s):
    i = pl.program_id(1)

    @pl.when(i == 0)
    def _():
        st_s[...] = s0_ref[0]

    ri = lax.broadcasted_iota(jnp.int32, (HG_HALF, HG_HALF), 0) // HG_KEY
    ci = lax.broadcasted_iota(jnp.int32, (HG_HALF, HG_HALF), 1) // HG_KEY
    diag_mask = ri == ci
    ones_bd = jnp.where(diag_mask, 1.0, 0.0).astype(BF16)
    _hg_direction(hqf_ref, hff_ref, hif_ref, of_ref, _hg_lower_bound(lb_ref, 0, layer), st_s.at[0],
                  q_s, f_s, kk_s, v_s, qt_s, kt_s, o_s, 1, ones_bd, diag_mask)
    _hg_direction(hqb_ref, hfb_ref, hib_ref, ob_ref, _hg_lower_bound(lb_ref, 1, layer), st_s.at[1],
                  q_s, f_s, kk_s, v_s, qt_s, kt_s, o_s, -1, ones_bd, diag_mask)

    @pl.when(i == nt - 1)
    def _():
        sfin_ref[0] = st_s[...]


def _hg_state_to_blocks(s):
    b = s.shape[0]
    st = s.transpose(0, 1, 2, 4, 3).reshape(b, 2, 2, 4, HG_VAL, HG_KEY)
    eye = jnp.eye(4, dtype=s.dtype)
    return jnp.einsum('bdgivk,ij->bdgivjk', st, eye).reshape(b, 2, 2, HG_HALF, HG_HALF)


def _hg_state_from_blocks(st):
    b = st.shape[0]
    st = st.reshape(b, 2, 2, 4, HG_VAL, 4, HG_KEY)
    diag = jnp.stack([st[:, :, :, i, :, i, :] for i in range(4)], axis=3)
    return diag.reshape(b, 2, HG_HEADS, HG_VAL, HG_KEY).transpose(0, 1, 2, 4, 3)


def _hgrn_scan(hq, hff, hfb, hi, lb, s0, layer, row0, nseq, seqlen):
    nt = seqlen // HG_TILE
    base = row0 // HG_TILE
    fwd = pl.BlockSpec((HG_TILE, HG_KD), lambda b, i: (base + b * nt + i, 0))
    bwd = pl.BlockSpec((HG_TILE, HG_KD), lambda b, i: (base + b * nt + nt - 1 - i, 0))
    ofw = pl.BlockSpec((HG_TILE, HG_KD), lambda b, i: (b * nt + i, 0))
    obw = pl.BlockSpec((HG_TILE, HG_KD), lambda b, i: (b * nt + nt - 1 - i, 0))
    st = pl.BlockSpec((1, 2, 2, HG_HALF, HG_HALF), lambda b, i: (b, 0, 0, 0, 0))
    tile = lambda: pltpu.VMEM((HG_SLABS, HG_TILE, 128), F32)
    return pl.pallas_call(
        functools.partial(_hgrn_kernel, nt, layer),
        grid=(nseq, nt),
        in_specs=[fwd, fwd, fwd, bwd, bwd, bwd, pl.BlockSpec((2 * DEPTH, HG_KD), lambda b, i: (0, 0)), st],
        out_specs=[ofw, obw, st],
        out_shape=[jax.ShapeDtypeStruct((nseq * seqlen, HG_WIDTH), F32)] * 2
        + [jax.ShapeDtypeStruct((nseq, 2, 2, HG_HALF, HG_HALF), F32)],
        scratch_shapes=[pltpu.VMEM((2, 2, HG_HALF, HG_HALF), F32)] + [tile() for _ in range(7)],
        compiler_params=_cparams("arbitrary", "arbitrary"),
        name="hgrn_scan",
    )(hq, hff, hi, hq, hfb, hi, lb.reshape(2 * DEPTH, HG_KD), s0)


def _cd_out_kernel(x_ref, occ_ref, ocl_ref, ofc_ref, ofl_ref, obc_ref, obl_ref, hg_ref, go_ref, wout_ref, gt_ref,
                   o_ref, wa_s, wb_s):
    @pl.when(pl.program_id(0) == 0)
    def _():
        wa_s[...] = wout_ref[0:MLA_WIDTH, :].astype(BF16)
        wb_s[...] = wout_ref[MLA_WIDTH:, :].astype(BF16)

    ri = lax.broadcasted_iota(jnp.int32, (HG_HALF, HG_HALF), 0) // HG_VAL
    ci = lax.broadcasted_iota(jnp.int32, (HG_HALF, HG_HALF), 1) // HG_VAL
    ones_bd = jnp.where(ri == ci, 1.0, 0.0).astype(BF16)
    o = _ctx_or_lat(ofc_ref, ofl_ref) + _ctx_or_lat(obc_ref, obl_ref)
    sq = o * o
    hi = sq.astype(BF16)
    lo = (sq - hi.astype(F32)).astype(BF16)
    ms = jnp.concatenate(
        [jnp.dot(hi[:, l], ones_bd, preferred_element_type=F32) + jnp.dot(lo[:, l], ones_bd, preferred_element_type=F32)
         for l in (slice(0, HG_HALF), slice(HG_HALF, 2 * HG_HALF))], axis=1) * (1.0 / HG_VAL)
    hg = hg_ref[...]
    od = o * lax.rsqrt(ms + EPS) * go_ref[...] * (hg * jax.nn.sigmoid(hg))
    out = (jnp.dot(_ctx_or_lat(occ_ref, ocl_ref).astype(BF16), wa_s[...], preferred_element_type=F32)
           + jnp.dot(od.astype(BF16), wb_s[...], preferred_element_type=F32))
    o_ref[...] = x_ref[...] + gt_ref[0] * out


def _cd_out(x, oc, of, ob, hg, g_o, w_out, mod, layer):
    row = lambda n: pl.BlockSpec((ROW_BLK, n), lambda i: (i, 0))
    full = lambda a, b: pl.BlockSpec((a, b), lambda i: (0, 0))
    return pl.pallas_call(
        _cd_out_kernel,
        grid=(N_ROW_BLK,),
        in_specs=[row(D_MODEL)] + _split_specs(MLA_WIDTH) + _split_specs(HG_WIDTH) + _split_specs(HG_WIDTH)
        + [row(HG_WIDTH), full(1, HG_WIDTH), full(MLA_WIDTH + HG_WIDTH, D_MODEL), _mod_spec(layer, 2)],
        out_specs=row(D_MODEL),
        out_shape=jax.ShapeDtypeStruct((N_TOK, D_MODEL), F32),
        scratch_shapes=[pltpu.VMEM((MLA_WIDTH, D_MODEL), BF16), pltpu.VMEM((HG_WIDTH, D_MODEL), BF16)],
        compiler_params=_cparams("arbitrary"),
        name="cd_out",
    )(x, oc[0], oc[1], of[0], of[1], ob[0], ob[1], hg, jnp.tile(g_o, HG_HEADS).reshape(1, HG_WIDTH), w_out, mod)


N_PAIRS = 6
N_CLASSES = N_GROUPS * N_PAIRS
CLS_ROWS = 32
MOE_BM = 256
MOE_NBLK = N_TOK // MOE_BM + N_CLASSES
MOE_ROWS = MOE_NBLK * MOE_BM


def _moe_route_kernel(x_ref, g_ref, sc_ref, sh_ref, rw_ref, rb_ref, h_ref, ti_ref, tw_ref, cnt_ref, base_s):
    @pl.when(pl.program_id(0) == 0)
    def _():
        base_s[...] = jnp.zeros_like(base_s)

    h = _norm_mod(x_ref[...], g_ref[...], sc_ref[0], sh_ref[0])
    logits = lax.dot_general(rw_ref[...], h, (((1,), (1,)), ((), ())), precision=HI,
                             preferred_element_type=F32)
    aff = jax.nn.sigmoid(logits)
    sel = aff + rb_ref[...]
    s = [sel[e:e + 1, :] for e in range(N_EXPERTS)]
    a = [aff[e:e + 1, :] for e in range(N_EXPERTS)]
    gs = []
    for g in range(N_GROUPS):
        m = s[4 * g:4 * g + 4]
        pairs = [m[i] + m[j] for i in range(4) for j in range(i + 1, 4)]
        gs.append(functools.reduce(jnp.maximum, pairs))
    gmax = functools.reduce(jnp.maximum, gs)
    taken = jnp.zeros_like(gmax) > 1.0
    gsel = []
    for g in range(N_GROUPS):
        hit = (gs[g] == gmax) & jnp.logical_not(taken)
        gsel.append(hit)
        taken = taken | hit
    e_lo = jnp.zeros(gmax.shape, jnp.int32)
    e_hi = jnp.zeros(gmax.shape, jnp.int32)
    a_lo = jnp.zeros_like(gmax)
    a_hi = jnp.zeros_like(gmax)
    nsel = jnp.zeros(gmax.shape, jnp.int32)
    for g in range(N_GROUPS):
        for i in range(4):
            e = 4 * g + i
            beat = jnp.zeros(gmax.shape, jnp.int32)
            for j in range(4):
                if j != i:
                    o = 4 * g + j
                    beat = beat + jnp.where((s[o] > s[e]) | ((s[o] == s[e]) & (j < i)), 1, 0)
            pick = gsel[g] & (beat < 2)
            is_first = pick & (nsel == 0)
            is_second = pick & (nsel == 1)
            e_lo = jnp.where(is_first, e, e_lo)
            a_lo = jnp.where(is_first, a[e], a_lo)
            e_hi = jnp.where(is_second, e, e_hi)
            a_hi = jnp.where(is_second, a[e], a_hi)
            nsel = nsel + jnp.where(pick, 1, 0)
    grp = e_lo // EXPERTS_PER_GROUP
    lo = e_lo - grp * EXPERTS_PER_GROUP
    hi = e_hi - grp * EXPERTS_PER_GROUP
    cls = grp * N_PAIRS + ((lo * (7 - lo)) >> 1) + (hi - lo - 1)
    wsum = a_lo + a_hi
    w_lo, w_hi = a_lo / wsum, a_hi / wsum
    onehot = (lax.broadcasted_iota(jnp.int32, (CLS_ROWS, ROW_BLK), 0) == cls).astype(F32)
    tt = lax.broadcasted_iota(jnp.int32, (ROW_BLK, ROW_BLK), 0) < lax.broadcasted_iota(jnp.int32, (ROW_BLK, ROW_BLK), 1)
    before = _dot(onehot, jnp.where(tt, 1.0, 0.0))
    base = base_s[...]
    rank = jnp.sum(onehot * (before + base[:, 0:1]), axis=0, keepdims=True).astype(jnp.int32)
    base = base + jnp.sum(onehot, axis=1, keepdims=True)
    base_s[...] = base
    cnt_ref[...] = base.astype(jnp.int32)
    ti_ref[0] = jnp.concatenate([cls, rank, e_lo, e_hi, jnp.zeros((4, ROW_BLK), jnp.int32)], axis=0)
    tw_ref[0] = jnp.concatenate([w_lo, w_hi, jnp.zeros((6, ROW_BLK), F32)], axis=0)
    h_ref[...] = h


def _moe_route(x, mod, g, router_w, router_b, layer):
    row = lambda n: pl.BlockSpec((ROW_BLK, n), lambda i: (i, 0))
    full = lambda a, b: pl.BlockSpec((a, b), lambda i: (0, 0))
    info = pl.BlockSpec((1, 8, ROW_BLK), lambda i: (i, 0, 0))
    return pl.pallas_call(
        _moe_route_kernel,
        grid=(N_ROW_BLK,),
        in_specs=[row(D_MODEL), full(1, D_MODEL), _mod_spec(layer, 4), _mod_spec(layer, 3),
                  full(N_EXPERTS, D_MODEL), full(N_EXPERTS, 1)],
        out_specs=[row(D_MODEL), info, info, full(CLS_ROWS, 128)],
        out_shape=[jax.ShapeDtypeStruct((N_TOK, D_MODEL), F32),
                   jax.ShapeDtypeStruct((N_ROW_BLK, 8, ROW_BLK), jnp.int32),
                   jax.ShapeDtypeStruct((N_ROW_BLK, 8, ROW_BLK), F32),
                   jax.ShapeDtypeStruct((CLS_ROWS, 128), jnp.int32)],
        scratch_shapes=[pltpu.VMEM((CLS_ROWS, 128), F32)],
        compiler_params=_cparams("arbitrary"),
        name="moe_route",
    )(x, g.reshape(1, D_MODEL), mod, mod, router_w.T, router_b.reshape(N_EXPERTS, 1))


def _moe_gather_rows(perm_s, wlo_ref, whi_ref, h_ref, stage_s, wlo_s, whi_s, blk, row0, nrows):
    slot = blk % 2
    base = blk * MOE_BM + row0
    for r in range(nrows):
        src = perm_s[base + r]
        dst = pl.ds(row0 + r, 1)
        stage_s[slot, dst, :] = h_ref[pl.ds(src, 1), :]
        wlo_s[slot, dst, :] = jnp.full((1, 128), wlo_ref[src], F32)
        whi_s[slot, dst, :] = jnp.full((1, 128), whi_ref[src], F32)


def _moe_experts_kernel(layer, elo_ref, ehi_ref, nblk_ref, pos_ref, wlo_ref, whi_ref, h_ref, wgl_ref, wul_ref, wdl_ref,
                        wgh_ref, wuh_ref, wdh_ref, y_ref, perm_s, stage_s, wlo_s, whi_s, part_s):
    del layer, elo_ref, ehi_ref
    j, f = pl.program_id(0), pl.program_id(1)
    half = MOE_BM // 2

    @pl.when((j == 0) & (f == 0))
    def _():
        part_s[...] = jnp.zeros_like(part_s)
        def init(r, c):
            perm_s[r] = 0
            return c

        lax.fori_loop(0, MOE_ROWS + MOE_BM, init, 0, unroll=16)

        def build(t, c):
            perm_s[pos_ref[t]] = t
            return c

        lax.fori_loop(0, N_TOK, build, 0, unroll=16)
        _moe_gather_rows(perm_s, wlo_ref, whi_ref, h_ref, stage_s, wlo_s, whi_s, 0, 0, MOE_BM)

    @pl.when(j < nblk_ref[0])
    def _():
        slot = j % 2
        h = stage_s[slot].astype(BF16)
        _moe_gather_rows(perm_s, wlo_ref, whi_ref, h_ref, stage_s, wlo_s, whi_s, j + 1, f * half, half)
        acc = None
        for wg, wu, wd, gate_s in ((wgl_ref, wul_ref, wdl_ref, wlo_s), (wgh_ref, wuh_ref, wdh_ref, whi_s)):
            g = _dot(h, wg[0, 0])
            u = _dot(h, wu[0, 0])
            hid = g * jax.nn.sigmoid(g) * u * gate_s[slot, :, 0:1]
            y = _dot(hid, wd[0, 0])
            acc = y if acc is None else acc + y
        y_ref[...] = acc + jnp.where(f == 1, part_s[...], 0.0)
        part_s[...] = acc

    @pl.when((j >= nblk_ref[0]) & (f == 0))
    def _():
        y_ref[...] = jnp.zeros_like(y_ref)


def _moe_experts(h, blk_elo, blk_ehi, nblk, pos, w_lo, w_hi, w_gate, w_up, w_down, layer):
    fh = D_FF // 2

    def pick(j, f, nb):
        last = nb[0] - 1
        return jnp.minimum(j, last), jnp.where(j <= last, f ^ (j & 1), 1 ^ (last & 1))

    def wspec(shape, which, down):
        def imap(j, f, elo, ehi, nb, *_):
            jj, a = pick(j, f, nb)
            e = (elo, ehi)[which][jj]
            return (layer, e, a, 0) if down else (layer, e, 0, a)
        return pl.BlockSpec((1, 1) + shape, imap)

    grid_spec = pltpu.PrefetchScalarGridSpec(
        num_scalar_prefetch=6,
        grid=(MOE_NBLK, 2),
        in_specs=[pl.BlockSpec((N_TOK, D_MODEL), lambda j, f, *_: (0, 0), pipeline_mode=pl.Buffered(1)),
                  wspec((D_MODEL, fh), 0, False), wspec((D_MODEL, fh), 0, False), wspec((fh, D_MODEL), 0, True),
                  wspec((D_MODEL, fh), 1, False), wspec((D_MODEL, fh), 1, False), wspec((fh, D_MODEL), 1, True)],
        out_specs=pl.BlockSpec((MOE_BM, D_MODEL), lambda j, f, *_: (j, 0)),
        scratch_shapes=[pltpu.SMEM((MOE_ROWS + MOE_BM,), jnp.int32), pltpu.VMEM((2, MOE_BM, D_MODEL), F32),
                        pltpu.VMEM((2, MOE_BM, 128), F32), pltpu.VMEM((2, MOE_BM, 128), F32),
                        pltpu.VMEM((MOE_BM, D_MODEL), F32)],
    )
    return pl.pallas_call(
        functools.partial(_moe_experts_kernel, layer),
        grid_spec=grid_spec,
        out_shape=jax.ShapeDtypeStruct((MOE_ROWS, D_MODEL), F32),
        compiler_params=_cparams("arbitrary", "arbitrary"),
        name="moe_experts",
    )(blk_elo, blk_ehi, nblk, pos, w_lo, w_hi, h, w_gate, w_up, w_down, w_gate, w_up, w_down)


def _moe_combine_kernel(pos_ref, x_ref, y_ref, gt_ref, o_ref, stage_s):
    i = pl.program_id(1)

    def gather(r, c):
        stage_s[pl.ds(r, 1), :] = y_ref[pl.ds(pos_ref[i * ROW_BLK + r], 1), :]
        return c

    lax.fori_loop(0, ROW_BLK, gather, 0, unroll=8)
    o_ref[...] = x_ref[...] + gt_ref[0] * stage_s[...]


def _moe_combine(x, y_sorted, pos, mod, layer):
    half = D_MODEL // 2
    grid_spec = pltpu.PrefetchScalarGridSpec(
        num_scalar_prefetch=1,
        grid=(2, N_ROW_BLK),
        in_specs=[pl.BlockSpec((ROW_BLK, half), lambda c, i, pos: (i, c)),
                  pl.BlockSpec((MOE_ROWS, half), lambda c, i, pos: (0, c), pipeline_mode=pl.Buffered(1)),
                  pl.BlockSpec((1, 1, half), lambda c, i, pos: ((layer * MOD_ROWS + _mod_group(i)) * N_MOD + 5, 0, c))],
        out_specs=pl.BlockSpec((ROW_BLK, half), lambda c, i, pos: (i, c)),
        scratch_shapes=[pltpu.VMEM((ROW_BLK, half), F32)],
    )
    return pl.pallas_call(
        _moe_combine_kernel,
        grid_spec=grid_spec,
        out_shape=jax.ShapeDtypeStruct((N_TOK, D_MODEL), F32),
        compiler_params=_cparams("arbitrary", "arbitrary"),
        name="moe_combine",
    )(pos, x, y_sorted, mod)


def _moe(x, mod, g, router_w, router_b, w_gate, w_up, w_down, layer):
    h, info, winfo, counts = _moe_route(x, mod, g, router_w, router_b, layer)
    cls = info[:, 0, :].reshape(N_TOK)
    rank = info[:, 1, :].reshape(N_TOK)
    w_lo = winfo[:, 0, :].reshape(N_TOK)
    w_hi = winfo[:, 1, :].reshape(N_TOK)
    cnt = counts[:N_CLASSES, 0]
    nb = (cnt + MOE_BM - 1) // MOE_BM
    ends = jnp.cumsum(nb)
    starts = ends - nb
    pos = (starts * MOE_BM)[cls] + rank
    blk = jnp.arange(MOE_NBLK, dtype=jnp.int32)
    blk_cls = jnp.minimum(jnp.sum((blk[:, None] >= ends[None, :]).astype(jnp.int32), axis=1), N_CLASSES - 1)
    pair_lo = jnp.asarray([0, 0, 0, 1, 1, 2], jnp.int32)
    pair_hi = jnp.asarray([1, 2, 3, 2, 3, 3], jnp.int32)
    grp = blk_cls // N_PAIRS
    blk_elo = (grp * EXPERTS_PER_GROUP + pair_lo[blk_cls % N_PAIRS]).astype(jnp.int32)
    blk_ehi = (grp * EXPERTS_PER_GROUP + pair_hi[blk_cls % N_PAIRS]).astype(jnp.int32)
    nblk = ends[-1:].astype(jnp.int32)
    y_sorted = _moe_experts(h, blk_elo, blk_ehi, nblk, pos.astype(jnp.int32), w_lo, w_hi, w_gate, w_up, w_down, layer)
    return _moe_combine(x, y_sorted, pos.astype(jnp.int32), mod, layer)


def _final_norm_kernel(x_ref, g_ref, o_ref):
    o_ref[...] = _rms(x_ref[...]) * g_ref[...]


def _final_norm(x, g, row0, rows):
    base = row0 // ROW_BLK
    return pl.pallas_call(
        _final_norm_kernel,
        grid=(rows // ROW_BLK,),
        in_specs=[pl.BlockSpec((ROW_BLK, D_MODEL), lambda i: (base + i, 0)), pl.BlockSpec((1, D_MODEL), lambda i: (0, 0))],
        out_specs=pl.BlockSpec((ROW_BLK, D_MODEL), lambda i: (i, 0)),
        out_shape=jax.ShapeDtypeStruct((rows, D_MODEL), F32),
        compiler_params=_cparams("arbitrary"),
        name="final_norm",
    )(x, g.reshape(1, D_MODEL))


def kernel(x_prompt, x_sample, cache_attn_k, cache_attn_v, state_ssm_re, state_ssm_im, cache_mla_ckv, cache_mla_kpe,
           state_hgrn, c, c_ctx, w_mod, b_mod, g_mix, g_ffn, g_final, router_w, router_b, moe_w_gate, moe_w_up,
           moe_w_down, ab_w_in, ab_sink, s5_lam_re, s5_lam_im, s5_log_dt, s5_b_re, s5_b_im, s5_c_re, s5_c_im, s5_d,
           s5_w_glu, ab_w_out, cd_w_in, mla_g_q, mla_w_q_up, mla_g_kv, mla_w_kv_up, hg_lower_bounds, hg_g_o, cd_w_out):
    x = jnp.concatenate([x_prompt.reshape(N_CTX_TOK, D_MODEL), x_sample.reshape(N_LAT_TOK, D_MODEL)], axis=0)
    cond = jnp.zeros((MOD_ROWS, D_MODEL), F32).at[0].set(c_ctx).at[1:1 + DEC_BATCH].set(c)
    mod = _modulation(cond, w_mod, b_mod)
    keep = ([], [], [], [], [], [], [])
    for l in range(DEPTH):
        j = l // 2
        if l % 2 == 0:
            q, k, v, u = _ab_in(x, mod, g_mix[l], ab_w_in[j], l)
            o_a = _attention_a(q, k, v, cache_attn_k[:, j].reshape(DEC_BATCH, PAST_LEN, A_KV_WIDTH),
                               cache_attn_v[:, j].reshape(DEC_BATCH, PAST_LEN, A_KV_WIDTH), ab_sink[j])
            y, fin_re, fin_im = _s5_scan(u, state_ssm_re[:, j], state_ssm_im[:, j], s5_lam_re[j], s5_lam_im[j],
                                         s5_log_dt[j], s5_b_re[j], s5_b_im[j], s5_c_re[j], s5_c_im[j])
            x = _ab_out(x, o_a, y, u, s5_d[j], s5_w_glu[j], ab_w_out[j], mod, l)
            keep[0].append(k[:N_CTX_TOK].reshape(BATCH, SEQ, A_KV_HEADS, HEAD_DIM))
            keep[1].append(v[:N_CTX_TOK].reshape(BATCH, SEQ, A_KV_HEADS, HEAD_DIM))
            keep[2].append(fin_re)
            keep[3].append(fin_im)
        else:
            qn, qr, ckv, kpe, kn, vm, hq, hff, hfb, hi, hg = _cd_in(
                x, mod, g_mix[l], cd_w_in[j], mla_g_q[j], mla_w_q_up[j], mla_g_kv[j], mla_w_kv_up[j], l)
            kcn, vc = _mla_cache_kv(cache_mla_ckv[:, j].reshape(DEC_BATCH * PAST_LEN, MLA_KV_RANK), mla_w_kv_up[j])
            o_c = _attention_mla(qn, qr, kn, kpe, vm, kcn, cache_mla_kpe[:, j], vc)
            s0_ctx = jnp.zeros((BATCH, 2, 2, HG_HALF, HG_HALF), F32)
            of_c, ob_c, s_fin = _hgrn_scan(hq, hff, hfb, hi, hg_lower_bounds, s0_ctx, l, 0, BATCH, SEQ)
            of_l, ob_l, _ = _hgrn_scan(hq, hff, hfb, hi, hg_lower_bounds, _hg_state_to_blocks(state_hgrn[:, j]), l,
                                       N_CTX_TOK, DEC_BATCH, DEC_SEQ)
            x = _cd_out(x, o_c, (of_c, of_l), (ob_c, ob_l), hg, hg_g_o[j], cd_w_out[j], mod, l)
            keep[4].append(ckv[:N_CTX_TOK].reshape(BATCH, SEQ, MLA_KV_RANK))
            keep[5].append(kpe[:N_CTX_TOK, :MLA_ROPE].reshape(BATCH, SEQ, MLA_ROPE))
            keep[6].append(_hg_state_from_blocks(s_fin))
        x = _moe(x, mod, g_ffn[l], router_w, router_b, moe_w_gate, moe_w_up, moe_w_down, l)
    y_ctx = _final_norm(x, g_final, 0, N_CTX_TOK)
    y_lat = _final_norm(x, g_final, N_CTX_TOK, N_LAT_TOK)
    return (y_ctx.reshape(BATCH, SEQ, D_MODEL), y_lat.reshape(DEC_BATCH, DEC_SEQ, D_MODEL),
            jnp.stack(keep[0], 1), jnp.stack(keep[1], 1), jnp.stack(keep[2], 1), jnp.stack(keep[3], 1),
            jnp.stack(keep[4], 1), jnp.stack(keep[5], 1), jnp.stack(keep[6], 1))
```

```python
import functools
import math

import numpy as np
import jax
import jax.numpy as jnp
from jax import lax
from jax.experimental import pallas as pl
from jax.experimental.pallas import tpu as pltpu

F32 = jnp.float32
BF16 = jnp.bfloat16

D_MODEL = 1024
BATCH = 16
SEQ = 256
DEPTH = 2
DEC_BATCH = 4
DEC_SEQ = 1024
PAST_LEN = 512
GRID_W = 64
N_MOD = 6
EPS = 1e-6
NEG_INF = -1e30
ROPE_BASE = 10000.0
HEAD_DIM = 64
A_HEADS = 8
A_KV_HEADS = 2
A_WINDOW = 128
A_WIDTH = A_HEADS * HEAD_DIM
A_KV_WIDTH = A_KV_HEADS * HEAD_DIM
S5_WIDTH = D_MODEL // 2
S5_GROUP = 16
S5_GROUPS = S5_WIDTH // S5_GROUP
S5_STATE = 64
MLA_HEADS = 8
MLA_Q_RANK = D_MODEL // 4
MLA_KV_RANK = D_MODEL // 8
MLA_NOPE = 64
MLA_ROPE = 32
MLA_V = 64
MLA_WIDTH = MLA_HEADS * MLA_V
HG_HEADS = 8
HG_KEY = 64
HG_VAL = 64
HG_KD = HG_HEADS * HG_KEY
HG_WIDTH = HG_HEADS * HG_VAL
HG_CHUNK = 16
N_EXPERTS = 16
N_GROUPS = 4
EXPERTS_PER_GROUP = N_EXPERTS // N_GROUPS
D_FF = D_MODEL // 2

N_CTX_TOK = BATCH * SEQ
N_LAT_TOK = DEC_BATCH * DEC_SEQ
N_TOK = N_CTX_TOK + N_LAT_TOK
ROW_BLK = 256
N_ROW_BLK = N_TOK // ROW_BLK
N_CTX_BLK = N_CTX_TOK // ROW_BLK
LAT_BLK_PER_SEQ = DEC_SEQ // ROW_BLK
MOD_ROWS = 8
VMEM_LIMIT = 56 * 1024 * 1024


def _cparams(*sem):
    return pltpu.CompilerParams(dimension_semantics=sem, vmem_limit_bytes=VMEM_LIMIT)


def _mod_group(i):
    return jnp.where(i < N_CTX_BLK, 0, 1 + (i - N_CTX_BLK) // LAT_BLK_PER_SEQ)


def _mod_spec(layer, which):
    return pl.BlockSpec((1, 1, D_MODEL), lambda i: ((layer * MOD_ROWS + _mod_group(i)) * N_MOD + which, 0, 0))


def _rope_blk(i):
    return jnp.where(i < N_CTX_BLK, 0, 1 + (i - N_CTX_BLK) % LAT_BLK_PER_SEQ)


def _rope_tables(rot_dim):
    n_freq = rot_dim // 4
    t = np.arange(DEC_SEQ)
    rows = (t // GRID_W).astype(np.float32)
    cols = (t % GRID_W).astype(np.float32)
    inv = (np.float32(ROPE_BASE) ** (-np.arange(n_freq, dtype=np.float32) / np.float32(n_freq))).astype(np.float32)
    ang_r = rows[:, None] * inv[None, :]
    ang_c = cols[:, None] * inv[None, :]
    ang = np.concatenate([ang_r, ang_r, ang_c, ang_c], axis=-1).astype(np.float32)
    reps = 128 // rot_dim
    cos = np.tile(np.cos(ang), (1, reps)).astype(np.float32)
    sin = np.tile(np.sin(ang), (1, reps)).astype(np.float32)
    lane = np.arange(128)
    first = (lane % (2 * n_freq)) < n_freq
    sin_a = np.where(first[None, :], -sin, 0.0).astype(np.float32)
    sin_b = np.where(first[None, :], 0.0, sin).astype(np.float32)
    ident = np.zeros((ROW_BLK, 128), np.float32)
    cos = np.concatenate([ident + 1.0, cos], axis=0)
    sin_a = np.concatenate([ident, sin_a], axis=0)
    sin_b = np.concatenate([ident, sin_b], axis=0)
    return jnp.asarray(cos), jnp.asarray(sin_a), jnp.asarray(sin_b)


def _apply_rope(x, cos, sin_a, sin_b, quarter):
    outs = []
    for j in range(x.shape[1] // 128):
        xt = x[:, 128 * j:128 * (j + 1)]
        up = pltpu.roll(xt, 128 - quarter, axis=1)
        dn = pltpu.roll(xt, quarter, axis=1)
        outs.append(xt * cos + up * sin_a + dn * sin_b)
    return outs[0] if len(outs) == 1 else jnp.concatenate(outs, axis=1)


def _rms(x):
    return x * lax.rsqrt(jnp.mean(x * x, axis=-1, keepdims=True) + EPS)


def _norm_mod(x, g, sc, sh):
    return _rms(x) * g * (1.0 + sc) + sh


def _dot(a, b):
    return jnp.dot(a.astype(BF16), b.astype(BF16), preferred_element_type=F32)


def _dot_nt(a, b):
    return lax.dot_general(a.astype(BF16), b.astype(BF16), (((1,), (1,)), ((), ())), preferred_element_type=F32)


def _mod_kernel(cond_ref, w_ref, b_ref, o_ref):
    c = cond_ref[...]
    s = c * jax.nn.sigmoid(c)
    o_ref[0] = _dot(s, w_ref[0]) + b_ref[0]


def _modulation(cond, w_mod, b_mod):
    nb = 1024
    out = pl.pallas_call(
        _mod_kernel,
        grid=(DEPTH, N_MOD * D_MODEL // nb),
        in_specs=[pl.BlockSpec((MOD_ROWS, D_MODEL), lambda l, n: (0, 0)),
                  pl.BlockSpec((1, D_MODEL, nb), lambda l, n: (l, 0, n)),
                  pl.BlockSpec((1, 1, nb), lambda l, n: (l, 0, n))],
        out_specs=pl.BlockSpec((1, MOD_ROWS, nb), lambda l, n: (l, 0, n)),
        out_shape=jax.ShapeDtypeStruct((DEPTH, MOD_ROWS, N_MOD * D_MODEL), F32),
        compiler_params=_cparams("arbitrary", "arbitrary"),
        name="modulation",
    )(cond, w_mod, b_mod.reshape(DEPTH, 1, N_MOD * D_MODEL))
    return out.reshape(DEPTH * MOD_ROWS * N_MOD, 1, D_MODEL)


def _ab_in_kernel(x_ref, g_ref, sc_ref, sh_ref, w_ref, cos_ref, sa_ref, sb_ref,
                  q_ref, k_ref, v_ref, u_ref, wq_s, wk_s, wv_s, wu_s):
    @pl.when(pl.program_id(0) == 0)
    def _():
        wq_s[...] = w_ref[:, 0:A_WIDTH].astype(BF16)
        wk_s[...] = w_ref[:, A_WIDTH:A_WIDTH + A_KV_WIDTH].astype(BF16)
        wv_s[...] = w_ref[:, A_WIDTH + A_KV_WIDTH:A_WIDTH + 2 * A_KV_WIDTH].astype(BF16)
        wu_s[...] = w_ref[:, A_WIDTH + 2 * A_KV_WIDTH:].astype(BF16)

    h = _norm_mod(x_ref[...], g_ref[...], sc_ref[0], sh_ref[0]).astype(BF16)
    cos, sa, sb = cos_ref[...], sa_ref[...], sb_ref[...]
    q = jnp.dot(h, wq_s[...], preferred_element_type=F32)
    q_ref[...] = _apply_rope(q, cos, sa, sb, HEAD_DIM // 4)
    k = jnp.dot(h, wk_s[...], preferred_element_type=F32)
    k_ref[...] = _apply_rope(k, cos, sa, sb, HEAD_DIM // 4)
    v_ref[...] = jnp.dot(h, wv_s[...], preferred_element_type=F32)
    u_ref[...] = jnp.dot(h, wu_s[...], preferred_element_type=F32)


def _ab_in(x, mod, g, w, layer):
    cos, sa, sb = _rope_tables(HEAD_DIM)
    d_in = w.shape[1]
    row = lambda n: pl.BlockSpec((ROW_BLK, n), lambda i: (i, 0))
    rope = pl.BlockSpec((ROW_BLK, 128), lambda i: (_rope_blk(i), 0))
    return pl.pallas_call(
        _ab_in_kernel,
        grid=(N_ROW_BLK,),
        in_specs=[row(D_MODEL), pl.BlockSpec((1, D_MODEL), lambda i: (0, 0)),
                  _mod_spec(layer, 1), _mod_spec(layer, 0),
                  pl.BlockSpec((D_MODEL, d_in), lambda i: (0, 0)), rope, rope, rope],
        out_specs=[row(A_WIDTH), row(A_KV_WIDTH), row(A_KV_WIDTH), row(S5_WIDTH)],
        out_shape=[jax.ShapeDtypeStruct((N_TOK, A_WIDTH), F32), jax.ShapeDtypeStruct((N_TOK, A_KV_WIDTH), F32),
                   jax.ShapeDtypeStruct((N_TOK, A_KV_WIDTH), F32), jax.ShapeDtypeStruct((N_TOK, S5_WIDTH), F32)],
        scratch_shapes=[pltpu.VMEM((D_MODEL, A_WIDTH), BF16), pltpu.VMEM((D_MODEL, A_KV_WIDTH), BF16),
                        pltpu.VMEM((D_MODEL, A_KV_WIDTH), BF16), pltpu.VMEM((D_MODEL, S5_WIDTH), BF16)],
        compiler_params=_cparams("arbitrary"),
        name="ab_in",
    )(x, g.reshape(1, D_MODEL), mod, mod, w, cos, sa, sb)


def _softmax_pv(s_list, v_list, sink):
    m = functools.reduce(jnp.maximum, [jnp.max(s, axis=-1, keepdims=True) for s in s_list])
    if sink is not None:
        m = jnp.maximum(m, sink)
    ps = [jnp.exp(s - m) for s in s_list]
    l = functools.reduce(jnp.add, [jnp.sum(p, axis=-1, keepdims=True) for p in ps])
    if sink is not None:
        l = l + jnp.exp(sink - m)
    o = functools.reduce(jnp.add, [_dot(p, v) for p, v in zip(ps, v_list)])
    return o / l


def _gqa_with_sink(sink_ref, q, k_of, v_of, bias, o_ref):
    g = A_HEADS // A_KV_HEADS
    r = q.shape[0]
    outs = [None] * A_HEADS
    for kh in range(A_KV_HEADS):
        heads = range(g * kh, g * (kh + 1))
        qg = jnp.concatenate([q[:, HEAD_DIM * h:HEAD_DIM * (h + 1)] for h in heads], axis=0)
        sink = jnp.concatenate([jnp.full((r, 1), sink_ref[h], F32) for h in heads], axis=0)
        s = _dot_nt(qg, k_of(kh))
        if bias is not None:
            s = s + jnp.concatenate([bias] * g, axis=0)
        m = jnp.maximum(jnp.max(s, axis=-1, keepdims=True), sink)
        p = jnp.exp(s - m)
        l = jnp.sum(p, axis=-1, keepdims=True) + jnp.exp(sink - m)
        o = _dot(p, v_of(kh)) / l
        for n, h in enumerate(heads):
            outs[h] = o[r * n:r * (n + 1)]
    o_ref[...] = jnp.concatenate(outs, axis=1)


def _attn_ctx_kernel(sink_ref, q_ref, k_ref, v_ref, o_ref):
    scale = HEAD_DIM ** -0.5
    g = A_HEADS // A_KV_HEADS
    outs = []
    for h in range(A_HEADS):
        kh = h // g
        q = q_ref[:, HEAD_DIM * h:HEAD_DIM * (h + 1)]
        k = k_ref[:, HEAD_DIM * kh:HEAD_DIM * (kh + 1)]
        v = v_ref[:, HEAD_DIM * kh:HEAD_DIM * (kh + 1)]
        s = _dot_nt(q, k) * scale
        outs.append(_softmax_pv([s], [v], sink_ref[h]))
    o_ref[...] = jnp.concatenate(outs, axis=1)


def _attn_lat_kernel(sink_ref, q_ref, kp_ref, kc_ref, kn_ref, vp_ref, vc_ref, vn_ref, kx_ref, vx_ref, o_ref):
    n = pl.program_id(1)
    nb = DEC_SEQ // A_WINDOW
    i = lax.broadcasted_iota(jnp.int32, (A_WINDOW, A_WINDOW), 0)
    j = lax.broadcasted_iota(jnp.int32, (A_WINDOW, A_WINDOW), 1)
    zero = jnp.zeros((A_WINDOW, A_WINDOW), F32)
    bias = jnp.concatenate([jnp.where((j >= i) & (n > 0), 0.0, NEG_INF), zero,
                            jnp.where((j <= i) & (n < nb - 1), 0.0, NEG_INF),
                            jnp.zeros((A_WINDOW, PAST_LEN), F32)], axis=1)

    def rows(p_ref, c_ref, n_ref, x_ref):
        def of(kh):
            sl = slice(HEAD_DIM * kh, HEAD_DIM * (kh + 1))
            return jnp.concatenate([p_ref[:, sl], c_ref[:, sl], n_ref[:, sl], x_ref[0, :, sl]], axis=0)
        return of

    _gqa_with_sink(sink_ref, q_ref[...] * HEAD_DIM ** -0.5, rows(kp_ref, kc_ref, kn_ref, kx_ref),
                   rows(vp_ref, vc_ref, vn_ref, vx_ref), bias, o_ref)


def _attention_a(q, k, v, cache_k, cache_v, sink):
    smem = pl.BlockSpec(memory_space=pltpu.SMEM)
    o = pl.pallas_call(
        _attn_ctx_kernel,
        grid=(BATCH,),
        in_specs=[smem, pl.BlockSpec((SEQ, A_WIDTH), lambda b: (b, 0)),
                  pl.BlockSpec((SEQ, A_KV_WIDTH), lambda b: (b, 0)), pl.BlockSpec((SEQ, A_KV_WIDTH), lambda b: (b, 0))],
        out_specs=pl.BlockSpec((SEQ, A_WIDTH), lambda b: (b, 0)),
        out_shape=jax.ShapeDtypeStruct((N_CTX_TOK, A_WIDTH), F32),
        compiler_params=_cparams("arbitrary"),
        name="attn_a_ctx",
    )(sink, q, k, v)
    nb = DEC_SEQ // A_WINDOW
    base = N_CTX_TOK // A_WINDOW
    cur = lambda b, n: (base + b * nb + n, 0)
    prev = lambda b, n: (base + b * nb + jnp.maximum(n - 1, 0), 0)
    nxt = lambda b, n: (base + b * nb + jnp.minimum(n + 1, nb - 1), 0)
    kv = lambda f: pl.BlockSpec((A_WINDOW, A_KV_WIDTH), f)
    cache = pl.BlockSpec((1, PAST_LEN, A_KV_WIDTH), lambda b, n: (b, 0, 0))
    o_lat = pl.pallas_call(
        _attn_lat_kernel,
        grid=(DEC_BATCH, nb),
        in_specs=[smem, pl.BlockSpec((A_WINDOW, A_WIDTH), cur), kv(prev), kv(cur), kv(nxt), kv(prev), kv(cur), kv(nxt),
                  cache, cache],
        out_specs=pl.BlockSpec((A_WINDOW, A_WIDTH), lambda b, n: (b * nb + n, 0)),
        out_shape=jax.ShapeDtypeStruct((N_LAT_TOK, A_WIDTH), F32),
        compiler_params=_cparams("arbitrary", "arbitrary"),
        name="attn_a_lat",
    )(sink, q, k, k, k, v, v, v, cache_k, cache_v)
    return o, o_lat


S5_CHUNK = 16
S5_OCT = 128 // S5_GROUP
S5_NOCT = S5_GROUPS // S5_OCT
S5_K = S5_CHUNK * 128
S5_PART = S5_OCT * S5_STATE
S5_SW = 4 * S5_PART
S5_ROWS_CTX = BATCH * SEQ // S5_CHUNK
S5_ROWS_LAT = DEC_BATCH * DEC_SEQ // S5_CHUNK
S5_ROWS = S5_ROWS_CTX + S5_ROWS_LAT
S5_NB = 4
HI = lax.Precision.HIGHEST


def _s5_disc_kernel(lr_ref, li_ref, ldt_ref, ar_ref, ai_ref, zr_ref, zi_ref):
    lr, li = lr_ref[...], li_ref[...]
    dt = jnp.exp(ldt_ref[...])
    mag = jnp.exp(lr * dt)
    ar, ai = mag * jnp.cos(li * dt), mag * jnp.sin(li * dt)
    den = lr * lr + li * li
    ar_ref[...] = ar
    ai_ref[...] = ai
    zr_ref[...] = ((ar - 1.0) * lr + ai * li) / den
    zi_ref[...] = (ai * lr - (ar - 1.0) * li) / den


def _cmul(xr, xi, yr, yi):
    return xr * yr - xi * yi, xr * yi + xi * yr


def _dot_nt_hi(a, b):
    return lax.dot_general(a, b, (((1,), (1,)), ((), ())), precision=HI, preferred_element_type=F32)


def _s5_prep_kernel(ar_ref, ai_ref, zr_ref, zi_ref, btr_ref, bti_ref, ctr_ref, cti_ref,
                    m_ref, win_ref, wout_ref, a16_ref, pw_s, w_s, k_s):
    t = pl.program_id(1)
    npw = S5_CHUNK + 1
    blk = lambda j: pl.ds(pl.multiple_of(j * 128, 128), 128)

    @pl.when(t == 0)
    def _():
        kd = []
        for d in range(2):
            ar, ai = ar_ref[0, d], ai_ref[0, d]
            pr, pi = jnp.ones_like(ar), jnp.zeros_like(ar)
            bbr, bbi = _cmul(zr_ref[0, d], zi_ref[0, d], btr_ref[0, d], bti_ref[0, d])
            for j in range(npw):
                pw_s[d, 0, j:j + 1, :] = pr
                pw_s[d, 1, j:j + 1, :] = pi
                if j < S5_CHUNK:
                    wr, wi = _cmul(pr, pi, bbr, bbi)
                    w_s[d, 0, 128 * j:128 * (j + 1), :] = wr.astype(BF16)
                    w_s[d, 1, 128 * j:128 * (j + 1), :] = wi.astype(BF16)
                pr, pi = _cmul(pr, pi, ar, ai)
            kd.append(_dot_nt(w_s[d, 0], ctr_ref[0, d]) - _dot_nt(w_s[d, 1], cti_ref[0, d]))
        for jj in range(2 * S5_CHUNK - 1):
            j = jj - (S5_CHUNK - 1)
            if j > 0:
                k = kd[0][128 * j:128 * (j + 1)]
            elif j < 0:
                k = kd[1][128 * -j:128 * (1 - j)]
            else:
                k = kd[0][0:128] + kd[1][0:128]
            k_s[128 * jj:128 * (jj + 1), :] = k
        a16_ref[0] = jnp.concatenate([pw_s[0, 0, S5_CHUNK:npw, :], pw_s[1, 0, S5_CHUNK:npw, :],
                                      pw_s[0, 1, S5_CHUNK:npw, :], pw_s[1, 1, S5_CHUNK:npw, :]], axis=1)

    for tp in range(S5_CHUNK):
        m_ref[0, :, 128 * tp:128 * (tp + 1)] = k_s[blk(S5_CHUNK - 1 + tp - t), :].astype(BF16)

    def power(d, e):
        return pw_s[d, 0, pl.ds(e, 1), :], pw_s[d, 1, pl.ds(e, 1), :]

    for d in range(2):
        j = (S5_CHUNK - 1 - t) if d == 0 else t
        win_ref[0, :, S5_PART * d:S5_PART * (d + 1)] = w_s[d, 0, blk(j), :]
        win_ref[0, :, S5_PART * (2 + d):S5_PART * (3 + d)] = w_s[d, 1, blk(j), :]
        er, ei = _cmul(*power(d, (t + 1) if d == 0 else (S5_CHUNK - t)), ctr_ref[0, d], cti_ref[0, d])
        wout_ref[0, :, S5_PART * d:S5_PART * (d + 1)] = er.astype(BF16)
        wout_ref[0, :, S5_PART * (2 + d):S5_PART * (3 + d)] = (-ei).astype(BF16)


def _s5_main_kernel(u_ref, win_ref, m_ref, wout_ref, a16_ref, h0_ref, y_ref, hfin_ref, uo_s, x_s, hs_s):
    s = pl.program_id(1)
    nq = S5_PART // 128

    @pl.when(s == 0)
    def _():
        for t in range(S5_CHUNK):
            uo_s[:, 128 * t:128 * (t + 1)] = u_ref[pl.ds(t, S5_ROWS, stride=S5_CHUNK), :].astype(BF16)

    @pl.when(s < S5_NB)
    def _():
        x = jnp.dot(uo_s[...], win_ref[0], preferred_element_type=F32)
        for q in range(nq):
            x_s[s * nq + q] = x[:, 128 * q:128 * (q + 1)]

    @pl.when(s == S5_NB - 1)
    def _():
        def run(row0, nb, nc, h):
            for c in range(nc):
                for d in range(2):
                    cc = c if d == 0 else nc - 1 - c
                    rows = pl.ds(row0 + cc, nb, stride=nc)
                    for q in range(nq):
                        kr, ki = d * nq + q, (2 + d) * nq + q
                        hr, hi = h[d][0][q], h[d][1][q]
                        hs_s[kr, rows, :] = hr
                        hs_s[ki, rows, :] = hi
                        ar = a16_ref[0, :, 128 * kr:128 * (kr + 1)]
                        ai = a16_ref[0, :, 128 * ki:128 * (ki + 1)]
                        h[d][0][q] = ar * hr - ai * hi + x_s[kr, rows, :]
                        h[d][1][q] = ar * hi + ai * hr + x_s[ki, rows, :]
            return h

        zero = jnp.zeros((BATCH, 128), F32)
        fin = run(0, BATCH, SEQ // S5_CHUNK, [[[zero] * nq, [zero] * nq] for _ in range(2)])
        for d in range(2):
            for ri in range(2):
                for q in range(nq):
                    k = (2 * ri + d) * nq + q
                    hfin_ref[0, :, 128 * k:128 * (k + 1)] = fin[d][ri][q]
        h0 = [[[h0_ref[0, :, 128 * ((2 * ri + d) * nq + q):128 * ((2 * ri + d) * nq + q + 1)] for q in range(nq)]
               for ri in range(2)] for d in range(2)]
        run(S5_ROWS_CTX, DEC_BATCH, DEC_SEQ // S5_CHUNK, h0)

    @pl.when(s >= S5_NB)
    def _():
        hs = jnp.concatenate([hs_s[k] for k in range(4 * nq)], axis=1).astype(BF16)
        y = (jnp.dot(uo_s[...], m_ref[0], preferred_element_type=F32)
             + lax.dot_general(hs, wout_ref[0], (((1,), (1,)), ((), ())), preferred_element_type=F32))
        for q in range(nq):
            t = (s - S5_NB) * nq + q
            y_ref[pl.ds(t, S5_ROWS, stride=S5_CHUNK), :] = y[:, 128 * q:128 * (q + 1)]


def _s5_octets(t, lanes):
    return t.reshape(2, S5_NOCT, 1, S5_OCT * lanes).transpose(1, 0, 2, 3)


def _s5_blockdiag(t):
    a, n = t.shape[2], t.shape[3]
    t = t.reshape(2, S5_NOCT, S5_OCT, a, n)
    bd = jnp.einsum('dogan,gh->dogahn', t, jnp.eye(S5_OCT, dtype=t.dtype))
    return bd.reshape(2, S5_NOCT, S5_OCT * a, S5_OCT * n).transpose(1, 0, 2, 3)


def _s5_state_to_lanes(h_re, h_im):
    b = h_re.shape[0]
    parts = jnp.stack([h_re[:, 0], h_re[:, 1], h_im[:, 0], h_im[:, 1]], axis=1)
    parts = parts.reshape(b, 4, S5_NOCT, S5_PART).transpose(2, 0, 1, 3)
    return parts.reshape(S5_NOCT, b, S5_SW)


def _s5_state_from_lanes(h):
    b = h.shape[1]
    parts = h.reshape(S5_NOCT, b, 4, S5_OCT, S5_STATE).transpose(1, 2, 0, 3, 4).reshape(b, 4, S5_GROUPS, S5_STATE)
    return parts[:, 0:2], parts[:, 2:4]


def _s5_scan(u, h0_re, h0_im, lam_re, lam_im, log_dt, b_re, b_im, c_re, c_im):
    ng, n = S5_GROUPS, S5_STATE
    rows = 2 * ng
    disc = pl.pallas_call(
        _s5_disc_kernel,
        out_shape=[jax.ShapeDtypeStruct((rows, n), F32)] * 4,
        name="s5_disc",
    )(lam_re.reshape(rows, n), lam_im.reshape(rows, n), log_dt.reshape(rows, 1))
    ar, ai, zr, zi = [_s5_octets(t.reshape(2, ng, n), n) for t in disc]
    bt = lambda t: _s5_blockdiag(t.transpose(0, 1, 3, 2))
    vec = pl.BlockSpec((1, 2, 1, S5_PART), lambda o, t: (o, 0, 0, 0))
    mat = pl.BlockSpec((1, 2, 128, S5_PART), lambda o, t: (o, 0, 0, 0))
    rowblk = lambda w: pl.BlockSpec((1, 128, w), lambda o, t: (o, t, 0))
    m, win, wout, a16 = pl.pallas_call(
        _s5_prep_kernel,
        grid=(S5_NOCT, S5_CHUNK),
        in_specs=[vec, vec, vec, vec, mat, mat, mat, mat],
        out_specs=[rowblk(S5_K), rowblk(S5_SW), rowblk(S5_SW), pl.BlockSpec((1, 1, S5_SW), lambda o, t: (o, 0, 0))],
        out_shape=[jax.ShapeDtypeStruct((S5_NOCT, S5_K, S5_K), BF16), jax.ShapeDtypeStruct((S5_NOCT, S5_K, S5_SW), BF16),
                   jax.ShapeDtypeStruct((S5_NOCT, S5_K, S5_SW), BF16), jax.ShapeDtypeStruct((S5_NOCT, 1, S5_SW), F32)],
        scratch_shapes=[pltpu.VMEM((2, 2, 24, S5_PART), F32), pltpu.VMEM((2, 2, S5_K, S5_PART), BF16),
                        pltpu.VMEM(((2 * S5_CHUNK - 1) * 128, 128), F32)],
        compiler_params=_cparams("arbitrary", "arbitrary"),
        name="s5_prep",
    )(ar, ai, zr, zi, bt(b_re), bt(b_im), _s5_blockdiag(c_re), _s5_blockdiag(c_im))
    nb = S5_NB
    y, hfin = pl.pallas_call(
        _s5_main_kernel,
        grid=(S5_NOCT, 2 * nb),
        in_specs=[pl.BlockSpec((N_TOK, 128), lambda o, s: (0, o)),
                  pl.BlockSpec((1, S5_K, S5_PART), lambda o, s: (o, 0, jnp.minimum(s, nb - 1))),
                  pl.BlockSpec((1, S5_K, S5_PART), lambda o, s: (o, 0, jnp.maximum(s - nb, 0))),
                  pl.BlockSpec((1, S5_PART, S5_SW), lambda o, s: (o, jnp.maximum(s - nb, 0), 0)),
                  pl.BlockSpec((1, 1, S5_SW), lambda o, s: (o, 0, 0)),
                  pl.BlockSpec((1, DEC_BATCH, S5_SW), lambda o, s: (o, 0, 0))],
        out_specs=[pl.BlockSpec((N_TOK, 128), lambda o, s: (0, o)),
                   pl.BlockSpec((1, BATCH, S5_SW), lambda o, s: (o, 0, 0))],
        out_shape=[jax.ShapeDtypeStruct((N_TOK, S5_WIDTH), F32), jax.ShapeDtypeStruct((S5_NOCT, BATCH, S5_SW), F32)],
        scratch_shapes=[pltpu.VMEM((S5_ROWS, S5_K), BF16), pltpu.VMEM((S5_SW // 128, S5_ROWS, 128), F32),
                        pltpu.VMEM((S5_SW // 128, S5_ROWS, 128), F32)],
        compiler_params=_cparams("arbitrary", "arbitrary"),
        name="s5_main",
    )(u, win, m, wout, a16, _s5_state_to_lanes(h0_re, h0_im))
    fin_re, fin_im = _s5_state_from_lanes(hfin)
    return y, fin_re, fin_im


def _ctx_or_lat(ctx_ref, lat_ref):
    return jnp.where(pl.program_id(0) < N_CTX_BLK, ctx_ref[...], lat_ref[...])


def _split_specs(width):
    return [pl.BlockSpec((ROW_BLK, width), lambda i: (jnp.minimum(i, N_CTX_BLK - 1), 0)),
            pl.BlockSpec((ROW_BLK, width), lambda i: (jnp.maximum(i - N_CTX_BLK, 0), 0))]


def _ab_out_kernel(x_ref, oac_ref, oal_ref, y_ref, u_ref, d_ref, wglu_ref, wout_ref, gt_ref, o_ref, wglu_s, wa_s, wb_s):
    @pl.when(pl.program_id(0) == 0)
    def _():
        wglu_s[...] = wglu_ref[...].astype(BF16)
        wa_s[...] = wout_ref[0:A_WIDTH, :].astype(BF16)
        wb_s[...] = wout_ref[A_WIDTH:, :].astype(BF16)

    g = jax.nn.gelu(y_ref[...] + d_ref[...] * u_ref[...])
    ob = g * jax.nn.sigmoid(jnp.dot(g.astype(BF16), wglu_s[...], preferred_element_type=F32))
    out = (jnp.dot(_ctx_or_lat(oac_ref, oal_ref).astype(BF16), wa_s[...], preferred_element_type=F32)
           + jnp.dot(ob.astype(BF16), wb_s[...], preferred_element_type=F32))
    o_ref[...] = x_ref[...] + gt_ref[0] * out


def _ab_out(x, oa, y, u, d_skip, w_glu, w_out, mod, layer):
    row = lambda n: pl.BlockSpec((ROW_BLK, n), lambda i: (i, 0))
    full = lambda a, b: pl.BlockSpec((a, b), lambda i: (0, 0))
    return pl.pallas_call(
        _ab_out_kernel,
        grid=(N_ROW_BLK,),
        in_specs=[row(D_MODEL)] + _split_specs(A_WIDTH) + [row(S5_WIDTH), row(S5_WIDTH), full(1, S5_WIDTH),
                  full(S5_WIDTH, S5_WIDTH), full(A_WIDTH + S5_WIDTH, D_MODEL), _mod_spec(layer, 2)],
        out_specs=row(D_MODEL),
        out_shape=jax.ShapeDtypeStruct((N_TOK, D_MODEL), F32),
        scratch_shapes=[pltpu.VMEM((S5_WIDTH, S5_WIDTH), BF16), pltpu.VMEM((A_WIDTH, D_MODEL), BF16),
                        pltpu.VMEM((S5_WIDTH, D_MODEL), BF16)],
        compiler_params=_cparams("arbitrary"),
        name="ab_out",
    )(x, oa[0], oa[1], y, u, d_skip.reshape(1, S5_WIDTH), w_glu, w_out, mod)


CD_HG0 = MLA_Q_RANK + MLA_KV_RANK + MLA_ROPE
KPE_LANES = 128


def _cd_in_kernel(x_ref, g_ref, sc_ref, sh_ref, w_ref, gq_ref, wqu_ref, gkv_ref, wkvu_ref, cos_ref, sa_ref, sb_ref,
                  qn_ref, qr_ref, ckv_ref, kpe_ref, kn_ref, vm_ref, hq_ref, hff_ref, hfb_ref, hi_ref, hg_ref,
                  wcq_s, wckv_s, wkpe_s, whg_s, wqu_s, wkvu_s):
    @pl.when(pl.program_id(0) == 0)
    def _():
        wcq_s[...] = w_ref[:, 0:MLA_Q_RANK].astype(BF16)
        wckv_s[...] = w_ref[:, MLA_Q_RANK:MLA_Q_RANK + MLA_KV_RANK].astype(BF16)
        kp = w_ref[:, MLA_Q_RANK + MLA_KV_RANK:CD_HG0].astype(BF16)
        wkpe_s[...] = jnp.concatenate([kp] * (KPE_LANES // MLA_ROPE), axis=1)
        whg_s[...] = w_ref[:, CD_HG0:].astype(BF16)
        wqu_s[...] = wqu_ref[...].astype(BF16)
        wkvu_s[...] = wkvu_ref[...].astype(BF16)

    h = _norm_mod(x_ref[...], g_ref[...], sc_ref[0], sh_ref[0]).astype(BF16)
    cos, sa, sb = cos_ref[...], sa_ref[...], sb_ref[...]
    nope = MLA_HEADS * MLA_NOPE
    cq = _rms(jnp.dot(h, wcq_s[...], preferred_element_type=F32)) * gq_ref[...]
    qq = jnp.dot(cq.astype(BF16), wqu_s[...], preferred_element_type=F32)
    qn_ref[...] = qq[:, 0:nope]
    qr_ref[...] = _apply_rope(qq[:, nope:], cos, sa, sb, MLA_ROPE // 4)
    ckv = _rms(jnp.dot(h, wckv_s[...], preferred_element_type=F32)) * gkv_ref[...]
    ckv_ref[...] = ckv
    kv = jnp.dot(ckv.astype(BF16), wkvu_s[...], preferred_element_type=F32)
    kn_ref[...] = kv[:, 0:nope]
    vm_ref[...] = kv[:, nope:]
    kpe_ref[...] = _apply_rope(jnp.dot(h, wkpe_s[...], preferred_element_type=F32), cos, sa, sb, MLA_ROPE // 4)
    hh = jnp.dot(h, whg_s[...], preferred_element_type=F32)
    for n, ref in enumerate((hq_ref, hff_ref, hfb_ref, hi_ref, hg_ref)):
        ref[...] = hh[:, HG_KD * n:HG_KD * (n + 1)]


def _mla_split_heads(w, a):
    k, n = w.shape
    w = w.reshape(k, MLA_HEADS, n // MLA_HEADS)
    return jnp.concatenate([w[:, :, :a].reshape(k, -1), w[:, :, a:].reshape(k, -1)], axis=1)


def _cd_in(x, mod, g, w, g_q, w_q_up, g_kv, w_kv_up, layer):
    cos, sa, sb = _rope_tables(MLA_ROPE)
    d_in = w.shape[1]
    row = lambda n: pl.BlockSpec((ROW_BLK, n), lambda i: (i, 0))
    full = lambda a, b: pl.BlockSpec((a, b), lambda i: (0, 0))
    rope = pl.BlockSpec((ROW_BLK, 128), lambda i: (_rope_blk(i), 0))
    widths = [MLA_HEADS * MLA_NOPE, MLA_HEADS * MLA_ROPE, MLA_KV_RANK, KPE_LANES, MLA_HEADS * MLA_NOPE, MLA_WIDTH] + [HG_KD] * 5
    nq = MLA_HEADS * (MLA_NOPE + MLA_ROPE)
    nkv = MLA_HEADS * (MLA_NOPE + MLA_V)
    return pl.pallas_call(
        _cd_in_kernel,
        grid=(N_ROW_BLK,),
        in_specs=[row(D_MODEL), full(1, D_MODEL), _mod_spec(layer, 1), _mod_spec(layer, 0), full(D_MODEL, d_in),
                  full(1, MLA_Q_RANK), full(MLA_Q_RANK, nq), full(1, MLA_KV_RANK), full(MLA_KV_RANK, nkv), rope, rope, rope],
        out_specs=[row(n) for n in widths],
        out_shape=[jax.ShapeDtypeStruct((N_TOK, n), F32) for n in widths],
        scratch_shapes=[pltpu.VMEM((D_MODEL, MLA_Q_RANK), BF16), pltpu.VMEM((D_MODEL, MLA_KV_RANK), BF16),
                        pltpu.VMEM((D_MODEL, KPE_LANES), BF16), pltpu.VMEM((D_MODEL, 5 * HG_KD), BF16),
                        pltpu.VMEM((MLA_Q_RANK, nq), BF16), pltpu.VMEM((MLA_KV_RANK, nkv), BF16)],
        compiler_params=_cparams("arbitrary"),
        name="cd_in",
    )(x, g.reshape(1, D_MODEL), mod, mod, w, g_q.reshape(1, -1), _mla_split_heads(w_q_up, MLA_NOPE),
      g_kv.reshape(1, -1), _mla_split_heads(w_kv_up, MLA_NOPE), cos, sa, sb)


def _mm_kernel(a_ref, w_ref, *o_refs):
    r = _dot(a_ref[...], w_ref[...])
    off = 0
    for o in o_refs:
        o[...] = r[:, off:off + o.shape[1]]
        off += o.shape[1]


def _mla_cache_kv(cckv, w_kv_up):
    n = MLA_HEADS * MLA_NOPE
    rows = cckv.shape[0]
    return pl.pallas_call(
        _mm_kernel,
        grid=(rows // PAST_LEN,),
        in_specs=[pl.BlockSpec((PAST_LEN, MLA_KV_RANK), lambda i: (i, 0)),
                  pl.BlockSpec((MLA_KV_RANK, 2 * n), lambda i: (0, 0))],
        out_specs=[pl.BlockSpec((PAST_LEN, n), lambda i: (i, 0))] * 2,
        out_shape=[jax.ShapeDtypeStruct((rows, n), F32)] * 2,
        compiler_params=_cparams("arbitrary"),
        name="mla_cache_kv",
    )(cckv, _mla_split_heads(w_kv_up, MLA_NOPE))


def _mla_heads(qn, qr, keys, o_ref):
    scale = (MLA_NOPE + MLA_ROPE) ** -0.5
    outs = []
    for h in range(MLA_HEADS):
        a = slice(MLA_NOPE * h, MLA_NOPE * (h + 1))
        r = slice(MLA_ROPE * h, MLA_ROPE * (h + 1))
        s_list = [(_dot_nt(qn[:, a], kn[:, a]) + _dot_nt(qr[:, r], kp[:, 0:MLA_ROPE])) * scale for kn, kp, _ in keys]
        outs.append(_softmax_pv(s_list, [v[:, a] for _, _, v in keys], None))
    o_ref[...] = jnp.concatenate(outs, axis=1)


def _mla_ctx_kernel(qn_ref, qr_ref, kn_ref, kp_ref, v_ref, o_ref):
    _mla_heads(qn_ref[...], qr_ref[...], [(kn_ref[...], kp_ref[...], v_ref[...])], o_ref)


def _mla_lat_kernel(qn_ref, qr_ref, kn_ref, kp_ref, v_ref, kcn_ref, kcp_ref, vc_ref, o_ref):
    _mla_heads(qn_ref[...], qr_ref[...],
               [(kcn_ref[...], kcp_ref[0], vc_ref[...]), (kn_ref[...], kp_ref[...], v_ref[...])], o_ref)


def _attention_mla(qn, qr, kn, kpe, vm, kcn, kcpe, vc):
    n = MLA_HEADS * MLA_NOPE
    nr = MLA_HEADS * MLA_ROPE
    blk = lambda w: pl.BlockSpec((SEQ, w), lambda b: (b, 0))
    o = pl.pallas_call(
        _mla_ctx_kernel,
        grid=(BATCH,),
        in_specs=[blk(n), blk(nr), blk(n), blk(KPE_LANES), blk(MLA_WIDTH)],
        out_specs=blk(MLA_WIDTH),
        out_shape=jax.ShapeDtypeStruct((N_CTX_TOK, MLA_WIDTH), F32),
        compiler_params=_cparams("arbitrary"),
        name="mla_ctx",
    )(qn, qr, kn, kpe, vm)
    nq = DEC_SEQ // ROW_BLK
    qblk = lambda w: pl.BlockSpec((ROW_BLK, w), lambda b, i: (N_CTX_BLK + b * nq + i, 0))
    seq = lambda w: pl.BlockSpec((DEC_SEQ, w), lambda b, i: (N_CTX_TOK // DEC_SEQ + b, 0))
    past = lambda w: pl.BlockSpec((PAST_LEN, w), lambda b, i: (b, 0))
    o_lat = pl.pallas_call(
        _mla_lat_kernel,
        grid=(DEC_BATCH, nq),
        in_specs=[qblk(n), qblk(nr), seq(n), seq(KPE_LANES), seq(MLA_WIDTH), past(n),
                  pl.BlockSpec((1, PAST_LEN, MLA_ROPE), lambda b, i: (b, 0, 0)), past(MLA_WIDTH)],
        out_specs=pl.BlockSpec((ROW_BLK, MLA_WIDTH), lambda b, i: (b * nq + i, 0)),
        out_shape=jax.ShapeDtypeStruct((N_LAT_TOK, MLA_WIDTH), F32),
        compiler_params=_cparams("arbitrary", "arbitrary"),
        name="mla_lat",
    )(qn, qr, kn, kpe, vm, kcn, kcpe, vc)
    return o, o_lat


HG_TILE = 128
HG_NC = HG_TILE // HG_CHUNK
HG_HALF = 256


HG_SLABS = HG_KD // 128


def _hg_put(ref, x):
    for j in range(HG_SLABS):
        ref[j] = x[:, 128 * j:128 * (j + 1)]


def _hg_get(ref, r, l):
    return jnp.concatenate([ref[2 * l, r, :], ref[2 * l + 1, r, :]], axis=1)


def _hg_plane(ref, p):
    return jnp.concatenate([ref[j, pl.ds(p, HG_NC, stride=HG_CHUNK), :] for j in range(HG_SLABS)], axis=1)


def _hg_put_plane(ref, p, x):
    for j in range(HG_SLABS):
        ref[j, pl.ds(p, HG_NC, stride=HG_CHUNK), :] = x[:, 128 * j:128 * (j + 1)]


def _hg_direction(hq_ref, hf_ref, hi_ref, o_ref, lb, st_ref, q_s, f_s, kk_s, v_s, qt_s, kt_s, o_s, sign,
                  ones_bd, diag_mask):
    hq = hq_ref[...]
    _hg_put(q_s, hq * jax.nn.sigmoid(hq))
    f = lb + (1.0 - lb) * jax.nn.sigmoid(hf_ref[...])
    _hg_put(f_s, f)
    _hg_put(kk_s, 1.0 - f)
    _hg_put(v_s, hi_ref[...])
    pos = (lambda i: i) if sign > 0 else (lambda i: HG_CHUNK - 1 - i)
    q = [_hg_plane(q_s, pos(i)) for i in range(HG_CHUNK)]
    fp = [_hg_plane(f_s, pos(i)) for i in range(HG_CHUNK)]
    kk = [_hg_plane(kk_s, pos(i)) for i in range(HG_CHUNK)]
    v = [_hg_plane(v_s, pos(i)) for i in range(HG_CHUNK)]

    def expand(planes):
        pb = jnp.concatenate(planes, axis=0).astype(BF16)
        e = jnp.concatenate([jnp.dot(pb[:, 0:HG_HALF], ones_bd, preferred_element_type=F32),
                             jnp.dot(pb[:, HG_HALF:], ones_bd, preferred_element_type=F32)], axis=1)
        return [e[HG_NC * n:HG_NC * (n + 1)] for n in range(len(planes))]

    o = [e * v[i] for i, e in enumerate(expand([q[i] * kk[i] for i in range(HG_CHUNK)]))]
    dec = list(fp)
    for d in range(1, HG_CHUNK):
        if d > 1:
            for i in range(d, HG_CHUNK):
                dec[i] = dec[i] * fp[i - d + 1]
        es = expand([q[i] * kk[i - d] * dec[i] for i in range(d, HG_CHUNK)])
        for n, i in enumerate(range(d, HG_CHUNK)):
            o[i] = o[i] + es[n] * v[i - d]
    for i in range(HG_CHUNK):
        _hg_put_plane(o_s, pos(i), o[i])

    incl = [fp[0]]
    for i in range(1, HG_CHUNK):
        incl.append(incl[-1] * fp[i])
    excl = [None] * HG_CHUNK
    excl[HG_CHUNK - 1] = jnp.ones_like(fp[0])
    for i in range(HG_CHUNK - 2, -1, -1):
        excl[i] = excl[i + 1] * fp[i + 1]
    for i in range(HG_CHUNK):
        _hg_put_plane(qt_s, pos(i), q[i] * incl[i])
        _hg_put_plane(kt_s, pos(i), kk[i] * excl[i])
    whole = incl[HG_CHUNK - 1]
    for cc in range(HG_NC):
        c = cc if sign > 0 else HG_NC - 1 - cc
        r = slice(HG_CHUNK * c, HG_CHUNK * (c + 1))
        for g in range(2):
            l = slice(HG_HALF * g, HG_HALF * (g + 1))
            st = st_ref[g]
            o_ref[r, l] = _hg_get(o_s, r, g) + _dot_nt(_hg_get(qt_s, r, g), st)
            ds = lax.dot_general(hi_ref[r, l].astype(BF16), _hg_get(kt_s, r, g).astype(BF16),
                                 (((0,), (0,)), ((), ())), preferred_element_type=F32)
            st_ref[g] = st * whole[c:c + 1, l] + jnp.where(diag_mask, ds, 0.0)


def _hg_lower_bound(lb_ref, d, layer):
    raw = [lb_ref[d * DEPTH + m:d * DEPTH + m + 1, :] for m in range(DEPTH)]
    mx = functools.reduce(jnp.maximum, raw)
    e = [jnp.exp(r - mx) for r in raw]
    tot = functools.reduce(jnp.add, e)
    return functools.reduce(jnp.add, e[1:layer + 1], jnp.zeros_like(tot)) / tot


def _hgrn_kernel(nt, layer, hqf_ref, hff_ref, hif_ref, hqb_ref, hfb_ref, hib_ref, lb_ref, s0_ref,
                 of_ref, ob_ref, sfin_ref, st_s, q_s, f_s, kk_s, v_s, qt_s, kt_s, o_s):
    i = pl.program_id(1)

    @pl.when(i == 0)
    def _():
        st_s[...] = s0_ref[0]

    ri = lax.broadcasted_iota(jnp.int32, (HG_HALF, HG_HALF), 0) // HG_KEY
    ci = lax.broadcasted_iota(jnp.int32, (HG_HALF, HG_HALF), 1) // HG_KEY
    diag_mask = ri == ci
    ones_bd = jnp.where(diag_mask, 1.0, 0.0).astype(BF16)
    _hg_direction(hqf_ref, hff_ref, hif_ref, of_ref, _hg_lower_bound(lb_ref, 0, layer), st_s.at[0],
                  q_s, f_s, kk_s, v_s, qt_s, kt_s, o_s, 1, ones_bd, diag_mask)
    _hg_direction(hqb_ref, hfb_ref, hib_ref, ob_ref, _hg_lower_bound(lb_ref, 1, layer), st_s.at[1],
                  q_s, f_s, kk_s, v_s, qt_s, kt_s, o_s, -1, ones_bd, diag_mask)

    @pl.when(i == nt - 1)
    def _():
        sfin_ref[0] = st_s[...]


def _hg_state_to_blocks(s):
    b = s.shape[0]
    st = s.transpose(0, 1, 2, 4, 3).reshape(b, 2, 2, 4, HG_VAL, HG_KEY)
    eye = jnp.eye(4, dtype=s.dtype)
    return jnp.einsum('bdgivk,ij->bdgivjk', st, eye).reshape(b, 2, 2, HG_HALF, HG_HALF)


def _hg_state_from_blocks(st):
    b = st.shape[0]
    st = st.reshape(b, 2, 2, 4, HG_VAL, 4, HG_KEY)
    diag = jnp.stack([st[:, :, :, i, :, i, :] for i in range(4)], axis=3)
    return diag.reshape(b, 2, HG_HEADS, HG_VAL, HG_KEY).transpose(0, 1, 2, 4, 3)


def _hgrn_scan(hq, hff, hfb, hi, lb, s0, layer, row0, nseq, seqlen):
    nt = seqlen // HG_TILE
    base = row0 // HG_TILE
    fwd = pl.BlockSpec((HG_TILE, HG_KD), lambda b, i: (base + b * nt + i, 0))
    bwd = pl.BlockSpec((HG_TILE, HG_KD), lambda b, i: (base + b * nt + nt - 1 - i, 0))
    ofw = pl.BlockSpec((HG_TILE, HG_KD), lambda b, i: (b * nt + i, 0))
    obw = pl.BlockSpec((HG_TILE, HG_KD), lambda b, i: (b * nt + nt - 1 - i, 0))
    st = pl.BlockSpec((1, 2, 2, HG_HALF, HG_HALF), lambda b, i: (b, 0, 0, 0, 0))
    tile = lambda: pltpu.VMEM((HG_SLABS, HG_TILE, 128), F32)
    return pl.pallas_call(
        functools.partial(_hgrn_kernel, nt, layer),
        grid=(nseq, nt),
        in_specs=[fwd, fwd, fwd, bwd, bwd, bwd, pl.BlockSpec((2 * DEPTH, HG_KD), lambda b, i: (0, 0)), st],
        out_specs=[ofw, obw, st],
        out_shape=[jax.ShapeDtypeStruct((nseq * seqlen, HG_WIDTH), F32)] * 2
        + [jax.ShapeDtypeStruct((nseq, 2, 2, HG_HALF, HG_HALF), F32)],
        scratch_shapes=[pltpu.VMEM((2, 2, HG_HALF, HG_HALF), F32)] + [tile() for _ in range(7)],
        compiler_params=_cparams("arbitrary", "arbitrary"),
        name="hgrn_scan",
    )(hq, hff, hi, hq, hfb, hi, lb.reshape(2 * DEPTH, HG_KD), s0)


def _cd_out_kernel(x_ref, occ_ref, ocl_ref, ofc_ref, ofl_ref, obc_ref, obl_ref, hg_ref, go_ref, wout_ref, gt_ref,
                   o_ref, wa_s, wb_s):
    @pl.when(pl.program_id(0) == 0)
    def _():
        wa_s[...] = wout_ref[0:MLA_WIDTH, :].astype(BF16)
        wb_s[...] = wout_ref[MLA_WIDTH:, :].astype(BF16)

    ri = lax.broadcasted_iota(jnp.int32, (HG_HALF, HG_HALF), 0) // HG_VAL
    ci = lax.broadcasted_iota(jnp.int32, (HG_HALF, HG_HALF), 1) // HG_VAL
    ones_bd = jnp.where(ri == ci, 1.0, 0.0).astype(BF16)
    o = _ctx_or_lat(ofc_ref, ofl_ref) + _ctx_or_lat(obc_ref, obl_ref)
    sq = o * o
    hi = sq.astype(BF16)
    lo = (sq - hi.astype(F32)).astype(BF16)
    ms = jnp.concatenate(
        [jnp.dot(hi[:, l], ones_bd, preferred_element_type=F32) + jnp.dot(lo[:, l], ones_bd, preferred_element_type=F32)
         for l in (slice(0, HG_HALF), slice(HG_HALF, 2 * HG_HALF))], axis=1) * (1.0 / HG_VAL)
    hg = hg_ref[...]
    od = o * lax.rsqrt(ms + EPS) * go_ref[...] * (hg * jax.nn.sigmoid(hg))
    out = (jnp.dot(_ctx_or_lat(occ_ref, ocl_ref).astype(BF16), wa_s[...], preferred_element_type=F32)
           + jnp.dot(od.astype(BF16), wb_s[...], preferred_element_type=F32))
    o_ref[...] = x_ref[...] + gt_ref[0] * out


def _cd_out(x, oc, of, ob, hg, g_o, w_out, mod, layer):
    row = lambda n: pl.BlockSpec((ROW_BLK, n), lambda i: (i, 0))
    full = lambda a, b: pl.BlockSpec((a, b), lambda i: (0, 0))
    return pl.pallas_call(
        _cd_out_kernel,
        grid=(N_ROW_BLK,),
        in_specs=[row(D_MODEL)] + _split_specs(MLA_WIDTH) + _split_specs(HG_WIDTH) + _split_specs(HG_WIDTH)
        + [row(HG_WIDTH), full(1, HG_WIDTH), full(MLA_WIDTH + HG_WIDTH, D_MODEL), _mod_spec(layer, 2)],
        out_specs=row(D_MODEL),
        out_shape=jax.ShapeDtypeStruct((N_TOK, D_MODEL), F32),
        scratch_shapes=[pltpu.VMEM((MLA_WIDTH, D_MODEL), BF16), pltpu.VMEM((HG_WIDTH, D_MODEL), BF16)],
        compiler_params=_cparams("arbitrary"),
        name="cd_out",
    )(x, oc[0], oc[1], of[0], of[1], ob[0], ob[1], hg, jnp.tile(g_o, HG_HEADS).reshape(1, HG_WIDTH), w_out, mod)


N_PAIRS = 6
N_CLASSES = N_GROUPS * N_PAIRS
CLS_ROWS = 32
MOE_BM = 256
MOE_NBLK = N_TOK // MOE_BM + N_CLASSES
MOE_ROWS = MOE_NBLK * MOE_BM


def _moe_route_kernel(x_ref, g_ref, sc_ref, sh_ref, rw_ref, rb_ref, h_ref, ti_ref, tw_ref, cnt_ref, base_s):
    @pl.when(pl.program_id(0) == 0)
    def _():
        base_s[...] = jnp.zeros_like(base_s)

    h = _norm_mod(x_ref[...], g_ref[...], sc_ref[0], sh_ref[0])
    logits = lax.dot_general(rw_ref[...], h, (((1,), (1,)), ((), ())), precision=HI,
                             preferred_element_type=F32)
    aff = jax.nn.sigmoid(logits)
    sel = aff + rb_ref[...]
    s = [sel[e:e + 1, :] for e in range(N_EXPERTS)]
    a = [aff[e:e + 1, :] for e in range(N_EXPERTS)]
    gs = []
    for g in range(N_GROUPS):
        m = s[4 * g:4 * g + 4]
        pairs = [m[i] + m[j] for i in range(4) for j in range(i + 1, 4)]
        gs.append(functools.reduce(jnp.maximum, pairs))
    gmax = functools.reduce(jnp.maximum, gs)
    taken = jnp.zeros_like(gmax) > 1.0
    gsel = []
    for g in range(N_GROUPS):
        hit = (gs[g] == gmax) & jnp.logical_not(taken)
        gsel.append(hit)
        taken = taken | hit
    e_lo = jnp.zeros(gmax.shape, jnp.int32)
    e_hi = jnp.zeros(gmax.shape, jnp.int32)
    a_lo = jnp.zeros_like(gmax)
    a_hi = jnp.zeros_like(gmax)
    nsel = jnp.zeros(gmax.shape, jnp.int32)
    for g in range(N_GROUPS):
        for i in range(4):
            e = 4 * g + i
            beat = jnp.zeros(gmax.shape, jnp.int32)
            for j in range(4):
                if j != i:
                    o = 4 * g + j
                    beat = beat + jnp.where((s[o] > s[e]) | ((s[o] == s[e]) & (j < i)), 1, 0)
            pick = gsel[g] & (beat < 2)
            is_first = pick & (nsel == 0)
            is_second = pick & (nsel == 1)
            e_lo = jnp.where(is_first, e, e_lo)
            a_lo = jnp.where(is_first, a[e], a_lo)
            e_hi = jnp.where(is_second, e, e_hi)
            a_hi = jnp.where(is_second, a[e], a_hi)
            nsel = nsel + jnp.where(pick, 1, 0)
    grp = e_lo // EXPERTS_PER_GROUP
    lo = e_lo - grp * EXPERTS_PER_GROUP
    hi = e_hi - grp * EXPERTS_PER_GROUP
    cls = grp * N_PAIRS + ((lo * (7 - lo)) >> 1) + (hi - lo - 1)
    wsum = a_lo + a_hi
    w_lo, w_hi = a_lo / wsum, a_hi / wsum
    onehot = (lax.broadcasted_iota(jnp.int32, (CLS_ROWS, ROW_BLK), 0) == cls).astype(F32)
    tt = lax.broadcasted_iota(jnp.int32, (ROW_BLK, ROW_BLK), 0) < lax.broadcasted_iota(jnp.int32, (ROW_BLK, ROW_BLK), 1)
    before = _dot(onehot, jnp.where(tt, 1.0, 0.0))
    base = base_s[...]
    rank = jnp.sum(onehot * (before + base[:, 0:1]), axis=0, keepdims=True).astype(jnp.int32)
    base = base + jnp.sum(onehot, axis=1, keepdims=True)
    base_s[...] = base
    cnt_ref[...] = base.astype(jnp.int32)
    ti_ref[0] = jnp.concatenate([cls, rank, e_lo, e_hi, jnp.zeros((4, ROW_BLK), jnp.int32)], axis=0)
    tw_ref[0] = jnp.concatenate([w_lo, w_hi, jnp.zeros((6, ROW_BLK), F32)], axis=0)
    h_ref[...] = h


def _moe_route(x, mod, g, router_w, router_b, layer):
    row = lambda n: pl.BlockSpec((ROW_BLK, n), lambda i: (i, 0))
    full = lambda a, b: pl.BlockSpec((a, b), lambda i: (0, 0))
    info = pl.BlockSpec((1, 8, ROW_BLK), lambda i: (i, 0, 0))
    return pl.pallas_call(
        _moe_route_kernel,
        grid=(N_ROW_BLK,),
        in_specs=[row(D_MODEL), full(1, D_MODEL), _mod_spec(layer, 4), _mod_spec(layer, 3),
                  full(N_EXPERTS, D_MODEL), full(N_EXPERTS, 1)],
        out_specs=[row(D_MODEL), info, info, full(CLS_ROWS, 128)],
        out_shape=[jax.ShapeDtypeStruct((N_TOK, D_MODEL), F32),
                   jax.ShapeDtypeStruct((N_ROW_BLK, 8, ROW_BLK), jnp.int32),
                   jax.ShapeDtypeStruct((N_ROW_BLK, 8, ROW_BLK), F32),
                   jax.ShapeDtypeStruct((CLS_ROWS, 128), jnp.int32)],
        scratch_shapes=[pltpu.VMEM((CLS_ROWS, 128), F32)],
        compiler_params=_cparams("arbitrary"),
        name="moe_route",
    )(x, g.reshape(1, D_MODEL), mod, mod, router_w.T, router_b.reshape(N_EXPERTS, 1))


def _moe_sort_kernel(pos_ref, wlo_ref, whi_ref, nblk_ref, h_ref, hs_ref, ws_ref, perm_ref, stage_s):
    j = pl.program_id(0)

    @pl.when(j == 0)
    def _():
        def init(r, c):
            perm_ref[r] = N_TOK
            return c

        lax.fori_loop(0, MOE_ROWS, init, 0, unroll=16)

        def build(t, c):
            perm_ref[pos_ref[t]] = t
            return c

        lax.fori_loop(0, N_TOK, build, 0, unroll=16)

    @pl.when(j < nblk_ref[0])
    def _():
        low = lax.broadcasted_iota(jnp.int32, (1, 128), 1) < 64
        base = j * MOE_BM
        for r in range(MOE_BM):
            src = jnp.minimum(perm_ref[base + r], N_TOK - 1)
            stage_s[r:r + 1, :] = h_ref[pl.ds(src, 1), :]
            ws_ref[r:r + 1, :] = jnp.where(low, wlo_ref[src], whi_ref[src])
        hs_ref[...] = stage_s[...].astype(BF16)

    @pl.when(j >= nblk_ref[0])
    def _():
        hs_ref[...] = jnp.zeros_like(hs_ref)
        ws_ref[...] = jnp.zeros_like(ws_ref)


def _moe_sort(h, pos, w_lo, w_hi, nblk):
    grid_spec = pltpu.PrefetchScalarGridSpec(
        num_scalar_prefetch=4,
        grid=(MOE_NBLK,),
        in_specs=[pl.BlockSpec((N_TOK, D_MODEL), lambda j, *_: (0, 0), pipeline_mode=pl.Buffered(1))],
        out_specs=[pl.BlockSpec((MOE_BM, D_MODEL), lambda j, *_: (j, 0)),
                   pl.BlockSpec((MOE_BM, 128), lambda j, *_: (j, 0)),
                   pl.BlockSpec(memory_space=pltpu.SMEM)],
        scratch_shapes=[pltpu.VMEM((MOE_BM, D_MODEL), F32)],
    )
    return pl.pallas_call(
        _moe_sort_kernel,
        grid_spec=grid_spec,
        out_shape=[jax.ShapeDtypeStruct((MOE_ROWS, D_MODEL), BF16), jax.ShapeDtypeStruct((MOE_ROWS, 128), F32),
                   jax.ShapeDtypeStruct((MOE_ROWS,), jnp.int32)],
        compiler_params=_cparams("arbitrary"),
        name="moe_sort",
    )(pos, w_lo, w_hi, nblk, h)


def _moe_experts_kernel(elo_ref, ehi_ref, nblk_ref, hs_ref, ws_ref, wgl_ref, wul_ref, wdl_ref,
                        wgh_ref, wuh_ref, wdh_ref, y_ref):
    del elo_ref, ehi_ref
    j = pl.program_id(0)

    @pl.when(j < nblk_ref[0])
    def _():
        h = hs_ref[...]
        acc = None
        for wg, wu, wd, gate in ((wgl_ref, wul_ref, wdl_ref, ws_ref[:, 0:1]), (wgh_ref, wuh_ref, wdh_ref, ws_ref[:, 64:65])):
            g = _dot(h, wg[0, 0])
            u = _dot(h, wu[0, 0])
            hid = g * jax.nn.sigmoid(g) * u * gate
            y = _dot(hid, wd[0, 0])
            acc = y if acc is None else acc + y
        y_ref[...] = acc

    @pl.when(j >= nblk_ref[0])
    def _():
        y_ref[...] = jnp.zeros_like(y_ref)


def _moe_experts(hs, ws, blk_elo, blk_ehi, nblk, w_gate, w_up, w_down, layer):
    last = lambda j, nb: jnp.minimum(j, nb[0] - 1)
    wspec = lambda a, b, which: pl.BlockSpec(
        (1, 1, a, b), lambda j, elo, ehi, nb: (layer, (elo, ehi)[which][last(j, nb)], 0, 0))
    grid_spec = pltpu.PrefetchScalarGridSpec(
        num_scalar_prefetch=3,
        grid=(MOE_NBLK,),
        in_specs=[pl.BlockSpec((MOE_BM, D_MODEL), lambda j, elo, ehi, nb: (last(j, nb), 0)),
                  pl.BlockSpec((MOE_BM, 128), lambda j, elo, ehi, nb: (last(j, nb), 0)),
                  wspec(D_MODEL, D_FF, 0), wspec(D_MODEL, D_FF, 0), wspec(D_FF, D_MODEL, 0),
                  wspec(D_MODEL, D_FF, 1), wspec(D_MODEL, D_FF, 1), wspec(D_FF, D_MODEL, 1)],
        out_specs=pl.BlockSpec((MOE_BM, D_MODEL), lambda j, *_: (j, 0)),
    )
    return pl.pallas_call(
        _moe_experts_kernel,
        grid_spec=grid_spec,
        out_shape=jax.ShapeDtypeStruct((MOE_ROWS, D_MODEL), F32),
        compiler_params=_cparams("arbitrary"),
        name="moe_experts",
    )(blk_elo, blk_ehi, nblk, hs, ws, w_gate, w_up, w_down, w_gate, w_up, w_down)


MOE_DUMP = 8


def _moe_combine_kernel(perm_ref, nblk_ref, y_ref, x_ref, gt_ref, o_ref, ytok_s):
    s = pl.program_id(0)

    @pl.when(s == 0)
    def _():
        ytok_s[N_TOK:, :] = jnp.zeros((MOE_DUMP, D_MODEL), F32)

    @pl.when(s < nblk_ref[0])
    def _():
        base = s * MOE_BM

        def put(r, c):
            ytok_s[pl.ds(perm_ref[base + r], 1), :] = y_ref[pl.ds(r, 1), :]
            return c

        lax.fori_loop(0, MOE_BM, put, 0, unroll=8)

    @pl.when(s >= MOE_NBLK)
    def _():
        i = s - MOE_NBLK
        o_ref[...] = x_ref[...] + gt_ref[0] * ytok_s[pl.ds(pl.multiple_of(i * ROW_BLK, ROW_BLK), ROW_BLK), :]


def _moe_combine(x, y_sorted, perm, nblk, mod, layer):
    tok = lambda s: jnp.maximum(s - MOE_NBLK, 0)
    grid_spec = pltpu.PrefetchScalarGridSpec(
        num_scalar_prefetch=2,
        grid=(MOE_NBLK + N_ROW_BLK,),
        in_specs=[pl.BlockSpec((MOE_BM, D_MODEL), lambda s, perm, nb: (jnp.minimum(s, nb[0] - 1), 0)),
                  pl.BlockSpec((ROW_BLK, D_MODEL), lambda s, perm, nb: (tok(s), 0)),
                  pl.BlockSpec((1, 1, D_MODEL),
                               lambda s, perm, nb: ((layer * MOD_ROWS + _mod_group(tok(s))) * N_MOD + 5, 0, 0))],
        out_specs=pl.BlockSpec((ROW_BLK, D_MODEL), lambda s, perm, nb: (tok(s), 0)),
        scratch_shapes=[pltpu.VMEM((N_TOK + MOE_DUMP, D_MODEL), F32)],
    )
    return pl.pallas_call(
        _moe_combine_kernel,
        grid_spec=grid_spec,
        out_shape=jax.ShapeDtypeStruct((N_TOK, D_MODEL), F32),
        compiler_params=_cparams("arbitrary"),
        name="moe_combine",
    )(perm, nblk, y_sorted, x, mod)


def _moe(x, mod, g, router_w, router_b, w_gate, w_up, w_down, layer):
    h, info, winfo, counts = _moe_route(x, mod, g, router_w, router_b, layer)
    cls = info[:, 0, :].reshape(N_TOK)
    rank = info[:, 1, :].reshape(N_TOK)
    w_lo = winfo[:, 0, :].reshape(N_TOK)
    w_hi = winfo[:, 1, :].reshape(N_TOK)
    cnt = counts[:N_CLASSES, 0]
    nb = (cnt + MOE_BM - 1) // MOE_BM
    ends = jnp.cumsum(nb)
    starts = ends - nb
    pos = ((starts * MOE_BM)[cls] + rank).astype(jnp.int32)
    blk = jnp.arange(MOE_NBLK, dtype=jnp.int32)
    blk_cls = jnp.minimum(jnp.sum((blk[:, None] >= ends[None, :]).astype(jnp.int32), axis=1), N_CLASSES - 1)
    pair_lo = jnp.asarray([0, 0, 0, 1, 1, 2], jnp.int32)
    pair_hi = jnp.asarray([1, 2, 3, 2, 3, 3], jnp.int32)
    grp = blk_cls // N_PAIRS
    blk_elo = (grp * EXPERTS_PER_GROUP + pair_lo[blk_cls % N_PAIRS]).astype(jnp.int32)
    blk_ehi = (grp * EXPERTS_PER_GROUP + pair_hi[blk_cls % N_PAIRS]).astype(jnp.int32)
    nblk = ends[-1:].astype(jnp.int32)
    hs, ws, perm = _moe_sort(h, pos, w_lo, w_hi, nblk)
    y_sorted = _moe_experts(hs, ws, blk_elo, blk_ehi, nblk, w_gate, w_up, w_down, layer)
    return _moe_combine(x, y_sorted, perm, nblk, mod, layer)


def _final_norm_kernel(x_ref, g_ref, o_ref):
    o_ref[...] = _rms(x_ref[...]) * g_ref[...]


def _final_norm(x, g, row0, rows):
    base = row0 // ROW_BLK
    return pl.pallas_call(
        _final_norm_kernel,
        grid=(rows // ROW_BLK,),
        in_specs=[pl.BlockSpec((ROW_BLK, D_MODEL), lambda i: (base + i, 0)), pl.BlockSpec((1, D_MODEL), lambda i: (0, 0))],
        out_specs=pl.BlockSpec((ROW_BLK, D_MODEL), lambda i: (i, 0)),
        out_shape=jax.ShapeDtypeStruct((rows, D_MODEL), F32),
        compiler_params=_cparams("arbitrary"),
        name="final_norm",
    )(x, g.reshape(1, D_MODEL))


def kernel(x_prompt, x_sample, cache_attn_k, cache_attn_v, state_ssm_re, state_ssm_im, cache_mla_ckv, cache_mla_kpe,
           state_hgrn, c, c_ctx, w_mod, b_mod, g_mix, g_ffn, g_final, router_w, router_b, moe_w_gate, moe_w_up,
           moe_w_down, ab_w_in, ab_sink, s5_lam_re, s5_lam_im, s5_log_dt, s5_b_re, s5_b_im, s5_c_re, s5_c_im, s5_d,
           s5_w_glu, ab_w_out, cd_w_in, mla_g_q, mla_w_q_up, mla_g_kv, mla_w_kv_up, hg_lower_bounds, hg_g_o, cd_w_out):
    x = jnp.concatenate([x_prompt.reshape(N_CTX_TOK, D_MODEL), x_sample.reshape(N_LAT_TOK, D_MODEL)], axis=0)
    cond = jnp.zeros((MOD_ROWS, D_MODEL), F32).at[0].set(c_ctx).at[1:1 + DEC_BATCH].set(c)
    mod = _modulation(cond, w_mod, b_mod)
    keep = ([], [], [], [], [], [], [])
    for l in range(DEPTH):
        j = l // 2
        if l % 2 == 0:
            q, k, v, u = _ab_in(x, mod, g_mix[l], ab_w_in[j], l)
            o_a = _attention_a(q, k, v, cache_attn_k[:, j].reshape(DEC_BATCH, PAST_LEN, A_KV_WIDTH),
                               cache_attn_v[:, j].reshape(DEC_BATCH, PAST_LEN, A_KV_WIDTH), ab_sink[j])
            y, fin_re, fin_im = _s5_scan(u, state_ssm_re[:, j], state_ssm_im[:, j], s5_lam_re[j], s5_lam_im[j],
                                         s5_log_dt[j], s5_b_re[j], s5_b_im[j], s5_c_re[j], s5_c_im[j])
            x = _ab_out(x, o_a, y, u, s5_d[j], s5_w_glu[j], ab_w_out[j], mod, l)
            keep[0].append(k[:N_CTX_TOK].reshape(BATCH, SEQ, A_KV_HEADS, HEAD_DIM))
            keep[1].append(v[:N_CTX_TOK].reshape(BATCH, SEQ, A_KV_HEADS, HEAD_DIM))
            keep[2].append(fin_re)
            keep[3].append(fin_im)
        else:
            qn, qr, ckv, kpe, kn, vm, hq, hff, hfb, hi, hg = _cd_in(
                x, mod, g_mix[l], cd_w_in[j], mla_g_q[j], mla_w_q_up[j], mla_g_kv[j], mla_w_kv_up[j], l)
            kcn, vc = _mla_cache_kv(cache_mla_ckv[:, j].reshape(DEC_BATCH * PAST_LEN, MLA_KV_RANK), mla_w_kv_up[j])
            o_c = _attention_mla(qn, qr, kn, kpe, vm, kcn, cache_mla_kpe[:, j], vc)
            s0_ctx = jnp.zeros((BATCH, 2, 2, HG_HALF, HG_HALF), F32)
            of_c, ob_c, s_fin = _hgrn_scan(hq, hff, hfb, hi, hg_lower_bounds, s0_ctx, l, 0, BATCH, SEQ)
            of_l, ob_l, _ = _hgrn_scan(hq, hff, hfb, hi, hg_lower_bounds, _hg_state_to_blocks(state_hgrn[:, j]), l,
                                       N_CTX_TOK, DEC_BATCH, DEC_SEQ)
            x = _cd_out(x, o_c, (of_c, of_l), (ob_c, ob_l), hg, hg_g_o[j], cd_w_out[j], mod, l)
            keep[4].append(ckv[:N_CTX_TOK].reshape(BATCH, SEQ, MLA_KV_RANK))
            keep[5].append(kpe[:N_CTX_TOK, :MLA_ROPE].reshape(BATCH, SEQ, MLA_ROPE))
            keep[6].append(_hg_state_from_blocks(s_fin))
        x = _moe(x, mod, g_ffn[l], router_w, router_b, moe_w_gate, moe_w_up, moe_w_down, l)
    y_ctx = _final_norm(x, g_final, 0, N_CTX_TOK)
    y_lat = _final_norm(x, g_final, N_CTX_TOK, N_LAT_TOK)
    return (y_ctx.reshape(BATCH, SEQ, D_MODEL), y_lat.reshape(DEC_BATCH, DEC_SEQ, D_MODEL),
            jnp.stack(keep[0], 1), jnp.stack(keep[1], 1), jnp.stack(keep[2], 1), jnp.stack(keep[3], 1),
            jnp.stack(keep[4], 1), jnp.stack(keep[5], 1), jnp.stack(keep[6], 1))
```

```python
import functools
import math

import numpy as np
import jax
import jax.numpy as jnp
from jax import lax
from jax.experimental import pallas as pl
from jax.experimental.pallas import tpu as pltpu

F32 = jnp.float32
BF16 = jnp.bfloat16

D_MODEL = 1024
BATCH = 16
SEQ = 256
DEPTH = 2
DEC_BATCH = 4
DEC_SEQ = 1024
PAST_LEN = 512
GRID_W = 64
N_MOD = 6
EPS = 1e-6
NEG_INF = -1e30
ROPE_BASE = 10000.0
HEAD_DIM = 64
A_HEADS = 8
A_KV_HEADS = 2
A_WINDOW = 128
A_WIDTH = A_HEADS * HEAD_DIM
A_KV_WIDTH = A_KV_HEADS * HEAD_DIM
S5_WIDTH = D_MODEL // 2
S5_GROUP = 16
S5_GROUPS = S5_WIDTH // S5_GROUP
S5_STATE = 64
MLA_HEADS = 8
MLA_Q_RANK = D_MODEL // 4
MLA_KV_RANK = D_MODEL // 8
MLA_NOPE = 64
MLA_ROPE = 32
MLA_V = 64
MLA_WIDTH = MLA_HEADS * MLA_V
HG_HEADS = 8
HG_KEY = 64
HG_VAL = 64
HG_KD = HG_HEADS * HG_KEY
HG_WIDTH = HG_HEADS * HG_VAL
HG_CHUNK = 16
N_EXPERTS = 16
N_GROUPS = 4
EXPERTS_PER_GROUP = N_EXPERTS // N_GROUPS
D_FF = D_MODEL // 2

N_CTX_TOK = BATCH * SEQ
N_LAT_TOK = DEC_BATCH * DEC_SEQ
N_TOK = N_CTX_TOK + N_LAT_TOK
ROW_BLK = 256
N_ROW_BLK = N_TOK // ROW_BLK
N_CTX_BLK = N_CTX_TOK // ROW_BLK
LAT_BLK_PER_SEQ = DEC_SEQ // ROW_BLK
MOD_ROWS = 8
VMEM_LIMIT = 56 * 1024 * 1024


def _cparams(*sem):
    return pltpu.CompilerParams(dimension_semantics=sem, vmem_limit_bytes=VMEM_LIMIT)


def _mod_group(i):
    return jnp.where(i < N_CTX_BLK, 0, 1 + (i - N_CTX_BLK) // LAT_BLK_PER_SEQ)


def _mod_spec(layer, which):
    return pl.BlockSpec((1, 1, D_MODEL), lambda i: ((layer * MOD_ROWS + _mod_group(i)) * N_MOD + which, 0, 0))


def _rope_blk(i):
    return jnp.where(i < N_CTX_BLK, 0, 1 + (i - N_CTX_BLK) % LAT_BLK_PER_SEQ)


def _rope_tables(rot_dim):
    n_freq = rot_dim // 4
    t = np.arange(DEC_SEQ)
    rows = (t // GRID_W).astype(np.float32)
    cols = (t % GRID_W).astype(np.float32)
    inv = (np.float32(ROPE_BASE) ** (-np.arange(n_freq, dtype=np.float32) / np.float32(n_freq))).astype(np.float32)
    ang_r = rows[:, None] * inv[None, :]
    ang_c = cols[:, None] * inv[None, :]
    ang = np.concatenate([ang_r, ang_r, ang_c, ang_c], axis=-1).astype(np.float32)
    reps = 128 // rot_dim
    cos = np.tile(np.cos(ang), (1, reps)).astype(np.float32)
    sin = np.tile(np.sin(ang), (1, reps)).astype(np.float32)
    lane = np.arange(128)
    first = (lane % (2 * n_freq)) < n_freq
    sin_a = np.where(first[None, :], -sin, 0.0).astype(np.float32)
    sin_b = np.where(first[None, :], 0.0, sin).astype(np.float32)
    ident = np.zeros((ROW_BLK, 128), np.float32)
    cos = np.concatenate([ident + 1.0, cos], axis=0)
    sin_a = np.concatenate([ident, sin_a], axis=0)
    sin_b = np.concatenate([ident, sin_b], axis=0)
    return jnp.asarray(cos), jnp.asarray(sin_a), jnp.asarray(sin_b)


def _apply_rope(x, cos, sin_a, sin_b, quarter):
    outs = []
    for j in range(x.shape[1] // 128):
        xt = x[:, 128 * j:128 * (j + 1)]
        up = pltpu.roll(xt, 128 - quarter, axis=1)
        dn = pltpu.roll(xt, quarter, axis=1)
        outs.append(xt * cos + up * sin_a + dn * sin_b)
    return outs[0] if len(outs) == 1 else jnp.concatenate(outs, axis=1)


def _rms(x):
    return x * lax.rsqrt(jnp.mean(x * x, axis=-1, keepdims=True) + EPS)


def _norm_mod(x, g, sc, sh):
    return _rms(x) * g * (1.0 + sc) + sh


def _dot(a, b):
    return jnp.dot(a.astype(BF16), b.astype(BF16), preferred_element_type=F32)


def _dot_nt(a, b):
    return lax.dot_general(a.astype(BF16), b.astype(BF16), (((1,), (1,)), ((), ())), preferred_element_type=F32)


def _mod_kernel(cond_ref, w_ref, b_ref, o_ref):
    c = cond_ref[...]
    s = c * jax.nn.sigmoid(c)
    o_ref[0] = _dot(s, w_ref[0]) + b_ref[0]


def _modulation(cond, w_mod, b_mod):
    nb = 1024
    out = pl.pallas_call(
        _mod_kernel,
        grid=(DEPTH, N_MOD * D_MODEL // nb),
        in_specs=[pl.BlockSpec((MOD_ROWS, D_MODEL), lambda l, n: (0, 0)),
                  pl.BlockSpec((1, D_MODEL, nb), lambda l, n: (l, 0, n)),
                  pl.BlockSpec((1, 1, nb), lambda l, n: (l, 0, n))],
        out_specs=pl.BlockSpec((1, MOD_ROWS, nb), lambda l, n: (l, 0, n)),
        out_shape=jax.ShapeDtypeStruct((DEPTH, MOD_ROWS, N_MOD * D_MODEL), F32),
        compiler_params=_cparams("arbitrary", "arbitrary"),
        name="modulation",
    )(cond, w_mod, b_mod.reshape(DEPTH, 1, N_MOD * D_MODEL))
    return out.reshape(DEPTH * MOD_ROWS * N_MOD, 1, D_MODEL)


def _ab_in_kernel(x_ref, g_ref, sc_ref, sh_ref, w_ref, cos_ref, sa_ref, sb_ref,
                  q_ref, k_ref, v_ref, u_ref, wq_s, wk_s, wv_s, wu_s):
    @pl.when(pl.program_id(0) == 0)
    def _():
        wq_s[...] = w_ref[:, 0:A_WIDTH].astype(BF16)
        wk_s[...] = w_ref[:, A_WIDTH:A_WIDTH + A_KV_WIDTH].astype(BF16)
        wv_s[...] = w_ref[:, A_WIDTH + A_KV_WIDTH:A_WIDTH + 2 * A_KV_WIDTH].astype(BF16)
        wu_s[...] = w_ref[:, A_WIDTH + 2 * A_KV_WIDTH:].astype(BF16)

    h = _norm_mod(x_ref[...], g_ref[...], sc_ref[0], sh_ref[0]).astype(BF16)
    cos, sa, sb = cos_ref[...], sa_ref[...], sb_ref[...]
    q = jnp.dot(h, wq_s[...], preferred_element_type=F32)
    q_ref[...] = _apply_rope(q, cos, sa, sb, HEAD_DIM // 4).astype(q_ref.dtype)
    k = jnp.dot(h, wk_s[...], preferred_element_type=F32)
    k_ref[...] = _apply_rope(k, cos, sa, sb, HEAD_DIM // 4)
    v_ref[...] = jnp.dot(h, wv_s[...], preferred_element_type=F32)
    u_ref[...] = jnp.dot(h, wu_s[...], preferred_element_type=F32)


def _ab_in(x, mod, g, w, layer):
    cos, sa, sb = _rope_tables(HEAD_DIM)
    d_in = w.shape[1]
    row = lambda n: pl.BlockSpec((ROW_BLK, n), lambda i: (i, 0))
    rope = pl.BlockSpec((ROW_BLK, 128), lambda i: (_rope_blk(i), 0))
    return pl.pallas_call(
        _ab_in_kernel,
        grid=(N_ROW_BLK,),
        in_specs=[row(D_MODEL), pl.BlockSpec((1, D_MODEL), lambda i: (0, 0)),
                  _mod_spec(layer, 1), _mod_spec(layer, 0),
                  pl.BlockSpec((D_MODEL, d_in), lambda i: (0, 0)), rope, rope, rope],
        out_specs=[row(A_WIDTH), row(A_KV_WIDTH), row(A_KV_WIDTH), row(S5_WIDTH)],
        out_shape=[jax.ShapeDtypeStruct((N_TOK, A_WIDTH), BF16), jax.ShapeDtypeStruct((N_TOK, A_KV_WIDTH), F32),
                   jax.ShapeDtypeStruct((N_TOK, A_KV_WIDTH), F32), jax.ShapeDtypeStruct((N_TOK, S5_WIDTH), F32)],
        scratch_shapes=[pltpu.VMEM((D_MODEL, A_WIDTH), BF16), pltpu.VMEM((D_MODEL, A_KV_WIDTH), BF16),
                        pltpu.VMEM((D_MODEL, A_KV_WIDTH), BF16), pltpu.VMEM((D_MODEL, S5_WIDTH), BF16)],
        compiler_params=_cparams("arbitrary"),
        name="ab_in",
    )(x, g.reshape(1, D_MODEL), mod, mod, w, cos, sa, sb)


def _softmax_pv(s_list, v_list, sink):
    m = functools.reduce(jnp.maximum, [jnp.max(s, axis=-1, keepdims=True) for s in s_list])
    if sink is not None:
        m = jnp.maximum(m, sink)
    ps = [jnp.exp(s - m) for s in s_list]
    l = functools.reduce(jnp.add, [jnp.sum(p, axis=-1, keepdims=True) for p in ps])
    if sink is not None:
        l = l + jnp.exp(sink - m)
    o = functools.reduce(jnp.add, [_dot(p, v) for p, v in zip(ps, v_list)])
    return o / l


def _gqa_with_sink(sink_ref, q, k_of, v_of, bias, o_ref):
    g = A_HEADS // A_KV_HEADS
    r = q.shape[0]
    outs = [None] * A_HEADS
    for kh in range(A_KV_HEADS):
        heads = range(g * kh, g * (kh + 1))
        qg = jnp.concatenate([q[:, HEAD_DIM * h:HEAD_DIM * (h + 1)] for h in heads], axis=0)
        sink = jnp.concatenate([jnp.full((r, 1), sink_ref[h], F32) for h in heads], axis=0)
        s = _dot_nt(qg, k_of(kh))
        if bias is not None:
            s = s + jnp.concatenate([bias] * g, axis=0)
        m = jnp.maximum(jnp.max(s, axis=-1, keepdims=True), sink)
        p = jnp.exp(s - m)
        l = jnp.sum(p, axis=-1, keepdims=True) + jnp.exp(sink - m)
        o = _dot(p, v_of(kh)) / l
        for n, h in enumerate(heads):
            outs[h] = o[r * n:r * (n + 1)]
    o_ref[...] = jnp.concatenate(outs, axis=1).astype(o_ref.dtype)


def _attn_ctx_kernel(sink_ref, q_ref, k_ref, v_ref, o_ref):
    scale = HEAD_DIM ** -0.5
    g = A_HEADS // A_KV_HEADS
    outs = []
    for h in range(A_HEADS):
        kh = h // g
        q = q_ref[:, HEAD_DIM * h:HEAD_DIM * (h + 1)]
        k = k_ref[:, HEAD_DIM * kh:HEAD_DIM * (kh + 1)]
        v = v_ref[:, HEAD_DIM * kh:HEAD_DIM * (kh + 1)]
        s = _dot_nt(q, k) * scale
        outs.append(_softmax_pv([s], [v], sink_ref[h]))
    o_ref[...] = jnp.concatenate(outs, axis=1).astype(o_ref.dtype)


def _attn_lat_kernel(sink_ref, q_ref, kp_ref, kc_ref, kn_ref, vp_ref, vc_ref, vn_ref, kx_ref, vx_ref, o_ref):
    n = pl.program_id(1)
    nb = DEC_SEQ // A_WINDOW
    i = lax.broadcasted_iota(jnp.int32, (A_WINDOW, A_WINDOW), 0)
    j = lax.broadcasted_iota(jnp.int32, (A_WINDOW, A_WINDOW), 1)
    zero = jnp.zeros((A_WINDOW, A_WINDOW), F32)
    bias = jnp.concatenate([jnp.where((j >= i) & (n > 0), 0.0, NEG_INF), zero,
                            jnp.where((j <= i) & (n < nb - 1), 0.0, NEG_INF),
                            jnp.zeros((A_WINDOW, PAST_LEN), F32)], axis=1)

    def rows(p_ref, c_ref, n_ref, x_ref):
        def of(kh):
            sl = slice(HEAD_DIM * kh, HEAD_DIM * (kh + 1))
            return jnp.concatenate([p_ref[:, sl], c_ref[:, sl], n_ref[:, sl], x_ref[0, :, sl]], axis=0)
        return of

    _gqa_with_sink(sink_ref, q_ref[...] * HEAD_DIM ** -0.5, rows(kp_ref, kc_ref, kn_ref, kx_ref),
                   rows(vp_ref, vc_ref, vn_ref, vx_ref), bias, o_ref)


def _attention_a(q, k, v, cache_k, cache_v, sink):
    smem = pl.BlockSpec(memory_space=pltpu.SMEM)
    o = pl.pallas_call(
        _attn_ctx_kernel,
        grid=(BATCH,),
        in_specs=[smem, pl.BlockSpec((SEQ, A_WIDTH), lambda b: (b, 0)),
                  pl.BlockSpec((SEQ, A_KV_WIDTH), lambda b: (b, 0)), pl.BlockSpec((SEQ, A_KV_WIDTH), lambda b: (b, 0))],
        out_specs=pl.BlockSpec((SEQ, A_WIDTH), lambda b: (b, 0)),
        out_shape=jax.ShapeDtypeStruct((N_CTX_TOK, A_WIDTH), BF16),
        compiler_params=_cparams("arbitrary"),
        name="attn_a_ctx",
    )(sink, q, k, v)
    nb = DEC_SEQ // A_WINDOW
    base = N_CTX_TOK // A_WINDOW
    cur = lambda b, n: (base + b * nb + n, 0)
    prev = lambda b, n: (base + b * nb + jnp.maximum(n - 1, 0), 0)
    nxt = lambda b, n: (base + b * nb + jnp.minimum(n + 1, nb - 1), 0)
    kv = lambda f: pl.BlockSpec((A_WINDOW, A_KV_WIDTH), f)
    cache = pl.BlockSpec((1, PAST_LEN, A_KV_WIDTH), lambda b, n: (b, 0, 0))
    o_lat = pl.pallas_call(
        _attn_lat_kernel,
        grid=(DEC_BATCH, nb),
        in_specs=[smem, pl.BlockSpec((A_WINDOW, A_WIDTH), cur), kv(prev), kv(cur), kv(nxt), kv(prev), kv(cur), kv(nxt),
                  cache, cache],
        out_specs=pl.BlockSpec((A_WINDOW, A_WIDTH), lambda b, n: (b * nb + n, 0)),
        out_shape=jax.ShapeDtypeStruct((N_LAT_TOK, A_WIDTH), BF16),
        compiler_params=_cparams("arbitrary", "arbitrary"),
        name="attn_a_lat",
    )(sink, q, k, k, k, v, v, v, cache_k, cache_v)
    return o, o_lat


S5_CHUNK = 16
S5_OCT = 128 // S5_GROUP
S5_NOCT = S5_GROUPS // S5_OCT
S5_K = S5_CHUNK * 128
S5_PART = S5_OCT * S5_STATE
S5_SW = 4 * S5_PART
S5_ROWS_CTX = BATCH * SEQ // S5_CHUNK
S5_ROWS_LAT = DEC_BATCH * DEC_SEQ // S5_CHUNK
S5_ROWS = S5_ROWS_CTX + S5_ROWS_LAT
S5_NB = 4
HI = lax.Precision.HIGHEST


def _s5_disc_kernel(lr_ref, li_ref, ldt_ref, ar_ref, ai_ref, zr_ref, zi_ref):
    lr, li = lr_ref[...], li_ref[...]
    dt = jnp.exp(ldt_ref[...])
    mag = jnp.exp(lr * dt)
    ar, ai = mag * jnp.cos(li * dt), mag * jnp.sin(li * dt)
    den = lr * lr + li * li
    ar_ref[...] = ar
    ai_ref[...] = ai
    zr_ref[...] = ((ar - 1.0) * lr + ai * li) / den
    zi_ref[...] = (ai * lr - (ar - 1.0) * li) / den


def _cmul(xr, xi, yr, yi):
    return xr * yr - xi * yi, xr * yi + xi * yr


def _dot_nt_hi(a, b):
    return lax.dot_general(a, b, (((1,), (1,)), ((), ())), precision=HI, preferred_element_type=F32)


def _s5_prep_kernel(ar_ref, ai_ref, zr_ref, zi_ref, btr_ref, bti_ref, ctr_ref, cti_ref,
                    m_ref, win_ref, wout_ref, a16_ref, pw_s, w_s, k_s):
    t = pl.program_id(1)
    npw = S5_CHUNK + 1
    blk = lambda j: pl.ds(pl.multiple_of(j * 128, 128), 128)

    @pl.when(t == 0)
    def _():
        kd = []
        for d in range(2):
            ar, ai = ar_ref[0, d], ai_ref[0, d]
            pr, pi = jnp.ones_like(ar), jnp.zeros_like(ar)
            bbr, bbi = _cmul(zr_ref[0, d], zi_ref[0, d], btr_ref[0, d], bti_ref[0, d])
            for j in range(npw):
                pw_s[d, 0, j:j + 1, :] = pr
                pw_s[d, 1, j:j + 1, :] = pi
                if j < S5_CHUNK:
                    wr, wi = _cmul(pr, pi, bbr, bbi)
                    w_s[d, 0, 128 * j:128 * (j + 1), :] = wr.astype(BF16)
                    w_s[d, 1, 128 * j:128 * (j + 1), :] = wi.astype(BF16)
                pr, pi = _cmul(pr, pi, ar, ai)
            kd.append(_dot_nt(w_s[d, 0], ctr_ref[0, d]) - _dot_nt(w_s[d, 1], cti_ref[0, d]))
        for jj in range(2 * S5_CHUNK - 1):
            j = jj - (S5_CHUNK - 1)
            if j > 0:
                k = kd[0][128 * j:128 * (j + 1)]
            elif j < 0:
                k = kd[1][128 * -j:128 * (1 - j)]
            else:
                k = kd[0][0:128] + kd[1][0:128]
            k_s[128 * jj:128 * (jj + 1), :] = k
        a16_ref[0] = jnp.concatenate([pw_s[0, 0, S5_CHUNK:npw, :], pw_s[1, 0, S5_CHUNK:npw, :],
                                      pw_s[0, 1, S5_CHUNK:npw, :], pw_s[1, 1, S5_CHUNK:npw, :]], axis=1)

    for tp in range(S5_CHUNK):
        m_ref[0, :, 128 * tp:128 * (tp + 1)] = k_s[blk(S5_CHUNK - 1 + tp - t), :].astype(BF16)

    def power(d, e):
        return pw_s[d, 0, pl.ds(e, 1), :], pw_s[d, 1, pl.ds(e, 1), :]

    for d in range(2):
        j = (S5_CHUNK - 1 - t) if d == 0 else t
        win_ref[0, :, S5_PART * d:S5_PART * (d + 1)] = w_s[d, 0, blk(j), :]
        win_ref[0, :, S5_PART * (2 + d):S5_PART * (3 + d)] = w_s[d, 1, blk(j), :]
        er, ei = _cmul(*power(d, (t + 1) if d == 0 else (S5_CHUNK - t)), ctr_ref[0, d], cti_ref[0, d])
        wout_ref[0, :, S5_PART * d:S5_PART * (d + 1)] = er.astype(BF16)
        wout_ref[0, :, S5_PART * (2 + d):S5_PART * (3 + d)] = (-ei).astype(BF16)


def _s5_main_kernel(u_ref, win_ref, m_ref, wout_ref, a16_ref, h0_ref, y_ref, hfin_ref, uo_s, x_s, hs_s):
    s = pl.program_id(1)
    nq = S5_PART // 128

    @pl.when(s == 0)
    def _():
        for t in range(S5_CHUNK):
            uo_s[:, 128 * t:128 * (t + 1)] = u_ref[pl.ds(t, S5_ROWS, stride=S5_CHUNK), :].astype(BF16)

    @pl.when(s < S5_NB)
    def _():
        x = jnp.dot(uo_s[...], win_ref[0], preferred_element_type=F32)
        for q in range(nq):
            x_s[s * nq + q] = x[:, 128 * q:128 * (q + 1)]

    @pl.when(s == S5_NB - 1)
    def _():
        def run(row0, nb, nc, h):
            for c in range(nc):
                for d in range(2):
                    cc = c if d == 0 else nc - 1 - c
                    rows = pl.ds(row0 + cc, nb, stride=nc)
                    for q in range(nq):
                        kr, ki = d * nq + q, (2 + d) * nq + q
                        hr, hi = h[d][0][q], h[d][1][q]
                        hs_s[kr, rows, :] = hr
                        hs_s[ki, rows, :] = hi
                        ar = a16_ref[0, :, 128 * kr:128 * (kr + 1)]
                        ai = a16_ref[0, :, 128 * ki:128 * (ki + 1)]
                        h[d][0][q] = ar * hr - ai * hi + x_s[kr, rows, :]
                        h[d][1][q] = ar * hi + ai * hr + x_s[ki, rows, :]
            return h

        zero = jnp.zeros((BATCH, 128), F32)
        fin = run(0, BATCH, SEQ // S5_CHUNK, [[[zero] * nq, [zero] * nq] for _ in range(2)])
        for d in range(2):
            for ri in range(2):
                for q in range(nq):
                    k = (2 * ri + d) * nq + q
                    hfin_ref[0, :, 128 * k:128 * (k + 1)] = fin[d][ri][q]
        h0 = [[[h0_ref[0, :, 128 * ((2 * ri + d) * nq + q):128 * ((2 * ri + d) * nq + q + 1)] for q in range(nq)]
               for ri in range(2)] for d in range(2)]
        run(S5_ROWS_CTX, DEC_BATCH, DEC_SEQ // S5_CHUNK, h0)

    @pl.when(s >= S5_NB)
    def _():
        hs = jnp.concatenate([hs_s[k] for k in range(4 * nq)], axis=1).astype(BF16)
        y = (jnp.dot(uo_s[...], m_ref[0], preferred_element_type=F32)
             + lax.dot_general(hs, wout_ref[0], (((1,), (1,)), ((), ())), preferred_element_type=F32))
        for q in range(nq):
            t = (s - S5_NB) * nq + q
            y_ref[pl.ds(t, S5_ROWS, stride=S5_CHUNK), :] = y[:, 128 * q:128 * (q + 1)]


def _s5_octets(t, lanes):
    return t.reshape(2, S5_NOCT, 1, S5_OCT * lanes).transpose(1, 0, 2, 3)


def _s5_blockdiag(t):
    a, n = t.shape[2], t.shape[3]
    t = t.reshape(2, S5_NOCT, S5_OCT, a, n)
    bd = jnp.einsum('dogan,gh->dogahn', t, jnp.eye(S5_OCT, dtype=t.dtype))
    return bd.reshape(2, S5_NOCT, S5_OCT * a, S5_OCT * n).transpose(1, 0, 2, 3)


def _s5_state_to_lanes(h_re, h_im):
    b = h_re.shape[0]
    parts = jnp.stack([h_re[:, 0], h_re[:, 1], h_im[:, 0], h_im[:, 1]], axis=1)
    parts = parts.reshape(b, 4, S5_NOCT, S5_PART).transpose(2, 0, 1, 3)
    return parts.reshape(S5_NOCT, b, S5_SW)


def _s5_state_from_lanes(h):
    b = h.shape[1]
    parts = h.reshape(S5_NOCT, b, 4, S5_OCT, S5_STATE).transpose(1, 2, 0, 3, 4).reshape(b, 4, S5_GROUPS, S5_STATE)
    return parts[:, 0:2], parts[:, 2:4]


def _s5_scan(u, h0_re, h0_im, lam_re, lam_im, log_dt, b_re, b_im, c_re, c_im):
    ng, n = S5_GROUPS, S5_STATE
    rows = 2 * ng
    disc = pl.pallas_call(
        _s5_disc_kernel,
        out_shape=[jax.ShapeDtypeStruct((rows, n), F32)] * 4,
        name="s5_disc",
    )(lam_re.reshape(rows, n), lam_im.reshape(rows, n), log_dt.reshape(rows, 1))
    ar, ai, zr, zi = [_s5_octets(t.reshape(2, ng, n), n) for t in disc]
    bt = lambda t: _s5_blockdiag(t.transpose(0, 1, 3, 2))
    vec = pl.BlockSpec((1, 2, 1, S5_PART), lambda o, t: (o, 0, 0, 0))
    mat = pl.BlockSpec((1, 2, 128, S5_PART), lambda o, t: (o, 0, 0, 0))
    rowblk = lambda w: pl.BlockSpec((1, 128, w), lambda o, t: (o, t, 0))
    m, win, wout, a16 = pl.pallas_call(
        _s5_prep_kernel,
        grid=(S5_NOCT, S5_CHUNK),
        in_specs=[vec, vec, vec, vec, mat, mat, mat, mat],
        out_specs=[rowblk(S5_K), rowblk(S5_SW), rowblk(S5_SW), pl.BlockSpec((1, 1, S5_SW), lambda o, t: (o, 0, 0))],
        out_shape=[jax.ShapeDtypeStruct((S5_NOCT, S5_K, S5_K), BF16), jax.ShapeDtypeStruct((S5_NOCT, S5_K, S5_SW), BF16),
                   jax.ShapeDtypeStruct((S5_NOCT, S5_K, S5_SW), BF16), jax.ShapeDtypeStruct((S5_NOCT, 1, S5_SW), F32)],
        scratch_shapes=[pltpu.VMEM((2, 2, 24, S5_PART), F32), pltpu.VMEM((2, 2, S5_K, S5_PART), BF16),
                        pltpu.VMEM(((2 * S5_CHUNK - 1) * 128, 128), F32)],
        compiler_params=_cparams("arbitrary", "arbitrary"),
        name="s5_prep",
    )(ar, ai, zr, zi, bt(b_re), bt(b_im), _s5_blockdiag(c_re), _s5_blockdiag(c_im))
    nb = S5_NB
    y, hfin = pl.pallas_call(
        _s5_main_kernel,
        grid=(S5_NOCT, 2 * nb),
        in_specs=[pl.BlockSpec((N_TOK, 128), lambda o, s: (0, o)),
                  pl.BlockSpec((1, S5_K, S5_PART), lambda o, s: (o, 0, jnp.minimum(s, nb - 1))),
                  pl.BlockSpec((1, S5_K, S5_PART), lambda o, s: (o, 0, jnp.maximum(s - nb, 0))),
                  pl.BlockSpec((1, S5_PART, S5_SW), lambda o, s: (o, jnp.maximum(s - nb, 0), 0)),
                  pl.BlockSpec((1, 1, S5_SW), lambda o, s: (o, 0, 0)),
                  pl.BlockSpec((1, DEC_BATCH, S5_SW), lambda o, s: (o, 0, 0))],
        out_specs=[pl.BlockSpec((N_TOK, 128), lambda o, s: (0, o)),
                   pl.BlockSpec((1, BATCH, S5_SW), lambda o, s: (o, 0, 0))],
        out_shape=[jax.ShapeDtypeStruct((N_TOK, S5_WIDTH), F32), jax.ShapeDtypeStruct((S5_NOCT, BATCH, S5_SW), F32)],
        scratch_shapes=[pltpu.VMEM((S5_ROWS, S5_K), BF16), pltpu.VMEM((S5_SW // 128, S5_ROWS, 128), F32),
                        pltpu.VMEM((S5_SW // 128, S5_ROWS, 128), F32)],
        compiler_params=_cparams("arbitrary", "arbitrary"),
        name="s5_main",
    )(u, win, m, wout, a16, _s5_state_to_lanes(h0_re, h0_im))
    fin_re, fin_im = _s5_state_from_lanes(hfin)
    return y, fin_re, fin_im


def _ctx_or_lat(ctx_ref, lat_ref):
    return jnp.where(pl.program_id(0) < N_CTX_BLK, ctx_ref[...], lat_ref[...])


def _split_specs(width):
    return [pl.BlockSpec((ROW_BLK, width), lambda i: (jnp.minimum(i, N_CTX_BLK - 1), 0)),
            pl.BlockSpec((ROW_BLK, width), lambda i: (jnp.maximum(i - N_CTX_BLK, 0), 0))]


def _ab_out_kernel(x_ref, oac_ref, oal_ref, y_ref, u_ref, d_ref, wglu_ref, wout_ref, gt_ref, o_ref, wglu_s, wa_s, wb_s):
    @pl.when(pl.program_id(0) == 0)
    def _():
        wglu_s[...] = wglu_ref[...].astype(BF16)
        wa_s[...] = wout_ref[0:A_WIDTH, :].astype(BF16)
        wb_s[...] = wout_ref[A_WIDTH:, :].astype(BF16)

    g = jax.nn.gelu(y_ref[...] + d_ref[...] * u_ref[...])
    ob = g * jax.nn.sigmoid(jnp.dot(g.astype(BF16), wglu_s[...], preferred_element_type=F32))
    out = (jnp.dot(_ctx_or_lat(oac_ref, oal_ref).astype(BF16), wa_s[...], preferred_element_type=F32)
           + jnp.dot(ob.astype(BF16), wb_s[...], preferred_element_type=F32))
    o_ref[...] = x_ref[...] + gt_ref[0] * out


def _ab_out(x, oa, y, u, d_skip, w_glu, w_out, mod, layer):
    row = lambda n: pl.BlockSpec((ROW_BLK, n), lambda i: (i, 0))
    full = lambda a, b: pl.BlockSpec((a, b), lambda i: (0, 0))
    return pl.pallas_call(
        _ab_out_kernel,
        grid=(N_ROW_BLK,),
        in_specs=[row(D_MODEL)] + _split_specs(A_WIDTH) + [row(S5_WIDTH), row(S5_WIDTH), full(1, S5_WIDTH),
                  full(S5_WIDTH, S5_WIDTH), full(A_WIDTH + S5_WIDTH, D_MODEL), _mod_spec(layer, 2)],
        out_specs=row(D_MODEL),
        out_shape=jax.ShapeDtypeStruct((N_TOK, D_MODEL), F32),
        scratch_shapes=[pltpu.VMEM((S5_WIDTH, S5_WIDTH), BF16), pltpu.VMEM((A_WIDTH, D_MODEL), BF16),
                        pltpu.VMEM((S5_WIDTH, D_MODEL), BF16)],
        compiler_params=_cparams("arbitrary"),
        name="ab_out",
    )(x, oa[0], oa[1], y, u, d_skip.reshape(1, S5_WIDTH), w_glu, w_out, mod)


CD_HG0 = MLA_Q_RANK + MLA_KV_RANK + MLA_ROPE
KPE_LANES = 128


def _cd_in_kernel(x_ref, g_ref, sc_ref, sh_ref, w_ref, gq_ref, wqu_ref, gkv_ref, wkvu_ref, cos_ref, sa_ref, sb_ref,
                  qn_ref, qr_ref, ckv_ref, kpe_ref, kn_ref, vm_ref, hq_ref, hff_ref, hfb_ref, hi_ref, hg_ref,
                  wcq_s, wckv_s, wkpe_s, whg_s, wqu_s, wkvu_s):
    @pl.when(pl.program_id(0) == 0)
    def _():
        wcq_s[...] = w_ref[:, 0:MLA_Q_RANK].astype(BF16)
        wckv_s[...] = w_ref[:, MLA_Q_RANK:MLA_Q_RANK + MLA_KV_RANK].astype(BF16)
        kp = w_ref[:, MLA_Q_RANK + MLA_KV_RANK:CD_HG0].astype(BF16)
        wkpe_s[...] = jnp.concatenate([kp] * (KPE_LANES // MLA_ROPE), axis=1)
        whg_s[...] = w_ref[:, CD_HG0:].astype(BF16)
        wqu_s[...] = wqu_ref[...].astype(BF16)
        wkvu_s[...] = wkvu_ref[...].astype(BF16)

    h = _norm_mod(x_ref[...], g_ref[...], sc_ref[0], sh_ref[0]).astype(BF16)
    cos, sa, sb = cos_ref[...], sa_ref[...], sb_ref[...]
    nope = MLA_HEADS * MLA_NOPE
    cq = _rms(jnp.dot(h, wcq_s[...], preferred_element_type=F32)) * gq_ref[...]
    qq = jnp.dot(cq.astype(BF16), wqu_s[...], preferred_element_type=F32)
    qn_ref[...] = qq[:, 0:nope].astype(qn_ref.dtype)
    qr_ref[...] = _apply_rope(qq[:, nope:], cos, sa, sb, MLA_ROPE // 4).astype(qr_ref.dtype)
    ckv = _rms(jnp.dot(h, wckv_s[...], preferred_element_type=F32)) * gkv_ref[...]
    ckv_ref[...] = ckv
    kv = jnp.dot(ckv.astype(BF16), wkvu_s[...], preferred_element_type=F32)
    kn_ref[...] = kv[:, 0:nope].astype(kn_ref.dtype)
    vm_ref[...] = kv[:, nope:].astype(vm_ref.dtype)
    kpe_ref[...] = _apply_rope(jnp.dot(h, wkpe_s[...], preferred_element_type=F32), cos, sa, sb, MLA_ROPE // 4)
    hh = jnp.dot(h, whg_s[...], preferred_element_type=F32)
    for n, ref in enumerate((hq_ref, hff_ref, hfb_ref, hi_ref, hg_ref)):
        ref[...] = hh[:, HG_KD * n:HG_KD * (n + 1)]


def _mla_split_heads(w, a):
    k, n = w.shape
    w = w.reshape(k, MLA_HEADS, n // MLA_HEADS)
    return jnp.concatenate([w[:, :, :a].reshape(k, -1), w[:, :, a:].reshape(k, -1)], axis=1)


def _cd_in(x, mod, g, w, g_q, w_q_up, g_kv, w_kv_up, layer):
    cos, sa, sb = _rope_tables(MLA_ROPE)
    d_in = w.shape[1]
    row = lambda n: pl.BlockSpec((ROW_BLK, n), lambda i: (i, 0))
    full = lambda a, b: pl.BlockSpec((a, b), lambda i: (0, 0))
    rope = pl.BlockSpec((ROW_BLK, 128), lambda i: (_rope_blk(i), 0))
    widths = [MLA_HEADS * MLA_NOPE, MLA_HEADS * MLA_ROPE, MLA_KV_RANK, KPE_LANES, MLA_HEADS * MLA_NOPE, MLA_WIDTH] + [HG_KD] * 5
    nq = MLA_HEADS * (MLA_NOPE + MLA_ROPE)
    nkv = MLA_HEADS * (MLA_NOPE + MLA_V)
    return pl.pallas_call(
        _cd_in_kernel,
        grid=(N_ROW_BLK,),
        in_specs=[row(D_MODEL), full(1, D_MODEL), _mod_spec(layer, 1), _mod_spec(layer, 0), full(D_MODEL, d_in),
                  full(1, MLA_Q_RANK), full(MLA_Q_RANK, nq), full(1, MLA_KV_RANK), full(MLA_KV_RANK, nkv), rope, rope, rope],
        out_specs=[row(n) for n in widths],
        out_shape=[jax.ShapeDtypeStruct((N_TOK, n), BF16 if i in (0, 1, 4, 5) else F32) for i, n in enumerate(widths)],
        scratch_shapes=[pltpu.VMEM((D_MODEL, MLA_Q_RANK), BF16), pltpu.VMEM((D_MODEL, MLA_KV_RANK), BF16),
                        pltpu.VMEM((D_MODEL, KPE_LANES), BF16), pltpu.VMEM((D_MODEL, 5 * HG_KD), BF16),
                        pltpu.VMEM((MLA_Q_RANK, nq), BF16), pltpu.VMEM((MLA_KV_RANK, nkv), BF16)],
        compiler_params=_cparams("arbitrary"),
        name="cd_in",
    )(x, g.reshape(1, D_MODEL), mod, mod, w, g_q.reshape(1, -1), _mla_split_heads(w_q_up, MLA_NOPE),
      g_kv.reshape(1, -1), _mla_split_heads(w_kv_up, MLA_NOPE), cos, sa, sb)


def _mm_kernel(a_ref, w_ref, *o_refs):
    r = _dot(a_ref[...], w_ref[...])
    off = 0
    for o in o_refs:
        o[...] = r[:, off:off + o.shape[1]]
        off += o.shape[1]


def _mla_cache_kv(cckv, w_kv_up):
    n = MLA_HEADS * MLA_NOPE
    rows = cckv.shape[0]
    return pl.pallas_call(
        _mm_kernel,
        grid=(rows // PAST_LEN,),
        in_specs=[pl.BlockSpec((PAST_LEN, MLA_KV_RANK), lambda i: (i, 0)),
                  pl.BlockSpec((MLA_KV_RANK, 2 * n), lambda i: (0, 0))],
        out_specs=[pl.BlockSpec((PAST_LEN, n), lambda i: (i, 0))] * 2,
        out_shape=[jax.ShapeDtypeStruct((rows, n), F32)] * 2,
        compiler_params=_cparams("arbitrary"),
        name="mla_cache_kv",
    )(cckv, _mla_split_heads(w_kv_up, MLA_NOPE))


def _mla_heads(qn, qr, keys, o_ref):
    scale = (MLA_NOPE + MLA_ROPE) ** -0.5
    outs = []
    for h in range(MLA_HEADS):
        a = slice(MLA_NOPE * h, MLA_NOPE * (h + 1))
        r = slice(MLA_ROPE * h, MLA_ROPE * (h + 1))
        s_list = [(_dot_nt(qn[:, a], kn[:, a]) + _dot_nt(qr[:, r], kp[:, 0:MLA_ROPE])) * scale for kn, kp, _ in keys]
        outs.append(_softmax_pv(s_list, [v[:, a] for _, _, v in keys], None))
    o_ref[...] = jnp.concatenate(outs, axis=1).astype(o_ref.dtype)


def _mla_ctx_kernel(qn_ref, qr_ref, kn_ref, kp_ref, v_ref, o_ref):
    _mla_heads(qn_ref[...], qr_ref[...], [(kn_ref[...], kp_ref[...], v_ref[...])], o_ref)


def _mla_lat_kernel(qn_ref, qr_ref, kn_ref, kp_ref, v_ref, kcn_ref, kcp_ref, vc_ref, o_ref):
    _mla_heads(qn_ref[...], qr_ref[...],
               [(kcn_ref[...], kcp_ref[0], vc_ref[...]), (kn_ref[...], kp_ref[...], v_ref[...])], o_ref)


def _attention_mla(qn, qr, kn, kpe, vm, kcn, kcpe, vc):
    n = MLA_HEADS * MLA_NOPE
    nr = MLA_HEADS * MLA_ROPE
    blk = lambda w: pl.BlockSpec((SEQ, w), lambda b: (b, 0))
    o = pl.pallas_call(
        _mla_ctx_kernel,
        grid=(BATCH,),
        in_specs=[blk(n), blk(nr), blk(n), blk(KPE_LANES), blk(MLA_WIDTH)],
        out_specs=blk(MLA_WIDTH),
        out_shape=jax.ShapeDtypeStruct((N_CTX_TOK, MLA_WIDTH), BF16),
        compiler_params=_cparams("arbitrary"),
        name="mla_ctx",
    )(qn, qr, kn, kpe, vm)
    nq = DEC_SEQ // ROW_BLK
    qblk = lambda w: pl.BlockSpec((ROW_BLK, w), lambda b, i: (N_CTX_BLK + b * nq + i, 0))
    seq = lambda w: pl.BlockSpec((DEC_SEQ, w), lambda b, i: (N_CTX_TOK // DEC_SEQ + b, 0))
    past = lambda w: pl.BlockSpec((PAST_LEN, w), lambda b, i: (b, 0))
    o_lat = pl.pallas_call(
        _mla_lat_kernel,
        grid=(DEC_BATCH, nq),
        in_specs=[qblk(n), qblk(nr), seq(n), seq(KPE_LANES), seq(MLA_WIDTH), past(n),
                  pl.BlockSpec((1, PAST_LEN, MLA_ROPE), lambda b, i: (b, 0, 0)), past(MLA_WIDTH)],
        out_specs=pl.BlockSpec((ROW_BLK, MLA_WIDTH), lambda b, i: (b * nq + i, 0)),
        out_shape=jax.ShapeDtypeStruct((N_LAT_TOK, MLA_WIDTH), BF16),
        compiler_params=_cparams("arbitrary", "arbitrary"),
        name="mla_lat",
    )(qn, qr, kn, kpe, vm, kcn, kcpe, vc)
    return o, o_lat


HG_TILE = 128
HG_NC = HG_TILE // HG_CHUNK
HG_HALF = 256


HG_SLABS = HG_KD // 128


def _hg_put(ref, x):
    for j in range(HG_SLABS):
        ref[j] = x[:, 128 * j:128 * (j + 1)]


def _hg_get(ref, r, l):
    return jnp.concatenate([ref[2 * l, r, :], ref[2 * l + 1, r, :]], axis=1)


def _hg_plane(ref, p):
    return jnp.concatenate([ref[j, pl.ds(p, HG_NC, stride=HG_CHUNK), :] for j in range(HG_SLABS)], axis=1)


def _hg_put_plane(ref, p, x):
    for j in range(HG_SLABS):
        ref[j, pl.ds(p, HG_NC, stride=HG_CHUNK), :] = x[:, 128 * j:128 * (j + 1)]


def _hg_direction(hq_ref, hf_ref, hi_ref, o_ref, lb, st_ref, q_s, f_s, kk_s, v_s, qt_s, kt_s, o_s, sign,
                  ones_bd, diag_mask):
    hq = hq_ref[...]
    _hg_put(q_s, hq * jax.nn.sigmoid(hq))
    f = lb + (1.0 - lb) * jax.nn.sigmoid(hf_ref[...])
    _hg_put(f_s, f)
    _hg_put(kk_s, 1.0 - f)
    _hg_put(v_s, hi_ref[...])
    pos = (lambda i: i) if sign > 0 else (lambda i: HG_CHUNK - 1 - i)
    q = [_hg_plane(q_s, pos(i)) for i in range(HG_CHUNK)]
    fp = [_hg_plane(f_s, pos(i)) for i in range(HG_CHUNK)]
    kk = [_hg_plane(kk_s, pos(i)) for i in range(HG_CHUNK)]
    v = [_hg_plane(v_s, pos(i)) for i in range(HG_CHUNK)]

    def expand(planes):
        pb = jnp.concatenate(planes, axis=0).astype(BF16)
        e = jnp.concatenate([jnp.dot(pb[:, 0:HG_HALF], ones_bd, preferred_element_type=F32),
                             jnp.dot(pb[:, HG_HALF:], ones_bd, preferred_element_type=F32)], axis=1)
        return [e[HG_NC * n:HG_NC * (n + 1)] for n in range(len(planes))]

    o = [e * v[i] for i, e in enumerate(expand([q[i] * kk[i] for i in range(HG_CHUNK)]))]
    dec = list(fp)
    for d in range(1, HG_CHUNK):
        if d > 1:
            for i in range(d, HG_CHUNK):
                dec[i] = dec[i] * fp[i - d + 1]
        es = expand([q[i] * kk[i - d] * dec[i] for i in range(d, HG_CHUNK)])
        for n, i in enumerate(range(d, HG_CHUNK)):
            o[i] = o[i] + es[n] * v[i - d]
    for i in range(HG_CHUNK):
        _hg_put_plane(o_s, pos(i), o[i])

    incl = [fp[0]]
    for i in range(1, HG_CHUNK):
        incl.append(incl[-1] * fp[i])
    excl = [None] * HG_CHUNK
    excl[HG_CHUNK - 1] = jnp.ones_like(fp[0])
    for i in range(HG_CHUNK - 2, -1, -1):
        excl[i] = excl[i + 1] * fp[i + 1]
    for i in range(HG_CHUNK):
        _hg_put_plane(qt_s, pos(i), q[i] * incl[i])
        _hg_put_plane(kt_s, pos(i), kk[i] * excl[i])
    whole = incl[HG_CHUNK - 1]
    for cc in range(HG_NC):
        c = cc if sign > 0 else HG_NC - 1 - cc
        r = slice(HG_CHUNK * c, HG_CHUNK * (c + 1))
        for g in range(2):
            l = slice(HG_HALF * g, HG_HALF * (g + 1))
            st = st_ref[g]
            o_ref[r, l] = _hg_get(o_s, r, g) + _dot_nt(_hg_get(qt_s, r, g), st)
            ds = lax.dot_general(hi_ref[r, l].astype(BF16), _hg_get(kt_s, r, g).astype(BF16),
                                 (((0,), (0,)), ((), ())), preferred_element_type=F32)
            st_ref[g] = st * whole[c:c + 1, l] + jnp.where(diag_mask, ds, 0.0)


def _hg_lower_bound(lb_ref, d, layer):
    raw = [lb_ref[d * DEPTH + m:d * DEPTH + m + 1, :] for m in range(DEPTH)]
    mx = functools.reduce(jnp.maximum, raw)
    e = [jnp.exp(r - mx) for r in raw]
    tot = functools.reduce(jnp.add, e)
    return functools.reduce(jnp.add, e[1:layer + 1], jnp.zeros_like(tot)) / tot


def _hgrn_kernel(nt, layer, hqf_ref, hff_ref, hif_ref, hqb_ref, hfb_ref, hib_ref, lb_ref, s0_ref,
                 of_ref, ob_ref, sfin_ref, st_s, q_s, f_s, kk_s, v_s, qt_s, kt_s, o_s):
    i = pl.program_id(1)

    @pl.when(i == 0)
    def _():
        st_s[...] = s0_ref[0]

    ri = lax.broadcasted_iota(jnp.int32, (HG_HALF, HG_HALF), 0) // HG_KEY
    ci = lax.broadcasted_iota(jnp.int32, (HG_HALF, HG_HALF), 1) // HG_KEY
    diag_mask = ri == ci
    ones_bd = jnp.where(diag_mask, 1.0, 0.0).astype(BF16)
    _hg_direction(hqf_ref, hff_ref, hif_ref, of_ref, _hg_lower_bound(lb_ref, 0, layer), st_s.at[0],
                  q_s, f_s, kk_s, v_s, qt_s, kt_s, o_s, 1, ones_bd, diag_mask)
    _hg_direction(hqb_ref, hfb_ref, hib_ref, ob_ref, _hg_lower_bound(lb_ref, 1, layer), st_s.at[1],
                  q_s, f_s, kk_s, v_s, qt_s, kt_s, o_s, -1, ones_bd, diag_mask)

    @pl.when(i == nt - 1)
    def _():
        sfin_ref[0] = st_s[...]


def _hg_state_to_blocks(s):
    b = s.shape[0]
    st = s.transpose(0, 1, 2, 4, 3).reshape(b, 2, 2, 4, HG_VAL, HG_KEY)
    eye = jnp.eye(4, dtype=s.dtype)
    return jnp.einsum('bdgivk,ij->bdgivjk', st, eye).reshape(b, 2, 2, HG_HALF, HG_HALF)


def _hg_state_from_blocks(st):
    b = st.shape[0]
    st = st.reshape(b, 2, 2, 4, HG_VAL, 4, HG_KEY)
    diag = jnp.stack([st[:, :, :, i, :, i, :] for i in range(4)], axis=3)
    return diag.reshape(b, 2, HG_HEADS, HG_VAL, HG_KEY).transpose(0, 1, 2, 4, 3)


def _hgrn_scan(hq, hff, hfb, hi, lb, s0, layer, row0, nseq, seqlen):
    nt = seqlen // HG_TILE
    base = row0 // HG_TILE
    fwd = pl.BlockSpec((HG_TILE, HG_KD), lambda b, i: (base + b * nt + i, 0))
    bwd = pl.BlockSpec((HG_TILE, HG_KD), lambda b, i: (base + b * nt + nt - 1 - i, 0))
    ofw = pl.BlockSpec((HG_TILE, HG_KD), lambda b, i: (b * nt + i, 0))
    obw = pl.BlockSpec((HG_TILE, HG_KD), lambda b, i: (b * nt + nt - 1 - i, 0))
    st = pl.BlockSpec((1, 2, 2, HG_HALF, HG_HALF), lambda b, i: (b, 0, 0, 0, 0))
    tile = lambda: pltpu.VMEM((HG_SLABS, HG_TILE, 128), F32)
    return pl.pallas_call(
        functools.partial(_hgrn_kernel, nt, layer),
        grid=(nseq, nt),
        in_specs=[fwd, fwd, fwd, bwd, bwd, bwd, pl.BlockSpec((2 * DEPTH, HG_KD), lambda b, i: (0, 0)), st],
        out_specs=[ofw, obw, st],
        out_shape=[jax.ShapeDtypeStruct((nseq * seqlen, HG_WIDTH), F32)] * 2
        + [jax.ShapeDtypeStruct((nseq, 2, 2, HG_HALF, HG_HALF), F32)],
        scratch_shapes=[pltpu.VMEM((2, 2, HG_HALF, HG_HALF), F32)] + [tile() for _ in range(7)],
        compiler_params=_cparams("arbitrary", "arbitrary"),
        name="hgrn_scan",
    )(hq, hff, hi, hq, hfb, hi, lb.reshape(2 * DEPTH, HG_KD), s0)


def _cd_out_kernel(x_ref, occ_ref, ocl_ref, ofc_ref, ofl_ref, obc_ref, obl_ref, hg_ref, go_ref, wout_ref, gt_ref,
                   o_ref, wa_s, wb_s):
    @pl.when(pl.program_id(0) == 0)
    def _():
        wa_s[...] = wout_ref[0:MLA_WIDTH, :].astype(BF16)
        wb_s[...] = wout_ref[MLA_WIDTH:, :].astype(BF16)

    ri = lax.broadcasted_iota(jnp.int32, (HG_HALF, HG_HALF), 0) // HG_VAL
    ci = lax.broadcasted_iota(jnp.int32, (HG_HALF, HG_HALF), 1) // HG_VAL
    ones_bd = jnp.where(ri == ci, 1.0, 0.0).astype(BF16)
    o = _ctx_or_lat(ofc_ref, ofl_ref) + _ctx_or_lat(obc_ref, obl_ref)
    sq = o * o
    hi = sq.astype(BF16)
    lo = (sq - hi.astype(F32)).astype(BF16)
    ms = jnp.concatenate(
        [jnp.dot(hi[:, l], ones_bd, preferred_element_type=F32) + jnp.dot(lo[:, l], ones_bd, preferred_element_type=F32)
         for l in (slice(0, HG_HALF), slice(HG_HALF, 2 * HG_HALF))], axis=1) * (1.0 / HG_VAL)
    hg = hg_ref[...]
    od = o * lax.rsqrt(ms + EPS) * go_ref[...] * (hg * jax.nn.sigmoid(hg))
    out = (jnp.dot(_ctx_or_lat(occ_ref, ocl_ref).astype(BF16), wa_s[...], preferred_element_type=F32)
           + jnp.dot(od.astype(BF16), wb_s[...], preferred_element_type=F32))
    o_ref[...] = x_ref[...] + gt_ref[0] * out


def _cd_out(x, oc, of, ob, hg, g_o, w_out, mod, layer):
    row = lambda n: pl.BlockSpec((ROW_BLK, n), lambda i: (i, 0))
    full = lambda a, b: pl.BlockSpec((a, b), lambda i: (0, 0))
    return pl.pallas_call(
        _cd_out_kernel,
        grid=(N_ROW_BLK,),
        in_specs=[row(D_MODEL)] + _split_specs(MLA_WIDTH) + _split_specs(HG_WIDTH) + _split_specs(HG_WIDTH)
        + [row(HG_WIDTH), full(1, HG_WIDTH), full(MLA_WIDTH + HG_WIDTH, D_MODEL), _mod_spec(layer, 2)],
        out_specs=row(D_MODEL),
        out_shape=jax.ShapeDtypeStruct((N_TOK, D_MODEL), F32),
        scratch_shapes=[pltpu.VMEM((MLA_WIDTH, D_MODEL), BF16), pltpu.VMEM((HG_WIDTH, D_MODEL), BF16)],
        compiler_params=_cparams("arbitrary"),
        name="cd_out",
    )(x, oc[0], oc[1], of[0], of[1], ob[0], ob[1], hg, jnp.tile(g_o, HG_HEADS).reshape(1, HG_WIDTH), w_out, mod)


N_PAIRS = 6
N_CLASSES = N_GROUPS * N_PAIRS
CLS_ROWS = 32
MOE_BM = 256
MOE_NBLK = N_TOK // MOE_BM + N_CLASSES
MOE_ROWS = MOE_NBLK * MOE_BM
PAIR_SLOTS = ((0, 1), (3, 1), (2, 1), (2, 0), (3, 0), (3, 2))


def _moe_route_kernel(x_ref, g_ref, sc_ref, sh_ref, rw_ref, rb_ref, h_ref, ti_ref, tw_ref, cnt_ref, base_s):
    @pl.when(pl.program_id(0) == 0)
    def _():
        base_s[...] = jnp.zeros_like(base_s)

    h = _norm_mod(x_ref[...], g_ref[...], sc_ref[0], sh_ref[0])
    logits = lax.dot_general(rw_ref[...], h, (((1,), (1,)), ((), ())), precision=HI,
                             preferred_element_type=F32)
    aff = jax.nn.sigmoid(logits)
    sel = aff + rb_ref[...]
    s = [sel[e:e + 1, :] for e in range(N_EXPERTS)]
    a = [aff[e:e + 1, :] for e in range(N_EXPERTS)]
    gs = []
    for g in range(N_GROUPS):
        m = s[4 * g:4 * g + 4]
        pairs = [m[i] + m[j] for i in range(4) for j in range(i + 1, 4)]
        gs.append(functools.reduce(jnp.maximum, pairs))
    gmax = functools.reduce(jnp.maximum, gs)
    taken = jnp.zeros_like(gmax) > 1.0
    gsel = []
    for g in range(N_GROUPS):
        hit = (gs[g] == gmax) & jnp.logical_not(taken)
        gsel.append(hit)
        taken = taken | hit
    e_lo = jnp.zeros(gmax.shape, jnp.int32)
    e_hi = jnp.zeros(gmax.shape, jnp.int32)
    a_lo = jnp.zeros_like(gmax)
    a_hi = jnp.zeros_like(gmax)
    nsel = jnp.zeros(gmax.shape, jnp.int32)
    for g in range(N_GROUPS):
        for i in range(4):
            e = 4 * g + i
            beat = jnp.zeros(gmax.shape, jnp.int32)
            for j in range(4):
                if j != i:
                    o = 4 * g + j
                    beat = beat + jnp.where((s[o] > s[e]) | ((s[o] == s[e]) & (j < i)), 1, 0)
            pick = gsel[g] & (beat < 2)
            is_first = pick & (nsel == 0)
            is_second = pick & (nsel == 1)
            e_lo = jnp.where(is_first, e, e_lo)
            a_lo = jnp.where(is_first, a[e], a_lo)
            e_hi = jnp.where(is_second, e, e_hi)
            a_hi = jnp.where(is_second, a[e], a_hi)
            nsel = nsel + jnp.where(pick, 1, 0)
    grp = e_lo // EXPERTS_PER_GROUP
    lo = e_lo - grp * EXPERTS_PER_GROUP
    hi = e_hi - grp * EXPERTS_PER_GROUP
    pair = ((lo * (7 - lo)) >> 1) + (hi - lo - 1)
    pair = jnp.where(pair == 1, 3, jnp.where(pair == 2, 4, jnp.where(pair == 3, 2, jnp.where(pair == 4, 1, pair))))
    cls = grp * N_PAIRS + pair
    wsum = a_lo + a_hi
    w_lo, w_hi = a_lo / wsum, a_hi / wsum
    onehot = (lax.broadcasted_iota(jnp.int32, (CLS_ROWS, ROW_BLK), 0) == cls).astype(F32)
    tt = lax.broadcasted_iota(jnp.int32, (ROW_BLK, ROW_BLK), 0) < lax.broadcasted_iota(jnp.int32, (ROW_BLK, ROW_BLK), 1)
    before = _dot(onehot, jnp.where(tt, 1.0, 0.0))
    base = base_s[...]
    rank = jnp.sum(onehot * (before + base[:, 0:1]), axis=0, keepdims=True).astype(jnp.int32)
    base = base + jnp.sum(onehot, axis=1, keepdims=True)
    base_s[...] = base
    cnt_ref[...] = base.astype(jnp.int32)
    ti_ref[0] = jnp.concatenate([cls, rank, e_lo, e_hi, jnp.zeros((4, ROW_BLK), jnp.int32)], axis=0)
    ident = lax.broadcasted_iota(jnp.int32, (ROW_BLK, ROW_BLK), 0) == lax.broadcasted_iota(jnp.int32, (ROW_BLK, ROW_BLK), 1)
    col = lambda r: jnp.sum(jnp.where(ident, r, 0.0), axis=1, keepdims=True)
    tw_ref[...] = jnp.where(lax.broadcasted_iota(jnp.int32, (ROW_BLK, 128), 1) < 64, col(w_lo), col(w_hi))
    h_ref[...] = h


def _moe_route(x, mod, g, router_w, router_b, layer):
    row = lambda n: pl.BlockSpec((ROW_BLK, n), lambda i: (i, 0))
    full = lambda a, b: pl.BlockSpec((a, b), lambda i: (0, 0))
    return pl.pallas_call(
        _moe_route_kernel,
        grid=(N_ROW_BLK,),
        in_specs=[row(D_MODEL), full(1, D_MODEL), _mod_spec(layer, 4), _mod_spec(layer, 3),
                  full(N_EXPERTS, D_MODEL), full(N_EXPERTS, 1)],
        out_specs=[row(D_MODEL), pl.BlockSpec((1, 8, ROW_BLK), lambda i: (i, 0, 0)), row(128), full(CLS_ROWS, 128)],
        out_shape=[jax.ShapeDtypeStruct((N_TOK, D_MODEL), F32),
                   jax.ShapeDtypeStruct((N_ROW_BLK, 8, ROW_BLK), jnp.int32),
                   jax.ShapeDtypeStruct((N_TOK, 128), F32),
                   jax.ShapeDtypeStruct((CLS_ROWS, 128), jnp.int32)],
        scratch_shapes=[pltpu.VMEM((CLS_ROWS, 128), F32)],
        compiler_params=_cparams("arbitrary"),
        name="moe_route",
    )(x, g.reshape(1, D_MODEL), mod, mod, router_w.T, router_b.reshape(N_EXPERTS, 1))


def _moe_sort_kernel(pos_ref, nblk_ref, swap_ref, h_ref, wt_ref, hs_ref, ws_ref, perm_ref, stage_s, wstage_s):
    j = pl.program_id(0)

    @pl.when(j == 0)
    def _():
        def init(r, c):
            perm_ref[r] = N_TOK
            return c

        lax.fori_loop(0, MOE_ROWS, init, 0, unroll=16)

        def build(t, c):
            perm_ref[pos_ref[t]] = t
            return c

        lax.fori_loop(0, N_TOK, build, 0, unroll=16)

    @pl.when(j < nblk_ref[0])
    def _():
        base = j * MOE_BM
        for r in range(MOE_BM):
            src = jnp.minimum(perm_ref[base + r], N_TOK - 1)
            stage_s[r:r + 1, :] = h_ref[pl.ds(src, 1), :]
            wstage_s[r:r + 1, :] = wt_ref[pl.ds(src, 1), :]
        hs_ref[...] = stage_s[...].astype(BF16)
        w = wstage_s[...]
        ws_ref[...] = jnp.where(swap_ref[j] == 1, pltpu.roll(w, 64, axis=1), w)

    @pl.when(j >= nblk_ref[0])
    def _():
        hs_ref[...] = jnp.zeros_like(hs_ref)
        ws_ref[...] = jnp.zeros_like(ws_ref)


def _moe_sort(h, wtok, pos, nblk, swap):
    res = lambda w: pl.BlockSpec((N_TOK, w), lambda j, *_: (0, 0), pipeline_mode=pl.Buffered(1))
    grid_spec = pltpu.PrefetchScalarGridSpec(
        num_scalar_prefetch=3,
        grid=(MOE_NBLK,),
        in_specs=[res(D_MODEL), res(128)],
        out_specs=[pl.BlockSpec((MOE_BM, D_MODEL), lambda j, *_: (j, 0)),
                   pl.BlockSpec((MOE_BM, 128), lambda j, *_: (j, 0)),
                   pl.BlockSpec(memory_space=pltpu.SMEM)],
        scratch_shapes=[pltpu.VMEM((MOE_BM, D_MODEL), F32), pltpu.VMEM((MOE_BM, 128), F32)],
    )
    return pl.pallas_call(
        _moe_sort_kernel,
        grid_spec=grid_spec,
        out_shape=[jax.ShapeDtypeStruct((MOE_ROWS, D_MODEL), BF16), jax.ShapeDtypeStruct((MOE_ROWS, 128), F32),
                   jax.ShapeDtypeStruct((MOE_ROWS,), jnp.int32)],
        compiler_params=_cparams("arbitrary"),
        name="moe_sort",
    )(pos, nblk, swap, h, wtok)


def _moe_experts_kernel(elo_ref, ehi_ref, nblk_ref, hs_ref, ws_ref, wgl_ref, wul_ref, wdl_ref,
                        wgh_ref, wuh_ref, wdh_ref, y_ref):
    del elo_ref, ehi_ref
    j = pl.program_id(0)

    @pl.when(j < nblk_ref[0])
    def _():
        h = hs_ref[...]
        acc = None
        for wg, wu, wd, gate in ((wgl_ref, wul_ref, wdl_ref, ws_ref[:, 0:1]), (wgh_ref, wuh_ref, wdh_ref, ws_ref[:, 64:65])):
            g = _dot(h, wg[0, 0])
            u = _dot(h, wu[0, 0])
            hid = g * jax.nn.sigmoid(g) * u * gate
            y = _dot(hid, wd[0, 0])
            acc = y if acc is None else acc + y
        y_ref[...] = acc

    @pl.when(j >= nblk_ref[0])
    def _():
        y_ref[...] = jnp.zeros_like(y_ref)


def _moe_experts(hs, ws, blk_elo, blk_ehi, nblk, w_gate, w_up, w_down, layer):
    last = lambda j, nb: jnp.minimum(j, nb[0] - 1)
    wspec = lambda a, b, which: pl.BlockSpec(
        (1, 1, a, b), lambda j, elo, ehi, nb: (layer, (elo, ehi)[which][last(j, nb)], 0, 0))
    grid_spec = pltpu.PrefetchScalarGridSpec(
        num_scalar_prefetch=3,
        grid=(MOE_NBLK,),
        in_specs=[pl.BlockSpec((MOE_BM, D_MODEL), lambda j, elo, ehi, nb: (last(j, nb), 0)),
                  pl.BlockSpec((MOE_BM, 128), lambda j, elo, ehi, nb: (last(j, nb), 0)),
                  wspec(D_MODEL, D_FF, 0), wspec(D_MODEL, D_FF, 0), wspec(D_FF, D_MODEL, 0),
                  wspec(D_MODEL, D_FF, 1), wspec(D_MODEL, D_FF, 1), wspec(D_FF, D_MODEL, 1)],
        out_specs=pl.BlockSpec((MOE_BM, D_MODEL), lambda j, *_: (j, 0)),
    )
    return pl.pallas_call(
        _moe_experts_kernel,
        grid_spec=grid_spec,
        out_shape=jax.ShapeDtypeStruct((MOE_ROWS, D_MODEL), F32),
        compiler_params=_cparams("arbitrary"),
        name="moe_experts",
    )(blk_elo, blk_ehi, nblk, hs, ws, w_gate, w_up, w_down, w_gate, w_up, w_down)


MOE_DUMP = 8


def _moe_combine_kernel(perm_ref, nblk_ref, y_ref, x_ref, gt_ref, o_ref, ytok_s):
    s = pl.program_id(0)

    @pl.when(s == 0)
    def _():
        ytok_s[N_TOK:, :] = jnp.zeros((MOE_DUMP, D_MODEL), F32)

    @pl.when(s < nblk_ref[0])
    def _():
        base = s * MOE_BM
        for r in range(MOE_BM):
            ytok_s[pl.ds(perm_ref[base + r], 1), :] = y_ref[r:r + 1, :]

    @pl.when(s >= MOE_NBLK)
    def _():
        i = s - MOE_NBLK
        o_ref[...] = x_ref[...] + gt_ref[0] * ytok_s[pl.ds(pl.multiple_of(i * ROW_BLK, ROW_BLK), ROW_BLK), :]


def _moe_combine(x, y_sorted, perm, nblk, mod, layer):
    tok = lambda s: jnp.maximum(s - MOE_NBLK, 0)
    grid_spec = pltpu.PrefetchScalarGridSpec(
        num_scalar_prefetch=2,
        grid=(MOE_NBLK + N_ROW_BLK,),
        in_specs=[pl.BlockSpec((MOE_BM, D_MODEL), lambda s, perm, nb: (jnp.minimum(s, nb[0] - 1), 0)),
                  pl.BlockSpec((ROW_BLK, D_MODEL), lambda s, perm, nb: (tok(s), 0)),
                  pl.BlockSpec((1, 1, D_MODEL),
                               lambda s, perm, nb: ((layer * MOD_ROWS + _mod_group(tok(s))) * N_MOD + 5, 0, 0))],
        out_specs=pl.BlockSpec((ROW_BLK, D_MODEL), lambda s, perm, nb: (tok(s), 0)),
        scratch_shapes=[pltpu.VMEM((N_TOK + MOE_DUMP, D_MODEL), F32)],
    )
    return pl.pallas_call(
        _moe_combine_kernel,
        grid_spec=grid_spec,
        out_shape=jax.ShapeDtypeStruct((N_TOK, D_MODEL), F32),
        compiler_params=_cparams("arbitrary"),
        name="moe_combine",
    )(perm, nblk, y_sorted, x, mod)


def _moe(x, mod, g, router_w, router_b, w_gate, w_up, w_down, layer):
    h, info, wtok, counts = _moe_route(x, mod, g, router_w, router_b, layer)
    cls = info[:, 0, :].reshape(N_TOK)
    rank = info[:, 1, :].reshape(N_TOK)
    cnt = counts[:N_CLASSES, 0]
    nb = (cnt + MOE_BM - 1) // MOE_BM
    ends = jnp.cumsum(nb)
    starts = ends - nb
    pos = ((starts * MOE_BM)[cls] + rank).astype(jnp.int32)
    blk = jnp.arange(MOE_NBLK, dtype=jnp.int32)
    blk_cls = jnp.minimum(jnp.sum((blk[:, None] >= ends[None, :]).astype(jnp.int32), axis=1), N_CLASSES - 1)
    slot_a = jnp.asarray([s[0] for s in PAIR_SLOTS], jnp.int32)
    slot_b = jnp.asarray([s[1] for s in PAIR_SLOTS], jnp.int32)
    grp = blk_cls // N_PAIRS
    blk_ea = (grp * EXPERTS_PER_GROUP + slot_a[blk_cls % N_PAIRS]).astype(jnp.int32)
    blk_eb = (grp * EXPERTS_PER_GROUP + slot_b[blk_cls % N_PAIRS]).astype(jnp.int32)
    swap = (blk_ea > blk_eb).astype(jnp.int32)
    nblk = ends[-1:].astype(jnp.int32)
    hs, ws, perm = _moe_sort(h, wtok, pos, nblk, swap)
    y_sorted = _moe_experts(hs, ws, blk_ea, blk_eb, nblk, w_gate, w_up, w_down, layer)
    return _moe_combine(x, y_sorted, perm, nblk, mod, layer)


def _final_norm_kernel(x_ref, g_ref, o_ref):
    o_ref[...] = _rms(x_ref[...]) * g_ref[...]


def _final_norm(x, g, row0, rows):
    base = row0 // ROW_BLK
    return pl.pallas_call(
        _final_norm_kernel,
        grid=(rows // ROW_BLK,),
        in_specs=[pl.BlockSpec((ROW_BLK, D_MODEL), lambda i: (base + i, 0)), pl.BlockSpec((1, D_MODEL), lambda i: (0, 0))],
        out_specs=pl.BlockSpec((ROW_BLK, D_MODEL), lambda i: (i, 0)),
        out_shape=jax.ShapeDtypeStruct((rows, D_MODEL), F32),
        compiler_params=_cparams("arbitrary"),
        name="final_norm",
    )(x, g.reshape(1, D_MODEL))


def kernel(x_prompt, x_sample, cache_attn_k, cache_attn_v, state_ssm_re, state_ssm_im, cache_mla_ckv, cache_mla_kpe,
           state_hgrn, c, c_ctx, w_mod, b_mod, g_mix, g_ffn, g_final, router_w, router_b, moe_w_gate, moe_w_up,
           moe_w_down, ab_w_in, ab_sink, s5_lam_re, s5_lam_im, s5_log_dt, s5_b_re, s5_b_im, s5_c_re, s5_c_im, s5_d,
           s5_w_glu, ab_w_out, cd_w_in, mla_g_q, mla_w_q_up, mla_g_kv, mla_w_kv_up, hg_lower_bounds, hg_g_o, cd_w_out):
    x = jnp.concatenate([x_prompt.reshape(N_CTX_TOK, D_MODEL), x_sample.reshape(N_LAT_TOK, D_MODEL)], axis=0)
    cond = jnp.zeros((MOD_ROWS, D_MODEL), F32).at[0].set(c_ctx).at[1:1 + DEC_BATCH].set(c)
    mod = _modulation(cond, w_mod, b_mod)
    keep = ([], [], [], [], [], [], [])
    for l in range(DEPTH):
        j = l // 2
        if l % 2 == 0:
            q, k, v, u = _ab_in(x, mod, g_mix[l], ab_w_in[j], l)
            o_a = _attention_a(q, k, v, cache_attn_k[:, j].reshape(DEC_BATCH, PAST_LEN, A_KV_WIDTH),
                               cache_attn_v[:, j].reshape(DEC_BATCH, PAST_LEN, A_KV_WIDTH), ab_sink[j])
            y, fin_re, fin_im = _s5_scan(u, state_ssm_re[:, j], state_ssm_im[:, j], s5_lam_re[j], s5_lam_im[j],
                                         s5_log_dt[j], s5_b_re[j], s5_b_im[j], s5_c_re[j], s5_c_im[j])
            x = _ab_out(x, o_a, y, u, s5_d[j], s5_w_glu[j], ab_w_out[j], mod, l)
            keep[0].append(k[:N_CTX_TOK].reshape(BATCH, SEQ, A_KV_HEADS, HEAD_DIM))
            keep[1].append(v[:N_CTX_TOK].reshape(BATCH, SEQ, A_KV_HEADS, HEAD_DIM))
            keep[2].append(fin_re)
            keep[3].append(fin_im)
        else:
            qn, qr, ckv, kpe, kn, vm, hq, hff, hfb, hi, hg = _cd_in(
                x, mod, g_mix[l], cd_w_in[j], mla_g_q[j], mla_w_q_up[j], mla_g_kv[j], mla_w_kv_up[j], l)
            kcn, vc = _mla_cache_kv(cache_mla_ckv[:, j].reshape(DEC_BATCH * PAST_LEN, MLA_KV_RANK), mla_w_kv_up[j])
            o_c = _attention_mla(qn, qr, kn, kpe, vm, kcn, cache_mla_kpe[:, j], vc)
            s0_ctx = jnp.zeros((BATCH, 2, 2, HG_HALF, HG_HALF), F32)
            of_c, ob_c, s_fin = _hgrn_scan(hq, hff, hfb, hi, hg_lower_bounds, s0_ctx, l, 0, BATCH, SEQ)
            of_l, ob_l, _ = _hgrn_scan(hq, hff, hfb, hi, hg_lower_bounds, _hg_state_to_blocks(state_hgrn[:, j]), l,
                                       N_CTX_TOK, DEC_BATCH, DEC_SEQ)
            x = _cd_out(x, o_c, (of_c, of_l), (ob_c, ob_l), hg, hg_g_o[j], cd_w_out[j], mod, l)
            keep[4].append(ckv[:N_CTX_TOK].reshape(BATCH, SEQ, MLA_KV_RANK))
            keep[5].append(kpe[:N_CTX_TOK, :MLA_ROPE].reshape(BATCH, SEQ, MLA_ROPE))
            keep[6].append(_hg_state_from_blocks(s_fin))
        x = _moe(x, mod, g_ffn[l], router_w, router_b, moe_w_gate, moe_w_up, moe_w_down, l)
    y_ctx = _final_norm(x, g_final, 0, N_CTX_TOK)
    y_lat = _final_norm(x, g_final, N_CTX_TOK, N_LAT_TOK)
    return (y_ctx.reshape(BATCH, SEQ, D_MODEL), y_lat.reshape(DEC_BATCH, DEC_SEQ, D_MODEL),
            jnp.stack(keep[0], 1), jnp.stack(keep[1], 1), jnp.stack(keep[2], 1), jnp.stack(keep[3], 1),
            jnp.stack(keep[4], 1), jnp.stack(keep[5], 1), jnp.stack(keep[6], 1))
```

```python
import functools
import math

import numpy as np
import jax
import jax.numpy as jnp
from jax import lax
from jax.experimental import pallas as pl
from jax.experimental.pallas import tpu as pltpu

F32 = jnp.float32
BF16 = jnp.bfloat16

D_MODEL = 1024
BATCH = 16
SEQ = 256
DEPTH = 2
DEC_BATCH = 4
DEC_SEQ = 1024
PAST_LEN = 512
GRID_W = 64
N_MOD = 6
EPS = 1e-6
NEG_INF = -1e30
ROPE_BASE = 10000.0
HEAD_DIM = 64
A_HEADS = 8
A_KV_HEADS = 2
A_WINDOW = 128
A_WIDTH = A_HEADS * HEAD_DIM
A_KV_WIDTH = A_KV_HEADS * HEAD_DIM
S5_WIDTH = D_MODEL // 2
S5_GROUP = 16
S5_GROUPS = S5_WIDTH // S5_GROUP
S5_STATE = 64
MLA_HEADS = 8
MLA_Q_RANK = D_MODEL // 4
MLA_KV_RANK = D_MODEL // 8
MLA_NOPE = 64
MLA_ROPE = 32
MLA_V = 64
MLA_WIDTH = MLA_HEADS * MLA_V
HG_HEADS = 8
HG_KEY = 64
HG_VAL = 64
HG_KD = HG_HEADS * HG_KEY
HG_WIDTH = HG_HEADS * HG_VAL
HG_CHUNK = 16
N_EXPERTS = 16
N_GROUPS = 4
EXPERTS_PER_GROUP = N_EXPERTS // N_GROUPS
D_FF = D_MODEL // 2

N_CTX_TOK = BATCH * SEQ
N_LAT_TOK = DEC_BATCH * DEC_SEQ
N_TOK = N_CTX_TOK + N_LAT_TOK
ROW_BLK = 256
N_ROW_BLK = N_TOK // ROW_BLK
N_CTX_BLK = N_CTX_TOK // ROW_BLK
LAT_BLK_PER_SEQ = DEC_SEQ // ROW_BLK
MOD_ROWS = 8
VMEM_LIMIT = 56 * 1024 * 1024


def _cparams(*sem):
    return pltpu.CompilerParams(dimension_semantics=sem, vmem_limit_bytes=VMEM_LIMIT)


def _mod_group(i):
    return jnp.where(i < N_CTX_BLK, 0, 1 + (i - N_CTX_BLK) // LAT_BLK_PER_SEQ)


def _mod_spec(layer, which):
    return pl.BlockSpec((1, 1, D_MODEL), lambda i: ((layer * MOD_ROWS + _mod_group(i)) * N_MOD + which, 0, 0))


def _rope_blk(i):
    return jnp.where(i < N_CTX_BLK, 0, 1 + (i - N_CTX_BLK) % LAT_BLK_PER_SEQ)


def _rope_tables(rot_dim):
    n_freq = rot_dim // 4
    t = np.arange(DEC_SEQ)
    rows = (t // GRID_W).astype(np.float32)
    cols = (t % GRID_W).astype(np.float32)
    inv = (np.float32(ROPE_BASE) ** (-np.arange(n_freq, dtype=np.float32) / np.float32(n_freq))).astype(np.float32)
    ang_r = rows[:, None] * inv[None, :]
    ang_c = cols[:, None] * inv[None, :]
    ang = np.concatenate([ang_r, ang_r, ang_c, ang_c], axis=-1).astype(np.float32)
    reps = 128 // rot_dim
    cos = np.tile(np.cos(ang), (1, reps)).astype(np.float32)
    sin = np.tile(np.sin(ang), (1, reps)).astype(np.float32)
    lane = np.arange(128)
    first = (lane % (2 * n_freq)) < n_freq
    sin_a = np.where(first[None, :], -sin, 0.0).astype(np.float32)
    sin_b = np.where(first[None, :], 0.0, sin).astype(np.float32)
    ident = np.zeros((ROW_BLK, 128), np.float32)
    cos = np.concatenate([ident + 1.0, cos], axis=0)
    sin_a = np.concatenate([ident, sin_a], axis=0)
    sin_b = np.concatenate([ident, sin_b], axis=0)
    return jnp.asarray(cos), jnp.asarray(sin_a), jnp.asarray(sin_b)


def _apply_rope(x, cos, sin_a, sin_b, quarter):
    outs = []
    for j in range(x.shape[1] // 128):
        xt = x[:, 128 * j:128 * (j + 1)]
        up = pltpu.roll(xt, 128 - quarter, axis=1)
        dn = pltpu.roll(xt, quarter, axis=1)
        outs.append(xt * cos + up * sin_a + dn * sin_b)
    return outs[0] if len(outs) == 1 else jnp.concatenate(outs, axis=1)


def _rms(x):
    return x * lax.rsqrt(jnp.mean(x * x, axis=-1, keepdims=True) + EPS)


def _norm_mod(x, g, sc, sh):
    return _rms(x) * g * (1.0 + sc) + sh


def _dot(a, b):
    return jnp.dot(a.astype(BF16), b.astype(BF16), preferred_element_type=F32)


def _dot_nt(a, b):
    return lax.dot_general(a.astype(BF16), b.astype(BF16), (((1,), (1,)), ((), ())), preferred_element_type=F32)


def _mod_kernel(cond_ref, w_ref, b_ref, o_ref):
    c = cond_ref[...]
    s = c * jax.nn.sigmoid(c)
    o_ref[0] = _dot(s, w_ref[0]) + b_ref[0]


def _modulation(cond, w_mod, b_mod):
    nb = 1024
    out = pl.pallas_call(
        _mod_kernel,
        grid=(DEPTH, N_MOD * D_MODEL // nb),
        in_specs=[pl.BlockSpec((MOD_ROWS, D_MODEL), lambda l, n: (0, 0)),
                  pl.BlockSpec((1, D_MODEL, nb), lambda l, n: (l, 0, n)),
                  pl.BlockSpec((1, 1, nb), lambda l, n: (l, 0, n))],
        out_specs=pl.BlockSpec((1, MOD_ROWS, nb), lambda l, n: (l, 0, n)),
        out_shape=jax.ShapeDtypeStruct((DEPTH, MOD_ROWS, N_MOD * D_MODEL), F32),
        compiler_params=_cparams("arbitrary", "arbitrary"),
        name="modulation",
    )(cond, w_mod, b_mod.reshape(DEPTH, 1, N_MOD * D_MODEL))
    return out.reshape(DEPTH * MOD_ROWS * N_MOD, 1, D_MODEL)


def _ab_in_kernel(x_ref, g_ref, sc_ref, sh_ref, w_ref, cos_ref, sa_ref, sb_ref,
                  q_ref, k_ref, v_ref, u_ref, wq_s, wk_s, wv_s, wu_s):
    @pl.when(pl.program_id(0) == 0)
    def _():
        wq_s[...] = w_ref[:, 0:A_WIDTH].astype(BF16)
        wk_s[...] = w_ref[:, A_WIDTH:A_WIDTH + A_KV_WIDTH].astype(BF16)
        wv_s[...] = w_ref[:, A_WIDTH + A_KV_WIDTH:A_WIDTH + 2 * A_KV_WIDTH].astype(BF16)
        wu_s[...] = w_ref[:, A_WIDTH + 2 * A_KV_WIDTH:].astype(BF16)

    h = _norm_mod(x_ref[...], g_ref[...], sc_ref[0], sh_ref[0]).astype(BF16)
    cos, sa, sb = cos_ref[...], sa_ref[...], sb_ref[...]
    q = jnp.dot(h, wq_s[...], preferred_element_type=F32)
    q_ref[...] = _apply_rope(q, cos, sa, sb, HEAD_DIM // 4).astype(q_ref.dtype)
    k = jnp.dot(h, wk_s[...], preferred_element_type=F32)
    k_ref[...] = _apply_rope(k, cos, sa, sb, HEAD_DIM // 4)
    v_ref[...] = jnp.dot(h, wv_s[...], preferred_element_type=F32)
    u_ref[...] = jnp.dot(h, wu_s[...], preferred_element_type=F32)


def _ab_in(x, mod, g, w, layer):
    cos, sa, sb = _rope_tables(HEAD_DIM)
    d_in = w.shape[1]
    row = lambda n: pl.BlockSpec((ROW_BLK, n), lambda i: (i, 0))
    rope = pl.BlockSpec((ROW_BLK, 128), lambda i: (_rope_blk(i), 0))
    return pl.pallas_call(
        _ab_in_kernel,
        grid=(N_ROW_BLK,),
        in_specs=[row(D_MODEL), pl.BlockSpec((1, D_MODEL), lambda i: (0, 0)),
                  _mod_spec(layer, 1), _mod_spec(layer, 0),
                  pl.BlockSpec((D_MODEL, d_in), lambda i: (0, 0)), rope, rope, rope],
        out_specs=[row(A_WIDTH), row(A_KV_WIDTH), row(A_KV_WIDTH), row(S5_WIDTH)],
        out_shape=[jax.ShapeDtypeStruct((N_TOK, A_WIDTH), BF16), jax.ShapeDtypeStruct((N_TOK, A_KV_WIDTH), F32),
                   jax.ShapeDtypeStruct((N_TOK, A_KV_WIDTH), F32), jax.ShapeDtypeStruct((N_TOK, S5_WIDTH), F32)],
        scratch_shapes=[pltpu.VMEM((D_MODEL, A_WIDTH), BF16), pltpu.VMEM((D_MODEL, A_KV_WIDTH), BF16),
                        pltpu.VMEM((D_MODEL, A_KV_WIDTH), BF16), pltpu.VMEM((D_MODEL, S5_WIDTH), BF16)],
        compiler_params=_cparams("arbitrary"),
        name="ab_in",
    )(x, g.reshape(1, D_MODEL), mod, mod, w, cos, sa, sb)


def _softmax_pv(s_list, v_list, sink):
    m = functools.reduce(jnp.maximum, [jnp.max(s, axis=-1, keepdims=True) for s in s_list])
    if sink is not None:
        m = jnp.maximum(m, sink)
    ps = [jnp.exp(s - m) for s in s_list]
    l = functools.reduce(jnp.add, [jnp.sum(p, axis=-1, keepdims=True) for p in ps])
    if sink is not None:
        l = l + jnp.exp(sink - m)
    o = functools.reduce(jnp.add, [_dot(p, v) for p, v in zip(ps, v_list)])
    return o / l


def _gqa_with_sink(sink_ref, q, k_of, v_of, bias, o_ref):
    g = A_HEADS // A_KV_HEADS
    r = q.shape[0]
    outs = [None] * A_HEADS
    for kh in range(A_KV_HEADS):
        heads = range(g * kh, g * (kh + 1))
        qg = jnp.concatenate([q[:, HEAD_DIM * h:HEAD_DIM * (h + 1)] for h in heads], axis=0)
        sink = jnp.concatenate([jnp.full((r, 1), sink_ref[h], F32) for h in heads], axis=0)
        s = _dot_nt(qg, k_of(kh))
        if bias is not None:
            s = s + jnp.concatenate([bias] * g, axis=0)
        m = jnp.maximum(jnp.max(s, axis=-1, keepdims=True), sink)
        p = jnp.exp(s - m)
        l = jnp.sum(p, axis=-1, keepdims=True) + jnp.exp(sink - m)
        o = _dot(p, v_of(kh)) / l
        for n, h in enumerate(heads):
            outs[h] = o[r * n:r * (n + 1)]
    o_ref[...] = jnp.concatenate(outs, axis=1).astype(o_ref.dtype)


def _attn_ctx_kernel(sink_ref, q_ref, k_ref, v_ref, o_ref):
    scale = HEAD_DIM ** -0.5
    g = A_HEADS // A_KV_HEADS
    outs = []
    for h in range(A_HEADS):
        kh = h // g
        q = q_ref[:, HEAD_DIM * h:HEAD_DIM * (h + 1)]
        k = k_ref[:, HEAD_DIM * kh:HEAD_DIM * (kh + 1)]
        v = v_ref[:, HEAD_DIM * kh:HEAD_DIM * (kh + 1)]
        s = _dot_nt(q, k) * scale
        outs.append(_softmax_pv([s], [v], sink_ref[h]))
    o_ref[...] = jnp.concatenate(outs, axis=1).astype(o_ref.dtype)


def _attn_lat_kernel(sink_ref, q_ref, kp_ref, kc_ref, kn_ref, vp_ref, vc_ref, vn_ref, kx_ref, vx_ref, o_ref):
    n = pl.program_id(1)
    nb = DEC_SEQ // A_WINDOW
    i = lax.broadcasted_iota(jnp.int32, (A_WINDOW, A_WINDOW), 0)
    j = lax.broadcasted_iota(jnp.int32, (A_WINDOW, A_WINDOW), 1)
    zero = jnp.zeros((A_WINDOW, A_WINDOW), F32)
    bias = jnp.concatenate([jnp.where((j >= i) & (n > 0), 0.0, NEG_INF), zero,
                            jnp.where((j <= i) & (n < nb - 1), 0.0, NEG_INF),
                            jnp.zeros((A_WINDOW, PAST_LEN), F32)], axis=1)

    def rows(p_ref, c_ref, n_ref, x_ref):
        def of(kh):
            sl = slice(HEAD_DIM * kh, HEAD_DIM * (kh + 1))
            return jnp.concatenate([p_ref[:, sl], c_ref[:, sl], n_ref[:, sl], x_ref[0, :, sl]], axis=0)
        return of

    _gqa_with_sink(sink_ref, q_ref[...] * HEAD_DIM ** -0.5, rows(kp_ref, kc_ref, kn_ref, kx_ref),
                   rows(vp_ref, vc_ref, vn_ref, vx_ref), bias, o_ref)


def _attention_a(q, k, v, cache_k, cache_v, sink):
    smem = pl.BlockSpec(memory_space=pltpu.SMEM)
    o = pl.pallas_call(
        _attn_ctx_kernel,
        grid=(BATCH,),
        in_specs=[smem, pl.BlockSpec((SEQ, A_WIDTH), lambda b: (b, 0)),
                  pl.BlockSpec((SEQ, A_KV_WIDTH), lambda b: (b, 0)), pl.BlockSpec((SEQ, A_KV_WIDTH), lambda b: (b, 0))],
        out_specs=pl.BlockSpec((SEQ, A_WIDTH), lambda b: (b, 0)),
        out_shape=jax.ShapeDtypeStruct((N_CTX_TOK, A_WIDTH), BF16),
        compiler_params=_cparams("arbitrary"),
        name="attn_a_ctx",
    )(sink, q, k, v)
    nb = DEC_SEQ // A_WINDOW
    base = N_CTX_TOK // A_WINDOW
    cur = lambda b, n: (base + b * nb + n, 0)
    prev = lambda b, n: (base + b * nb + jnp.maximum(n - 1, 0), 0)
    nxt = lambda b, n: (base + b * nb + jnp.minimum(n + 1, nb - 1), 0)
    kv = lambda f: pl.BlockSpec((A_WINDOW, A_KV_WIDTH), f)
    cache = pl.BlockSpec((1, PAST_LEN, A_KV_WIDTH), lambda b, n: (b, 0, 0))
    o_lat = pl.pallas_call(
        _attn_lat_kernel,
        grid=(DEC_BATCH, nb),
        in_specs=[smem, pl.BlockSpec((A_WINDOW, A_WIDTH), cur), kv(prev), kv(cur), kv(nxt), kv(prev), kv(cur), kv(nxt),
                  cache, cache],
        out_specs=pl.BlockSpec((A_WINDOW, A_WIDTH), lambda b, n: (b * nb + n, 0)),
        out_shape=jax.ShapeDtypeStruct((N_LAT_TOK, A_WIDTH), BF16),
        compiler_params=_cparams("arbitrary", "arbitrary"),
        name="attn_a_lat",
    )(sink, q, k, k, k, v, v, v, cache_k, cache_v)
    return o, o_lat


S5_CHUNK = 16
S5_OCT = 128 // S5_GROUP
S5_NOCT = S5_GROUPS // S5_OCT
S5_K = S5_CHUNK * 128
S5_PART = S5_OCT * S5_STATE
S5_SW = 4 * S5_PART
S5_ROWS_CTX = BATCH * SEQ // S5_CHUNK
S5_ROWS_LAT = DEC_BATCH * DEC_SEQ // S5_CHUNK
S5_ROWS = S5_ROWS_CTX + S5_ROWS_LAT
S5_NB = 4
HI = lax.Precision.HIGHEST


def _s5_disc_kernel(lr_ref, li_ref, ldt_ref, ar_ref, ai_ref, zr_ref, zi_ref):
    lr, li = lr_ref[...], li_ref[...]
    dt = jnp.exp(ldt_ref[...])
    mag = jnp.exp(lr * dt)
    ar, ai = mag * jnp.cos(li * dt), mag * jnp.sin(li * dt)
    den = lr * lr + li * li
    ar_ref[...] = ar
    ai_ref[...] = ai
    zr_ref[...] = ((ar - 1.0) * lr + ai * li) / den
    zi_ref[...] = (ai * lr - (ar - 1.0) * li) / den


def _cmul(xr, xi, yr, yi):
    return xr * yr - xi * yi, xr * yi + xi * yr


def _dot_nt_hi(a, b):
    return lax.dot_general(a, b, (((1,), (1,)), ((), ())), precision=HI, preferred_element_type=F32)


def _s5_prep_kernel(ar_ref, ai_ref, zr_ref, zi_ref, btr_ref, bti_ref, ctr_ref, cti_ref,
                    m_ref, win_ref, wout_ref, a16_ref, pw_s, w_s, k_s):
    t = pl.program_id(1)
    npw = S5_CHUNK + 1
    blk = lambda j: pl.ds(pl.multiple_of(j * 128, 128), 128)

    @pl.when(t == 0)
    def _():
        kd = []
        for d in range(2):
            ar, ai = ar_ref[0, d], ai_ref[0, d]
            pr, pi = jnp.ones_like(ar), jnp.zeros_like(ar)
            bbr, bbi = _cmul(zr_ref[0, d], zi_ref[0, d], btr_ref[0, d], bti_ref[0, d])
            for j in range(npw):
                pw_s[d, 0, j:j + 1, :] = pr
                pw_s[d, 1, j:j + 1, :] = pi
                if j < S5_CHUNK:
                    wr, wi = _cmul(pr, pi, bbr, bbi)
                    w_s[d, 0, 128 * j:128 * (j + 1), :] = wr.astype(BF16)
                    w_s[d, 1, 128 * j:128 * (j + 1), :] = wi.astype(BF16)
                pr, pi = _cmul(pr, pi, ar, ai)
            kd.append(_dot_nt(w_s[d, 0], ctr_ref[0, d]) - _dot_nt(w_s[d, 1], cti_ref[0, d]))
        for jj in range(2 * S5_CHUNK - 1):
            j = jj - (S5_CHUNK - 1)
            if j > 0:
                k = kd[0][128 * j:128 * (j + 1)]
            elif j < 0:
                k = kd[1][128 * -j:128 * (1 - j)]
            else:
                k = kd[0][0:128] + kd[1][0:128]
            k_s[128 * jj:128 * (jj + 1), :] = k
        a16_ref[0] = jnp.concatenate([pw_s[0, 0, S5_CHUNK:npw, :], pw_s[1, 0, S5_CHUNK:npw, :],
                                      pw_s[0, 1, S5_CHUNK:npw, :], pw_s[1, 1, S5_CHUNK:npw, :]], axis=1)

    for tp in range(S5_CHUNK):
        m_ref[0, :, 128 * tp:128 * (tp + 1)] = k_s[blk(S5_CHUNK - 1 + tp - t), :].astype(BF16)

    def power(d, e):
        return pw_s[d, 0, pl.ds(e, 1), :], pw_s[d, 1, pl.ds(e, 1), :]

    for d in range(2):
        j = (S5_CHUNK - 1 - t) if d == 0 else t
        win_ref[0, :, S5_PART * d:S5_PART * (d + 1)] = w_s[d, 0, blk(j), :]
        win_ref[0, :, S5_PART * (2 + d):S5_PART * (3 + d)] = w_s[d, 1, blk(j), :]
        er, ei = _cmul(*power(d, (t + 1) if d == 0 else (S5_CHUNK - t)), ctr_ref[0, d], cti_ref[0, d])
        wout_ref[0, :, S5_PART * d:S5_PART * (d + 1)] = er.astype(BF16)
        wout_ref[0, :, S5_PART * (2 + d):S5_PART * (3 + d)] = (-ei).astype(BF16)


def _s5_main_kernel(u_ref, win_ref, m_ref, wout_ref, a16_ref, h0_ref, y_ref, hfin_ref, uo_s, x_s, hs_s):
    s = pl.program_id(1)
    nq = S5_PART // 128

    @pl.when(s == 0)
    def _():
        for t in range(S5_CHUNK):
            uo_s[:, 128 * t:128 * (t + 1)] = u_ref[pl.ds(t, S5_ROWS, stride=S5_CHUNK), :].astype(BF16)

    @pl.when(s < S5_NB)
    def _():
        x = jnp.dot(uo_s[...], win_ref[0], preferred_element_type=F32)
        for q in range(nq):
            x_s[s * nq + q] = x[:, 128 * q:128 * (q + 1)]

    @pl.when(s == S5_NB - 1)
    def _():
        def run(row0, nb, nc, h):
            for c in range(nc):
                for d in range(2):
                    cc = c if d == 0 else nc - 1 - c
                    rows = pl.ds(row0 + cc, nb, stride=nc)
                    for q in range(nq):
                        kr, ki = d * nq + q, (2 + d) * nq + q
                        hr, hi = h[d][0][q], h[d][1][q]
                        hs_s[kr, rows, :] = hr
                        hs_s[ki, rows, :] = hi
                        ar = a16_ref[0, :, 128 * kr:128 * (kr + 1)]
                        ai = a16_ref[0, :, 128 * ki:128 * (ki + 1)]
                        h[d][0][q] = ar * hr - ai * hi + x_s[kr, rows, :]
                        h[d][1][q] = ar * hi + ai * hr + x_s[ki, rows, :]
            return h

        zero = jnp.zeros((BATCH, 128), F32)
        fin = run(0, BATCH, SEQ // S5_CHUNK, [[[zero] * nq, [zero] * nq] for _ in range(2)])
        for d in range(2):
            for ri in range(2):
                for q in range(nq):
                    k = (2 * ri + d) * nq + q
                    hfin_ref[0, :, 128 * k:128 * (k + 1)] = fin[d][ri][q]
        h0 = [[[h0_ref[0, :, 128 * ((2 * ri + d) * nq + q):128 * ((2 * ri + d) * nq + q + 1)] for q in range(nq)]
               for ri in range(2)] for d in range(2)]
        run(S5_ROWS_CTX, DEC_BATCH, DEC_SEQ // S5_CHUNK, h0)

    @pl.when(s >= S5_NB)
    def _():
        hs = jnp.concatenate([hs_s[k] for k in range(4 * nq)], axis=1).astype(BF16)
        y = (jnp.dot(uo_s[...], m_ref[0], preferred_element_type=F32)
             + lax.dot_general(hs, wout_ref[0], (((1,), (1,)), ((), ())), preferred_element_type=F32))
        for q in range(nq):
            t = (s - S5_NB) * nq + q
            y_ref[pl.ds(t, S5_ROWS, stride=S5_CHUNK), :] = y[:, 128 * q:128 * (q + 1)]


def _s5_octets(t, lanes):
    return t.reshape(2, S5_NOCT, 1, S5_OCT * lanes).transpose(1, 0, 2, 3)


def _s5_blockdiag(t):
    a, n = t.shape[2], t.shape[3]
    t = t.reshape(2, S5_NOCT, S5_OCT, a, n)
    bd = jnp.einsum('dogan,gh->dogahn', t, jnp.eye(S5_OCT, dtype=t.dtype))
    return bd.reshape(2, S5_NOCT, S5_OCT * a, S5_OCT * n).transpose(1, 0, 2, 3)


def _s5_state_to_lanes(h_re, h_im):
    b = h_re.shape[0]
    parts = jnp.stack([h_re[:, 0], h_re[:, 1], h_im[:, 0], h_im[:, 1]], axis=1)
    parts = parts.reshape(b, 4, S5_NOCT, S5_PART).transpose(2, 0, 1, 3)
    return parts.reshape(S5_NOCT, b, S5_SW)


def _s5_state_from_lanes(h):
    b = h.shape[1]
    parts = h.reshape(S5_NOCT, b, 4, S5_OCT, S5_STATE).transpose(1, 2, 0, 3, 4).reshape(b, 4, S5_GROUPS, S5_STATE)
    return parts[:, 0:2], parts[:, 2:4]


def _s5_scan(u, h0_re, h0_im, lam_re, lam_im, log_dt, b_re, b_im, c_re, c_im):
    ng, n = S5_GROUPS, S5_STATE
    rows = 2 * ng
    disc = pl.pallas_call(
        _s5_disc_kernel,
        out_shape=[jax.ShapeDtypeStruct((rows, n), F32)] * 4,
        name="s5_disc",
    )(lam_re.reshape(rows, n), lam_im.reshape(rows, n), log_dt.reshape(rows, 1))
    ar, ai, zr, zi = [_s5_octets(t.reshape(2, ng, n), n) for t in disc]
    bt = lambda t: _s5_blockdiag(t.transpose(0, 1, 3, 2))
    vec = pl.BlockSpec((1, 2, 1, S5_PART), lambda o, t: (o, 0, 0, 0))
    mat = pl.BlockSpec((1, 2, 128, S5_PART), lambda o, t: (o, 0, 0, 0))
    rowblk = lambda w: pl.BlockSpec((1, 128, w), lambda o, t: (o, t, 0))
    m, win, wout, a16 = pl.pallas_call(
        _s5_prep_kernel,
        grid=(S5_NOCT, S5_CHUNK),
        in_specs=[vec, vec, vec, vec, mat, mat, mat, mat],
        out_specs=[rowblk(S5_K), rowblk(S5_SW), rowblk(S5_SW), pl.BlockSpec((1, 1, S5_SW), lambda o, t: (o, 0, 0))],
        out_shape=[jax.ShapeDtypeStruct((S5_NOCT, S5_K, S5_K), BF16), jax.ShapeDtypeStruct((S5_NOCT, S5_K, S5_SW), BF16),
                   jax.ShapeDtypeStruct((S5_NOCT, S5_K, S5_SW), BF16), jax.ShapeDtypeStruct((S5_NOCT, 1, S5_SW), F32)],
        scratch_shapes=[pltpu.VMEM((2, 2, 24, S5_PART), F32), pltpu.VMEM((2, 2, S5_K, S5_PART), BF16),
                        pltpu.VMEM(((2 * S5_CHUNK - 1) * 128, 128), F32)],
        compiler_params=_cparams("arbitrary", "arbitrary"),
        name="s5_prep",
    )(ar, ai, zr, zi, bt(b_re), bt(b_im), _s5_blockdiag(c_re), _s5_blockdiag(c_im))
    nb = S5_NB
    y, hfin = pl.pallas_call(
        _s5_main_kernel,
        grid=(S5_NOCT, 2 * nb),
        in_specs=[pl.BlockSpec((N_TOK, 128), lambda o, s: (0, o)),
                  pl.BlockSpec((1, S5_K, S5_PART), lambda o, s: (o, 0, jnp.minimum(s, nb - 1))),
                  pl.BlockSpec((1, S5_K, S5_PART), lambda o, s: (o, 0, jnp.maximum(s - nb, 0))),
                  pl.BlockSpec((1, S5_PART, S5_SW), lambda o, s: (o, jnp.maximum(s - nb, 0), 0)),
                  pl.BlockSpec((1, 1, S5_SW), lambda o, s: (o, 0, 0)),
                  pl.BlockSpec((1, DEC_BATCH, S5_SW), lambda o, s: (o, 0, 0))],
        out_specs=[pl.BlockSpec((N_TOK, 128), lambda o, s: (0, o)),
                   pl.BlockSpec((1, BATCH, S5_SW), lambda o, s: (o, 0, 0))],
        out_shape=[jax.ShapeDtypeStruct((N_TOK, S5_WIDTH), F32), jax.ShapeDtypeStruct((S5_NOCT, BATCH, S5_SW), F32)],
        scratch_shapes=[pltpu.VMEM((S5_ROWS, S5_K), BF16), pltpu.VMEM((S5_SW // 128, S5_ROWS, 128), F32),
                        pltpu.VMEM((S5_SW // 128, S5_ROWS, 128), F32)],
        compiler_params=_cparams("arbitrary", "arbitrary"),
        name="s5_main",
    )(u, win, m, wout, a16, _s5_state_to_lanes(h0_re, h0_im))
    fin_re, fin_im = _s5_state_from_lanes(hfin)
    return y, fin_re, fin_im


def _ctx_or_lat(ctx_ref, lat_ref):
    return jnp.where(pl.program_id(0) < N_CTX_BLK, ctx_ref[...], lat_ref[...])


def _split_specs(width):
    return [pl.BlockSpec((ROW_BLK, width), lambda i: (jnp.minimum(i, N_CTX_BLK - 1), 0)),
            pl.BlockSpec((ROW_BLK, width), lambda i: (jnp.maximum(i - N_CTX_BLK, 0), 0))]


def _ab_out_kernel(x_ref, oac_ref, oal_ref, y_ref, u_ref, d_ref, wglu_ref, wout_ref, gt_ref, o_ref, wglu_s, wa_s, wb_s):
    @pl.when(pl.program_id(0) == 0)
    def _():
        wglu_s[...] = wglu_ref[...].astype(BF16)
        wa_s[...] = wout_ref[0:A_WIDTH, :].astype(BF16)
        wb_s[...] = wout_ref[A_WIDTH:, :].astype(BF16)

    g = jax.nn.gelu(y_ref[...] + d_ref[...] * u_ref[...])
    ob = g * jax.nn.sigmoid(jnp.dot(g.astype(BF16), wglu_s[...], preferred_element_type=F32))
    out = (jnp.dot(_ctx_or_lat(oac_ref, oal_ref).astype(BF16), wa_s[...], preferred_element_type=F32)
           + jnp.dot(ob.astype(BF16), wb_s[...], preferred_element_type=F32))
    o_ref[...] = x_ref[...] + gt_ref[0] * out


def _ab_out(x, oa, y, u, d_skip, w_glu, w_out, mod, layer):
    row = lambda n: pl.BlockSpec((ROW_BLK, n), lambda i: (i, 0))
    full = lambda a, b: pl.BlockSpec((a, b), lambda i: (0, 0))
    return pl.pallas_call(
        _ab_out_kernel,
        grid=(N_ROW_BLK,),
        in_specs=[row(D_MODEL)] + _split_specs(A_WIDTH) + [row(S5_WIDTH), row(S5_WIDTH), full(1, S5_WIDTH),
                  full(S5_WIDTH, S5_WIDTH), full(A_WIDTH + S5_WIDTH, D_MODEL), _mod_spec(layer, 2)],
        out_specs=row(D_MODEL),
        out_shape=jax.ShapeDtypeStruct((N_TOK, D_MODEL), F32),
        scratch_shapes=[pltpu.VMEM((S5_WIDTH, S5_WIDTH), BF16), pltpu.VMEM((A_WIDTH, D_MODEL), BF16),
                        pltpu.VMEM((S5_WIDTH, D_MODEL), BF16)],
        compiler_params=_cparams("arbitrary"),
        name="ab_out",
    )(x, oa[0], oa[1], y, u, d_skip.reshape(1, S5_WIDTH), w_glu, w_out, mod)


CD_HG0 = MLA_Q_RANK + MLA_KV_RANK + MLA_ROPE
KPE_LANES = 128


def _cd_in_kernel(x_ref, g_ref, sc_ref, sh_ref, w_ref, gq_ref, wqu_ref, gkv_ref, wkvu_ref, cos_ref, sa_ref, sb_ref,
                  qn_ref, qr_ref, ckv_ref, kpe_ref, kn_ref, vm_ref, hq_ref, hff_ref, hfb_ref, hi_ref, hg_ref,
                  wcq_s, wckv_s, wkpe_s, whg_s, wqu_s, wkvu_s):
    @pl.when(pl.program_id(0) == 0)
    def _():
        wcq_s[...] = w_ref[:, 0:MLA_Q_RANK].astype(BF16)
        wckv_s[...] = w_ref[:, MLA_Q_RANK:MLA_Q_RANK + MLA_KV_RANK].astype(BF16)
        kp = w_ref[:, MLA_Q_RANK + MLA_KV_RANK:CD_HG0].astype(BF16)
        wkpe_s[...] = jnp.concatenate([kp] * (KPE_LANES // MLA_ROPE), axis=1)
        whg_s[...] = w_ref[:, CD_HG0:].astype(BF16)
        wqu_s[...] = wqu_ref[...].astype(BF16)
        wkvu_s[...] = wkvu_ref[...].astype(BF16)

    h = _norm_mod(x_ref[...], g_ref[...], sc_ref[0], sh_ref[0]).astype(BF16)
    cos, sa, sb = cos_ref[...], sa_ref[...], sb_ref[...]
    nope = MLA_HEADS * MLA_NOPE
    cq = _rms(jnp.dot(h, wcq_s[...], preferred_element_type=F32)) * gq_ref[...]
    qq = jnp.dot(cq.astype(BF16), wqu_s[...], preferred_element_type=F32)
    qn_ref[...] = qq[:, 0:nope].astype(qn_ref.dtype)
    qr_ref[...] = _apply_rope(qq[:, nope:], cos, sa, sb, MLA_ROPE // 4).astype(qr_ref.dtype)
    ckv = _rms(jnp.dot(h, wckv_s[...], preferred_element_type=F32)) * gkv_ref[...]
    ckv_ref[...] = ckv
    kv = jnp.dot(ckv.astype(BF16), wkvu_s[...], preferred_element_type=F32)
    kn_ref[...] = kv[:, 0:nope].astype(kn_ref.dtype)
    vm_ref[...] = kv[:, nope:].astype(vm_ref.dtype)
    kpe_ref[...] = _apply_rope(jnp.dot(h, wkpe_s[...], preferred_element_type=F32), cos, sa, sb, MLA_ROPE // 4)
    hh = jnp.dot(h, whg_s[...], preferred_element_type=F32)
    for n, ref in enumerate((hq_ref, hff_ref, hfb_ref, hi_ref, hg_ref)):
        ref[...] = hh[:, HG_KD * n:HG_KD * (n + 1)]


def _mla_split_heads(w, a):
    k, n = w.shape
    w = w.reshape(k, MLA_HEADS, n // MLA_HEADS)
    return jnp.concatenate([w[:, :, :a].reshape(k, -1), w[:, :, a:].reshape(k, -1)], axis=1)


def _cd_in(x, mod, g, w, g_q, w_q_up, g_kv, w_kv_up, layer):
    cos, sa, sb = _rope_tables(MLA_ROPE)
    d_in = w.shape[1]
    row = lambda n: pl.BlockSpec((ROW_BLK, n), lambda i: (i, 0))
    full = lambda a, b: pl.BlockSpec((a, b), lambda i: (0, 0))
    rope = pl.BlockSpec((ROW_BLK, 128), lambda i: (_rope_blk(i), 0))
    widths = [MLA_HEADS * MLA_NOPE, MLA_HEADS * MLA_ROPE, MLA_KV_RANK, KPE_LANES, MLA_HEADS * MLA_NOPE, MLA_WIDTH] + [HG_KD] * 5
    nq = MLA_HEADS * (MLA_NOPE + MLA_ROPE)
    nkv = MLA_HEADS * (MLA_NOPE + MLA_V)
    return pl.pallas_call(
        _cd_in_kernel,
        grid=(N_ROW_BLK,),
        in_specs=[row(D_MODEL), full(1, D_MODEL), _mod_spec(layer, 1), _mod_spec(layer, 0), full(D_MODEL, d_in),
                  full(1, MLA_Q_RANK), full(MLA_Q_RANK, nq), full(1, MLA_KV_RANK), full(MLA_KV_RANK, nkv), rope, rope, rope],
        out_specs=[row(n) for n in widths],
        out_shape=[jax.ShapeDtypeStruct((N_TOK, n), BF16 if i in (0, 1, 4, 5) else F32) for i, n in enumerate(widths)],
        scratch_shapes=[pltpu.VMEM((D_MODEL, MLA_Q_RANK), BF16), pltpu.VMEM((D_MODEL, MLA_KV_RANK), BF16),
                        pltpu.VMEM((D_MODEL, KPE_LANES), BF16), pltpu.VMEM((D_MODEL, 5 * HG_KD), BF16),
                        pltpu.VMEM((MLA_Q_RANK, nq), BF16), pltpu.VMEM((MLA_KV_RANK, nkv), BF16)],
        compiler_params=_cparams("arbitrary"),
        name="cd_in",
    )(x, g.reshape(1, D_MODEL), mod, mod, w, g_q.reshape(1, -1), _mla_split_heads(w_q_up, MLA_NOPE),
      g_kv.reshape(1, -1), _mla_split_heads(w_kv_up, MLA_NOPE), cos, sa, sb)


def _mm_kernel(a_ref, w_ref, *o_refs):
    r = _dot(a_ref[...], w_ref[...])
    off = 0
    for o in o_refs:
        o[...] = r[:, off:off + o.shape[1]]
        off += o.shape[1]


def _mla_cache_kv(cckv, w_kv_up):
    n = MLA_HEADS * MLA_NOPE
    rows = cckv.shape[0]
    return pl.pallas_call(
        _mm_kernel,
        grid=(rows // PAST_LEN,),
        in_specs=[pl.BlockSpec((PAST_LEN, MLA_KV_RANK), lambda i: (i, 0)),
                  pl.BlockSpec((MLA_KV_RANK, 2 * n), lambda i: (0, 0))],
        out_specs=[pl.BlockSpec((PAST_LEN, n), lambda i: (i, 0))] * 2,
        out_shape=[jax.ShapeDtypeStruct((rows, n), F32)] * 2,
        compiler_params=_cparams("arbitrary"),
        name="mla_cache_kv",
    )(cckv, _mla_split_heads(w_kv_up, MLA_NOPE))


def _mla_heads(qn, qr, keys, o_ref):
    scale = (MLA_NOPE + MLA_ROPE) ** -0.5
    outs = []
    for h in range(MLA_HEADS):
        a = slice(MLA_NOPE * h, MLA_NOPE * (h + 1))
        r = slice(MLA_ROPE * h, MLA_ROPE * (h + 1))
        s_list = [(_dot_nt(qn[:, a], kn[:, a]) + _dot_nt(qr[:, r], kp[:, 0:MLA_ROPE])) * scale for kn, kp, _ in keys]
        outs.append(_softmax_pv(s_list, [v[:, a] for _, _, v in keys], None))
    o_ref[...] = jnp.concatenate(outs, axis=1).astype(o_ref.dtype)


def _mla_ctx_kernel(qn_ref, qr_ref, kn_ref, kp_ref, v_ref, o_ref):
    _mla_heads(qn_ref[...], qr_ref[...], [(kn_ref[...], kp_ref[...], v_ref[...])], o_ref)


def _mla_lat_kernel(qn_ref, qr_ref, kn_ref, kp_ref, v_ref, kcn_ref, kcp_ref, vc_ref, o_ref):
    _mla_heads(qn_ref[...], qr_ref[...],
               [(kcn_ref[...], kcp_ref[0], vc_ref[...]), (kn_ref[...], kp_ref[...], v_ref[...])], o_ref)


def _attention_mla(qn, qr, kn, kpe, vm, kcn, kcpe, vc):
    n = MLA_HEADS * MLA_NOPE
    nr = MLA_HEADS * MLA_ROPE
    blk = lambda w: pl.BlockSpec((SEQ, w), lambda b: (b, 0))
    o = pl.pallas_call(
        _mla_ctx_kernel,
        grid=(BATCH,),
        in_specs=[blk(n), blk(nr), blk(n), blk(KPE_LANES), blk(MLA_WIDTH)],
        out_specs=blk(MLA_WIDTH),
        out_shape=jax.ShapeDtypeStruct((N_CTX_TOK, MLA_WIDTH), BF16),
        compiler_params=_cparams("arbitrary"),
        name="mla_ctx",
    )(qn, qr, kn, kpe, vm)
    nq = DEC_SEQ // ROW_BLK
    qblk = lambda w: pl.BlockSpec((ROW_BLK, w), lambda b, i: (N_CTX_BLK + b * nq + i, 0))
    seq = lambda w: pl.BlockSpec((DEC_SEQ, w), lambda b, i: (N_CTX_TOK // DEC_SEQ + b, 0))
    past = lambda w: pl.BlockSpec((PAST_LEN, w), lambda b, i: (b, 0))
    o_lat = pl.pallas_call(
        _mla_lat_kernel,
        grid=(DEC_BATCH, nq),
        in_specs=[qblk(n), qblk(nr), seq(n), seq(KPE_LANES), seq(MLA_WIDTH), past(n),
                  pl.BlockSpec((1, PAST_LEN, MLA_ROPE), lambda b, i: (b, 0, 0)), past(MLA_WIDTH)],
        out_specs=pl.BlockSpec((ROW_BLK, MLA_WIDTH), lambda b, i: (b * nq + i, 0)),
        out_shape=jax.ShapeDtypeStruct((N_LAT_TOK, MLA_WIDTH), BF16),
        compiler_params=_cparams("arbitrary", "arbitrary"),
        name="mla_lat",
    )(qn, qr, kn, kpe, vm, kcn, kcpe, vc)
    return o, o_lat


HG_TILE = 128
HG_NC = HG_TILE // HG_CHUNK
HG_HALF = 256


HG_SLABS = HG_KD // 128


def _hg_token_plane(raw_s, p):
    return jnp.concatenate([raw_s[j, pl.ds(p, HG_NC, stride=HG_CHUNK), :] for j in range(HG_SLABS)], axis=1)


def _hg_put(ref, x):
    for j in range(HG_SLABS):
        ref[j] = x[:, 128 * j:128 * (j + 1)]


def _hg_get(ref, c, l):
    rows = pl.ds(c, HG_CHUNK, stride=HG_NC)
    return jnp.concatenate([ref[2 * l, rows, :], ref[2 * l + 1, rows, :]], axis=1)


def _hg_plane(ref, p):
    return jnp.concatenate([ref[j, HG_NC * p:HG_NC * (p + 1), :] for j in range(HG_SLABS)], axis=1)


def _hg_put_plane(ref, p, x):
    for j in range(HG_SLABS):
        ref[j, HG_NC * p:HG_NC * (p + 1), :] = x[:, 128 * j:128 * (j + 1)]


HG_PAIRS = HG_CHUNK * (HG_CHUNK + 1) // 2
HG_PAIR_ROWS = HG_PAIRS * HG_NC


def _hg_pair_rows(d, i):
    n = d * HG_CHUNK - d * (d - 1) // 2 + (i - d)
    return slice(HG_NC * n, HG_NC * (n + 1))


def _hg_direction(hq_ref, hf_ref, hi_ref, o_ref, lb, st_ref, raw_s, q_s, f_s, kk_s, v_s, qt_s, kt_s, o_s, p_s, e_s,
                  sign, ones_bd, head_mask):
    _hg_put(raw_s, hq_ref[...])
    for p in range(HG_CHUNK):
        hq = _hg_token_plane(raw_s, p)
        _hg_put_plane(q_s, p, hq * jax.nn.sigmoid(hq))
    _hg_put(raw_s, hf_ref[...])
    for p in range(HG_CHUNK):
        f = lb + (1.0 - lb) * jax.nn.sigmoid(_hg_token_plane(raw_s, p))
        _hg_put_plane(f_s, p, f)
        _hg_put_plane(kk_s, p, 1.0 - f)
    _hg_put(raw_s, hi_ref[...])
    for p in range(HG_CHUNK):
        _hg_put_plane(v_s, p, _hg_token_plane(raw_s, p))
    pos = (lambda i: i) if sign > 0 else (lambda i: HG_CHUNK - 1 - i)
    plane = lambda ref, i: _hg_plane(ref, pos(i))

    for i in range(HG_CHUNK):
        q = plane(q_s, i)
        p_s[_hg_pair_rows(0, i), :] = q * plane(kk_s, i)
        dec = None
        for d in range(1, i + 1):
            fd = plane(f_s, i - d + 1)
            dec = fd if dec is None else dec * fd
            p_s[_hg_pair_rows(d, i), :] = q * plane(kk_s, i - d) * dec
    step = HG_PAIR_ROWS // 4
    for c in range(0, HG_PAIR_ROWS, step):
        pb = p_s[c:c + step, :].astype(BF16)
        e_s[c:c + step, 0:HG_HALF] = jnp.dot(pb[:, 0:HG_HALF], ones_bd, preferred_element_type=F32)
        e_s[c:c + step, HG_HALF:] = jnp.dot(pb[:, HG_HALF:], ones_bd, preferred_element_type=F32)
    for i in range(HG_CHUNK):
        o = None
        for d in range(i + 1):
            t = e_s[_hg_pair_rows(d, i), :] * plane(v_s, i - d)
            o = t if o is None else o + t
        _hg_put_plane(o_s, pos(i), o)

    incl = None
    for i in range(HG_CHUNK):
        fi = plane(f_s, i)
        incl = fi if incl is None else incl * fi
        _hg_put_plane(qt_s, pos(i), plane(q_s, i) * incl)
    whole = incl
    excl = None
    for i in range(HG_CHUNK - 1, -1, -1):
        kt = plane(kk_s, i)
        if excl is not None:
            kt = kt * excl
        _hg_put_plane(kt_s, pos(i), kt)
        fi = plane(f_s, i)
        excl = fi if excl is None else excl * fi

    tile4 = lambda x: jnp.where(head_mask, jnp.concatenate([x] * 4, axis=0), 0.0).astype(BF16)
    for cc in range(HG_NC):
        c = cc if sign > 0 else HG_NC - 1 - cc
        r = slice(HG_CHUNK * c, HG_CHUNK * (c + 1))
        for g in range(2):
            l = slice(HG_HALF * g, HG_HALF * (g + 1))
            st = st_ref[g]
            oc = _dot_nt(tile4(_hg_get(qt_s, c, g)), st)
            o_ref[r, l] = _hg_get(o_s, c, g) + jnp.concatenate(
                [oc[HG_CHUNK * h:HG_CHUNK * (h + 1), :] for h in range(4)], axis=1)
            v = hi_ref[r, l]
            vs = jnp.concatenate([v[:, HG_VAL * h:HG_VAL * (h + 1)] for h in range(4)], axis=0)
            ds = lax.dot_general(vs.astype(BF16), tile4(_hg_get(kt_s, c, g)), (((0,), (0,)), ((), ())),
                                 preferred_element_type=F32)
            st_ref[g] = st * whole[c:c + 1, l] + ds


def _hg_lower_bound(lb_ref, d, layer):
    raw = [lb_ref[d * DEPTH + m:d * DEPTH + m + 1, :] for m in range(DEPTH)]
    mx = functools.reduce(jnp.maximum, raw)
    e = [jnp.exp(r - mx) for r in raw]
    tot = functools.reduce(jnp.add, e)
    return functools.reduce(jnp.add, e[1:layer + 1], jnp.zeros_like(tot)) / tot


def _hgrn_kernel(nt, layer, hqf_ref, hff_ref, hif_ref, hqb_ref, hfb_ref, hib_ref, lb_ref, s0_ref,
                 of_ref, ob_ref, sfin_ref, st_s, raw_s, q_s, f_s, kk_s, v_s, qt_s, kt_s, o_s, p_s, e_s):
    i = pl.program_id(1)

    @pl.when(i == 0)
    def _():
        st_s[...] = s0_ref[0]

    ri = lax.broadcasted_iota(jnp.int32, (HG_HALF, HG_HALF), 0) // HG_KEY
    ci = lax.broadcasted_iota(jnp.int32, (HG_HALF, HG_HALF), 1) // HG_KEY
    ones_bd = jnp.where(ri == ci, 1.0, 0.0).astype(BF16)
    head_mask = (lax.broadcasted_iota(jnp.int32, (4 * HG_CHUNK, HG_HALF), 0) // HG_CHUNK
                 == lax.broadcasted_iota(jnp.int32, (4 * HG_CHUNK, HG_HALF), 1) // HG_KEY)
    _hg_direction(hqf_ref, hff_ref, hif_ref, of_ref, _hg_lower_bound(lb_ref, 0, layer), st_s.at[0],
                  raw_s, q_s, f_s, kk_s, v_s, qt_s, kt_s, o_s, p_s, e_s, 1, ones_bd, head_mask)
    _hg_direction(hqb_ref, hfb_ref, hib_ref, ob_ref, _hg_lower_bound(lb_ref, 1, layer), st_s.at[1],
                  raw_s, q_s, f_s, kk_s, v_s, qt_s, kt_s, o_s, p_s, e_s, -1, ones_bd, head_mask)

    @pl.when(i == nt - 1)
    def _():
        sfin_ref[0] = st_s[...]


def _hg_state_to_blocks(s):
    b = s.shape[0]
    st = s.reshape(b, 2, 2, 4, HG_KEY, HG_VAL).transpose(0, 1, 2, 5, 3, 4)
    return st.reshape(b, 2, 2, HG_VAL, HG_HALF)


def _hg_state_from_blocks(st):
    b = st.shape[0]
    st = st.reshape(b, 2, 2, HG_VAL, 4, HG_KEY).transpose(0, 1, 2, 4, 5, 3)
    return st.reshape(b, 2, HG_HEADS, HG_KEY, HG_VAL)


def _hgrn_scan(hq, hff, hfb, hi, lb, s0, layer, row0, nseq, seqlen):
    nt = seqlen // HG_TILE
    base = row0 // HG_TILE
    fwd = pl.BlockSpec((HG_TILE, HG_KD), lambda b, i: (base + b * nt + i, 0))
    bwd = pl.BlockSpec((HG_TILE, HG_KD), lambda b, i: (base + b * nt + nt - 1 - i, 0))
    ofw = pl.BlockSpec((HG_TILE, HG_KD), lambda b, i: (b * nt + i, 0))
    obw = pl.BlockSpec((HG_TILE, HG_KD), lambda b, i: (b * nt + nt - 1 - i, 0))
    st = pl.BlockSpec((1, 2, 2, HG_VAL, HG_HALF), lambda b, i: (b, 0, 0, 0, 0))
    tile = lambda: pltpu.VMEM((HG_SLABS, HG_TILE, 128), F32)
    pairs = lambda: pltpu.VMEM((HG_PAIR_ROWS, HG_KD), F32)
    return pl.pallas_call(
        functools.partial(_hgrn_kernel, nt, layer),
        grid=(nseq, nt),
        in_specs=[fwd, fwd, fwd, bwd, bwd, bwd, pl.BlockSpec((2 * DEPTH, HG_KD), lambda b, i: (0, 0)), st],
        out_specs=[ofw, obw, st],
        out_shape=[jax.ShapeDtypeStruct((nseq * seqlen, HG_WIDTH), F32)] * 2
        + [jax.ShapeDtypeStruct((nseq, 2, 2, HG_VAL, HG_HALF), F32)],
        scratch_shapes=[pltpu.VMEM((2, 2, HG_VAL, HG_HALF), F32)] + [tile() for _ in range(8)] + [pairs(), pairs()],
        compiler_params=_cparams("arbitrary", "arbitrary"),
        name="hgrn_scan",
    )(hq, hff, hi, hq, hfb, hi, lb.reshape(2 * DEPTH, HG_KD), s0)


def _cd_out_kernel(x_ref, occ_ref, ocl_ref, ofc_ref, ofl_ref, obc_ref, obl_ref, hg_ref, go_ref, wout_ref, gt_ref,
                   o_ref, wa_s, wb_s):
    @pl.when(pl.program_id(0) == 0)
    def _():
        wa_s[...] = wout_ref[0:MLA_WIDTH, :].astype(BF16)
        wb_s[...] = wout_ref[MLA_WIDTH:, :].astype(BF16)

    ri = lax.broadcasted_iota(jnp.int32, (HG_HALF, HG_HALF), 0) // HG_VAL
    ci = lax.broadcasted_iota(jnp.int32, (HG_HALF, HG_HALF), 1) // HG_VAL
    ones_bd = jnp.where(ri == ci, 1.0, 0.0).astype(BF16)
    o = _ctx_or_lat(ofc_ref, ofl_ref) + _ctx_or_lat(obc_ref, obl_ref)
    sq = o * o
    hi = sq.astype(BF16)
    lo = (sq - hi.astype(F32)).astype(BF16)
    ms = jnp.concatenate(
        [jnp.dot(hi[:, l], ones_bd, preferred_element_type=F32) + jnp.dot(lo[:, l], ones_bd, preferred_element_type=F32)
         for l in (slice(0, HG_HALF), slice(HG_HALF, 2 * HG_HALF))], axis=1) * (1.0 / HG_VAL)
    hg = hg_ref[...]
    od = o * lax.rsqrt(ms + EPS) * go_ref[...] * (hg * jax.nn.sigmoid(hg))
    out = (jnp.dot(_ctx_or_lat(occ_ref, ocl_ref).astype(BF16), wa_s[...], preferred_element_type=F32)
           + jnp.dot(od.astype(BF16), wb_s[...], preferred_element_type=F32))
    o_ref[...] = x_ref[...] + gt_ref[0] * out


def _cd_out(x, oc, of, ob, hg, g_o, w_out, mod, layer):
    row = lambda n: pl.BlockSpec((ROW_BLK, n), lambda i: (i, 0))
    full = lambda a, b: pl.BlockSpec((a, b), lambda i: (0, 0))
    return pl.pallas_call(
        _cd_out_kernel,
        grid=(N_ROW_BLK,),
        in_specs=[row(D_MODEL)] + _split_specs(MLA_WIDTH) + _split_specs(HG_WIDTH) + _split_specs(HG_WIDTH)
        + [row(HG_WIDTH), full(1, HG_WIDTH), full(MLA_WIDTH + HG_WIDTH, D_MODEL), _mod_spec(layer, 2)],
        out_specs=row(D_MODEL),
        out_shape=jax.ShapeDtypeStruct((N_TOK, D_MODEL), F32),
        scratch_shapes=[pltpu.VMEM((MLA_WIDTH, D_MODEL), BF16), pltpu.VMEM((HG_WIDTH, D_MODEL), BF16)],
        compiler_params=_cparams("arbitrary"),
        name="cd_out",
    )(x, oc[0], oc[1], of[0], of[1], ob[0], ob[1], hg, jnp.tile(g_o, HG_HEADS).reshape(1, HG_WIDTH), w_out, mod)


N_PAIRS = 6
N_CLASSES = N_GROUPS * N_PAIRS
CLS_ROWS = 32
MOE_BM = 256
MOE_NBLK = N_TOK // MOE_BM + N_CLASSES
MOE_ROWS = MOE_NBLK * MOE_BM
PAIR_SLOTS = ((0, 1), (3, 1), (2, 1), (2, 0), (3, 0), (3, 2))


def _moe_route_kernel(x_ref, g_ref, sc_ref, sh_ref, rw_ref, rb_ref, h_ref, ti_ref, tw_ref, cnt_ref, base_s):
    @pl.when(pl.program_id(0) == 0)
    def _():
        base_s[...] = jnp.zeros_like(base_s)

    h = _norm_mod(x_ref[...], g_ref[...], sc_ref[0], sh_ref[0])
    logits = lax.dot_general(rw_ref[...], h, (((1,), (1,)), ((), ())), precision=HI,
                             preferred_element_type=F32)
    aff = jax.nn.sigmoid(logits)
    sel = aff + rb_ref[...]
    s = [sel[e:e + 1, :] for e in range(N_EXPERTS)]
    a = [aff[e:e + 1, :] for e in range(N_EXPERTS)]
    gs = []
    for g in range(N_GROUPS):
        m = s[4 * g:4 * g + 4]
        pairs = [m[i] + m[j] for i in range(4) for j in range(i + 1, 4)]
        gs.append(functools.reduce(jnp.maximum, pairs))
    gmax = functools.reduce(jnp.maximum, gs)
    taken = jnp.zeros_like(gmax) > 1.0
    gsel = []
    for g in range(N_GROUPS):
        hit = (gs[g] == gmax) & jnp.logical_not(taken)
        gsel.append(hit)
        taken = taken | hit
    e_lo = jnp.zeros(gmax.shape, jnp.int32)
    e_hi = jnp.zeros(gmax.shape, jnp.int32)
    a_lo = jnp.zeros_like(gmax)
    a_hi = jnp.zeros_like(gmax)
    nsel = jnp.zeros(gmax.shape, jnp.int32)
    for g in range(N_GROUPS):
        for i in range(4):
            e = 4 * g + i
            beat = jnp.zeros(gmax.shape, jnp.int32)
            for j in range(4):
                if j != i:
                    o = 4 * g + j
                    beat = beat + jnp.where((s[o] > s[e]) | ((s[o] == s[e]) & (j < i)), 1, 0)
            pick = gsel[g] & (beat < 2)
            is_first = pick & (nsel == 0)
            is_second = pick & (nsel == 1)
            e_lo = jnp.where(is_first, e, e_lo)
            a_lo = jnp.where(is_first, a[e], a_lo)
            e_hi = jnp.where(is_second, e, e_hi)
            a_hi = jnp.where(is_second, a[e], a_hi)
            nsel = nsel + jnp.where(pick, 1, 0)
    grp = e_lo // EXPERTS_PER_GROUP
    lo = e_lo - grp * EXPERTS_PER_GROUP
    hi = e_hi - grp * EXPERTS_PER_GROUP
    pair = ((lo * (7 - lo)) >> 1) + (hi - lo - 1)
    pair = jnp.where(pair == 1, 3, jnp.where(pair == 2, 4, jnp.where(pair == 3, 2, jnp.where(pair == 4, 1, pair))))
    cls = grp * N_PAIRS + pair
    wsum = a_lo + a_hi
    w_lo, w_hi = a_lo / wsum, a_hi / wsum
    onehot = (lax.broadcasted_iota(jnp.int32, (CLS_ROWS, ROW_BLK), 0) == cls).astype(F32)
    tt = lax.broadcasted_iota(jnp.int32, (ROW_BLK, ROW_BLK), 0) < lax.broadcasted_iota(jnp.int32, (ROW_BLK, ROW_BLK), 1)
    before = _dot(onehot, jnp.where(tt, 1.0, 0.0))
    base = base_s[...]
    rank = jnp.sum(onehot * (before + base[:, 0:1]), axis=0, keepdims=True).astype(jnp.int32)
    base = base + jnp.sum(onehot, axis=1, keepdims=True)
    base_s[...] = base
    cnt_ref[...] = base.astype(jnp.int32)
    ti_ref[0] = jnp.concatenate([cls, rank, e_lo, e_hi, jnp.zeros((4, ROW_BLK), jnp.int32)], axis=0)
    ident = lax.broadcasted_iota(jnp.int32, (ROW_BLK, ROW_BLK), 0) == lax.broadcasted_iota(jnp.int32, (ROW_BLK, ROW_BLK), 1)
    col = lambda r: jnp.sum(jnp.where(ident, r, 0.0), axis=1, keepdims=True)
    tw_ref[...] = jnp.where(lax.broadcasted_iota(jnp.int32, (ROW_BLK, 128), 1) < 64, col(w_lo), col(w_hi))
    h_ref[...] = h


def _moe_route(x, mod, g, router_w, router_b, layer):
    row = lambda n: pl.BlockSpec((ROW_BLK, n), lambda i: (i, 0))
    full = lambda a, b: pl.BlockSpec((a, b), lambda i: (0, 0))
    return pl.pallas_call(
        _moe_route_kernel,
        grid=(N_ROW_BLK,),
        in_specs=[row(D_MODEL), full(1, D_MODEL), _mod_spec(layer, 4), _mod_spec(layer, 3),
                  full(N_EXPERTS, D_MODEL), full(N_EXPERTS, 1)],
        out_specs=[row(D_MODEL), pl.BlockSpec((1, 8, ROW_BLK), lambda i: (i, 0, 0)), row(128), full(CLS_ROWS, 128)],
        out_shape=[jax.ShapeDtypeStruct((N_TOK, D_MODEL), F32),
                   jax.ShapeDtypeStruct((N_ROW_BLK, 8, ROW_BLK), jnp.int32),
                   jax.ShapeDtypeStruct((N_TOK, 128), F32),
                   jax.ShapeDtypeStruct((CLS_ROWS, 128), jnp.int32)],
        scratch_shapes=[pltpu.VMEM((CLS_ROWS, 128), F32)],
        compiler_params=_cparams("arbitrary"),
        name="moe_route",
    )(x, g.reshape(1, D_MODEL), mod, mod, router_w.T, router_b.reshape(N_EXPERTS, 1))


def _moe_sort_kernel(pos_ref, nblk_ref, swap_ref, h_ref, wt_ref, hs_ref, ws_ref, perm_ref, stage_s, wstage_s):
    j = pl.program_id(0)

    @pl.when(j == 0)
    def _():
        def init(r, c):
            perm_ref[r] = N_TOK
            return c

        lax.fori_loop(0, MOE_ROWS, init, 0, unroll=16)

        def build(t, c):
            perm_ref[pos_ref[t]] = t
            return c

        lax.fori_loop(0, N_TOK, build, 0, unroll=16)

    @pl.when(j < nblk_ref[0])
    def _():
        base = j * MOE_BM
        for r in range(MOE_BM):
            src = jnp.minimum(perm_ref[base + r], N_TOK - 1)
            stage_s[r:r + 1, :] = h_ref[pl.ds(src, 1), :]
            wstage_s[r:r + 1, :] = wt_ref[pl.ds(src, 1), :]
        hs_ref[...] = stage_s[...].astype(BF16)
        w = wstage_s[...]
        ws_ref[...] = jnp.where(swap_ref[j] == 1, pltpu.roll(w, 64, axis=1), w)

    @pl.when(j >= nblk_ref[0])
    def _():
        hs_ref[...] = jnp.zeros_like(hs_ref)
        ws_ref[...] = jnp.zeros_like(ws_ref)


def _moe_sort(h, wtok, pos, nblk, swap):
    res = lambda w: pl.BlockSpec((N_TOK, w), lambda j, *_: (0, 0), pipeline_mode=pl.Buffered(1))
    grid_spec = pltpu.PrefetchScalarGridSpec(
        num_scalar_prefetch=3,
        grid=(MOE_NBLK,),
        in_specs=[res(D_MODEL), res(128)],
        out_specs=[pl.BlockSpec((MOE_BM, D_MODEL), lambda j, *_: (j, 0)),
                   pl.BlockSpec((MOE_BM, 128), lambda j, *_: (j, 0)),
                   pl.BlockSpec(memory_space=pltpu.SMEM)],
        scratch_shapes=[pltpu.VMEM((MOE_BM, D_MODEL), F32), pltpu.VMEM((MOE_BM, 128), F32)],
    )
    return pl.pallas_call(
        _moe_sort_kernel,
        grid_spec=grid_spec,
        out_shape=[jax.ShapeDtypeStruct((MOE_ROWS, D_MODEL), BF16), jax.ShapeDtypeStruct((MOE_ROWS, 128), F32),
                   jax.ShapeDtypeStruct((MOE_ROWS,), jnp.int32)],
        compiler_params=_cparams("arbitrary"),
        name="moe_sort",
    )(pos, nblk, swap, h, wtok)


def _moe_experts_kernel(elo_ref, ehi_ref, nblk_ref, hs_ref, ws_ref, wgl_ref, wul_ref, wdl_ref,
                        wgh_ref, wuh_ref, wdh_ref, y_ref):
    del elo_ref, ehi_ref
    j = pl.program_id(0)

    @pl.when(j < nblk_ref[0])
    def _():
        h = hs_ref[...]
        acc = None
        for wg, wu, wd, gate in ((wgl_ref, wul_ref, wdl_ref, ws_ref[:, 0:1]), (wgh_ref, wuh_ref, wdh_ref, ws_ref[:, 64:65])):
            g = _dot(h, wg[0, 0])
            u = _dot(h, wu[0, 0])
            hid = g * jax.nn.sigmoid(g) * u * gate
            y = _dot(hid, wd[0, 0])
            acc = y if acc is None else acc + y
        y_ref[...] = acc

    @pl.when(j >= nblk_ref[0])
    def _():
        y_ref[...] = jnp.zeros_like(y_ref)


def _moe_experts(hs, ws, blk_elo, blk_ehi, nblk, w_gate, w_up, w_down, layer):
    last = lambda j, nb: jnp.minimum(j, nb[0] - 1)
    wspec = lambda a, b, which: pl.BlockSpec(
        (1, 1, a, b), lambda j, elo, ehi, nb: (layer, (elo, ehi)[which][last(j, nb)], 0, 0))
    grid_spec = pltpu.PrefetchScalarGridSpec(
        num_scalar_prefetch=3,
        grid=(MOE_NBLK,),
        in_specs=[pl.BlockSpec((MOE_BM, D_MODEL), lambda j, elo, ehi, nb: (last(j, nb), 0)),
                  pl.BlockSpec((MOE_BM, 128), lambda j, elo, ehi, nb: (last(j, nb), 0)),
                  wspec(D_MODEL, D_FF, 0), wspec(D_MODEL, D_FF, 0), wspec(D_FF, D_MODEL, 0),
                  wspec(D_MODEL, D_FF, 1), wspec(D_MODEL, D_FF, 1), wspec(D_FF, D_MODEL, 1)],
        out_specs=pl.BlockSpec((MOE_BM, D_MODEL), lambda j, *_: (j, 0)),
    )
    return pl.pallas_call(
        _moe_experts_kernel,
        grid_spec=grid_spec,
        out_shape=jax.ShapeDtypeStruct((MOE_ROWS, D_MODEL), F32),
        compiler_params=_cparams("arbitrary"),
        name="moe_experts",
    )(blk_elo, blk_ehi, nblk, hs, ws, w_gate, w_up, w_down, w_gate, w_up, w_down)


MOE_DUMP = 8


def _moe_combine_kernel(perm_ref, nblk_ref, y_ref, x_ref, gt_ref, o_ref, ytok_s):
    s = pl.program_id(0)

    @pl.when(s == 0)
    def _():
        ytok_s[N_TOK:, :] = jnp.zeros((MOE_DUMP, D_MODEL), F32)

    @pl.when(s < nblk_ref[0])
    def _():
        base = s * MOE_BM
        for r in range(MOE_BM):
            ytok_s[pl.ds(perm_ref[base + r], 1), :] = y_ref[r:r + 1, :]

    @pl.when(s >= MOE_NBLK)
    def _():
        i = s - MOE_NBLK
        o_ref[...] = x_ref[...] + gt_ref[0] * ytok_s[pl.ds(pl.multiple_of(i * ROW_BLK, ROW_BLK), ROW_BLK), :]


def _moe_combine(x, y_sorted, perm, nblk, mod, layer):
    tok = lambda s: jnp.maximum(s - MOE_NBLK, 0)
    grid_spec = pltpu.PrefetchScalarGridSpec(
        num_scalar_prefetch=2,
        grid=(MOE_NBLK + N_ROW_BLK,),
        in_specs=[pl.BlockSpec((MOE_BM, D_MODEL), lambda s, perm, nb: (jnp.minimum(s, nb[0] - 1), 0)),
                  pl.BlockSpec((ROW_BLK, D_MODEL), lambda s, perm, nb: (tok(s), 0)),
                  pl.BlockSpec((1, 1, D_MODEL),
                               lambda s, perm, nb: ((layer * MOD_ROWS + _mod_group(tok(s))) * N_MOD + 5, 0, 0))],
        out_specs=pl.BlockSpec((ROW_BLK, D_MODEL), lambda s, perm, nb: (tok(s), 0)),
        scratch_shapes=[pltpu.VMEM((N_TOK + MOE_DUMP, D_MODEL), F32)],
    )
    return pl.pallas_call(
        _moe_combine_kernel,
        grid_spec=grid_spec,
        out_shape=jax.ShapeDtypeStruct((N_TOK, D_MODEL), F32),
        compiler_params=_cparams("arbitrary"),
        name="moe_combine",
    )(perm, nblk, y_sorted, x, mod)


def _moe(x, mod, g, router_w, router_b, w_gate, w_up, w_down, layer):
    h, info, wtok, counts = _moe_route(x, mod, g, router_w, router_b, layer)
    cls = info[:, 0, :].reshape(N_TOK)
    rank = info[:, 1, :].reshape(N_TOK)
    cnt = counts[:N_CLASSES, 0]
    nb = (cnt + MOE_BM - 1) // MOE_BM
    ends = jnp.cumsum(nb)
    starts = ends - nb
    pos = ((starts * MOE_BM)[cls] + rank).astype(jnp.int32)
    blk = jnp.arange(MOE_NBLK, dtype=jnp.int32)
    blk_cls = jnp.minimum(jnp.sum((blk[:, None] >= ends[None, :]).astype(jnp.int32), axis=1), N_CLASSES - 1)
    slot_a = jnp.asarray([s[0] for s in PAIR_SLOTS], jnp.int32)
    slot_b = jnp.asarray([s[1] for s in PAIR_SLOTS], jnp.int32)
    grp = blk_cls // N_PAIRS
    blk_ea = (grp * EXPERTS_PER_GROUP + slot_a[blk_cls % N_PAIRS]).astype(jnp.int32)
    blk_eb = (grp * EXPERTS_PER_GROUP + slot_b[blk_cls % N_PAIRS]).astype(jnp.int32)
    swap = (blk_ea > blk_eb).astype(jnp.int32)
    nblk = ends[-1:].astype(jnp.int32)
    hs, ws, perm = _moe_sort(h, wtok, pos, nblk, swap)
    y_sorted = _moe_experts(hs, ws, blk_ea, blk_eb, nblk, w_gate, w_up, w_down, layer)
    return _moe_combine(x, y_sorted, perm, nblk, mod, layer)


def _final_norm_kernel(x_ref, g_ref, o_ref):
    o_ref[...] = _rms(x_ref[...]) * g_ref[...]


def _final_norm(x, g, row0, rows):
    base = row0 // ROW_BLK
    return pl.pallas_call(
        _final_norm_kernel,
        grid=(rows // ROW_BLK,),
        in_specs=[pl.BlockSpec((ROW_BLK, D_MODEL), lambda i: (base + i, 0)), pl.BlockSpec((1, D_MODEL), lambda i: (0, 0))],
        out_specs=pl.BlockSpec((ROW_BLK, D_MODEL), lambda i: (i, 0)),
        out_shape=jax.ShapeDtypeStruct((rows, D_MODEL), F32),
        compiler_params=_cparams("arbitrary"),
        name="final_norm",
    )(x, g.reshape(1, D_MODEL))


def kernel(x_prompt, x_sample, cache_attn_k, cache_attn_v, state_ssm_re, state_ssm_im, cache_mla_ckv, cache_mla_kpe,
           state_hgrn, c, c_ctx, w_mod, b_mod, g_mix, g_ffn, g_final, router_w, router_b, moe_w_gate, moe_w_up,
           moe_w_down, ab_w_in, ab_sink, s5_lam_re, s5_lam_im, s5_log_dt, s5_b_re, s5_b_im, s5_c_re, s5_c_im, s5_d,
           s5_w_glu, ab_w_out, cd_w_in, mla_g_q, mla_w_q_up, mla_g_kv, mla_w_kv_up, hg_lower_bounds, hg_g_o, cd_w_out):
    x = jnp.concatenate([x_prompt.reshape(N_CTX_TOK, D_MODEL), x_sample.reshape(N_LAT_TOK, D_MODEL)], axis=0)
    cond = jnp.zeros((MOD_ROWS, D_MODEL), F32).at[0].set(c_ctx).at[1:1 + DEC_BATCH].set(c)
    mod = _modulation(cond, w_mod, b_mod)
    keep = ([], [], [], [], [], [], [])
    for l in range(DEPTH):
        j = l // 2
        if l % 2 == 0:
            q, k, v, u = _ab_in(x, mod, g_mix[l], ab_w_in[j], l)
            o_a = _attention_a(q, k, v, cache_attn_k[:, j].reshape(DEC_BATCH, PAST_LEN, A_KV_WIDTH),
                               cache_attn_v[:, j].reshape(DEC_BATCH, PAST_LEN, A_KV_WIDTH), ab_sink[j])
            y, fin_re, fin_im = _s5_scan(u, state_ssm_re[:, j], state_ssm_im[:, j], s5_lam_re[j], s5_lam_im[j],
                                         s5_log_dt[j], s5_b_re[j], s5_b_im[j], s5_c_re[j], s5_c_im[j])
            x = _ab_out(x, o_a, y, u, s5_d[j], s5_w_glu[j], ab_w_out[j], mod, l)
            keep[0].append(k[:N_CTX_TOK].reshape(BATCH, SEQ, A_KV_HEADS, HEAD_DIM))
            keep[1].append(v[:N_CTX_TOK].reshape(BATCH, SEQ, A_KV_HEADS, HEAD_DIM))
            keep[2].append(fin_re)
            keep[3].append(fin_im)
        else:
            qn, qr, ckv, kpe, kn, vm, hq, hff, hfb, hi, hg = _cd_in(
                x, mod, g_mix[l], cd_w_in[j], mla_g_q[j], mla_w_q_up[j], mla_g_kv[j], mla_w_kv_up[j], l)
            kcn, vc = _mla_cache_kv(cache_mla_ckv[:, j].reshape(DEC_BATCH * PAST_LEN, MLA_KV_RANK), mla_w_kv_up[j])
            o_c = _attention_mla(qn, qr, kn, kpe, vm, kcn, cache_mla_kpe[:, j], vc)
            s0_ctx = jnp.zeros((BATCH, 2, 2, HG_VAL, HG_HALF), F32)
            of_c, ob_c, s_fin = _hgrn_scan(hq, hff, hfb, hi, hg_lower_bounds, s0_ctx, l, 0, BATCH, SEQ)
            of_l, ob_l, _ = _hgrn_scan(hq, hff, hfb, hi, hg_lower_bounds, _hg_state_to_blocks(state_hgrn[:, j]), l,
                                       N_CTX_TOK, DEC_BATCH, DEC_SEQ)
            x = _cd_out(x, o_c, (of_c, of_l), (ob_c, ob_l), hg, hg_g_o[j], cd_w_out[j], mod, l)
            keep[4].append(ckv[:N_CTX_TOK].reshape(BATCH, SEQ, MLA_KV_RANK))
            keep[5].append(kpe[:N_CTX_TOK, :MLA_ROPE].reshape(BATCH, SEQ, MLA_ROPE))
            keep[6].append(_hg_state_from_blocks(s_fin))
        x = _moe(x, mod, g_ffn[l], router_w, router_b, moe_w_gate, moe_w_up, moe_w_down, l)
    y_ctx = _final_norm(x, g_final, 0, N_CTX_TOK)
    y_lat = _final_norm(x, g_final, N_CTX_TOK, N_LAT_TOK)
    return (y_ctx.reshape(BATCH, SEQ, D_MODEL), y_lat.reshape(DEC_BATCH, DEC_SEQ, D_MODEL),
            jnp.stack(keep[0], 1), jnp.stack(keep[1], 1), jnp.stack(keep[2], 1), jnp.stack(keep[3], 1),
            jnp.stack(keep[4], 1), jnp.stack(keep[5], 1), jnp.stack(keep[6], 1))
```

```python
import functools
import math

import numpy as np
import jax
import jax.numpy as jnp
from jax import lax
from jax.experimental import pallas as pl
from jax.experimental.pallas import tpu as pltpu
from jax.experimental.pallas import tpu_sc as plsc

F32 = jnp.float32
BF16 = jnp.bfloat16

D_MODEL = 1024
BATCH = 16
SEQ = 256
DEPTH = 2
DEC_BATCH = 4
DEC_SEQ = 1024
PAST_LEN = 512
GRID_W = 64
N_MOD = 6
EPS = 1e-6
NEG_INF = -1e30
ROPE_BASE = 10000.0
HEAD_DIM = 64
A_HEADS = 8
A_KV_HEADS = 2
A_WINDOW = 128
A_WIDTH = A_HEADS * HEAD_DIM
A_KV_WIDTH = A_KV_HEADS * HEAD_DIM
S5_WIDTH = D_MODEL // 2
S5_GROUP = 16
S5_GROUPS = S5_WIDTH // S5_GROUP
S5_STATE = 64
MLA_HEADS = 8
MLA_Q_RANK = D_MODEL // 4
MLA_KV_RANK = D_MODEL // 8
MLA_NOPE = 64
MLA_ROPE = 32
MLA_V = 64
MLA_WIDTH = MLA_HEADS * MLA_V
HG_HEADS = 8
HG_KEY = 64
HG_VAL = 64
HG_KD = HG_HEADS * HG_KEY
HG_WIDTH = HG_HEADS * HG_VAL
HG_CHUNK = 16
N_EXPERTS = 16
N_GROUPS = 4
EXPERTS_PER_GROUP = N_EXPERTS // N_GROUPS
D_FF = D_MODEL // 2

N_CTX_TOK = BATCH * SEQ
N_LAT_TOK = DEC_BATCH * DEC_SEQ
N_TOK = N_CTX_TOK + N_LAT_TOK
ROW_BLK = 256
N_ROW_BLK = N_TOK // ROW_BLK
N_CTX_BLK = N_CTX_TOK // ROW_BLK
LAT_BLK_PER_SEQ = DEC_SEQ // ROW_BLK
MOD_ROWS = 8
VMEM_LIMIT = 56 * 1024 * 1024


def _cparams(*sem):
    return pltpu.CompilerParams(dimension_semantics=sem, vmem_limit_bytes=VMEM_LIMIT)


def _mod_group(i):
    return jnp.where(i < N_CTX_BLK, 0, 1 + (i - N_CTX_BLK) // LAT_BLK_PER_SEQ)


def _mod_spec(layer, which):
    return pl.BlockSpec((1, 1, D_MODEL), lambda i: ((layer * MOD_ROWS + _mod_group(i)) * N_MOD + which, 0, 0))


def _rope_blk(i):
    return jnp.where(i < N_CTX_BLK, 0, 1 + (i - N_CTX_BLK) % LAT_BLK_PER_SEQ)


def _rope_tables(rot_dim):
    n_freq = rot_dim // 4
    t = np.arange(DEC_SEQ)
    rows = (t // GRID_W).astype(np.float32)
    cols = (t % GRID_W).astype(np.float32)
    inv = (np.float32(ROPE_BASE) ** (-np.arange(n_freq, dtype=np.float32) / np.float32(n_freq))).astype(np.float32)
    ang_r = rows[:, None] * inv[None, :]
    ang_c = cols[:, None] * inv[None, :]
    ang = np.concatenate([ang_r, ang_r, ang_c, ang_c], axis=-1).astype(np.float32)
    reps = 128 // rot_dim
    cos = np.tile(np.cos(ang), (1, reps)).astype(np.float32)
    sin = np.tile(np.sin(ang), (1, reps)).astype(np.float32)
    lane = np.arange(128)
    first = (lane % (2 * n_freq)) < n_freq
    sin_a = np.where(first[None, :], -sin, 0.0).astype(np.float32)
    sin_b = np.where(first[None, :], 0.0, sin).astype(np.float32)
    ident = np.zeros((ROW_BLK, 128), np.float32)
    cos = np.concatenate([ident + 1.0, cos], axis=0)
    sin_a = np.concatenate([ident, sin_a], axis=0)
    sin_b = np.concatenate([ident, sin_b], axis=0)
    return jnp.asarray(cos), jnp.asarray(sin_a), jnp.asarray(sin_b)


def _apply_rope(x, cos, sin_a, sin_b, quarter):
    outs = []
    for j in range(x.shape[1] // 128):
        xt = x[:, 128 * j:128 * (j + 1)]
        up = pltpu.roll(xt, 128 - quarter, axis=1)
        dn = pltpu.roll(xt, quarter, axis=1)
        outs.append(xt * cos + up * sin_a + dn * sin_b)
    return outs[0] if len(outs) == 1 else jnp.concatenate(outs, axis=1)


def _rms(x):
    return x * lax.rsqrt(jnp.mean(x * x, axis=-1, keepdims=True) + EPS)


def _norm_mod(x, g, sc, sh):
    return _rms(x) * g * (1.0 + sc) + sh


def _dot(a, b):
    return jnp.dot(a.astype(BF16), b.astype(BF16), preferred_element_type=F32)


def _dot_nt(a, b):
    return lax.dot_general(a.astype(BF16), b.astype(BF16), (((1,), (1,)), ((), ())), preferred_element_type=F32)


def _mod_kernel(cond_ref, w_ref, b_ref, o_ref):
    c = cond_ref[...]
    s = c * jax.nn.sigmoid(c)
    o_ref[0] = _dot(s, w_ref[0]) + b_ref[0]


def _modulation(cond, w_mod, b_mod):
    nb = 1024
    out = pl.pallas_call(
        _mod_kernel,
        grid=(DEPTH, N_MOD * D_MODEL // nb),
        in_specs=[pl.BlockSpec((MOD_ROWS, D_MODEL), lambda l, n: (0, 0)),
                  pl.BlockSpec((1, D_MODEL, nb), lambda l, n: (l, 0, n)),
                  pl.BlockSpec((1, 1, nb), lambda l, n: (l, 0, n))],
        out_specs=pl.BlockSpec((1, MOD_ROWS, nb), lambda l, n: (l, 0, n)),
        out_shape=jax.ShapeDtypeStruct((DEPTH, MOD_ROWS, N_MOD * D_MODEL), F32),
        compiler_params=_cparams("arbitrary", "arbitrary"),
        name="modulation",
    )(cond, w_mod, b_mod.reshape(DEPTH, 1, N_MOD * D_MODEL))
    return out.reshape(DEPTH * MOD_ROWS * N_MOD, 1, D_MODEL)


def _ab_in_kernel(x_ref, g_ref, sc_ref, sh_ref, w_ref, cos_ref, sa_ref, sb_ref,
                  q_ref, k_ref, v_ref, u_ref, wq_s, wk_s, wv_s, wu_s):
    @pl.when(pl.program_id(0) == 0)
    def _():
        wq_s[...] = w_ref[:, 0:A_WIDTH].astype(BF16)
        wk_s[...] = w_ref[:, A_WIDTH:A_WIDTH + A_KV_WIDTH].astype(BF16)
        wv_s[...] = w_ref[:, A_WIDTH + A_KV_WIDTH:A_WIDTH + 2 * A_KV_WIDTH].astype(BF16)
        wu_s[...] = w_ref[:, A_WIDTH + 2 * A_KV_WIDTH:].astype(BF16)

    h = _norm_mod(x_ref[...], g_ref[...], sc_ref[0], sh_ref[0]).astype(BF16)
    cos, sa, sb = cos_ref[...], sa_ref[...], sb_ref[...]
    q = jnp.dot(h, wq_s[...], preferred_element_type=F32)
    q_ref[...] = _apply_rope(q, cos, sa, sb, HEAD_DIM // 4).astype(q_ref.dtype)
    k = jnp.dot(h, wk_s[...], preferred_element_type=F32)
    k_ref[...] = _apply_rope(k, cos, sa, sb, HEAD_DIM // 4)
    v_ref[...] = jnp.dot(h, wv_s[...], preferred_element_type=F32)
    u_ref[...] = jnp.dot(h, wu_s[...], preferred_element_type=F32)


def _ab_in(x, mod, g, w, layer):
    cos, sa, sb = _rope_tables(HEAD_DIM)
    d_in = w.shape[1]
    row = lambda n: pl.BlockSpec((ROW_BLK, n), lambda i: (i, 0))
    rope = pl.BlockSpec((ROW_BLK, 128), lambda i: (_rope_blk(i), 0))
    return pl.pallas_call(
        _ab_in_kernel,
        grid=(N_ROW_BLK,),
        in_specs=[row(D_MODEL), pl.BlockSpec((1, D_MODEL), lambda i: (0, 0)),
                  _mod_spec(layer, 1), _mod_spec(layer, 0),
                  pl.BlockSpec((D_MODEL, d_in), lambda i: (0, 0)), rope, rope, rope],
        out_specs=[row(A_WIDTH), row(A_KV_WIDTH), row(A_KV_WIDTH), row(S5_WIDTH)],
        out_shape=[jax.ShapeDtypeStruct((N_TOK, A_WIDTH), BF16), jax.ShapeDtypeStruct((N_TOK, A_KV_WIDTH), F32),
                   jax.ShapeDtypeStruct((N_TOK, A_KV_WIDTH), F32), jax.ShapeDtypeStruct((N_TOK, S5_WIDTH), F32)],
        scratch_shapes=[pltpu.VMEM((D_MODEL, A_WIDTH), BF16), pltpu.VMEM((D_MODEL, A_KV_WIDTH), BF16),
                        pltpu.VMEM((D_MODEL, A_KV_WIDTH), BF16), pltpu.VMEM((D_MODEL, S5_WIDTH), BF16)],
        compiler_params=_cparams("arbitrary"),
        name="ab_in",
    )(x, g.reshape(1, D_MODEL), mod, mod, w, cos, sa, sb)


def _softmax_pv(s_list, v_list, sink):
    m = functools.reduce(jnp.maximum, [jnp.max(s, axis=-1, keepdims=True) for s in s_list])
    if sink is not None:
        m = jnp.maximum(m, sink)
    ps = [jnp.exp(s - m) for s in s_list]
    l = functools.reduce(jnp.add, [jnp.sum(p, axis=-1, keepdims=True) for p in ps])
    if sink is not None:
        l = l + jnp.exp(sink - m)
    o = functools.reduce(jnp.add, [_dot(p, v) for p, v in zip(ps, v_list)])
    return o / l


def _gqa_with_sink(sink_ref, q, k_of, v_of, bias, o_ref):
    g = A_HEADS // A_KV_HEADS
    r = q.shape[0]
    outs = [None] * A_HEADS
    for kh in range(A_KV_HEADS):
        heads = range(g * kh, g * (kh + 1))
        qg = jnp.concatenate([q[:, HEAD_DIM * h:HEAD_DIM * (h + 1)] for h in heads], axis=0)
        sink = jnp.concatenate([jnp.full((r, 1), sink_ref[h], F32) for h in heads], axis=0)
        s = _dot_nt(qg, k_of(kh))
        if bias is not None:
            s = s + jnp.concatenate([bias] * g, axis=0)
        m = jnp.maximum(jnp.max(s, axis=-1, keepdims=True), sink)
        p = jnp.exp(s - m)
        l = jnp.sum(p, axis=-1, keepdims=True) + jnp.exp(sink - m)
        o = _dot(p, v_of(kh)) / l
        for n, h in enumerate(heads):
            outs[h] = o[r * n:r * (n + 1)]
    o_ref[...] = jnp.concatenate(outs, axis=1).astype(o_ref.dtype)


def _attn_ctx_kernel(sink_ref, q_ref, k_ref, v_ref, o_ref):
    scale = HEAD_DIM ** -0.5
    g = A_HEADS // A_KV_HEADS
    outs = []
    for h in range(A_HEADS):
        kh = h // g
        q = q_ref[:, HEAD_DIM * h:HEAD_DIM * (h + 1)]
        k = k_ref[:, HEAD_DIM * kh:HEAD_DIM * (kh + 1)]
        v = v_ref[:, HEAD_DIM * kh:HEAD_DIM * (kh + 1)]
        s = _dot_nt(q, k) * scale
        outs.append(_softmax_pv([s], [v], sink_ref[h]))
    o_ref[...] = jnp.concatenate(outs, axis=1).astype(o_ref.dtype)


def _attn_lat_kernel(sink_ref, q_ref, kp_ref, kc_ref, kn_ref, vp_ref, vc_ref, vn_ref, kx_ref, vx_ref, o_ref):
    n = pl.program_id(1)
    nb = DEC_SEQ // A_WINDOW
    i = lax.broadcasted_iota(jnp.int32, (A_WINDOW, A_WINDOW), 0)
    j = lax.broadcasted_iota(jnp.int32, (A_WINDOW, A_WINDOW), 1)
    zero = jnp.zeros((A_WINDOW, A_WINDOW), F32)
    bias = jnp.concatenate([jnp.where((j >= i) & (n > 0), 0.0, NEG_INF), zero,
                            jnp.where((j <= i) & (n < nb - 1), 0.0, NEG_INF),
                            jnp.zeros((A_WINDOW, PAST_LEN), F32)], axis=1)

    def rows(p_ref, c_ref, n_ref, x_ref):
        def of(kh):
            sl = slice(HEAD_DIM * kh, HEAD_DIM * (kh + 1))
            return jnp.concatenate([p_ref[:, sl], c_ref[:, sl], n_ref[:, sl], x_ref[0, :, sl]], axis=0)
        return of

    _gqa_with_sink(sink_ref, q_ref[...] * HEAD_DIM ** -0.5, rows(kp_ref, kc_ref, kn_ref, kx_ref),
                   rows(vp_ref, vc_ref, vn_ref, vx_ref), bias, o_ref)


def _attention_a(q, k, v, cache_k, cache_v, sink):
    smem = pl.BlockSpec(memory_space=pltpu.SMEM)
    o = pl.pallas_call(
        _attn_ctx_kernel,
        grid=(BATCH,),
        in_specs=[smem, pl.BlockSpec((SEQ, A_WIDTH), lambda b: (b, 0)),
                  pl.BlockSpec((SEQ, A_KV_WIDTH), lambda b: (b, 0)), pl.BlockSpec((SEQ, A_KV_WIDTH), lambda b: (b, 0))],
        out_specs=pl.BlockSpec((SEQ, A_WIDTH), lambda b: (b, 0)),
        out_shape=jax.ShapeDtypeStruct((N_CTX_TOK, A_WIDTH), BF16),
        compiler_params=_cparams("arbitrary"),
        name="attn_a_ctx",
    )(sink, q, k, v)
    nb = DEC_SEQ // A_WINDOW
    base = N_CTX_TOK // A_WINDOW
    cur = lambda b, n: (base + b * nb + n, 0)
    prev = lambda b, n: (base + b * nb + jnp.maximum(n - 1, 0), 0)
    nxt = lambda b, n: (base + b * nb + jnp.minimum(n + 1, nb - 1), 0)
    kv = lambda f: pl.BlockSpec((A_WINDOW, A_KV_WIDTH), f)
    cache = pl.BlockSpec((1, PAST_LEN, A_KV_WIDTH), lambda b, n: (b, 0, 0))
    o_lat = pl.pallas_call(
        _attn_lat_kernel,
        grid=(DEC_BATCH, nb),
        in_specs=[smem, pl.BlockSpec((A_WINDOW, A_WIDTH), cur), kv(prev), kv(cur), kv(nxt), kv(prev), kv(cur), kv(nxt),
                  cache, cache],
        out_specs=pl.BlockSpec((A_WINDOW, A_WIDTH), lambda b, n: (b * nb + n, 0)),
        out_shape=jax.ShapeDtypeStruct((N_LAT_TOK, A_WIDTH), BF16),
        compiler_params=_cparams("arbitrary", "arbitrary"),
        name="attn_a_lat",
    )(sink, q, k, k, k, v, v, v, cache_k, cache_v)
    return o, o_lat


S5_CHUNK = 16
S5_OCT = 128 // S5_GROUP
S5_NOCT = S5_GROUPS // S5_OCT
S5_K = S5_CHUNK * 128
S5_PART = S5_OCT * S5_STATE
S5_SW = 4 * S5_PART
S5_ROWS_CTX = BATCH * SEQ // S5_CHUNK
S5_ROWS_LAT = DEC_BATCH * DEC_SEQ // S5_CHUNK
S5_ROWS = S5_ROWS_CTX + S5_ROWS_LAT
S5_NB = 4
HI = lax.Precision.HIGHEST


def _s5_disc_kernel(lr_ref, li_ref, ldt_ref, ar_ref, ai_ref, zr_ref, zi_ref):
    lr, li = lr_ref[...], li_ref[...]
    dt = jnp.exp(ldt_ref[...])
    mag = jnp.exp(lr * dt)
    ar, ai = mag * jnp.cos(li * dt), mag * jnp.sin(li * dt)
    den = lr * lr + li * li
    ar_ref[...] = ar
    ai_ref[...] = ai
    zr_ref[...] = ((ar - 1.0) * lr + ai * li) / den
    zi_ref[...] = (ai * lr - (ar - 1.0) * li) / den


def _cmul(xr, xi, yr, yi):
    return xr * yr - xi * yi, xr * yi + xi * yr


def _dot_nt_hi(a, b):
    return lax.dot_general(a, b, (((1,), (1,)), ((), ())), precision=HI, preferred_element_type=F32)


def _s5_prep_kernel(ar_ref, ai_ref, zr_ref, zi_ref, btr_ref, bti_ref, ctr_ref, cti_ref,
                    m_ref, win_ref, wout_ref, a16_ref, pw_s, w_s, k_s):
    t = pl.program_id(1)
    npw = S5_CHUNK + 1
    blk = lambda j: pl.ds(pl.multiple_of(j * 128, 128), 128)

    @pl.when(t == 0)
    def _():
        kd = []
        for d in range(2):
            ar, ai = ar_ref[0, d], ai_ref[0, d]
            pr, pi = jnp.ones_like(ar), jnp.zeros_like(ar)
            bbr, bbi = _cmul(zr_ref[0, d], zi_ref[0, d], btr_ref[0, d], bti_ref[0, d])
            for j in range(npw):
                pw_s[d, 0, j:j + 1, :] = pr
                pw_s[d, 1, j:j + 1, :] = pi
                if j < S5_CHUNK:
                    wr, wi = _cmul(pr, pi, bbr, bbi)
                    w_s[d, 0, 128 * j:128 * (j + 1), :] = wr.astype(BF16)
                    w_s[d, 1, 128 * j:128 * (j + 1), :] = wi.astype(BF16)
                pr, pi = _cmul(pr, pi, ar, ai)
            kd.append(_dot_nt(w_s[d, 0], ctr_ref[0, d]) - _dot_nt(w_s[d, 1], cti_ref[0, d]))
        for jj in range(2 * S5_CHUNK - 1):
            j = jj - (S5_CHUNK - 1)
            if j > 0:
                k = kd[0][128 * j:128 * (j + 1)]
            elif j < 0:
                k = kd[1][128 * -j:128 * (1 - j)]
            else:
                k = kd[0][0:128] + kd[1][0:128]
            k_s[128 * jj:128 * (jj + 1), :] = k
        a16_ref[0] = jnp.concatenate([pw_s[0, 0, S5_CHUNK:npw, :], pw_s[1, 0, S5_CHUNK:npw, :],
                                      pw_s[0, 1, S5_CHUNK:npw, :], pw_s[1, 1, S5_CHUNK:npw, :]], axis=1)

    for tp in range(S5_CHUNK):
        m_ref[0, :, 128 * tp:128 * (tp + 1)] = k_s[blk(S5_CHUNK - 1 + tp - t), :].astype(BF16)

    def power(d, e):
        return pw_s[d, 0, pl.ds(e, 1), :], pw_s[d, 1, pl.ds(e, 1), :]

    for d in range(2):
        j = (S5_CHUNK - 1 - t) if d == 0 else t
        win_ref[0, :, S5_PART * d:S5_PART * (d + 1)] = w_s[d, 0, blk(j), :]
        win_ref[0, :, S5_PART * (2 + d):S5_PART * (3 + d)] = w_s[d, 1, blk(j), :]
        er, ei = _cmul(*power(d, (t + 1) if d == 0 else (S5_CHUNK - t)), ctr_ref[0, d], cti_ref[0, d])
        wout_ref[0, :, S5_PART * d:S5_PART * (d + 1)] = er.astype(BF16)
        wout_ref[0, :, S5_PART * (2 + d):S5_PART * (3 + d)] = (-ei).astype(BF16)


def _s5_main_kernel(u_ref, win_ref, m_ref, wout_ref, a16_ref, h0_ref, y_ref, hfin_ref, uo_s, x_s, hs_s):
    s = pl.program_id(1)
    nq = S5_PART // 128

    @pl.when(s == 0)
    def _():
        for t in range(S5_CHUNK):
            uo_s[:, 128 * t:128 * (t + 1)] = u_ref[pl.ds(t, S5_ROWS, stride=S5_CHUNK), :].astype(BF16)

    @pl.when(s < S5_NB)
    def _():
        x = jnp.dot(uo_s[...], win_ref[0], preferred_element_type=F32)
        for q in range(nq):
            x_s[s * nq + q] = x[:, 128 * q:128 * (q + 1)]

    @pl.when(s == S5_NB - 1)
    def _():
        def run(row0, nb, nc, h):
            for c in range(nc):
                for d in range(2):
                    cc = c if d == 0 else nc - 1 - c
                    rows = pl.ds(row0 + cc, nb, stride=nc)
                    for q in range(nq):
                        kr, ki = d * nq + q, (2 + d) * nq + q
                        hr, hi = h[d][0][q], h[d][1][q]
                        hs_s[kr, rows, :] = hr
                        hs_s[ki, rows, :] = hi
                        ar = a16_ref[0, :, 128 * kr:128 * (kr + 1)]
                        ai = a16_ref[0, :, 128 * ki:128 * (ki + 1)]
                        h[d][0][q] = ar * hr - ai * hi + x_s[kr, rows, :]
                        h[d][1][q] = ar * hi + ai * hr + x_s[ki, rows, :]
            return h

        zero = jnp.zeros((BATCH, 128), F32)
        fin = run(0, BATCH, SEQ // S5_CHUNK, [[[zero] * nq, [zero] * nq] for _ in range(2)])
        for d in range(2):
            for ri in range(2):
                for q in range(nq):
                    k = (2 * ri + d) * nq + q
                    hfin_ref[0, :, 128 * k:128 * (k + 1)] = fin[d][ri][q]
        h0 = [[[h0_ref[0, :, 128 * ((2 * ri + d) * nq + q):128 * ((2 * ri + d) * nq + q + 1)] for q in range(nq)]
               for ri in range(2)] for d in range(2)]
        run(S5_ROWS_CTX, DEC_BATCH, DEC_SEQ // S5_CHUNK, h0)

    @pl.when(s >= S5_NB)
    def _():
        hs = jnp.concatenate([hs_s[k] for k in range(4 * nq)], axis=1).astype(BF16)
        y = (jnp.dot(uo_s[...], m_ref[0], preferred_element_type=F32)
             + lax.dot_general(hs, wout_ref[0], (((1,), (1,)), ((), ())), preferred_element_type=F32))
        for q in range(nq):
            t = (s - S5_NB) * nq + q
            y_ref[pl.ds(t, S5_ROWS, stride=S5_CHUNK), :] = y[:, 128 * q:128 * (q + 1)]


def _s5_octets(t, lanes):
    return t.reshape(2, S5_NOCT, 1, S5_OCT * lanes).transpose(1, 0, 2, 3)


def _s5_blockdiag(t):
    a, n = t.shape[2], t.shape[3]
    t = t.reshape(2, S5_NOCT, S5_OCT, a, n)
    bd = jnp.einsum('dogan,gh->dogahn', t, jnp.eye(S5_OCT, dtype=t.dtype))
    return bd.reshape(2, S5_NOCT, S5_OCT * a, S5_OCT * n).transpose(1, 0, 2, 3)


def _s5_state_to_lanes(h_re, h_im):
    b = h_re.shape[0]
    parts = jnp.stack([h_re[:, 0], h_re[:, 1], h_im[:, 0], h_im[:, 1]], axis=1)
    parts = parts.reshape(b, 4, S5_NOCT, S5_PART).transpose(2, 0, 1, 3)
    return parts.reshape(S5_NOCT, b, S5_SW)


def _s5_state_from_lanes(h):
    b = h.shape[1]
    parts = h.reshape(S5_NOCT, b, 4, S5_OCT, S5_STATE).transpose(1, 2, 0, 3, 4).reshape(b, 4, S5_GROUPS, S5_STATE)
    return parts[:, 0:2], parts[:, 2:4]


def _s5_scan(u, h0_re, h0_im, lam_re, lam_im, log_dt, b_re, b_im, c_re, c_im):
    ng, n = S5_GROUPS, S5_STATE
    rows = 2 * ng
    disc = pl.pallas_call(
        _s5_disc_kernel,
        out_shape=[jax.ShapeDtypeStruct((rows, n), F32)] * 4,
        name="s5_disc",
    )(lam_re.reshape(rows, n), lam_im.reshape(rows, n), log_dt.reshape(rows, 1))
    ar, ai, zr, zi = [_s5_octets(t.reshape(2, ng, n), n) for t in disc]
    bt = lambda t: _s5_blockdiag(t.transpose(0, 1, 3, 2))
    vec = pl.BlockSpec((1, 2, 1, S5_PART), lambda o, t: (o, 0, 0, 0))
    mat = pl.BlockSpec((1, 2, 128, S5_PART), lambda o, t: (o, 0, 0, 0))
    rowblk = lambda w: pl.BlockSpec((1, 128, w), lambda o, t: (o, t, 0))
    m, win, wout, a16 = pl.pallas_call(
        _s5_prep_kernel,
        grid=(S5_NOCT, S5_CHUNK),
        in_specs=[vec, vec, vec, vec, mat, mat, mat, mat],
        out_specs=[rowblk(S5_K), rowblk(S5_SW), rowblk(S5_SW), pl.BlockSpec((1, 1, S5_SW), lambda o, t: (o, 0, 0))],
        out_shape=[jax.ShapeDtypeStruct((S5_NOCT, S5_K, S5_K), BF16), jax.ShapeDtypeStruct((S5_NOCT, S5_K, S5_SW), BF16),
                   jax.ShapeDtypeStruct((S5_NOCT, S5_K, S5_SW), BF16), jax.ShapeDtypeStruct((S5_NOCT, 1, S5_SW), F32)],
        scratch_shapes=[pltpu.VMEM((2, 2, 24, S5_PART), F32), pltpu.VMEM((2, 2, S5_K, S5_PART), BF16),
                        pltpu.VMEM(((2 * S5_CHUNK - 1) * 128, 128), F32)],
        compiler_params=_cparams("arbitrary", "arbitrary"),
        name="s5_prep",
    )(ar, ai, zr, zi, bt(b_re), bt(b_im), _s5_blockdiag(c_re), _s5_blockdiag(c_im))
    nb = S5_NB
    y, hfin = pl.pallas_call(
        _s5_main_kernel,
        grid=(S5_NOCT, 2 * nb),
        in_specs=[pl.BlockSpec((N_TOK, 128), lambda o, s: (0, o)),
                  pl.BlockSpec((1, S5_K, S5_PART), lambda o, s: (o, 0, jnp.minimum(s, nb - 1))),
                  pl.BlockSpec((1, S5_K, S5_PART), lambda o, s: (o, 0, jnp.maximum(s - nb, 0))),
                  pl.BlockSpec((1, S5_PART, S5_SW), lambda o, s: (o, jnp.maximum(s - nb, 0), 0)),
                  pl.BlockSpec((1, 1, S5_SW), lambda o, s: (o, 0, 0)),
                  pl.BlockSpec((1, DEC_BATCH, S5_SW), lambda o, s: (o, 0, 0))],
        out_specs=[pl.BlockSpec((N_TOK, 128), lambda o, s: (0, o)),
                   pl.BlockSpec((1, BATCH, S5_SW), lambda o, s: (o, 0, 0))],
        out_shape=[jax.ShapeDtypeStruct((N_TOK, S5_WIDTH), F32), jax.ShapeDtypeStruct((S5_NOCT, BATCH, S5_SW), F32)],
        scratch_shapes=[pltpu.VMEM((S5_ROWS, S5_K), BF16), pltpu.VMEM((S5_SW // 128, S5_ROWS, 128), F32),
                        pltpu.VMEM((S5_SW // 128, S5_ROWS, 128), F32)],
        compiler_params=_cparams("arbitrary", "arbitrary"),
        name="s5_main",
    )(u, win, m, wout, a16, _s5_state_to_lanes(h0_re, h0_im))
    fin_re, fin_im = _s5_state_from_lanes(hfin)
    return y, fin_re, fin_im


def _ctx_or_lat(ctx_ref, lat_ref):
    return jnp.where(pl.program_id(0) < N_CTX_BLK, ctx_ref[...], lat_ref[...])


def _split_specs(width):
    return [pl.BlockSpec((ROW_BLK, width), lambda i: (jnp.minimum(i, N_CTX_BLK - 1), 0)),
            pl.BlockSpec((ROW_BLK, width), lambda i: (jnp.maximum(i - N_CTX_BLK, 0), 0))]


def _ab_out_kernel(x_ref, oac_ref, oal_ref, y_ref, u_ref, d_ref, wglu_ref, wout_ref, gt_ref, o_ref, wglu_s, wa_s, wb_s):
    @pl.when(pl.program_id(0) == 0)
    def _():
        wglu_s[...] = wglu_ref[...].astype(BF16)
        wa_s[...] = wout_ref[0:A_WIDTH, :].astype(BF16)
        wb_s[...] = wout_ref[A_WIDTH:, :].astype(BF16)

    g = jax.nn.gelu(y_ref[...] + d_ref[...] * u_ref[...])
    ob = g * jax.nn.sigmoid(jnp.dot(g.astype(BF16), wglu_s[...], preferred_element_type=F32))
    out = (jnp.dot(_ctx_or_lat(oac_ref, oal_ref).astype(BF16), wa_s[...], preferred_element_type=F32)
           + jnp.dot(ob.astype(BF16), wb_s[...], preferred_element_type=F32))
    o_ref[...] = x_ref[...] + gt_ref[0] * out


def _ab_out(x, oa, y, u, d_skip, w_glu, w_out, mod, layer):
    row = lambda n: pl.BlockSpec((ROW_BLK, n), lambda i: (i, 0))
    full = lambda a, b: pl.BlockSpec((a, b), lambda i: (0, 0))
    return pl.pallas_call(
        _ab_out_kernel,
        grid=(N_ROW_BLK,),
        in_specs=[row(D_MODEL)] + _split_specs(A_WIDTH) + [row(S5_WIDTH), row(S5_WIDTH), full(1, S5_WIDTH),
                  full(S5_WIDTH, S5_WIDTH), full(A_WIDTH + S5_WIDTH, D_MODEL), _mod_spec(layer, 2)],
        out_specs=row(D_MODEL),
        out_shape=jax.ShapeDtypeStruct((N_TOK, D_MODEL), F32),
        scratch_shapes=[pltpu.VMEM((S5_WIDTH, S5_WIDTH), BF16), pltpu.VMEM((A_WIDTH, D_MODEL), BF16),
                        pltpu.VMEM((S5_WIDTH, D_MODEL), BF16)],
        compiler_params=_cparams("arbitrary"),
        name="ab_out",
    )(x, oa[0], oa[1], y, u, d_skip.reshape(1, S5_WIDTH), w_glu, w_out, mod)


CD_HG0 = MLA_Q_RANK + MLA_KV_RANK + MLA_ROPE
KPE_LANES = 128


def _cd_in_kernel(x_ref, g_ref, sc_ref, sh_ref, w_ref, gq_ref, wqu_ref, gkv_ref, wkvu_ref, cos_ref, sa_ref, sb_ref,
                  qn_ref, qr_ref, ckv_ref, kpe_ref, kn_ref, vm_ref, hq_ref, hff_ref, hfb_ref, hi_ref, hg_ref,
                  wcq_s, wckv_s, wkpe_s, whg_s, wqu_s, wkvu_s):
    @pl.when(pl.program_id(0) == 0)
    def _():
        wcq_s[...] = w_ref[:, 0:MLA_Q_RANK].astype(BF16)
        wckv_s[...] = w_ref[:, MLA_Q_RANK:MLA_Q_RANK + MLA_KV_RANK].astype(BF16)
        kp = w_ref[:, MLA_Q_RANK + MLA_KV_RANK:CD_HG0].astype(BF16)
        wkpe_s[...] = jnp.concatenate([kp] * (KPE_LANES // MLA_ROPE), axis=1)
        whg_s[...] = w_ref[:, CD_HG0:].astype(BF16)
        wqu_s[...] = wqu_ref[...].astype(BF16)
        wkvu_s[...] = wkvu_ref[...].astype(BF16)

    h = _norm_mod(x_ref[...], g_ref[...], sc_ref[0], sh_ref[0]).astype(BF16)
    cos, sa, sb = cos_ref[...], sa_ref[...], sb_ref[...]
    nope = MLA_HEADS * MLA_NOPE
    cq = _rms(jnp.dot(h, wcq_s[...], preferred_element_type=F32)) * gq_ref[...]
    qq = jnp.dot(cq.astype(BF16), wqu_s[...], preferred_element_type=F32)
    qn_ref[...] = qq[:, 0:nope].astype(qn_ref.dtype)
    qr_ref[...] = _apply_rope(qq[:, nope:], cos, sa, sb, MLA_ROPE // 4).astype(qr_ref.dtype)
    ckv = _rms(jnp.dot(h, wckv_s[...], preferred_element_type=F32)) * gkv_ref[...]
    ckv_ref[...] = ckv
    kv = jnp.dot(ckv.astype(BF16), wkvu_s[...], preferred_element_type=F32)
    kn_ref[...] = kv[:, 0:nope].astype(kn_ref.dtype)
    vm_ref[...] = kv[:, nope:].astype(vm_ref.dtype)
    kpe_ref[...] = _apply_rope(jnp.dot(h, wkpe_s[...], preferred_element_type=F32), cos, sa, sb, MLA_ROPE // 4)
    hh = jnp.dot(h, whg_s[...], preferred_element_type=F32)
    for n, ref in enumerate((hq_ref, hff_ref, hfb_ref, hi_ref, hg_ref)):
        ref[...] = hh[:, HG_KD * n:HG_KD * (n + 1)]


def _mla_split_heads(w, a):
    k, n = w.shape
    w = w.reshape(k, MLA_HEADS, n // MLA_HEADS)
    return jnp.concatenate([w[:, :, :a].reshape(k, -1), w[:, :, a:].reshape(k, -1)], axis=1)


def _cd_in(x, mod, g, w, g_q, w_q_up, g_kv, w_kv_up, layer):
    cos, sa, sb = _rope_tables(MLA_ROPE)
    d_in = w.shape[1]
    row = lambda n: pl.BlockSpec((ROW_BLK, n), lambda i: (i, 0))
    full = lambda a, b: pl.BlockSpec((a, b), lambda i: (0, 0))
    rope = pl.BlockSpec((ROW_BLK, 128), lambda i: (_rope_blk(i), 0))
    widths = [MLA_HEADS * MLA_NOPE, MLA_HEADS * MLA_ROPE, MLA_KV_RANK, KPE_LANES, MLA_HEADS * MLA_NOPE, MLA_WIDTH] + [HG_KD] * 5
    nq = MLA_HEADS * (MLA_NOPE + MLA_ROPE)
    nkv = MLA_HEADS * (MLA_NOPE + MLA_V)
    return pl.pallas_call(
        _cd_in_kernel,
        grid=(N_ROW_BLK,),
        in_specs=[row(D_MODEL), full(1, D_MODEL), _mod_spec(layer, 1), _mod_spec(layer, 0), full(D_MODEL, d_in),
                  full(1, MLA_Q_RANK), full(MLA_Q_RANK, nq), full(1, MLA_KV_RANK), full(MLA_KV_RANK, nkv), rope, rope, rope],
        out_specs=[row(n) for n in widths],
        out_shape=[jax.ShapeDtypeStruct((N_TOK, n), BF16 if i in (0, 1, 4, 5) else F32) for i, n in enumerate(widths)],
        scratch_shapes=[pltpu.VMEM((D_MODEL, MLA_Q_RANK), BF16), pltpu.VMEM((D_MODEL, MLA_KV_RANK), BF16),
                        pltpu.VMEM((D_MODEL, KPE_LANES), BF16), pltpu.VMEM((D_MODEL, 5 * HG_KD), BF16),
                        pltpu.VMEM((MLA_Q_RANK, nq), BF16), pltpu.VMEM((MLA_KV_RANK, nkv), BF16)],
        compiler_params=_cparams("arbitrary"),
        name="cd_in",
    )(x, g.reshape(1, D_MODEL), mod, mod, w, g_q.reshape(1, -1), _mla_split_heads(w_q_up, MLA_NOPE),
      g_kv.reshape(1, -1), _mla_split_heads(w_kv_up, MLA_NOPE), cos, sa, sb)


def _mm_kernel(a_ref, w_ref, *o_refs):
    r = _dot(a_ref[...], w_ref[...])
    off = 0
    for o in o_refs:
        o[...] = r[:, off:off + o.shape[1]]
        off += o.shape[1]


def _mla_cache_kv(cckv, w_kv_up):
    n = MLA_HEADS * MLA_NOPE
    rows = cckv.shape[0]
    return pl.pallas_call(
        _mm_kernel,
        grid=(rows // PAST_LEN,),
        in_specs=[pl.BlockSpec((PAST_LEN, MLA_KV_RANK), lambda i: (i, 0)),
                  pl.BlockSpec((MLA_KV_RANK, 2 * n), lambda i: (0, 0))],
        out_specs=[pl.BlockSpec((PAST_LEN, n), lambda i: (i, 0))] * 2,
        out_shape=[jax.ShapeDtypeStruct((rows, n), F32)] * 2,
        compiler_params=_cparams("arbitrary"),
        name="mla_cache_kv",
    )(cckv, _mla_split_heads(w_kv_up, MLA_NOPE))


def _mla_heads(qn, qr, keys, o_ref):
    scale = (MLA_NOPE + MLA_ROPE) ** -0.5
    outs = []
    for h in range(MLA_HEADS):
        a = slice(MLA_NOPE * h, MLA_NOPE * (h + 1))
        r = slice(MLA_ROPE * h, MLA_ROPE * (h + 1))
        s_list = [(_dot_nt(qn[:, a], kn[:, a]) + _dot_nt(qr[:, r], kp[:, 0:MLA_ROPE])) * scale for kn, kp, _ in keys]
        outs.append(_softmax_pv(s_list, [v[:, a] for _, _, v in keys], None))
    o_ref[...] = jnp.concatenate(outs, axis=1).astype(o_ref.dtype)


def _mla_ctx_kernel(qn_ref, qr_ref, kn_ref, kp_ref, v_ref, o_ref):
    _mla_heads(qn_ref[...], qr_ref[...], [(kn_ref[...], kp_ref[...], v_ref[...])], o_ref)


def _mla_lat_kernel(qn_ref, qr_ref, kn_ref, kp_ref, v_ref, kcn_ref, kcp_ref, vc_ref, o_ref):
    _mla_heads(qn_ref[...], qr_ref[...],
               [(kcn_ref[...], kcp_ref[0], vc_ref[...]), (kn_ref[...], kp_ref[...], v_ref[...])], o_ref)


def _attention_mla(qn, qr, kn, kpe, vm, kcn, kcpe, vc):
    n = MLA_HEADS * MLA_NOPE
    nr = MLA_HEADS * MLA_ROPE
    blk = lambda w: pl.BlockSpec((SEQ, w), lambda b: (b, 0))
    o = pl.pallas_call(
        _mla_ctx_kernel,
        grid=(BATCH,),
        in_specs=[blk(n), blk(nr), blk(n), blk(KPE_LANES), blk(MLA_WIDTH)],
        out_specs=blk(MLA_WIDTH),
        out_shape=jax.ShapeDtypeStruct((N_CTX_TOK, MLA_WIDTH), BF16),
        compiler_params=_cparams("arbitrary"),
        name="mla_ctx",
    )(qn, qr, kn, kpe, vm)
    nq = DEC_SEQ // ROW_BLK
    qblk = lambda w: pl.BlockSpec((ROW_BLK, w), lambda b, i: (N_CTX_BLK + b * nq + i, 0))
    seq = lambda w: pl.BlockSpec((DEC_SEQ, w), lambda b, i: (N_CTX_TOK // DEC_SEQ + b, 0))
    past = lambda w: pl.BlockSpec((PAST_LEN, w), lambda b, i: (b, 0))
    o_lat = pl.pallas_call(
        _mla_lat_kernel,
        grid=(DEC_BATCH, nq),
        in_specs=[qblk(n), qblk(nr), seq(n), seq(KPE_LANES), seq(MLA_WIDTH), past(n),
                  pl.BlockSpec((1, PAST_LEN, MLA_ROPE), lambda b, i: (b, 0, 0)), past(MLA_WIDTH)],
        out_specs=pl.BlockSpec((ROW_BLK, MLA_WIDTH), lambda b, i: (b * nq + i, 0)),
        out_shape=jax.ShapeDtypeStruct((N_LAT_TOK, MLA_WIDTH), BF16),
        compiler_params=_cparams("arbitrary", "arbitrary"),
        name="mla_lat",
    )(qn, qr, kn, kpe, vm, kcn, kcpe, vc)
    return o, o_lat


HG_TILE = 128
HG_NC = HG_TILE // HG_CHUNK
HG_HALF = 256


HG_SLABS = HG_KD // 128


def _hg_token_plane(raw_s, p):
    return jnp.concatenate([raw_s[j, pl.ds(p, HG_NC, stride=HG_CHUNK), :] for j in range(HG_SLABS)], axis=1)


def _hg_put(ref, x):
    for j in range(HG_SLABS):
        ref[j] = x[:, 128 * j:128 * (j + 1)]


def _hg_get(ref, c, l):
    rows = pl.ds(c, HG_CHUNK, stride=HG_NC)
    return jnp.concatenate([ref[2 * l, rows, :], ref[2 * l + 1, rows, :]], axis=1)


def _hg_plane(ref, p):
    return jnp.concatenate([ref[j, HG_NC * p:HG_NC * (p + 1), :] for j in range(HG_SLABS)], axis=1)


def _hg_put_plane(ref, p, x):
    for j in range(HG_SLABS):
        ref[j, HG_NC * p:HG_NC * (p + 1), :] = x[:, 128 * j:128 * (j + 1)]


HG_PAIRS = HG_CHUNK * (HG_CHUNK + 1) // 2
HG_PAIR_ROWS = HG_PAIRS * HG_NC


def _hg_pair_rows(d, i):
    n = d * HG_CHUNK - d * (d - 1) // 2 + (i - d)
    return slice(HG_NC * n, HG_NC * (n + 1))


def _hg_direction(hq_ref, hf_ref, hi_ref, o_ref, lb, st_ref, raw_s, q_s, f_s, kk_s, v_s, qt_s, kt_s, o_s, p_s, e_s,
                  sign, ones_bd, head_mask):
    _hg_put(raw_s, hq_ref[...])
    for p in range(HG_CHUNK):
        hq = _hg_token_plane(raw_s, p)
        _hg_put_plane(q_s, p, hq * jax.nn.sigmoid(hq))
    _hg_put(raw_s, hf_ref[...])
    for p in range(HG_CHUNK):
        f = lb + (1.0 - lb) * jax.nn.sigmoid(_hg_token_plane(raw_s, p))
        _hg_put_plane(f_s, p, f)
        _hg_put_plane(kk_s, p, 1.0 - f)
    _hg_put(raw_s, hi_ref[...])
    for p in range(HG_CHUNK):
        _hg_put_plane(v_s, p, _hg_token_plane(raw_s, p))
    pos = (lambda i: i) if sign > 0 else (lambda i: HG_CHUNK - 1 - i)
    plane = lambda ref, i: _hg_plane(ref, pos(i))

    for i in range(HG_CHUNK):
        q = plane(q_s, i)
        p_s[_hg_pair_rows(0, i), :] = q * plane(kk_s, i)
        dec = None
        for d in range(1, i + 1):
            fd = plane(f_s, i - d + 1)
            dec = fd if dec is None else dec * fd
            p_s[_hg_pair_rows(d, i), :] = q * plane(kk_s, i - d) * dec
    step = HG_PAIR_ROWS // 4
    for c in range(0, HG_PAIR_ROWS, step):
        pb = p_s[c:c + step, :].astype(BF16)
        e_s[c:c + step, 0:HG_HALF] = jnp.dot(pb[:, 0:HG_HALF], ones_bd, preferred_element_type=F32)
        e_s[c:c + step, HG_HALF:] = jnp.dot(pb[:, HG_HALF:], ones_bd, preferred_element_type=F32)
    for i in range(HG_CHUNK):
        o = None
        for d in range(i + 1):
            t = e_s[_hg_pair_rows(d, i), :] * plane(v_s, i - d)
            o = t if o is None else o + t
        _hg_put_plane(o_s, pos(i), o)

    incl = None
    for i in range(HG_CHUNK):
        fi = plane(f_s, i)
        incl = fi if incl is None else incl * fi
        _hg_put_plane(qt_s, pos(i), plane(q_s, i) * incl)
    whole = incl
    excl = None
    for i in range(HG_CHUNK - 1, -1, -1):
        kt = plane(kk_s, i)
        if excl is not None:
            kt = kt * excl
        _hg_put_plane(kt_s, pos(i), kt)
        fi = plane(f_s, i)
        excl = fi if excl is None else excl * fi

    tile4 = lambda x: jnp.where(head_mask, jnp.concatenate([x] * 4, axis=0), 0.0).astype(BF16)
    for cc in range(HG_NC):
        c = cc if sign > 0 else HG_NC - 1 - cc
        r = slice(HG_CHUNK * c, HG_CHUNK * (c + 1))
        for g in range(2):
            l = slice(HG_HALF * g, HG_HALF * (g + 1))
            st = st_ref[g]
            oc = _dot_nt(tile4(_hg_get(qt_s, c, g)), st)
            o_ref[r, l] = _hg_get(o_s, c, g) + jnp.concatenate(
                [oc[HG_CHUNK * h:HG_CHUNK * (h + 1), :] for h in range(4)], axis=1)
            v = hi_ref[r, l]
            vs = jnp.concatenate([v[:, HG_VAL * h:HG_VAL * (h + 1)] for h in range(4)], axis=0)
            ds = lax.dot_general(vs.astype(BF16), tile4(_hg_get(kt_s, c, g)), (((0,), (0,)), ((), ())),
                                 preferred_element_type=F32)
            st_ref[g] = st * whole[c:c + 1, l] + ds


def _hg_lower_bound(lb_ref, d, layer):
    raw = [lb_ref[d * DEPTH + m:d * DEPTH + m + 1, :] for m in range(DEPTH)]
    mx = functools.reduce(jnp.maximum, raw)
    e = [jnp.exp(r - mx) for r in raw]
    tot = functools.reduce(jnp.add, e)
    return functools.reduce(jnp.add, e[1:layer + 1], jnp.zeros_like(tot)) / tot


def _hgrn_kernel(nt, layer, hqf_ref, hff_ref, hif_ref, hqb_ref, hfb_ref, hib_ref, lb_ref, s0_ref,
                 of_ref, ob_ref, sfin_ref, st_s, raw_s, q_s, f_s, kk_s, v_s, qt_s, kt_s, o_s, p_s, e_s):
    i = pl.program_id(1)

    @pl.when(i == 0)
    def _():
        st_s[...] = s0_ref[0]

    ri = lax.broadcasted_iota(jnp.int32, (HG_HALF, HG_HALF), 0) // HG_KEY
    ci = lax.broadcasted_iota(jnp.int32, (HG_HALF, HG_HALF), 1) // HG_KEY
    ones_bd = jnp.where(ri == ci, 1.0, 0.0).astype(BF16)
    head_mask = (lax.broadcasted_iota(jnp.int32, (4 * HG_CHUNK, HG_HALF), 0) // HG_CHUNK
                 == lax.broadcasted_iota(jnp.int32, (4 * HG_CHUNK, HG_HALF), 1) // HG_KEY)
    _hg_direction(hqf_ref, hff_ref, hif_ref, of_ref, _hg_lower_bound(lb_ref, 0, layer), st_s.at[0],
                  raw_s, q_s, f_s, kk_s, v_s, qt_s, kt_s, o_s, p_s, e_s, 1, ones_bd, head_mask)
    _hg_direction(hqb_ref, hfb_ref, hib_ref, ob_ref, _hg_lower_bound(lb_ref, 1, layer), st_s.at[1],
                  raw_s, q_s, f_s, kk_s, v_s, qt_s, kt_s, o_s, p_s, e_s, -1, ones_bd, head_mask)

    @pl.when(i == nt - 1)
    def _():
        sfin_ref[0] = st_s[...]


def _hg_state_to_blocks(s):
    b = s.shape[0]
    st = s.reshape(b, 2, 2, 4, HG_KEY, HG_VAL).transpose(0, 1, 2, 5, 3, 4)
    return st.reshape(b, 2, 2, HG_VAL, HG_HALF)


def _hg_state_from_blocks(st):
    b = st.shape[0]
    st = st.reshape(b, 2, 2, HG_VAL, 4, HG_KEY).transpose(0, 1, 2, 4, 5, 3)
    return st.reshape(b, 2, HG_HEADS, HG_KEY, HG_VAL)


def _hgrn_scan(hq, hff, hfb, hi, lb, s0, layer, row0, nseq, seqlen):
    nt = seqlen // HG_TILE
    base = row0 // HG_TILE
    fwd = pl.BlockSpec((HG_TILE, HG_KD), lambda b, i: (base + b * nt + i, 0))
    bwd = pl.BlockSpec((HG_TILE, HG_KD), lambda b, i: (base + b * nt + nt - 1 - i, 0))
    ofw = pl.BlockSpec((HG_TILE, HG_KD), lambda b, i: (b * nt + i, 0))
    obw = pl.BlockSpec((HG_TILE, HG_KD), lambda b, i: (b * nt + nt - 1 - i, 0))
    st = pl.BlockSpec((1, 2, 2, HG_VAL, HG_HALF), lambda b, i: (b, 0, 0, 0, 0))
    tile = lambda: pltpu.VMEM((HG_SLABS, HG_TILE, 128), F32)
    pairs = lambda: pltpu.VMEM((HG_PAIR_ROWS, HG_KD), F32)
    return pl.pallas_call(
        functools.partial(_hgrn_kernel, nt, layer),
        grid=(nseq, nt),
        in_specs=[fwd, fwd, fwd, bwd, bwd, bwd, pl.BlockSpec((2 * DEPTH, HG_KD), lambda b, i: (0, 0)), st],
        out_specs=[ofw, obw, st],
        out_shape=[jax.ShapeDtypeStruct((nseq * seqlen, HG_WIDTH), F32)] * 2
        + [jax.ShapeDtypeStruct((nseq, 2, 2, HG_VAL, HG_HALF), F32)],
        scratch_shapes=[pltpu.VMEM((2, 2, HG_VAL, HG_HALF), F32)] + [tile() for _ in range(8)] + [pairs(), pairs()],
        compiler_params=_cparams("arbitrary", "arbitrary"),
        name="hgrn_scan",
    )(hq, hff, hi, hq, hfb, hi, lb.reshape(2 * DEPTH, HG_KD), s0)


def _cd_out_kernel(x_ref, occ_ref, ocl_ref, ofc_ref, ofl_ref, obc_ref, obl_ref, hg_ref, go_ref, wout_ref, gt_ref,
                   o_ref, wa_s, wb_s):
    @pl.when(pl.program_id(0) == 0)
    def _():
        wa_s[...] = wout_ref[0:MLA_WIDTH, :].astype(BF16)
        wb_s[...] = wout_ref[MLA_WIDTH:, :].astype(BF16)

    ri = lax.broadcasted_iota(jnp.int32, (HG_HALF, HG_HALF), 0) // HG_VAL
    ci = lax.broadcasted_iota(jnp.int32, (HG_HALF, HG_HALF), 1) // HG_VAL
    ones_bd = jnp.where(ri == ci, 1.0, 0.0).astype(BF16)
    o = _ctx_or_lat(ofc_ref, ofl_ref) + _ctx_or_lat(obc_ref, obl_ref)
    sq = o * o
    hi = sq.astype(BF16)
    lo = (sq - hi.astype(F32)).astype(BF16)
    ms = jnp.concatenate(
        [jnp.dot(hi[:, l], ones_bd, preferred_element_type=F32) + jnp.dot(lo[:, l], ones_bd, preferred_element_type=F32)
         for l in (slice(0, HG_HALF), slice(HG_HALF, 2 * HG_HALF))], axis=1) * (1.0 / HG_VAL)
    hg = hg_ref[...]
    od = o * lax.rsqrt(ms + EPS) * go_ref[...] * (hg * jax.nn.sigmoid(hg))
    out = (jnp.dot(_ctx_or_lat(occ_ref, ocl_ref).astype(BF16), wa_s[...], preferred_element_type=F32)
           + jnp.dot(od.astype(BF16), wb_s[...], preferred_element_type=F32))
    o_ref[...] = x_ref[...] + gt_ref[0] * out


def _cd_out(x, oc, of, ob, hg, g_o, w_out, mod, layer):
    row = lambda n: pl.BlockSpec((ROW_BLK, n), lambda i: (i, 0))
    full = lambda a, b: pl.BlockSpec((a, b), lambda i: (0, 0))
    return pl.pallas_call(
        _cd_out_kernel,
        grid=(N_ROW_BLK,),
        in_specs=[row(D_MODEL)] + _split_specs(MLA_WIDTH) + _split_specs(HG_WIDTH) + _split_specs(HG_WIDTH)
        + [row(HG_WIDTH), full(1, HG_WIDTH), full(MLA_WIDTH + HG_WIDTH, D_MODEL), _mod_spec(layer, 2)],
        out_specs=row(D_MODEL),
        out_shape=jax.ShapeDtypeStruct((N_TOK, D_MODEL), F32),
        scratch_shapes=[pltpu.VMEM((MLA_WIDTH, D_MODEL), BF16), pltpu.VMEM((HG_WIDTH, D_MODEL), BF16)],
        compiler_params=_cparams("arbitrary"),
        name="cd_out",
    )(x, oc[0], oc[1], of[0], of[1], ob[0], ob[1], hg, jnp.tile(g_o, HG_HEADS).reshape(1, HG_WIDTH), w_out, mod)


N_PAIRS = 6
N_CLASSES = N_GROUPS * N_PAIRS
CLS_ROWS = 32
MOE_BM = 256
MOE_NBLK = N_TOK // MOE_BM + N_CLASSES
MOE_ROWS = MOE_NBLK * MOE_BM
PAIR_SLOTS = ((0, 1), (3, 1), (2, 1), (2, 0), (3, 0), (3, 2))


def _moe_route_kernel(x_ref, g_ref, sc_ref, sh_ref, rw_ref, rb_ref, h_ref, ti_ref, tw_ref, cnt_ref, base_s):
    @pl.when(pl.program_id(0) == 0)
    def _():
        base_s[...] = jnp.zeros_like(base_s)

    h = _norm_mod(x_ref[...], g_ref[...], sc_ref[0], sh_ref[0])
    logits = lax.dot_general(rw_ref[...], h, (((1,), (1,)), ((), ())), precision=HI,
                             preferred_element_type=F32)
    aff = jax.nn.sigmoid(logits)
    sel = aff + rb_ref[...]
    s = [sel[e:e + 1, :] for e in range(N_EXPERTS)]
    a = [aff[e:e + 1, :] for e in range(N_EXPERTS)]
    gs = []
    for g in range(N_GROUPS):
        m = s[4 * g:4 * g + 4]
        pairs = [m[i] + m[j] for i in range(4) for j in range(i + 1, 4)]
        gs.append(functools.reduce(jnp.maximum, pairs))
    gmax = functools.reduce(jnp.maximum, gs)
    taken = jnp.zeros_like(gmax) > 1.0
    gsel = []
    for g in range(N_GROUPS):
        hit = (gs[g] == gmax) & jnp.logical_not(taken)
        gsel.append(hit)
        taken = taken | hit
    e_lo = jnp.zeros(gmax.shape, jnp.int32)
    e_hi = jnp.zeros(gmax.shape, jnp.int32)
    a_lo = jnp.zeros_like(gmax)
    a_hi = jnp.zeros_like(gmax)
    nsel = jnp.zeros(gmax.shape, jnp.int32)
    for g in range(N_GROUPS):
        for i in range(4):
            e = 4 * g + i
            beat = jnp.zeros(gmax.shape, jnp.int32)
            for j in range(4):
                if j != i:
                    o = 4 * g + j
                    beat = beat + jnp.where((s[o] > s[e]) | ((s[o] == s[e]) & (j < i)), 1, 0)
            pick = gsel[g] & (beat < 2)
            is_first = pick & (nsel == 0)
            is_second = pick & (nsel == 1)
            e_lo = jnp.where(is_first, e, e_lo)
            a_lo = jnp.where(is_first, a[e], a_lo)
            e_hi = jnp.where(is_second, e, e_hi)
            a_hi = jnp.where(is_second, a[e], a_hi)
            nsel = nsel + jnp.where(pick, 1, 0)
    grp = e_lo // EXPERTS_PER_GROUP
    lo = e_lo - grp * EXPERTS_PER_GROUP
    hi = e_hi - grp * EXPERTS_PER_GROUP
    pair = ((lo * (7 - lo)) >> 1) + (hi - lo - 1)
    pair = jnp.where(pair == 1, 3, jnp.where(pair == 2, 4, jnp.where(pair == 3, 2, jnp.where(pair == 4, 1, pair))))
    cls = grp * N_PAIRS + pair
    wsum = a_lo + a_hi
    w_lo, w_hi = a_lo / wsum, a_hi / wsum
    onehot = (lax.broadcasted_iota(jnp.int32, (CLS_ROWS, ROW_BLK), 0) == cls).astype(F32)
    tt = lax.broadcasted_iota(jnp.int32, (ROW_BLK, ROW_BLK), 0) < lax.broadcasted_iota(jnp.int32, (ROW_BLK, ROW_BLK), 1)
    before = _dot(onehot, jnp.where(tt, 1.0, 0.0))
    base = base_s[...]
    rank = jnp.sum(onehot * (before + base[:, 0:1]), axis=0, keepdims=True).astype(jnp.int32)
    base = base + jnp.sum(onehot, axis=1, keepdims=True)
    base_s[...] = base
    cnt_ref[...] = base.astype(jnp.int32)
    ti_ref[0] = jnp.concatenate([cls, rank, e_lo, e_hi, jnp.zeros((4, ROW_BLK), jnp.int32)], axis=0)
    ident = lax.broadcasted_iota(jnp.int32, (ROW_BLK, ROW_BLK), 0) == lax.broadcasted_iota(jnp.int32, (ROW_BLK, ROW_BLK), 1)
    col = lambda r: jnp.sum(jnp.where(ident, r, 0.0), axis=1, keepdims=True)
    tw_ref[...] = jnp.where(lax.broadcasted_iota(jnp.int32, (ROW_BLK, 128), 1) < 64, col(w_lo), col(w_hi))
    h_ref[...] = h


def _moe_route(x, mod, g, router_w, router_b, layer):
    row = lambda n: pl.BlockSpec((ROW_BLK, n), lambda i: (i, 0))
    full = lambda a, b: pl.BlockSpec((a, b), lambda i: (0, 0))
    return pl.pallas_call(
        _moe_route_kernel,
        grid=(N_ROW_BLK,),
        in_specs=[row(D_MODEL), full(1, D_MODEL), _mod_spec(layer, 4), _mod_spec(layer, 3),
                  full(N_EXPERTS, D_MODEL), full(N_EXPERTS, 1)],
        out_specs=[row(D_MODEL), pl.BlockSpec((1, 8, ROW_BLK), lambda i: (i, 0, 0)), row(128), full(CLS_ROWS, 128)],
        out_shape=[jax.ShapeDtypeStruct((N_TOK, D_MODEL), F32),
                   jax.ShapeDtypeStruct((N_ROW_BLK, 8, ROW_BLK), jnp.int32),
                   jax.ShapeDtypeStruct((N_TOK, 128), F32),
                   jax.ShapeDtypeStruct((CLS_ROWS, 128), jnp.int32)],
        scratch_shapes=[pltpu.VMEM((CLS_ROWS, 128), F32)],
        compiler_params=_cparams("arbitrary"),
        name="moe_route",
    )(x, g.reshape(1, D_MODEL), mod, mod, router_w.T, router_b.reshape(N_EXPERTS, 1))


def _moe_sort_kernel(pos_ref, nblk_ref, swap_ref, h_ref, wt_ref, hs_ref, ws_ref, perm_ref, stage_s, wstage_s):
    j = pl.program_id(0)

    @pl.when(j == 0)
    def _():
        def init(r, c):
            perm_ref[r] = N_TOK
            return c

        lax.fori_loop(0, MOE_ROWS, init, 0, unroll=16)

        def build(t, c):
            perm_ref[pos_ref[t]] = t
            return c

        lax.fori_loop(0, N_TOK, build, 0, unroll=16)

    @pl.when(j < nblk_ref[0])
    def _():
        base = j * MOE_BM
        for r in range(MOE_BM):
            src = jnp.minimum(perm_ref[base + r], N_TOK - 1)
            stage_s[r:r + 1, :] = h_ref[pl.ds(src, 1), :]
            wstage_s[r:r + 1, :] = wt_ref[pl.ds(src, 1), :]
        hs_ref[...] = stage_s[...].astype(BF16)
        w = wstage_s[...]
        ws_ref[...] = jnp.where(swap_ref[j] == 1, pltpu.roll(w, 64, axis=1), w)

    @pl.when(j >= nblk_ref[0])
    def _():
        hs_ref[...] = jnp.zeros_like(hs_ref)
        ws_ref[...] = jnp.zeros_like(ws_ref)


def _moe_sort(h, wtok, pos, nblk, swap):
    res = lambda w: pl.BlockSpec((N_TOK, w), lambda j, *_: (0, 0), pipeline_mode=pl.Buffered(1))
    grid_spec = pltpu.PrefetchScalarGridSpec(
        num_scalar_prefetch=3,
        grid=(MOE_NBLK,),
        in_specs=[res(D_MODEL), res(128)],
        out_specs=[pl.BlockSpec((MOE_BM, D_MODEL), lambda j, *_: (j, 0)),
                   pl.BlockSpec((MOE_BM, 128), lambda j, *_: (j, 0)),
                   pl.BlockSpec(memory_space=pltpu.SMEM)],
        scratch_shapes=[pltpu.VMEM((MOE_BM, D_MODEL), F32), pltpu.VMEM((MOE_BM, 128), F32)],
    )
    return pl.pallas_call(
        _moe_sort_kernel,
        grid_spec=grid_spec,
        out_shape=[jax.ShapeDtypeStruct((MOE_ROWS, D_MODEL), BF16), jax.ShapeDtypeStruct((MOE_ROWS, 128), F32),
                   jax.ShapeDtypeStruct((MOE_ROWS,), jnp.int32)],
        compiler_params=_cparams("arbitrary"),
        name="moe_sort",
    )(pos, nblk, swap, h, wtok)


def _moe_experts_kernel(elo_ref, ehi_ref, nblk_ref, hs_ref, ws_ref, wgl_ref, wul_ref, wdl_ref,
                        wgh_ref, wuh_ref, wdh_ref, y_ref):
    del elo_ref, ehi_ref
    j = pl.program_id(0)

    @pl.when(j < nblk_ref[0])
    def _():
        h = hs_ref[...]
        acc = None
        for wg, wu, wd, gate in ((wgl_ref, wul_ref, wdl_ref, ws_ref[:, 0:1]), (wgh_ref, wuh_ref, wdh_ref, ws_ref[:, 64:65])):
            g = _dot(h, wg[0, 0])
            u = _dot(h, wu[0, 0])
            hid = g * jax.nn.sigmoid(g) * u * gate
            y = _dot(hid, wd[0, 0])
            acc = y if acc is None else acc + y
        y_ref[...] = acc

    @pl.when(j >= nblk_ref[0])
    def _():
        y_ref[...] = jnp.zeros_like(y_ref)


def _moe_experts(hs, ws, blk_elo, blk_ehi, nblk, w_gate, w_up, w_down, layer):
    last = lambda j, nb: jnp.minimum(j, nb[0] - 1)
    wspec = lambda a, b, which: pl.BlockSpec(
        (1, 1, a, b), lambda j, elo, ehi, nb: (layer, (elo, ehi)[which][last(j, nb)], 0, 0))
    grid_spec = pltpu.PrefetchScalarGridSpec(
        num_scalar_prefetch=3,
        grid=(MOE_NBLK,),
        in_specs=[pl.BlockSpec((MOE_BM, D_MODEL), lambda j, elo, ehi, nb: (last(j, nb), 0)),
                  pl.BlockSpec((MOE_BM, 128), lambda j, elo, ehi, nb: (last(j, nb), 0)),
                  wspec(D_MODEL, D_FF, 0), wspec(D_MODEL, D_FF, 0), wspec(D_FF, D_MODEL, 0),
                  wspec(D_MODEL, D_FF, 1), wspec(D_MODEL, D_FF, 1), wspec(D_FF, D_MODEL, 1)],
        out_specs=pl.BlockSpec((MOE_BM, D_MODEL), lambda j, *_: (j, 0)),
    )
    return pl.pallas_call(
        _moe_experts_kernel,
        grid_spec=grid_spec,
        out_shape=jax.ShapeDtypeStruct((MOE_ROWS, D_MODEL), F32),
        compiler_params=_cparams("arbitrary"),
        name="moe_experts",
    )(blk_elo, blk_ehi, nblk, hs, ws, w_gate, w_up, w_down, w_gate, w_up, w_down)


SC_ROWS = 64


def _sc_gather_rows(table, idx):
    info = plsc.get_sparse_core_info()
    nc, nw = info.num_cores, info.num_cores * info.num_subcores
    b, d = idx.shape[0], table.shape[1]
    per_w = b // nw
    assert per_w * nw == b and per_w % SC_ROWS == 0
    mesh = plsc.VectorSubcoreMesh(core_axis_name="c", subcore_axis_name="s")

    @functools.partial(
        pl.kernel, mesh=mesh, out_type=jax.ShapeDtypeStruct((b, d), table.dtype),
        scratch_types=[pltpu.VMEM((SC_ROWS,), jnp.int32), pltpu.VMEM((SC_ROWS, d), table.dtype),
                       pltpu.SemaphoreType.DMA])
    def gather(table_hbm, idx_hbm, out_hbm, idx_v, rows_v, sem):
        wid = lax.axis_index("s") * nc + lax.axis_index("c")
        for c in range(per_w // SC_ROWS):
            base = wid * per_w + c * SC_ROWS
            pltpu.sync_copy(idx_hbm.at[pl.ds(base, SC_ROWS)], idx_v)
            pltpu.async_copy(table_hbm.at[idx_v], rows_v, sem).wait()
            pltpu.sync_copy(rows_v, out_hbm.at[pl.ds(base, SC_ROWS)])

    return gather(table, idx)


def _moe_residual_kernel(x_ref, y_ref, gt_ref, o_ref):
    o_ref[...] = x_ref[...] + gt_ref[0] * y_ref[...]


def _moe_combine(x, y_sorted, pos, mod, layer):
    row = pl.BlockSpec((ROW_BLK, D_MODEL), lambda i: (i, 0))
    return pl.pallas_call(
        _moe_residual_kernel,
        grid=(N_ROW_BLK,),
        in_specs=[row, row, _mod_spec(layer, 5)],
        out_specs=row,
        out_shape=jax.ShapeDtypeStruct((N_TOK, D_MODEL), F32),
        compiler_params=_cparams("arbitrary"),
        name="moe_residual",
    )(x, _sc_gather_rows(y_sorted, pos), mod)


def _moe(x, mod, g, router_w, router_b, w_gate, w_up, w_down, layer):
    h, info, wtok, counts = _moe_route(x, mod, g, router_w, router_b, layer)
    cls = info[:, 0, :].reshape(N_TOK)
    rank = info[:, 1, :].reshape(N_TOK)
    cnt = counts[:N_CLASSES, 0]
    nb = (cnt + MOE_BM - 1) // MOE_BM
    ends = jnp.cumsum(nb)
    starts = ends - nb
    pos = ((starts * MOE_BM)[cls] + rank).astype(jnp.int32)
    blk = jnp.arange(MOE_NBLK, dtype=jnp.int32)
    blk_cls = jnp.minimum(jnp.sum((blk[:, None] >= ends[None, :]).astype(jnp.int32), axis=1), N_CLASSES - 1)
    slot_a = jnp.asarray([s[0] for s in PAIR_SLOTS], jnp.int32)
    slot_b = jnp.asarray([s[1] for s in PAIR_SLOTS], jnp.int32)
    grp = blk_cls // N_PAIRS
    blk_ea = (grp * EXPERTS_PER_GROUP + slot_a[blk_cls % N_PAIRS]).astype(jnp.int32)
    blk_eb = (grp * EXPERTS_PER_GROUP + slot_b[blk_cls % N_PAIRS]).astype(jnp.int32)
    swap = (blk_ea > blk_eb).astype(jnp.int32)
    nblk = ends[-1:].astype(jnp.int32)
    hs, ws, perm = _moe_sort(h, wtok, pos, nblk, swap)
    y_sorted = _moe_experts(hs, ws, blk_ea, blk_eb, nblk, w_gate, w_up, w_down, layer)
    return _moe_combine(x, y_sorted, pos, mod, layer)


def _final_norm_kernel(x_ref, g_ref, o_ref):
    o_ref[...] = _rms(x_ref[...]) * g_ref[...]


def _final_norm(x, g, row0, rows):
    base = row0 // ROW_BLK
    return pl.pallas_call(
        _final_norm_kernel,
        grid=(rows // ROW_BLK,),
        in_specs=[pl.BlockSpec((ROW_BLK, D_MODEL), lambda i: (base + i, 0)), pl.BlockSpec((1, D_MODEL), lambda i: (0, 0))],
        out_specs=pl.BlockSpec((ROW_BLK, D_MODEL), lambda i: (i, 0)),
        out_shape=jax.ShapeDtypeStruct((rows, D_MODEL), F32),
        compiler_params=_cparams("arbitrary"),
        name="final_norm",
    )(x, g.reshape(1, D_MODEL))


def kernel(x_prompt, x_sample, cache_attn_k, cache_attn_v, state_ssm_re, state_ssm_im, cache_mla_ckv, cache_mla_kpe,
           state_hgrn, c, c_ctx, w_mod, b_mod, g_mix, g_ffn, g_final, router_w, router_b, moe_w_gate, moe_w_up,
           moe_w_down, ab_w_in, ab_sink, s5_lam_re, s5_lam_im, s5_log_dt, s5_b_re, s5_b_im, s5_c_re, s5_c_im, s5_d,
           s5_w_glu, ab_w_out, cd_w_in, mla_g_q, mla_w_q_up, mla_g_kv, mla_w_kv_up, hg_lower_bounds, hg_g_o, cd_w_out):
    x = jnp.concatenate([x_prompt.reshape(N_CTX_TOK, D_MODEL), x_sample.reshape(N_LAT_TOK, D_MODEL)], axis=0)
    cond = jnp.zeros((MOD_ROWS, D_MODEL), F32).at[0].set(c_ctx).at[1:1 + DEC_BATCH].set(c)
    mod = _modulation(cond, w_mod, b_mod)
    keep = ([], [], [], [], [], [], [])
    for l in range(DEPTH):
        j = l // 2
        if l % 2 == 0:
            q, k, v, u = _ab_in(x, mod, g_mix[l], ab_w_in[j], l)
            o_a = _attention_a(q, k, v, cache_attn_k[:, j].reshape(DEC_BATCH, PAST_LEN, A_KV_WIDTH),
                               cache_attn_v[:, j].reshape(DEC_BATCH, PAST_LEN, A_KV_WIDTH), ab_sink[j])
            y, fin_re, fin_im = _s5_scan(u, state_ssm_re[:, j], state_ssm_im[:, j], s5_lam_re[j], s5_lam_im[j],
                                         s5_log_dt[j], s5_b_re[j], s5_b_im[j], s5_c_re[j], s5_c_im[j])
            x = _ab_out(x, o_a, y, u, s5_d[j], s5_w_glu[j], ab_w_out[j], mod, l)
            keep[0].append(k[:N_CTX_TOK].reshape(BATCH, SEQ, A_KV_HEADS, HEAD_DIM))
            keep[1].append(v[:N_CTX_TOK].reshape(BATCH, SEQ, A_KV_HEADS, HEAD_DIM))
            keep[2].append(fin_re)
            keep[3].append(fin_im)
        else:
            qn, qr, ckv, kpe, kn, vm, hq, hff, hfb, hi, hg = _cd_in(
                x, mod, g_mix[l], cd_w_in[j], mla_g_q[j], mla_w_q_up[j], mla_g_kv[j], mla_w_kv_up[j], l)
            kcn, vc = _mla_cache_kv(cache_mla_ckv[:, j].reshape(DEC_BATCH * PAST_LEN, MLA_KV_RANK), mla_w_kv_up[j])
            o_c = _attention_mla(qn, qr, kn, kpe, vm, kcn, cache_mla_kpe[:, j], vc)
            s0_ctx = jnp.zeros((BATCH, 2, 2, HG_VAL, HG_HALF), F32)
            of_c, ob_c, s_fin = _hgrn_scan(hq, hff, hfb, hi, hg_lower_bounds, s0_ctx, l, 0, BATCH, SEQ)
            of_l, ob_l, _ = _hgrn_scan(hq, hff, hfb, hi, hg_lower_bounds, _hg_state_to_blocks(state_hgrn[:, j]), l,
                                       N_CTX_TOK, DEC_BATCH, DEC_SEQ)
            x = _cd_out(x, o_c, (of_c, of_l), (ob_c, ob_l), hg, hg_g_o[j], cd_w_out[j], mod, l)
            keep[4].append(ckv[:N_CTX_TOK].reshape(BATCH, SEQ, MLA_KV_RANK))
            keep[5].append(kpe[:N_CTX_TOK, :MLA_ROPE].reshape(BATCH, SEQ, MLA_ROPE))
            keep[6].append(_hg_state_from_blocks(s_fin))
        x = _moe(x, mod, g_ffn[l], router_w, router_b, moe_w_gate, moe_w_up, moe_w_down, l)
    y_ctx = _final_norm(x, g_final, 0, N_CTX_TOK)
    y_lat = _final_norm(x, g_final, N_CTX_TOK, N_LAT_TOK)
    return (y_ctx.reshape(BATCH, SEQ, D_MODEL), y_lat.reshape(DEC_BATCH, DEC_SEQ, D_MODEL),
            jnp.stack(keep[0], 1), jnp.stack(keep[1], 1), jnp.stack(keep[2], 1), jnp.stack(keep[3], 1),
            jnp.stack(keep[4], 1), jnp.stack(keep[5], 1), jnp.stack(keep[6], 1))
```

```python
import functools
import math

import numpy as np
import jax
import jax.numpy as jnp
from jax import lax
from jax.experimental import pallas as pl
from jax.experimental.pallas import tpu as pltpu
from jax.experimental.pallas import tpu_sc as plsc

F32 = jnp.float32
BF16 = jnp.bfloat16

D_MODEL = 1024
BATCH = 16
SEQ = 256
DEPTH = 2
DEC_BATCH = 4
DEC_SEQ = 1024
PAST_LEN = 512
GRID_W = 64
N_MOD = 6
EPS = 1e-6
NEG_INF = -1e30
ROPE_BASE = 10000.0
HEAD_DIM = 64
A_HEADS = 8
A_KV_HEADS = 2
A_WINDOW = 128
A_WIDTH = A_HEADS * HEAD_DIM
A_KV_WIDTH = A_KV_HEADS * HEAD_DIM
S5_WIDTH = D_MODEL // 2
S5_GROUP = 16
S5_GROUPS = S5_WIDTH // S5_GROUP
S5_STATE = 64
MLA_HEADS = 8
MLA_Q_RANK = D_MODEL // 4
MLA_KV_RANK = D_MODEL // 8
MLA_NOPE = 64
MLA_ROPE = 32
MLA_V = 64
MLA_WIDTH = MLA_HEADS * MLA_V
HG_HEADS = 8
HG_KEY = 64
HG_VAL = 64
HG_KD = HG_HEADS * HG_KEY
HG_WIDTH = HG_HEADS * HG_VAL
HG_CHUNK = 16
N_EXPERTS = 16
N_GROUPS = 4
EXPERTS_PER_GROUP = N_EXPERTS // N_GROUPS
D_FF = D_MODEL // 2

N_CTX_TOK = BATCH * SEQ
N_LAT_TOK = DEC_BATCH * DEC_SEQ
N_TOK = N_CTX_TOK + N_LAT_TOK
ROW_BLK = 256
N_ROW_BLK = N_TOK // ROW_BLK
N_CTX_BLK = N_CTX_TOK // ROW_BLK
LAT_BLK_PER_SEQ = DEC_SEQ // ROW_BLK
MOD_ROWS = 8
VMEM_LIMIT = 56 * 1024 * 1024


def _cparams(*sem):
    return pltpu.CompilerParams(dimension_semantics=sem, vmem_limit_bytes=VMEM_LIMIT)


def _mod_group(i):
    return jnp.where(i < N_CTX_BLK, 0, 1 + (i - N_CTX_BLK) // LAT_BLK_PER_SEQ)


def _mod_spec(layer, which):
    return pl.BlockSpec((1, 1, D_MODEL), lambda i: ((layer * MOD_ROWS + _mod_group(i)) * N_MOD + which, 0, 0))


def _rope_blk(i):
    return jnp.where(i < N_CTX_BLK, 0, 1 + (i - N_CTX_BLK) % LAT_BLK_PER_SEQ)


def _rope_tables(rot_dim):
    n_freq = rot_dim // 4
    t = np.arange(DEC_SEQ)
    rows = (t // GRID_W).astype(np.float32)
    cols = (t % GRID_W).astype(np.float32)
    inv = (np.float32(ROPE_BASE) ** (-np.arange(n_freq, dtype=np.float32) / np.float32(n_freq))).astype(np.float32)
    ang_r = rows[:, None] * inv[None, :]
    ang_c = cols[:, None] * inv[None, :]
    ang = np.concatenate([ang_r, ang_r, ang_c, ang_c], axis=-1).astype(np.float32)
    reps = 128 // rot_dim
    cos = np.tile(np.cos(ang), (1, reps)).astype(np.float32)
    sin = np.tile(np.sin(ang), (1, reps)).astype(np.float32)
    lane = np.arange(128)
    first = (lane % (2 * n_freq)) < n_freq
    sin_a = np.where(first[None, :], -sin, 0.0).astype(np.float32)
    sin_b = np.where(first[None, :], 0.0, sin).astype(np.float32)
    ident = np.zeros((ROW_BLK, 128), np.float32)
    cos = np.concatenate([ident + 1.0, cos], axis=0)
    sin_a = np.concatenate([ident, sin_a], axis=0)
    sin_b = np.concatenate([ident, sin_b], axis=0)
    return jnp.asarray(cos), jnp.asarray(sin_a), jnp.asarray(sin_b)


def _apply_rope(x, cos, sin_a, sin_b, quarter):
    outs = []
    for j in range(x.shape[1] // 128):
        xt = x[:, 128 * j:128 * (j + 1)]
        up = pltpu.roll(xt, 128 - quarter, axis=1)
        dn = pltpu.roll(xt, quarter, axis=1)
        outs.append(xt * cos + up * sin_a + dn * sin_b)
    return outs[0] if len(outs) == 1 else jnp.concatenate(outs, axis=1)


def _rms(x):
    return x * lax.rsqrt(jnp.mean(x * x, axis=-1, keepdims=True) + EPS)


def _norm_mod(x, g, sc, sh):
    return _rms(x) * g * (1.0 + sc) + sh


def _dot(a, b):
    return jnp.dot(a.astype(BF16), b.astype(BF16), preferred_element_type=F32)


def _dot_nt(a, b):
    return lax.dot_general(a.astype(BF16), b.astype(BF16), (((1,), (1,)), ((), ())), preferred_element_type=F32)


def _mod_kernel(cond_ref, w_ref, b_ref, o_ref):
    c = cond_ref[...]
    s = c * jax.nn.sigmoid(c)
    o_ref[0] = _dot(s, w_ref[0]) + b_ref[0]


def _modulation(cond, w_mod, b_mod):
    nb = 1024
    out = pl.pallas_call(
        _mod_kernel,
        grid=(DEPTH, N_MOD * D_MODEL // nb),
        in_specs=[pl.BlockSpec((MOD_ROWS, D_MODEL), lambda l, n: (0, 0)),
                  pl.BlockSpec((1, D_MODEL, nb), lambda l, n: (l, 0, n)),
                  pl.BlockSpec((1, 1, nb), lambda l, n: (l, 0, n))],
        out_specs=pl.BlockSpec((1, MOD_ROWS, nb), lambda l, n: (l, 0, n)),
        out_shape=jax.ShapeDtypeStruct((DEPTH, MOD_ROWS, N_MOD * D_MODEL), F32),
        compiler_params=_cparams("arbitrary", "arbitrary"),
        name="modulation",
    )(cond, w_mod, b_mod.reshape(DEPTH, 1, N_MOD * D_MODEL))
    return out.reshape(DEPTH * MOD_ROWS * N_MOD, 1, D_MODEL)


def _ab_in_kernel(x_ref, g_ref, sc_ref, sh_ref, w_ref, cos_ref, sa_ref, sb_ref,
                  q_ref, k_ref, v_ref, u_ref, wq_s, wk_s, wv_s, wu_s):
    @pl.when(pl.program_id(0) == 0)
    def _():
        wq_s[...] = w_ref[:, 0:A_WIDTH].astype(BF16)
        wk_s[...] = w_ref[:, A_WIDTH:A_WIDTH + A_KV_WIDTH].astype(BF16)
        wv_s[...] = w_ref[:, A_WIDTH + A_KV_WIDTH:A_WIDTH + 2 * A_KV_WIDTH].astype(BF16)
        wu_s[...] = w_ref[:, A_WIDTH + 2 * A_KV_WIDTH:].astype(BF16)

    h = _norm_mod(x_ref[...], g_ref[...], sc_ref[0], sh_ref[0]).astype(BF16)
    cos, sa, sb = cos_ref[...], sa_ref[...], sb_ref[...]
    q = jnp.dot(h, wq_s[...], preferred_element_type=F32)
    q_ref[...] = _apply_rope(q, cos, sa, sb, HEAD_DIM // 4).astype(q_ref.dtype)
    k = jnp.dot(h, wk_s[...], preferred_element_type=F32)
    k_ref[...] = _apply_rope(k, cos, sa, sb, HEAD_DIM // 4)
    v_ref[...] = jnp.dot(h, wv_s[...], preferred_element_type=F32)
    u_ref[...] = jnp.dot(h, wu_s[...], preferred_element_type=F32)


def _ab_in(x, mod, g, w, layer):
    cos, sa, sb = _rope_tables(HEAD_DIM)
    d_in = w.shape[1]
    row = lambda n: pl.BlockSpec((ROW_BLK, n), lambda i: (i, 0))
    rope = pl.BlockSpec((ROW_BLK, 128), lambda i: (_rope_blk(i), 0))
    return pl.pallas_call(
        _ab_in_kernel,
        grid=(N_ROW_BLK,),
        in_specs=[row(D_MODEL), pl.BlockSpec((1, D_MODEL), lambda i: (0, 0)),
                  _mod_spec(layer, 1), _mod_spec(layer, 0),
                  pl.BlockSpec((D_MODEL, d_in), lambda i: (0, 0)), rope, rope, rope],
        out_specs=[row(A_WIDTH), row(A_KV_WIDTH), row(A_KV_WIDTH), row(S5_WIDTH)],
        out_shape=[jax.ShapeDtypeStruct((N_TOK, A_WIDTH), BF16), jax.ShapeDtypeStruct((N_TOK, A_KV_WIDTH), F32),
                   jax.ShapeDtypeStruct((N_TOK, A_KV_WIDTH), F32), jax.ShapeDtypeStruct((N_TOK, S5_WIDTH), F32)],
        scratch_shapes=[pltpu.VMEM((D_MODEL, A_WIDTH), BF16), pltpu.VMEM((D_MODEL, A_KV_WIDTH), BF16),
                        pltpu.VMEM((D_MODEL, A_KV_WIDTH), BF16), pltpu.VMEM((D_MODEL, S5_WIDTH), BF16)],
        compiler_params=_cparams("arbitrary"),
        name="ab_in",
    )(x, g.reshape(1, D_MODEL), mod, mod, w, cos, sa, sb)


def _softmax_pv(s_list, v_list, sink):
    m = functools.reduce(jnp.maximum, [jnp.max(s, axis=-1, keepdims=True) for s in s_list])
    if sink is not None:
        m = jnp.maximum(m, sink)
    ps = [jnp.exp(s - m) for s in s_list]
    l = functools.reduce(jnp.add, [jnp.sum(p, axis=-1, keepdims=True) for p in ps])
    if sink is not None:
        l = l + jnp.exp(sink - m)
    o = functools.reduce(jnp.add, [_dot(p, v) for p, v in zip(ps, v_list)])
    return o / l


def _gqa_with_sink(sink_ref, q, k_of, v_of, bias, o_ref):
    g = A_HEADS // A_KV_HEADS
    r = q.shape[0]
    outs = [None] * A_HEADS
    for kh in range(A_KV_HEADS):
        heads = range(g * kh, g * (kh + 1))
        qg = jnp.concatenate([q[:, HEAD_DIM * h:HEAD_DIM * (h + 1)] for h in heads], axis=0)
        sink = jnp.concatenate([jnp.full((r, 1), sink_ref[h], F32) for h in heads], axis=0)
        s = _dot_nt(qg, k_of(kh))
        if bias is not None:
            s = s + jnp.concatenate([bias] * g, axis=0)
        m = jnp.maximum(jnp.max(s, axis=-1, keepdims=True), sink)
        p = jnp.exp(s - m)
        l = jnp.sum(p, axis=-1, keepdims=True) + jnp.exp(sink - m)
        o = _dot(p, v_of(kh)) / l
        for n, h in enumerate(heads):
            outs[h] = o[r * n:r * (n + 1)]
    o_ref[...] = jnp.concatenate(outs, axis=1).astype(o_ref.dtype)


def _attn_ctx_kernel(sink_ref, q_ref, k_ref, v_ref, o_ref):
    scale = HEAD_DIM ** -0.5
    g = A_HEADS // A_KV_HEADS
    outs = []
    for h in range(A_HEADS):
        kh = h // g
        q = q_ref[:, HEAD_DIM * h:HEAD_DIM * (h + 1)]
        k = k_ref[:, HEAD_DIM * kh:HEAD_DIM * (kh + 1)]
        v = v_ref[:, HEAD_DIM * kh:HEAD_DIM * (kh + 1)]
        s = _dot_nt(q, k) * scale
        outs.append(_softmax_pv([s], [v], sink_ref[h]))
    o_ref[...] = jnp.concatenate(outs, axis=1).astype(o_ref.dtype)


def _attn_lat_kernel(sink_ref, q_ref, kp_ref, kc_ref, kn_ref, vp_ref, vc_ref, vn_ref, kx_ref, vx_ref, o_ref):
    n = pl.program_id(1)
    nb = DEC_SEQ // A_WINDOW
    i = lax.broadcasted_iota(jnp.int32, (A_WINDOW, A_WINDOW), 0)
    j = lax.broadcasted_iota(jnp.int32, (A_WINDOW, A_WINDOW), 1)
    zero = jnp.zeros((A_WINDOW, A_WINDOW), F32)
    bias = jnp.concatenate([jnp.where((j >= i) & (n > 0), 0.0, NEG_INF), zero,
                            jnp.where((j <= i) & (n < nb - 1), 0.0, NEG_INF),
                            jnp.zeros((A_WINDOW, PAST_LEN), F32)], axis=1)

    def rows(p_ref, c_ref, n_ref, x_ref):
        def of(kh):
            sl = slice(HEAD_DIM * kh, HEAD_DIM * (kh + 1))
            return jnp.concatenate([p_ref[:, sl], c_ref[:, sl], n_ref[:, sl], x_ref[0, :, sl]], axis=0)
        return of

    _gqa_with_sink(sink_ref, q_ref[...] * HEAD_DIM ** -0.5, rows(kp_ref, kc_ref, kn_ref, kx_ref),
                   rows(vp_ref, vc_ref, vn_ref, vx_ref), bias, o_ref)


def _attention_a(q, k, v, cache_k, cache_v, sink):
    smem = pl.BlockSpec(memory_space=pltpu.SMEM)
    o = pl.pallas_call(
        _attn_ctx_kernel,
        grid=(BATCH,),
        in_specs=[smem, pl.BlockSpec((SEQ, A_WIDTH), lambda b: (b, 0)),
                  pl.BlockSpec((SEQ, A_KV_WIDTH), lambda b: (b, 0)), pl.BlockSpec((SEQ, A_KV_WIDTH), lambda b: (b, 0))],
        out_specs=pl.BlockSpec((SEQ, A_WIDTH), lambda b: (b, 0)),
        out_shape=jax.ShapeDtypeStruct((N_CTX_TOK, A_WIDTH), BF16),
        compiler_params=_cparams("arbitrary"),
        name="attn_a_ctx",
    )(sink, q, k, v)
    nb = DEC_SEQ // A_WINDOW
    base = N_CTX_TOK // A_WINDOW
    cur = lambda b, n: (base + b * nb + n, 0)
    prev = lambda b, n: (base + b * nb + jnp.maximum(n - 1, 0), 0)
    nxt = lambda b, n: (base + b * nb + jnp.minimum(n + 1, nb - 1), 0)
    kv = lambda f: pl.BlockSpec((A_WINDOW, A_KV_WIDTH), f)
    cache = pl.BlockSpec((1, PAST_LEN, A_KV_WIDTH), lambda b, n: (b, 0, 0))
    o_lat = pl.pallas_call(
        _attn_lat_kernel,
        grid=(DEC_BATCH, nb),
        in_specs=[smem, pl.BlockSpec((A_WINDOW, A_WIDTH), cur), kv(prev), kv(cur), kv(nxt), kv(prev), kv(cur), kv(nxt),
                  cache, cache],
        out_specs=pl.BlockSpec((A_WINDOW, A_WIDTH), lambda b, n: (b * nb + n, 0)),
        out_shape=jax.ShapeDtypeStruct((N_LAT_TOK, A_WIDTH), BF16),
        compiler_params=_cparams("arbitrary", "arbitrary"),
        name="attn_a_lat",
    )(sink, q, k, k, k, v, v, v, cache_k, cache_v)
    return o, o_lat


S5_CHUNK = 16
S5_OCT = 128 // S5_GROUP
S5_NOCT = S5_GROUPS // S5_OCT
S5_K = S5_CHUNK * 128
S5_PART = S5_OCT * S5_STATE
S5_SW = 4 * S5_PART
S5_ROWS_CTX = BATCH * SEQ // S5_CHUNK
S5_ROWS_LAT = DEC_BATCH * DEC_SEQ // S5_CHUNK
S5_ROWS = S5_ROWS_CTX + S5_ROWS_LAT
S5_NB = 4
HI = lax.Precision.HIGHEST


def _s5_disc_kernel(lr_ref, li_ref, ldt_ref, ar_ref, ai_ref, zr_ref, zi_ref):
    lr, li = lr_ref[...], li_ref[...]
    dt = jnp.exp(ldt_ref[...])
    mag = jnp.exp(lr * dt)
    ar, ai = mag * jnp.cos(li * dt), mag * jnp.sin(li * dt)
    den = lr * lr + li * li
    ar_ref[...] = ar
    ai_ref[...] = ai
    zr_ref[...] = ((ar - 1.0) * lr + ai * li) / den
    zi_ref[...] = (ai * lr - (ar - 1.0) * li) / den


def _cmul(xr, xi, yr, yi):
    return xr * yr - xi * yi, xr * yi + xi * yr


def _dot_nt_hi(a, b):
    return lax.dot_general(a, b, (((1,), (1,)), ((), ())), precision=HI, preferred_element_type=F32)


def _s5_prep_kernel(ar_ref, ai_ref, zr_ref, zi_ref, btr_ref, bti_ref, ctr_ref, cti_ref,
                    m_ref, win_ref, wout_ref, a16_ref, pw_s, w_s, k_s):
    t = pl.program_id(1)
    npw = S5_CHUNK + 1
    blk = lambda j: pl.ds(pl.multiple_of(j * 128, 128), 128)

    @pl.when(t == 0)
    def _():
        kd = []
        for d in range(2):
            ar, ai = ar_ref[0, d], ai_ref[0, d]
            pr, pi = jnp.ones_like(ar), jnp.zeros_like(ar)
            bbr, bbi = _cmul(zr_ref[0, d], zi_ref[0, d], btr_ref[0, d], bti_ref[0, d])
            for j in range(npw):
                pw_s[d, 0, j:j + 1, :] = pr
                pw_s[d, 1, j:j + 1, :] = pi
                if j < S5_CHUNK:
                    wr, wi = _cmul(pr, pi, bbr, bbi)
                    w_s[d, 0, 128 * j:128 * (j + 1), :] = wr.astype(BF16)
                    w_s[d, 1, 128 * j:128 * (j + 1), :] = wi.astype(BF16)
                pr, pi = _cmul(pr, pi, ar, ai)
            kd.append(_dot_nt(w_s[d, 0], ctr_ref[0, d]) - _dot_nt(w_s[d, 1], cti_ref[0, d]))
        for jj in range(2 * S5_CHUNK - 1):
            j = jj - (S5_CHUNK - 1)
            if j > 0:
                k = kd[0][128 * j:128 * (j + 1)]
            elif j < 0:
                k = kd[1][128 * -j:128 * (1 - j)]
            else:
                k = kd[0][0:128] + kd[1][0:128]
            k_s[128 * jj:128 * (jj + 1), :] = k
        a16_ref[0] = jnp.concatenate([pw_s[0, 0, S5_CHUNK:npw, :], pw_s[1, 0, S5_CHUNK:npw, :],
                                      pw_s[0, 1, S5_CHUNK:npw, :], pw_s[1, 1, S5_CHUNK:npw, :]], axis=1)

    for tp in range(S5_CHUNK):
        m_ref[0, :, 128 * tp:128 * (tp + 1)] = k_s[blk(S5_CHUNK - 1 + tp - t), :].astype(BF16)

    def power(d, e):
        return pw_s[d, 0, pl.ds(e, 1), :], pw_s[d, 1, pl.ds(e, 1), :]

    for d in range(2):
        j = (S5_CHUNK - 1 - t) if d == 0 else t
        win_ref[0, :, S5_PART * d:S5_PART * (d + 1)] = w_s[d, 0, blk(j), :]
        win_ref[0, :, S5_PART * (2 + d):S5_PART * (3 + d)] = w_s[d, 1, blk(j), :]
        er, ei = _cmul(*power(d, (t + 1) if d == 0 else (S5_CHUNK - t)), ctr_ref[0, d], cti_ref[0, d])
        wout_ref[0, :, S5_PART * d:S5_PART * (d + 1)] = er.astype(BF16)
        wout_ref[0, :, S5_PART * (2 + d):S5_PART * (3 + d)] = (-ei).astype(BF16)


def _s5_main_kernel(u_ref, win_ref, m_ref, wout_ref, a16_ref, h0_ref, y_ref, hfin_ref, uo_s, x_s, hs_s):
    s = pl.program_id(1)
    nq = S5_PART // 128

    @pl.when(s == 0)
    def _():
        for t in range(S5_CHUNK):
            uo_s[:, 128 * t:128 * (t + 1)] = u_ref[pl.ds(t, S5_ROWS, stride=S5_CHUNK), :].astype(BF16)

    @pl.when(s < S5_NB)
    def _():
        x = jnp.dot(uo_s[...], win_ref[0], preferred_element_type=F32)
        for q in range(nq):
            x_s[s * nq + q] = x[:, 128 * q:128 * (q + 1)]

    @pl.when(s == S5_NB - 1)
    def _():
        def run(row0, nb, nc, h):
            for c in range(nc):
                for d in range(2):
                    cc = c if d == 0 else nc - 1 - c
                    rows = pl.ds(row0 + cc, nb, stride=nc)
                    for q in range(nq):
                        kr, ki = d * nq + q, (2 + d) * nq + q
                        hr, hi = h[d][0][q], h[d][1][q]
                        hs_s[kr, rows, :] = hr
                        hs_s[ki, rows, :] = hi
                        ar = a16_ref[0, :, 128 * kr:128 * (kr + 1)]
                        ai = a16_ref[0, :, 128 * ki:128 * (ki + 1)]
                        h[d][0][q] = ar * hr - ai * hi + x_s[kr, rows, :]
                        h[d][1][q] = ar * hi + ai * hr + x_s[ki, rows, :]
            return h

        zero = jnp.zeros((BATCH, 128), F32)
        fin = run(0, BATCH, SEQ // S5_CHUNK, [[[zero] * nq, [zero] * nq] for _ in range(2)])
        for d in range(2):
            for ri in range(2):
                for q in range(nq):
                    k = (2 * ri + d) * nq + q
                    hfin_ref[0, :, 128 * k:128 * (k + 1)] = fin[d][ri][q]
        h0 = [[[h0_ref[0, :, 128 * ((2 * ri + d) * nq + q):128 * ((2 * ri + d) * nq + q + 1)] for q in range(nq)]
               for ri in range(2)] for d in range(2)]
        run(S5_ROWS_CTX, DEC_BATCH, DEC_SEQ // S5_CHUNK, h0)

    @pl.when(s >= S5_NB)
    def _():
        hs = jnp.concatenate([hs_s[k] for k in range(4 * nq)], axis=1).astype(BF16)
        y = (jnp.dot(uo_s[...], m_ref[0], preferred_element_type=F32)
             + lax.dot_general(hs, wout_ref[0], (((1,), (1,)), ((), ())), preferred_element_type=F32))
        for q in range(nq):
            t = (s - S5_NB) * nq + q
            y_ref[pl.ds(t, S5_ROWS, stride=S5_CHUNK), :] = y[:, 128 * q:128 * (q + 1)]


def _s5_octets(t, lanes):
    return t.reshape(2, S5_NOCT, 1, S5_OCT * lanes).transpose(1, 0, 2, 3)


def _s5_blockdiag(t):
    a, n = t.shape[2], t.shape[3]
    t = t.reshape(2, S5_NOCT, S5_OCT, a, n)
    bd = jnp.einsum('dogan,gh->dogahn', t, jnp.eye(S5_OCT, dtype=t.dtype))
    return bd.reshape(2, S5_NOCT, S5_OCT * a, S5_OCT * n).transpose(1, 0, 2, 3)


def _s5_state_to_lanes(h_re, h_im):
    b = h_re.shape[0]
    parts = jnp.stack([h_re[:, 0], h_re[:, 1], h_im[:, 0], h_im[:, 1]], axis=1)
    parts = parts.reshape(b, 4, S5_NOCT, S5_PART).transpose(2, 0, 1, 3)
    return parts.reshape(S5_NOCT, b, S5_SW)


def _s5_state_from_lanes(h):
    b = h.shape[1]
    parts = h.reshape(S5_NOCT, b, 4, S5_OCT, S5_STATE).transpose(1, 2, 0, 3, 4).reshape(b, 4, S5_GROUPS, S5_STATE)
    return parts[:, 0:2], parts[:, 2:4]


def _s5_scan(u, h0_re, h0_im, lam_re, lam_im, log_dt, b_re, b_im, c_re, c_im):
    ng, n = S5_GROUPS, S5_STATE
    rows = 2 * ng
    disc = pl.pallas_call(
        _s5_disc_kernel,
        out_shape=[jax.ShapeDtypeStruct((rows, n), F32)] * 4,
        name="s5_disc",
    )(lam_re.reshape(rows, n), lam_im.reshape(rows, n), log_dt.reshape(rows, 1))
    ar, ai, zr, zi = [_s5_octets(t.reshape(2, ng, n), n) for t in disc]
    bt = lambda t: _s5_blockdiag(t.transpose(0, 1, 3, 2))
    vec = pl.BlockSpec((1, 2, 1, S5_PART), lambda o, t: (o, 0, 0, 0))
    mat = pl.BlockSpec((1, 2, 128, S5_PART), lambda o, t: (o, 0, 0, 0))
    rowblk = lambda w: pl.BlockSpec((1, 128, w), lambda o, t: (o, t, 0))
    m, win, wout, a16 = pl.pallas_call(
        _s5_prep_kernel,
        grid=(S5_NOCT, S5_CHUNK),
        in_specs=[vec, vec, vec, vec, mat, mat, mat, mat],
        out_specs=[rowblk(S5_K), rowblk(S5_SW), rowblk(S5_SW), pl.BlockSpec((1, 1, S5_SW), lambda o, t: (o, 0, 0))],
        out_shape=[jax.ShapeDtypeStruct((S5_NOCT, S5_K, S5_K), BF16), jax.ShapeDtypeStruct((S5_NOCT, S5_K, S5_SW), BF16),
                   jax.ShapeDtypeStruct((S5_NOCT, S5_K, S5_SW), BF16), jax.ShapeDtypeStruct((S5_NOCT, 1, S5_SW), F32)],
        scratch_shapes=[pltpu.VMEM((2, 2, 24, S5_PART), F32), pltpu.VMEM((2, 2, S5_K, S5_PART), BF16),
                        pltpu.VMEM(((2 * S5_CHUNK - 1) * 128, 128), F32)],
        compiler_params=_cparams("arbitrary", "arbitrary"),
        name="s5_prep",
    )(ar, ai, zr, zi, bt(b_re), bt(b_im), _s5_blockdiag(c_re), _s5_blockdiag(c_im))
    nb = S5_NB
    y, hfin = pl.pallas_call(
        _s5_main_kernel,
        grid=(S5_NOCT, 2 * nb),
        in_specs=[pl.BlockSpec((N_TOK, 128), lambda o, s: (0, o)),
                  pl.BlockSpec((1, S5_K, S5_PART), lambda o, s: (o, 0, jnp.minimum(s, nb - 1))),
                  pl.BlockSpec((1, S5_K, S5_PART), lambda o, s: (o, 0, jnp.maximum(s - nb, 0))),
                  pl.BlockSpec((1, S5_PART, S5_SW), lambda o, s: (o, jnp.maximum(s - nb, 0), 0)),
                  pl.BlockSpec((1, 1, S5_SW), lambda o, s: (o, 0, 0)),
                  pl.BlockSpec((1, DEC_BATCH, S5_SW), lambda o, s: (o, 0, 0))],
        out_specs=[pl.BlockSpec((N_TOK, 128), lambda o, s: (0, o)),
                   pl.BlockSpec((1, BATCH, S5_SW), lambda o, s: (o, 0, 0))],
        out_shape=[jax.ShapeDtypeStruct((N_TOK, S5_WIDTH), F32), jax.ShapeDtypeStruct((S5_NOCT, BATCH, S5_SW), F32)],
        scratch_shapes=[pltpu.VMEM((S5_ROWS, S5_K), BF16), pltpu.VMEM((S5_SW // 128, S5_ROWS, 128), F32),
                        pltpu.VMEM((S5_SW // 128, S5_ROWS, 128), F32)],
        compiler_params=_cparams("arbitrary", "arbitrary"),
        name="s5_main",
    )(u, win, m, wout, a16, _s5_state_to_lanes(h0_re, h0_im))
    fin_re, fin_im = _s5_state_from_lanes(hfin)
    return y, fin_re, fin_im


def _ctx_or_lat(ctx_ref, lat_ref):
    return jnp.where(pl.program_id(0) < N_CTX_BLK, ctx_ref[...], lat_ref[...])


def _split_specs(width):
    return [pl.BlockSpec((ROW_BLK, width), lambda i: (jnp.minimum(i, N_CTX_BLK - 1), 0)),
            pl.BlockSpec((ROW_BLK, width), lambda i: (jnp.maximum(i - N_CTX_BLK, 0), 0))]


def _ab_out_kernel(x_ref, oac_ref, oal_ref, y_ref, u_ref, d_ref, wglu_ref, wout_ref, gt_ref, o_ref, wglu_s, wa_s, wb_s):
    @pl.when(pl.program_id(0) == 0)
    def _():
        wglu_s[...] = wglu_ref[...].astype(BF16)
        wa_s[...] = wout_ref[0:A_WIDTH, :].astype(BF16)
        wb_s[...] = wout_ref[A_WIDTH:, :].astype(BF16)

    g = jax.nn.gelu(y_ref[...] + d_ref[...] * u_ref[...])
    ob = g * jax.nn.sigmoid(jnp.dot(g.astype(BF16), wglu_s[...], preferred_element_type=F32))
    out = (jnp.dot(_ctx_or_lat(oac_ref, oal_ref).astype(BF16), wa_s[...], preferred_element_type=F32)
           + jnp.dot(ob.astype(BF16), wb_s[...], preferred_element_type=F32))
    o_ref[...] = x_ref[...] + gt_ref[0] * out


def _ab_out(x, oa, y, u, d_skip, w_glu, w_out, mod, layer):
    row = lambda n: pl.BlockSpec((ROW_BLK, n), lambda i: (i, 0))
    full = lambda a, b: pl.BlockSpec((a, b), lambda i: (0, 0))
    return pl.pallas_call(
        _ab_out_kernel,
        grid=(N_ROW_BLK,),
        in_specs=[row(D_MODEL)] + _split_specs(A_WIDTH) + [row(S5_WIDTH), row(S5_WIDTH), full(1, S5_WIDTH),
                  full(S5_WIDTH, S5_WIDTH), full(A_WIDTH + S5_WIDTH, D_MODEL), _mod_spec(layer, 2)],
        out_specs=row(D_MODEL),
        out_shape=jax.ShapeDtypeStruct((N_TOK, D_MODEL), F32),
        scratch_shapes=[pltpu.VMEM((S5_WIDTH, S5_WIDTH), BF16), pltpu.VMEM((A_WIDTH, D_MODEL), BF16),
                        pltpu.VMEM((S5_WIDTH, D_MODEL), BF16)],
        compiler_params=_cparams("arbitrary"),
        name="ab_out",
    )(x, oa[0], oa[1], y, u, d_skip.reshape(1, S5_WIDTH), w_glu, w_out, mod)


CD_HG0 = MLA_Q_RANK + MLA_KV_RANK + MLA_ROPE
KPE_LANES = 128


def _cd_in_kernel(x_ref, g_ref, sc_ref, sh_ref, w_ref, gq_ref, wqu_ref, gkv_ref, wkvu_ref, cos_ref, sa_ref, sb_ref,
                  qn_ref, qr_ref, ckv_ref, kpe_ref, kn_ref, vm_ref, hq_ref, hff_ref, hfb_ref, hi_ref, hg_ref,
                  wcq_s, wckv_s, wkpe_s, whg_s, wqu_s, wkvu_s):
    @pl.when(pl.program_id(0) == 0)
    def _():
        wcq_s[...] = w_ref[:, 0:MLA_Q_RANK].astype(BF16)
        wckv_s[...] = w_ref[:, MLA_Q_RANK:MLA_Q_RANK + MLA_KV_RANK].astype(BF16)
        kp = w_ref[:, MLA_Q_RANK + MLA_KV_RANK:CD_HG0].astype(BF16)
        wkpe_s[...] = jnp.concatenate([kp] * (KPE_LANES // MLA_ROPE), axis=1)
        whg_s[...] = w_ref[:, CD_HG0:].astype(BF16)
        wqu_s[...] = wqu_ref[...].astype(BF16)
        wkvu_s[...] = wkvu_ref[...].astype(BF16)

    h = _norm_mod(x_ref[...], g_ref[...], sc_ref[0], sh_ref[0]).astype(BF16)
    cos, sa, sb = cos_ref[...], sa_ref[...], sb_ref[...]
    nope = MLA_HEADS * MLA_NOPE
    cq = _rms(jnp.dot(h, wcq_s[...], preferred_element_type=F32)) * gq_ref[...]
    qq = jnp.dot(cq.astype(BF16), wqu_s[...], preferred_element_type=F32)
    qn_ref[...] = qq[:, 0:nope].astype(qn_ref.dtype)
    qr_ref[...] = _apply_rope(qq[:, nope:], cos, sa, sb, MLA_ROPE // 4).astype(qr_ref.dtype)
    ckv = _rms(jnp.dot(h, wckv_s[...], preferred_element_type=F32)) * gkv_ref[...]
    ckv_ref[...] = ckv
    kv = jnp.dot(ckv.astype(BF16), wkvu_s[...], preferred_element_type=F32)
    kn_ref[...] = kv[:, 0:nope].astype(kn_ref.dtype)
    vm_ref[...] = kv[:, nope:].astype(vm_ref.dtype)
    kpe_ref[...] = _apply_rope(jnp.dot(h, wkpe_s[...], preferred_element_type=F32), cos, sa, sb, MLA_ROPE // 4)
    hh = jnp.dot(h, whg_s[...], preferred_element_type=F32)
    for n, ref in enumerate((hq_ref, hff_ref, hfb_ref, hi_ref, hg_ref)):
        ref[...] = hh[:, HG_KD * n:HG_KD * (n + 1)]


def _mla_split_heads(w, a):
    k, n = w.shape
    w = w.reshape(k, MLA_HEADS, n // MLA_HEADS)
    return jnp.concatenate([w[:, :, :a].reshape(k, -1), w[:, :, a:].reshape(k, -1)], axis=1)


def _cd_in(x, mod, g, w, g_q, w_q_up, g_kv, w_kv_up, layer):
    cos, sa, sb = _rope_tables(MLA_ROPE)
    d_in = w.shape[1]
    row = lambda n: pl.BlockSpec((ROW_BLK, n), lambda i: (i, 0))
    full = lambda a, b: pl.BlockSpec((a, b), lambda i: (0, 0))
    rope = pl.BlockSpec((ROW_BLK, 128), lambda i: (_rope_blk(i), 0))
    widths = [MLA_HEADS * MLA_NOPE, MLA_HEADS * MLA_ROPE, MLA_KV_RANK, KPE_LANES, MLA_HEADS * MLA_NOPE, MLA_WIDTH] + [HG_KD] * 5
    nq = MLA_HEADS * (MLA_NOPE + MLA_ROPE)
    nkv = MLA_HEADS * (MLA_NOPE + MLA_V)
    return pl.pallas_call(
        _cd_in_kernel,
        grid=(N_ROW_BLK,),
        in_specs=[row(D_MODEL), full(1, D_MODEL), _mod_spec(layer, 1), _mod_spec(layer, 0), full(D_MODEL, d_in),
                  full(1, MLA_Q_RANK), full(MLA_Q_RANK, nq), full(1, MLA_KV_RANK), full(MLA_KV_RANK, nkv), rope, rope, rope],
        out_specs=[row(n) for n in widths],
        out_shape=[jax.ShapeDtypeStruct((N_TOK, n), BF16 if i in (0, 1, 4, 5) else F32) for i, n in enumerate(widths)],
        scratch_shapes=[pltpu.VMEM((D_MODEL, MLA_Q_RANK), BF16), pltpu.VMEM((D_MODEL, MLA_KV_RANK), BF16),
                        pltpu.VMEM((D_MODEL, KPE_LANES), BF16), pltpu.VMEM((D_MODEL, 5 * HG_KD), BF16),
                        pltpu.VMEM((MLA_Q_RANK, nq), BF16), pltpu.VMEM((MLA_KV_RANK, nkv), BF16)],
        compiler_params=_cparams("arbitrary"),
        name="cd_in",
    )(x, g.reshape(1, D_MODEL), mod, mod, w, g_q.reshape(1, -1), _mla_split_heads(w_q_up, MLA_NOPE),
      g_kv.reshape(1, -1), _mla_split_heads(w_kv_up, MLA_NOPE), cos, sa, sb)


def _mm_kernel(a_ref, w_ref, *o_refs):
    r = _dot(a_ref[...], w_ref[...])
    off = 0
    for o in o_refs:
        o[...] = r[:, off:off + o.shape[1]]
        off += o.shape[1]


def _mla_cache_kv(cckv, w_kv_up):
    n = MLA_HEADS * MLA_NOPE
    rows = cckv.shape[0]
    return pl.pallas_call(
        _mm_kernel,
        grid=(rows // PAST_LEN,),
        in_specs=[pl.BlockSpec((PAST_LEN, MLA_KV_RANK), lambda i: (i, 0)),
                  pl.BlockSpec((MLA_KV_RANK, 2 * n), lambda i: (0, 0))],
        out_specs=[pl.BlockSpec((PAST_LEN, n), lambda i: (i, 0))] * 2,
        out_shape=[jax.ShapeDtypeStruct((rows, n), F32)] * 2,
        compiler_params=_cparams("arbitrary"),
        name="mla_cache_kv",
    )(cckv, _mla_split_heads(w_kv_up, MLA_NOPE))


def _mla_heads(qn, qr, keys, o_ref):
    scale = (MLA_NOPE + MLA_ROPE) ** -0.5
    outs = []
    for h in range(MLA_HEADS):
        a = slice(MLA_NOPE * h, MLA_NOPE * (h + 1))
        r = slice(MLA_ROPE * h, MLA_ROPE * (h + 1))
        s_list = [(_dot_nt(qn[:, a], kn[:, a]) + _dot_nt(qr[:, r], kp[:, 0:MLA_ROPE])) * scale for kn, kp, _ in keys]
        outs.append(_softmax_pv(s_list, [v[:, a] for _, _, v in keys], None))
    o_ref[...] = jnp.concatenate(outs, axis=1).astype(o_ref.dtype)


def _mla_ctx_kernel(qn_ref, qr_ref, kn_ref, kp_ref, v_ref, o_ref):
    _mla_heads(qn_ref[...], qr_ref[...], [(kn_ref[...], kp_ref[...], v_ref[...])], o_ref)


def _mla_lat_kernel(qn_ref, qr_ref, kn_ref, kp_ref, v_ref, kcn_ref, kcp_ref, vc_ref, o_ref):
    _mla_heads(qn_ref[...], qr_ref[...],
               [(kcn_ref[...], kcp_ref[0], vc_ref[...]), (kn_ref[...], kp_ref[...], v_ref[...])], o_ref)


def _attention_mla(qn, qr, kn, kpe, vm, kcn, kcpe, vc):
    n = MLA_HEADS * MLA_NOPE
    nr = MLA_HEADS * MLA_ROPE
    blk = lambda w: pl.BlockSpec((SEQ, w), lambda b: (b, 0))
    o = pl.pallas_call(
        _mla_ctx_kernel,
        grid=(BATCH,),
        in_specs=[blk(n), blk(nr), blk(n), blk(KPE_LANES), blk(MLA_WIDTH)],
        out_specs=blk(MLA_WIDTH),
        out_shape=jax.ShapeDtypeStruct((N_CTX_TOK, MLA_WIDTH), BF16),
        compiler_params=_cparams("arbitrary"),
        name="mla_ctx",
    )(qn, qr, kn, kpe, vm)
    nq = DEC_SEQ // ROW_BLK
    qblk = lambda w: pl.BlockSpec((ROW_BLK, w), lambda b, i: (N_CTX_BLK + b * nq + i, 0))
    seq = lambda w: pl.BlockSpec((DEC_SEQ, w), lambda b, i: (N_CTX_TOK // DEC_SEQ + b, 0))
    past = lambda w: pl.BlockSpec((PAST_LEN, w), lambda b, i: (b, 0))
    o_lat = pl.pallas_call(
        _mla_lat_kernel,
        grid=(DEC_BATCH, nq),
        in_specs=[qblk(n), qblk(nr), seq(n), seq(KPE_LANES), seq(MLA_WIDTH), past(n),
                  pl.BlockSpec((1, PAST_LEN, MLA_ROPE), lambda b, i: (b, 0, 0)), past(MLA_WIDTH)],
        out_specs=pl.BlockSpec((ROW_BLK, MLA_WIDTH), lambda b, i: (b * nq + i, 0)),
        out_shape=jax.ShapeDtypeStruct((N_LAT_TOK, MLA_WIDTH), BF16),
        compiler_params=_cparams("arbitrary", "arbitrary"),
        name="mla_lat",
    )(qn, qr, kn, kpe, vm, kcn, kcpe, vc)
    return o, o_lat


HG_TILE = 128
HG_NC = HG_TILE // HG_CHUNK
HG_HALF = 256


HG_SLABS = HG_KD // 128


def _hg_token_plane(raw_s, p):
    return jnp.concatenate([raw_s[j, pl.ds(p, HG_NC, stride=HG_CHUNK), :] for j in range(HG_SLABS)], axis=1)


def _hg_put(ref, x):
    for j in range(HG_SLABS):
        ref[j] = x[:, 128 * j:128 * (j + 1)]


def _hg_get(ref, c, l):
    rows = pl.ds(c, HG_CHUNK, stride=HG_NC)
    return jnp.concatenate([ref[2 * l, rows, :], ref[2 * l + 1, rows, :]], axis=1)


def _hg_plane(ref, p):
    return jnp.concatenate([ref[j, HG_NC * p:HG_NC * (p + 1), :] for j in range(HG_SLABS)], axis=1)


def _hg_put_plane(ref, p, x):
    for j in range(HG_SLABS):
        ref[j, HG_NC * p:HG_NC * (p + 1), :] = x[:, 128 * j:128 * (j + 1)]


HG_PAIRS = HG_CHUNK * (HG_CHUNK + 1) // 2
HG_PAIR_ROWS = HG_PAIRS * HG_NC


def _hg_pair_rows(d, i):
    n = d * HG_CHUNK - d * (d - 1) // 2 + (i - d)
    return slice(HG_NC * n, HG_NC * (n + 1))


def _hg_direction(hq_ref, hf_ref, hi_ref, o_ref, lb, st_ref, raw_s, q_s, f_s, kk_s, v_s, qt_s, kt_s, o_s, p_s, e_s,
                  sign, ones_bd, head_mask):
    _hg_put(raw_s, hq_ref[...])
    for p in range(HG_CHUNK):
        hq = _hg_token_plane(raw_s, p)
        _hg_put_plane(q_s, p, hq * jax.nn.sigmoid(hq))
    _hg_put(raw_s, hf_ref[...])
    for p in range(HG_CHUNK):
        f = lb + (1.0 - lb) * jax.nn.sigmoid(_hg_token_plane(raw_s, p))
        _hg_put_plane(f_s, p, f)
        _hg_put_plane(kk_s, p, 1.0 - f)
    _hg_put(raw_s, hi_ref[...])
    for p in range(HG_CHUNK):
        _hg_put_plane(v_s, p, _hg_token_plane(raw_s, p))
    pos = (lambda i: i) if sign > 0 else (lambda i: HG_CHUNK - 1 - i)
    plane = lambda ref, i: _hg_plane(ref, pos(i))

    for i in range(HG_CHUNK):
        q = plane(q_s, i)
        p_s[_hg_pair_rows(0, i), :] = q * plane(kk_s, i)
        dec = None
        for d in range(1, i + 1):
            fd = plane(f_s, i - d + 1)
            dec = fd if dec is None else dec * fd
            p_s[_hg_pair_rows(d, i), :] = q * plane(kk_s, i - d) * dec
    step = HG_PAIR_ROWS // 4
    for c in range(0, HG_PAIR_ROWS, step):
        pb = p_s[c:c + step, :].astype(BF16)
        e_s[c:c + step, 0:HG_HALF] = jnp.dot(pb[:, 0:HG_HALF], ones_bd, preferred_element_type=F32)
        e_s[c:c + step, HG_HALF:] = jnp.dot(pb[:, HG_HALF:], ones_bd, preferred_element_type=F32)
    for i in range(HG_CHUNK):
        o = None
        for d in range(i + 1):
            t = e_s[_hg_pair_rows(d, i), :] * plane(v_s, i - d)
            o = t if o is None else o + t
        _hg_put_plane(o_s, pos(i), o)

    incl = None
    for i in range(HG_CHUNK):
        fi = plane(f_s, i)
        incl = fi if incl is None else incl * fi
        _hg_put_plane(qt_s, pos(i), plane(q_s, i) * incl)
    whole = incl
    excl = None
    for i in range(HG_CHUNK - 1, -1, -1):
        kt = plane(kk_s, i)
        if excl is not None:
            kt = kt * excl
        _hg_put_plane(kt_s, pos(i), kt)
        fi = plane(f_s, i)
        excl = fi if excl is None else excl * fi

    tile4 = lambda x: jnp.where(head_mask, jnp.concatenate([x] * 4, axis=0), 0.0).astype(BF16)
    for cc in range(HG_NC):
        c = cc if sign > 0 else HG_NC - 1 - cc
        r = slice(HG_CHUNK * c, HG_CHUNK * (c + 1))
        for g in range(2):
            l = slice(HG_HALF * g, HG_HALF * (g + 1))
            st = st_ref[g]
            oc = _dot_nt(tile4(_hg_get(qt_s, c, g)), st)
            o_ref[r, l] = _hg_get(o_s, c, g) + jnp.concatenate(
                [oc[HG_CHUNK * h:HG_CHUNK * (h + 1), :] for h in range(4)], axis=1)
            v = hi_ref[r, l]
            vs = jnp.concatenate([v[:, HG_VAL * h:HG_VAL * (h + 1)] for h in range(4)], axis=0)
            ds = lax.dot_general(vs.astype(BF16), tile4(_hg_get(kt_s, c, g)), (((0,), (0,)), ((), ())),
                                 preferred_element_type=F32)
            st_ref[g] = st * whole[c:c + 1, l] + ds


def _hg_lower_bound(lb_ref, d, layer):
    raw = [lb_ref[d * DEPTH + m:d * DEPTH + m + 1, :] for m in range(DEPTH)]
    mx = functools.reduce(jnp.maximum, raw)
    e = [jnp.exp(r - mx) for r in raw]
    tot = functools.reduce(jnp.add, e)
    return functools.reduce(jnp.add, e[1:layer + 1], jnp.zeros_like(tot)) / tot


def _hgrn_kernel(nt, layer, hqf_ref, hff_ref, hif_ref, hqb_ref, hfb_ref, hib_ref, lb_ref, s0_ref,
                 of_ref, ob_ref, sfin_ref, st_s, raw_s, q_s, f_s, kk_s, v_s, qt_s, kt_s, o_s, p_s, e_s):
    i = pl.program_id(1)

    @pl.when(i == 0)
    def _():
        st_s[...] = s0_ref[0]

    ri = lax.broadcasted_iota(jnp.int32, (HG_HALF, HG_HALF), 0) // HG_KEY
    ci = lax.broadcasted_iota(jnp.int32, (HG_HALF, HG_HALF), 1) // HG_KEY
    ones_bd = jnp.where(ri == ci, 1.0, 0.0).astype(BF16)
    head_mask = (lax.broadcasted_iota(jnp.int32, (4 * HG_CHUNK, HG_HALF), 0) // HG_CHUNK
                 == lax.broadcasted_iota(jnp.int32, (4 * HG_CHUNK, HG_HALF), 1) // HG_KEY)
    _hg_direction(hqf_ref, hff_ref, hif_ref, of_ref, _hg_lower_bound(lb_ref, 0, layer), st_s.at[0],
                  raw_s, q_s, f_s, kk_s, v_s, qt_s, kt_s, o_s, p_s, e_s, 1, ones_bd, head_mask)
    _hg_direction(hqb_ref, hfb_ref, hib_ref, ob_ref, _hg_lower_bound(lb_ref, 1, layer), st_s.at[1],
                  raw_s, q_s, f_s, kk_s, v_s, qt_s, kt_s, o_s, p_s, e_s, -1, ones_bd, head_mask)

    @pl.when(i == nt - 1)
    def _():
        sfin_ref[0] = st_s[...]


def _hg_state_to_blocks(s):
    b = s.shape[0]
    st = s.reshape(b, 2, 2, 4, HG_KEY, HG_VAL).transpose(0, 1, 2, 5, 3, 4)
    return st.reshape(b, 2, 2, HG_VAL, HG_HALF)


def _hg_state_from_blocks(st):
    b = st.shape[0]
    st = st.reshape(b, 2, 2, HG_VAL, 4, HG_KEY).transpose(0, 1, 2, 4, 5, 3)
    return st.reshape(b, 2, HG_HEADS, HG_KEY, HG_VAL)


def _hgrn_scan(hq, hff, hfb, hi, lb, s0, layer, row0, nseq, seqlen):
    nt = seqlen // HG_TILE
    base = row0 // HG_TILE
    fwd = pl.BlockSpec((HG_TILE, HG_KD), lambda b, i: (base + b * nt + i, 0))
    bwd = pl.BlockSpec((HG_TILE, HG_KD), lambda b, i: (base + b * nt + nt - 1 - i, 0))
    ofw = pl.BlockSpec((HG_TILE, HG_KD), lambda b, i: (b * nt + i, 0))
    obw = pl.BlockSpec((HG_TILE, HG_KD), lambda b, i: (b * nt + nt - 1 - i, 0))
    st = pl.BlockSpec((1, 2, 2, HG_VAL, HG_HALF), lambda b, i: (b, 0, 0, 0, 0))
    tile = lambda: pltpu.VMEM((HG_SLABS, HG_TILE, 128), F32)
    pairs = lambda: pltpu.VMEM((HG_PAIR_ROWS, HG_KD), F32)
    return pl.pallas_call(
        functools.partial(_hgrn_kernel, nt, layer),
        grid=(nseq, nt),
        in_specs=[fwd, fwd, fwd, bwd, bwd, bwd, pl.BlockSpec((2 * DEPTH, HG_KD), lambda b, i: (0, 0)), st],
        out_specs=[ofw, obw, st],
        out_shape=[jax.ShapeDtypeStruct((nseq * seqlen, HG_WIDTH), F32)] * 2
        + [jax.ShapeDtypeStruct((nseq, 2, 2, HG_VAL, HG_HALF), F32)],
        scratch_shapes=[pltpu.VMEM((2, 2, HG_VAL, HG_HALF), F32)] + [tile() for _ in range(8)] + [pairs(), pairs()],
        compiler_params=_cparams("arbitrary", "arbitrary"),
        name="hgrn_scan",
    )(hq, hff, hi, hq, hfb, hi, lb.reshape(2 * DEPTH, HG_KD), s0)


def _cd_out_kernel(x_ref, occ_ref, ocl_ref, ofc_ref, ofl_ref, obc_ref, obl_ref, hg_ref, go_ref, wout_ref, gt_ref,
                   o_ref, wa_s, wb_s):
    @pl.when(pl.program_id(0) == 0)
    def _():
        wa_s[...] = wout_ref[0:MLA_WIDTH, :].astype(BF16)
        wb_s[...] = wout_ref[MLA_WIDTH:, :].astype(BF16)

    ri = lax.broadcasted_iota(jnp.int32, (HG_HALF, HG_HALF), 0) // HG_VAL
    ci = lax.broadcasted_iota(jnp.int32, (HG_HALF, HG_HALF), 1) // HG_VAL
    ones_bd = jnp.where(ri == ci, 1.0, 0.0).astype(BF16)
    o = _ctx_or_lat(ofc_ref, ofl_ref) + _ctx_or_lat(obc_ref, obl_ref)
    sq = o * o
    hi = sq.astype(BF16)
    lo = (sq - hi.astype(F32)).astype(BF16)
    ms = jnp.concatenate(
        [jnp.dot(hi[:, l], ones_bd, preferred_element_type=F32) + jnp.dot(lo[:, l], ones_bd, preferred_element_type=F32)
         for l in (slice(0, HG_HALF), slice(HG_HALF, 2 * HG_HALF))], axis=1) * (1.0 / HG_VAL)
    hg = hg_ref[...]
    od = o * lax.rsqrt(ms + EPS) * go_ref[...] * (hg * jax.nn.sigmoid(hg))
    out = (jnp.dot(_ctx_or_lat(occ_ref, ocl_ref).astype(BF16), wa_s[...], preferred_element_type=F32)
           + jnp.dot(od.astype(BF16), wb_s[...], preferred_element_type=F32))
    o_ref[...] = x_ref[...] + gt_ref[0] * out


def _cd_out(x, oc, of, ob, hg, g_o, w_out, mod, layer):
    row = lambda n: pl.BlockSpec((ROW_BLK, n), lambda i: (i, 0))
    full = lambda a, b: pl.BlockSpec((a, b), lambda i: (0, 0))
    return pl.pallas_call(
        _cd_out_kernel,
        grid=(N_ROW_BLK,),
        in_specs=[row(D_MODEL)] + _split_specs(MLA_WIDTH) + _split_specs(HG_WIDTH) + _split_specs(HG_WIDTH)
        + [row(HG_WIDTH), full(1, HG_WIDTH), full(MLA_WIDTH + HG_WIDTH, D_MODEL), _mod_spec(layer, 2)],
        out_specs=row(D_MODEL),
        out_shape=jax.ShapeDtypeStruct((N_TOK, D_MODEL), F32),
        scratch_shapes=[pltpu.VMEM((MLA_WIDTH, D_MODEL), BF16), pltpu.VMEM((HG_WIDTH, D_MODEL), BF16)],
        compiler_params=_cparams("arbitrary"),
        name="cd_out",
    )(x, oc[0], oc[1], of[0], of[1], ob[0], ob[1], hg, jnp.tile(g_o, HG_HEADS).reshape(1, HG_WIDTH), w_out, mod)


N_PAIRS = 6
N_CLASSES = N_GROUPS * N_PAIRS
CLS_ROWS = 32
MOE_BM = 256
MOE_NBLK = N_TOK // MOE_BM + N_CLASSES
MOE_ROWS = MOE_NBLK * MOE_BM
PAIR_SLOTS = ((0, 1), (3, 1), (2, 1), (2, 0), (3, 0), (3, 2))


def _moe_route_kernel(x_ref, g_ref, sc_ref, sh_ref, rw_ref, rb_ref, h_ref, ti_ref, tw_ref, cnt_ref, base_s):
    @pl.when(pl.program_id(0) == 0)
    def _():
        base_s[...] = jnp.zeros_like(base_s)

    h = _norm_mod(x_ref[...], g_ref[...], sc_ref[0], sh_ref[0])
    logits = lax.dot_general(rw_ref[...], h, (((1,), (1,)), ((), ())), precision=HI,
                             preferred_element_type=F32)
    aff = jax.nn.sigmoid(logits)
    sel = aff + rb_ref[...]
    s = [sel[e:e + 1, :] for e in range(N_EXPERTS)]
    a = [aff[e:e + 1, :] for e in range(N_EXPERTS)]
    gs = []
    for g in range(N_GROUPS):
        m = s[4 * g:4 * g + 4]
        pairs = [m[i] + m[j] for i in range(4) for j in range(i + 1, 4)]
        gs.append(functools.reduce(jnp.maximum, pairs))
    gmax = functools.reduce(jnp.maximum, gs)
    taken = jnp.zeros_like(gmax) > 1.0
    gsel = []
    for g in range(N_GROUPS):
        hit = (gs[g] == gmax) & jnp.logical_not(taken)
        gsel.append(hit)
        taken = taken | hit
    e_lo = jnp.zeros(gmax.shape, jnp.int32)
    e_hi = jnp.zeros(gmax.shape, jnp.int32)
    a_lo = jnp.zeros_like(gmax)
    a_hi = jnp.zeros_like(gmax)
    nsel = jnp.zeros(gmax.shape, jnp.int32)
    for g in range(N_GROUPS):
        for i in range(4):
            e = 4 * g + i
            beat = jnp.zeros(gmax.shape, jnp.int32)
            for j in range(4):
                if j != i:
                    o = 4 * g + j
                    beat = beat + jnp.where((s[o] > s[e]) | ((s[o] == s[e]) & (j < i)), 1, 0)
            pick = gsel[g] & (beat < 2)
            is_first = pick & (nsel == 0)
            is_second = pick & (nsel == 1)
            e_lo = jnp.where(is_first, e, e_lo)
            a_lo = jnp.where(is_first, a[e], a_lo)
            e_hi = jnp.where(is_second, e, e_hi)
            a_hi = jnp.where(is_second, a[e], a_hi)
            nsel = nsel + jnp.where(pick, 1, 0)
    grp = e_lo // EXPERTS_PER_GROUP
    lo = e_lo - grp * EXPERTS_PER_GROUP
    hi = e_hi - grp * EXPERTS_PER_GROUP
    pair = ((lo * (7 - lo)) >> 1) + (hi - lo - 1)
    pair = jnp.where(pair == 1, 3, jnp.where(pair == 2, 4, jnp.where(pair == 3, 2, jnp.where(pair == 4, 1, pair))))
    cls = grp * N_PAIRS + pair
    wsum = a_lo + a_hi
    w_lo, w_hi = a_lo / wsum, a_hi / wsum
    onehot = (lax.broadcasted_iota(jnp.int32, (CLS_ROWS, ROW_BLK), 0) == cls).astype(F32)
    tt = lax.broadcasted_iota(jnp.int32, (ROW_BLK, ROW_BLK), 0) < lax.broadcasted_iota(jnp.int32, (ROW_BLK, ROW_BLK), 1)
    before = _dot(onehot, jnp.where(tt, 1.0, 0.0))
    base = base_s[...]
    rank = jnp.sum(onehot * (before + base[:, 0:1]), axis=0, keepdims=True).astype(jnp.int32)
    base = base + jnp.sum(onehot, axis=1, keepdims=True)
    base_s[...] = base
    cnt_ref[...] = base.astype(jnp.int32)
    ti_ref[0] = jnp.concatenate([cls, rank, e_lo, e_hi, jnp.zeros((4, ROW_BLK), jnp.int32)], axis=0)
    ident = lax.broadcasted_iota(jnp.int32, (ROW_BLK, ROW_BLK), 0) == lax.broadcasted_iota(jnp.int32, (ROW_BLK, ROW_BLK), 1)
    col = lambda r: jnp.sum(jnp.where(ident, r, 0.0), axis=1, keepdims=True)
    tw_ref[...] = jnp.where(lax.broadcasted_iota(jnp.int32, (ROW_BLK, 128), 1) < 64, col(w_lo), col(w_hi))
    h_ref[...] = h


def _moe_route(x, mod, g, router_w, router_b, layer):
    row = lambda n: pl.BlockSpec((ROW_BLK, n), lambda i: (i, 0))
    full = lambda a, b: pl.BlockSpec((a, b), lambda i: (0, 0))
    return pl.pallas_call(
        _moe_route_kernel,
        grid=(N_ROW_BLK,),
        in_specs=[row(D_MODEL), full(1, D_MODEL), _mod_spec(layer, 4), _mod_spec(layer, 3),
                  full(N_EXPERTS, D_MODEL), full(N_EXPERTS, 1)],
        out_specs=[row(D_MODEL), pl.BlockSpec((1, 8, ROW_BLK), lambda i: (i, 0, 0)), row(128), full(CLS_ROWS, 128)],
        out_shape=[jax.ShapeDtypeStruct((N_TOK, D_MODEL), F32),
                   jax.ShapeDtypeStruct((N_ROW_BLK, 8, ROW_BLK), jnp.int32),
                   jax.ShapeDtypeStruct((N_TOK, 128), F32),
                   jax.ShapeDtypeStruct((CLS_ROWS, 128), jnp.int32)],
        scratch_shapes=[pltpu.VMEM((CLS_ROWS, 128), F32)],
        compiler_params=_cparams("arbitrary"),
        name="moe_route",
    )(x, g.reshape(1, D_MODEL), mod, mod, router_w.T, router_b.reshape(N_EXPERTS, 1))


def _moe_experts_kernel(elo_ref, ehi_ref, nblk_ref, swap_ref, hs_ref, ws_ref, wgl_ref, wul_ref, wdl_ref,
                        wgh_ref, wuh_ref, wdh_ref, y_ref):
    del elo_ref, ehi_ref
    j = pl.program_id(0)

    @pl.when(j < nblk_ref[0])
    def _():
        h = hs_ref[...].astype(BF16)
        swap = swap_ref[j] == 1
        gate_a = jnp.where(swap, ws_ref[:, 64:65], ws_ref[:, 0:1])
        gate_b = jnp.where(swap, ws_ref[:, 0:1], ws_ref[:, 64:65])
        acc = None
        for wg, wu, wd, gate in ((wgl_ref, wul_ref, wdl_ref, gate_a), (wgh_ref, wuh_ref, wdh_ref, gate_b)):
            g = _dot(h, wg[0, 0])
            u = _dot(h, wu[0, 0])
            hid = g * jax.nn.sigmoid(g) * u * gate
            y = _dot(hid, wd[0, 0])
            acc = y if acc is None else acc + y
        y_ref[...] = acc

    @pl.when(j >= nblk_ref[0])
    def _():
        y_ref[...] = jnp.zeros_like(y_ref)


def _moe_experts(hs, ws, blk_elo, blk_ehi, nblk, swap, w_gate, w_up, w_down, layer):
    last = lambda j, nb: jnp.minimum(j, nb[0] - 1)
    wspec = lambda a, b, which: pl.BlockSpec(
        (1, 1, a, b), lambda j, elo, ehi, nb, sw: (layer, (elo, ehi)[which][last(j, nb)], 0, 0))
    grid_spec = pltpu.PrefetchScalarGridSpec(
        num_scalar_prefetch=4,
        grid=(MOE_NBLK,),
        in_specs=[pl.BlockSpec((MOE_BM, D_MODEL), lambda j, elo, ehi, nb, sw: (last(j, nb), 0)),
                  pl.BlockSpec((MOE_BM, 128), lambda j, elo, ehi, nb, sw: (last(j, nb), 0)),
                  wspec(D_MODEL, D_FF, 0), wspec(D_MODEL, D_FF, 0), wspec(D_FF, D_MODEL, 0),
                  wspec(D_MODEL, D_FF, 1), wspec(D_MODEL, D_FF, 1), wspec(D_FF, D_MODEL, 1)],
        out_specs=pl.BlockSpec((MOE_BM, D_MODEL), lambda j, *_: (j, 0)),
    )
    return pl.pallas_call(
        _moe_experts_kernel,
        grid_spec=grid_spec,
        out_shape=jax.ShapeDtypeStruct((MOE_ROWS, D_MODEL), F32),
        compiler_params=_cparams("arbitrary"),
        name="moe_experts",
    )(blk_elo, blk_ehi, nblk, swap, hs, ws, w_gate, w_up, w_down, w_gate, w_up, w_down)


SC_ROWS = 64


def _sc_gather_rows(table, idx):
    info = plsc.get_sparse_core_info()
    nc, nw = info.num_cores, info.num_cores * info.num_subcores
    b, d = idx.shape[0], table.shape[1]
    per_w = b // nw
    assert per_w * nw == b and per_w % SC_ROWS == 0
    mesh = plsc.VectorSubcoreMesh(core_axis_name="c", subcore_axis_name="s")

    @functools.partial(
        pl.kernel, mesh=mesh, out_type=jax.ShapeDtypeStruct((b, d), table.dtype),
        scratch_types=[pltpu.VMEM((SC_ROWS,), jnp.int32), pltpu.VMEM((SC_ROWS, d), table.dtype),
                       pltpu.SemaphoreType.DMA])
    def gather(table_hbm, idx_hbm, out_hbm, idx_v, rows_v, sem):
        wid = lax.axis_index("s") * nc + lax.axis_index("c")
        for c in range(per_w // SC_ROWS):
            base = wid * per_w + c * SC_ROWS
            pltpu.sync_copy(idx_hbm.at[pl.ds(base, SC_ROWS)], idx_v)
            pltpu.async_copy(table_hbm.at[idx_v], rows_v, sem).wait()
            pltpu.sync_copy(rows_v, out_hbm.at[pl.ds(base, SC_ROWS)])

    return gather(table, idx)


def _moe_residual_kernel(x_ref, y_ref, gt_ref, o_ref):
    o_ref[...] = x_ref[...] + gt_ref[0] * y_ref[...]


def _moe_combine(x, y_sorted, pos, mod, layer):
    row = pl.BlockSpec((ROW_BLK, D_MODEL), lambda i: (i, 0))
    return pl.pallas_call(
        _moe_residual_kernel,
        grid=(N_ROW_BLK,),
        in_specs=[row, row, _mod_spec(layer, 5)],
        out_specs=row,
        out_shape=jax.ShapeDtypeStruct((N_TOK, D_MODEL), F32),
        compiler_params=_cparams("arbitrary"),
        name="moe_residual",
    )(x, _sc_gather_rows(y_sorted, pos), mod)


def _moe(x, mod, g, router_w, router_b, w_gate, w_up, w_down, layer):
    h, info, wtok, counts = _moe_route(x, mod, g, router_w, router_b, layer)
    cls = info[:, 0, :].reshape(N_TOK)
    rank = info[:, 1, :].reshape(N_TOK)
    cnt = counts[:N_CLASSES, 0]
    nb = (cnt + MOE_BM - 1) // MOE_BM
    ends = jnp.cumsum(nb)
    starts = ends - nb
    pos = ((starts * MOE_BM)[cls] + rank).astype(jnp.int32)
    blk = jnp.arange(MOE_NBLK, dtype=jnp.int32)
    blk_cls = jnp.minimum(jnp.sum((blk[:, None] >= ends[None, :]).astype(jnp.int32), axis=1), N_CLASSES - 1)
    slot_a = jnp.asarray([s[0] for s in PAIR_SLOTS], jnp.int32)
    slot_b = jnp.asarray([s[1] for s in PAIR_SLOTS], jnp.int32)
    grp = blk_cls // N_PAIRS
    blk_ea = (grp * EXPERTS_PER_GROUP + slot_a[blk_cls % N_PAIRS]).astype(jnp.int32)
    blk_eb = (grp * EXPERTS_PER_GROUP + slot_b[blk_cls % N_PAIRS]).astype(jnp.int32)
    swap = (blk_ea > blk_eb).astype(jnp.int32)
    nblk = ends[-1:].astype(jnp.int32)
    perm = jnp.zeros((MOE_ROWS,), jnp.int32).at[pos].set(jnp.arange(N_TOK, dtype=jnp.int32))
    hs = _sc_gather_rows(h, perm)
    ws = _sc_gather_rows(wtok, perm)
    y_sorted = _moe_experts(hs, ws, blk_ea, blk_eb, nblk, swap, w_gate, w_up, w_down, layer)
    return _moe_combine(x, y_sorted, pos, mod, layer)


def _final_norm_kernel(x_ref, g_ref, o_ref):
    o_ref[...] = _rms(x_ref[...]) * g_ref[...]


def _final_norm(x, g, row0, rows):
    base = row0 // ROW_BLK
    return pl.pallas_call(
        _final_norm_kernel,
        grid=(rows // ROW_BLK,),
        in_specs=[pl.BlockSpec((ROW_BLK, D_MODEL), lambda i: (base + i, 0)), pl.BlockSpec((1, D_MODEL), lambda i: (0, 0))],
        out_specs=pl.BlockSpec((ROW_BLK, D_MODEL), lambda i: (i, 0)),
        out_shape=jax.ShapeDtypeStruct((rows, D_MODEL), F32),
        compiler_params=_cparams("arbitrary"),
        name="final_norm",
    )(x, g.reshape(1, D_MODEL))


def kernel(x_prompt, x_sample, cache_attn_k, cache_attn_v, state_ssm_re, state_ssm_im, cache_mla_ckv, cache_mla_kpe,
           state_hgrn, c, c_ctx, w_mod, b_mod, g_mix, g_ffn, g_final, router_w, router_b, moe_w_gate, moe_w_up,
           moe_w_down, ab_w_in, ab_sink, s5_lam_re, s5_lam_im, s5_log_dt, s5_b_re, s5_b_im, s5_c_re, s5_c_im, s5_d,
           s5_w_glu, ab_w_out, cd_w_in, mla_g_q, mla_w_q_up, mla_g_kv, mla_w_kv_up, hg_lower_bounds, hg_g_o, cd_w_out):
    x = jnp.concatenate([x_prompt.reshape(N_CTX_TOK, D_MODEL), x_sample.reshape(N_LAT_TOK, D_MODEL)], axis=0)
    cond = jnp.zeros((MOD_ROWS, D_MODEL), F32).at[0].set(c_ctx).at[1:1 + DEC_BATCH].set(c)
    mod = _modulation(cond, w_mod, b_mod)
    keep = ([], [], [], [], [], [], [])
    for l in range(DEPTH):
        j = l // 2
        if l % 2 == 0:
            q, k, v, u = _ab_in(x, mod, g_mix[l], ab_w_in[j], l)
            o_a = _attention_a(q, k, v, cache_attn_k[:, j].reshape(DEC_BATCH, PAST_LEN, A_KV_WIDTH),
                               cache_attn_v[:, j].reshape(DEC_BATCH, PAST_LEN, A_KV_WIDTH), ab_sink[j])
            y, fin_re, fin_im = _s5_scan(u, state_ssm_re[:, j], state_ssm_im[:, j], s5_lam_re[j], s5_lam_im[j],
                                         s5_log_dt[j], s5_b_re[j], s5_b_im[j], s5_c_re[j], s5_c_im[j])
            x = _ab_out(x, o_a, y, u, s5_d[j], s5_w_glu[j], ab_w_out[j], mod, l)
            keep[0].append(k[:N_CTX_TOK].reshape(BATCH, SEQ, A_KV_HEADS, HEAD_DIM))
            keep[1].append(v[:N_CTX_TOK].reshape(BATCH, SEQ, A_KV_HEADS, HEAD_DIM))
            keep[2].append(fin_re)
            keep[3].append(fin_im)
        else:
            qn, qr, ckv, kpe, kn, vm, hq, hff, hfb, hi, hg = _cd_in(
                x, mod, g_mix[l], cd_w_in[j], mla_g_q[j], mla_w_q_up[j], mla_g_kv[j], mla_w_kv_up[j], l)
            kcn, vc = _mla_cache_kv(cache_mla_ckv[:, j].reshape(DEC_BATCH * PAST_LEN, MLA_KV_RANK), mla_w_kv_up[j])
            o_c = _attention_mla(qn, qr, kn, kpe, vm, kcn, cache_mla_kpe[:, j], vc)
            s0_ctx = jnp.zeros((BATCH, 2, 2, HG_VAL, HG_HALF), F32)
            of_c, ob_c, s_fin = _hgrn_scan(hq, hff, hfb, hi, hg_lower_bounds, s0_ctx, l, 0, BATCH, SEQ)
            of_l, ob_l, _ = _hgrn_scan(hq, hff, hfb, hi, hg_lower_bounds, _hg_state_to_blocks(state_hgrn[:, j]), l,
                                       N_CTX_TOK, DEC_BATCH, DEC_SEQ)
            x = _cd_out(x, o_c, (of_c, of_l), (ob_c, ob_l), hg, hg_g_o[j], cd_w_out[j], mod, l)
            keep[4].append(ckv[:N_CTX_TOK].reshape(BATCH, SEQ, MLA_KV_RANK))
            keep[5].append(kpe[:N_CTX_TOK, :MLA_ROPE].reshape(BATCH, SEQ, MLA_ROPE))
            keep[6].append(_hg_state_from_blocks(s_fin))
        x = _moe(x, mod, g_ffn[l], router_w, router_b, moe_w_gate, moe_w_up, moe_w_down, l)
    y_ctx = _final_norm(x, g_final, 0, N_CTX_TOK)
    y_lat = _final_norm(x, g_final, N_CTX_TOK, N_LAT_TOK)
    return (y_ctx.reshape(BATCH, SEQ, D_MODEL), y_lat.reshape(DEC_BATCH, DEC_SEQ, D_MODEL),
            jnp.stack(keep[0], 1), jnp.stack(keep[1], 1), jnp.stack(keep[2], 1), jnp.stack(keep[3], 1),
            jnp.stack(keep[4], 1), jnp.stack(keep[5], 1), jnp.stack(keep[6], 1))
```

```python
import functools
import math

import numpy as np
import jax
import jax.numpy as jnp
from jax import lax
from jax.experimental import pallas as pl
from jax.experimental.pallas import tpu as pltpu
from jax.experimental.pallas import tpu_sc as plsc

F32 = jnp.float32
BF16 = jnp.bfloat16

D_MODEL = 1024
BATCH = 16
SEQ = 256
DEPTH = 2
DEC_BATCH = 4
DEC_SEQ = 1024
PAST_LEN = 512
GRID_W = 64
N_MOD = 6
EPS = 1e-6
NEG_INF = -1e30
ROPE_BASE = 10000.0
HEAD_DIM = 64
A_HEADS = 8
A_KV_HEADS = 2
A_WINDOW = 128
A_WIDTH = A_HEADS * HEAD_DIM
A_KV_WIDTH = A_KV_HEADS * HEAD_DIM
S5_WIDTH = D_MODEL // 2
S5_GROUP = 16
S5_GROUPS = S5_WIDTH // S5_GROUP
S5_STATE = 64
MLA_HEADS = 8
MLA_Q_RANK = D_MODEL // 4
MLA_KV_RANK = D_MODEL // 8
MLA_NOPE = 64
MLA_ROPE = 32
MLA_V = 64
MLA_WIDTH = MLA_HEADS * MLA_V
HG_HEADS = 8
HG_KEY = 64
HG_VAL = 64
HG_KD = HG_HEADS * HG_KEY
HG_WIDTH = HG_HEADS * HG_VAL
HG_CHUNK = 16
N_EXPERTS = 16
N_GROUPS = 4
EXPERTS_PER_GROUP = N_EXPERTS // N_GROUPS
D_FF = D_MODEL // 2

N_CTX_TOK = BATCH * SEQ
N_LAT_TOK = DEC_BATCH * DEC_SEQ
N_TOK = N_CTX_TOK + N_LAT_TOK
ROW_BLK = 256
N_ROW_BLK = N_TOK // ROW_BLK
N_CTX_BLK = N_CTX_TOK // ROW_BLK
LAT_BLK_PER_SEQ = DEC_SEQ // ROW_BLK
MOD_ROWS = 8
VMEM_LIMIT = 56 * 1024 * 1024


def _cparams(*sem):
    return pltpu.CompilerParams(dimension_semantics=sem, vmem_limit_bytes=VMEM_LIMIT)


def _mod_group(i):
    return jnp.where(i < N_CTX_BLK, 0, 1 + (i - N_CTX_BLK) // LAT_BLK_PER_SEQ)


def _mod_spec(layer, which):
    return pl.BlockSpec((1, 1, D_MODEL), lambda i: ((layer * MOD_ROWS + _mod_group(i)) * N_MOD + which, 0, 0))


def _rope_blk(i):
    return jnp.where(i < N_CTX_BLK, 0, 1 + (i - N_CTX_BLK) % LAT_BLK_PER_SEQ)


def _rope_tables(rot_dim):
    n_freq = rot_dim // 4
    t = np.arange(DEC_SEQ)
    rows = (t // GRID_W).astype(np.float32)
    cols = (t % GRID_W).astype(np.float32)
    inv = (np.float32(ROPE_BASE) ** (-np.arange(n_freq, dtype=np.float32) / np.float32(n_freq))).astype(np.float32)
    ang_r = rows[:, None] * inv[None, :]
    ang_c = cols[:, None] * inv[None, :]
    ang = np.concatenate([ang_r, ang_r, ang_c, ang_c], axis=-1).astype(np.float32)
    reps = 128 // rot_dim
    cos = np.tile(np.cos(ang), (1, reps)).astype(np.float32)
    sin = np.tile(np.sin(ang), (1, reps)).astype(np.float32)
    lane = np.arange(128)
    first = (lane % (2 * n_freq)) < n_freq
    sin_a = np.where(first[None, :], -sin, 0.0).astype(np.float32)
    sin_b = np.where(first[None, :], 0.0, sin).astype(np.float32)
    ident = np.zeros((ROW_BLK, 128), np.float32)
    cos = np.concatenate([ident + 1.0, cos], axis=0)
    sin_a = np.concatenate([ident, sin_a], axis=0)
    sin_b = np.concatenate([ident, sin_b], axis=0)
    return jnp.asarray(cos), jnp.asarray(sin_a), jnp.asarray(sin_b)


def _apply_rope(x, cos, sin_a, sin_b, quarter):
    outs = []
    for j in range(x.shape[1] // 128):
        xt = x[:, 128 * j:128 * (j + 1)]
        up = pltpu.roll(xt, 128 - quarter, axis=1)
        dn = pltpu.roll(xt, quarter, axis=1)
        outs.append(xt * cos + up * sin_a + dn * sin_b)
    return outs[0] if len(outs) == 1 else jnp.concatenate(outs, axis=1)


def _rms(x):
    return x * lax.rsqrt(jnp.mean(x * x, axis=-1, keepdims=True) + EPS)


def _norm_mod(x, g, sc, sh):
    return _rms(x) * g * (1.0 + sc) + sh


def _dot(a, b):
    return jnp.dot(a.astype(BF16), b.astype(BF16), preferred_element_type=F32)


def _dot_nt(a, b):
    return lax.dot_general(a.astype(BF16), b.astype(BF16), (((1,), (1,)), ((), ())), preferred_element_type=F32)


def _mod_kernel(cond_ref, w_ref, b_ref, o_ref):
    c = cond_ref[...]
    s = c * jax.nn.sigmoid(c)
    o_ref[0] = _dot(s, w_ref[0]) + b_ref[0]


def _modulation(cond, w_mod, b_mod):
    nb = 1024
    out = pl.pallas_call(
        _mod_kernel,
        grid=(DEPTH, N_MOD * D_MODEL // nb),
        in_specs=[pl.BlockSpec((MOD_ROWS, D_MODEL), lambda l, n: (0, 0)),
                  pl.BlockSpec((1, D_MODEL, nb), lambda l, n: (l, 0, n)),
                  pl.BlockSpec((1, 1, nb), lambda l, n: (l, 0, n))],
        out_specs=pl.BlockSpec((1, MOD_ROWS, nb), lambda l, n: (l, 0, n)),
        out_shape=jax.ShapeDtypeStruct((DEPTH, MOD_ROWS, N_MOD * D_MODEL), F32),
        compiler_params=_cparams("arbitrary", "arbitrary"),
        name="modulation",
    )(cond, w_mod, b_mod.reshape(DEPTH, 1, N_MOD * D_MODEL))
    return out.reshape(DEPTH * MOD_ROWS * N_MOD, 1, D_MODEL)


def _ab_in_kernel(x_ref, g_ref, sc_ref, sh_ref, w_ref, cos_ref, sa_ref, sb_ref,
                  q_ref, k_ref, v_ref, u_ref, wq_s, wk_s, wv_s, wu_s):
    @pl.when(pl.program_id(0) == 0)
    def _():
        wq_s[...] = w_ref[:, 0:A_WIDTH].astype(BF16)
        wk_s[...] = w_ref[:, A_WIDTH:A_WIDTH + A_KV_WIDTH].astype(BF16)
        wv_s[...] = w_ref[:, A_WIDTH + A_KV_WIDTH:A_WIDTH + 2 * A_KV_WIDTH].astype(BF16)
        wu_s[...] = w_ref[:, A_WIDTH + 2 * A_KV_WIDTH:].astype(BF16)

    h = _norm_mod(x_ref[...], g_ref[...], sc_ref[0], sh_ref[0]).astype(BF16)
    cos, sa, sb = cos_ref[...], sa_ref[...], sb_ref[...]
    q = jnp.dot(h, wq_s[...], preferred_element_type=F32)
    q_ref[...] = _apply_rope(q, cos, sa, sb, HEAD_DIM // 4).astype(q_ref.dtype)
    k = jnp.dot(h, wk_s[...], preferred_element_type=F32)
    k_ref[...] = _apply_rope(k, cos, sa, sb, HEAD_DIM // 4)
    v_ref[...] = jnp.dot(h, wv_s[...], preferred_element_type=F32)
    u_ref[...] = jnp.dot(h, wu_s[...], preferred_element_type=F32)


def _ab_in(x, mod, g, w, layer):
    cos, sa, sb = _rope_tables(HEAD_DIM)
    d_in = w.shape[1]
    row = lambda n: pl.BlockSpec((ROW_BLK, n), lambda i: (i, 0))
    rope = pl.BlockSpec((ROW_BLK, 128), lambda i: (_rope_blk(i), 0))
    return pl.pallas_call(
        _ab_in_kernel,
        grid=(N_ROW_BLK,),
        in_specs=[row(D_MODEL), pl.BlockSpec((1, D_MODEL), lambda i: (0, 0)),
                  _mod_spec(layer, 1), _mod_spec(layer, 0),
                  pl.BlockSpec((D_MODEL, d_in), lambda i: (0, 0)), rope, rope, rope],
        out_specs=[row(A_WIDTH), row(A_KV_WIDTH), row(A_KV_WIDTH), row(S5_WIDTH)],
        out_shape=[jax.ShapeDtypeStruct((N_TOK, A_WIDTH), BF16), jax.ShapeDtypeStruct((N_TOK, A_KV_WIDTH), F32),
                   jax.ShapeDtypeStruct((N_TOK, A_KV_WIDTH), F32), jax.ShapeDtypeStruct((N_TOK, S5_WIDTH), F32)],
        scratch_shapes=[pltpu.VMEM((D_MODEL, A_WIDTH), BF16), pltpu.VMEM((D_MODEL, A_KV_WIDTH), BF16),
                        pltpu.VMEM((D_MODEL, A_KV_WIDTH), BF16), pltpu.VMEM((D_MODEL, S5_WIDTH), BF16)],
        compiler_params=_cparams("arbitrary"),
        name="ab_in",
    )(x, g.reshape(1, D_MODEL), mod, mod, w, cos, sa, sb)


def _softmax_pv(s_list, v_list, sink):
    m = functools.reduce(jnp.maximum, [jnp.max(s, axis=-1, keepdims=True) for s in s_list])
    if sink is not None:
        m = jnp.maximum(m, sink)
    ps = [jnp.exp(s - m) for s in s_list]
    l = functools.reduce(jnp.add, [jnp.sum(p, axis=-1, keepdims=True) for p in ps])
    if sink is not None:
        l = l + jnp.exp(sink - m)
    o = functools.reduce(jnp.add, [_dot(p, v) for p, v in zip(ps, v_list)])
    return o / l


def _gqa_with_sink(sink_ref, q, k_of, v_of, bias, o_ref):
    g = A_HEADS // A_KV_HEADS
    r = q.shape[0]
    outs = [None] * A_HEADS
    for kh in range(A_KV_HEADS):
        heads = range(g * kh, g * (kh + 1))
        qg = jnp.concatenate([q[:, HEAD_DIM * h:HEAD_DIM * (h + 1)] for h in heads], axis=0)
        sink = jnp.concatenate([jnp.full((r, 1), sink_ref[h], F32) for h in heads], axis=0)
        s = _dot_nt(qg, k_of(kh))
        if bias is not None:
            s = s + jnp.concatenate([bias] * g, axis=0)
        m = jnp.maximum(jnp.max(s, axis=-1, keepdims=True), sink)
        p = jnp.exp(s - m)
        l = jnp.sum(p, axis=-1, keepdims=True) + jnp.exp(sink - m)
        o = _dot(p, v_of(kh)) / l
        for n, h in enumerate(heads):
            outs[h] = o[r * n:r * (n + 1)]
    o_ref[...] = jnp.concatenate(outs, axis=1).astype(o_ref.dtype)


def _attn_ctx_kernel(sink_ref, q_ref, k_ref, v_ref, o_ref):
    scale = HEAD_DIM ** -0.5
    g = A_HEADS // A_KV_HEADS
    outs = []
    for h in range(A_HEADS):
        kh = h // g
        q = q_ref[:, HEAD_DIM * h:HEAD_DIM * (h + 1)]
        k = k_ref[:, HEAD_DIM * kh:HEAD_DIM * (kh + 1)]
        v = v_ref[:, HEAD_DIM * kh:HEAD_DIM * (kh + 1)]
        s = _dot_nt(q, k) * scale
        outs.append(_softmax_pv([s], [v], sink_ref[h]))
    o_ref[...] = jnp.concatenate(outs, axis=1).astype(o_ref.dtype)


def _attn_lat_kernel(sink_ref, q_ref, kp_ref, kc_ref, kn_ref, vp_ref, vc_ref, vn_ref, kx_ref, vx_ref, o_ref):
    n = pl.program_id(1)
    nb = DEC_SEQ // A_WINDOW
    i = lax.broadcasted_iota(jnp.int32, (A_WINDOW, A_WINDOW), 0)
    j = lax.broadcasted_iota(jnp.int32, (A_WINDOW, A_WINDOW), 1)
    zero = jnp.zeros((A_WINDOW, A_WINDOW), F32)
    bias = jnp.concatenate([jnp.where((j >= i) & (n > 0), 0.0, NEG_INF), zero,
                            jnp.where((j <= i) & (n < nb - 1), 0.0, NEG_INF),
                            jnp.zeros((A_WINDOW, PAST_LEN), F32)], axis=1)

    def rows(p_ref, c_ref, n_ref, x_ref):
        def of(kh):
            sl = slice(HEAD_DIM * kh, HEAD_DIM * (kh + 1))
            return jnp.concatenate([p_ref[:, sl], c_ref[:, sl], n_ref[:, sl], x_ref[0, :, sl]], axis=0)
        return of

    _gqa_with_sink(sink_ref, q_ref[...] * HEAD_DIM ** -0.5, rows(kp_ref, kc_ref, kn_ref, kx_ref),
                   rows(vp_ref, vc_ref, vn_ref, vx_ref), bias, o_ref)


def _attention_a(q, k, v, cache_k, cache_v, sink):
    smem = pl.BlockSpec(memory_space=pltpu.SMEM)
    o = pl.pallas_call(
        _attn_ctx_kernel,
        grid=(BATCH,),
        in_specs=[smem, pl.BlockSpec((SEQ, A_WIDTH), lambda b: (b, 0)),
                  pl.BlockSpec((SEQ, A_KV_WIDTH), lambda b: (b, 0)), pl.BlockSpec((SEQ, A_KV_WIDTH), lambda b: (b, 0))],
        out_specs=pl.BlockSpec((SEQ, A_WIDTH), lambda b: (b, 0)),
        out_shape=jax.ShapeDtypeStruct((N_CTX_TOK, A_WIDTH), BF16),
        compiler_params=_cparams("arbitrary"),
        name="attn_a_ctx",
    )(sink, q, k, v)
    nb = DEC_SEQ // A_WINDOW
    base = N_CTX_TOK // A_WINDOW
    cur = lambda b, n: (base + b * nb + n, 0)
    prev = lambda b, n: (base + b * nb + jnp.maximum(n - 1, 0), 0)
    nxt = lambda b, n: (base + b * nb + jnp.minimum(n + 1, nb - 1), 0)
    kv = lambda f: pl.BlockSpec((A_WINDOW, A_KV_WIDTH), f)
    cache = pl.BlockSpec((1, PAST_LEN, A_KV_WIDTH), lambda b, n: (b, 0, 0))
    o_lat = pl.pallas_call(
        _attn_lat_kernel,
        grid=(DEC_BATCH, nb),
        in_specs=[smem, pl.BlockSpec((A_WINDOW, A_WIDTH), cur), kv(prev), kv(cur), kv(nxt), kv(prev), kv(cur), kv(nxt),
                  cache, cache],
        out_specs=pl.BlockSpec((A_WINDOW, A_WIDTH), lambda b, n: (b * nb + n, 0)),
        out_shape=jax.ShapeDtypeStruct((N_LAT_TOK, A_WIDTH), BF16),
        compiler_params=_cparams("arbitrary", "arbitrary"),
        name="attn_a_lat",
    )(sink, q, k, k, k, v, v, v, cache_k, cache_v)
    return o, o_lat


S5_CHUNK = 16
S5_OCT = 128 // S5_GROUP
S5_NOCT = S5_GROUPS // S5_OCT
S5_K = S5_CHUNK * 128
S5_PART = S5_OCT * S5_STATE
S5_SW = 4 * S5_PART
S5_ROWS_CTX = BATCH * SEQ // S5_CHUNK
S5_ROWS_LAT = DEC_BATCH * DEC_SEQ // S5_CHUNK
S5_ROWS = S5_ROWS_CTX + S5_ROWS_LAT
S5_NB = 4
HI = lax.Precision.HIGHEST


def _s5_disc_kernel(lr_ref, li_ref, ldt_ref, ar_ref, ai_ref, zr_ref, zi_ref):
    lr, li = lr_ref[...], li_ref[...]
    dt = jnp.exp(ldt_ref[...])
    mag = jnp.exp(lr * dt)
    ar, ai = mag * jnp.cos(li * dt), mag * jnp.sin(li * dt)
    den = lr * lr + li * li
    ar_ref[...] = ar
    ai_ref[...] = ai
    zr_ref[...] = ((ar - 1.0) * lr + ai * li) / den
    zi_ref[...] = (ai * lr - (ar - 1.0) * li) / den


def _cmul(xr, xi, yr, yi):
    return xr * yr - xi * yi, xr * yi + xi * yr


def _dot_nt_hi(a, b):
    return lax.dot_general(a, b, (((1,), (1,)), ((), ())), precision=HI, preferred_element_type=F32)


def _s5_prep_kernel(ar_ref, ai_ref, zr_ref, zi_ref, btr_ref, bti_ref, ctr_ref, cti_ref,
                    m_ref, win_ref, wout_ref, a16_ref, pw_s, w_s, k_s):
    t = pl.program_id(1)
    npw = S5_CHUNK + 1
    blk = lambda j: pl.ds(pl.multiple_of(j * 128, 128), 128)

    @pl.when(t == 0)
    def _():
        kd = []
        for d in range(2):
            ar, ai = ar_ref[0, d], ai_ref[0, d]
            pr, pi = jnp.ones_like(ar), jnp.zeros_like(ar)
            bbr, bbi = _cmul(zr_ref[0, d], zi_ref[0, d], btr_ref[0, d], bti_ref[0, d])
            for j in range(npw):
                pw_s[d, 0, j:j + 1, :] = pr
                pw_s[d, 1, j:j + 1, :] = pi
                if j < S5_CHUNK:
                    wr, wi = _cmul(pr, pi, bbr, bbi)
                    w_s[d, 0, 128 * j:128 * (j + 1), :] = wr.astype(BF16)
                    w_s[d, 1, 128 * j:128 * (j + 1), :] = wi.astype(BF16)
                pr, pi = _cmul(pr, pi, ar, ai)
            kd.append(_dot_nt(w_s[d, 0], ctr_ref[0, d]) - _dot_nt(w_s[d, 1], cti_ref[0, d]))
        for jj in range(2 * S5_CHUNK - 1):
            j = jj - (S5_CHUNK - 1)
            if j > 0:
                k = kd[0][128 * j:128 * (j + 1)]
            elif j < 0:
                k = kd[1][128 * -j:128 * (1 - j)]
            else:
                k = kd[0][0:128] + kd[1][0:128]
            k_s[128 * jj:128 * (jj + 1), :] = k
        a16_ref[0] = jnp.concatenate([pw_s[0, 0, S5_CHUNK:npw, :], pw_s[1, 0, S5_CHUNK:npw, :],
                                      pw_s[0, 1, S5_CHUNK:npw, :], pw_s[1, 1, S5_CHUNK:npw, :]], axis=1)

    for tp in range(S5_CHUNK):
        m_ref[0, :, 128 * tp:128 * (tp + 1)] = k_s[blk(S5_CHUNK - 1 + tp - t), :].astype(BF16)

    def power(d, e):
        return pw_s[d, 0, pl.ds(e, 1), :], pw_s[d, 1, pl.ds(e, 1), :]

    for d in range(2):
        j = (S5_CHUNK - 1 - t) if d == 0 else t
        win_ref[0, :, S5_PART * d:S5_PART * (d + 1)] = w_s[d, 0, blk(j), :]
        win_ref[0, :, S5_PART * (2 + d):S5_PART * (3 + d)] = w_s[d, 1, blk(j), :]
        er, ei = _cmul(*power(d, (t + 1) if d == 0 else (S5_CHUNK - t)), ctr_ref[0, d], cti_ref[0, d])
        wout_ref[0, :, S5_PART * d:S5_PART * (d + 1)] = er.astype(BF16)
        wout_ref[0, :, S5_PART * (2 + d):S5_PART * (3 + d)] = (-ei).astype(BF16)


def _s5_main_kernel(u_ref, win_ref, m_ref, wout_ref, a16_ref, h0_ref, y_ref, hfin_ref, uo_s, x_s, hs_s):
    s = pl.program_id(1)
    nq = S5_PART // 128

    @pl.when(s == 0)
    def _():
        for t in range(S5_CHUNK):
            uo_s[:, 128 * t:128 * (t + 1)] = u_ref[pl.ds(t, S5_ROWS, stride=S5_CHUNK), :].astype(BF16)

    @pl.when(s < S5_NB)
    def _():
        x = jnp.dot(uo_s[...], win_ref[0], preferred_element_type=F32)
        for q in range(nq):
            x_s[s * nq + q] = x[:, 128 * q:128 * (q + 1)]

    @pl.when(s == S5_NB - 1)
    def _():
        def run(row0, nb, nc, h):
            for c in range(nc):
                for d in range(2):
                    cc = c if d == 0 else nc - 1 - c
                    rows = pl.ds(row0 + cc, nb, stride=nc)
                    for q in range(nq):
                        kr, ki = d * nq + q, (2 + d) * nq + q
                        hr, hi = h[d][0][q], h[d][1][q]
                        hs_s[kr, rows, :] = hr
                        hs_s[ki, rows, :] = hi
                        ar = a16_ref[0, :, 128 * kr:128 * (kr + 1)]
                        ai = a16_ref[0, :, 128 * ki:128 * (ki + 1)]
                        h[d][0][q] = ar * hr - ai * hi + x_s[kr, rows, :]
                        h[d][1][q] = ar * hi + ai * hr + x_s[ki, rows, :]
            return h

        zero = jnp.zeros((BATCH, 128), F32)
        fin = run(0, BATCH, SEQ // S5_CHUNK, [[[zero] * nq, [zero] * nq] for _ in range(2)])
        for d in range(2):
            for ri in range(2):
                for q in range(nq):
                    k = (2 * ri + d) * nq + q
                    hfin_ref[0, :, 128 * k:128 * (k + 1)] = fin[d][ri][q]
        h0 = [[[h0_ref[0, :, 128 * ((2 * ri + d) * nq + q):128 * ((2 * ri + d) * nq + q + 1)] for q in range(nq)]
               for ri in range(2)] for d in range(2)]
        run(S5_ROWS_CTX, DEC_BATCH, DEC_SEQ // S5_CHUNK, h0)

    @pl.when(s >= S5_NB)
    def _():
        hs = jnp.concatenate([hs_s[k] for k in range(4 * nq)], axis=1).astype(BF16)
        y = (jnp.dot(uo_s[...], m_ref[0], preferred_element_type=F32)
             + lax.dot_general(hs, wout_ref[0], (((1,), (1,)), ((), ())), preferred_element_type=F32))
        for q in range(nq):
            t = (s - S5_NB) * nq + q
            y_ref[pl.ds(t, S5_ROWS, stride=S5_CHUNK), :] = y[:, 128 * q:128 * (q + 1)]


def _s5_octets(t, lanes):
    return t.reshape(2, S5_NOCT, 1, S5_OCT * lanes).transpose(1, 0, 2, 3)


def _s5_blockdiag(t):
    a, n = t.shape[2], t.shape[3]
    t = t.reshape(2, S5_NOCT, S5_OCT, a, n)
    bd = jnp.einsum('dogan,gh->dogahn', t, jnp.eye(S5_OCT, dtype=t.dtype))
    return bd.reshape(2, S5_NOCT, S5_OCT * a, S5_OCT * n).transpose(1, 0, 2, 3)


def _s5_state_to_lanes(h_re, h_im):
    b = h_re.shape[0]
    parts = jnp.stack([h_re[:, 0], h_re[:, 1], h_im[:, 0], h_im[:, 1]], axis=1)
    parts = parts.reshape(b, 4, S5_NOCT, S5_PART).transpose(2, 0, 1, 3)
    return parts.reshape(S5_NOCT, b, S5_SW)


def _s5_state_from_lanes(h):
    b = h.shape[1]
    parts = h.reshape(S5_NOCT, b, 4, S5_OCT, S5_STATE).transpose(1, 2, 0, 3, 4).reshape(b, 4, S5_GROUPS, S5_STATE)
    return parts[:, 0:2], parts[:, 2:4]


def _s5_scan(u, h0_re, h0_im, lam_re, lam_im, log_dt, b_re, b_im, c_re, c_im):
    ng, n = S5_GROUPS, S5_STATE
    rows = 2 * ng
    disc = pl.pallas_call(
        _s5_disc_kernel,
        out_shape=[jax.ShapeDtypeStruct((rows, n), F32)] * 4,
        name="s5_disc",
    )(lam_re.reshape(rows, n), lam_im.reshape(rows, n), log_dt.reshape(rows, 1))
    ar, ai, zr, zi = [_s5_octets(t.reshape(2, ng, n), n) for t in disc]
    bt = lambda t: _s5_blockdiag(t.transpose(0, 1, 3, 2))
    vec = pl.BlockSpec((1, 2, 1, S5_PART), lambda o, t: (o, 0, 0, 0))
    mat = pl.BlockSpec((1, 2, 128, S5_PART), lambda o, t: (o, 0, 0, 0))
    rowblk = lambda w: pl.BlockSpec((1, 128, w), lambda o, t: (o, t, 0))
    m, win, wout, a16 = pl.pallas_call(
        _s5_prep_kernel,
        grid=(S5_NOCT, S5_CHUNK),
        in_specs=[vec, vec, vec, vec, mat, mat, mat, mat],
        out_specs=[rowblk(S5_K), rowblk(S5_SW), rowblk(S5_SW), pl.BlockSpec((1, 1, S5_SW), lambda o, t: (o, 0, 0))],
        out_shape=[jax.ShapeDtypeStruct((S5_NOCT, S5_K, S5_K), BF16), jax.ShapeDtypeStruct((S5_NOCT, S5_K, S5_SW), BF16),
                   jax.ShapeDtypeStruct((S5_NOCT, S5_K, S5_SW), BF16), jax.ShapeDtypeStruct((S5_NOCT, 1, S5_SW), F32)],
        scratch_shapes=[pltpu.VMEM((2, 2, 24, S5_PART), F32), pltpu.VMEM((2, 2, S5_K, S5_PART), BF16),
                        pltpu.VMEM(((2 * S5_CHUNK - 1) * 128, 128), F32)],
        compiler_params=_cparams("arbitrary", "arbitrary"),
        name="s5_prep",
    )(ar, ai, zr, zi, bt(b_re), bt(b_im), _s5_blockdiag(c_re), _s5_blockdiag(c_im))
    nb = S5_NB
    y, hfin = pl.pallas_call(
        _s5_main_kernel,
        grid=(S5_NOCT, 2 * nb),
        in_specs=[pl.BlockSpec((N_TOK, 128), lambda o, s: (0, o)),
                  pl.BlockSpec((1, S5_K, S5_PART), lambda o, s: (o, 0, jnp.minimum(s, nb - 1))),
                  pl.BlockSpec((1, S5_K, S5_PART), lambda o, s: (o, 0, jnp.maximum(s - nb, 0))),
                  pl.BlockSpec((1, S5_PART, S5_SW), lambda o, s: (o, jnp.maximum(s - nb, 0), 0)),
                  pl.BlockSpec((1, 1, S5_SW), lambda o, s: (o, 0, 0)),
                  pl.BlockSpec((1, DEC_BATCH, S5_SW), lambda o, s: (o, 0, 0))],
        out_specs=[pl.BlockSpec((N_TOK, 128), lambda o, s: (0, o)),
                   pl.BlockSpec((1, BATCH, S5_SW), lambda o, s: (o, 0, 0))],
        out_shape=[jax.ShapeDtypeStruct((N_TOK, S5_WIDTH), F32), jax.ShapeDtypeStruct((S5_NOCT, BATCH, S5_SW), F32)],
        scratch_shapes=[pltpu.VMEM((S5_ROWS, S5_K), BF16), pltpu.VMEM((S5_SW // 128, S5_ROWS, 128), F32),
                        pltpu.VMEM((S5_SW // 128, S5_ROWS, 128), F32)],
        compiler_params=_cparams("arbitrary", "arbitrary"),
        name="s5_main",
    )(u, win, m, wout, a16, _s5_state_to_lanes(h0_re, h0_im))
    fin_re, fin_im = _s5_state_from_lanes(hfin)
    return y, fin_re, fin_im


def _ctx_or_lat(ctx_ref, lat_ref):
    return jnp.where(pl.program_id(0) < N_CTX_BLK, ctx_ref[...], lat_ref[...])


def _split_specs(width):
    return [pl.BlockSpec((ROW_BLK, width), lambda i: (jnp.minimum(i, N_CTX_BLK - 1), 0)),
            pl.BlockSpec((ROW_BLK, width), lambda i: (jnp.maximum(i - N_CTX_BLK, 0), 0))]


def _ab_out_kernel(x_ref, oac_ref, oal_ref, y_ref, u_ref, d_ref, wglu_ref, wout_ref, gt_ref, o_ref, wglu_s, wa_s, wb_s):
    @pl.when(pl.program_id(0) == 0)
    def _():
        wglu_s[...] = wglu_ref[...].astype(BF16)
        wa_s[...] = wout_ref[0:A_WIDTH, :].astype(BF16)
        wb_s[...] = wout_ref[A_WIDTH:, :].astype(BF16)

    g = jax.nn.gelu(y_ref[...] + d_ref[...] * u_ref[...])
    ob = g * jax.nn.sigmoid(jnp.dot(g.astype(BF16), wglu_s[...], preferred_element_type=F32))
    out = (jnp.dot(_ctx_or_lat(oac_ref, oal_ref).astype(BF16), wa_s[...], preferred_element_type=F32)
           + jnp.dot(ob.astype(BF16), wb_s[...], preferred_element_type=F32))
    o_ref[...] = x_ref[...] + gt_ref[0] * out


def _ab_out(x, oa, y, u, d_skip, w_glu, w_out, mod, layer):
    row = lambda n: pl.BlockSpec((ROW_BLK, n), lambda i: (i, 0))
    full = lambda a, b: pl.BlockSpec((a, b), lambda i: (0, 0))
    return pl.pallas_call(
        _ab_out_kernel,
        grid=(N_ROW_BLK,),
        in_specs=[row(D_MODEL)] + _split_specs(A_WIDTH) + [row(S5_WIDTH), row(S5_WIDTH), full(1, S5_WIDTH),
                  full(S5_WIDTH, S5_WIDTH), full(A_WIDTH + S5_WIDTH, D_MODEL), _mod_spec(layer, 2)],
        out_specs=row(D_MODEL),
        out_shape=jax.ShapeDtypeStruct((N_TOK, D_MODEL), F32),
        scratch_shapes=[pltpu.VMEM((S5_WIDTH, S5_WIDTH), BF16), pltpu.VMEM((A_WIDTH, D_MODEL), BF16),
                        pltpu.VMEM((S5_WIDTH, D_MODEL), BF16)],
        compiler_params=_cparams("arbitrary"),
        name="ab_out",
    )(x, oa[0], oa[1], y, u, d_skip.reshape(1, S5_WIDTH), w_glu, w_out, mod)


CD_HG0 = MLA_Q_RANK + MLA_KV_RANK + MLA_ROPE
KPE_LANES = 128


def _cd_in_kernel(x_ref, g_ref, sc_ref, sh_ref, w_ref, gq_ref, wqu_ref, gkv_ref, wkvu_ref, cos_ref, sa_ref, sb_ref,
                  qn_ref, qr_ref, ckv_ref, kpe_ref, kn_ref, vm_ref, hq_ref, hff_ref, hfb_ref, hi_ref, hg_ref,
                  wcq_s, wckv_s, wkpe_s, whg_s, wqu_s, wkvu_s):
    @pl.when(pl.program_id(0) == 0)
    def _():
        wcq_s[...] = w_ref[:, 0:MLA_Q_RANK].astype(BF16)
        wckv_s[...] = w_ref[:, MLA_Q_RANK:MLA_Q_RANK + MLA_KV_RANK].astype(BF16)
        kp = w_ref[:, MLA_Q_RANK + MLA_KV_RANK:CD_HG0].astype(BF16)
        wkpe_s[...] = jnp.concatenate([kp] * (KPE_LANES // MLA_ROPE), axis=1)
        whg_s[...] = w_ref[:, CD_HG0:].astype(BF16)
        wqu_s[...] = wqu_ref[...].astype(BF16)
        wkvu_s[...] = wkvu_ref[...].astype(BF16)

    h = _norm_mod(x_ref[...], g_ref[...], sc_ref[0], sh_ref[0]).astype(BF16)
    cos, sa, sb = cos_ref[...], sa_ref[...], sb_ref[...]
    nope = MLA_HEADS * MLA_NOPE
    cq = _rms(jnp.dot(h, wcq_s[...], preferred_element_type=F32)) * gq_ref[...]
    qq = jnp.dot(cq.astype(BF16), wqu_s[...], preferred_element_type=F32)
    qn_ref[...] = qq[:, 0:nope].astype(qn_ref.dtype)
    qr_ref[...] = _apply_rope(qq[:, nope:], cos, sa, sb, MLA_ROPE // 4).astype(qr_ref.dtype)
    ckv = _rms(jnp.dot(h, wckv_s[...], preferred_element_type=F32)) * gkv_ref[...]
    ckv_ref[...] = ckv
    kv = jnp.dot(ckv.astype(BF16), wkvu_s[...], preferred_element_type=F32)
    kn_ref[...] = kv[:, 0:nope].astype(kn_ref.dtype)
    vm_ref[...] = kv[:, nope:].astype(vm_ref.dtype)
    kpe_ref[...] = _apply_rope(jnp.dot(h, wkpe_s[...], preferred_element_type=F32), cos, sa, sb, MLA_ROPE // 4)
    hh = jnp.dot(h, whg_s[...], preferred_element_type=F32)
    for n, ref in enumerate((hq_ref, hff_ref, hfb_ref, hi_ref, hg_ref)):
        ref[...] = hh[:, HG_KD * n:HG_KD * (n + 1)]


def _mla_split_heads(w, a):
    k, n = w.shape
    w = w.reshape(k, MLA_HEADS, n // MLA_HEADS)
    return jnp.concatenate([w[:, :, :a].reshape(k, -1), w[:, :, a:].reshape(k, -1)], axis=1)


def _cd_in(x, mod, g, w, g_q, w_q_up, g_kv, w_kv_up, layer):
    cos, sa, sb = _rope_tables(MLA_ROPE)
    d_in = w.shape[1]
    row = lambda n: pl.BlockSpec((ROW_BLK, n), lambda i: (i, 0))
    full = lambda a, b: pl.BlockSpec((a, b), lambda i: (0, 0))
    rope = pl.BlockSpec((ROW_BLK, 128), lambda i: (_rope_blk(i), 0))
    widths = [MLA_HEADS * MLA_NOPE, MLA_HEADS * MLA_ROPE, MLA_KV_RANK, KPE_LANES, MLA_HEADS * MLA_NOPE, MLA_WIDTH] + [HG_KD] * 5
    nq = MLA_HEADS * (MLA_NOPE + MLA_ROPE)
    nkv = MLA_HEADS * (MLA_NOPE + MLA_V)
    return pl.pallas_call(
        _cd_in_kernel,
        grid=(N_ROW_BLK,),
        in_specs=[row(D_MODEL), full(1, D_MODEL), _mod_spec(layer, 1), _mod_spec(layer, 0), full(D_MODEL, d_in),
                  full(1, MLA_Q_RANK), full(MLA_Q_RANK, nq), full(1, MLA_KV_RANK), full(MLA_KV_RANK, nkv), rope, rope, rope],
        out_specs=[row(n) for n in widths],
        out_shape=[jax.ShapeDtypeStruct((N_TOK, n), BF16 if i in (0, 1, 4, 5) else F32) for i, n in enumerate(widths)],
        scratch_shapes=[pltpu.VMEM((D_MODEL, MLA_Q_RANK), BF16), pltpu.VMEM((D_MODEL, MLA_KV_RANK), BF16),
                        pltpu.VMEM((D_MODEL, KPE_LANES), BF16), pltpu.VMEM((D_MODEL, 5 * HG_KD), BF16),
                        pltpu.VMEM((MLA_Q_RANK, nq), BF16), pltpu.VMEM((MLA_KV_RANK, nkv), BF16)],
        compiler_params=_cparams("arbitrary"),
        name="cd_in",
    )(x, g.reshape(1, D_MODEL), mod, mod, w, g_q.reshape(1, -1), _mla_split_heads(w_q_up, MLA_NOPE),
      g_kv.reshape(1, -1), _mla_split_heads(w_kv_up, MLA_NOPE), cos, sa, sb)


def _mm_kernel(a_ref, w_ref, *o_refs):
    r = _dot(a_ref[...], w_ref[...])
    off = 0
    for o in o_refs:
        o[...] = r[:, off:off + o.shape[1]]
        off += o.shape[1]


def _mla_cache_kv(cckv, w_kv_up):
    n = MLA_HEADS * MLA_NOPE
    rows = cckv.shape[0]
    return pl.pallas_call(
        _mm_kernel,
        grid=(rows // PAST_LEN,),
        in_specs=[pl.BlockSpec((PAST_LEN, MLA_KV_RANK), lambda i: (i, 0)),
                  pl.BlockSpec((MLA_KV_RANK, 2 * n), lambda i: (0, 0))],
        out_specs=[pl.BlockSpec((PAST_LEN, n), lambda i: (i, 0))] * 2,
        out_shape=[jax.ShapeDtypeStruct((rows, n), F32)] * 2,
        compiler_params=_cparams("arbitrary"),
        name="mla_cache_kv",
    )(cckv, _mla_split_heads(w_kv_up, MLA_NOPE))


def _mla_heads(qn, qr, keys, o_ref):
    scale = (MLA_NOPE + MLA_ROPE) ** -0.5
    outs = []
    for h in range(MLA_HEADS):
        a = slice(MLA_NOPE * h, MLA_NOPE * (h + 1))
        r = slice(MLA_ROPE * h, MLA_ROPE * (h + 1))
        s_list = [(_dot_nt(qn[:, a], kn[:, a]) + _dot_nt(qr[:, r], kp[:, 0:MLA_ROPE])) * scale for kn, kp, _ in keys]
        outs.append(_softmax_pv(s_list, [v[:, a] for _, _, v in keys], None))
    o_ref[...] = jnp.concatenate(outs, axis=1).astype(o_ref.dtype)


def _mla_ctx_kernel(qn_ref, qr_ref, kn_ref, kp_ref, v_ref, o_ref):
    _mla_heads(qn_ref[...], qr_ref[...], [(kn_ref[...], kp_ref[...], v_ref[...])], o_ref)


def _mla_lat_kernel(qn_ref, qr_ref, kn_ref, kp_ref, v_ref, kcn_ref, kcp_ref, vc_ref, o_ref):
    _mla_heads(qn_ref[...], qr_ref[...],
               [(kcn_ref[...], kcp_ref[0], vc_ref[...]), (kn_ref[...], kp_ref[...], v_ref[...])], o_ref)


def _attention_mla(qn, qr, kn, kpe, vm, kcn, kcpe, vc):
    n = MLA_HEADS * MLA_NOPE
    nr = MLA_HEADS * MLA_ROPE
    blk = lambda w: pl.BlockSpec((SEQ, w), lambda b: (b, 0))
    o = pl.pallas_call(
        _mla_ctx_kernel,
        grid=(BATCH,),
        in_specs=[blk(n), blk(nr), blk(n), blk(KPE_LANES), blk(MLA_WIDTH)],
        out_specs=blk(MLA_WIDTH),
        out_shape=jax.ShapeDtypeStruct((N_CTX_TOK, MLA_WIDTH), BF16),
        compiler_params=_cparams("arbitrary"),
        name="mla_ctx",
    )(qn, qr, kn, kpe, vm)
    nq = DEC_SEQ // ROW_BLK
    qblk = lambda w: pl.BlockSpec((ROW_BLK, w), lambda b, i: (N_CTX_BLK + b * nq + i, 0))
    seq = lambda w: pl.BlockSpec((DEC_SEQ, w), lambda b, i: (N_CTX_TOK // DEC_SEQ + b, 0))
    past = lambda w: pl.BlockSpec((PAST_LEN, w), lambda b, i: (b, 0))
    o_lat = pl.pallas_call(
        _mla_lat_kernel,
        grid=(DEC_BATCH, nq),
        in_specs=[qblk(n), qblk(nr), seq(n), seq(KPE_LANES), seq(MLA_WIDTH), past(n),
                  pl.BlockSpec((1, PAST_LEN, MLA_ROPE), lambda b, i: (b, 0, 0)), past(MLA_WIDTH)],
        out_specs=pl.BlockSpec((ROW_BLK, MLA_WIDTH), lambda b, i: (b * nq + i, 0)),
        out_shape=jax.ShapeDtypeStruct((N_LAT_TOK, MLA_WIDTH), BF16),
        compiler_params=_cparams("arbitrary", "arbitrary"),
        name="mla_lat",
    )(qn, qr, kn, kpe, vm, kcn, kcpe, vc)
    return o, o_lat


HG_TILE = 128
HG_NC = HG_TILE // HG_CHUNK
HG_HALF = 256


HG_SLABS = HG_KD // 128


def _hg_token_plane(raw_s, p):
    return jnp.concatenate([raw_s[j, pl.ds(p, HG_NC, stride=HG_CHUNK), :] for j in range(HG_SLABS)], axis=1)


def _hg_put(ref, x):
    for j in range(HG_SLABS):
        ref[j] = x[:, 128 * j:128 * (j + 1)]


def _hg_get(ref, c, l):
    rows = pl.ds(c, HG_CHUNK, stride=HG_NC)
    return jnp.concatenate([ref[2 * l, rows, :], ref[2 * l + 1, rows, :]], axis=1)


def _hg_plane(ref, p):
    return jnp.concatenate([ref[j, HG_NC * p:HG_NC * (p + 1), :] for j in range(HG_SLABS)], axis=1)


def _hg_put_plane(ref, p, x):
    for j in range(HG_SLABS):
        ref[j, HG_NC * p:HG_NC * (p + 1), :] = x[:, 128 * j:128 * (j + 1)]


HG_PAIRS = HG_CHUNK * (HG_CHUNK + 1) // 2
HG_PAIR_ROWS = HG_PAIRS * HG_NC


def _hg_pair_rows(d, i):
    n = d * HG_CHUNK - d * (d - 1) // 2 + (i - d)
    return slice(HG_NC * n, HG_NC * (n + 1))


def _hg_direction(hq_ref, hf_ref, hi_ref, o_ref, lb, st_ref, raw_s, q_s, f_s, kk_s, v_s, qt_s, kt_s, o_s, p_s, e_s,
                  sign, ones_bd, head_mask):
    _hg_put(raw_s, hq_ref[...])
    for p in range(HG_CHUNK):
        hq = _hg_token_plane(raw_s, p)
        _hg_put_plane(q_s, p, hq * jax.nn.sigmoid(hq))
    _hg_put(raw_s, hf_ref[...])
    for p in range(HG_CHUNK):
        f = lb + (1.0 - lb) * jax.nn.sigmoid(_hg_token_plane(raw_s, p))
        _hg_put_plane(f_s, p, f)
        _hg_put_plane(kk_s, p, 1.0 - f)
    _hg_put(raw_s, hi_ref[...])
    for p in range(HG_CHUNK):
        _hg_put_plane(v_s, p, _hg_token_plane(raw_s, p))
    pos = (lambda i: i) if sign > 0 else (lambda i: HG_CHUNK - 1 - i)
    plane = lambda ref, i: _hg_plane(ref, pos(i))

    for i in range(HG_CHUNK):
        q = plane(q_s, i)
        p_s[_hg_pair_rows(0, i), :] = q * plane(kk_s, i)
        dec = None
        for d in range(1, i + 1):
            fd = plane(f_s, i - d + 1)
            dec = fd if dec is None else dec * fd
            p_s[_hg_pair_rows(d, i), :] = q * plane(kk_s, i - d) * dec
    step = HG_PAIR_ROWS // 4
    for c in range(0, HG_PAIR_ROWS, step):
        pb = p_s[c:c + step, :].astype(BF16)
        e_s[c:c + step, 0:HG_HALF] = jnp.dot(pb[:, 0:HG_HALF], ones_bd, preferred_element_type=F32)
        e_s[c:c + step, HG_HALF:] = jnp.dot(pb[:, HG_HALF:], ones_bd, preferred_element_type=F32)
    for i in range(HG_CHUNK):
        o = None
        for d in range(i + 1):
            t = e_s[_hg_pair_rows(d, i), :] * plane(v_s, i - d)
            o = t if o is None else o + t
        _hg_put_plane(o_s, pos(i), o)

    incl = None
    for i in range(HG_CHUNK):
        fi = plane(f_s, i)
        incl = fi if incl is None else incl * fi
        _hg_put_plane(qt_s, pos(i), plane(q_s, i) * incl)
    whole = incl
    excl = None
    for i in range(HG_CHUNK - 1, -1, -1):
        kt = plane(kk_s, i)
        if excl is not None:
            kt = kt * excl
        _hg_put_plane(kt_s, pos(i), kt)
        fi = plane(f_s, i)
        excl = fi if excl is None else excl * fi

    tile4 = lambda x: jnp.where(head_mask, jnp.concatenate([x] * 4, axis=0), 0.0).astype(BF16)
    for cc in range(HG_NC):
        c = cc if sign > 0 else HG_NC - 1 - cc
        r = slice(HG_CHUNK * c, HG_CHUNK * (c + 1))
        for g in range(2):
            l = slice(HG_HALF * g, HG_HALF * (g + 1))
            st = st_ref[g]
            oc = _dot_nt(tile4(_hg_get(qt_s, c, g)), st)
            o_ref[r, l] = _hg_get(o_s, c, g) + jnp.concatenate(
                [oc[HG_CHUNK * h:HG_CHUNK * (h + 1), :] for h in range(4)], axis=1)
            v = hi_ref[r, l]
            vs = jnp.concatenate([v[:, HG_VAL * h:HG_VAL * (h + 1)] for h in range(4)], axis=0)
            ds = lax.dot_general(vs.astype(BF16), tile4(_hg_get(kt_s, c, g)), (((0,), (0,)), ((), ())),
                                 preferred_element_type=F32)
            st_ref[g] = st * whole[c:c + 1, l] + ds


def _hg_lower_bound(lb_ref, d, layer):
    raw = [lb_ref[d * DEPTH + m:d * DEPTH + m + 1, :] for m in range(DEPTH)]
    mx = functools.reduce(jnp.maximum, raw)
    e = [jnp.exp(r - mx) for r in raw]
    tot = functools.reduce(jnp.add, e)
    return functools.reduce(jnp.add, e[1:layer + 1], jnp.zeros_like(tot)) / tot


def _hgrn_kernel(nt, layer, hqf_ref, hff_ref, hif_ref, hqb_ref, hfb_ref, hib_ref, lb_ref, s0_ref,
                 of_ref, ob_ref, sfin_ref, st_s, raw_s, q_s, f_s, kk_s, v_s, qt_s, kt_s, o_s, p_s, e_s):
    i = pl.program_id(1)

    @pl.when(i == 0)
    def _():
        st_s[...] = s0_ref[0]

    ri = lax.broadcasted_iota(jnp.int32, (HG_HALF, HG_HALF), 0) // HG_KEY
    ci = lax.broadcasted_iota(jnp.int32, (HG_HALF, HG_HALF), 1) // HG_KEY
    ones_bd = jnp.where(ri == ci, 1.0, 0.0).astype(BF16)
    head_mask = (lax.broadcasted_iota(jnp.int32, (4 * HG_CHUNK, HG_HALF), 0) // HG_CHUNK
                 == lax.broadcasted_iota(jnp.int32, (4 * HG_CHUNK, HG_HALF), 1) // HG_KEY)
    _hg_direction(hqf_ref, hff_ref, hif_ref, of_ref, _hg_lower_bound(lb_ref, 0, layer), st_s.at[0],
                  raw_s, q_s, f_s, kk_s, v_s, qt_s, kt_s, o_s, p_s, e_s, 1, ones_bd, head_mask)
    _hg_direction(hqb_ref, hfb_ref, hib_ref, ob_ref, _hg_lower_bound(lb_ref, 1, layer), st_s.at[1],
                  raw_s, q_s, f_s, kk_s, v_s, qt_s, kt_s, o_s, p_s, e_s, -1, ones_bd, head_mask)

    @pl.when(i == nt - 1)
    def _():
        sfin_ref[0] = st_s[...]


def _hg_state_to_blocks(s):
    b = s.shape[0]
    st = s.reshape(b, 2, 2, 4, HG_KEY, HG_VAL).transpose(0, 1, 2, 5, 3, 4)
    return st.reshape(b, 2, 2, HG_VAL, HG_HALF)


def _hg_state_from_blocks(st):
    b = st.shape[0]
    st = st.reshape(b, 2, 2, HG_VAL, 4, HG_KEY).transpose(0, 1, 2, 4, 5, 3)
    return st.reshape(b, 2, HG_HEADS, HG_KEY, HG_VAL)


def _hgrn_scan(hq, hff, hfb, hi, lb, s0, layer, row0, nseq, seqlen):
    nt = seqlen // HG_TILE
    base = row0 // HG_TILE
    fwd = pl.BlockSpec((HG_TILE, HG_KD), lambda b, i: (base + b * nt + i, 0))
    bwd = pl.BlockSpec((HG_TILE, HG_KD), lambda b, i: (base + b * nt + nt - 1 - i, 0))
    ofw = pl.BlockSpec((HG_TILE, HG_KD), lambda b, i: (b * nt + i, 0))
    obw = pl.BlockSpec((HG_TILE, HG_KD), lambda b, i: (b * nt + nt - 1 - i, 0))
    st = pl.BlockSpec((1, 2, 2, HG_VAL, HG_HALF), lambda b, i: (b, 0, 0, 0, 0))
    tile = lambda: pltpu.VMEM((HG_SLABS, HG_TILE, 128), F32)
    pairs = lambda: pltpu.VMEM((HG_PAIR_ROWS, HG_KD), F32)
    return pl.pallas_call(
        functools.partial(_hgrn_kernel, nt, layer),
        grid=(nseq, nt),
        in_specs=[fwd, fwd, fwd, bwd, bwd, bwd, pl.BlockSpec((2 * DEPTH, HG_KD), lambda b, i: (0, 0)), st],
        out_specs=[ofw, obw, st],
        out_shape=[jax.ShapeDtypeStruct((nseq * seqlen, HG_WIDTH), F32)] * 2
        + [jax.ShapeDtypeStruct((nseq, 2, 2, HG_VAL, HG_HALF), F32)],
        scratch_shapes=[pltpu.VMEM((2, 2, HG_VAL, HG_HALF), F32)] + [tile() for _ in range(8)] + [pairs(), pairs()],
        compiler_params=_cparams("arbitrary", "arbitrary"),
        name="hgrn_scan",
    )(hq, hff, hi, hq, hfb, hi, lb.reshape(2 * DEPTH, HG_KD), s0)


def _cd_out_kernel(x_ref, occ_ref, ocl_ref, ofc_ref, ofl_ref, obc_ref, obl_ref, hg_ref, go_ref, wout_ref, gt_ref,
                   o_ref, wa_s, wb_s):
    @pl.when(pl.program_id(0) == 0)
    def _():
        wa_s[...] = wout_ref[0:MLA_WIDTH, :].astype(BF16)
        wb_s[...] = wout_ref[MLA_WIDTH:, :].astype(BF16)

    ri = lax.broadcasted_iota(jnp.int32, (HG_HALF, HG_HALF), 0) // HG_VAL
    ci = lax.broadcasted_iota(jnp.int32, (HG_HALF, HG_HALF), 1) // HG_VAL
    ones_bd = jnp.where(ri == ci, 1.0, 0.0).astype(BF16)
    o = _ctx_or_lat(ofc_ref, ofl_ref) + _ctx_or_lat(obc_ref, obl_ref)
    sq = o * o
    hi = sq.astype(BF16)
    lo = (sq - hi.astype(F32)).astype(BF16)
    ms = jnp.concatenate(
        [jnp.dot(hi[:, l], ones_bd, preferred_element_type=F32) + jnp.dot(lo[:, l], ones_bd, preferred_element_type=F32)
         for l in (slice(0, HG_HALF), slice(HG_HALF, 2 * HG_HALF))], axis=1) * (1.0 / HG_VAL)
    hg = hg_ref[...]
    od = o * lax.rsqrt(ms + EPS) * go_ref[...] * (hg * jax.nn.sigmoid(hg))
    out = (jnp.dot(_ctx_or_lat(occ_ref, ocl_ref).astype(BF16), wa_s[...], preferred_element_type=F32)
           + jnp.dot(od.astype(BF16), wb_s[...], preferred_element_type=F32))
    o_ref[...] = x_ref[...] + gt_ref[0] * out


def _cd_out(x, oc, of, ob, hg, g_o, w_out, mod, layer):
    row = lambda n: pl.BlockSpec((ROW_BLK, n), lambda i: (i, 0))
    full = lambda a, b: pl.BlockSpec((a, b), lambda i: (0, 0))
    return pl.pallas_call(
        _cd_out_kernel,
        grid=(N_ROW_BLK,),
        in_specs=[row(D_MODEL)] + _split_specs(MLA_WIDTH) + _split_specs(HG_WIDTH) + _split_specs(HG_WIDTH)
        + [row(HG_WIDTH), full(1, HG_WIDTH), full(MLA_WIDTH + HG_WIDTH, D_MODEL), _mod_spec(layer, 2)],
        out_specs=row(D_MODEL),
        out_shape=jax.ShapeDtypeStruct((N_TOK, D_MODEL), F32),
        scratch_shapes=[pltpu.VMEM((MLA_WIDTH, D_MODEL), BF16), pltpu.VMEM((HG_WIDTH, D_MODEL), BF16)],
        compiler_params=_cparams("arbitrary"),
        name="cd_out",
    )(x, oc[0], oc[1], of[0], of[1], ob[0], ob[1], hg, jnp.tile(g_o, HG_HEADS).reshape(1, HG_WIDTH), w_out, mod)


N_PAIRS = 6
N_CLASSES = N_GROUPS * N_PAIRS
CLS_ROWS = 32
MOE_BM = 256
MOE_NBLK = N_TOK // MOE_BM + N_CLASSES
MOE_ROWS = MOE_NBLK * MOE_BM
PAIR_SLOTS = ((0, 1), (3, 1), (2, 1), (2, 0), (3, 0), (3, 2))


def _moe_route_kernel(x_ref, g_ref, sc_ref, sh_ref, rw_ref, rb_ref, h_ref, ti_ref, tw_ref, cnt_ref, base_s):
    @pl.when(pl.program_id(0) == 0)
    def _():
        base_s[...] = jnp.zeros_like(base_s)

    h = _norm_mod(x_ref[...], g_ref[...], sc_ref[0], sh_ref[0])
    logits = lax.dot_general(rw_ref[...], h, (((1,), (1,)), ((), ())), precision=HI,
                             preferred_element_type=F32)
    aff = jax.nn.sigmoid(logits)
    sel = aff + rb_ref[...]
    s = [sel[e:e + 1, :] for e in range(N_EXPERTS)]
    a = [aff[e:e + 1, :] for e in range(N_EXPERTS)]
    gs = []
    for g in range(N_GROUPS):
        m = s[4 * g:4 * g + 4]
        pairs = [m[i] + m[j] for i in range(4) for j in range(i + 1, 4)]
        gs.append(functools.reduce(jnp.maximum, pairs))
    gmax = functools.reduce(jnp.maximum, gs)
    taken = jnp.zeros_like(gmax) > 1.0
    gsel = []
    for g in range(N_GROUPS):
        hit = (gs[g] == gmax) & jnp.logical_not(taken)
        gsel.append(hit)
        taken = taken | hit
    e_lo = jnp.zeros(gmax.shape, jnp.int32)
    e_hi = jnp.zeros(gmax.shape, jnp.int32)
    a_lo = jnp.zeros_like(gmax)
    a_hi = jnp.zeros_like(gmax)
    nsel = jnp.zeros(gmax.shape, jnp.int32)
    for g in range(N_GROUPS):
        for i in range(4):
            e = 4 * g + i
            beat = jnp.zeros(gmax.shape, jnp.int32)
            for j in range(4):
                if j != i:
                    o = 4 * g + j
                    beat = beat + jnp.where((s[o] > s[e]) | ((s[o] == s[e]) & (j < i)), 1, 0)
            pick = gsel[g] & (beat < 2)
            is_first = pick & (nsel == 0)
            is_second = pick & (nsel == 1)
            e_lo = jnp.where(is_first, e, e_lo)
            a_lo = jnp.where(is_first, a[e], a_lo)
            e_hi = jnp.where(is_second, e, e_hi)
            a_hi = jnp.where(is_second, a[e], a_hi)
            nsel = nsel + jnp.where(pick, 1, 0)
    grp = e_lo // EXPERTS_PER_GROUP
    lo = e_lo - grp * EXPERTS_PER_GROUP
    hi = e_hi - grp * EXPERTS_PER_GROUP
    pair = ((lo * (7 - lo)) >> 1) + (hi - lo - 1)
    pair = jnp.where(pair == 1, 3, jnp.where(pair == 2, 4, jnp.where(pair == 3, 2, jnp.where(pair == 4, 1, pair))))
    cls = grp * N_PAIRS + pair
    wsum = a_lo + a_hi
    w_lo, w_hi = a_lo / wsum, a_hi / wsum
    onehot = (lax.broadcasted_iota(jnp.int32, (CLS_ROWS, ROW_BLK), 0) == cls).astype(F32)
    tt = lax.broadcasted_iota(jnp.int32, (ROW_BLK, ROW_BLK), 0) < lax.broadcasted_iota(jnp.int32, (ROW_BLK, ROW_BLK), 1)
    before = _dot(onehot, jnp.where(tt, 1.0, 0.0))
    base = base_s[...]
    rank = jnp.sum(onehot * (before + base[:, 0:1]), axis=0, keepdims=True).astype(jnp.int32)
    base = base + jnp.sum(onehot, axis=1, keepdims=True)
    base_s[...] = base
    cnt_ref[...] = base.astype(jnp.int32)
    ti_ref[0] = jnp.concatenate([cls, rank, e_lo, e_hi, jnp.zeros((4, ROW_BLK), jnp.int32)], axis=0)
    ident = lax.broadcasted_iota(jnp.int32, (ROW_BLK, ROW_BLK), 0) == lax.broadcasted_iota(jnp.int32, (ROW_BLK, ROW_BLK), 1)
    col = lambda r: jnp.sum(jnp.where(ident, r, 0.0), axis=1, keepdims=True)
    tw_ref[...] = jnp.where(lax.broadcasted_iota(jnp.int32, (ROW_BLK, 128), 1) < 64, col(w_lo), col(w_hi))
    h_ref[...] = h


def _moe_route(x, mod, g, router_w, router_b, layer):
    row = lambda n: pl.BlockSpec((ROW_BLK, n), lambda i: (i, 0))
    full = lambda a, b: pl.BlockSpec((a, b), lambda i: (0, 0))
    return pl.pallas_call(
        _moe_route_kernel,
        grid=(N_ROW_BLK,),
        in_specs=[row(D_MODEL), full(1, D_MODEL), _mod_spec(layer, 4), _mod_spec(layer, 3),
                  full(N_EXPERTS, D_MODEL), full(N_EXPERTS, 1)],
        out_specs=[row(D_MODEL), pl.BlockSpec((1, 8, ROW_BLK), lambda i: (i, 0, 0)), row(128), full(CLS_ROWS, 128)],
        out_shape=[jax.ShapeDtypeStruct((N_TOK, D_MODEL), F32),
                   jax.ShapeDtypeStruct((N_ROW_BLK, 8, ROW_BLK), jnp.int32),
                   jax.ShapeDtypeStruct((N_TOK, 128), F32),
                   jax.ShapeDtypeStruct((CLS_ROWS, 128), jnp.int32)],
        scratch_shapes=[pltpu.VMEM((CLS_ROWS, 128), F32)],
        compiler_params=_cparams("arbitrary"),
        name="moe_route",
    )(x, g.reshape(1, D_MODEL), mod, mod, router_w.T, router_b.reshape(N_EXPERTS, 1))


def _moe_experts_kernel(elo_ref, ehi_ref, nblk_ref, swap_ref, hs_ref, ws_ref, wgl_ref, wul_ref, wdl_ref,
                        wgh_ref, wuh_ref, wdh_ref, y_ref):
    del elo_ref, ehi_ref
    j = pl.program_id(0)

    @pl.when(j < nblk_ref[0])
    def _():
        h = hs_ref[...].astype(BF16)
        swap = swap_ref[j] == 1
        gate_a = jnp.where(swap, ws_ref[:, 64:65], ws_ref[:, 0:1])
        gate_b = jnp.where(swap, ws_ref[:, 0:1], ws_ref[:, 64:65])
        acc = None
        for wg, wu, wd, gate in ((wgl_ref, wul_ref, wdl_ref, gate_a), (wgh_ref, wuh_ref, wdh_ref, gate_b)):
            g = _dot(h, wg[0, 0])
            u = _dot(h, wu[0, 0])
            hid = g * jax.nn.sigmoid(g) * u * gate
            y = _dot(hid, wd[0, 0])
            acc = y if acc is None else acc + y
        y_ref[...] = acc

    @pl.when(j >= nblk_ref[0])
    def _():
        y_ref[...] = jnp.zeros_like(y_ref)


def _moe_experts(hs, ws, blk_elo, blk_ehi, nblk, swap, w_gate, w_up, w_down, layer):
    last = lambda j, nb: jnp.minimum(j, nb[0] - 1)
    wspec = lambda a, b, which: pl.BlockSpec(
        (1, 1, a, b), lambda j, elo, ehi, nb, sw: (layer, (elo, ehi)[which][last(j, nb)], 0, 0))
    grid_spec = pltpu.PrefetchScalarGridSpec(
        num_scalar_prefetch=4,
        grid=(MOE_NBLK,),
        in_specs=[pl.BlockSpec((MOE_BM, D_MODEL), lambda j, elo, ehi, nb, sw: (last(j, nb), 0)),
                  pl.BlockSpec((MOE_BM, 128), lambda j, elo, ehi, nb, sw: (last(j, nb), 0)),
                  wspec(D_MODEL, D_FF, 0), wspec(D_MODEL, D_FF, 0), wspec(D_FF, D_MODEL, 0),
                  wspec(D_MODEL, D_FF, 1), wspec(D_MODEL, D_FF, 1), wspec(D_FF, D_MODEL, 1)],
        out_specs=pl.BlockSpec((MOE_BM, D_MODEL), lambda j, *_: (j, 0)),
    )
    return pl.pallas_call(
        _moe_experts_kernel,
        grid_spec=grid_spec,
        out_shape=jax.ShapeDtypeStruct((MOE_ROWS, D_MODEL), F32),
        compiler_params=_cparams("arbitrary"),
        name="moe_experts",
    )(blk_elo, blk_ehi, nblk, swap, hs, ws, w_gate, w_up, w_down, w_gate, w_up, w_down)


SC_ROWS = 64


def _sc_gather_rows(table, idx):
    info = plsc.get_sparse_core_info()
    nc, nw = info.num_cores, info.num_cores * info.num_subcores
    b, d = idx.shape[0], table.shape[1]
    per_w = b // nw
    assert per_w * nw == b and per_w % SC_ROWS == 0
    mesh = plsc.VectorSubcoreMesh(core_axis_name="c", subcore_axis_name="s")

    @functools.partial(
        pl.kernel, mesh=mesh, out_type=jax.ShapeDtypeStruct((b, d), table.dtype),
        scratch_types=[pltpu.VMEM((SC_ROWS,), jnp.int32), pltpu.VMEM((SC_ROWS, d), table.dtype),
                       pltpu.SemaphoreType.DMA])
    def gather(table_hbm, idx_hbm, out_hbm, idx_v, rows_v, sem):
        wid = lax.axis_index("s") * nc + lax.axis_index("c")
        for c in range(per_w // SC_ROWS):
            base = wid * per_w + c * SC_ROWS
            pltpu.sync_copy(idx_hbm.at[pl.ds(base, SC_ROWS)], idx_v)
            pltpu.async_copy(table_hbm.at[idx_v], rows_v, sem).wait()
            pltpu.sync_copy(rows_v, out_hbm.at[pl.ds(base, SC_ROWS)])

    return gather(table, idx)


def _moe_residual_kernel(x_ref, y_ref, gt_ref, o_ref):
    o_ref[...] = x_ref[...] + gt_ref[0] * y_ref[...]


def _moe_combine(x, y_sorted, pos, mod, layer):
    row = pl.BlockSpec((ROW_BLK, D_MODEL), lambda i: (i, 0))
    return pl.pallas_call(
        _moe_residual_kernel,
        grid=(N_ROW_BLK,),
        in_specs=[row, row, _mod_spec(layer, 5)],
        out_specs=row,
        out_shape=jax.ShapeDtypeStruct((N_TOK, D_MODEL), F32),
        compiler_params=_cparams("arbitrary"),
        name="moe_residual",
    )(x, _sc_gather_rows(y_sorted, pos), mod)


def _moe(x, mod, g, router_w, router_b, w_gate, w_up, w_down, layer):
    h, info, wtok, counts = _moe_route(x, mod, g, router_w, router_b, layer)
    cls = info[:, 0, :].reshape(N_TOK)
    rank = info[:, 1, :].reshape(N_TOK)
    cnt = counts[:N_CLASSES, 0]
    nb = (cnt + MOE_BM - 1) // MOE_BM
    ends = jnp.cumsum(nb)
    starts = ends - nb
    pos = ((starts * MOE_BM)[cls] + rank).astype(jnp.int32)
    blk = jnp.arange(MOE_NBLK, dtype=jnp.int32)
    blk_cls = jnp.minimum(jnp.sum((blk[:, None] >= ends[None, :]).astype(jnp.int32), axis=1), N_CLASSES - 1)
    slot_a = jnp.asarray([s[0] for s in PAIR_SLOTS], jnp.int32)
    slot_b = jnp.asarray([s[1] for s in PAIR_SLOTS], jnp.int32)
    grp = blk_cls // N_PAIRS
    blk_ea = (grp * EXPERTS_PER_GROUP + slot_a[blk_cls % N_PAIRS]).astype(jnp.int32)
    blk_eb = (grp * EXPERTS_PER_GROUP + slot_b[blk_cls % N_PAIRS]).astype(jnp.int32)
    swap = (blk_ea > blk_eb).astype(jnp.int32)
    nblk = ends[-1:].astype(jnp.int32)
    perm = (jnp.arange(MOE_ROWS, dtype=jnp.int32) % N_TOK).at[pos].set(jnp.arange(N_TOK, dtype=jnp.int32))
    hs = _sc_gather_rows(h, perm)
    ws = _sc_gather_rows(wtok, perm)
    y_sorted = _moe_experts(hs, ws, blk_ea, blk_eb, nblk, swap, w_gate, w_up, w_down, layer)
    return _moe_combine(x, y_sorted, pos, mod, layer)


def _final_norm_kernel(x_ref, g_ref, o_ref):
    o_ref[...] = _rms(x_ref[...]) * g_ref[...]


def _final_norm(x, g, row0, rows):
    base = row0 // ROW_BLK
    return pl.pallas_call(
        _final_norm_kernel,
        grid=(rows // ROW_BLK,),
        in_specs=[pl.BlockSpec((ROW_BLK, D_MODEL), lambda i: (base + i, 0)), pl.BlockSpec((1, D_MODEL), lambda i: (0, 0))],
        out_specs=pl.BlockSpec((ROW_BLK, D_MODEL), lambda i: (i, 0)),
        out_shape=jax.ShapeDtypeStruct((rows, D_MODEL), F32),
        compiler_params=_cparams("arbitrary"),
        name="final_norm",
    )(x, g.reshape(1, D_MODEL))


def kernel(x_prompt, x_sample, cache_attn_k, cache_attn_v, state_ssm_re, state_ssm_im, cache_mla_ckv, cache_mla_kpe,
           state_hgrn, c, c_ctx, w_mod, b_mod, g_mix, g_ffn, g_final, router_w, router_b, moe_w_gate, moe_w_up,
           moe_w_down, ab_w_in, ab_sink, s5_lam_re, s5_lam_im, s5_log_dt, s5_b_re, s5_b_im, s5_c_re, s5_c_im, s5_d,
           s5_w_glu, ab_w_out, cd_w_in, mla_g_q, mla_w_q_up, mla_g_kv, mla_w_kv_up, hg_lower_bounds, hg_g_o, cd_w_out):
    x = jnp.concatenate([x_prompt.reshape(N_CTX_TOK, D_MODEL), x_sample.reshape(N_LAT_TOK, D_MODEL)], axis=0)
    cond = jnp.zeros((MOD_ROWS, D_MODEL), F32).at[0].set(c_ctx).at[1:1 + DEC_BATCH].set(c)
    mod = _modulation(cond, w_mod, b_mod)
    keep = ([], [], [], [], [], [], [])
    for l in range(DEPTH):
        j = l // 2
        if l % 2 == 0:
            q, k, v, u = _ab_in(x, mod, g_mix[l], ab_w_in[j], l)
            o_a = _attention_a(q, k, v, cache_attn_k[:, j].reshape(DEC_BATCH, PAST_LEN, A_KV_WIDTH),
                               cache_attn_v[:, j].reshape(DEC_BATCH, PAST_LEN, A_KV_WIDTH), ab_sink[j])
            y, fin_re, fin_im = _s5_scan(u, state_ssm_re[:, j], state_ssm_im[:, j], s5_lam_re[j], s5_lam_im[j],
                                         s5_log_dt[j], s5_b_re[j], s5_b_im[j], s5_c_re[j], s5_c_im[j])
            x = _ab_out(x, o_a, y, u, s5_d[j], s5_w_glu[j], ab_w_out[j], mod, l)
            keep[0].append(k[:N_CTX_TOK].reshape(BATCH, SEQ, A_KV_HEADS, HEAD_DIM))
            keep[1].append(v[:N_CTX_TOK].reshape(BATCH, SEQ, A_KV_HEADS, HEAD_DIM))
            keep[2].append(fin_re)
            keep[3].append(fin_im)
        else:
            qn, qr, ckv, kpe, kn, vm, hq, hff, hfb, hi, hg = _cd_in(
                x, mod, g_mix[l], cd_w_in[j], mla_g_q[j], mla_w_q_up[j], mla_g_kv[j], mla_w_kv_up[j], l)
            kcn, vc = _mla_cache_kv(cache_mla_ckv[:, j].reshape(DEC_BATCH * PAST_LEN, MLA_KV_RANK), mla_w_kv_up[j])
            o_c = _attention_mla(qn, qr, kn, kpe, vm, kcn, cache_mla_kpe[:, j], vc)
            s0_ctx = jnp.zeros((BATCH, 2, 2, HG_VAL, HG_HALF), F32)
            of_c, ob_c, s_fin = _hgrn_scan(hq, hff, hfb, hi, hg_lower_bounds, s0_ctx, l, 0, BATCH, SEQ)
            of_l, ob_l, _ = _hgrn_scan(hq, hff, hfb, hi, hg_lower_bounds, _hg_state_to_blocks(state_hgrn[:, j]), l,
                                       N_CTX_TOK, DEC_BATCH, DEC_SEQ)
            x = _cd_out(x, o_c, (of_c, of_l), (ob_c, ob_l), hg, hg_g_o[j], cd_w_out[j], mod, l)
            keep[4].append(ckv[:N_CTX_TOK].reshape(BATCH, SEQ, MLA_KV_RANK))
            keep[5].append(kpe[:N_CTX_TOK, :MLA_ROPE].reshape(BATCH, SEQ, MLA_ROPE))
            keep[6].append(_hg_state_from_blocks(s_fin))
        x = _moe(x, mod, g_ffn[l], router_w, router_b, moe_w_gate, moe_w_up, moe_w_down, l)
    y_ctx = _final_norm(x, g_final, 0, N_CTX_TOK)
    y_lat = _final_norm(x, g_final, N_CTX_TOK, N_LAT_TOK)
    return (y_ctx.reshape(BATCH, SEQ, D_MODEL), y_lat.reshape(DEC_BATCH, DEC_SEQ, D_MODEL),
            jnp.stack(keep[0], 1), jnp.stack(keep[1], 1), jnp.stack(keep[2], 1), jnp.stack(keep[3], 1),
            jnp.stack(keep[4], 1), jnp.stack(keep[5], 1), jnp.stack(keep[6], 1))
```

```python
import functools
import math

import numpy as np
import jax
import jax.numpy as jnp
from jax import lax
from jax.experimental import pallas as pl
from jax.experimental.pallas import tpu as pltpu
from jax.experimental.pallas import tpu_sc as plsc

F32 = jnp.float32
BF16 = jnp.bfloat16

D_MODEL = 1024
BATCH = 16
SEQ = 256
DEPTH = 2
DEC_BATCH = 4
DEC_SEQ = 1024
PAST_LEN = 512
GRID_W = 64
N_MOD = 6
EPS = 1e-6
NEG_INF = -1e30
ROPE_BASE = 10000.0
HEAD_DIM = 64
A_HEADS = 8
A_KV_HEADS = 2
A_WINDOW = 128
A_WIDTH = A_HEADS * HEAD_DIM
A_KV_WIDTH = A_KV_HEADS * HEAD_DIM
S5_WIDTH = D_MODEL // 2
S5_GROUP = 16
S5_GROUPS = S5_WIDTH // S5_GROUP
S5_STATE = 64
MLA_HEADS = 8
MLA_Q_RANK = D_MODEL // 4
MLA_KV_RANK = D_MODEL // 8
MLA_NOPE = 64
MLA_ROPE = 32
MLA_V = 64
MLA_WIDTH = MLA_HEADS * MLA_V
HG_HEADS = 8
HG_KEY = 64
HG_VAL = 64
HG_KD = HG_HEADS * HG_KEY
HG_WIDTH = HG_HEADS * HG_VAL
HG_CHUNK = 16
N_EXPERTS = 16
N_GROUPS = 4
EXPERTS_PER_GROUP = N_EXPERTS // N_GROUPS
D_FF = D_MODEL // 2

N_CTX_TOK = BATCH * SEQ
N_LAT_TOK = DEC_BATCH * DEC_SEQ
N_TOK = N_CTX_TOK + N_LAT_TOK
ROW_BLK = 256
N_ROW_BLK = N_TOK // ROW_BLK
N_CTX_BLK = N_CTX_TOK // ROW_BLK
LAT_BLK_PER_SEQ = DEC_SEQ // ROW_BLK
MOD_ROWS = 8
VMEM_LIMIT = 56 * 1024 * 1024


def _cparams(*sem):
    return pltpu.CompilerParams(dimension_semantics=sem, vmem_limit_bytes=VMEM_LIMIT)


def _mod_group(i):
    return jnp.where(i < N_CTX_BLK, 0, 1 + (i - N_CTX_BLK) // LAT_BLK_PER_SEQ)


def _mod_spec(layer, which):
    return pl.BlockSpec((1, 1, D_MODEL), lambda i: ((layer * MOD_ROWS + _mod_group(i)) * N_MOD + which, 0, 0))


def _rope_blk(i):
    return jnp.where(i < N_CTX_BLK, 0, 1 + (i - N_CTX_BLK) % LAT_BLK_PER_SEQ)


def _rope_tables(rot_dim, only=None):
    n_freq = rot_dim // 4
    t = np.arange(DEC_SEQ)
    rows = (t // GRID_W).astype(np.float32)
    cols = (t % GRID_W).astype(np.float32)
    inv = (np.float32(ROPE_BASE) ** (-np.arange(n_freq, dtype=np.float32) / np.float32(n_freq))).astype(np.float32)
    ang_r = rows[:, None] * inv[None, :]
    ang_c = cols[:, None] * inv[None, :]
    ang = np.concatenate([ang_r, ang_r, ang_c, ang_c], axis=-1).astype(np.float32)
    reps = 128 // rot_dim
    cos = np.tile(np.cos(ang), (1, reps)).astype(np.float32)
    sin = np.tile(np.sin(ang), (1, reps)).astype(np.float32)
    lane = np.arange(128)
    first = (lane % (2 * n_freq)) < n_freq
    sin_a = np.where(first[None, :], -sin, 0.0).astype(np.float32)
    sin_b = np.where(first[None, :], 0.0, sin).astype(np.float32)
    if only is not None:
        keep = ((lane >= only[0]) & (lane < only[1]))[None, :]
        cos, sin_a, sin_b = np.where(keep, cos, 1.0), np.where(keep, sin_a, 0.0), np.where(keep, sin_b, 0.0)
        cos, sin_a, sin_b = cos.astype(np.float32), sin_a.astype(np.float32), sin_b.astype(np.float32)
    ident = np.zeros((ROW_BLK, 128), np.float32)
    cos = np.concatenate([ident + 1.0, cos], axis=0)
    sin_a = np.concatenate([ident, sin_a], axis=0)
    sin_b = np.concatenate([ident, sin_b], axis=0)
    return jnp.asarray(cos), jnp.asarray(sin_a), jnp.asarray(sin_b)


def _apply_rope(x, cos, sin_a, sin_b, quarter):
    outs = []
    for j in range(x.shape[1] // 128):
        xt = x[:, 128 * j:128 * (j + 1)]
        up = pltpu.roll(xt, 128 - quarter, axis=1)
        dn = pltpu.roll(xt, quarter, axis=1)
        outs.append(xt * cos + up * sin_a + dn * sin_b)
    return outs[0] if len(outs) == 1 else jnp.concatenate(outs, axis=1)


def _rms(x):
    return x * lax.rsqrt(jnp.mean(x * x, axis=-1, keepdims=True) + EPS)


def _norm_mod(x, g, sc, sh):
    return _rms(x) * g * (1.0 + sc) + sh


def _dot(a, b):
    return jnp.dot(a.astype(BF16), b.astype(BF16), preferred_element_type=F32)


def _dot_nt(a, b):
    return lax.dot_general(a.astype(BF16), b.astype(BF16), (((1,), (1,)), ((), ())), preferred_element_type=F32)


def _mod_kernel(cond_ref, w_ref, b_ref, o_ref):
    c = cond_ref[...]
    s = c * jax.nn.sigmoid(c)
    o_ref[0] = _dot(s, w_ref[0]) + b_ref[0]


def _modulation(cond, w_mod, b_mod):
    nb = 1024
    out = pl.pallas_call(
        _mod_kernel,
        grid=(DEPTH, N_MOD * D_MODEL // nb),
        in_specs=[pl.BlockSpec((MOD_ROWS, D_MODEL), lambda l, n: (0, 0)),
                  pl.BlockSpec((1, D_MODEL, nb), lambda l, n: (l, 0, n)),
                  pl.BlockSpec((1, 1, nb), lambda l, n: (l, 0, n))],
        out_specs=pl.BlockSpec((1, MOD_ROWS, nb), lambda l, n: (l, 0, n)),
        out_shape=jax.ShapeDtypeStruct((DEPTH, MOD_ROWS, N_MOD * D_MODEL), F32),
        compiler_params=_cparams("arbitrary", "arbitrary"),
        name="modulation",
    )(cond, w_mod, b_mod.reshape(DEPTH, 1, N_MOD * D_MODEL))
    return out.reshape(DEPTH * MOD_ROWS * N_MOD, 1, D_MODEL)


def _ab_in_kernel(x_ref, g_ref, sc_ref, sh_ref, w_ref, cos_ref, sa_ref, sb_ref,
                  q_ref, k_ref, v_ref, u_ref, wq_s, wk_s, wv_s, wu_s):
    @pl.when(pl.program_id(0) == 0)
    def _():
        wq_s[...] = w_ref[:, 0:A_WIDTH].astype(BF16)
        wk_s[...] = w_ref[:, A_WIDTH:A_WIDTH + A_KV_WIDTH].astype(BF16)
        wv_s[...] = w_ref[:, A_WIDTH + A_KV_WIDTH:A_WIDTH + 2 * A_KV_WIDTH].astype(BF16)
        wu_s[...] = w_ref[:, A_WIDTH + 2 * A_KV_WIDTH:].astype(BF16)

    h = _norm_mod(x_ref[...], g_ref[...], sc_ref[0], sh_ref[0]).astype(BF16)
    cos, sa, sb = cos_ref[...], sa_ref[...], sb_ref[...]
    q = jnp.dot(h, wq_s[...], preferred_element_type=F32)
    q_ref[...] = _apply_rope(q, cos, sa, sb, HEAD_DIM // 4).astype(q_ref.dtype)
    k = jnp.dot(h, wk_s[...], preferred_element_type=F32)
    k_ref[...] = _apply_rope(k, cos, sa, sb, HEAD_DIM // 4)
    v_ref[...] = jnp.dot(h, wv_s[...], preferred_element_type=F32)
    u_ref[...] = jnp.dot(h, wu_s[...], preferred_element_type=F32)


def _ab_in(x, mod, g, w, layer):
    cos, sa, sb = _rope_tables(HEAD_DIM)
    d_in = w.shape[1]
    row = lambda n: pl.BlockSpec((ROW_BLK, n), lambda i: (i, 0))
    rope = pl.BlockSpec((ROW_BLK, 128), lambda i: (_rope_blk(i), 0))
    return pl.pallas_call(
        _ab_in_kernel,
        grid=(N_ROW_BLK,),
        in_specs=[row(D_MODEL), pl.BlockSpec((1, D_MODEL), lambda i: (0, 0)),
                  _mod_spec(layer, 1), _mod_spec(layer, 0),
                  pl.BlockSpec((D_MODEL, d_in), lambda i: (0, 0)), rope, rope, rope],
        out_specs=[row(A_WIDTH), row(A_KV_WIDTH), row(A_KV_WIDTH), row(S5_WIDTH)],
        out_shape=[jax.ShapeDtypeStruct((N_TOK, A_WIDTH), BF16), jax.ShapeDtypeStruct((N_TOK, A_KV_WIDTH), F32),
                   jax.ShapeDtypeStruct((N_TOK, A_KV_WIDTH), F32), jax.ShapeDtypeStruct((N_TOK, S5_WIDTH), F32)],
        scratch_shapes=[pltpu.VMEM((D_MODEL, A_WIDTH), BF16), pltpu.VMEM((D_MODEL, A_KV_WIDTH), BF16),
                        pltpu.VMEM((D_MODEL, A_KV_WIDTH), BF16), pltpu.VMEM((D_MODEL, S5_WIDTH), BF16)],
        compiler_params=_cparams("arbitrary"),
        name="ab_in",
    )(x, g.reshape(1, D_MODEL), mod, mod, w, cos, sa, sb)


def _softmax_pv(s_list, v_list, sink):
    m = functools.reduce(jnp.maximum, [jnp.max(s, axis=-1, keepdims=True) for s in s_list])
    if sink is not None:
        m = jnp.maximum(m, sink)
    ps = [jnp.exp(s - m) for s in s_list]
    l = functools.reduce(jnp.add, [jnp.sum(p, axis=-1, keepdims=True) for p in ps])
    if sink is not None:
        l = l + jnp.exp(sink - m)
    o = functools.reduce(jnp.add, [_dot(p, v) for p, v in zip(ps, v_list)])
    return o / l


def _gqa_with_sink(sink_ref, q, k_of, v_of, bias, o_ref):
    g = A_HEADS // A_KV_HEADS
    r = q.shape[0]
    outs = [None] * A_HEADS
    for kh in range(A_KV_HEADS):
        heads = range(g * kh, g * (kh + 1))
        qg = jnp.concatenate([q[:, HEAD_DIM * h:HEAD_DIM * (h + 1)] for h in heads], axis=0)
        sink = jnp.concatenate([jnp.full((r, 1), sink_ref[h], F32) for h in heads], axis=0)
        s = _dot_nt(qg, k_of(kh))
        if bias is not None:
            s = s + jnp.concatenate([bias] * g, axis=0)
        m = jnp.maximum(jnp.max(s, axis=-1, keepdims=True), sink)
        p = jnp.exp(s - m)
        l = jnp.sum(p, axis=-1, keepdims=True) + jnp.exp(sink - m)
        o = _dot(p, v_of(kh)) / l
        for n, h in enumerate(heads):
            outs[h] = o[r * n:r * (n + 1)]
    o_ref[...] = jnp.concatenate(outs, axis=1).astype(o_ref.dtype)


def _attn_ctx_kernel(sink_ref, q_ref, k_ref, v_ref, o_ref):
    scale = HEAD_DIM ** -0.5
    g = A_HEADS // A_KV_HEADS
    outs = []
    for h in range(A_HEADS):
        kh = h // g
        q = q_ref[:, HEAD_DIM * h:HEAD_DIM * (h + 1)]
        k = k_ref[:, HEAD_DIM * kh:HEAD_DIM * (kh + 1)]
        v = v_ref[:, HEAD_DIM * kh:HEAD_DIM * (kh + 1)]
        s = _dot_nt(q, k) * scale
        outs.append(_softmax_pv([s], [v], sink_ref[h]))
    o_ref[...] = jnp.concatenate(outs, axis=1).astype(o_ref.dtype)


def _attn_lat_kernel(sink_ref, q_ref, kp_ref, kc_ref, kn_ref, vp_ref, vc_ref, vn_ref, kx_ref, vx_ref, o_ref):
    n = pl.program_id(1)
    nb = DEC_SEQ // A_WINDOW
    i = lax.broadcasted_iota(jnp.int32, (A_WINDOW, A_WINDOW), 0)
    j = lax.broadcasted_iota(jnp.int32, (A_WINDOW, A_WINDOW), 1)
    zero = jnp.zeros((A_WINDOW, A_WINDOW), F32)
    bias = jnp.concatenate([jnp.where((j >= i) & (n > 0), 0.0, NEG_INF), zero,
                            jnp.where((j <= i) & (n < nb - 1), 0.0, NEG_INF),
                            jnp.zeros((A_WINDOW, PAST_LEN), F32)], axis=1)

    def rows(p_ref, c_ref, n_ref, x_ref):
        def of(kh):
            sl = slice(HEAD_DIM * kh, HEAD_DIM * (kh + 1))
            return jnp.concatenate([p_ref[:, sl], c_ref[:, sl], n_ref[:, sl], x_ref[0, :, sl]], axis=0)
        return of

    _gqa_with_sink(sink_ref, q_ref[...] * HEAD_DIM ** -0.5, rows(kp_ref, kc_ref, kn_ref, kx_ref),
                   rows(vp_ref, vc_ref, vn_ref, vx_ref), bias, o_ref)


def _attention_a(q, k, v, cache_k, cache_v, sink):
    smem = pl.BlockSpec(memory_space=pltpu.SMEM)
    o = pl.pallas_call(
        _attn_ctx_kernel,
        grid=(BATCH,),
        in_specs=[smem, pl.BlockSpec((SEQ, A_WIDTH), lambda b: (b, 0)),
                  pl.BlockSpec((SEQ, A_KV_WIDTH), lambda b: (b, 0)), pl.BlockSpec((SEQ, A_KV_WIDTH), lambda b: (b, 0))],
        out_specs=pl.BlockSpec((SEQ, A_WIDTH), lambda b: (b, 0)),
        out_shape=jax.ShapeDtypeStruct((N_CTX_TOK, A_WIDTH), BF16),
        compiler_params=_cparams("arbitrary"),
        name="attn_a_ctx",
    )(sink, q, k, v)
    nb = DEC_SEQ // A_WINDOW
    base = N_CTX_TOK // A_WINDOW
    cur = lambda b, n: (base + b * nb + n, 0)
    prev = lambda b, n: (base + b * nb + jnp.maximum(n - 1, 0), 0)
    nxt = lambda b, n: (base + b * nb + jnp.minimum(n + 1, nb - 1), 0)
    kv = lambda f: pl.BlockSpec((A_WINDOW, A_KV_WIDTH), f)
    cache = pl.BlockSpec((1, PAST_LEN, A_KV_WIDTH), lambda b, n: (b, 0, 0))
    o_lat = pl.pallas_call(
        _attn_lat_kernel,
        grid=(DEC_BATCH, nb),
        in_specs=[smem, pl.BlockSpec((A_WINDOW, A_WIDTH), cur), kv(prev), kv(cur), kv(nxt), kv(prev), kv(cur), kv(nxt),
                  cache, cache],
        out_specs=pl.BlockSpec((A_WINDOW, A_WIDTH), lambda b, n: (b * nb + n, 0)),
        out_shape=jax.ShapeDtypeStruct((N_LAT_TOK, A_WIDTH), BF16),
        compiler_params=_cparams("arbitrary", "arbitrary"),
        name="attn_a_lat",
    )(sink, q, k, k, k, v, v, v, cache_k, cache_v)
    return o, o_lat


S5_CHUNK = 16
S5_OCT = 128 // S5_GROUP
S5_NOCT = S5_GROUPS // S5_OCT
S5_K = S5_CHUNK * 128
S5_PART = S5_OCT * S5_STATE
S5_SW = 4 * S5_PART
S5_ROWS_CTX = BATCH * SEQ // S5_CHUNK
S5_ROWS_LAT = DEC_BATCH * DEC_SEQ // S5_CHUNK
S5_ROWS = S5_ROWS_CTX + S5_ROWS_LAT
S5_NB = 4
HI = lax.Precision.HIGHEST


def _s5_disc_kernel(lr_ref, li_ref, ldt_ref, ar_ref, ai_ref, zr_ref, zi_ref):
    lr, li = lr_ref[...], li_ref[...]
    dt = jnp.exp(ldt_ref[...])
    mag = jnp.exp(lr * dt)
    ar, ai = mag * jnp.cos(li * dt), mag * jnp.sin(li * dt)
    den = lr * lr + li * li
    ar_ref[...] = ar
    ai_ref[...] = ai
    zr_ref[...] = ((ar - 1.0) * lr + ai * li) / den
    zi_ref[...] = (ai * lr - (ar - 1.0) * li) / den


def _cmul(xr, xi, yr, yi):
    return xr * yr - xi * yi, xr * yi + xi * yr


def _dot_nt_hi(a, b):
    return lax.dot_general(a, b, (((1,), (1,)), ((), ())), precision=HI, preferred_element_type=F32)


def _s5_prep_kernel(ar_ref, ai_ref, zr_ref, zi_ref, btr_ref, bti_ref, ctr_ref, cti_ref,
                    m_ref, win_ref, wout_ref, a16_ref, pw_s, w_s, k_s):
    t = pl.program_id(1)
    npw = S5_CHUNK + 1
    blk = lambda j: pl.ds(pl.multiple_of(j * 128, 128), 128)

    @pl.when(t == 0)
    def _():
        kd = []
        for d in range(2):
            ar, ai = ar_ref[0, d], ai_ref[0, d]
            pr, pi = jnp.ones_like(ar), jnp.zeros_like(ar)
            bbr, bbi = _cmul(zr_ref[0, d], zi_ref[0, d], btr_ref[0, d], bti_ref[0, d])
            for j in range(npw):
                pw_s[d, 0, j:j + 1, :] = pr
                pw_s[d, 1, j:j + 1, :] = pi
                if j < S5_CHUNK:
                    wr, wi = _cmul(pr, pi, bbr, bbi)
                    w_s[d, 0, 128 * j:128 * (j + 1), :] = wr.astype(BF16)
                    w_s[d, 1, 128 * j:128 * (j + 1), :] = wi.astype(BF16)
                pr, pi = _cmul(pr, pi, ar, ai)
            kd.append(_dot_nt(w_s[d, 0], ctr_ref[0, d]) - _dot_nt(w_s[d, 1], cti_ref[0, d]))
        for jj in range(2 * S5_CHUNK - 1):
            j = jj - (S5_CHUNK - 1)
            if j > 0:
                k = kd[0][128 * j:128 * (j + 1)]
            elif j < 0:
                k = kd[1][128 * -j:128 * (1 - j)]
            else:
                k = kd[0][0:128] + kd[1][0:128]
            k_s[128 * jj:128 * (jj + 1), :] = k
        a16_ref[0] = jnp.concatenate([pw_s[0, 0, S5_CHUNK:npw, :], pw_s[1, 0, S5_CHUNK:npw, :],
                                      pw_s[0, 1, S5_CHUNK:npw, :], pw_s[1, 1, S5_CHUNK:npw, :]], axis=1)

    for tp in range(S5_CHUNK):
        m_ref[0, :, 128 * tp:128 * (tp + 1)] = k_s[blk(S5_CHUNK - 1 + tp - t), :].astype(BF16)

    def power(d, e):
        return pw_s[d, 0, pl.ds(e, 1), :], pw_s[d, 1, pl.ds(e, 1), :]

    for d in range(2):
        j = (S5_CHUNK - 1 - t) if d == 0 else t
        win_ref[0, :, S5_PART * d:S5_PART * (d + 1)] = w_s[d, 0, blk(j), :]
        win_ref[0, :, S5_PART * (2 + d):S5_PART * (3 + d)] = w_s[d, 1, blk(j), :]
        er, ei = _cmul(*power(d, (t + 1) if d == 0 else (S5_CHUNK - t)), ctr_ref[0, d], cti_ref[0, d])
        wout_ref[0, :, S5_PART * d:S5_PART * (d + 1)] = er.astype(BF16)
        wout_ref[0, :, S5_PART * (2 + d):S5_PART * (3 + d)] = (-ei).astype(BF16)


def _s5_main_kernel(u_ref, win_ref, m_ref, wout_ref, a16_ref, h0_ref, y_ref, hfin_ref, uo_s, x_s, hs_s):
    s = pl.program_id(1)
    nq = S5_PART // 128

    @pl.when(s == 0)
    def _():
        for t in range(S5_CHUNK):
            uo_s[:, 128 * t:128 * (t + 1)] = u_ref[pl.ds(t, S5_ROWS, stride=S5_CHUNK), :].astype(BF16)

    @pl.when(s < S5_NB)
    def _():
        x = jnp.dot(uo_s[...], win_ref[0], preferred_element_type=F32)
        for q in range(nq):
            x_s[s * nq + q] = x[:, 128 * q:128 * (q + 1)]

    @pl.when(s == S5_NB - 1)
    def _():
        def run(row0, nb, nc, h):
            for c in range(nc):
                for d in range(2):
                    cc = c if d == 0 else nc - 1 - c
                    rows = pl.ds(row0 + cc, nb, stride=nc)
                    for q in range(nq):
                        kr, ki = d * nq + q, (2 + d) * nq + q
                        hr, hi = h[d][0][q], h[d][1][q]
                        hs_s[kr, rows, :] = hr
                        hs_s[ki, rows, :] = hi
                        ar = a16_ref[0, :, 128 * kr:128 * (kr + 1)]
                        ai = a16_ref[0, :, 128 * ki:128 * (ki + 1)]
                        h[d][0][q] = ar * hr - ai * hi + x_s[kr, rows, :]
                        h[d][1][q] = ar * hi + ai * hr + x_s[ki, rows, :]
            return h

        zero = jnp.zeros((BATCH, 128), F32)
        fin = run(0, BATCH, SEQ // S5_CHUNK, [[[zero] * nq, [zero] * nq] for _ in range(2)])
        for d in range(2):
            for ri in range(2):
                for q in range(nq):
                    k = (2 * ri + d) * nq + q
                    hfin_ref[0, :, 128 * k:128 * (k + 1)] = fin[d][ri][q]
        h0 = [[[h0_ref[0, :, 128 * ((2 * ri + d) * nq + q):128 * ((2 * ri + d) * nq + q + 1)] for q in range(nq)]
               for ri in range(2)] for d in range(2)]
        run(S5_ROWS_CTX, DEC_BATCH, DEC_SEQ // S5_CHUNK, h0)

    @pl.when(s >= S5_NB)
    def _():
        hs = jnp.concatenate([hs_s[k] for k in range(4 * nq)], axis=1).astype(BF16)
        y = (jnp.dot(uo_s[...], m_ref[0], preferred_element_type=F32)
             + lax.dot_general(hs, wout_ref[0], (((1,), (1,)), ((), ())), preferred_element_type=F32))
        for q in range(nq):
            t = (s - S5_NB) * nq + q
            y_ref[pl.ds(t, S5_ROWS, stride=S5_CHUNK), :] = y[:, 128 * q:128 * (q + 1)]


def _s5_octets(t, lanes):
    return t.reshape(2, S5_NOCT, 1, S5_OCT * lanes).transpose(1, 0, 2, 3)


def _s5_blockdiag(t):
    a, n = t.shape[2], t.shape[3]
    t = t.reshape(2, S5_NOCT, S5_OCT, a, n)
    bd = jnp.einsum('dogan,gh->dogahn', t, jnp.eye(S5_OCT, dtype=t.dtype))
    return bd.reshape(2, S5_NOCT, S5_OCT * a, S5_OCT * n).transpose(1, 0, 2, 3)


def _s5_state_to_lanes(h_re, h_im):
    b = h_re.shape[0]
    parts = jnp.stack([h_re[:, 0], h_re[:, 1], h_im[:, 0], h_im[:, 1]], axis=1)
    parts = parts.reshape(b, 4, S5_NOCT, S5_PART).transpose(2, 0, 1, 3)
    return parts.reshape(S5_NOCT, b, S5_SW)


def _s5_state_from_lanes(h):
    b = h.shape[1]
    parts = h.reshape(S5_NOCT, b, 4, S5_OCT, S5_STATE).transpose(1, 2, 0, 3, 4).reshape(b, 4, S5_GROUPS, S5_STATE)
    return parts[:, 0:2], parts[:, 2:4]


def _s5_scan(u, h0_re, h0_im, lam_re, lam_im, log_dt, b_re, b_im, c_re, c_im):
    ng, n = S5_GROUPS, S5_STATE
    rows = 2 * ng
    disc = pl.pallas_call(
        _s5_disc_kernel,
        out_shape=[jax.ShapeDtypeStruct((rows, n), F32)] * 4,
        name="s5_disc",
    )(lam_re.reshape(rows, n), lam_im.reshape(rows, n), log_dt.reshape(rows, 1))
    ar, ai, zr, zi = [_s5_octets(t.reshape(2, ng, n), n) for t in disc]
    bt = lambda t: _s5_blockdiag(t.transpose(0, 1, 3, 2))
    vec = pl.BlockSpec((1, 2, 1, S5_PART), lambda o, t: (o, 0, 0, 0))
    mat = pl.BlockSpec((1, 2, 128, S5_PART), lambda o, t: (o, 0, 0, 0))
    rowblk = lambda w: pl.BlockSpec((1, 128, w), lambda o, t: (o, t, 0))
    m, win, wout, a16 = pl.pallas_call(
        _s5_prep_kernel,
        grid=(S5_NOCT, S5_CHUNK),
        in_specs=[vec, vec, vec, vec, mat, mat, mat, mat],
        out_specs=[rowblk(S5_K), rowblk(S5_SW), rowblk(S5_SW), pl.BlockSpec((1, 1, S5_SW), lambda o, t: (o, 0, 0))],
        out_shape=[jax.ShapeDtypeStruct((S5_NOCT, S5_K, S5_K), BF16), jax.ShapeDtypeStruct((S5_NOCT, S5_K, S5_SW), BF16),
                   jax.ShapeDtypeStruct((S5_NOCT, S5_K, S5_SW), BF16), jax.ShapeDtypeStruct((S5_NOCT, 1, S5_SW), F32)],
        scratch_shapes=[pltpu.VMEM((2, 2, 24, S5_PART), F32), pltpu.VMEM((2, 2, S5_K, S5_PART), BF16),
                        pltpu.VMEM(((2 * S5_CHUNK - 1) * 128, 128), F32)],
        compiler_params=_cparams("arbitrary", "arbitrary"),
        name="s5_prep",
    )(ar, ai, zr, zi, bt(b_re), bt(b_im), _s5_blockdiag(c_re), _s5_blockdiag(c_im))
    nb = S5_NB
    y, hfin = pl.pallas_call(
        _s5_main_kernel,
        grid=(S5_NOCT, 2 * nb),
        in_specs=[pl.BlockSpec((N_TOK, 128), lambda o, s: (0, o)),
                  pl.BlockSpec((1, S5_K, S5_PART), lambda o, s: (o, 0, jnp.minimum(s, nb - 1))),
                  pl.BlockSpec((1, S5_K, S5_PART), lambda o, s: (o, 0, jnp.maximum(s - nb, 0))),
                  pl.BlockSpec((1, S5_PART, S5_SW), lambda o, s: (o, jnp.maximum(s - nb, 0), 0)),
                  pl.BlockSpec((1, 1, S5_SW), lambda o, s: (o, 0, 0)),
                  pl.BlockSpec((1, DEC_BATCH, S5_SW), lambda o, s: (o, 0, 0))],
        out_specs=[pl.BlockSpec((N_TOK, 128), lambda o, s: (0, o)),
                   pl.BlockSpec((1, BATCH, S5_SW), lambda o, s: (o, 0, 0))],
        out_shape=[jax.ShapeDtypeStruct((N_TOK, S5_WIDTH), F32), jax.ShapeDtypeStruct((S5_NOCT, BATCH, S5_SW), F32)],
        scratch_shapes=[pltpu.VMEM((S5_ROWS, S5_K), BF16), pltpu.VMEM((S5_SW // 128, S5_ROWS, 128), F32),
                        pltpu.VMEM((S5_SW // 128, S5_ROWS, 128), F32)],
        compiler_params=_cparams("arbitrary", "arbitrary"),
        name="s5_main",
    )(u, win, m, wout, a16, _s5_state_to_lanes(h0_re, h0_im))
    fin_re, fin_im = _s5_state_from_lanes(hfin)
    return y, fin_re, fin_im


def _ctx_or_lat(ctx_ref, lat_ref):
    return jnp.where(pl.program_id(0) < N_CTX_BLK, ctx_ref[...], lat_ref[...])


def _split_specs(width):
    return [pl.BlockSpec((ROW_BLK, width), lambda i: (jnp.minimum(i, N_CTX_BLK - 1), 0)),
            pl.BlockSpec((ROW_BLK, width), lambda i: (jnp.maximum(i - N_CTX_BLK, 0), 0))]


def _ab_out_kernel(x_ref, oac_ref, oal_ref, y_ref, u_ref, d_ref, wglu_ref, wout_ref, gt_ref, o_ref, wglu_s, wa_s, wb_s):
    @pl.when(pl.program_id(0) == 0)
    def _():
        wglu_s[...] = wglu_ref[...].astype(BF16)
        wa_s[...] = wout_ref[0:A_WIDTH, :].astype(BF16)
        wb_s[...] = wout_ref[A_WIDTH:, :].astype(BF16)

    g = jax.nn.gelu(y_ref[...] + d_ref[...] * u_ref[...])
    ob = g * jax.nn.sigmoid(jnp.dot(g.astype(BF16), wglu_s[...], preferred_element_type=F32))
    out = (jnp.dot(_ctx_or_lat(oac_ref, oal_ref).astype(BF16), wa_s[...], preferred_element_type=F32)
           + jnp.dot(ob.astype(BF16), wb_s[...], preferred_element_type=F32))
    o_ref[...] = x_ref[...] + gt_ref[0] * out


def _ab_out(x, oa, y, u, d_skip, w_glu, w_out, mod, layer):
    row = lambda n: pl.BlockSpec((ROW_BLK, n), lambda i: (i, 0))
    full = lambda a, b: pl.BlockSpec((a, b), lambda i: (0, 0))
    return pl.pallas_call(
        _ab_out_kernel,
        grid=(N_ROW_BLK,),
        in_specs=[row(D_MODEL)] + _split_specs(A_WIDTH) + [row(S5_WIDTH), row(S5_WIDTH), full(1, S5_WIDTH),
                  full(S5_WIDTH, S5_WIDTH), full(A_WIDTH + S5_WIDTH, D_MODEL), _mod_spec(layer, 2)],
        out_specs=row(D_MODEL),
        out_shape=jax.ShapeDtypeStruct((N_TOK, D_MODEL), F32),
        scratch_shapes=[pltpu.VMEM((S5_WIDTH, S5_WIDTH), BF16), pltpu.VMEM((A_WIDTH, D_MODEL), BF16),
                        pltpu.VMEM((S5_WIDTH, D_MODEL), BF16)],
        compiler_params=_cparams("arbitrary"),
        name="ab_out",
    )(x, oa[0], oa[1], y, u, d_skip.reshape(1, S5_WIDTH), w_glu, w_out, mod)


CD_HG0 = MLA_Q_RANK + MLA_KV_RANK + MLA_ROPE
KPE_LANES = 128


MLA_TILE = 128
MLA_QK = MLA_HEADS * MLA_TILE


def _mla_head_tiles(w, a):
    k, n = w.shape
    w = w.reshape(k, MLA_HEADS, n // MLA_HEADS)
    tiles = jnp.pad(w[:, :, :a], ((0, 0), (0, 0), (0, MLA_TILE - a)))
    return jnp.concatenate([tiles.reshape(k, -1), w[:, :, a:].reshape(k, -1)], axis=1)


def _mla_key_tiles(k_tiles, kpe):
    lane = lax.broadcasted_iota(jnp.int32, (1, MLA_TILE), 1)
    pe = jnp.where((lane >= MLA_NOPE) & (lane < MLA_NOPE + MLA_ROPE), kpe, 0.0)
    return k_tiles + jnp.concatenate([pe] * MLA_HEADS, axis=1)


def _cd_in_kernel(x_ref, g_ref, sc_ref, sh_ref, w_ref, gq_ref, wqu_ref, gkv_ref, wkvu_ref, cos_ref, sa_ref, sb_ref,
                  cq_ref, sq_ref, tq_ref, qf_ref, ckv_ref, kpe_ref, kf_ref, vm_ref, hq_ref, hff_ref, hfb_ref, hi_ref,
                  hg_ref, wcq_s, wckv_s, wkpe_s, whg_s, wqu_s, wkvu_s):
    @pl.when(pl.program_id(0) == 0)
    def _():
        wcq_s[...] = w_ref[:, 0:MLA_Q_RANK].astype(BF16)
        wckv_s[...] = w_ref[:, MLA_Q_RANK:MLA_Q_RANK + MLA_KV_RANK].astype(BF16)
        kp = w_ref[:, MLA_Q_RANK + MLA_KV_RANK:CD_HG0].astype(BF16)
        wkpe_s[...] = jnp.concatenate([kp] * (KPE_LANES // MLA_ROPE), axis=1)
        whg_s[...] = w_ref[:, CD_HG0:].astype(BF16)
        wqu_s[...] = wqu_ref[...].astype(BF16)
        wkvu_s[...] = wkvu_ref[...].astype(BF16)

    h = _norm_mod(x_ref[...], g_ref[...], sc_ref[0], sh_ref[0]).astype(BF16)
    cq = _rms(jnp.dot(h, wcq_s[...], preferred_element_type=F32)) * gq_ref[...]
    qq = jnp.dot(cq.astype(BF16), wqu_s[...], preferred_element_type=F32)
    qf_ref[...] = _apply_rope(qq, cq_ref[...], sq_ref[...], tq_ref[...], MLA_ROPE // 4).astype(qf_ref.dtype)
    ckv = _rms(jnp.dot(h, wckv_s[...], preferred_element_type=F32)) * gkv_ref[...]
    ckv_ref[...] = ckv
    kv = jnp.dot(ckv.astype(BF16), wkvu_s[...], preferred_element_type=F32)
    kpe = _apply_rope(jnp.dot(h, wkpe_s[...], preferred_element_type=F32), cos_ref[...], sa_ref[...], sb_ref[...],
                      MLA_ROPE // 4)
    kpe_ref[...] = kpe
    kf_ref[...] = _mla_key_tiles(kv[:, 0:MLA_QK], kpe).astype(kf_ref.dtype)
    vm_ref[...] = kv[:, MLA_QK:].astype(vm_ref.dtype)
    hh = jnp.dot(h, whg_s[...], preferred_element_type=F32)
    for n, ref in enumerate((hq_ref, hff_ref, hfb_ref, hi_ref, hg_ref)):
        ref[...] = hh[:, HG_KD * n:HG_KD * (n + 1)]


def _cd_in(x, mod, g, w, g_q, w_q_up, g_kv, w_kv_up, layer):
    cos, sa, sb = _rope_tables(MLA_ROPE)
    cos_q, sa_q, sb_q = _rope_tables(MLA_ROPE, only=(MLA_NOPE, MLA_NOPE + MLA_ROPE))
    d_in = w.shape[1]
    row = lambda n: pl.BlockSpec((ROW_BLK, n), lambda i: (i, 0))
    full = lambda a, b: pl.BlockSpec((a, b), lambda i: (0, 0))
    rope = pl.BlockSpec((ROW_BLK, 128), lambda i: (_rope_blk(i), 0))
    outs = [(MLA_QK, BF16), (MLA_KV_RANK, F32), (KPE_LANES, F32), (MLA_QK, BF16), (MLA_WIDTH, BF16)] + [(HG_KD, F32)] * 5
    nkv = MLA_QK + MLA_WIDTH
    return pl.pallas_call(
        _cd_in_kernel,
        grid=(N_ROW_BLK,),
        in_specs=[row(D_MODEL), full(1, D_MODEL), _mod_spec(layer, 1), _mod_spec(layer, 0), full(D_MODEL, d_in),
                  full(1, MLA_Q_RANK), full(MLA_Q_RANK, MLA_QK), full(1, MLA_KV_RANK), full(MLA_KV_RANK, nkv),
                  rope, rope, rope, rope, rope, rope],
        out_specs=[row(n) for n, _ in outs],
        out_shape=[jax.ShapeDtypeStruct((N_TOK, n), dt) for n, dt in outs],
        scratch_shapes=[pltpu.VMEM((D_MODEL, MLA_Q_RANK), BF16), pltpu.VMEM((D_MODEL, MLA_KV_RANK), BF16),
                        pltpu.VMEM((D_MODEL, KPE_LANES), BF16), pltpu.VMEM((D_MODEL, 5 * HG_KD), BF16),
                        pltpu.VMEM((MLA_Q_RANK, MLA_QK), BF16), pltpu.VMEM((MLA_KV_RANK, nkv), BF16)],
        compiler_params=_cparams("arbitrary"),
        name="cd_in",
    )(x, g.reshape(1, D_MODEL), mod, mod, w, g_q.reshape(1, -1), _mla_head_tiles(w_q_up, MLA_NOPE + MLA_ROPE),
      g_kv.reshape(1, -1), _mla_head_tiles(w_kv_up, MLA_NOPE), cos, sa, sb, cos_q, sa_q, sb_q)


def _mla_cache_kernel(c_ref, kpe_ref, w_ref, kf_ref, v_ref):
    kv = _dot(c_ref[...], w_ref[...])
    kf_ref[...] = _mla_key_tiles(kv[:, 0:MLA_QK], kpe_ref[...]).astype(kf_ref.dtype)
    v_ref[...] = kv[:, MLA_QK:].astype(v_ref.dtype)


def _mla_cache_kv(cckv, ckpe, w_kv_up):
    rows = cckv.shape[0]
    blk = lambda w: pl.BlockSpec((PAST_LEN, w), lambda i: (i, 0))
    return pl.pallas_call(
        _mla_cache_kernel,
        grid=(rows // PAST_LEN,),
        in_specs=[blk(MLA_KV_RANK), blk(KPE_LANES), pl.BlockSpec((MLA_KV_RANK, MLA_QK + MLA_WIDTH), lambda i: (0, 0))],
        out_specs=[blk(MLA_QK), blk(MLA_WIDTH)],
        out_shape=[jax.ShapeDtypeStruct((rows, MLA_QK), BF16), jax.ShapeDtypeStruct((rows, MLA_WIDTH), BF16)],
        compiler_params=_cparams("arbitrary"),
        name="mla_cache_kv",
    )(cckv, jnp.tile(ckpe, (1, KPE_LANES // MLA_ROPE)), _mla_head_tiles(w_kv_up, MLA_NOPE))


def _mla_heads(qf, keys, o_ref):
    scale = (MLA_NOPE + MLA_ROPE) ** -0.5
    outs = []
    for h in range(MLA_HEADS):
        t = slice(MLA_TILE * h, MLA_TILE * (h + 1))
        a = slice(MLA_V * h, MLA_V * (h + 1))
        s_list = [_dot_nt(qf[:, t], kf[:, t]) * scale for kf, _ in keys]
        outs.append(_softmax_pv(s_list, [v[:, a] for _, v in keys], None))
    o_ref[...] = jnp.concatenate(outs, axis=1).astype(o_ref.dtype)


def _mla_ctx_kernel(qf_ref, kf_ref, v_ref, o_ref):
    _mla_heads(qf_ref[...], [(kf_ref[...], v_ref[...])], o_ref)


def _mla_lat_kernel(qf_ref, kf_ref, v_ref, kcf_ref, vc_ref, o_ref):
    _mla_heads(qf_ref[...], [(kcf_ref[...], vc_ref[...]), (kf_ref[...], v_ref[...])], o_ref)


def _attention_mla(qf, kf, vm, kcf, vc):
    blk = lambda w: pl.BlockSpec((SEQ, w), lambda b: (b, 0))
    o = pl.pallas_call(
        _mla_ctx_kernel,
        grid=(BATCH,),
        in_specs=[blk(MLA_QK), blk(MLA_QK), blk(MLA_WIDTH)],
        out_specs=blk(MLA_WIDTH),
        out_shape=jax.ShapeDtypeStruct((N_CTX_TOK, MLA_WIDTH), BF16),
        compiler_params=_cparams("arbitrary"),
        name="mla_ctx",
    )(qf, kf, vm)
    nq = DEC_SEQ // ROW_BLK
    qblk = lambda w: pl.BlockSpec((ROW_BLK, w), lambda b, i: (N_CTX_BLK + b * nq + i, 0))
    seq = lambda w: pl.BlockSpec((DEC_SEQ, w), lambda b, i: (N_CTX_TOK // DEC_SEQ + b, 0))
    past = lambda w: pl.BlockSpec((PAST_LEN, w), lambda b, i: (b, 0))
    o_lat = pl.pallas_call(
        _mla_lat_kernel,
        grid=(DEC_BATCH, nq),
        in_specs=[qblk(MLA_QK), seq(MLA_QK), seq(MLA_WIDTH), past(MLA_QK), past(MLA_WIDTH)],
        out_specs=pl.BlockSpec((ROW_BLK, MLA_WIDTH), lambda b, i: (b * nq + i, 0)),
        out_shape=jax.ShapeDtypeStruct((N_LAT_TOK, MLA_WIDTH), BF16),
        compiler_params=_cparams("arbitrary", "arbitrary"),
        name="mla_lat",
    )(qf, kf, vm, kcf, vc)
    return o, o_lat


HG_TILE = 128
HG_NC = HG_TILE // HG_CHUNK
HG_HALF = 256


HG_SLABS = HG_KD // 128


def _hg_token_plane(raw_s, p):
    return jnp.concatenate([raw_s[j, pl.ds(p, HG_NC, stride=HG_CHUNK), :] for j in range(HG_SLABS)], axis=1)


def _hg_put(ref, x):
    for j in range(HG_SLABS):
        ref[j] = x[:, 128 * j:128 * (j + 1)]


def _hg_get(ref, c, l):
    rows = pl.ds(c, HG_CHUNK, stride=HG_NC)
    return jnp.concatenate([ref[2 * l, rows, :], ref[2 * l + 1, rows, :]], axis=1)


def _hg_plane(ref, p):
    return jnp.concatenate([ref[j, HG_NC * p:HG_NC * (p + 1), :] for j in range(HG_SLABS)], axis=1)


def _hg_put_plane(ref, p, x):
    for j in range(HG_SLABS):
        ref[j, HG_NC * p:HG_NC * (p + 1), :] = x[:, 128 * j:128 * (j + 1)]


HG_PAIRS = HG_CHUNK * (HG_CHUNK + 1) // 2
HG_PAIR_ROWS = HG_PAIRS * HG_NC


def _hg_pair_rows(d, i):
    n = d * HG_CHUNK - d * (d - 1) // 2 + (i - d)
    return slice(HG_NC * n, HG_NC * (n + 1))


def _hg_direction(hq_ref, hf_ref, hi_ref, o_ref, lb, st_ref, raw_s, q_s, f_s, kk_s, v_s, qt_s, kt_s, o_s, p_s, e_s,
                  sign, ones_bd, head_mask):
    _hg_put(raw_s, hq_ref[...])
    for p in range(HG_CHUNK):
        hq = _hg_token_plane(raw_s, p)
        _hg_put_plane(q_s, p, hq * jax.nn.sigmoid(hq))
    _hg_put(raw_s, hf_ref[...])
    for p in range(HG_CHUNK):
        f = lb + (1.0 - lb) * jax.nn.sigmoid(_hg_token_plane(raw_s, p))
        _hg_put_plane(f_s, p, f)
        _hg_put_plane(kk_s, p, 1.0 - f)
    _hg_put(raw_s, hi_ref[...])
    for p in range(HG_CHUNK):
        _hg_put_plane(v_s, p, _hg_token_plane(raw_s, p))
    pos = (lambda i: i) if sign > 0 else (lambda i: HG_CHUNK - 1 - i)
    plane = lambda ref, i: _hg_plane(ref, pos(i))

    for i in range(HG_CHUNK):
        q = plane(q_s, i)
        p_s[_hg_pair_rows(0, i), :] = q * plane(kk_s, i)
        dec = None
        for d in range(1, i + 1):
            fd = plane(f_s, i - d + 1)
            dec = fd if dec is None else dec * fd
            p_s[_hg_pair_rows(d, i), :] = q * plane(kk_s, i - d) * dec
    step = HG_PAIR_ROWS // 4
    for c in range(0, HG_PAIR_ROWS, step):
        pb = p_s[c:c + step, :].astype(BF16)
        e_s[c:c + step, 0:HG_HALF] = jnp.dot(pb[:, 0:HG_HALF], ones_bd, preferred_element_type=F32)
        e_s[c:c + step, HG_HALF:] = jnp.dot(pb[:, HG_HALF:], ones_bd, preferred_element_type=F32)
    for i in range(HG_CHUNK):
        o = None
        for d in range(i + 1):
            t = e_s[_hg_pair_rows(d, i), :] * plane(v_s, i - d)
            o = t if o is None else o + t
        _hg_put_plane(o_s, pos(i), o)

    incl = None
    for i in range(HG_CHUNK):
        fi = plane(f_s, i)
        incl = fi if incl is None else incl * fi
        _hg_put_plane(qt_s, pos(i), plane(q_s, i) * incl)
    whole = incl
    excl = None
    for i in range(HG_CHUNK - 1, -1, -1):
        kt = plane(kk_s, i)
        if excl is not None:
            kt = kt * excl
        _hg_put_plane(kt_s, pos(i), kt)
        fi = plane(f_s, i)
        excl = fi if excl is None else excl * fi

    tile4 = lambda x: jnp.where(head_mask, jnp.concatenate([x] * 4, axis=0), 0.0).astype(BF16)
    for cc in range(HG_NC):
        c = cc if sign > 0 else HG_NC - 1 - cc
        r = slice(HG_CHUNK * c, HG_CHUNK * (c + 1))
        for g in range(2):
            l = slice(HG_HALF * g, HG_HALF * (g + 1))
            st = st_ref[g]
            oc = _dot_nt(tile4(_hg_get(qt_s, c, g)), st)
            o_ref[r, l] = _hg_get(o_s, c, g) + jnp.concatenate(
                [oc[HG_CHUNK * h:HG_CHUNK * (h + 1), :] for h in range(4)], axis=1)
            v = hi_ref[r, l]
            vs = jnp.concatenate([v[:, HG_VAL * h:HG_VAL * (h + 1)] for h in range(4)], axis=0)
            ds = lax.dot_general(vs.astype(BF16), tile4(_hg_get(kt_s, c, g)), (((0,), (0,)), ((), ())),
                                 preferred_element_type=F32)
            st_ref[g] = st * whole[c:c + 1, l] + ds


def _hg_lower_bound(lb_ref, d, layer):
    raw = [lb_ref[d * DEPTH + m:d * DEPTH + m + 1, :] for m in range(DEPTH)]
    mx = functools.reduce(jnp.maximum, raw)
    e = [jnp.exp(r - mx) for r in raw]
    tot = functools.reduce(jnp.add, e)
    return functools.reduce(jnp.add, e[1:layer + 1], jnp.zeros_like(tot)) / tot


def _hgrn_kernel(nt, layer, hqf_ref, hff_ref, hif_ref, hqb_ref, hfb_ref, hib_ref, lb_ref, s0_ref,
                 of_ref, ob_ref, sfin_ref, st_s, raw_s, q_s, f_s, kk_s, v_s, qt_s, kt_s, o_s, p_s, e_s):
    i = pl.program_id(1)

    @pl.when(i == 0)
    def _():
        st_s[...] = s0_ref[0]

    ri = lax.broadcasted_iota(jnp.int32, (HG_HALF, HG_HALF), 0) // HG_KEY
    ci = lax.broadcasted_iota(jnp.int32, (HG_HALF, HG_HALF), 1) // HG_KEY
    ones_bd = jnp.where(ri == ci, 1.0, 0.0).astype(BF16)
    head_mask = (lax.broadcasted_iota(jnp.int32, (4 * HG_CHUNK, HG_HALF), 0) // HG_CHUNK
                 == lax.broadcasted_iota(jnp.int32, (4 * HG_CHUNK, HG_HALF), 1) // HG_KEY)
    _hg_direction(hqf_ref, hff_ref, hif_ref, of_ref, _hg_lower_bound(lb_ref, 0, layer), st_s.at[0],
                  raw_s, q_s, f_s, kk_s, v_s, qt_s, kt_s, o_s, p_s, e_s, 1, ones_bd, head_mask)
    _hg_direction(hqb_ref, hfb_ref, hib_ref, ob_ref, _hg_lower_bound(lb_ref, 1, layer), st_s.at[1],
                  raw_s, q_s, f_s, kk_s, v_s, qt_s, kt_s, o_s, p_s, e_s, -1, ones_bd, head_mask)

    @pl.when(i == nt - 1)
    def _():
        sfin_ref[0] = st_s[...]


def _hg_state_to_blocks(s):
    b = s.shape[0]
    st = s.reshape(b, 2, 2, 4, HG_KEY, HG_VAL).transpose(0, 1, 2, 5, 3, 4)
    return st.reshape(b, 2, 2, HG_VAL, HG_HALF)


def _hg_state_from_blocks(st):
    b = st.shape[0]
    st = st.reshape(b, 2, 2, HG_VAL, 4, HG_KEY).transpose(0, 1, 2, 4, 5, 3)
    return st.reshape(b, 2, HG_HEADS, HG_KEY, HG_VAL)


def _hgrn_scan(hq, hff, hfb, hi, lb, s0, layer, row0, nseq, seqlen):
    nt = seqlen // HG_TILE
    base = row0 // HG_TILE
    fwd = pl.BlockSpec((HG_TILE, HG_KD), lambda b, i: (base + b * nt + i, 0))
    bwd = pl.BlockSpec((HG_TILE, HG_KD), lambda b, i: (base + b * nt + nt - 1 - i, 0))
    ofw = pl.BlockSpec((HG_TILE, HG_KD), lambda b, i: (b * nt + i, 0))
    obw = pl.BlockSpec((HG_TILE, HG_KD), lambda b, i: (b * nt + nt - 1 - i, 0))
    st = pl.BlockSpec((1, 2, 2, HG_VAL, HG_HALF), lambda b, i: (b, 0, 0, 0, 0))
    tile = lambda: pltpu.VMEM((HG_SLABS, HG_TILE, 128), F32)
    pairs = lambda: pltpu.VMEM((HG_PAIR_ROWS, HG_KD), F32)
    return pl.pallas_call(
        functools.partial(_hgrn_kernel, nt, layer),
        grid=(nseq, nt),
        in_specs=[fwd, fwd, fwd, bwd, bwd, bwd, pl.BlockSpec((2 * DEPTH, HG_KD), lambda b, i: (0, 0)), st],
        out_specs=[ofw, obw, st],
        out_shape=[jax.ShapeDtypeStruct((nseq * seqlen, HG_WIDTH), F32)] * 2
        + [jax.ShapeDtypeStruct((nseq, 2, 2, HG_VAL, HG_HALF), F32)],
        scratch_shapes=[pltpu.VMEM((2, 2, HG_VAL, HG_HALF), F32)] + [tile() for _ in range(8)] + [pairs(), pairs()],
        compiler_params=_cparams("arbitrary", "arbitrary"),
        name="hgrn_scan",
    )(hq, hff, hi, hq, hfb, hi, lb.reshape(2 * DEPTH, HG_KD), s0)


def _cd_out_kernel(x_ref, occ_ref, ocl_ref, ofc_ref, ofl_ref, obc_ref, obl_ref, hg_ref, go_ref, wout_ref, gt_ref,
                   o_ref, wa_s, wb_s):
    @pl.when(pl.program_id(0) == 0)
    def _():
        wa_s[...] = wout_ref[0:MLA_WIDTH, :].astype(BF16)
        wb_s[...] = wout_ref[MLA_WIDTH:, :].astype(BF16)

    ri = lax.broadcasted_iota(jnp.int32, (HG_HALF, HG_HALF), 0) // HG_VAL
    ci = lax.broadcasted_iota(jnp.int32, (HG_HALF, HG_HALF), 1) // HG_VAL
    ones_bd = jnp.where(ri == ci, 1.0, 0.0).astype(BF16)
    o = _ctx_or_lat(ofc_ref, ofl_ref) + _ctx_or_lat(obc_ref, obl_ref)
    sq = o * o
    hi = sq.astype(BF16)
    lo = (sq - hi.astype(F32)).astype(BF16)
    ms = jnp.concatenate(
        [jnp.dot(hi[:, l], ones_bd, preferred_element_type=F32) + jnp.dot(lo[:, l], ones_bd, preferred_element_type=F32)
         for l in (slice(0, HG_HALF), slice(HG_HALF, 2 * HG_HALF))], axis=1) * (1.0 / HG_VAL)
    hg = hg_ref[...]
    od = o * lax.rsqrt(ms + EPS) * go_ref[...] * (hg * jax.nn.sigmoid(hg))
    out = (jnp.dot(_ctx_or_lat(occ_ref, ocl_ref).astype(BF16), wa_s[...], preferred_element_type=F32)
           + jnp.dot(od.astype(BF16), wb_s[...], preferred_element_type=F32))
    o_ref[...] = x_ref[...] + gt_ref[0] * out


def _cd_out(x, oc, of, ob, hg, g_o, w_out, mod, layer):
    row = lambda n: pl.BlockSpec((ROW_BLK, n), lambda i: (i, 0))
    full = lambda a, b: pl.BlockSpec((a, b), lambda i: (0, 0))
    return pl.pallas_call(
        _cd_out_kernel,
        grid=(N_ROW_BLK,),
        in_specs=[row(D_MODEL)] + _split_specs(MLA_WIDTH) + _split_specs(HG_WIDTH) + _split_specs(HG_WIDTH)
        + [row(HG_WIDTH), full(1, HG_WIDTH), full(MLA_WIDTH + HG_WIDTH, D_MODEL), _mod_spec(layer, 2)],
        out_specs=row(D_MODEL),
        out_shape=jax.ShapeDtypeStruct((N_TOK, D_MODEL), F32),
        scratch_shapes=[pltpu.VMEM((MLA_WIDTH, D_MODEL), BF16), pltpu.VMEM((HG_WIDTH, D_MODEL), BF16)],
        compiler_params=_cparams("arbitrary"),
        name="cd_out",
    )(x, oc[0], oc[1], of[0], of[1], ob[0], ob[1], hg, jnp.tile(g_o, HG_HEADS).reshape(1, HG_WIDTH), w_out, mod)


N_PAIRS = 6
N_CLASSES = N_GROUPS * N_PAIRS
CLS_ROWS = 32
MOE_BM = 256
MOE_NBLK = N_TOK // MOE_BM + N_CLASSES
MOE_ROWS = MOE_NBLK * MOE_BM
PAIR_SLOTS = ((0, 1), (3, 1), (2, 1), (2, 0), (3, 0), (3, 2))


def _moe_route_kernel(x_ref, g_ref, sc_ref, sh_ref, rw_ref, rb_ref, h_ref, ti_ref, tw_ref, cnt_ref, base_s):
    @pl.when(pl.program_id(0) == 0)
    def _():
        base_s[...] = jnp.zeros_like(base_s)

    h = _norm_mod(x_ref[...], g_ref[...], sc_ref[0], sh_ref[0])
    logits = lax.dot_general(rw_ref[...], h, (((1,), (1,)), ((), ())), precision=HI,
                             preferred_element_type=F32)
    aff = jax.nn.sigmoid(logits)
    sel = aff + rb_ref[...]
    s = [sel[e:e + 1, :] for e in range(N_EXPERTS)]
    a = [aff[e:e + 1, :] for e in range(N_EXPERTS)]
    gs = []
    for g in range(N_GROUPS):
        m = s[4 * g:4 * g + 4]
        pairs = [m[i] + m[j] for i in range(4) for j in range(i + 1, 4)]
        gs.append(functools.reduce(jnp.maximum, pairs))
    gmax = functools.reduce(jnp.maximum, gs)
    taken = jnp.zeros_like(gmax) > 1.0
    gsel = []
    for g in range(N_GROUPS):
        hit = (gs[g] == gmax) & jnp.logical_not(taken)
        gsel.append(hit)
        taken = taken | hit
    e_lo = jnp.zeros(gmax.shape, jnp.int32)
    e_hi = jnp.zeros(gmax.shape, jnp.int32)
    a_lo = jnp.zeros_like(gmax)
    a_hi = jnp.zeros_like(gmax)
    nsel = jnp.zeros(gmax.shape, jnp.int32)
    for g in range(N_GROUPS):
        for i in range(4):
            e = 4 * g + i
            beat = jnp.zeros(gmax.shape, jnp.int32)
            for j in range(4):
                if j != i:
                    o = 4 * g + j
                    beat = beat + jnp.where((s[o] > s[e]) | ((s[o] == s[e]) & (j < i)), 1, 0)
            pick = gsel[g] & (beat < 2)
            is_first = pick & (nsel == 0)
            is_second = pick & (nsel == 1)
            e_lo = jnp.where(is_first, e, e_lo)
            a_lo = jnp.where(is_first, a[e], a_lo)
            e_hi = jnp.where(is_second, e, e_hi)
            a_hi = jnp.where(is_second, a[e], a_hi)
            nsel = nsel + jnp.where(pick, 1, 0)
    grp = e_lo // EXPERTS_PER_GROUP
    lo = e_lo - grp * EXPERTS_PER_GROUP
    hi = e_hi - grp * EXPERTS_PER_GROUP
    pair = ((lo * (7 - lo)) >> 1) + (hi - lo - 1)
    pair = jnp.where(pair == 1, 3, jnp.where(pair == 2, 4, jnp.where(pair == 3, 2, jnp.where(pair == 4, 1, pair))))
    cls = grp * N_PAIRS + pair
    wsum = a_lo + a_hi
    w_lo, w_hi = a_lo / wsum, a_hi / wsum
    onehot = (lax.broadcasted_iota(jnp.int32, (CLS_ROWS, ROW_BLK), 0) == cls).astype(F32)
    tt = lax.broadcasted_iota(jnp.int32, (ROW_BLK, ROW_BLK), 0) < lax.broadcasted_iota(jnp.int32, (ROW_BLK, ROW_BLK), 1)
    before = _dot(onehot, jnp.where(tt, 1.0, 0.0))
    base = base_s[...]
    rank = jnp.sum(onehot * (before + base[:, 0:1]), axis=0, keepdims=True).astype(jnp.int32)
    base = base + jnp.sum(onehot, axis=1, keepdims=True)
    base_s[...] = base
    cnt_ref[...] = base.astype(jnp.int32)
    ti_ref[0] = jnp.concatenate([cls, rank, e_lo, e_hi, jnp.zeros((4, ROW_BLK), jnp.int32)], axis=0)
    ident = lax.broadcasted_iota(jnp.int32, (ROW_BLK, ROW_BLK), 0) == lax.broadcasted_iota(jnp.int32, (ROW_BLK, ROW_BLK), 1)
    col = lambda r: jnp.sum(jnp.where(ident, r, 0.0), axis=1, keepdims=True)
    tw_ref[...] = jnp.where(lax.broadcasted_iota(jnp.int32, (ROW_BLK, 128), 1) < 64, col(w_lo), col(w_hi))
    h_ref[...] = h


def _moe_route(x, mod, g, router_w, router_b, layer):
    row = lambda n: pl.BlockSpec((ROW_BLK, n), lambda i: (i, 0))
    full = lambda a, b: pl.BlockSpec((a, b), lambda i: (0, 0))
    return pl.pallas_call(
        _moe_route_kernel,
        grid=(N_ROW_BLK,),
        in_specs=[row(D_MODEL), full(1, D_MODEL), _mod_spec(layer, 4), _mod_spec(layer, 3),
                  full(N_EXPERTS, D_MODEL), full(N_EXPERTS, 1)],
        out_specs=[row(D_MODEL), pl.BlockSpec((1, 8, ROW_BLK), lambda i: (i, 0, 0)), row(128), full(CLS_ROWS, 128)],
        out_shape=[jax.ShapeDtypeStruct((N_TOK, D_MODEL), F32),
                   jax.ShapeDtypeStruct((N_ROW_BLK, 8, ROW_BLK), jnp.int32),
                   jax.ShapeDtypeStruct((N_TOK, 128), F32),
                   jax.ShapeDtypeStruct((CLS_ROWS, 128), jnp.int32)],
        scratch_shapes=[pltpu.VMEM((CLS_ROWS, 128), F32)],
        compiler_params=_cparams("arbitrary"),
        name="moe_route",
    )(x, g.reshape(1, D_MODEL), mod, mod, router_w.T, router_b.reshape(N_EXPERTS, 1))


def _moe_sort_kernel(pos_ref, nblk_ref, h_ref, wt_ref, hs_ref, ws_ref, perm_s, stage_s):
    j = pl.program_id(0)

    @pl.when(j == 0)
    def _():
        def init(r, c):
            perm_s[r] = N_TOK - 1
            return c

        lax.fori_loop(0, MOE_ROWS, init, 0, unroll=16)

        def build(t, c):
            perm_s[pos_ref[t]] = t
            return c

        lax.fori_loop(0, N_TOK, build, 0, unroll=16)

    @pl.when(j < nblk_ref[0])
    def _():
        base = j * MOE_BM
        for r in range(MOE_BM):
            src = perm_s[base + r]
            stage_s[r:r + 1, :] = h_ref[pl.ds(src, 1), :]
            ws_ref[r:r + 1, :] = wt_ref[pl.ds(src, 1), :]
        hs_ref[...] = stage_s[...].astype(BF16)

    @pl.when(j >= nblk_ref[0])
    def _():
        hs_ref[...] = jnp.zeros_like(hs_ref)
        ws_ref[...] = jnp.zeros_like(ws_ref)


def _moe_sort(h, wtok, pos, nblk):
    res = lambda w: pl.BlockSpec((N_TOK, w), lambda j, *_: (0, 0), pipeline_mode=pl.Buffered(1))
    grid_spec = pltpu.PrefetchScalarGridSpec(
        num_scalar_prefetch=2,
        grid=(MOE_NBLK,),
        in_specs=[res(D_MODEL), res(128)],
        out_specs=[pl.BlockSpec((MOE_BM, D_MODEL), lambda j, *_: (j, 0)),
                   pl.BlockSpec((MOE_BM, 128), lambda j, *_: (j, 0))],
        scratch_shapes=[pltpu.SMEM((MOE_ROWS,), jnp.int32), pltpu.VMEM((MOE_BM, D_MODEL), F32)],
    )
    return pl.pallas_call(
        _moe_sort_kernel,
        grid_spec=grid_spec,
        out_shape=[jax.ShapeDtypeStruct((MOE_ROWS, D_MODEL), BF16), jax.ShapeDtypeStruct((MOE_ROWS, 128), F32)],
        compiler_params=_cparams("arbitrary"),
        name="moe_sort",
    )(pos, nblk, h, wtok)


def _moe_experts_kernel(elo_ref, ehi_ref, nblk_ref, swap_ref, hs_ref, ws_ref, wgl_ref, wul_ref, wdl_ref,
                        wgh_ref, wuh_ref, wdh_ref, y_ref):
    del elo_ref, ehi_ref
    j = pl.program_id(0)

    @pl.when(j < nblk_ref[0])
    def _():
        h = hs_ref[...].astype(BF16)
        swap = swap_ref[j] == 1
        gate_a = jnp.where(swap, ws_ref[:, 64:65], ws_ref[:, 0:1])
        gate_b = jnp.where(swap, ws_ref[:, 0:1], ws_ref[:, 64:65])
        acc = None
        for wg, wu, wd, gate in ((wgl_ref, wul_ref, wdl_ref, gate_a), (wgh_ref, wuh_ref, wdh_ref, gate_b)):
            g = _dot(h, wg[0, 0])
            u = _dot(h, wu[0, 0])
            hid = g * jax.nn.sigmoid(g) * u * gate
            y = _dot(hid, wd[0, 0])
            acc = y if acc is None else acc + y
        y_ref[...] = acc

    @pl.when(j >= nblk_ref[0])
    def _():
        y_ref[...] = jnp.zeros_like(y_ref)


def _moe_experts(hs, ws, blk_elo, blk_ehi, nblk, swap, w_gate, w_up, w_down, layer):
    last = lambda j, nb: jnp.minimum(j, nb[0] - 1)
    wspec = lambda a, b, which: pl.BlockSpec(
        (1, 1, a, b), lambda j, elo, ehi, nb, sw: (layer, (elo, ehi)[which][last(j, nb)], 0, 0))
    grid_spec = pltpu.PrefetchScalarGridSpec(
        num_scalar_prefetch=4,
        grid=(MOE_NBLK,),
        in_specs=[pl.BlockSpec((MOE_BM, D_MODEL), lambda j, elo, ehi, nb, sw: (last(j, nb), 0)),
                  pl.BlockSpec((MOE_BM, 128), lambda j, elo, ehi, nb, sw: (last(j, nb), 0)),
                  wspec(D_MODEL, D_FF, 0), wspec(D_MODEL, D_FF, 0), wspec(D_FF, D_MODEL, 0),
                  wspec(D_MODEL, D_FF, 1), wspec(D_MODEL, D_FF, 1), wspec(D_FF, D_MODEL, 1)],
        out_specs=pl.BlockSpec((MOE_BM, D_MODEL), lambda j, *_: (j, 0)),
    )
    return pl.pallas_call(
        _moe_experts_kernel,
        grid_spec=grid_spec,
        out_shape=jax.ShapeDtypeStruct((MOE_ROWS, D_MODEL), F32),
        compiler_params=_cparams("arbitrary"),
        name="moe_experts",
    )(blk_elo, blk_ehi, nblk, swap, hs, ws, w_gate, w_up, w_down, w_gate, w_up, w_down)


SC_ROWS = 64


def _sc_gather_rows(table, idx):
    info = plsc.get_sparse_core_info()
    nc, nw = info.num_cores, info.num_cores * info.num_subcores
    b, d = idx.shape[0], table.shape[1]
    per_w = b // nw
    assert per_w * nw == b and per_w % SC_ROWS == 0
    mesh = plsc.VectorSubcoreMesh(core_axis_name="c", subcore_axis_name="s")

    @functools.partial(
        pl.kernel, mesh=mesh, out_type=jax.ShapeDtypeStruct((b, d), table.dtype),
        scratch_types=[pltpu.VMEM((SC_ROWS,), jnp.int32), pltpu.VMEM((SC_ROWS, d), table.dtype),
                       pltpu.SemaphoreType.DMA])
    def gather(table_hbm, idx_hbm, out_hbm, idx_v, rows_v, sem):
        wid = lax.axis_index("s") * nc + lax.axis_index("c")
        for c in range(per_w // SC_ROWS):
            base = wid * per_w + c * SC_ROWS
            pltpu.sync_copy(idx_hbm.at[pl.ds(base, SC_ROWS)], idx_v)
            pltpu.async_copy(table_hbm.at[idx_v], rows_v, sem).wait()
            pltpu.sync_copy(rows_v, out_hbm.at[pl.ds(base, SC_ROWS)])

    return gather(table, idx)


def _moe_residual_kernel(x_ref, y_ref, gt_ref, o_ref):
    o_ref[...] = x_ref[...] + gt_ref[0] * y_ref[...]


def _moe_combine(x, y_sorted, pos, mod, layer):
    row = pl.BlockSpec((ROW_BLK, D_MODEL), lambda i: (i, 0))
    return pl.pallas_call(
        _moe_residual_kernel,
        grid=(N_ROW_BLK,),
        in_specs=[row, row, _mod_spec(layer, 5)],
        out_specs=row,
        out_shape=jax.ShapeDtypeStruct((N_TOK, D_MODEL), F32),
        compiler_params=_cparams("arbitrary"),
        name="moe_residual",
    )(x, _sc_gather_rows(y_sorted, pos), mod)


def _moe(x, mod, g, router_w, router_b, w_gate, w_up, w_down, layer):
    h, info, wtok, counts = _moe_route(x, mod, g, router_w, router_b, layer)
    cls = info[:, 0, :].reshape(N_TOK)
    rank = info[:, 1, :].reshape(N_TOK)
    cnt = counts[:N_CLASSES, 0]
    nb = (cnt + MOE_BM - 1) // MOE_BM
    ends = jnp.cumsum(nb)
    starts = ends - nb
    pos = ((starts * MOE_BM)[cls] + rank).astype(jnp.int32)
    blk = jnp.arange(MOE_NBLK, dtype=jnp.int32)
    blk_cls = jnp.minimum(jnp.sum((blk[:, None] >= ends[None, :]).astype(jnp.int32), axis=1), N_CLASSES - 1)
    slot_a = jnp.asarray([s[0] for s in PAIR_SLOTS], jnp.int32)
    slot_b = jnp.asarray([s[1] for s in PAIR_SLOTS], jnp.int32)
    grp = blk_cls // N_PAIRS
    blk_ea = (grp * EXPERTS_PER_GROUP + slot_a[blk_cls % N_PAIRS]).astype(jnp.int32)
    blk_eb = (grp * EXPERTS_PER_GROUP + slot_b[blk_cls % N_PAIRS]).astype(jnp.int32)
    swap = (blk_ea > blk_eb).astype(jnp.int32)
    nblk = ends[-1:].astype(jnp.int32)
    hs, ws = _moe_sort(h, wtok, pos, nblk)
    y_sorted = _moe_experts(hs, ws, blk_ea, blk_eb, nblk, swap, w_gate, w_up, w_down, layer)
    return _moe_combine(x, y_sorted, pos, mod, layer)


def _final_norm_kernel(x_ref, g_ref, o_ref):
    o_ref[...] = _rms(x_ref[...]) * g_ref[...]


def _final_norm(x, g, row0, rows):
    base = row0 // ROW_BLK
    return pl.pallas_call(
        _final_norm_kernel,
        grid=(rows // ROW_BLK,),
        in_specs=[pl.BlockSpec((ROW_BLK, D_MODEL), lambda i: (base + i, 0)), pl.BlockSpec((1, D_MODEL), lambda i: (0, 0))],
        out_specs=pl.BlockSpec((ROW_BLK, D_MODEL), lambda i: (i, 0)),
        out_shape=jax.ShapeDtypeStruct((rows, D_MODEL), F32),
        compiler_params=_cparams("arbitrary"),
        name="final_norm",
    )(x, g.reshape(1, D_MODEL))


def kernel(x_prompt, x_sample, cache_attn_k, cache_attn_v, state_ssm_re, state_ssm_im, cache_mla_ckv, cache_mla_kpe,
           state_hgrn, c, c_ctx, w_mod, b_mod, g_mix, g_ffn, g_final, router_w, router_b, moe_w_gate, moe_w_up,
           moe_w_down, ab_w_in, ab_sink, s5_lam_re, s5_lam_im, s5_log_dt, s5_b_re, s5_b_im, s5_c_re, s5_c_im, s5_d,
           s5_w_glu, ab_w_out, cd_w_in, mla_g_q, mla_w_q_up, mla_g_kv, mla_w_kv_up, hg_lower_bounds, hg_g_o, cd_w_out):
    x = jnp.concatenate([x_prompt.reshape(N_CTX_TOK, D_MODEL), x_sample.reshape(N_LAT_TOK, D_MODEL)], axis=0)
    cond = jnp.zeros((MOD_ROWS, D_MODEL), F32).at[0].set(c_ctx).at[1:1 + DEC_BATCH].set(c)
    mod = _modulation(cond, w_mod, b_mod)
    keep = ([], [], [], [], [], [], [])
    for l in range(DEPTH):
        j = l // 2
        if l % 2 == 0:
            q, k, v, u = _ab_in(x, mod, g_mix[l], ab_w_in[j], l)
            o_a = _attention_a(q, k, v, cache_attn_k[:, j].reshape(DEC_BATCH, PAST_LEN, A_KV_WIDTH),
                               cache_attn_v[:, j].reshape(DEC_BATCH, PAST_LEN, A_KV_WIDTH), ab_sink[j])
            y, fin_re, fin_im = _s5_scan(u, state_ssm_re[:, j], state_ssm_im[:, j], s5_lam_re[j], s5_lam_im[j],
                                         s5_log_dt[j], s5_b_re[j], s5_b_im[j], s5_c_re[j], s5_c_im[j])
            x = _ab_out(x, o_a, y, u, s5_d[j], s5_w_glu[j], ab_w_out[j], mod, l)
            keep[0].append(k[:N_CTX_TOK].reshape(BATCH, SEQ, A_KV_HEADS, HEAD_DIM))
            keep[1].append(v[:N_CTX_TOK].reshape(BATCH, SEQ, A_KV_HEADS, HEAD_DIM))
            keep[2].append(fin_re)
            keep[3].append(fin_im)
        else:
            qf, ckv, kpe, kf, vm, hq, hff, hfb, hi, hg = _cd_in(
                x, mod, g_mix[l], cd_w_in[j], mla_g_q[j], mla_w_q_up[j], mla_g_kv[j], mla_w_kv_up[j], l)
            kcf, vc = _mla_cache_kv(cache_mla_ckv[:, j].reshape(DEC_BATCH * PAST_LEN, MLA_KV_RANK),
                                    cache_mla_kpe[:, j].reshape(DEC_BATCH * PAST_LEN, MLA_ROPE), mla_w_kv_up[j])
            o_c = _attention_mla(qf, kf, vm, kcf, vc)
            s0_ctx = jnp.zeros((BATCH, 2, 2, HG_VAL, HG_HALF), F32)
            of_c, ob_c, s_fin = _hgrn_scan(hq, hff, hfb, hi, hg_lower_bounds, s0_ctx, l, 0, BATCH, SEQ)
            of_l, ob_l, _ = _hgrn_scan(hq, hff, hfb, hi, hg_lower_bounds, _hg_state_to_blocks(state_hgrn[:, j]), l,
                                       N_CTX_TOK, DEC_BATCH, DEC_SEQ)
            x = _cd_out(x, o_c, (of_c, of_l), (ob_c, ob_l), hg, hg_g_o[j], cd_w_out[j], mod, l)
            keep[4].append(ckv[:N_CTX_TOK].reshape(BATCH, SEQ, MLA_KV_RANK))
            keep[5].append(kpe[:N_CTX_TOK, :MLA_ROPE].reshape(BATCH, SEQ, MLA_ROPE))
            keep[6].append(_hg_state_from_blocks(s_fin))
        x = _moe(x, mod, g_ffn[l], router_w, router_b, moe_w_gate, moe_w_up, moe_w_down, l)
    y_ctx = _final_norm(x, g_final, 0, N_CTX_TOK)
    y_lat = _final_norm(x, g_final, N_CTX_TOK, N_LAT_TOK)
    return (y_ctx.reshape(BATCH, SEQ, D_MODEL), y_lat.reshape(DEC_BATCH, DEC_SEQ, D_MODEL),
            jnp.stack(keep[0], 1), jnp.stack(keep[1], 1), jnp.stack(keep[2], 1), jnp.stack(keep[3], 1),
            jnp.stack(keep[4], 1), jnp.stack(keep[5], 1), jnp.stack(keep[6], 1))
```

```python
import functools
import math

import numpy as np
import jax
import jax.numpy as jnp
from jax import lax
from jax.experimental import pallas as pl
from jax.experimental.pallas import tpu as pltpu
from jax.experimental.pallas import tpu_sc as plsc

F32 = jnp.float32
BF16 = jnp.bfloat16

D_MODEL = 1024
BATCH = 16
SEQ = 256
DEPTH = 2
DEC_BATCH = 4
DEC_SEQ = 1024
PAST_LEN = 512
GRID_W = 64
N_MOD = 6
EPS = 1e-6
NEG_INF = -1e30
ROPE_BASE = 10000.0
HEAD_DIM = 64
A_HEADS = 8
A_KV_HEADS = 2
A_WINDOW = 128
A_WIDTH = A_HEADS * HEAD_DIM
A_KV_WIDTH = A_KV_HEADS * HEAD_DIM
S5_WIDTH = D_MODEL // 2
S5_GROUP = 16
S5_GROUPS = S5_WIDTH // S5_GROUP
S5_STATE = 64
MLA_HEADS = 8
MLA_Q_RANK = D_MODEL // 4
MLA_KV_RANK = D_MODEL // 8
MLA_NOPE = 64
MLA_ROPE = 32
MLA_V = 64
MLA_WIDTH = MLA_HEADS * MLA_V
HG_HEADS = 8
HG_KEY = 64
HG_VAL = 64
HG_KD = HG_HEADS * HG_KEY
HG_WIDTH = HG_HEADS * HG_VAL
HG_CHUNK = 16
N_EXPERTS = 16
N_GROUPS = 4
EXPERTS_PER_GROUP = N_EXPERTS // N_GROUPS
D_FF = D_MODEL // 2

N_CTX_TOK = BATCH * SEQ
N_LAT_TOK = DEC_BATCH * DEC_SEQ
N_TOK = N_CTX_TOK + N_LAT_TOK
ROW_BLK = 256
N_ROW_BLK = N_TOK // ROW_BLK
N_CTX_BLK = N_CTX_TOK // ROW_BLK
LAT_BLK_PER_SEQ = DEC_SEQ // ROW_BLK
MOD_ROWS = 8
VMEM_LIMIT = 56 * 1024 * 1024


def _cparams(*sem):
    return pltpu.CompilerParams(dimension_semantics=sem, vmem_limit_bytes=VMEM_LIMIT)


def _mod_group(i):
    return jnp.where(i < N_CTX_BLK, 0, 1 + (i - N_CTX_BLK) // LAT_BLK_PER_SEQ)


def _mod_spec(layer, which):
    return pl.BlockSpec((1, 1, D_MODEL), lambda i: ((layer * MOD_ROWS + _mod_group(i)) * N_MOD + which, 0, 0))


def _rope_blk(i):
    return jnp.where(i < N_CTX_BLK, 0, 1 + (i - N_CTX_BLK) % LAT_BLK_PER_SEQ)


def _rope_tables(rot_dim, only=None):
    n_freq = rot_dim // 4
    t = np.arange(DEC_SEQ)
    rows = (t // GRID_W).astype(np.float32)
    cols = (t % GRID_W).astype(np.float32)
    inv = (np.float32(ROPE_BASE) ** (-np.arange(n_freq, dtype=np.float32) / np.float32(n_freq))).astype(np.float32)
    ang_r = rows[:, None] * inv[None, :]
    ang_c = cols[:, None] * inv[None, :]
    ang = np.concatenate([ang_r, ang_r, ang_c, ang_c], axis=-1).astype(np.float32)
    reps = 128 // rot_dim
    cos = np.tile(np.cos(ang), (1, reps)).astype(np.float32)
    sin = np.tile(np.sin(ang), (1, reps)).astype(np.float32)
    lane = np.arange(128)
    first = (lane % (2 * n_freq)) < n_freq
    sin_a = np.where(first[None, :], -sin, 0.0).astype(np.float32)
    sin_b = np.where(first[None, :], 0.0, sin).astype(np.float32)
    if only is not None:
        keep = ((lane >= only[0]) & (lane < only[1]))[None, :]
        cos, sin_a, sin_b = np.where(keep, cos, 1.0), np.where(keep, sin_a, 0.0), np.where(keep, sin_b, 0.0)
        cos, sin_a, sin_b = cos.astype(np.float32), sin_a.astype(np.float32), sin_b.astype(np.float32)
    ident = np.zeros((ROW_BLK, 128), np.float32)
    cos = np.concatenate([ident + 1.0, cos], axis=0)
    sin_a = np.concatenate([ident, sin_a], axis=0)
    sin_b = np.concatenate([ident, sin_b], axis=0)
    return jnp.asarray(cos), jnp.asarray(sin_a), jnp.asarray(sin_b)


def _apply_rope(x, cos, sin_a, sin_b, quarter):
    outs = []
    for j in range(x.shape[1] // 128):
        xt = x[:, 128 * j:128 * (j + 1)]
        up = pltpu.roll(xt, 128 - quarter, axis=1)
        dn = pltpu.roll(xt, quarter, axis=1)
        outs.append(xt * cos + up * sin_a + dn * sin_b)
    return outs[0] if len(outs) == 1 else jnp.concatenate(outs, axis=1)


def _rms(x):
    return x * lax.rsqrt(jnp.mean(x * x, axis=-1, keepdims=True) + EPS)


def _norm_mod(x, g, sc, sh):
    return _rms(x) * g * (1.0 + sc) + sh


def _dot(a, b):
    return jnp.dot(a.astype(BF16), b.astype(BF16), preferred_element_type=F32)


def _dot_nt(a, b):
    return lax.dot_general(a.astype(BF16), b.astype(BF16), (((1,), (1,)), ((), ())), preferred_element_type=F32)


def _mod_kernel(cond_ref, w_ref, b_ref, o_ref):
    c = cond_ref[...]
    s = c * jax.nn.sigmoid(c)
    o_ref[0] = _dot(s, w_ref[0]) + b_ref[0]


def _modulation(cond, w_mod, b_mod):
    nb = 1024
    out = pl.pallas_call(
        _mod_kernel,
        grid=(DEPTH, N_MOD * D_MODEL // nb),
        in_specs=[pl.BlockSpec((MOD_ROWS, D_MODEL), lambda l, n: (0, 0)),
                  pl.BlockSpec((1, D_MODEL, nb), lambda l, n: (l, 0, n)),
                  pl.BlockSpec((1, 1, nb), lambda l, n: (l, 0, n))],
        out_specs=pl.BlockSpec((1, MOD_ROWS, nb), lambda l, n: (l, 0, n)),
        out_shape=jax.ShapeDtypeStruct((DEPTH, MOD_ROWS, N_MOD * D_MODEL), F32),
        compiler_params=_cparams("arbitrary", "arbitrary"),
        name="modulation",
    )(cond, w_mod, b_mod.reshape(DEPTH, 1, N_MOD * D_MODEL))
    return out.reshape(DEPTH * MOD_ROWS * N_MOD, 1, D_MODEL)


def _ab_in_kernel(xc_ref, xl_ref, g_ref, sc_ref, sh_ref, w_ref, cos_ref, sa_ref, sb_ref,
                  q_ref, k_ref, v_ref, u_ref, wq_s, wk_s, wv_s, wu_s):
    @pl.when(pl.program_id(0) == 0)
    def _():
        wq_s[...] = w_ref[:, 0:A_WIDTH].astype(BF16)
        wk_s[...] = w_ref[:, A_WIDTH:A_WIDTH + A_KV_WIDTH].astype(BF16)
        wv_s[...] = w_ref[:, A_WIDTH + A_KV_WIDTH:A_WIDTH + 2 * A_KV_WIDTH].astype(BF16)
        wu_s[...] = w_ref[:, A_WIDTH + 2 * A_KV_WIDTH:].astype(BF16)

    h = _norm_mod(_ctx_or_lat(xc_ref, xl_ref), g_ref[...], sc_ref[0], sh_ref[0]).astype(BF16)
    cos, sa, sb = cos_ref[...], sa_ref[...], sb_ref[...]
    q = jnp.dot(h, wq_s[...], preferred_element_type=F32)
    q_ref[...] = _apply_rope(q, cos, sa, sb, HEAD_DIM // 4).astype(q_ref.dtype)
    k = jnp.dot(h, wk_s[...], preferred_element_type=F32)
    k_ref[...] = _apply_rope(k, cos, sa, sb, HEAD_DIM // 4)
    v_ref[...] = jnp.dot(h, wv_s[...], preferred_element_type=F32)
    u_ref[...] = jnp.dot(h, wu_s[...], preferred_element_type=F32)


def _ab_in(x, mod, g, w, layer):
    cos, sa, sb = _rope_tables(HEAD_DIM)
    d_in = w.shape[1]
    row = lambda n: pl.BlockSpec((ROW_BLK, n), lambda i: (i, 0))
    rope = pl.BlockSpec((ROW_BLK, 128), lambda i: (_rope_blk(i), 0))
    return pl.pallas_call(
        _ab_in_kernel,
        grid=(N_ROW_BLK,),
        in_specs=_split_specs(D_MODEL) + [pl.BlockSpec((1, D_MODEL), lambda i: (0, 0)),
                  _mod_spec(layer, 1), _mod_spec(layer, 0),
                  pl.BlockSpec((D_MODEL, d_in), lambda i: (0, 0)), rope, rope, rope],
        out_specs=[row(A_WIDTH), row(A_KV_WIDTH), row(A_KV_WIDTH), row(S5_WIDTH)],
        out_shape=[jax.ShapeDtypeStruct((N_TOK, A_WIDTH), BF16), jax.ShapeDtypeStruct((N_TOK, A_KV_WIDTH), F32),
                   jax.ShapeDtypeStruct((N_TOK, A_KV_WIDTH), F32), jax.ShapeDtypeStruct((N_TOK, S5_WIDTH), F32)],
        scratch_shapes=[pltpu.VMEM((D_MODEL, A_WIDTH), BF16), pltpu.VMEM((D_MODEL, A_KV_WIDTH), BF16),
                        pltpu.VMEM((D_MODEL, A_KV_WIDTH), BF16), pltpu.VMEM((D_MODEL, S5_WIDTH), BF16)],
        compiler_params=_cparams("arbitrary"),
        name="ab_in",
    )(x[0], x[1], g.reshape(1, D_MODEL), mod, mod, w, cos, sa, sb)


def _softmax_pv(s_list, v_list, sink):
    m = functools.reduce(jnp.maximum, [jnp.max(s, axis=-1, keepdims=True) for s in s_list])
    if sink is not None:
        m = jnp.maximum(m, sink)
    ps = [jnp.exp(s - m) for s in s_list]
    l = functools.reduce(jnp.add, [jnp.sum(p, axis=-1, keepdims=True) for p in ps])
    if sink is not None:
        l = l + jnp.exp(sink - m)
    o = functools.reduce(jnp.add, [_dot(p, v) for p, v in zip(ps, v_list)])
    return o / l


def _gqa_with_sink(sink_ref, q, k_of, v_of, bias, o_ref):
    g = A_HEADS // A_KV_HEADS
    r = q.shape[0]
    outs = [None] * A_HEADS
    for kh in range(A_KV_HEADS):
        heads = range(g * kh, g * (kh + 1))
        qg = jnp.concatenate([q[:, HEAD_DIM * h:HEAD_DIM * (h + 1)] for h in heads], axis=0)
        sink = jnp.concatenate([jnp.full((r, 1), sink_ref[h], F32) for h in heads], axis=0)
        s = _dot_nt(qg, k_of(kh))
        if bias is not None:
            s = s + jnp.concatenate([bias] * g, axis=0)
        m = jnp.maximum(jnp.max(s, axis=-1, keepdims=True), sink)
        p = jnp.exp(s - m)
        l = jnp.sum(p, axis=-1, keepdims=True) + jnp.exp(sink - m)
        o = _dot(p, v_of(kh)) / l
        for n, h in enumerate(heads):
            outs[h] = o[r * n:r * (n + 1)]
    o_ref[...] = jnp.concatenate(outs, axis=1).astype(o_ref.dtype)


def _attn_ctx_kernel(sink_ref, q_ref, k_ref, v_ref, o_ref):
    scale = HEAD_DIM ** -0.5
    g = A_HEADS // A_KV_HEADS
    outs = []
    for h in range(A_HEADS):
        kh = h // g
        q = q_ref[:, HEAD_DIM * h:HEAD_DIM * (h + 1)]
        k = k_ref[:, HEAD_DIM * kh:HEAD_DIM * (kh + 1)]
        v = v_ref[:, HEAD_DIM * kh:HEAD_DIM * (kh + 1)]
        s = _dot_nt(q, k) * scale
        outs.append(_softmax_pv([s], [v], sink_ref[h]))
    o_ref[...] = jnp.concatenate(outs, axis=1).astype(o_ref.dtype)


def _attn_lat_kernel(sink_ref, q_ref, kp_ref, kc_ref, kn_ref, vp_ref, vc_ref, vn_ref, kx_ref, vx_ref, o_ref):
    n = pl.program_id(1)
    nb = DEC_SEQ // A_WINDOW
    i = lax.broadcasted_iota(jnp.int32, (A_WINDOW, A_WINDOW), 0)
    j = lax.broadcasted_iota(jnp.int32, (A_WINDOW, A_WINDOW), 1)
    zero = jnp.zeros((A_WINDOW, A_WINDOW), F32)
    bias = jnp.concatenate([jnp.where((j >= i) & (n > 0), 0.0, NEG_INF), zero,
                            jnp.where((j <= i) & (n < nb - 1), 0.0, NEG_INF),
                            jnp.zeros((A_WINDOW, PAST_LEN), F32)], axis=1)

    def rows(p_ref, c_ref, n_ref, x_ref):
        def of(kh):
            sl = slice(HEAD_DIM * kh, HEAD_DIM * (kh + 1))
            return jnp.concatenate([p_ref[:, sl], c_ref[:, sl], n_ref[:, sl], x_ref[0, :, sl]], axis=0)
        return of

    _gqa_with_sink(sink_ref, q_ref[...] * HEAD_DIM ** -0.5, rows(kp_ref, kc_ref, kn_ref, kx_ref),
                   rows(vp_ref, vc_ref, vn_ref, vx_ref), bias, o_ref)


def _attention_a(q, k, v, cache_k, cache_v, sink):
    smem = pl.BlockSpec(memory_space=pltpu.SMEM)
    o = pl.pallas_call(
        _attn_ctx_kernel,
        grid=(BATCH,),
        in_specs=[smem, pl.BlockSpec((SEQ, A_WIDTH), lambda b: (b, 0)),
                  pl.BlockSpec((SEQ, A_KV_WIDTH), lambda b: (b, 0)), pl.BlockSpec((SEQ, A_KV_WIDTH), lambda b: (b, 0))],
        out_specs=pl.BlockSpec((SEQ, A_WIDTH), lambda b: (b, 0)),
        out_shape=jax.ShapeDtypeStruct((N_CTX_TOK, A_WIDTH), BF16),
        compiler_params=_cparams("arbitrary"),
        name="attn_a_ctx",
    )(sink, q, k, v)
    nb = DEC_SEQ // A_WINDOW
    base = N_CTX_TOK // A_WINDOW
    cur = lambda b, n: (base + b * nb + n, 0)
    prev = lambda b, n: (base + b * nb + jnp.maximum(n - 1, 0), 0)
    nxt = lambda b, n: (base + b * nb + jnp.minimum(n + 1, nb - 1), 0)
    kv = lambda f: pl.BlockSpec((A_WINDOW, A_KV_WIDTH), f)
    cache = pl.BlockSpec((1, PAST_LEN, A_KV_WIDTH), lambda b, n: (b, 0, 0))
    o_lat = pl.pallas_call(
        _attn_lat_kernel,
        grid=(DEC_BATCH, nb),
        in_specs=[smem, pl.BlockSpec((A_WINDOW, A_WIDTH), cur), kv(prev), kv(cur), kv(nxt), kv(prev), kv(cur), kv(nxt),
                  cache, cache],
        out_specs=pl.BlockSpec((A_WINDOW, A_WIDTH), lambda b, n: (b * nb + n, 0)),
        out_shape=jax.ShapeDtypeStruct((N_LAT_TOK, A_WIDTH), BF16),
        compiler_params=_cparams("arbitrary", "arbitrary"),
        name="attn_a_lat",
    )(sink, q, k, k, k, v, v, v, cache_k, cache_v)
    return o, o_lat


S5_CHUNK = 16
S5_OCT = 128 // S5_GROUP
S5_NOCT = S5_GROUPS // S5_OCT
S5_K = S5_CHUNK * 128
S5_PART = S5_OCT * S5_STATE
S5_SW = 4 * S5_PART
S5_ROWS_CTX = BATCH * SEQ // S5_CHUNK
S5_ROWS_LAT = DEC_BATCH * DEC_SEQ // S5_CHUNK
S5_ROWS = S5_ROWS_CTX + S5_ROWS_LAT
S5_NB = 4
HI = lax.Precision.HIGHEST


def _s5_disc_kernel(lr_ref, li_ref, ldt_ref, ar_ref, ai_ref, zr_ref, zi_ref):
    lr, li = lr_ref[...], li_ref[...]
    dt = jnp.exp(ldt_ref[...])
    mag = jnp.exp(lr * dt)
    ar, ai = mag * jnp.cos(li * dt), mag * jnp.sin(li * dt)
    den = lr * lr + li * li
    ar_ref[...] = ar
    ai_ref[...] = ai
    zr_ref[...] = ((ar - 1.0) * lr + ai * li) / den
    zi_ref[...] = (ai * lr - (ar - 1.0) * li) / den


def _cmul(xr, xi, yr, yi):
    return xr * yr - xi * yi, xr * yi + xi * yr


def _dot_nt_hi(a, b):
    return lax.dot_general(a, b, (((1,), (1,)), ((), ())), precision=HI, preferred_element_type=F32)


def _s5_prep_kernel(ar_ref, ai_ref, zr_ref, zi_ref, btr_ref, bti_ref, ctr_ref, cti_ref,
                    m_ref, win_ref, wout_ref, a16_ref, pw_s, w_s, k_s):
    t = pl.program_id(1)
    npw = S5_CHUNK + 1
    blk = lambda j: pl.ds(pl.multiple_of(j * 128, 128), 128)

    @pl.when(t == 0)
    def _():
        kd = []
        for d in range(2):
            ar, ai = ar_ref[0, d], ai_ref[0, d]
            pr, pi = jnp.ones_like(ar), jnp.zeros_like(ar)
            bbr, bbi = _cmul(zr_ref[0, d], zi_ref[0, d], btr_ref[0, d], bti_ref[0, d])
            for j in range(npw):
                pw_s[d, 0, j:j + 1, :] = pr
                pw_s[d, 1, j:j + 1, :] = pi
                if j < S5_CHUNK:
                    wr, wi = _cmul(pr, pi, bbr, bbi)
                    w_s[d, 0, 128 * j:128 * (j + 1), :] = wr.astype(BF16)
                    w_s[d, 1, 128 * j:128 * (j + 1), :] = wi.astype(BF16)
                pr, pi = _cmul(pr, pi, ar, ai)
            kd.append(_dot_nt(w_s[d, 0], ctr_ref[0, d]) - _dot_nt(w_s[d, 1], cti_ref[0, d]))
        for jj in range(2 * S5_CHUNK - 1):
            j = jj - (S5_CHUNK - 1)
            if j > 0:
                k = kd[0][128 * j:128 * (j + 1)]
            elif j < 0:
                k = kd[1][128 * -j:128 * (1 - j)]
            else:
                k = kd[0][0:128] + kd[1][0:128]
            k_s[128 * jj:128 * (jj + 1), :] = k
        a16_ref[0] = jnp.concatenate([pw_s[0, 0, S5_CHUNK:npw, :], pw_s[1, 0, S5_CHUNK:npw, :],
                                      pw_s[0, 1, S5_CHUNK:npw, :], pw_s[1, 1, S5_CHUNK:npw, :]], axis=1)

    for tp in range(S5_CHUNK):
        m_ref[0, :, 128 * tp:128 * (tp + 1)] = k_s[blk(S5_CHUNK - 1 + tp - t), :].astype(BF16)

    def power(d, e):
        return pw_s[d, 0, pl.ds(e, 1), :], pw_s[d, 1, pl.ds(e, 1), :]

    for d in range(2):
        j = (S5_CHUNK - 1 - t) if d == 0 else t
        win_ref[0, :, S5_PART * d:S5_PART * (d + 1)] = w_s[d, 0, blk(j), :]
        win_ref[0, :, S5_PART * (2 + d):S5_PART * (3 + d)] = w_s[d, 1, blk(j), :]
        er, ei = _cmul(*power(d, (t + 1) if d == 0 else (S5_CHUNK - t)), ctr_ref[0, d], cti_ref[0, d])
        wout_ref[0, :, S5_PART * d:S5_PART * (d + 1)] = er.astype(BF16)
        wout_ref[0, :, S5_PART * (2 + d):S5_PART * (3 + d)] = (-ei).astype(BF16)


def _s5_main_kernel(u_ref, win_ref, m_ref, wout_ref, a16_ref, h0_ref, y_ref, hfin_ref, uo_s, x_s, hs_s):
    s = pl.program_id(1)
    nq = S5_PART // 128

    @pl.when(s == 0)
    def _():
        for t in range(S5_CHUNK):
            uo_s[:, 128 * t:128 * (t + 1)] = u_ref[pl.ds(t, S5_ROWS, stride=S5_CHUNK), :].astype(BF16)

    @pl.when(s < S5_NB)
    def _():
        x = jnp.dot(uo_s[...], win_ref[0], preferred_element_type=F32)
        for q in range(nq):
            x_s[s * nq + q] = x[:, 128 * q:128 * (q + 1)]

    @pl.when(s == S5_NB - 1)
    def _():
        def run(row0, nb, nc, h):
            for c in range(nc):
                for d in range(2):
                    cc = c if d == 0 else nc - 1 - c
                    rows = pl.ds(row0 + cc, nb, stride=nc)
                    for q in range(nq):
                        kr, ki = d * nq + q, (2 + d) * nq + q
                        hr, hi = h[d][0][q], h[d][1][q]
                        hs_s[kr, rows, :] = hr
                        hs_s[ki, rows, :] = hi
                        ar = a16_ref[0, :, 128 * kr:128 * (kr + 1)]
                        ai = a16_ref[0, :, 128 * ki:128 * (ki + 1)]
                        h[d][0][q] = ar * hr - ai * hi + x_s[kr, rows, :]
                        h[d][1][q] = ar * hi + ai * hr + x_s[ki, rows, :]
            return h

        zero = jnp.zeros((BATCH, 128), F32)
        fin = run(0, BATCH, SEQ // S5_CHUNK, [[[zero] * nq, [zero] * nq] for _ in range(2)])
        for d in range(2):
            for ri in range(2):
                for q in range(nq):
                    k = (2 * ri + d) * nq + q
                    hfin_ref[0, :, 128 * k:128 * (k + 1)] = fin[d][ri][q]
        h0 = [[[h0_ref[0, :, 128 * ((2 * ri + d) * nq + q):128 * ((2 * ri + d) * nq + q + 1)] for q in range(nq)]
               for ri in range(2)] for d in range(2)]
        run(S5_ROWS_CTX, DEC_BATCH, DEC_SEQ // S5_CHUNK, h0)

    @pl.when(s >= S5_NB)
    def _():
        hs = jnp.concatenate([hs_s[k] for k in range(4 * nq)], axis=1).astype(BF16)
        y = (jnp.dot(uo_s[...], m_ref[0], preferred_element_type=F32)
             + lax.dot_general(hs, wout_ref[0], (((1,), (1,)), ((), ())), preferred_element_type=F32))
        for q in range(nq):
            t = (s - S5_NB) * nq + q
            y_ref[pl.ds(t, S5_ROWS, stride=S5_CHUNK), :] = y[:, 128 * q:128 * (q + 1)]


def _s5_octets(t, lanes):
    return t.reshape(2, S5_NOCT, 1, S5_OCT * lanes).transpose(1, 0, 2, 3)


def _s5_blockdiag(t):
    a, n = t.shape[2], t.shape[3]
    t = t.reshape(2, S5_NOCT, S5_OCT, a, n)
    bd = jnp.einsum('dogan,gh->dogahn', t, jnp.eye(S5_OCT, dtype=t.dtype))
    return bd.reshape(2, S5_NOCT, S5_OCT * a, S5_OCT * n).transpose(1, 0, 2, 3)


def _s5_state_to_lanes(h_re, h_im):
    b = h_re.shape[0]
    parts = jnp.stack([h_re[:, 0], h_re[:, 1], h_im[:, 0], h_im[:, 1]], axis=1)
    parts = parts.reshape(b, 4, S5_NOCT, S5_PART).transpose(2, 0, 1, 3)
    return parts.reshape(S5_NOCT, b, S5_SW)


def _s5_state_from_lanes(h):
    b = h.shape[1]
    parts = h.reshape(S5_NOCT, b, 4, S5_OCT, S5_STATE).transpose(1, 2, 0, 3, 4).reshape(b, 4, S5_GROUPS, S5_STATE)
    return parts[:, 0:2], parts[:, 2:4]


def _s5_scan(u, h0_re, h0_im, lam_re, lam_im, log_dt, b_re, b_im, c_re, c_im):
    ng, n = S5_GROUPS, S5_STATE
    rows = 2 * ng
    disc = pl.pallas_call(
        _s5_disc_kernel,
        out_shape=[jax.ShapeDtypeStruct((rows, n), F32)] * 4,
        name="s5_disc",
    )(lam_re.reshape(rows, n), lam_im.reshape(rows, n), log_dt.reshape(rows, 1))
    ar, ai, zr, zi = [_s5_octets(t.reshape(2, ng, n), n) for t in disc]
    bt = lambda t: _s5_blockdiag(t.transpose(0, 1, 3, 2))
    vec = pl.BlockSpec((1, 2, 1, S5_PART), lambda o, t: (o, 0, 0, 0))
    mat = pl.BlockSpec((1, 2, 128, S5_PART), lambda o, t: (o, 0, 0, 0))
    rowblk = lambda w: pl.BlockSpec((1, 128, w), lambda o, t: (o, t, 0))
    m, win, wout, a16 = pl.pallas_call(
        _s5_prep_kernel,
        grid=(S5_NOCT, S5_CHUNK),
        in_specs=[vec, vec, vec, vec, mat, mat, mat, mat],
        out_specs=[rowblk(S5_K), rowblk(S5_SW), rowblk(S5_SW), pl.BlockSpec((1, 1, S5_SW), lambda o, t: (o, 0, 0))],
        out_shape=[jax.ShapeDtypeStruct((S5_NOCT, S5_K, S5_K), BF16), jax.ShapeDtypeStruct((S5_NOCT, S5_K, S5_SW), BF16),
                   jax.ShapeDtypeStruct((S5_NOCT, S5_K, S5_SW), BF16), jax.ShapeDtypeStruct((S5_NOCT, 1, S5_SW), F32)],
        scratch_shapes=[pltpu.VMEM((2, 2, 24, S5_PART), F32), pltpu.VMEM((2, 2, S5_K, S5_PART), BF16),
                        pltpu.VMEM(((2 * S5_CHUNK - 1) * 128, 128), F32)],
        compiler_params=_cparams("arbitrary", "arbitrary"),
        name="s5_prep",
    )(ar, ai, zr, zi, bt(b_re), bt(b_im), _s5_blockdiag(c_re), _s5_blockdiag(c_im))
    nb = S5_NB
    y, hfin = pl.pallas_call(
        _s5_main_kernel,
        grid=(S5_NOCT, 2 * nb),
        in_specs=[pl.BlockSpec((N_TOK, 128), lambda o, s: (0, o)),
                  pl.BlockSpec((1, S5_K, S5_PART), lambda o, s: (o, 0, jnp.minimum(s, nb - 1))),
                  pl.BlockSpec((1, S5_K, S5_PART), lambda o, s: (o, 0, jnp.maximum(s - nb, 0))),
                  pl.BlockSpec((1, S5_PART, S5_SW), lambda o, s: (o, jnp.maximum(s - nb, 0), 0)),
                  pl.BlockSpec((1, 1, S5_SW), lambda o, s: (o, 0, 0)),
                  pl.BlockSpec((1, DEC_BATCH, S5_SW), lambda o, s: (o, 0, 0))],
        out_specs=[pl.BlockSpec((N_TOK, 128), lambda o, s: (0, o)),
                   pl.BlockSpec((1, BATCH, S5_SW), lambda o, s: (o, 0, 0))],
        out_shape=[jax.ShapeDtypeStruct((N_TOK, S5_WIDTH), F32), jax.ShapeDtypeStruct((S5_NOCT, BATCH, S5_SW), F32)],
        scratch_shapes=[pltpu.VMEM((S5_ROWS, S5_K), BF16), pltpu.VMEM((S5_SW // 128, S5_ROWS, 128), F32),
                        pltpu.VMEM((S5_SW // 128, S5_ROWS, 128), F32)],
        compiler_params=_cparams("arbitrary", "arbitrary"),
        name="s5_main",
    )(u, win, m, wout, a16, _s5_state_to_lanes(h0_re, h0_im))
    fin_re, fin_im = _s5_state_from_lanes(hfin)
    return y, fin_re, fin_im


def _ctx_or_lat(ctx_ref, lat_ref):
    return jnp.where(pl.program_id(0) < N_CTX_BLK, ctx_ref[...], lat_ref[...])


def _split_specs(width):
    return [pl.BlockSpec((ROW_BLK, width), lambda i: (jnp.minimum(i, N_CTX_BLK - 1), 0)),
            pl.BlockSpec((ROW_BLK, width), lambda i: (jnp.maximum(i - N_CTX_BLK, 0), 0))]


def _ab_out_kernel(xc_ref, xl_ref, oac_ref, oal_ref, y_ref, u_ref, d_ref, wglu_ref, wout_ref, gt_ref, o_ref,
                   wglu_s, wa_s, wb_s):
    @pl.when(pl.program_id(0) == 0)
    def _():
        wglu_s[...] = wglu_ref[...].astype(BF16)
        wa_s[...] = wout_ref[0:A_WIDTH, :].astype(BF16)
        wb_s[...] = wout_ref[A_WIDTH:, :].astype(BF16)

    g = jax.nn.gelu(y_ref[...] + d_ref[...] * u_ref[...])
    ob = g * jax.nn.sigmoid(jnp.dot(g.astype(BF16), wglu_s[...], preferred_element_type=F32))
    out = (jnp.dot(_ctx_or_lat(oac_ref, oal_ref).astype(BF16), wa_s[...], preferred_element_type=F32)
           + jnp.dot(ob.astype(BF16), wb_s[...], preferred_element_type=F32))
    o_ref[...] = _ctx_or_lat(xc_ref, xl_ref) + gt_ref[0] * out


def _ab_out(x, oa, y, u, d_skip, w_glu, w_out, mod, layer):
    row = lambda n: pl.BlockSpec((ROW_BLK, n), lambda i: (i, 0))
    full = lambda a, b: pl.BlockSpec((a, b), lambda i: (0, 0))
    return pl.pallas_call(
        _ab_out_kernel,
        grid=(N_ROW_BLK,),
        in_specs=_split_specs(D_MODEL) + _split_specs(A_WIDTH) + [row(S5_WIDTH), row(S5_WIDTH), full(1, S5_WIDTH),
                  full(S5_WIDTH, S5_WIDTH), full(A_WIDTH + S5_WIDTH, D_MODEL), _mod_spec(layer, 2)],
        out_specs=row(D_MODEL),
        out_shape=jax.ShapeDtypeStruct((N_TOK, D_MODEL), F32),
        scratch_shapes=[pltpu.VMEM((S5_WIDTH, S5_WIDTH), BF16), pltpu.VMEM((A_WIDTH, D_MODEL), BF16),
                        pltpu.VMEM((S5_WIDTH, D_MODEL), BF16)],
        compiler_params=_cparams("arbitrary"),
        name="ab_out",
    )(x[0], x[1], oa[0], oa[1], y, u, d_skip.reshape(1, S5_WIDTH), w_glu, w_out, mod)


CD_HG0 = MLA_Q_RANK + MLA_KV_RANK + MLA_ROPE
KPE_LANES = 128


MLA_TILE = 128
MLA_QK = MLA_HEADS * MLA_TILE


def _mla_head_tiles(w, a):
    k, n = w.shape
    w = w.reshape(k, MLA_HEADS, n // MLA_HEADS)
    tiles = jnp.pad(w[:, :, :a], ((0, 0), (0, 0), (0, MLA_TILE - a)))
    return jnp.concatenate([tiles.reshape(k, -1), w[:, :, a:].reshape(k, -1)], axis=1)


def _mla_key_tiles(k_tiles, kpe):
    lane = lax.broadcasted_iota(jnp.int32, (1, MLA_TILE), 1)
    pe = jnp.where((lane >= MLA_NOPE) & (lane < MLA_NOPE + MLA_ROPE), kpe, 0.0)
    return k_tiles + jnp.concatenate([pe] * MLA_HEADS, axis=1)


def _cd_in_kernel(x_ref, y_ref, gp_ref, g_ref, sc_ref, sh_ref, w_ref, gq_ref, wqu_ref, gkv_ref, wkvu_ref,
                  cos_ref, sa_ref, sb_ref, cq_ref, sq_ref, tq_ref,
                  x1_ref, qf_ref, ckv_ref, kpe_ref, kf_ref, vm_ref, hq_ref, hff_ref, hfb_ref, hi_ref, hg_ref,
                  wcq_s, wckv_s, wkpe_s, whg_s, wqu_s, wkvu_s):
    @pl.when(pl.program_id(0) == 0)
    def _():
        wcq_s[...] = w_ref[:, 0:MLA_Q_RANK].astype(BF16)
        wckv_s[...] = w_ref[:, MLA_Q_RANK:MLA_Q_RANK + MLA_KV_RANK].astype(BF16)
        kp = w_ref[:, MLA_Q_RANK + MLA_KV_RANK:CD_HG0].astype(BF16)
        wkpe_s[...] = jnp.concatenate([kp] * (KPE_LANES // MLA_ROPE), axis=1)
        whg_s[...] = w_ref[:, CD_HG0:].astype(BF16)
        wqu_s[...] = wqu_ref[...].astype(BF16)
        wkvu_s[...] = wkvu_ref[...].astype(BF16)

    x = x_ref[...] + gp_ref[0] * y_ref[...]
    x1_ref[...] = x
    h = _norm_mod(x, g_ref[...], sc_ref[0], sh_ref[0]).astype(BF16)
    cq = _rms(jnp.dot(h, wcq_s[...], preferred_element_type=F32)) * gq_ref[...]
    qq = jnp.dot(cq.astype(BF16), wqu_s[...], preferred_element_type=F32)
    qf_ref[...] = _apply_rope(qq, cq_ref[...], sq_ref[...], tq_ref[...], MLA_ROPE // 4).astype(qf_ref.dtype)
    ckv = _rms(jnp.dot(h, wckv_s[...], preferred_element_type=F32)) * gkv_ref[...]
    ckv_ref[...] = ckv
    kv = jnp.dot(ckv.astype(BF16), wkvu_s[...], preferred_element_type=F32)
    kpe = _apply_rope(jnp.dot(h, wkpe_s[...], preferred_element_type=F32), cos_ref[...], sa_ref[...], sb_ref[...],
                      MLA_ROPE // 4)
    kpe_ref[...] = kpe
    kf_ref[...] = _mla_key_tiles(kv[:, 0:MLA_QK], kpe).astype(kf_ref.dtype)
    vm_ref[...] = kv[:, MLA_QK:].astype(vm_ref.dtype)
    hh = jnp.dot(h, whg_s[...], preferred_element_type=F32)
    for n, ref in enumerate((hq_ref, hff_ref, hfb_ref, hi_ref, hg_ref)):
        ref[...] = hh[:, HG_KD * n:HG_KD * (n + 1)]


def _cd_in(x, y_prev, mod, g, w, g_q, w_q_up, g_kv, w_kv_up, layer):
    cos, sa, sb = _rope_tables(MLA_ROPE)
    cos_q, sa_q, sb_q = _rope_tables(MLA_ROPE, only=(MLA_NOPE, MLA_NOPE + MLA_ROPE))
    d_in = w.shape[1]
    row = lambda n: pl.BlockSpec((ROW_BLK, n), lambda i: (i, 0))
    full = lambda a, b: pl.BlockSpec((a, b), lambda i: (0, 0))
    rope = pl.BlockSpec((ROW_BLK, 128), lambda i: (_rope_blk(i), 0))
    outs = ([(D_MODEL, F32), (MLA_QK, BF16), (MLA_KV_RANK, F32), (KPE_LANES, F32), (MLA_QK, BF16), (MLA_WIDTH, BF16)]
            + [(HG_KD, F32)] * 5)
    nkv = MLA_QK + MLA_WIDTH
    return pl.pallas_call(
        _cd_in_kernel,
        grid=(N_ROW_BLK,),
        in_specs=[row(D_MODEL), row(D_MODEL), _mod_spec(layer - 1, 5),
                  full(1, D_MODEL), _mod_spec(layer, 1), _mod_spec(layer, 0), full(D_MODEL, d_in),
                  full(1, MLA_Q_RANK), full(MLA_Q_RANK, MLA_QK), full(1, MLA_KV_RANK), full(MLA_KV_RANK, nkv),
                  rope, rope, rope, rope, rope, rope],
        out_specs=[row(n) for n, _ in outs],
        out_shape=[jax.ShapeDtypeStruct((N_TOK, n), dt) for n, dt in outs],
        scratch_shapes=[pltpu.VMEM((D_MODEL, MLA_Q_RANK), BF16), pltpu.VMEM((D_MODEL, MLA_KV_RANK), BF16),
                        pltpu.VMEM((D_MODEL, KPE_LANES), BF16), pltpu.VMEM((D_MODEL, 5 * HG_KD), BF16),
                        pltpu.VMEM((MLA_Q_RANK, MLA_QK), BF16), pltpu.VMEM((MLA_KV_RANK, nkv), BF16)],
        compiler_params=_cparams("arbitrary"),
        name="cd_in",
    )(x, y_prev, mod, g.reshape(1, D_MODEL), mod, mod, w, g_q.reshape(1, -1), _mla_head_tiles(w_q_up, MLA_NOPE + MLA_ROPE),
      g_kv.reshape(1, -1), _mla_head_tiles(w_kv_up, MLA_NOPE), cos, sa, sb, cos_q, sa_q, sb_q)


def _mla_cache_kernel(c_ref, kpe_ref, w_ref, kf_ref, v_ref):
    kv = _dot(c_ref[...], w_ref[...])
    kf_ref[...] = _mla_key_tiles(kv[:, 0:MLA_QK], kpe_ref[...]).astype(kf_ref.dtype)
    v_ref[...] = kv[:, MLA_QK:].astype(v_ref.dtype)


def _mla_cache_kv(cckv, ckpe, w_kv_up):
    rows = cckv.shape[0]
    blk = lambda w: pl.BlockSpec((PAST_LEN, w), lambda i: (i, 0))
    return pl.pallas_call(
        _mla_cache_kernel,
        grid=(rows // PAST_LEN,),
        in_specs=[blk(MLA_KV_RANK), blk(KPE_LANES), pl.BlockSpec((MLA_KV_RANK, MLA_QK + MLA_WIDTH), lambda i: (0, 0))],
        out_specs=[blk(MLA_QK), blk(MLA_WIDTH)],
        out_shape=[jax.ShapeDtypeStruct((rows, MLA_QK), BF16), jax.ShapeDtypeStruct((rows, MLA_WIDTH), BF16)],
        compiler_params=_cparams("arbitrary"),
        name="mla_cache_kv",
    )(cckv, jnp.tile(ckpe, (1, KPE_LANES // MLA_ROPE)), _mla_head_tiles(w_kv_up, MLA_NOPE))


def _mla_heads(qf, keys, o_ref):
    scale = (MLA_NOPE + MLA_ROPE) ** -0.5
    outs = []
    for h in range(MLA_HEADS):
        t = slice(MLA_TILE * h, MLA_TILE * (h + 1))
        a = slice(MLA_V * h, MLA_V * (h + 1))
        s_list = [_dot_nt(qf[:, t], kf[:, t]) * scale for kf, _ in keys]
        outs.append(_softmax_pv(s_list, [v[:, a] for _, v in keys], None))
    o_ref[...] = jnp.concatenate(outs, axis=1).astype(o_ref.dtype)


def _mla_ctx_kernel(qf_ref, kf_ref, v_ref, o_ref):
    _mla_heads(qf_ref[...], [(kf_ref[...], v_ref[...])], o_ref)


def _mla_lat_kernel(qf_ref, kf_ref, v_ref, kcf_ref, vc_ref, o_ref):
    _mla_heads(qf_ref[...], [(kcf_ref[...], vc_ref[...]), (kf_ref[...], v_ref[...])], o_ref)


def _attention_mla(qf, kf, vm, kcf, vc):
    blk = lambda w: pl.BlockSpec((SEQ, w), lambda b: (b, 0))
    o = pl.pallas_call(
        _mla_ctx_kernel,
        grid=(BATCH,),
        in_specs=[blk(MLA_QK), blk(MLA_QK), blk(MLA_WIDTH)],
        out_specs=blk(MLA_WIDTH),
        out_shape=jax.ShapeDtypeStruct((N_CTX_TOK, MLA_WIDTH), BF16),
        compiler_params=_cparams("arbitrary"),
        name="mla_ctx",
    )(qf, kf, vm)
    nq = DEC_SEQ // ROW_BLK
    qblk = lambda w: pl.BlockSpec((ROW_BLK, w), lambda b, i: (N_CTX_BLK + b * nq + i, 0))
    seq = lambda w: pl.BlockSpec((DEC_SEQ, w), lambda b, i: (N_CTX_TOK // DEC_SEQ + b, 0))
    past = lambda w: pl.BlockSpec((PAST_LEN, w), lambda b, i: (b, 0))
    o_lat = pl.pallas_call(
        _mla_lat_kernel,
        grid=(DEC_BATCH, nq),
        in_specs=[qblk(MLA_QK), seq(MLA_QK), seq(MLA_WIDTH), past(MLA_QK), past(MLA_WIDTH)],
        out_specs=pl.BlockSpec((ROW_BLK, MLA_WIDTH), lambda b, i: (b * nq + i, 0)),
        out_shape=jax.ShapeDtypeStruct((N_LAT_TOK, MLA_WIDTH), BF16),
        compiler_params=_cparams("arbitrary", "arbitrary"),
        name="mla_lat",
    )(qf, kf, vm, kcf, vc)
    return o, o_lat


HG_TILE = 128
HG_NC = HG_TILE // HG_CHUNK
HG_HALF = 256


HG_SLABS = HG_KD // 128


def _hg_token_plane(raw_s, p):
    return jnp.concatenate([raw_s[j, pl.ds(p, HG_NC, stride=HG_CHUNK), :] for j in range(HG_SLABS)], axis=1)


def _hg_put(ref, x):
    for j in range(HG_SLABS):
        ref[j] = x[:, 128 * j:128 * (j + 1)]


def _hg_get(ref, c, l):
    rows = pl.ds(c, HG_CHUNK, stride=HG_NC)
    return jnp.concatenate([ref[2 * l, rows, :], ref[2 * l + 1, rows, :]], axis=1)


def _hg_plane(ref, p):
    return jnp.concatenate([ref[j, HG_NC * p:HG_NC * (p + 1), :] for j in range(HG_SLABS)], axis=1)


def _hg_put_plane(ref, p, x):
    for j in range(HG_SLABS):
        ref[j, HG_NC * p:HG_NC * (p + 1), :] = x[:, 128 * j:128 * (j + 1)]


HG_PAIRS = HG_CHUNK * (HG_CHUNK + 1) // 2
HG_PAIR_ROWS = HG_PAIRS * HG_NC


def _hg_pair_rows(d, i):
    n = d * HG_CHUNK - d * (d - 1) // 2 + (i - d)
    return slice(HG_NC * n, HG_NC * (n + 1))


def _hg_direction(hq_ref, hf_ref, hi_ref, o_ref, lb, st_ref, raw_s, q_s, f_s, kk_s, v_s, qt_s, kt_s, o_s, p_s, e_s,
                  sign, ones_bd, head_mask):
    _hg_put(raw_s, hq_ref[...])
    for p in range(HG_CHUNK):
        hq = _hg_token_plane(raw_s, p)
        _hg_put_plane(q_s, p, hq * jax.nn.sigmoid(hq))
    _hg_put(raw_s, hf_ref[...])
    for p in range(HG_CHUNK):
        f = lb + (1.0 - lb) * jax.nn.sigmoid(_hg_token_plane(raw_s, p))
        _hg_put_plane(f_s, p, f)
        _hg_put_plane(kk_s, p, 1.0 - f)
    _hg_put(raw_s, hi_ref[...])
    for p in range(HG_CHUNK):
        _hg_put_plane(v_s, p, _hg_token_plane(raw_s, p))
    pos = (lambda i: i) if sign > 0 else (lambda i: HG_CHUNK - 1 - i)
    plane = lambda ref, i: _hg_plane(ref, pos(i))

    for i in range(HG_CHUNK):
        q = plane(q_s, i)
        p_s[_hg_pair_rows(0, i), :] = q * plane(kk_s, i)
        dec = None
        for d in range(1, i + 1):
            fd = plane(f_s, i - d + 1)
            dec = fd if dec is None else dec * fd
            p_s[_hg_pair_rows(d, i), :] = q * plane(kk_s, i - d) * dec
    step = HG_PAIR_ROWS // 4
    for c in range(0, HG_PAIR_ROWS, step):
        pb = p_s[c:c + step, :].astype(BF16)
        e_s[c:c + step, 0:HG_HALF] = jnp.dot(pb[:, 0:HG_HALF], ones_bd, preferred_element_type=F32)
        e_s[c:c + step, HG_HALF:] = jnp.dot(pb[:, HG_HALF:], ones_bd, preferred_element_type=F32)
    for i in range(HG_CHUNK):
        o = None
        for d in range(i + 1):
            t = e_s[_hg_pair_rows(d, i), :] * plane(v_s, i - d)
            o = t if o is None else o + t
        _hg_put_plane(o_s, pos(i), o)

    incl = None
    for i in range(HG_CHUNK):
        fi = plane(f_s, i)
        incl = fi if incl is None else incl * fi
        _hg_put_plane(qt_s, pos(i), plane(q_s, i) * incl)
    whole = incl
    excl = None
    for i in range(HG_CHUNK - 1, -1, -1):
        kt = plane(kk_s, i)
        if excl is not None:
            kt = kt * excl
        _hg_put_plane(kt_s, pos(i), kt)
        fi = plane(f_s, i)
        excl = fi if excl is None else excl * fi

    tile4 = lambda x: jnp.where(head_mask, jnp.concatenate([x] * 4, axis=0), 0.0).astype(BF16)
    for cc in range(HG_NC):
        c = cc if sign > 0 else HG_NC - 1 - cc
        r = slice(HG_CHUNK * c, HG_CHUNK * (c + 1))
        for g in range(2):
            l = slice(HG_HALF * g, HG_HALF * (g + 1))
            st = st_ref[g]
            oc = _dot_nt(tile4(_hg_get(qt_s, c, g)), st)
            o_ref[r, l] = _hg_get(o_s, c, g) + jnp.concatenate(
                [oc[HG_CHUNK * h:HG_CHUNK * (h + 1), :] for h in range(4)], axis=1)
            v = hi_ref[r, l]
            vs = jnp.concatenate([v[:, HG_VAL * h:HG_VAL * (h + 1)] for h in range(4)], axis=0)
            ds = lax.dot_general(vs.astype(BF16), tile4(_hg_get(kt_s, c, g)), (((0,), (0,)), ((), ())),
                                 preferred_element_type=F32)
            st_ref[g] = st * whole[c:c + 1, l] + ds


def _hg_lower_bound(lb_ref, d, layer):
    raw = [lb_ref[d * DEPTH + m:d * DEPTH + m + 1, :] for m in range(DEPTH)]
    mx = functools.reduce(jnp.maximum, raw)
    e = [jnp.exp(r - mx) for r in raw]
    tot = functools.reduce(jnp.add, e)
    return functools.reduce(jnp.add, e[1:layer + 1], jnp.zeros_like(tot)) / tot


def _hgrn_kernel(nt, layer, hqf_ref, hff_ref, hif_ref, hqb_ref, hfb_ref, hib_ref, lb_ref, s0_ref,
                 of_ref, ob_ref, sfin_ref, st_s, raw_s, q_s, f_s, kk_s, v_s, qt_s, kt_s, o_s, p_s, e_s):
    i = pl.program_id(1)

    @pl.when(i == 0)
    def _():
        st_s[...] = s0_ref[0]

    ri = lax.broadcasted_iota(jnp.int32, (HG_HALF, HG_HALF), 0) // HG_KEY
    ci = lax.broadcasted_iota(jnp.int32, (HG_HALF, HG_HALF), 1) // HG_KEY
    ones_bd = jnp.where(ri == ci, 1.0, 0.0).astype(BF16)
    head_mask = (lax.broadcasted_iota(jnp.int32, (4 * HG_CHUNK, HG_HALF), 0) // HG_CHUNK
                 == lax.broadcasted_iota(jnp.int32, (4 * HG_CHUNK, HG_HALF), 1) // HG_KEY)
    _hg_direction(hqf_ref, hff_ref, hif_ref, of_ref, _hg_lower_bound(lb_ref, 0, layer), st_s.at[0],
                  raw_s, q_s, f_s, kk_s, v_s, qt_s, kt_s, o_s, p_s, e_s, 1, ones_bd, head_mask)
    _hg_direction(hqb_ref, hfb_ref, hib_ref, ob_ref, _hg_lower_bound(lb_ref, 1, layer), st_s.at[1],
                  raw_s, q_s, f_s, kk_s, v_s, qt_s, kt_s, o_s, p_s, e_s, -1, ones_bd, head_mask)

    @pl.when(i == nt - 1)
    def _():
        sfin_ref[0] = st_s[...]


def _hg_state_to_blocks(s):
    b = s.shape[0]
    st = s.reshape(b, 2, 2, 4, HG_KEY, HG_VAL).transpose(0, 1, 2, 5, 3, 4)
    return st.reshape(b, 2, 2, HG_VAL, HG_HALF)


def _hg_state_from_blocks(st):
    b = st.shape[0]
    st = st.reshape(b, 2, 2, HG_VAL, 4, HG_KEY).transpose(0, 1, 2, 4, 5, 3)
    return st.reshape(b, 2, HG_HEADS, HG_KEY, HG_VAL)


def _hgrn_scan(hq, hff, hfb, hi, lb, s0, layer, row0, nseq, seqlen):
    nt = seqlen // HG_TILE
    base = row0 // HG_TILE
    fwd = pl.BlockSpec((HG_TILE, HG_KD), lambda b, i: (base + b * nt + i, 0))
    bwd = pl.BlockSpec((HG_TILE, HG_KD), lambda b, i: (base + b * nt + nt - 1 - i, 0))
    ofw = pl.BlockSpec((HG_TILE, HG_KD), lambda b, i: (b * nt + i, 0))
    obw = pl.BlockSpec((HG_TILE, HG_KD), lambda b, i: (b * nt + nt - 1 - i, 0))
    st = pl.BlockSpec((1, 2, 2, HG_VAL, HG_HALF), lambda b, i: (b, 0, 0, 0, 0))
    tile = lambda: pltpu.VMEM((HG_SLABS, HG_TILE, 128), F32)
    pairs = lambda: pltpu.VMEM((HG_PAIR_ROWS, HG_KD), F32)
    return pl.pallas_call(
        functools.partial(_hgrn_kernel, nt, layer),
        grid=(nseq, nt),
        in_specs=[fwd, fwd, fwd, bwd, bwd, bwd, pl.BlockSpec((2 * DEPTH, HG_KD), lambda b, i: (0, 0)), st],
        out_specs=[ofw, obw, st],
        out_shape=[jax.ShapeDtypeStruct((nseq * seqlen, HG_WIDTH), F32)] * 2
        + [jax.ShapeDtypeStruct((nseq, 2, 2, HG_VAL, HG_HALF), F32)],
        scratch_shapes=[pltpu.VMEM((2, 2, HG_VAL, HG_HALF), F32)] + [tile() for _ in range(8)] + [pairs(), pairs()],
        compiler_params=_cparams("arbitrary", "arbitrary"),
        name="hgrn_scan",
    )(hq, hff, hi, hq, hfb, hi, lb.reshape(2 * DEPTH, HG_KD), s0)


def _cd_out_kernel(x_ref, occ_ref, ocl_ref, ofc_ref, ofl_ref, obc_ref, obl_ref, hg_ref, go_ref, wout_ref, gt_ref,
                   o_ref, wa_s, wb_s):
    @pl.when(pl.program_id(0) == 0)
    def _():
        wa_s[...] = wout_ref[0:MLA_WIDTH, :].astype(BF16)
        wb_s[...] = wout_ref[MLA_WIDTH:, :].astype(BF16)

    ri = lax.broadcasted_iota(jnp.int32, (HG_HALF, HG_HALF), 0) // HG_VAL
    ci = lax.broadcasted_iota(jnp.int32, (HG_HALF, HG_HALF), 1) // HG_VAL
    ones_bd = jnp.where(ri == ci, 1.0, 0.0).astype(BF16)
    o = _ctx_or_lat(ofc_ref, ofl_ref) + _ctx_or_lat(obc_ref, obl_ref)
    sq = o * o
    hi = sq.astype(BF16)
    lo = (sq - hi.astype(F32)).astype(BF16)
    ms = jnp.concatenate(
        [jnp.dot(hi[:, l], ones_bd, preferred_element_type=F32) + jnp.dot(lo[:, l], ones_bd, preferred_element_type=F32)
         for l in (slice(0, HG_HALF), slice(HG_HALF, 2 * HG_HALF))], axis=1) * (1.0 / HG_VAL)
    hg = hg_ref[...]
    od = o * lax.rsqrt(ms + EPS) * go_ref[...] * (hg * jax.nn.sigmoid(hg))
    out = (jnp.dot(_ctx_or_lat(occ_ref, ocl_ref).astype(BF16), wa_s[...], preferred_element_type=F32)
           + jnp.dot(od.astype(BF16), wb_s[...], preferred_element_type=F32))
    o_ref[...] = x_ref[...] + gt_ref[0] * out


def _cd_out(x, oc, of, ob, hg, g_o, w_out, mod, layer):
    row = lambda n: pl.BlockSpec((ROW_BLK, n), lambda i: (i, 0))
    full = lambda a, b: pl.BlockSpec((a, b), lambda i: (0, 0))
    return pl.pallas_call(
        _cd_out_kernel,
        grid=(N_ROW_BLK,),
        in_specs=[row(D_MODEL)] + _split_specs(MLA_WIDTH) + _split_specs(HG_WIDTH) + _split_specs(HG_WIDTH)
        + [row(HG_WIDTH), full(1, HG_WIDTH), full(MLA_WIDTH + HG_WIDTH, D_MODEL), _mod_spec(layer, 2)],
        out_specs=row(D_MODEL),
        out_shape=jax.ShapeDtypeStruct((N_TOK, D_MODEL), F32),
        scratch_shapes=[pltpu.VMEM((MLA_WIDTH, D_MODEL), BF16), pltpu.VMEM((HG_WIDTH, D_MODEL), BF16)],
        compiler_params=_cparams("arbitrary"),
        name="cd_out",
    )(x, oc[0], oc[1], of[0], of[1], ob[0], ob[1], hg, jnp.tile(g_o, HG_HEADS).reshape(1, HG_WIDTH), w_out, mod)


N_PAIRS = 6
N_CLASSES = N_GROUPS * N_PAIRS
CLS_ROWS = 32
MOE_BM = 256
MOE_NBLK = N_TOK // MOE_BM + N_CLASSES
MOE_ROWS = MOE_NBLK * MOE_BM
PAIR_SLOTS = ((0, 1), (3, 1), (2, 1), (2, 0), (3, 0), (3, 2))


def _moe_route_kernel(x_ref, g_ref, sc_ref, sh_ref, rw_ref, rb_ref, h_ref, ti_ref, tw_ref, cnt_ref, base_s):
    @pl.when(pl.program_id(0) == 0)
    def _():
        base_s[...] = jnp.zeros_like(base_s)

    h = _norm_mod(x_ref[...], g_ref[...], sc_ref[0], sh_ref[0])
    logits = lax.dot_general(rw_ref[...], h, (((1,), (1,)), ((), ())), precision=HI,
                             preferred_element_type=F32)
    aff = jax.nn.sigmoid(logits)
    sel = aff + rb_ref[...]
    s = [sel[e:e + 1, :] for e in range(N_EXPERTS)]
    a = [aff[e:e + 1, :] for e in range(N_EXPERTS)]
    gs = []
    for g in range(N_GROUPS):
        m = s[4 * g:4 * g + 4]
        pairs = [m[i] + m[j] for i in range(4) for j in range(i + 1, 4)]
        gs.append(functools.reduce(jnp.maximum, pairs))
    gmax = functools.reduce(jnp.maximum, gs)
    taken = jnp.zeros_like(gmax) > 1.0
    gsel = []
    for g in range(N_GROUPS):
        hit = (gs[g] == gmax) & jnp.logical_not(taken)
        gsel.append(hit)
        taken = taken | hit
    e_lo = jnp.zeros(gmax.shape, jnp.int32)
    e_hi = jnp.zeros(gmax.shape, jnp.int32)
    a_lo = jnp.zeros_like(gmax)
    a_hi = jnp.zeros_like(gmax)
    nsel = jnp.zeros(gmax.shape, jnp.int32)
    for g in range(N_GROUPS):
        for i in range(4):
            e = 4 * g + i
            beat = jnp.zeros(gmax.shape, jnp.int32)
            for j in range(4):
                if j != i:
                    o = 4 * g + j
                    beat = beat + jnp.where((s[o] > s[e]) | ((s[o] == s[e]) & (j < i)), 1, 0)
            pick = gsel[g] & (beat < 2)
            is_first = pick & (nsel == 0)
            is_second = pick & (nsel == 1)
            e_lo = jnp.where(is_first, e, e_lo)
            a_lo = jnp.where(is_first, a[e], a_lo)
            e_hi = jnp.where(is_second, e, e_hi)
            a_hi = jnp.where(is_second, a[e], a_hi)
            nsel = nsel + jnp.where(pick, 1, 0)
    grp = e_lo // EXPERTS_PER_GROUP
    lo = e_lo - grp * EXPERTS_PER_GROUP
    hi = e_hi - grp * EXPERTS_PER_GROUP
    pair = ((lo * (7 - lo)) >> 1) + (hi - lo - 1)
    pair = jnp.where(pair == 1, 3, jnp.where(pair == 2, 4, jnp.where(pair == 3, 2, jnp.where(pair == 4, 1, pair))))
    cls = grp * N_PAIRS + pair
    wsum = a_lo + a_hi
    w_lo, w_hi = a_lo / wsum, a_hi / wsum
    onehot = (lax.broadcasted_iota(jnp.int32, (CLS_ROWS, ROW_BLK), 0) == cls).astype(F32)
    tt = lax.broadcasted_iota(jnp.int32, (ROW_BLK, ROW_BLK), 0) < lax.broadcasted_iota(jnp.int32, (ROW_BLK, ROW_BLK), 1)
    before = _dot(onehot, jnp.where(tt, 1.0, 0.0))
    base = base_s[...]
    rank = jnp.sum(onehot * (before + base[:, 0:1]), axis=0, keepdims=True).astype(jnp.int32)
    base = base + jnp.sum(onehot, axis=1, keepdims=True)
    base_s[...] = base
    cnt_ref[...] = base.astype(jnp.int32)
    ti_ref[0] = jnp.concatenate([cls, rank, e_lo, e_hi, jnp.zeros((4, ROW_BLK), jnp.int32)], axis=0)
    ident = lax.broadcasted_iota(jnp.int32, (ROW_BLK, ROW_BLK), 0) == lax.broadcasted_iota(jnp.int32, (ROW_BLK, ROW_BLK), 1)
    col = lambda r: jnp.sum(jnp.where(ident, r, 0.0), axis=1, keepdims=True)
    tw_ref[...] = jnp.where(lax.broadcasted_iota(jnp.int32, (ROW_BLK, 128), 1) < 64, col(w_lo), col(w_hi))
    h_ref[...] = h


def _moe_route(x, mod, g, router_w, router_b, layer):
    row = lambda n: pl.BlockSpec((ROW_BLK, n), lambda i: (i, 0))
    full = lambda a, b: pl.BlockSpec((a, b), lambda i: (0, 0))
    return pl.pallas_call(
        _moe_route_kernel,
        grid=(N_ROW_BLK,),
        in_specs=[row(D_MODEL), full(1, D_MODEL), _mod_spec(layer, 4), _mod_spec(layer, 3),
                  full(N_EXPERTS, D_MODEL), full(N_EXPERTS, 1)],
        out_specs=[row(D_MODEL), pl.BlockSpec((1, 8, ROW_BLK), lambda i: (i, 0, 0)), row(128), full(CLS_ROWS, 128)],
        out_shape=[jax.ShapeDtypeStruct((N_TOK, D_MODEL), F32),
                   jax.ShapeDtypeStruct((N_ROW_BLK, 8, ROW_BLK), jnp.int32),
                   jax.ShapeDtypeStruct((N_TOK, 128), F32),
                   jax.ShapeDtypeStruct((CLS_ROWS, 128), jnp.int32)],
        scratch_shapes=[pltpu.VMEM((CLS_ROWS, 128), F32)],
        compiler_params=_cparams("arbitrary"),
        name="moe_route",
    )(x, g.reshape(1, D_MODEL), mod, mod, router_w.T, router_b.reshape(N_EXPERTS, 1))


def _moe_sort_kernel(pos_ref, nblk_ref, h_ref, wt_ref, hs_ref, ws_ref, perm_s, stage_s):
    j = pl.program_id(0)

    @pl.when(j == 0)
    def _():
        def init(r, c):
            perm_s[r] = N_TOK - 1
            return c

        lax.fori_loop(0, MOE_ROWS, init, 0, unroll=16)

        def build(t, c):
            perm_s[pos_ref[t]] = t
            return c

        lax.fori_loop(0, N_TOK, build, 0, unroll=16)

    @pl.when(j < nblk_ref[0])
    def _():
        base = j * MOE_BM
        for r in range(MOE_BM):
            src = perm_s[base + r]
            stage_s[r:r + 1, :] = h_ref[pl.ds(src, 1), :]
            ws_ref[r:r + 1, :] = wt_ref[pl.ds(src, 1), :]
        hs_ref[...] = stage_s[...].astype(BF16)

    @pl.when(j >= nblk_ref[0])
    def _():
        hs_ref[...] = jnp.zeros_like(hs_ref)
        ws_ref[...] = jnp.zeros_like(ws_ref)


def _moe_sort(h, wtok, pos, nblk):
    res = lambda w: pl.BlockSpec((N_TOK, w), lambda j, *_: (0, 0), pipeline_mode=pl.Buffered(1))
    grid_spec = pltpu.PrefetchScalarGridSpec(
        num_scalar_prefetch=2,
        grid=(MOE_NBLK,),
        in_specs=[res(D_MODEL), res(128)],
        out_specs=[pl.BlockSpec((MOE_BM, D_MODEL), lambda j, *_: (j, 0)),
                   pl.BlockSpec((MOE_BM, 128), lambda j, *_: (j, 0))],
        scratch_shapes=[pltpu.SMEM((MOE_ROWS,), jnp.int32), pltpu.VMEM((MOE_BM, D_MODEL), F32)],
    )
    return pl.pallas_call(
        _moe_sort_kernel,
        grid_spec=grid_spec,
        out_shape=[jax.ShapeDtypeStruct((MOE_ROWS, D_MODEL), BF16), jax.ShapeDtypeStruct((MOE_ROWS, 128), F32)],
        compiler_params=_cparams("arbitrary"),
        name="moe_sort",
    )(pos, nblk, h, wtok)


def _moe_experts_kernel(elo_ref, ehi_ref, nblk_ref, swap_ref, hs_ref, ws_ref, wgl_ref, wul_ref, wdl_ref,
                        wgh_ref, wuh_ref, wdh_ref, y_ref):
    del elo_ref, ehi_ref
    j = pl.program_id(0)

    @pl.when(j < nblk_ref[0])
    def _():
        h = hs_ref[...].astype(BF16)
        swap = swap_ref[j] == 1
        gate_a = jnp.where(swap, ws_ref[:, 64:65], ws_ref[:, 0:1])
        gate_b = jnp.where(swap, ws_ref[:, 0:1], ws_ref[:, 64:65])
        acc = None
        for wg, wu, wd, gate in ((wgl_ref, wul_ref, wdl_ref, gate_a), (wgh_ref, wuh_ref, wdh_ref, gate_b)):
            g = _dot(h, wg[0, 0])
            u = _dot(h, wu[0, 0])
            hid = g * jax.nn.sigmoid(g) * u * gate
            y = _dot(hid, wd[0, 0])
            acc = y if acc is None else acc + y
        y_ref[...] = acc

    @pl.when(j >= nblk_ref[0])
    def _():
        y_ref[...] = jnp.zeros_like(y_ref)


def _moe_experts(hs, ws, blk_elo, blk_ehi, nblk, swap, w_gate, w_up, w_down, layer):
    last = lambda j, nb: jnp.minimum(j, nb[0] - 1)
    wspec = lambda a, b, which: pl.BlockSpec(
        (1, 1, a, b), lambda j, elo, ehi, nb, sw: (layer, (elo, ehi)[which][last(j, nb)], 0, 0))
    grid_spec = pltpu.PrefetchScalarGridSpec(
        num_scalar_prefetch=4,
        grid=(MOE_NBLK,),
        in_specs=[pl.BlockSpec((MOE_BM, D_MODEL), lambda j, elo, ehi, nb, sw: (last(j, nb), 0)),
                  pl.BlockSpec((MOE_BM, 128), lambda j, elo, ehi, nb, sw: (last(j, nb), 0)),
                  wspec(D_MODEL, D_FF, 0), wspec(D_MODEL, D_FF, 0), wspec(D_FF, D_MODEL, 0),
                  wspec(D_MODEL, D_FF, 1), wspec(D_MODEL, D_FF, 1), wspec(D_FF, D_MODEL, 1)],
        out_specs=pl.BlockSpec((MOE_BM, D_MODEL), lambda j, *_: (j, 0)),
    )
    return pl.pallas_call(
        _moe_experts_kernel,
        grid_spec=grid_spec,
        out_shape=jax.ShapeDtypeStruct((MOE_ROWS, D_MODEL), F32),
        compiler_params=_cparams("arbitrary"),
        name="moe_experts",
    )(blk_elo, blk_ehi, nblk, swap, hs, ws, w_gate, w_up, w_down, w_gate, w_up, w_down)


SC_ROWS = 64


def _sc_gather_rows(table, idx):
    info = plsc.get_sparse_core_info()
    nc, nw = info.num_cores, info.num_cores * info.num_subcores
    b, d = idx.shape[0], table.shape[1]
    per_w = b // nw
    assert per_w * nw == b and per_w % SC_ROWS == 0
    mesh = plsc.VectorSubcoreMesh(core_axis_name="c", subcore_axis_name="s")

    @functools.partial(
        pl.kernel, mesh=mesh, out_type=jax.ShapeDtypeStruct((b, d), table.dtype),
        scratch_types=[pltpu.VMEM((SC_ROWS,), jnp.int32), pltpu.VMEM((SC_ROWS, d), table.dtype),
                       pltpu.SemaphoreType.DMA])
    def gather(table_hbm, idx_hbm, out_hbm, idx_v, rows_v, sem):
        wid = lax.axis_index("s") * nc + lax.axis_index("c")
        for c in range(per_w // SC_ROWS):
            base = wid * per_w + c * SC_ROWS
            pltpu.sync_copy(idx_hbm.at[pl.ds(base, SC_ROWS)], idx_v)
            pltpu.async_copy(table_hbm.at[idx_v], rows_v, sem).wait()
            pltpu.sync_copy(rows_v, out_hbm.at[pl.ds(base, SC_ROWS)])

    return gather(table, idx)


def _moe_residual_kernel(x_ref, y_ref, gt_ref, o_ref):
    o_ref[...] = x_ref[...] + gt_ref[0] * y_ref[...]


def _moe_residual(x, y_tok, mod, layer):
    row = pl.BlockSpec((ROW_BLK, D_MODEL), lambda i: (i, 0))
    return pl.pallas_call(
        _moe_residual_kernel,
        grid=(N_ROW_BLK,),
        in_specs=[row, row, _mod_spec(layer, 5)],
        out_specs=row,
        out_shape=jax.ShapeDtypeStruct((N_TOK, D_MODEL), F32),
        compiler_params=_cparams("arbitrary"),
        name="moe_residual",
    )(x, y_tok, mod)


def _moe(x, mod, g, router_w, router_b, w_gate, w_up, w_down, layer):
    h, info, wtok, counts = _moe_route(x, mod, g, router_w, router_b, layer)
    cls = info[:, 0, :].reshape(N_TOK)
    rank = info[:, 1, :].reshape(N_TOK)
    cnt = counts[:N_CLASSES, 0]
    nb = (cnt + MOE_BM - 1) // MOE_BM
    ends = jnp.cumsum(nb)
    starts = ends - nb
    pos = ((starts * MOE_BM)[cls] + rank).astype(jnp.int32)
    blk = jnp.arange(MOE_NBLK, dtype=jnp.int32)
    blk_cls = jnp.minimum(jnp.sum((blk[:, None] >= ends[None, :]).astype(jnp.int32), axis=1), N_CLASSES - 1)
    slot_a = jnp.asarray([s[0] for s in PAIR_SLOTS], jnp.int32)
    slot_b = jnp.asarray([s[1] for s in PAIR_SLOTS], jnp.int32)
    grp = blk_cls // N_PAIRS
    blk_ea = (grp * EXPERTS_PER_GROUP + slot_a[blk_cls % N_PAIRS]).astype(jnp.int32)
    blk_eb = (grp * EXPERTS_PER_GROUP + slot_b[blk_cls % N_PAIRS]).astype(jnp.int32)
    swap = (blk_ea > blk_eb).astype(jnp.int32)
    nblk = ends[-1:].astype(jnp.int32)
    hs, ws = _moe_sort(h, wtok, pos, nblk)
    y_sorted = _moe_experts(hs, ws, blk_ea, blk_eb, nblk, swap, w_gate, w_up, w_down, layer)
    return _sc_gather_rows(y_sorted, pos)


def _final_norm_kernel(x_ref, y_ref, gt_ref, g_ref, o_ref):
    o_ref[...] = _rms(x_ref[...] + gt_ref[0] * y_ref[...]) * g_ref[...]


def _final_norm(x, y_tok, mod, layer, g, row0, rows):
    base = row0 // ROW_BLK
    row = pl.BlockSpec((ROW_BLK, D_MODEL), lambda i: (base + i, 0))
    gate = pl.BlockSpec((1, 1, D_MODEL), lambda i: ((layer * MOD_ROWS + _mod_group(base + i)) * N_MOD + 5, 0, 0))
    return pl.pallas_call(
        _final_norm_kernel,
        grid=(rows // ROW_BLK,),
        in_specs=[row, row, gate, pl.BlockSpec((1, D_MODEL), lambda i: (0, 0))],
        out_specs=pl.BlockSpec((ROW_BLK, D_MODEL), lambda i: (i, 0)),
        out_shape=jax.ShapeDtypeStruct((rows, D_MODEL), F32),
        compiler_params=_cparams("arbitrary"),
        name="final_norm",
    )(x, y_tok, mod, g.reshape(1, D_MODEL))


def kernel(x_prompt, x_sample, cache_attn_k, cache_attn_v, state_ssm_re, state_ssm_im, cache_mla_ckv, cache_mla_kpe,
           state_hgrn, c, c_ctx, w_mod, b_mod, g_mix, g_ffn, g_final, router_w, router_b, moe_w_gate, moe_w_up,
           moe_w_down, ab_w_in, ab_sink, s5_lam_re, s5_lam_im, s5_log_dt, s5_b_re, s5_b_im, s5_c_re, s5_c_im, s5_d,
           s5_w_glu, ab_w_out, cd_w_in, mla_g_q, mla_w_q_up, mla_g_kv, mla_w_kv_up, hg_lower_bounds, hg_g_o, cd_w_out):
    x = (x_prompt.reshape(N_CTX_TOK, D_MODEL), x_sample.reshape(N_LAT_TOK, D_MODEL))
    y_tok = None
    cond = jnp.zeros((MOD_ROWS, D_MODEL), F32).at[0].set(c_ctx).at[1:1 + DEC_BATCH].set(c)
    mod = _modulation(cond, w_mod, b_mod)
    keep = ([], [], [], [], [], [], [])
    for l in range(DEPTH):
        j = l // 2
        if l % 2 == 0:
            if y_tok is not None:
                x = _moe_residual(x, y_tok, mod, l - 1)
                x = (x[:N_CTX_TOK], x[N_CTX_TOK:])
            q, k, v, u = _ab_in(x, mod, g_mix[l], ab_w_in[j], l)
            o_a = _attention_a(q, k, v, cache_attn_k[:, j].reshape(DEC_BATCH, PAST_LEN, A_KV_WIDTH),
                               cache_attn_v[:, j].reshape(DEC_BATCH, PAST_LEN, A_KV_WIDTH), ab_sink[j])
            y, fin_re, fin_im = _s5_scan(u, state_ssm_re[:, j], state_ssm_im[:, j], s5_lam_re[j], s5_lam_im[j],
                                         s5_log_dt[j], s5_b_re[j], s5_b_im[j], s5_c_re[j], s5_c_im[j])
            x = _ab_out(x, o_a, y, u, s5_d[j], s5_w_glu[j], ab_w_out[j], mod, l)
            keep[0].append(k[:N_CTX_TOK].reshape(BATCH, SEQ, A_KV_HEADS, HEAD_DIM))
            keep[1].append(v[:N_CTX_TOK].reshape(BATCH, SEQ, A_KV_HEADS, HEAD_DIM))
            keep[2].append(fin_re)
            keep[3].append(fin_im)
        else:
            x, qf, ckv, kpe, kf, vm, hq, hff, hfb, hi, hg = _cd_in(
                x, y_tok, mod, g_mix[l], cd_w_in[j], mla_g_q[j], mla_w_q_up[j], mla_g_kv[j], mla_w_kv_up[j], l)
            kcf, vc = _mla_cache_kv(cache_mla_ckv[:, j].reshape(DEC_BATCH * PAST_LEN, MLA_KV_RANK),
                                    cache_mla_kpe[:, j].reshape(DEC_BATCH * PAST_LEN, MLA_ROPE), mla_w_kv_up[j])
            o_c = _attention_mla(qf, kf, vm, kcf, vc)
            s0_ctx = jnp.zeros((BATCH, 2, 2, HG_VAL, HG_HALF), F32)
            of_c, ob_c, s_fin = _hgrn_scan(hq, hff, hfb, hi, hg_lower_bounds, s0_ctx, l, 0, BATCH, SEQ)
            of_l, ob_l, _ = _hgrn_scan(hq, hff, hfb, hi, hg_lower_bounds, _hg_state_to_blocks(state_hgrn[:, j]), l,
                                       N_CTX_TOK, DEC_BATCH, DEC_SEQ)
            x = _cd_out(x, o_c, (of_c, of_l), (ob_c, ob_l), hg, hg_g_o[j], cd_w_out[j], mod, l)
            keep[4].append(ckv[:N_CTX_TOK].reshape(BATCH, SEQ, MLA_KV_RANK))
            keep[5].append(kpe[:N_CTX_TOK, :MLA_ROPE].reshape(BATCH, SEQ, MLA_ROPE))
            keep[6].append(_hg_state_from_blocks(s_fin))
        y_tok = _moe(x, mod, g_ffn[l], router_w, router_b, moe_w_gate, moe_w_up, moe_w_down, l)
    y_ctx = _final_norm(x, y_tok, mod, DEPTH - 1, g_final, 0, N_CTX_TOK)
    y_lat = _final_norm(x, y_tok, mod, DEPTH - 1, g_final, N_CTX_TOK, N_LAT_TOK)
    return (y_ctx.reshape(BATCH, SEQ, D_MODEL), y_lat.reshape(DEC_BATCH, DEC_SEQ, D_MODEL),
            jnp.stack(keep[0], 1), jnp.stack(keep[1], 1), jnp.stack(keep[2], 1), jnp.stack(keep[3], 1),
            jnp.stack(keep[4], 1), jnp.stack(keep[5], 1), jnp.stack(keep[6], 1))
```

```python
import functools
import math

import numpy as np
import jax
import jax.numpy as jnp
from jax import lax
from jax.experimental import pallas as pl
from jax.experimental.pallas import tpu as pltpu
from jax.experimental.pallas import tpu_sc as plsc

F32 = jnp.float32
BF16 = jnp.bfloat16

D_MODEL = 1024
BATCH = 16
SEQ = 256
DEPTH = 2
DEC_BATCH = 4
DEC_SEQ = 1024
PAST_LEN = 512
GRID_W = 64
N_MOD = 6
EPS = 1e-6
NEG_INF = -1e30
ROPE_BASE = 10000.0
HEAD_DIM = 64
A_HEADS = 8
A_KV_HEADS = 2
A_WINDOW = 128
A_WIDTH = A_HEADS * HEAD_DIM
A_KV_WIDTH = A_KV_HEADS * HEAD_DIM
S5_WIDTH = D_MODEL // 2
S5_GROUP = 16
S5_GROUPS = S5_WIDTH // S5_GROUP
S5_STATE = 64
MLA_HEADS = 8
MLA_Q_RANK = D_MODEL // 4
MLA_KV_RANK = D_MODEL // 8
MLA_NOPE = 64
MLA_ROPE = 32
MLA_V = 64
MLA_WIDTH = MLA_HEADS * MLA_V
HG_HEADS = 8
HG_KEY = 64
HG_VAL = 64
HG_KD = HG_HEADS * HG_KEY
HG_WIDTH = HG_HEADS * HG_VAL
HG_CHUNK = 16
N_EXPERTS = 16
N_GROUPS = 4
EXPERTS_PER_GROUP = N_EXPERTS // N_GROUPS
D_FF = D_MODEL // 2

N_CTX_TOK = BATCH * SEQ
N_LAT_TOK = DEC_BATCH * DEC_SEQ
N_TOK = N_CTX_TOK + N_LAT_TOK
ROW_BLK = 512
MLA_QBLK = 256
N_ROW_BLK = N_TOK // ROW_BLK
N_CTX_BLK = N_CTX_TOK // ROW_BLK
LAT_BLK_PER_SEQ = DEC_SEQ // ROW_BLK
MOD_ROWS = 8
VMEM_LIMIT = 56 * 1024 * 1024


def _cparams(*sem):
    return pltpu.CompilerParams(dimension_semantics=sem, vmem_limit_bytes=VMEM_LIMIT)


def _mod_group(i):
    return jnp.where(i < N_CTX_BLK, 0, 1 + (i - N_CTX_BLK) // LAT_BLK_PER_SEQ)


def _mod_spec(layer, which):
    return pl.BlockSpec((1, 1, D_MODEL), lambda i: ((layer * MOD_ROWS + _mod_group(i)) * N_MOD + which, 0, 0))


def _rope_blk(i):
    return jnp.where(i < N_CTX_BLK, 0, 1 + (i - N_CTX_BLK) % LAT_BLK_PER_SEQ)


def _rope_tables(rot_dim, only=None):
    n_freq = rot_dim // 4
    t = np.arange(DEC_SEQ)
    rows = (t // GRID_W).astype(np.float32)
    cols = (t % GRID_W).astype(np.float32)
    inv = (np.float32(ROPE_BASE) ** (-np.arange(n_freq, dtype=np.float32) / np.float32(n_freq))).astype(np.float32)
    ang_r = rows[:, None] * inv[None, :]
    ang_c = cols[:, None] * inv[None, :]
    ang = np.concatenate([ang_r, ang_r, ang_c, ang_c], axis=-1).astype(np.float32)
    reps = 128 // rot_dim
    cos = np.tile(np.cos(ang), (1, reps)).astype(np.float32)
    sin = np.tile(np.sin(ang), (1, reps)).astype(np.float32)
    lane = np.arange(128)
    first = (lane % (2 * n_freq)) < n_freq
    sin_a = np.where(first[None, :], -sin, 0.0).astype(np.float32)
    sin_b = np.where(first[None, :], 0.0, sin).astype(np.float32)
    if only is not None:
        keep = ((lane >= only[0]) & (lane < only[1]))[None, :]
        cos, sin_a, sin_b = np.where(keep, cos, 1.0), np.where(keep, sin_a, 0.0), np.where(keep, sin_b, 0.0)
        cos, sin_a, sin_b = cos.astype(np.float32), sin_a.astype(np.float32), sin_b.astype(np.float32)
    ident = np.zeros((ROW_BLK, 128), np.float32)
    cos = np.concatenate([ident + 1.0, cos], axis=0)
    sin_a = np.concatenate([ident, sin_a], axis=0)
    sin_b = np.concatenate([ident, sin_b], axis=0)
    return jnp.asarray(cos), jnp.asarray(sin_a), jnp.asarray(sin_b)


def _apply_rope(x, cos, sin_a, sin_b, quarter):
    outs = []
    for j in range(x.shape[1] // 128):
        xt = x[:, 128 * j:128 * (j + 1)]
        up = pltpu.roll(xt, 128 - quarter, axis=1)
        dn = pltpu.roll(xt, quarter, axis=1)
        outs.append(xt * cos + up * sin_a + dn * sin_b)
    return outs[0] if len(outs) == 1 else jnp.concatenate(outs, axis=1)


def _rms(x):
    return x * lax.rsqrt(jnp.mean(x * x, axis=-1, keepdims=True) + EPS)


def _norm_mod(x, g, sc, sh):
    return _rms(x) * g * (1.0 + sc) + sh


def _dot(a, b):
    return jnp.dot(a.astype(BF16), b.astype(BF16), preferred_element_type=F32)


def _dot_nt(a, b):
    return lax.dot_general(a.astype(BF16), b.astype(BF16), (((1,), (1,)), ((), ())), preferred_element_type=F32)


def _mod_kernel(cond_ref, w_ref, b_ref, o_ref):
    c = cond_ref[...]
    s = c * jax.nn.sigmoid(c)
    o_ref[0] = _dot(s, w_ref[0]) + b_ref[0]


def _modulation(cond, w_mod, b_mod):
    nb = 1024
    out = pl.pallas_call(
        _mod_kernel,
        grid=(DEPTH, N_MOD * D_MODEL // nb),
        in_specs=[pl.BlockSpec((MOD_ROWS, D_MODEL), lambda l, n: (0, 0)),
                  pl.BlockSpec((1, D_MODEL, nb), lambda l, n: (l, 0, n)),
                  pl.BlockSpec((1, 1, nb), lambda l, n: (l, 0, n))],
        out_specs=pl.BlockSpec((1, MOD_ROWS, nb), lambda l, n: (l, 0, n)),
        out_shape=jax.ShapeDtypeStruct((DEPTH, MOD_ROWS, N_MOD * D_MODEL), F32),
        compiler_params=_cparams("arbitrary", "arbitrary"),
        name="modulation",
    )(cond, w_mod, b_mod.reshape(DEPTH, 1, N_MOD * D_MODEL))
    return out.reshape(DEPTH * MOD_ROWS * N_MOD, 1, D_MODEL)


def _ab_in_kernel(xc_ref, xl_ref, g_ref, sc_ref, sh_ref, w_ref, cos_ref, sa_ref, sb_ref,
                  q_ref, k_ref, v_ref, u_ref, wq_s, wk_s, wv_s, wu_s):
    @pl.when(pl.program_id(0) == 0)
    def _():
        wq_s[...] = w_ref[:, 0:A_WIDTH].astype(BF16)
        wk_s[...] = w_ref[:, A_WIDTH:A_WIDTH + A_KV_WIDTH].astype(BF16)
        wv_s[...] = w_ref[:, A_WIDTH + A_KV_WIDTH:A_WIDTH + 2 * A_KV_WIDTH].astype(BF16)
        wu_s[...] = w_ref[:, A_WIDTH + 2 * A_KV_WIDTH:].astype(BF16)

    h = _norm_mod(_ctx_or_lat(xc_ref, xl_ref), g_ref[...], sc_ref[0], sh_ref[0]).astype(BF16)
    cos, sa, sb = cos_ref[...], sa_ref[...], sb_ref[...]
    q = jnp.dot(h, wq_s[...], preferred_element_type=F32)
    q_ref[...] = _apply_rope(q, cos, sa, sb, HEAD_DIM // 4).astype(q_ref.dtype)
    k = jnp.dot(h, wk_s[...], preferred_element_type=F32)
    k_ref[...] = _apply_rope(k, cos, sa, sb, HEAD_DIM // 4)
    v_ref[...] = jnp.dot(h, wv_s[...], preferred_element_type=F32)
    u_ref[...] = jnp.dot(h, wu_s[...], preferred_element_type=F32)


def _ab_in(x, mod, g, w, layer):
    cos, sa, sb = _rope_tables(HEAD_DIM)
    d_in = w.shape[1]
    row = lambda n: pl.BlockSpec((ROW_BLK, n), lambda i: (i, 0))
    rope = pl.BlockSpec((ROW_BLK, 128), lambda i: (_rope_blk(i), 0))
    return pl.pallas_call(
        _ab_in_kernel,
        grid=(N_ROW_BLK,),
        in_specs=_split_specs(D_MODEL) + [pl.BlockSpec((1, D_MODEL), lambda i: (0, 0)),
                  _mod_spec(layer, 1), _mod_spec(layer, 0),
                  pl.BlockSpec((D_MODEL, d_in), lambda i: (0, 0), pipeline_mode=pl.Buffered(1)), rope, rope, rope],
        out_specs=[row(A_WIDTH), row(A_KV_WIDTH), row(A_KV_WIDTH), row(S5_WIDTH)],
        out_shape=[jax.ShapeDtypeStruct((N_TOK, A_WIDTH), BF16), jax.ShapeDtypeStruct((N_TOK, A_KV_WIDTH), F32),
                   jax.ShapeDtypeStruct((N_TOK, A_KV_WIDTH), F32), jax.ShapeDtypeStruct((N_TOK, S5_WIDTH), F32)],
        scratch_shapes=[pltpu.VMEM((D_MODEL, A_WIDTH), BF16), pltpu.VMEM((D_MODEL, A_KV_WIDTH), BF16),
                        pltpu.VMEM((D_MODEL, A_KV_WIDTH), BF16), pltpu.VMEM((D_MODEL, S5_WIDTH), BF16)],
        compiler_params=_cparams("arbitrary"),
        name="ab_in",
    )(x[0], x[1], g.reshape(1, D_MODEL), mod, mod, w, cos, sa, sb)


def _softmax_pv(s_list, v_list, sink):
    m = functools.reduce(jnp.maximum, [jnp.max(s, axis=-1, keepdims=True) for s in s_list])
    if sink is not None:
        m = jnp.maximum(m, sink)
    ps = [jnp.exp(s - m) for s in s_list]
    l = functools.reduce(jnp.add, [jnp.sum(p, axis=-1, keepdims=True) for p in ps])
    if sink is not None:
        l = l + jnp.exp(sink - m)
    o = functools.reduce(jnp.add, [_dot(p, v) for p, v in zip(ps, v_list)])
    return o / l


def _gqa_with_sink(sink_ref, q, k_of, v_of, bias, o_ref):
    g = A_HEADS // A_KV_HEADS
    r = q.shape[0]
    outs = [None] * A_HEADS
    for kh in range(A_KV_HEADS):
        heads = range(g * kh, g * (kh + 1))
        qg = jnp.concatenate([q[:, HEAD_DIM * h:HEAD_DIM * (h + 1)] for h in heads], axis=0)
        sink = jnp.concatenate([jnp.full((r, 1), sink_ref[h], F32) for h in heads], axis=0)
        s = _dot_nt(qg, k_of(kh))
        if bias is not None:
            s = s + jnp.concatenate([bias] * g, axis=0)
        m = jnp.maximum(jnp.max(s, axis=-1, keepdims=True), sink)
        p = jnp.exp(s - m)
        l = jnp.sum(p, axis=-1, keepdims=True) + jnp.exp(sink - m)
        o = _dot(p, v_of(kh)) / l
        for n, h in enumerate(heads):
            outs[h] = o[r * n:r * (n + 1)]
    o_ref[...] = jnp.concatenate(outs, axis=1).astype(o_ref.dtype)


def _attn_ctx_kernel(sink_ref, q_ref, k_ref, v_ref, o_ref):
    scale = HEAD_DIM ** -0.5
    g = A_HEADS // A_KV_HEADS
    outs = []
    for h in range(A_HEADS):
        kh = h // g
        q = q_ref[:, HEAD_DIM * h:HEAD_DIM * (h + 1)]
        k = k_ref[:, HEAD_DIM * kh:HEAD_DIM * (kh + 1)]
        v = v_ref[:, HEAD_DIM * kh:HEAD_DIM * (kh + 1)]
        s = _dot_nt(q, k) * scale
        outs.append(_softmax_pv([s], [v], sink_ref[h]))
    o_ref[...] = jnp.concatenate(outs, axis=1).astype(o_ref.dtype)


def _attn_lat_kernel(sink_ref, q_ref, kp_ref, kc_ref, kn_ref, vp_ref, vc_ref, vn_ref, kx_ref, vx_ref, o_ref):
    n = pl.program_id(1)
    nb = DEC_SEQ // A_WINDOW
    i = lax.broadcasted_iota(jnp.int32, (A_WINDOW, A_WINDOW), 0)
    j = lax.broadcasted_iota(jnp.int32, (A_WINDOW, A_WINDOW), 1)
    zero = jnp.zeros((A_WINDOW, A_WINDOW), F32)
    bias = jnp.concatenate([jnp.where((j >= i) & (n > 0), 0.0, NEG_INF), zero,
                            jnp.where((j <= i) & (n < nb - 1), 0.0, NEG_INF),
                            jnp.zeros((A_WINDOW, PAST_LEN), F32)], axis=1)

    def rows(p_ref, c_ref, n_ref, x_ref):
        def of(kh):
            sl = slice(HEAD_DIM * kh, HEAD_DIM * (kh + 1))
            return jnp.concatenate([p_ref[:, sl], c_ref[:, sl], n_ref[:, sl], x_ref[0, :, sl]], axis=0)
        return of

    _gqa_with_sink(sink_ref, q_ref[...] * HEAD_DIM ** -0.5, rows(kp_ref, kc_ref, kn_ref, kx_ref),
                   rows(vp_ref, vc_ref, vn_ref, vx_ref), bias, o_ref)


def _attention_a(q, k, v, cache_k, cache_v, sink):
    smem = pl.BlockSpec(memory_space=pltpu.SMEM)
    o = pl.pallas_call(
        _attn_ctx_kernel,
        grid=(BATCH,),
        in_specs=[smem, pl.BlockSpec((SEQ, A_WIDTH), lambda b: (b, 0)),
                  pl.BlockSpec((SEQ, A_KV_WIDTH), lambda b: (b, 0)), pl.BlockSpec((SEQ, A_KV_WIDTH), lambda b: (b, 0))],
        out_specs=pl.BlockSpec((SEQ, A_WIDTH), lambda b: (b, 0)),
        out_shape=jax.ShapeDtypeStruct((N_CTX_TOK, A_WIDTH), BF16),
        compiler_params=_cparams("arbitrary"),
        name="attn_a_ctx",
    )(sink, q, k, v)
    nb = DEC_SEQ // A_WINDOW
    base = N_CTX_TOK // A_WINDOW
    cur = lambda b, n: (base + b * nb + n, 0)
    prev = lambda b, n: (base + b * nb + jnp.maximum(n - 1, 0), 0)
    nxt = lambda b, n: (base + b * nb + jnp.minimum(n + 1, nb - 1), 0)
    kv = lambda f: pl.BlockSpec((A_WINDOW, A_KV_WIDTH), f)
    cache = pl.BlockSpec((1, PAST_LEN, A_KV_WIDTH), lambda b, n: (b, 0, 0))
    o_lat = pl.pallas_call(
        _attn_lat_kernel,
        grid=(DEC_BATCH, nb),
        in_specs=[smem, pl.BlockSpec((A_WINDOW, A_WIDTH), cur), kv(prev), kv(cur), kv(nxt), kv(prev), kv(cur), kv(nxt),
                  cache, cache],
        out_specs=pl.BlockSpec((A_WINDOW, A_WIDTH), lambda b, n: (b * nb + n, 0)),
        out_shape=jax.ShapeDtypeStruct((N_LAT_TOK, A_WIDTH), BF16),
        compiler_params=_cparams("arbitrary", "arbitrary"),
        name="attn_a_lat",
    )(sink, q, k, k, k, v, v, v, cache_k, cache_v)
    return o, o_lat


S5_CHUNK = 16
S5_OCT = 128 // S5_GROUP
S5_NOCT = S5_GROUPS // S5_OCT
S5_K = S5_CHUNK * 128
S5_PART = S5_OCT * S5_STATE
S5_SW = 4 * S5_PART
S5_ROWS_CTX = BATCH * SEQ // S5_CHUNK
S5_ROWS_LAT = DEC_BATCH * DEC_SEQ // S5_CHUNK
S5_ROWS = S5_ROWS_CTX + S5_ROWS_LAT
S5_NB = 4
HI = lax.Precision.HIGHEST


def _s5_disc_kernel(lr_ref, li_ref, ldt_ref, ar_ref, ai_ref, zr_ref, zi_ref):
    lr, li = lr_ref[...], li_ref[...]
    dt = jnp.exp(ldt_ref[...])
    mag = jnp.exp(lr * dt)
    ar, ai = mag * jnp.cos(li * dt), mag * jnp.sin(li * dt)
    den = lr * lr + li * li
    ar_ref[...] = ar
    ai_ref[...] = ai
    zr_ref[...] = ((ar - 1.0) * lr + ai * li) / den
    zi_ref[...] = (ai * lr - (ar - 1.0) * li) / den


def _cmul(xr, xi, yr, yi):
    return xr * yr - xi * yi, xr * yi + xi * yr


def _dot_nt_hi(a, b):
    return lax.dot_general(a, b, (((1,), (1,)), ((), ())), precision=HI, preferred_element_type=F32)


def _s5_prep_kernel(ar_ref, ai_ref, zr_ref, zi_ref, btr_ref, bti_ref, ctr_ref, cti_ref,
                    m_ref, win_ref, wout_ref, a16_ref, pw_s, w_s, k_s):
    t = pl.program_id(1)
    npw = S5_CHUNK + 1
    blk = lambda j: pl.ds(pl.multiple_of(j * 128, 128), 128)

    @pl.when(t == 0)
    def _():
        kd = []
        for d in range(2):
            ar, ai = ar_ref[0, d], ai_ref[0, d]
            pr, pi = jnp.ones_like(ar), jnp.zeros_like(ar)
            bbr, bbi = _cmul(zr_ref[0, d], zi_ref[0, d], btr_ref[0, d], bti_ref[0, d])
            for j in range(npw):
                pw_s[d, 0, j:j + 1, :] = pr
                pw_s[d, 1, j:j + 1, :] = pi
                if j < S5_CHUNK:
                    wr, wi = _cmul(pr, pi, bbr, bbi)
                    w_s[d, 0, 128 * j:128 * (j + 1), :] = wr.astype(BF16)
                    w_s[d, 1, 128 * j:128 * (j + 1), :] = wi.astype(BF16)
                pr, pi = _cmul(pr, pi, ar, ai)
            kd.append(_dot_nt(w_s[d, 0], ctr_ref[0, d]) - _dot_nt(w_s[d, 1], cti_ref[0, d]))
        for jj in range(2 * S5_CHUNK - 1):
            j = jj - (S5_CHUNK - 1)
            if j > 0:
                k = kd[0][128 * j:128 * (j + 1)]
            elif j < 0:
                k = kd[1][128 * -j:128 * (1 - j)]
            else:
                k = kd[0][0:128] + kd[1][0:128]
            k_s[128 * jj:128 * (jj + 1), :] = k
        a16_ref[0] = jnp.concatenate([pw_s[0, 0, S5_CHUNK:npw, :], pw_s[1, 0, S5_CHUNK:npw, :],
                                      pw_s[0, 1, S5_CHUNK:npw, :], pw_s[1, 1, S5_CHUNK:npw, :]], axis=1)

    for tp in range(S5_CHUNK):
        m_ref[0, :, 128 * tp:128 * (tp + 1)] = k_s[blk(S5_CHUNK - 1 + tp - t), :].astype(BF16)

    def power(d, e):
        return pw_s[d, 0, pl.ds(e, 1), :], pw_s[d, 1, pl.ds(e, 1), :]

    for d in range(2):
        j = (S5_CHUNK - 1 - t) if d == 0 else t
        win_ref[0, :, S5_PART * d:S5_PART * (d + 1)] = w_s[d, 0, blk(j), :]
        win_ref[0, :, S5_PART * (2 + d):S5_PART * (3 + d)] = w_s[d, 1, blk(j), :]
        er, ei = _cmul(*power(d, (t + 1) if d == 0 else (S5_CHUNK - t)), ctr_ref[0, d], cti_ref[0, d])
        wout_ref[0, :, S5_PART * d:S5_PART * (d + 1)] = er.astype(BF16)
        wout_ref[0, :, S5_PART * (2 + d):S5_PART * (3 + d)] = (-ei).astype(BF16)


def _s5_main_kernel(u_ref, win_ref, m_ref, wout_ref, a16_ref, h0_ref, y_ref, hfin_ref, uo_s, x_s, hs_s):
    s = pl.program_id(1)
    nq = S5_PART // 128

    @pl.when(s == 0)
    def _():
        for t in range(S5_CHUNK):
            uo_s[:, 128 * t:128 * (t + 1)] = u_ref[pl.ds(t, S5_ROWS, stride=S5_CHUNK), :].astype(BF16)

    @pl.when(s < S5_NB)
    def _():
        x = jnp.dot(uo_s[...], win_ref[0], preferred_element_type=F32)
        for q in range(nq):
            x_s[s * nq + q] = x[:, 128 * q:128 * (q + 1)]

    @pl.when(s == S5_NB - 1)
    def _():
        def run(row0, nb, nc, h):
            for c in range(nc):
                for d in range(2):
                    cc = c if d == 0 else nc - 1 - c
                    rows = pl.ds(row0 + cc, nb, stride=nc)
                    for q in range(nq):
                        kr, ki = d * nq + q, (2 + d) * nq + q
                        hr, hi = h[d][0][q], h[d][1][q]
                        hs_s[kr, rows, :] = hr
                        hs_s[ki, rows, :] = hi
                        ar = a16_ref[0, :, 128 * kr:128 * (kr + 1)]
                        ai = a16_ref[0, :, 128 * ki:128 * (ki + 1)]
                        h[d][0][q] = ar * hr - ai * hi + x_s[kr, rows, :]
                        h[d][1][q] = ar * hi + ai * hr + x_s[ki, rows, :]
            return h

        zero = jnp.zeros((BATCH, 128), F32)
        fin = run(0, BATCH, SEQ // S5_CHUNK, [[[zero] * nq, [zero] * nq] for _ in range(2)])
        for d in range(2):
            for ri in range(2):
                for q in range(nq):
                    k = (2 * ri + d) * nq + q
                    hfin_ref[0, :, 128 * k:128 * (k + 1)] = fin[d][ri][q]
        h0 = [[[h0_ref[0, :, 128 * ((2 * ri + d) * nq + q):128 * ((2 * ri + d) * nq + q + 1)] for q in range(nq)]
               for ri in range(2)] for d in range(2)]
        run(S5_ROWS_CTX, DEC_BATCH, DEC_SEQ // S5_CHUNK, h0)

    @pl.when(s >= S5_NB)
    def _():
        hs = jnp.concatenate([hs_s[k] for k in range(4 * nq)], axis=1).astype(BF16)
        y = (jnp.dot(uo_s[...], m_ref[0], preferred_element_type=F32)
             + lax.dot_general(hs, wout_ref[0], (((1,), (1,)), ((), ())), preferred_element_type=F32))
        for q in range(nq):
            t = (s - S5_NB) * nq + q
            y_ref[pl.ds(t, S5_ROWS, stride=S5_CHUNK), :] = y[:, 128 * q:128 * (q + 1)]


def _s5_octets(t, lanes):
    return t.reshape(2, S5_NOCT, 1, S5_OCT * lanes).transpose(1, 0, 2, 3)


def _s5_blockdiag(t):
    a, n = t.shape[2], t.shape[3]
    t = t.reshape(2, S5_NOCT, S5_OCT, a, n)
    bd = jnp.einsum('dogan,gh->dogahn', t, jnp.eye(S5_OCT, dtype=t.dtype))
    return bd.reshape(2, S5_NOCT, S5_OCT * a, S5_OCT * n).transpose(1, 0, 2, 3)


def _s5_state_to_lanes(h_re, h_im):
    b = h_re.shape[0]
    parts = jnp.stack([h_re[:, 0], h_re[:, 1], h_im[:, 0], h_im[:, 1]], axis=1)
    parts = parts.reshape(b, 4, S5_NOCT, S5_PART).transpose(2, 0, 1, 3)
    return parts.reshape(S5_NOCT, b, S5_SW)


def _s5_state_from_lanes(h):
    b = h.shape[1]
    parts = h.reshape(S5_NOCT, b, 4, S5_OCT, S5_STATE).transpose(1, 2, 0, 3, 4).reshape(b, 4, S5_GROUPS, S5_STATE)
    return parts[:, 0:2], parts[:, 2:4]


def _s5_scan(u, h0_re, h0_im, lam_re, lam_im, log_dt, b_re, b_im, c_re, c_im):
    ng, n = S5_GROUPS, S5_STATE
    rows = 2 * ng
    disc = pl.pallas_call(
        _s5_disc_kernel,
        out_shape=[jax.ShapeDtypeStruct((rows, n), F32)] * 4,
        name="s5_disc",
    )(lam_re.reshape(rows, n), lam_im.reshape(rows, n), log_dt.reshape(rows, 1))
    ar, ai, zr, zi = [_s5_octets(t.reshape(2, ng, n), n) for t in disc]
    bt = lambda t: _s5_blockdiag(t.transpose(0, 1, 3, 2))
    vec = pl.BlockSpec((1, 2, 1, S5_PART), lambda o, t: (o, 0, 0, 0))
    mat = pl.BlockSpec((1, 2, 128, S5_PART), lambda o, t: (o, 0, 0, 0))
    rowblk = lambda w: pl.BlockSpec((1, 128, w), lambda o, t: (o, t, 0))
    m, win, wout, a16 = pl.pallas_call(
        _s5_prep_kernel,
        grid=(S5_NOCT, S5_CHUNK),
        in_specs=[vec, vec, vec, vec, mat, mat, mat, mat],
        out_specs=[rowblk(S5_K), rowblk(S5_SW), rowblk(S5_SW), pl.BlockSpec((1, 1, S5_SW), lambda o, t: (o, 0, 0))],
        out_shape=[jax.ShapeDtypeStruct((S5_NOCT, S5_K, S5_K), BF16), jax.ShapeDtypeStruct((S5_NOCT, S5_K, S5_SW), BF16),
                   jax.ShapeDtypeStruct((S5_NOCT, S5_K, S5_SW), BF16), jax.ShapeDtypeStruct((S5_NOCT, 1, S5_SW), F32)],
        scratch_shapes=[pltpu.VMEM((2, 2, 24, S5_PART), F32), pltpu.VMEM((2, 2, S5_K, S5_PART), BF16),
                        pltpu.VMEM(((2 * S5_CHUNK - 1) * 128, 128), F32)],
        compiler_params=_cparams("arbitrary", "arbitrary"),
        name="s5_prep",
    )(ar, ai, zr, zi, bt(b_re), bt(b_im), _s5_blockdiag(c_re), _s5_blockdiag(c_im))
    nb = S5_NB
    y, hfin = pl.pallas_call(
        _s5_main_kernel,
        grid=(S5_NOCT, 2 * nb),
        in_specs=[pl.BlockSpec((N_TOK, 128), lambda o, s: (0, o)),
                  pl.BlockSpec((1, S5_K, S5_PART), lambda o, s: (o, 0, jnp.minimum(s, nb - 1))),
                  pl.BlockSpec((1, S5_K, S5_PART), lambda o, s: (o, 0, jnp.maximum(s - nb, 0))),
                  pl.BlockSpec((1, S5_PART, S5_SW), lambda o, s: (o, jnp.maximum(s - nb, 0), 0)),
                  pl.BlockSpec((1, 1, S5_SW), lambda o, s: (o, 0, 0)),
                  pl.BlockSpec((1, DEC_BATCH, S5_SW), lambda o, s: (o, 0, 0))],
        out_specs=[pl.BlockSpec((N_TOK, 128), lambda o, s: (0, o)),
                   pl.BlockSpec((1, BATCH, S5_SW), lambda o, s: (o, 0, 0))],
        out_shape=[jax.ShapeDtypeStruct((N_TOK, S5_WIDTH), F32), jax.ShapeDtypeStruct((S5_NOCT, BATCH, S5_SW), F32)],
        scratch_shapes=[pltpu.VMEM((S5_ROWS, S5_K), BF16), pltpu.VMEM((S5_SW // 128, S5_ROWS, 128), F32),
                        pltpu.VMEM((S5_SW // 128, S5_ROWS, 128), F32)],
        compiler_params=_cparams("arbitrary", "arbitrary"),
        name="s5_main",
    )(u, win, m, wout, a16, _s5_state_to_lanes(h0_re, h0_im))
    fin_re, fin_im = _s5_state_from_lanes(hfin)
    return y, fin_re, fin_im


def _ctx_or_lat(ctx_ref, lat_ref):
    return jnp.where(pl.program_id(0) < N_CTX_BLK, ctx_ref[...], lat_ref[...])


def _split_specs(width):
    return [pl.BlockSpec((ROW_BLK, width), lambda i: (jnp.minimum(i, N_CTX_BLK - 1), 0)),
            pl.BlockSpec((ROW_BLK, width), lambda i: (jnp.maximum(i - N_CTX_BLK, 0), 0))]


def _ab_out_kernel(xc_ref, xl_ref, oac_ref, oal_ref, y_ref, u_ref, d_ref, wglu_ref, wout_ref, gt_ref, o_ref,
                   wglu_s, wa_s, wb_s):
    @pl.when(pl.program_id(0) == 0)
    def _():
        wglu_s[...] = wglu_ref[...].astype(BF16)
        wa_s[...] = wout_ref[0:A_WIDTH, :].astype(BF16)
        wb_s[...] = wout_ref[A_WIDTH:, :].astype(BF16)

    g = jax.nn.gelu(y_ref[...] + d_ref[...] * u_ref[...])
    ob = g * jax.nn.sigmoid(jnp.dot(g.astype(BF16), wglu_s[...], preferred_element_type=F32))
    out = (jnp.dot(_ctx_or_lat(oac_ref, oal_ref).astype(BF16), wa_s[...], preferred_element_type=F32)
           + jnp.dot(ob.astype(BF16), wb_s[...], preferred_element_type=F32))
    o_ref[...] = _ctx_or_lat(xc_ref, xl_ref) + gt_ref[0] * out


def _ab_out(x, oa, y, u, d_skip, w_glu, w_out, mod, layer):
    row = lambda n: pl.BlockSpec((ROW_BLK, n), lambda i: (i, 0))
    full = lambda a, b: pl.BlockSpec((a, b), lambda i: (0, 0), pipeline_mode=pl.Buffered(1))
    return pl.pallas_call(
        _ab_out_kernel,
        grid=(N_ROW_BLK,),
        in_specs=_split_specs(D_MODEL) + _split_specs(A_WIDTH) + [row(S5_WIDTH), row(S5_WIDTH), full(1, S5_WIDTH),
                  full(S5_WIDTH, S5_WIDTH), full(A_WIDTH + S5_WIDTH, D_MODEL), _mod_spec(layer, 2)],
        out_specs=row(D_MODEL),
        out_shape=jax.ShapeDtypeStruct((N_TOK, D_MODEL), F32),
        scratch_shapes=[pltpu.VMEM((S5_WIDTH, S5_WIDTH), BF16), pltpu.VMEM((A_WIDTH, D_MODEL), BF16),
                        pltpu.VMEM((S5_WIDTH, D_MODEL), BF16)],
        compiler_params=_cparams("arbitrary"),
        name="ab_out",
    )(x[0], x[1], oa[0], oa[1], y, u, d_skip.reshape(1, S5_WIDTH), w_glu, w_out, mod)


CD_HG0 = MLA_Q_RANK + MLA_KV_RANK + MLA_ROPE
KPE_LANES = 128


MLA_TILE = 128
MLA_QK = MLA_HEADS * MLA_TILE


def _mla_head_tiles(w, a):
    k, n = w.shape
    w = w.reshape(k, MLA_HEADS, n // MLA_HEADS)
    tiles = jnp.pad(w[:, :, :a], ((0, 0), (0, 0), (0, MLA_TILE - a)))
    return jnp.concatenate([tiles.reshape(k, -1), w[:, :, a:].reshape(k, -1)], axis=1)


def _mla_key_tiles(k_tiles, kpe):
    lane = lax.broadcasted_iota(jnp.int32, (1, MLA_TILE), 1)
    pe = jnp.where((lane >= MLA_NOPE) & (lane < MLA_NOPE + MLA_ROPE), kpe, 0.0)
    return k_tiles + jnp.concatenate([pe] * MLA_HEADS, axis=1)


def _cd_in_kernel(x_ref, y_ref, gp_ref, g_ref, sc_ref, sh_ref, w_ref, gq_ref, wqu_ref, gkv_ref, wkvu_ref,
                  cos_ref, sa_ref, sb_ref, cq_ref, sq_ref, tq_ref,
                  x1_ref, qf_ref, ckv_ref, kpe_ref, kf_ref, vm_ref, hq_ref, hff_ref, hfb_ref, hi_ref, hg_ref):
    dot = lambda a, b: jnp.dot(a, b, preferred_element_type=F32)
    c0, c1, c2 = MLA_Q_RANK, MLA_Q_RANK + MLA_KV_RANK, MLA_Q_RANK + MLA_KV_RANK + KPE_LANES
    x = x_ref[...] + gp_ref[0] * y_ref[...]
    x1_ref[...] = x
    h = _norm_mod(x, g_ref[...], sc_ref[0], sh_ref[0]).astype(BF16)
    cq = _rms(dot(h, w_ref[:, 0:c0])) * gq_ref[...]
    qq = dot(cq.astype(BF16), wqu_ref[...])
    qf_ref[...] = _apply_rope(qq, cq_ref[...], sq_ref[...], tq_ref[...], MLA_ROPE // 4).astype(qf_ref.dtype)
    ckv = _rms(dot(h, w_ref[:, c0:c1])) * gkv_ref[...]
    ckv_ref[...] = ckv
    kv = dot(ckv.astype(BF16), wkvu_ref[...])
    kpe = _apply_rope(dot(h, w_ref[:, c1:c2]), cos_ref[...], sa_ref[...], sb_ref[...], MLA_ROPE // 4)
    kpe_ref[...] = kpe
    kf_ref[...] = _mla_key_tiles(kv[:, 0:MLA_QK], kpe).astype(kf_ref.dtype)
    vm_ref[...] = kv[:, MLA_QK:].astype(vm_ref.dtype)
    for n, ref in enumerate((hq_ref, hff_ref, hfb_ref, hi_ref, hg_ref)):
        ref[...] = dot(h, w_ref[:, c2 + HG_KD * n:c2 + HG_KD * (n + 1)])


def _cd_in(x, y_prev, mod, g, w, g_q, w_q_up, g_kv, w_kv_up, layer):
    cos, sa, sb = _rope_tables(MLA_ROPE)
    cos_q, sa_q, sb_q = _rope_tables(MLA_ROPE, only=(MLA_NOPE, MLA_NOPE + MLA_ROPE))
    c1 = MLA_Q_RANK + MLA_KV_RANK
    w = jnp.concatenate([w[:, :c1], jnp.tile(w[:, c1:CD_HG0], (1, KPE_LANES // MLA_ROPE)), w[:, CD_HG0:]],
                        axis=1).astype(BF16)
    d_in = w.shape[1]
    row = lambda n: pl.BlockSpec((ROW_BLK, n), lambda i: (i, 0))
    full = lambda a, b: pl.BlockSpec((a, b), lambda i: (0, 0), pipeline_mode=pl.Buffered(1))
    rope = pl.BlockSpec((ROW_BLK, 128), lambda i: (_rope_blk(i), 0))
    outs = ([(D_MODEL, F32), (MLA_QK, BF16), (MLA_KV_RANK, F32), (KPE_LANES, F32), (MLA_QK, BF16), (MLA_WIDTH, BF16)]
            + [(HG_KD, F32)] * 5)
    nkv = MLA_QK + MLA_WIDTH
    return pl.pallas_call(
        _cd_in_kernel,
        grid=(N_ROW_BLK,),
        in_specs=[row(D_MODEL), row(D_MODEL), _mod_spec(layer - 1, 5),
                  full(1, D_MODEL), _mod_spec(layer, 1), _mod_spec(layer, 0), full(D_MODEL, d_in),
                  full(1, MLA_Q_RANK), full(MLA_Q_RANK, MLA_QK), full(1, MLA_KV_RANK), full(MLA_KV_RANK, nkv),
                  rope, rope, rope, rope, rope, rope],
        out_specs=[row(n) for n, _ in outs],
        out_shape=[jax.ShapeDtypeStruct((N_TOK, n), dt) for n, dt in outs],
        compiler_params=_cparams("arbitrary"),
        name="cd_in",
    )(x, y_prev, mod, g.reshape(1, D_MODEL), mod, mod, w, g_q.reshape(1, -1),
      _mla_head_tiles(w_q_up, MLA_NOPE + MLA_ROPE).astype(BF16), g_kv.reshape(1, -1),
      _mla_head_tiles(w_kv_up, MLA_NOPE).astype(BF16), cos, sa, sb, cos_q, sa_q, sb_q)


def _mla_cache_kernel(c_ref, kpe_ref, w_ref, kf_ref, v_ref):
    kv = _dot(c_ref[...], w_ref[...])
    kf_ref[...] = _mla_key_tiles(kv[:, 0:MLA_QK], kpe_ref[...]).astype(kf_ref.dtype)
    v_ref[...] = kv[:, MLA_QK:].astype(v_ref.dtype)


def _mla_cache_kv(cckv, ckpe, w_kv_up):
    rows = cckv.shape[0]
    blk = lambda w: pl.BlockSpec((PAST_LEN, w), lambda i: (i, 0))
    return pl.pallas_call(
        _mla_cache_kernel,
        grid=(rows // PAST_LEN,),
        in_specs=[blk(MLA_KV_RANK), blk(KPE_LANES), pl.BlockSpec((MLA_KV_RANK, MLA_QK + MLA_WIDTH), lambda i: (0, 0))],
        out_specs=[blk(MLA_QK), blk(MLA_WIDTH)],
        out_shape=[jax.ShapeDtypeStruct((rows, MLA_QK), BF16), jax.ShapeDtypeStruct((rows, MLA_WIDTH), BF16)],
        compiler_params=_cparams("arbitrary"),
        name="mla_cache_kv",
    )(cckv, jnp.tile(ckpe, (1, KPE_LANES // MLA_ROPE)), _mla_head_tiles(w_kv_up, MLA_NOPE))


def _mla_heads(qf, keys, o_ref):
    scale = (MLA_NOPE + MLA_ROPE) ** -0.5
    outs = []
    for h in range(MLA_HEADS):
        t = slice(MLA_TILE * h, MLA_TILE * (h + 1))
        a = slice(MLA_V * h, MLA_V * (h + 1))
        s_list = [_dot_nt(qf[:, t], kf[:, t]) * scale for kf, _ in keys]
        outs.append(_softmax_pv(s_list, [v[:, a] for _, v in keys], None))
    o_ref[...] = jnp.concatenate(outs, axis=1).astype(o_ref.dtype)


def _mla_ctx_kernel(qf_ref, kf_ref, v_ref, o_ref):
    _mla_heads(qf_ref[...], [(kf_ref[...], v_ref[...])], o_ref)


def _mla_lat_kernel(qf_ref, kf_ref, v_ref, kcf_ref, vc_ref, o_ref):
    _mla_heads(qf_ref[...], [(kcf_ref[...], vc_ref[...]), (kf_ref[...], v_ref[...])], o_ref)


def _attention_mla(qf, kf, vm, kcf, vc):
    blk = lambda w: pl.BlockSpec((SEQ, w), lambda b: (b, 0))
    o = pl.pallas_call(
        _mla_ctx_kernel,
        grid=(BATCH,),
        in_specs=[blk(MLA_QK), blk(MLA_QK), blk(MLA_WIDTH)],
        out_specs=blk(MLA_WIDTH),
        out_shape=jax.ShapeDtypeStruct((N_CTX_TOK, MLA_WIDTH), BF16),
        compiler_params=_cparams("arbitrary"),
        name="mla_ctx",
    )(qf, kf, vm)
    nq = DEC_SEQ // MLA_QBLK
    qblk = lambda w: pl.BlockSpec((MLA_QBLK, w), lambda b, i: (N_CTX_TOK // MLA_QBLK + b * nq + i, 0))
    seq = lambda w: pl.BlockSpec((DEC_SEQ, w), lambda b, i: (N_CTX_TOK // DEC_SEQ + b, 0))
    past = lambda w: pl.BlockSpec((PAST_LEN, w), lambda b, i: (b, 0))
    o_lat = pl.pallas_call(
        _mla_lat_kernel,
        grid=(DEC_BATCH, nq),
        in_specs=[qblk(MLA_QK), seq(MLA_QK), seq(MLA_WIDTH), past(MLA_QK), past(MLA_WIDTH)],
        out_specs=pl.BlockSpec((MLA_QBLK, MLA_WIDTH), lambda b, i: (b * nq + i, 0)),
        out_shape=jax.ShapeDtypeStruct((N_LAT_TOK, MLA_WIDTH), BF16),
        compiler_params=_cparams("arbitrary", "arbitrary"),
        name="mla_lat",
    )(qf, kf, vm, kcf, vc)
    return o, o_lat


HG_TILE = 128
HG_NC = HG_TILE // HG_CHUNK
HG_HALF = 256


HG_SLABS = HG_KD // 128


def _hg_token_plane(raw_s, p):
    return jnp.concatenate([raw_s[j, pl.ds(p, HG_NC, stride=HG_CHUNK), :] for j in range(HG_SLABS)], axis=1)


def _hg_put(ref, x):
    for j in range(HG_SLABS):
        ref[j] = x[:, 128 * j:128 * (j + 1)]


def _hg_get(ref, c, l):
    rows = pl.ds(c, HG_CHUNK, stride=HG_NC)
    return jnp.concatenate([ref[2 * l, rows, :], ref[2 * l + 1, rows, :]], axis=1)


def _hg_plane(ref, p):
    return jnp.concatenate([ref[j, HG_NC * p:HG_NC * (p + 1), :] for j in range(HG_SLABS)], axis=1)


def _hg_put_plane(ref, p, x):
    for j in range(HG_SLABS):
        ref[j, HG_NC * p:HG_NC * (p + 1), :] = x[:, 128 * j:128 * (j + 1)]


HG_PAIRS = HG_CHUNK * (HG_CHUNK + 1) // 2
HG_PAIR_ROWS = HG_PAIRS * HG_NC


def _hg_pair_rows(d, i):
    n = d * HG_CHUNK - d * (d - 1) // 2 + (i - d)
    return slice(HG_NC * n, HG_NC * (n + 1))


def _hg_direction(hq_ref, hf_ref, hi_ref, o_ref, lb, st_ref, raw_s, q_s, f_s, kk_s, v_s, qt_s, kt_s, o_s, p_s, e_s,
                  sign, ones_bd, head_mask):
    _hg_put(raw_s, hq_ref[...])
    for p in range(HG_CHUNK):
        hq = _hg_token_plane(raw_s, p)
        _hg_put_plane(q_s, p, hq * jax.nn.sigmoid(hq))
    _hg_put(raw_s, hf_ref[...])
    for p in range(HG_CHUNK):
        f = lb + (1.0 - lb) * jax.nn.sigmoid(_hg_token_plane(raw_s, p))
        _hg_put_plane(f_s, p, f)
        _hg_put_plane(kk_s, p, 1.0 - f)
    _hg_put(raw_s, hi_ref[...])
    for p in range(HG_CHUNK):
        _hg_put_plane(v_s, p, _hg_token_plane(raw_s, p))
    pos = (lambda i: i) if sign > 0 else (lambda i: HG_CHUNK - 1 - i)
    plane = lambda ref, i: _hg_plane(ref, pos(i))

    for i in range(HG_CHUNK):
        q = plane(q_s, i)
        p_s[_hg_pair_rows(0, i), :] = q * plane(kk_s, i)
        dec = None
        for d in range(1, i + 1):
            fd = plane(f_s, i - d + 1)
            dec = fd if dec is None else dec * fd
            p_s[_hg_pair_rows(d, i), :] = q * plane(kk_s, i - d) * dec
    step = HG_PAIR_ROWS // 4
    for c in range(0, HG_PAIR_ROWS, step):
        pb = p_s[c:c + step, :].astype(BF16)
        e_s[c:c + step, 0:HG_HALF] = jnp.dot(pb[:, 0:HG_HALF], ones_bd, preferred_element_type=F32)
        e_s[c:c + step, HG_HALF:] = jnp.dot(pb[:, HG_HALF:], ones_bd, preferred_element_type=F32)
    for i in range(HG_CHUNK):
        o = None
        for d in range(i + 1):
            t = e_s[_hg_pair_rows(d, i), :] * plane(v_s, i - d)
            o = t if o is None else o + t
        _hg_put_plane(o_s, pos(i), o)

    incl = None
    for i in range(HG_CHUNK):
        fi = plane(f_s, i)
        incl = fi if incl is None else incl * fi
        _hg_put_plane(qt_s, pos(i), plane(q_s, i) * incl)
    whole = incl
    excl = None
    for i in range(HG_CHUNK - 1, -1, -1):
        kt = plane(kk_s, i)
        if excl is not None:
            kt = kt * excl
        _hg_put_plane(kt_s, pos(i), kt)
        fi = plane(f_s, i)
        excl = fi if excl is None else excl * fi

    tile4 = lambda x: jnp.where(head_mask, jnp.concatenate([x] * 4, axis=0), 0.0).astype(BF16)
    for cc in range(HG_NC):
        c = cc if sign > 0 else HG_NC - 1 - cc
        r = slice(HG_CHUNK * c, HG_CHUNK * (c + 1))
        for g in range(2):
            l = slice(HG_HALF * g, HG_HALF * (g + 1))
            st = st_ref[g]
            oc = _dot_nt(tile4(_hg_get(qt_s, c, g)), st)
            o_ref[r, l] = _hg_get(o_s, c, g) + jnp.concatenate(
                [oc[HG_CHUNK * h:HG_CHUNK * (h + 1), :] for h in range(4)], axis=1)
            v = hi_ref[r, l]
            vs = jnp.concatenate([v[:, HG_VAL * h:HG_VAL * (h + 1)] for h in range(4)], axis=0)
            ds = lax.dot_general(vs.astype(BF16), tile4(_hg_get(kt_s, c, g)), (((0,), (0,)), ((), ())),
                                 preferred_element_type=F32)
            st_ref[g] = st * whole[c:c + 1, l] + ds


def _hg_lower_bound(lb_ref, d, layer):
    raw = [lb_ref[d * DEPTH + m:d * DEPTH + m + 1, :] for m in range(DEPTH)]
    mx = functools.reduce(jnp.maximum, raw)
    e = [jnp.exp(r - mx) for r in raw]
    tot = functools.reduce(jnp.add, e)
    return functools.reduce(jnp.add, e[1:layer + 1], jnp.zeros_like(tot)) / tot


def _hgrn_kernel(nt, layer, hqf_ref, hff_ref, hif_ref, hqb_ref, hfb_ref, hib_ref, lb_ref, s0_ref,
                 of_ref, ob_ref, sfin_ref, st_s, raw_s, q_s, f_s, kk_s, v_s, qt_s, kt_s, o_s, p_s, e_s):
    i = pl.program_id(1)

    @pl.when(i == 0)
    def _():
        st_s[...] = s0_ref[0]

    ri = lax.broadcasted_iota(jnp.int32, (HG_HALF, HG_HALF), 0) // HG_KEY
    ci = lax.broadcasted_iota(jnp.int32, (HG_HALF, HG_HALF), 1) // HG_KEY
    ones_bd = jnp.where(ri == ci, 1.0, 0.0).astype(BF16)
    head_mask = (lax.broadcasted_iota(jnp.int32, (4 * HG_CHUNK, HG_HALF), 0) // HG_CHUNK
                 == lax.broadcasted_iota(jnp.int32, (4 * HG_CHUNK, HG_HALF), 1) // HG_KEY)
    _hg_direction(hqf_ref, hff_ref, hif_ref, of_ref, _hg_lower_bound(lb_ref, 0, layer), st_s.at[0],
                  raw_s, q_s, f_s, kk_s, v_s, qt_s, kt_s, o_s, p_s, e_s, 1, ones_bd, head_mask)
    _hg_direction(hqb_ref, hfb_ref, hib_ref, ob_ref, _hg_lower_bound(lb_ref, 1, layer), st_s.at[1],
                  raw_s, q_s, f_s, kk_s, v_s, qt_s, kt_s, o_s, p_s, e_s, -1, ones_bd, head_mask)

    @pl.when(i == nt - 1)
    def _():
        sfin_ref[0] = st_s[...]


def _hg_state_to_blocks(s):
    b = s.shape[0]
    st = s.reshape(b, 2, 2, 4, HG_KEY, HG_VAL).transpose(0, 1, 2, 5, 3, 4)
    return st.reshape(b, 2, 2, HG_VAL, HG_HALF)


def _hg_state_from_blocks(st):
    b = st.shape[0]
    st = st.reshape(b, 2, 2, HG_VAL, 4, HG_KEY).transpose(0, 1, 2, 4, 5, 3)
    return st.reshape(b, 2, HG_HEADS, HG_KEY, HG_VAL)


def _hgrn_scan(hq, hff, hfb, hi, lb, s0, layer, row0, nseq, seqlen):
    nt = seqlen // HG_TILE
    base = row0 // HG_TILE
    fwd = pl.BlockSpec((HG_TILE, HG_KD), lambda b, i: (base + b * nt + i, 0))
    bwd = pl.BlockSpec((HG_TILE, HG_KD), lambda b, i: (base + b * nt + nt - 1 - i, 0))
    ofw = pl.BlockSpec((HG_TILE, HG_KD), lambda b, i: (b * nt + i, 0))
    obw = pl.BlockSpec((HG_TILE, HG_KD), lambda b, i: (b * nt + nt - 1 - i, 0))
    st = pl.BlockSpec((1, 2, 2, HG_VAL, HG_HALF), lambda b, i: (b, 0, 0, 0, 0))
    tile = lambda: pltpu.VMEM((HG_SLABS, HG_TILE, 128), F32)
    pairs = lambda: pltpu.VMEM((HG_PAIR_ROWS, HG_KD), F32)
    return pl.pallas_call(
        functools.partial(_hgrn_kernel, nt, layer),
        grid=(nseq, nt),
        in_specs=[fwd, fwd, fwd, bwd, bwd, bwd, pl.BlockSpec((2 * DEPTH, HG_KD), lambda b, i: (0, 0)), st],
        out_specs=[ofw, obw, st],
        out_shape=[jax.ShapeDtypeStruct((nseq * seqlen, HG_WIDTH), F32)] * 2
        + [jax.ShapeDtypeStruct((nseq, 2, 2, HG_VAL, HG_HALF), F32)],
        scratch_shapes=[pltpu.VMEM((2, 2, HG_VAL, HG_HALF), F32)] + [tile() for _ in range(8)] + [pairs(), pairs()],
        compiler_params=_cparams("arbitrary", "arbitrary"),
        name="hgrn_scan",
    )(hq, hff, hi, hq, hfb, hi, lb.reshape(2 * DEPTH, HG_KD), s0)


def _cd_out_kernel(x_ref, occ_ref, ocl_ref, ofc_ref, ofl_ref, obc_ref, obl_ref, hg_ref, go_ref, wout_ref, gt_ref,
                   o_ref, wa_s, wb_s):
    @pl.when(pl.program_id(0) == 0)
    def _():
        wa_s[...] = wout_ref[0:MLA_WIDTH, :].astype(BF16)
        wb_s[...] = wout_ref[MLA_WIDTH:, :].astype(BF16)

    ri = lax.broadcasted_iota(jnp.int32, (HG_HALF, HG_HALF), 0) // HG_VAL
    ci = lax.broadcasted_iota(jnp.int32, (HG_HALF, HG_HALF), 1) // HG_VAL
    ones_bd = jnp.where(ri == ci, 1.0, 0.0).astype(BF16)
    o = _ctx_or_lat(ofc_ref, ofl_ref) + _ctx_or_lat(obc_ref, obl_ref)
    sq = o * o
    hi = sq.astype(BF16)
    lo = (sq - hi.astype(F32)).astype(BF16)
    ms = jnp.concatenate(
        [jnp.dot(hi[:, l], ones_bd, preferred_element_type=F32) + jnp.dot(lo[:, l], ones_bd, preferred_element_type=F32)
         for l in (slice(0, HG_HALF), slice(HG_HALF, 2 * HG_HALF))], axis=1) * (1.0 / HG_VAL)
    hg = hg_ref[...]
    od = o * lax.rsqrt(ms + EPS) * go_ref[...] * (hg * jax.nn.sigmoid(hg))
    out = (jnp.dot(_ctx_or_lat(occ_ref, ocl_ref).astype(BF16), wa_s[...], preferred_element_type=F32)
           + jnp.dot(od.astype(BF16), wb_s[...], preferred_element_type=F32))
    o_ref[...] = x_ref[...] + gt_ref[0] * out


def _cd_out(x, oc, of, ob, hg, g_o, w_out, mod, layer):
    row = lambda n: pl.BlockSpec((ROW_BLK, n), lambda i: (i, 0))
    full = lambda a, b: pl.BlockSpec((a, b), lambda i: (0, 0), pipeline_mode=pl.Buffered(1))
    return pl.pallas_call(
        _cd_out_kernel,
        grid=(N_ROW_BLK,),
        in_specs=[row(D_MODEL)] + _split_specs(MLA_WIDTH) + _split_specs(HG_WIDTH) + _split_specs(HG_WIDTH)
        + [row(HG_WIDTH), full(1, HG_WIDTH), full(MLA_WIDTH + HG_WIDTH, D_MODEL), _mod_spec(layer, 2)],
        out_specs=row(D_MODEL),
        out_shape=jax.ShapeDtypeStruct((N_TOK, D_MODEL), F32),
        scratch_shapes=[pltpu.VMEM((MLA_WIDTH, D_MODEL), BF16), pltpu.VMEM((HG_WIDTH, D_MODEL), BF16)],
        compiler_params=_cparams("arbitrary"),
        name="cd_out",
    )(x, oc[0], oc[1], of[0], of[1], ob[0], ob[1], hg, jnp.tile(g_o, HG_HEADS).reshape(1, HG_WIDTH), w_out, mod)


N_PAIRS = 6
N_CLASSES = N_GROUPS * N_PAIRS
CLS_ROWS = 32
MOE_BM = 256
MOE_NBLK = N_TOK // MOE_BM + N_CLASSES
MOE_ROWS = MOE_NBLK * MOE_BM
PAIR_SLOTS = ((0, 1), (3, 1), (2, 1), (2, 0), (3, 0), (3, 2))


def _moe_route_kernel(x_ref, g_ref, sc_ref, sh_ref, rw_ref, rb_ref, h_ref, ti_ref, tw_ref, cnt_ref, base_s):
    @pl.when(pl.program_id(0) == 0)
    def _():
        base_s[...] = jnp.zeros_like(base_s)

    h = _norm_mod(x_ref[...], g_ref[...], sc_ref[0], sh_ref[0])
    logits = lax.dot_general(rw_ref[...], h, (((1,), (1,)), ((), ())), precision=HI,
                             preferred_element_type=F32)
    aff = jax.nn.sigmoid(logits)
    sel = aff + rb_ref[...]
    s = [sel[e:e + 1, :] for e in range(N_EXPERTS)]
    a = [aff[e:e + 1, :] for e in range(N_EXPERTS)]
    gs = []
    for g in range(N_GROUPS):
        m = s[4 * g:4 * g + 4]
        pairs = [m[i] + m[j] for i in range(4) for j in range(i + 1, 4)]
        gs.append(functools.reduce(jnp.maximum, pairs))
    gmax = functools.reduce(jnp.maximum, gs)
    taken = jnp.zeros_like(gmax) > 1.0
    gsel = []
    for g in range(N_GROUPS):
        hit = (gs[g] == gmax) & jnp.logical_not(taken)
        gsel.append(hit)
        taken = taken | hit
    e_lo = jnp.zeros(gmax.shape, jnp.int32)
    e_hi = jnp.zeros(gmax.shape, jnp.int32)
    a_lo = jnp.zeros_like(gmax)
    a_hi = jnp.zeros_like(gmax)
    nsel = jnp.zeros(gmax.shape, jnp.int32)
    for g in range(N_GROUPS):
        for i in range(4):
            e = 4 * g + i
            beat = jnp.zeros(gmax.shape, jnp.int32)
            for j in range(4):
                if j != i:
                    o = 4 * g + j
                    beat = beat + jnp.where((s[o] > s[e]) | ((s[o] == s[e]) & (j < i)), 1, 0)
            pick = gsel[g] & (beat < 2)
            is_first = pick & (nsel == 0)
            is_second = pick & (nsel == 1)
            e_lo = jnp.where(is_first, e, e_lo)
            a_lo = jnp.where(is_first, a[e], a_lo)
            e_hi = jnp.where(is_second, e, e_hi)
            a_hi = jnp.where(is_second, a[e], a_hi)
            nsel = nsel + jnp.where(pick, 1, 0)
    grp = e_lo // EXPERTS_PER_GROUP
    lo = e_lo - grp * EXPERTS_PER_GROUP
    hi = e_hi - grp * EXPERTS_PER_GROUP
    pair = ((lo * (7 - lo)) >> 1) + (hi - lo - 1)
    pair = jnp.where(pair == 1, 3, jnp.where(pair == 2, 4, jnp.where(pair == 3, 2, jnp.where(pair == 4, 1, pair))))
    cls = grp * N_PAIRS + pair
    wsum = a_lo + a_hi
    w_lo, w_hi = a_lo / wsum, a_hi / wsum
    onehot = (lax.broadcasted_iota(jnp.int32, (CLS_ROWS, ROW_BLK), 0) == cls).astype(F32)
    tt = lax.broadcasted_iota(jnp.int32, (ROW_BLK, ROW_BLK), 0) < lax.broadcasted_iota(jnp.int32, (ROW_BLK, ROW_BLK), 1)
    before = _dot(onehot, jnp.where(tt, 1.0, 0.0))
    base = base_s[...]
    rank = jnp.sum(onehot * (before + base[:, 0:1]), axis=0, keepdims=True).astype(jnp.int32)
    base = base + jnp.sum(onehot, axis=1, keepdims=True)
    base_s[...] = base
    cnt_ref[...] = base.astype(jnp.int32)
    ti_ref[0] = jnp.concatenate([cls, rank, e_lo, e_hi, jnp.zeros((4, ROW_BLK), jnp.int32)], axis=0)
    ident = lax.broadcasted_iota(jnp.int32, (ROW_BLK, ROW_BLK), 0) == lax.broadcasted_iota(jnp.int32, (ROW_BLK, ROW_BLK), 1)
    col = lambda r: jnp.sum(jnp.where(ident, r, 0.0), axis=1, keepdims=True)
    tw_ref[...] = jnp.where(lax.broadcasted_iota(jnp.int32, (ROW_BLK, 128), 1) < 64, col(w_lo), col(w_hi))
    h_ref[...] = h


def _moe_route(x, mod, g, router_w, router_b, layer):
    row = lambda n: pl.BlockSpec((ROW_BLK, n), lambda i: (i, 0))
    full = lambda a, b: pl.BlockSpec((a, b), lambda i: (0, 0), pipeline_mode=pl.Buffered(1))
    return pl.pallas_call(
        _moe_route_kernel,
        grid=(N_ROW_BLK,),
        in_specs=[row(D_MODEL), full(1, D_MODEL), _mod_spec(layer, 4), _mod_spec(layer, 3),
                  full(N_EXPERTS, D_MODEL), full(N_EXPERTS, 1)],
        out_specs=[row(D_MODEL), pl.BlockSpec((1, 8, ROW_BLK), lambda i: (i, 0, 0)), row(128),
                   pl.BlockSpec((CLS_ROWS, 128), lambda i: (0, 0))],
        out_shape=[jax.ShapeDtypeStruct((N_TOK, D_MODEL), F32),
                   jax.ShapeDtypeStruct((N_ROW_BLK, 8, ROW_BLK), jnp.int32),
                   jax.ShapeDtypeStruct((N_TOK, 128), F32),
                   jax.ShapeDtypeStruct((CLS_ROWS, 128), jnp.int32)],
        scratch_shapes=[pltpu.VMEM((CLS_ROWS, 128), F32)],
        compiler_params=_cparams("arbitrary"),
        name="moe_route",
    )(x, g.reshape(1, D_MODEL), mod, mod, router_w.T, router_b.reshape(N_EXPERTS, 1))


def _moe_sort_kernel(pos_ref, nblk_ref, h_ref, wt_ref, hs_ref, ws_ref, perm_s, stage_s):
    j = pl.program_id(0)

    @pl.when(j == 0)
    def _():
        def init(r, c):
            perm_s[r] = N_TOK - 1
            return c

        lax.fori_loop(0, MOE_ROWS, init, 0, unroll=16)

        def build(t, c):
            perm_s[pos_ref[t]] = t
            return c

        lax.fori_loop(0, N_TOK, build, 0, unroll=16)

    @pl.when(j < nblk_ref[0])
    def _():
        base = j * MOE_BM
        for r in range(MOE_BM):
            src = perm_s[base + r]
            stage_s[r:r + 1, :] = h_ref[pl.ds(src, 1), :]
            ws_ref[r:r + 1, :] = wt_ref[pl.ds(src, 1), :]
        hs_ref[...] = stage_s[...].astype(BF16)

    @pl.when(j >= nblk_ref[0])
    def _():
        hs_ref[...] = jnp.zeros_like(hs_ref)
        ws_ref[...] = jnp.zeros_like(ws_ref)


def _moe_sort(h, wtok, pos, nblk):
    res = lambda w: pl.BlockSpec((N_TOK, w), lambda j, *_: (0, 0), pipeline_mode=pl.Buffered(1))
    grid_spec = pltpu.PrefetchScalarGridSpec(
        num_scalar_prefetch=2,
        grid=(MOE_NBLK,),
        in_specs=[res(D_MODEL), res(128)],
        out_specs=[pl.BlockSpec((MOE_BM, D_MODEL), lambda j, *_: (j, 0)),
                   pl.BlockSpec((MOE_BM, 128), lambda j, *_: (j, 0))],
        scratch_shapes=[pltpu.SMEM((MOE_ROWS,), jnp.int32), pltpu.VMEM((MOE_BM, D_MODEL), F32)],
    )
    return pl.pallas_call(
        _moe_sort_kernel,
        grid_spec=grid_spec,
        out_shape=[jax.ShapeDtypeStruct((MOE_ROWS, D_MODEL), BF16), jax.ShapeDtypeStruct((MOE_ROWS, 128), F32)],
        compiler_params=_cparams("arbitrary"),
        name="moe_sort",
    )(pos, nblk, h, wtok)


def _moe_experts_kernel(elo_ref, ehi_ref, nblk_ref, swap_ref, hs_ref, ws_ref, wgl_ref, wul_ref, wdl_ref,
                        wgh_ref, wuh_ref, wdh_ref, y_ref):
    del elo_ref, ehi_ref
    j = pl.program_id(0)

    @pl.when(j < nblk_ref[0])
    def _():
        h = hs_ref[...].astype(BF16)
        swap = swap_ref[j] == 1
        gate_a = jnp.where(swap, ws_ref[:, 64:65], ws_ref[:, 0:1])
        gate_b = jnp.where(swap, ws_ref[:, 0:1], ws_ref[:, 64:65])
        acc = None
        for wg, wu, wd, gate in ((wgl_ref, wul_ref, wdl_ref, gate_a), (wgh_ref, wuh_ref, wdh_ref, gate_b)):
            g = _dot(h, wg[0, 0])
            u = _dot(h, wu[0, 0])
            hid = g * jax.nn.sigmoid(g) * u * gate
            y = _dot(hid, wd[0, 0])
            acc = y if acc is None else acc + y
        y_ref[...] = acc

    @pl.when(j >= nblk_ref[0])
    def _():
        y_ref[...] = jnp.zeros_like(y_ref)


def _moe_experts(hs, ws, blk_elo, blk_ehi, nblk, swap, w_gate, w_up, w_down, layer):
    last = lambda j, nb: jnp.minimum(j, nb[0] - 1)
    wspec = lambda a, b, which: pl.BlockSpec(
        (1, 1, a, b), lambda j, elo, ehi, nb, sw: (layer, (elo, ehi)[which][last(j, nb)], 0, 0))
    grid_spec = pltpu.PrefetchScalarGridSpec(
        num_scalar_prefetch=4,
        grid=(MOE_NBLK,),
        in_specs=[pl.BlockSpec((MOE_BM, D_MODEL), lambda j, elo, ehi, nb, sw: (last(j, nb), 0)),
                  pl.BlockSpec((MOE_BM, 128), lambda j, elo, ehi, nb, sw: (last(j, nb), 0)),
                  wspec(D_MODEL, D_FF, 0), wspec(D_MODEL, D_FF, 0), wspec(D_FF, D_MODEL, 0),
                  wspec(D_MODEL, D_FF, 1), wspec(D_MODEL, D_FF, 1), wspec(D_FF, D_MODEL, 1)],
        out_specs=pl.BlockSpec((MOE_BM, D_MODEL), lambda j, *_: (j, 0)),
    )
    return pl.pallas_call(
        _moe_experts_kernel,
        grid_spec=grid_spec,
        out_shape=jax.ShapeDtypeStruct((MOE_ROWS, D_MODEL), F32),
        compiler_params=_cparams("arbitrary"),
        name="moe_experts",
    )(blk_elo, blk_ehi, nblk, swap, hs, ws, w_gate, w_up, w_down, w_gate, w_up, w_down)


SC_ROWS = 64


def _sc_gather_rows(table, idx):
    info = plsc.get_sparse_core_info()
    nc, nw = info.num_cores, info.num_cores * info.num_subcores
    b, d = idx.shape[0], table.shape[1]
    per_w = b // nw
    assert per_w * nw == b and per_w % SC_ROWS == 0
    mesh = plsc.VectorSubcoreMesh(core_axis_name="c", subcore_axis_name="s")

    @functools.partial(
        pl.kernel, mesh=mesh, out_type=jax.ShapeDtypeStruct((b, d), table.dtype),
        scratch_types=[pltpu.VMEM((SC_ROWS,), jnp.int32), pltpu.VMEM((SC_ROWS, d), table.dtype),
                       pltpu.SemaphoreType.DMA])
    def gather(table_hbm, idx_hbm, out_hbm, idx_v, rows_v, sem):
        wid = lax.axis_index("s") * nc + lax.axis_index("c")
        for c in range(per_w // SC_ROWS):
            base = wid * per_w + c * SC_ROWS
            pltpu.sync_copy(idx_hbm.at[pl.ds(base, SC_ROWS)], idx_v)
            pltpu.async_copy(table_hbm.at[idx_v], rows_v, sem).wait()
            pltpu.sync_copy(rows_v, out_hbm.at[pl.ds(base, SC_ROWS)])

    return gather(table, idx)


def _moe_residual_kernel(x_ref, y_ref, gt_ref, o_ref):
    o_ref[...] = x_ref[...] + gt_ref[0] * y_ref[...]


def _moe_residual(x, y_tok, mod, layer):
    row = pl.BlockSpec((ROW_BLK, D_MODEL), lambda i: (i, 0))
    return pl.pallas_call(
        _moe_residual_kernel,
        grid=(N_ROW_BLK,),
        in_specs=[row, row, _mod_spec(layer, 5)],
        out_specs=row,
        out_shape=jax.ShapeDtypeStruct((N_TOK, D_MODEL), F32),
        compiler_params=_cparams("arbitrary"),
        name="moe_residual",
    )(x, y_tok, mod)


def _moe(x, mod, g, router_w, router_b, w_gate, w_up, w_down, layer):
    h, info, wtok, counts = _moe_route(x, mod, g, router_w, router_b, layer)
    cls = info[:, 0, :].reshape(N_TOK)
    rank = info[:, 1, :].reshape(N_TOK)
    cnt = counts[:N_CLASSES, 0]
    nb = (cnt + MOE_BM - 1) // MOE_BM
    ends = jnp.cumsum(nb)
    starts = ends - nb
    pos = ((starts * MOE_BM)[cls] + rank).astype(jnp.int32)
    blk = jnp.arange(MOE_NBLK, dtype=jnp.int32)
    blk_cls = jnp.minimum(jnp.sum((blk[:, None] >= ends[None, :]).astype(jnp.int32), axis=1), N_CLASSES - 1)
    slot_a = jnp.asarray([s[0] for s in PAIR_SLOTS], jnp.int32)
    slot_b = jnp.asarray([s[1] for s in PAIR_SLOTS], jnp.int32)
    grp = blk_cls // N_PAIRS
    blk_ea = (grp * EXPERTS_PER_GROUP + slot_a[blk_cls % N_PAIRS]).astype(jnp.int32)
    blk_eb = (grp * EXPERTS_PER_GROUP + slot_b[blk_cls % N_PAIRS]).astype(jnp.int32)
    swap = (blk_ea > blk_eb).astype(jnp.int32)
    nblk = ends[-1:].astype(jnp.int32)
    hs, ws = _moe_sort(h, wtok, pos, nblk)
    y_sorted = _moe_experts(hs, ws, blk_ea, blk_eb, nblk, swap, w_gate, w_up, w_down, layer)
    return _sc_gather_rows(y_sorted, pos)


def _final_norm_kernel(x_ref, y_ref, gt_ref, g_ref, o_ref):
    o_ref[...] = _rms(x_ref[...] + gt_ref[0] * y_ref[...]) * g_ref[...]


def _final_norm(x, y_tok, mod, layer, g, row0, rows):
    base = row0 // ROW_BLK
    row = pl.BlockSpec((ROW_BLK, D_MODEL), lambda i: (base + i, 0))
    gate = pl.BlockSpec((1, 1, D_MODEL), lambda i: ((layer * MOD_ROWS + _mod_group(base + i)) * N_MOD + 5, 0, 0))
    return pl.pallas_call(
        _final_norm_kernel,
        grid=(rows // ROW_BLK,),
        in_specs=[row, row, gate, pl.BlockSpec((1, D_MODEL), lambda i: (0, 0))],
        out_specs=pl.BlockSpec((ROW_BLK, D_MODEL), lambda i: (i, 0)),
        out_shape=jax.ShapeDtypeStruct((rows, D_MODEL), F32),
        compiler_params=_cparams("arbitrary"),
        name="final_norm",
    )(x, y_tok, mod, g.reshape(1, D_MODEL))


def kernel(x_prompt, x_sample, cache_attn_k, cache_attn_v, state_ssm_re, state_ssm_im, cache_mla_ckv, cache_mla_kpe,
           state_hgrn, c, c_ctx, w_mod, b_mod, g_mix, g_ffn, g_final, router_w, router_b, moe_w_gate, moe_w_up,
           moe_w_down, ab_w_in, ab_sink, s5_lam_re, s5_lam_im, s5_log_dt, s5_b_re, s5_b_im, s5_c_re, s5_c_im, s5_d,
           s5_w_glu, ab_w_out, cd_w_in, mla_g_q, mla_w_q_up, mla_g_kv, mla_w_kv_up, hg_lower_bounds, hg_g_o, cd_w_out):
    x = (x_prompt.reshape(N_CTX_TOK, D_MODEL), x_sample.reshape(N_LAT_TOK, D_MODEL))
    y_tok = None
    cond = jnp.zeros((MOD_ROWS, D_MODEL), F32).at[0].set(c_ctx).at[1:1 + DEC_BATCH].set(c)
    mod = _modulation(cond, w_mod, b_mod)
    keep = ([], [], [], [], [], [], [])
    for l in range(DEPTH):
        j = l // 2
        if l % 2 == 0:
            if y_tok is not None:
                x = _moe_residual(x, y_tok, mod, l - 1)
                x = (x[:N_CTX_TOK], x[N_CTX_TOK:])
            q, k, v, u = _ab_in(x, mod, g_mix[l], ab_w_in[j], l)
            o_a = _attention_a(q, k, v, cache_attn_k[:, j].reshape(DEC_BATCH, PAST_LEN, A_KV_WIDTH),
                               cache_attn_v[:, j].reshape(DEC_BATCH, PAST_LEN, A_KV_WIDTH), ab_sink[j])
            y, fin_re, fin_im = _s5_scan(u, state_ssm_re[:, j], state_ssm_im[:, j], s5_lam_re[j], s5_lam_im[j],
                                         s5_log_dt[j], s5_b_re[j], s5_b_im[j], s5_c_re[j], s5_c_im[j])
            x = _ab_out(x, o_a, y, u, s5_d[j], s5_w_glu[j], ab_w_out[j], mod, l)
            keep[0].append(k[:N_CTX_TOK].reshape(BATCH, SEQ, A_KV_HEADS, HEAD_DIM))
            keep[1].append(v[:N_CTX_TOK].reshape(BATCH, SEQ, A_KV_HEADS, HEAD_DIM))
            keep[2].append(fin_re)
            keep[3].append(fin_im)
        else:
            x, qf, ckv, kpe, kf, vm, hq, hff, hfb, hi, hg = _cd_in(
                x, y_tok, mod, g_mix[l], cd_w_in[j], mla_g_q[j], mla_w_q_up[j], mla_g_kv[j], mla_w_kv_up[j], l)
            kcf, vc = _mla_cache_kv(cache_mla_ckv[:, j].reshape(DEC_BATCH * PAST_LEN, MLA_KV_RANK),
                                    cache_mla_kpe[:, j].reshape(DEC_BATCH * PAST_LEN, MLA_ROPE), mla_w_kv_up[j])
            o_c = _attention_mla(qf, kf, vm, kcf, vc)
            s0_ctx = jnp.zeros((BATCH, 2, 2, HG_VAL, HG_HALF), F32)
            of_c, ob_c, s_fin = _hgrn_scan(hq, hff, hfb, hi, hg_lower_bounds, s0_ctx, l, 0, BATCH, SEQ)
            of_l, ob_l, _ = _hgrn_scan(hq, hff, hfb, hi, hg_lower_bounds, _hg_state_to_blocks(state_hgrn[:, j]), l,
                                       N_CTX_TOK, DEC_BATCH, DEC_SEQ)
            x = _cd_out(x, o_c, (of_c, of_l), (ob_c, ob_l), hg, hg_g_o[j], cd_w_out[j], mod, l)
            keep[4].append(ckv[:N_CTX_TOK].reshape(BATCH, SEQ, MLA_KV_RANK))
            keep[5].append(kpe[:N_CTX_TOK, :MLA_ROPE].reshape(BATCH, SEQ, MLA_ROPE))
            keep[6].append(_hg_state_from_blocks(s_fin))
        y_tok = _moe(x, mod, g_ffn[l], router_w, router_b, moe_w_gate, moe_w_up, moe_w_down, l)
    y_ctx = _final_norm(x, y_tok, mod, DEPTH - 1, g_final, 0, N_CTX_TOK)
    y_lat = _final_norm(x, y_tok, mod, DEPTH - 1, g_final, N_CTX_TOK, N_LAT_TOK)
    return (y_ctx.reshape(BATCH, SEQ, D_MODEL), y_lat.reshape(DEC_BATCH, DEC_SEQ, D_MODEL),
            jnp.stack(keep[0], 1), jnp.stack(keep[1], 1), jnp.stack(keep[2], 1), jnp.stack(keep[3], 1),
            jnp.stack(keep[4], 1), jnp.stack(keep[5], 1), jnp.stack(keep[6], 1))
```

```python
import functools
import math

import numpy as np
import jax
import jax.numpy as jnp
from jax import lax
from jax.experimental import pallas as pl
from jax.experimental.pallas import tpu as pltpu
from jax.experimental.pallas import tpu_sc as plsc

F32 = jnp.float32
BF16 = jnp.bfloat16

D_MODEL = 1024
BATCH = 16
SEQ = 256
DEPTH = 2
DEC_BATCH = 4
DEC_SEQ = 1024
PAST_LEN = 512
GRID_W = 64
N_MOD = 6
EPS = 1e-6
NEG_INF = -1e30
ROPE_BASE = 10000.0
HEAD_DIM = 64
A_HEADS = 8
A_KV_HEADS = 2
A_WINDOW = 128
A_WIDTH = A_HEADS * HEAD_DIM
A_KV_WIDTH = A_KV_HEADS * HEAD_DIM
S5_WIDTH = D_MODEL // 2
S5_GROUP = 16
S5_GROUPS = S5_WIDTH // S5_GROUP
S5_STATE = 64
MLA_HEADS = 8
MLA_Q_RANK = D_MODEL // 4
MLA_KV_RANK = D_MODEL // 8
MLA_NOPE = 64
MLA_ROPE = 32
MLA_V = 64
MLA_WIDTH = MLA_HEADS * MLA_V
HG_HEADS = 8
HG_KEY = 64
HG_VAL = 64
HG_KD = HG_HEADS * HG_KEY
HG_WIDTH = HG_HEADS * HG_VAL
HG_CHUNK = 16
N_EXPERTS = 16
N_GROUPS = 4
EXPERTS_PER_GROUP = N_EXPERTS // N_GROUPS
D_FF = D_MODEL // 2

N_CTX_TOK = BATCH * SEQ
N_LAT_TOK = DEC_BATCH * DEC_SEQ
N_TOK = N_CTX_TOK + N_LAT_TOK
ROW_BLK = 512
MLA_QBLK = 256
N_ROW_BLK = N_TOK // ROW_BLK
N_CTX_BLK = N_CTX_TOK // ROW_BLK
LAT_BLK_PER_SEQ = DEC_SEQ // ROW_BLK
MOD_ROWS = 8
VMEM_LIMIT = 56 * 1024 * 1024


def _cparams(*sem):
    return pltpu.CompilerParams(dimension_semantics=sem, vmem_limit_bytes=VMEM_LIMIT)


def _mod_group(i):
    return jnp.where(i < N_CTX_BLK, 0, 1 + (i - N_CTX_BLK) // LAT_BLK_PER_SEQ)


def _mod_spec(layer, which):
    return pl.BlockSpec((1, 1, D_MODEL), lambda i: ((layer * MOD_ROWS + _mod_group(i)) * N_MOD + which, 0, 0))


def _rope_blk(i):
    return jnp.where(i < N_CTX_BLK, 0, 1 + (i - N_CTX_BLK) % LAT_BLK_PER_SEQ)


def _rope_tables(rot_dim, only=None):
    n_freq = rot_dim // 4
    t = np.arange(DEC_SEQ)
    rows = (t // GRID_W).astype(np.float32)
    cols = (t % GRID_W).astype(np.float32)
    inv = (np.float32(ROPE_BASE) ** (-np.arange(n_freq, dtype=np.float32) / np.float32(n_freq))).astype(np.float32)
    ang_r = rows[:, None] * inv[None, :]
    ang_c = cols[:, None] * inv[None, :]
    ang = np.concatenate([ang_r, ang_r, ang_c, ang_c], axis=-1).astype(np.float32)
    reps = 128 // rot_dim
    cos = np.tile(np.cos(ang), (1, reps)).astype(np.float32)
    sin = np.tile(np.sin(ang), (1, reps)).astype(np.float32)
    lane = np.arange(128)
    first = (lane % (2 * n_freq)) < n_freq
    sin_a = np.where(first[None, :], -sin, 0.0).astype(np.float32)
    sin_b = np.where(first[None, :], 0.0, sin).astype(np.float32)
    if only is not None:
        keep = ((lane >= only[0]) & (lane < only[1]))[None, :]
        cos, sin_a, sin_b = np.where(keep, cos, 1.0), np.where(keep, sin_a, 0.0), np.where(keep, sin_b, 0.0)
        cos, sin_a, sin_b = cos.astype(np.float32), sin_a.astype(np.float32), sin_b.astype(np.float32)
    ident = np.zeros((ROW_BLK, 128), np.float32)
    cos = np.concatenate([ident + 1.0, cos], axis=0)
    sin_a = np.concatenate([ident, sin_a], axis=0)
    sin_b = np.concatenate([ident, sin_b], axis=0)
    return jnp.asarray(cos), jnp.asarray(sin_a), jnp.asarray(sin_b)


def _apply_rope(x, cos, sin_a, sin_b, quarter):
    outs = []
    for j in range(x.shape[1] // 128):
        xt = x[:, 128 * j:128 * (j + 1)]
        up = pltpu.roll(xt, 128 - quarter, axis=1)
        dn = pltpu.roll(xt, quarter, axis=1)
        outs.append(xt * cos + up * sin_a + dn * sin_b)
    return outs[0] if len(outs) == 1 else jnp.concatenate(outs, axis=1)


def _rms(x):
    return x * lax.rsqrt(jnp.mean(x * x, axis=-1, keepdims=True) + EPS)


def _norm_mod(x, g, sc, sh):
    return _rms(x) * g * (1.0 + sc) + sh


def _dot(a, b):
    return jnp.dot(a.astype(BF16), b.astype(BF16), preferred_element_type=F32)


def _dot_nt(a, b):
    return lax.dot_general(a.astype(BF16), b.astype(BF16), (((1,), (1,)), ((), ())), preferred_element_type=F32)


def _mod_kernel(cond_ref, w_ref, b_ref, o_ref):
    c = cond_ref[...]
    s = c * jax.nn.sigmoid(c)
    o_ref[0] = _dot(s, w_ref[0]) + b_ref[0]


def _modulation(cond, w_mod, b_mod):
    nb = 1024
    out = pl.pallas_call(
        _mod_kernel,
        grid=(DEPTH, N_MOD * D_MODEL // nb),
        in_specs=[pl.BlockSpec((MOD_ROWS, D_MODEL), lambda l, n: (0, 0)),
                  pl.BlockSpec((1, D_MODEL, nb), lambda l, n: (l, 0, n)),
                  pl.BlockSpec((1, 1, nb), lambda l, n: (l, 0, n))],
        out_specs=pl.BlockSpec((1, MOD_ROWS, nb), lambda l, n: (l, 0, n)),
        out_shape=jax.ShapeDtypeStruct((DEPTH, MOD_ROWS, N_MOD * D_MODEL), F32),
        compiler_params=_cparams("arbitrary", "arbitrary"),
        name="modulation",
    )(cond, w_mod, b_mod.reshape(DEPTH, 1, N_MOD * D_MODEL))
    return out.reshape(DEPTH * MOD_ROWS * N_MOD, 1, D_MODEL)


def _ab_in_kernel(xc_ref, xl_ref, g_ref, sc_ref, sh_ref, w_ref, cos_ref, sa_ref, sb_ref,
                  q_ref, k_ref, v_ref, u_ref, wq_s, wk_s, wv_s, wu_s):
    @pl.when(pl.program_id(0) == 0)
    def _():
        wq_s[...] = w_ref[:, 0:A_WIDTH].astype(BF16)
        wk_s[...] = w_ref[:, A_WIDTH:A_WIDTH + A_KV_WIDTH].astype(BF16)
        wv_s[...] = w_ref[:, A_WIDTH + A_KV_WIDTH:A_WIDTH + 2 * A_KV_WIDTH].astype(BF16)
        wu_s[...] = w_ref[:, A_WIDTH + 2 * A_KV_WIDTH:].astype(BF16)

    h = _norm_mod(_ctx_or_lat(xc_ref, xl_ref), g_ref[...], sc_ref[0], sh_ref[0]).astype(BF16)
    cos, sa, sb = cos_ref[...], sa_ref[...], sb_ref[...]
    q = jnp.dot(h, wq_s[...], preferred_element_type=F32)
    q_ref[...] = _apply_rope(q, cos, sa, sb, HEAD_DIM // 4).astype(q_ref.dtype)
    k = jnp.dot(h, wk_s[...], preferred_element_type=F32)
    k_ref[...] = _apply_rope(k, cos, sa, sb, HEAD_DIM // 4)
    v_ref[...] = jnp.dot(h, wv_s[...], preferred_element_type=F32)
    u_ref[...] = jnp.dot(h, wu_s[...], preferred_element_type=F32)


def _ab_in(x, mod, g, w, layer):
    cos, sa, sb = _rope_tables(HEAD_DIM)
    d_in = w.shape[1]
    row = lambda n: pl.BlockSpec((ROW_BLK, n), lambda i: (i, 0))
    rope = pl.BlockSpec((ROW_BLK, 128), lambda i: (_rope_blk(i), 0))
    return pl.pallas_call(
        _ab_in_kernel,
        grid=(N_ROW_BLK,),
        in_specs=_split_specs(D_MODEL) + [pl.BlockSpec((1, D_MODEL), lambda i: (0, 0)),
                  _mod_spec(layer, 1), _mod_spec(layer, 0),
                  pl.BlockSpec((D_MODEL, d_in), lambda i: (0, 0), pipeline_mode=pl.Buffered(1)), rope, rope, rope],
        out_specs=[row(A_WIDTH), row(A_KV_WIDTH), row(A_KV_WIDTH), row(S5_WIDTH)],
        out_shape=[jax.ShapeDtypeStruct((N_TOK, A_WIDTH), BF16), jax.ShapeDtypeStruct((N_TOK, A_KV_WIDTH), F32),
                   jax.ShapeDtypeStruct((N_TOK, A_KV_WIDTH), F32), jax.ShapeDtypeStruct((N_TOK, S5_WIDTH), F32)],
        scratch_shapes=[pltpu.VMEM((D_MODEL, A_WIDTH), BF16), pltpu.VMEM((D_MODEL, A_KV_WIDTH), BF16),
                        pltpu.VMEM((D_MODEL, A_KV_WIDTH), BF16), pltpu.VMEM((D_MODEL, S5_WIDTH), BF16)],
        compiler_params=_cparams("arbitrary"),
        name="ab_in",
    )(x[0], x[1], g.reshape(1, D_MODEL), mod, mod, w, cos, sa, sb)


def _softmax_pv(s_list, v_list, sink):
    m = functools.reduce(jnp.maximum, [jnp.max(s, axis=-1, keepdims=True) for s in s_list])
    if sink is not None:
        m = jnp.maximum(m, sink)
    ps = [jnp.exp(s - m) for s in s_list]
    l = functools.reduce(jnp.add, [jnp.sum(p, axis=-1, keepdims=True) for p in ps])
    if sink is not None:
        l = l + jnp.exp(sink - m)
    o = functools.reduce(jnp.add, [_dot(p, v) for p, v in zip(ps, v_list)])
    return o / l


def _gqa_with_sink(sink_ref, q, k_of, v_of, bias, o_ref):
    g = A_HEADS // A_KV_HEADS
    r = q.shape[0]
    outs = [None] * A_HEADS
    for kh in range(A_KV_HEADS):
        heads = range(g * kh, g * (kh + 1))
        qg = jnp.concatenate([q[:, HEAD_DIM * h:HEAD_DIM * (h + 1)] for h in heads], axis=0)
        sink = jnp.concatenate([jnp.full((r, 1), sink_ref[h], F32) for h in heads], axis=0)
        s = _dot_nt(qg, k_of(kh))
        if bias is not None:
            s = s + jnp.concatenate([bias] * g, axis=0)
        m = jnp.maximum(jnp.max(s, axis=-1, keepdims=True), sink)
        p = jnp.exp(s - m)
        l = jnp.sum(p, axis=-1, keepdims=True) + jnp.exp(sink - m)
        o = _dot(p, v_of(kh)) / l
        for n, h in enumerate(heads):
            outs[h] = o[r * n:r * (n + 1)]
    o_ref[...] = jnp.concatenate(outs, axis=1).astype(o_ref.dtype)


def _attn_ctx_kernel(sink_ref, q_ref, k_ref, v_ref, o_ref):
    scale = HEAD_DIM ** -0.5
    g = A_HEADS // A_KV_HEADS
    outs = []
    for h in range(A_HEADS):
        kh = h // g
        q = q_ref[:, HEAD_DIM * h:HEAD_DIM * (h + 1)]
        k = k_ref[:, HEAD_DIM * kh:HEAD_DIM * (kh + 1)]
        v = v_ref[:, HEAD_DIM * kh:HEAD_DIM * (kh + 1)]
        s = _dot_nt(q, k) * scale
        outs.append(_softmax_pv([s], [v], sink_ref[h]))
    o_ref[...] = jnp.concatenate(outs, axis=1).astype(o_ref.dtype)


def _attn_lat_kernel(sink_ref, q_ref, kp_ref, kc_ref, kn_ref, vp_ref, vc_ref, vn_ref, kx_ref, vx_ref, o_ref):
    n = pl.program_id(1)
    nb = DEC_SEQ // A_WINDOW
    i = lax.broadcasted_iota(jnp.int32, (A_WINDOW, A_WINDOW), 0)
    j = lax.broadcasted_iota(jnp.int32, (A_WINDOW, A_WINDOW), 1)
    zero = jnp.zeros((A_WINDOW, A_WINDOW), F32)
    bias = jnp.concatenate([jnp.where((j >= i) & (n > 0), 0.0, NEG_INF), zero,
                            jnp.where((j <= i) & (n < nb - 1), 0.0, NEG_INF),
                            jnp.zeros((A_WINDOW, PAST_LEN), F32)], axis=1)

    def rows(p_ref, c_ref, n_ref, x_ref):
        def of(kh):
            sl = slice(HEAD_DIM * kh, HEAD_DIM * (kh + 1))
            return jnp.concatenate([p_ref[:, sl], c_ref[:, sl], n_ref[:, sl], x_ref[0, :, sl]], axis=0)
        return of

    _gqa_with_sink(sink_ref, q_ref[...] * HEAD_DIM ** -0.5, rows(kp_ref, kc_ref, kn_ref, kx_ref),
                   rows(vp_ref, vc_ref, vn_ref, vx_ref), bias, o_ref)


def _attention_a(q, k, v, cache_k, cache_v, sink):
    smem = pl.BlockSpec(memory_space=pltpu.SMEM)
    o = pl.pallas_call(
        _attn_ctx_kernel,
        grid=(BATCH,),
        in_specs=[smem, pl.BlockSpec((SEQ, A_WIDTH), lambda b: (b, 0)),
                  pl.BlockSpec((SEQ, A_KV_WIDTH), lambda b: (b, 0)), pl.BlockSpec((SEQ, A_KV_WIDTH), lambda b: (b, 0))],
        out_specs=pl.BlockSpec((SEQ, A_WIDTH), lambda b: (b, 0)),
        out_shape=jax.ShapeDtypeStruct((N_CTX_TOK, A_WIDTH), BF16),
        compiler_params=_cparams("arbitrary"),
        name="attn_a_ctx",
    )(sink, q, k, v)
    nb = DEC_SEQ // A_WINDOW
    base = N_CTX_TOK // A_WINDOW
    cur = lambda b, n: (base + b * nb + n, 0)
    prev = lambda b, n: (base + b * nb + jnp.maximum(n - 1, 0), 0)
    nxt = lambda b, n: (base + b * nb + jnp.minimum(n + 1, nb - 1), 0)
    kv = lambda f: pl.BlockSpec((A_WINDOW, A_KV_WIDTH), f)
    cache = pl.BlockSpec((1, PAST_LEN, A_KV_WIDTH), lambda b, n: (b, 0, 0))
    o_lat = pl.pallas_call(
        _attn_lat_kernel,
        grid=(DEC_BATCH, nb),
        in_specs=[smem, pl.BlockSpec((A_WINDOW, A_WIDTH), cur), kv(prev), kv(cur), kv(nxt), kv(prev), kv(cur), kv(nxt),
                  cache, cache],
        out_specs=pl.BlockSpec((A_WINDOW, A_WIDTH), lambda b, n: (b * nb + n, 0)),
        out_shape=jax.ShapeDtypeStruct((N_LAT_TOK, A_WIDTH), BF16),
        compiler_params=_cparams("arbitrary", "arbitrary"),
        name="attn_a_lat",
    )(sink, q, k, k, k, v, v, v, cache_k, cache_v)
    return o, o_lat


S5_CHUNK = 16
S5_OCT = 128 // S5_GROUP
S5_NOCT = S5_GROUPS // S5_OCT
S5_K = S5_CHUNK * 128
S5_PART = S5_OCT * S5_STATE
S5_SW = 4 * S5_PART
S5_ROWS_CTX = BATCH * SEQ // S5_CHUNK
S5_ROWS_LAT = DEC_BATCH * DEC_SEQ // S5_CHUNK
S5_ROWS = S5_ROWS_CTX + S5_ROWS_LAT
S5_NB = 4
S5_TP = 4
HI = lax.Precision.HIGHEST


def _s5_disc_kernel(lr_ref, li_ref, ldt_ref, ar_ref, ai_ref, zr_ref, zi_ref):
    lr, li = lr_ref[...], li_ref[...]
    dt = jnp.exp(ldt_ref[...])
    mag = jnp.exp(lr * dt)
    ar, ai = mag * jnp.cos(li * dt), mag * jnp.sin(li * dt)
    den = lr * lr + li * li
    ar_ref[...] = ar
    ai_ref[...] = ai
    zr_ref[...] = ((ar - 1.0) * lr + ai * li) / den
    zi_ref[...] = (ai * lr - (ar - 1.0) * li) / den


def _cmul(xr, xi, yr, yi):
    return xr * yr - xi * yi, xr * yi + xi * yr


def _dot_nt_hi(a, b):
    return lax.dot_general(a, b, (((1,), (1,)), ((), ())), precision=HI, preferred_element_type=F32)


def _s5_prep_kernel(ar_ref, ai_ref, zr_ref, zi_ref, btr_ref, bti_ref, ctr_ref, cti_ref,
                    m_ref, win_ref, wout_ref, a16_ref, pw_s, w_s, k_s):
    npw = S5_CHUNK + 1
    blk = lambda j: pl.ds(pl.multiple_of(j * 128, 128), 128)

    @pl.when(pl.program_id(1) == 0)
    def _():
        kd = []
        for d in range(2):
            ar, ai = ar_ref[0, d], ai_ref[0, d]
            pr, pi = jnp.ones_like(ar), jnp.zeros_like(ar)
            bbr, bbi = _cmul(zr_ref[0, d], zi_ref[0, d], btr_ref[0, d], bti_ref[0, d])
            for j in range(npw):
                pw_s[d, 0, j:j + 1, :] = pr
                pw_s[d, 1, j:j + 1, :] = pi
                if j < S5_CHUNK:
                    wr, wi = _cmul(pr, pi, bbr, bbi)
                    w_s[d, 0, 128 * j:128 * (j + 1), :] = wr.astype(BF16)
                    w_s[d, 1, 128 * j:128 * (j + 1), :] = wi.astype(BF16)
                pr, pi = _cmul(pr, pi, ar, ai)
            kd.append(_dot_nt(w_s[d, 0], ctr_ref[0, d]) - _dot_nt(w_s[d, 1], cti_ref[0, d]))
        for jj in range(2 * S5_CHUNK - 1):
            j = jj - (S5_CHUNK - 1)
            if j > 0:
                k = kd[0][128 * j:128 * (j + 1)]
            elif j < 0:
                k = kd[1][128 * -j:128 * (1 - j)]
            else:
                k = kd[0][0:128] + kd[1][0:128]
            k_s[128 * jj:128 * (jj + 1), :] = k
        a16_ref[0] = jnp.concatenate([pw_s[0, 0, S5_CHUNK:npw, :], pw_s[1, 0, S5_CHUNK:npw, :],
                                      pw_s[0, 1, S5_CHUNK:npw, :], pw_s[1, 1, S5_CHUNK:npw, :]], axis=1)

    def power(d, e):
        return pw_s[d, 0, pl.ds(e, 1), :], pw_s[d, 1, pl.ds(e, 1), :]

    for tt in range(S5_TP):
        t = pl.program_id(1) * S5_TP + tt
        rows = slice(128 * tt, 128 * (tt + 1))
        for tp in range(S5_CHUNK):
            m_ref[0, rows, 128 * tp:128 * (tp + 1)] = k_s[blk(S5_CHUNK - 1 + tp - t), :].astype(BF16)
        for d in range(2):
            j = (S5_CHUNK - 1 - t) if d == 0 else t
            win_ref[0, rows, S5_PART * d:S5_PART * (d + 1)] = w_s[d, 0, blk(j), :]
            win_ref[0, rows, S5_PART * (2 + d):S5_PART * (3 + d)] = w_s[d, 1, blk(j), :]
            er, ei = _cmul(*power(d, (t + 1) if d == 0 else (S5_CHUNK - t)), ctr_ref[0, d], cti_ref[0, d])
            wout_ref[0, rows, S5_PART * d:S5_PART * (d + 1)] = er.astype(BF16)
            wout_ref[0, rows, S5_PART * (2 + d):S5_PART * (3 + d)] = (-ei).astype(BF16)


def _s5_main_kernel(u_ref, win_ref, m_ref, wout_ref, a16_ref, h0_ref, y_ref, hfin_ref, uo_s, x_s, hs_s):
    s = pl.program_id(1)
    nq = S5_PART // 128

    @pl.when(s == 0)
    def _():
        for t in range(S5_CHUNK):
            uo_s[:, 128 * t:128 * (t + 1)] = u_ref[pl.ds(t, S5_ROWS, stride=S5_CHUNK), :].astype(BF16)

    @pl.when(s < S5_NB)
    def _():
        x = jnp.dot(uo_s[...], win_ref[0], preferred_element_type=F32)
        for q in range(nq):
            x_s[s * nq + q] = x[:, 128 * q:128 * (q + 1)]

    @pl.when(s == S5_NB - 1)
    def _():
        def run(row0, nb, nc, h):
            for c in range(nc):
                for d in range(2):
                    cc = c if d == 0 else nc - 1 - c
                    rows = pl.ds(row0 + cc, nb, stride=nc)
                    for q in range(nq):
                        kr, ki = d * nq + q, (2 + d) * nq + q
                        hr, hi = h[d][0][q], h[d][1][q]
                        hs_s[kr, rows, :] = hr
                        hs_s[ki, rows, :] = hi
                        ar = a16_ref[0, :, 128 * kr:128 * (kr + 1)]
                        ai = a16_ref[0, :, 128 * ki:128 * (ki + 1)]
                        h[d][0][q] = ar * hr - ai * hi + x_s[kr, rows, :]
                        h[d][1][q] = ar * hi + ai * hr + x_s[ki, rows, :]
            return h

        zero = jnp.zeros((BATCH, 128), F32)
        fin = run(0, BATCH, SEQ // S5_CHUNK, [[[zero] * nq, [zero] * nq] for _ in range(2)])
        for d in range(2):
            for ri in range(2):
                for q in range(nq):
                    k = (2 * ri + d) * nq + q
                    hfin_ref[0, :, 128 * k:128 * (k + 1)] = fin[d][ri][q]
        h0 = [[[h0_ref[0, :, 128 * ((2 * ri + d) * nq + q):128 * ((2 * ri + d) * nq + q + 1)] for q in range(nq)]
               for ri in range(2)] for d in range(2)]
        run(S5_ROWS_CTX, DEC_BATCH, DEC_SEQ // S5_CHUNK, h0)

    @pl.when(s >= S5_NB)
    def _():
        hs = jnp.concatenate([hs_s[k] for k in range(4 * nq)], axis=1).astype(BF16)
        y = (jnp.dot(uo_s[...], m_ref[0], preferred_element_type=F32)
             + lax.dot_general(hs, wout_ref[0], (((1,), (1,)), ((), ())), preferred_element_type=F32))
        for q in range(nq):
            t = (s - S5_NB) * nq + q
            y_ref[pl.ds(t, S5_ROWS, stride=S5_CHUNK), :] = y[:, 128 * q:128 * (q + 1)]


def _s5_octets(t, lanes):
    return t.reshape(2, S5_NOCT, 1, S5_OCT * lanes).transpose(1, 0, 2, 3)


def _s5_blockdiag(t):
    a, n = t.shape[2], t.shape[3]
    t = t.reshape(2, S5_NOCT, S5_OCT, a, n)
    bd = jnp.einsum('dogan,gh->dogahn', t, jnp.eye(S5_OCT, dtype=t.dtype))
    return bd.reshape(2, S5_NOCT, S5_OCT * a, S5_OCT * n).transpose(1, 0, 2, 3)


def _s5_state_to_lanes(h_re, h_im):
    b = h_re.shape[0]
    parts = jnp.stack([h_re[:, 0], h_re[:, 1], h_im[:, 0], h_im[:, 1]], axis=1)
    parts = parts.reshape(b, 4, S5_NOCT, S5_PART).transpose(2, 0, 1, 3)
    return parts.reshape(S5_NOCT, b, S5_SW)


def _s5_state_from_lanes(h):
    b = h.shape[1]
    parts = h.reshape(S5_NOCT, b, 4, S5_OCT, S5_STATE).transpose(1, 2, 0, 3, 4).reshape(b, 4, S5_GROUPS, S5_STATE)
    return parts[:, 0:2], parts[:, 2:4]


def _s5_scan(u, h0_re, h0_im, lam_re, lam_im, log_dt, b_re, b_im, c_re, c_im):
    ng, n = S5_GROUPS, S5_STATE
    rows = 2 * ng
    disc = pl.pallas_call(
        _s5_disc_kernel,
        out_shape=[jax.ShapeDtypeStruct((rows, n), F32)] * 4,
        name="s5_disc",
    )(lam_re.reshape(rows, n), lam_im.reshape(rows, n), log_dt.reshape(rows, 1))
    ar, ai, zr, zi = [_s5_octets(t.reshape(2, ng, n), n) for t in disc]
    bt = lambda t: _s5_blockdiag(t.transpose(0, 1, 3, 2))
    vec = pl.BlockSpec((1, 2, 1, S5_PART), lambda o, t: (o, 0, 0, 0))
    mat = pl.BlockSpec((1, 2, 128, S5_PART), lambda o, t: (o, 0, 0, 0))
    rowblk = lambda w: pl.BlockSpec((1, 128 * S5_TP, w), lambda o, t: (o, t, 0))
    m, win, wout, a16 = pl.pallas_call(
        _s5_prep_kernel,
        grid=(S5_NOCT, S5_CHUNK // S5_TP),
        in_specs=[vec, vec, vec, vec, mat, mat, mat, mat],
        out_specs=[rowblk(S5_K), rowblk(S5_SW), rowblk(S5_SW), pl.BlockSpec((1, 1, S5_SW), lambda o, t: (o, 0, 0))],
        out_shape=[jax.ShapeDtypeStruct((S5_NOCT, S5_K, S5_K), BF16), jax.ShapeDtypeStruct((S5_NOCT, S5_K, S5_SW), BF16),
                   jax.ShapeDtypeStruct((S5_NOCT, S5_K, S5_SW), BF16), jax.ShapeDtypeStruct((S5_NOCT, 1, S5_SW), F32)],
        scratch_shapes=[pltpu.VMEM((2, 2, 24, S5_PART), F32), pltpu.VMEM((2, 2, S5_K, S5_PART), BF16),
                        pltpu.VMEM(((2 * S5_CHUNK - 1) * 128, 128), F32)],
        compiler_params=_cparams("arbitrary", "arbitrary"),
        name="s5_prep",
    )(ar, ai, zr, zi, bt(b_re), bt(b_im), _s5_blockdiag(c_re), _s5_blockdiag(c_im))
    nb = S5_NB
    y, hfin = pl.pallas_call(
        _s5_main_kernel,
        grid=(S5_NOCT, 2 * nb),
        in_specs=[pl.BlockSpec((N_TOK, 128), lambda o, s: (0, o)),
                  pl.BlockSpec((1, S5_K, S5_PART), lambda o, s: (o, 0, jnp.minimum(s, nb - 1))),
                  pl.BlockSpec((1, S5_K, S5_PART), lambda o, s: (o, 0, jnp.maximum(s - nb, 0))),
                  pl.BlockSpec((1, S5_PART, S5_SW), lambda o, s: (o, jnp.maximum(s - nb, 0), 0)),
                  pl.BlockSpec((1, 1, S5_SW), lambda o, s: (o, 0, 0)),
                  pl.BlockSpec((1, DEC_BATCH, S5_SW), lambda o, s: (o, 0, 0))],
        out_specs=[pl.BlockSpec((N_TOK, 128), lambda o, s: (0, o)),
                   pl.BlockSpec((1, BATCH, S5_SW), lambda o, s: (o, 0, 0))],
        out_shape=[jax.ShapeDtypeStruct((N_TOK, S5_WIDTH), F32), jax.ShapeDtypeStruct((S5_NOCT, BATCH, S5_SW), F32)],
        scratch_shapes=[pltpu.VMEM((S5_ROWS, S5_K), BF16), pltpu.VMEM((S5_SW // 128, S5_ROWS, 128), F32),
                        pltpu.VMEM((S5_SW // 128, S5_ROWS, 128), F32)],
        compiler_params=_cparams("arbitrary", "arbitrary"),
        name="s5_main",
    )(u, win, m, wout, a16, _s5_state_to_lanes(h0_re, h0_im))
    fin_re, fin_im = _s5_state_from_lanes(hfin)
    return y, fin_re, fin_im


def _ctx_or_lat(ctx_ref, lat_ref):
    return jnp.where(pl.program_id(0) < N_CTX_BLK, ctx_ref[...], lat_ref[...])


def _split_specs(width):
    return [pl.BlockSpec((ROW_BLK, width), lambda i: (jnp.minimum(i, N_CTX_BLK - 1), 0)),
            pl.BlockSpec((ROW_BLK, width), lambda i: (jnp.maximum(i - N_CTX_BLK, 0), 0))]


def _ab_out_kernel(xc_ref, xl_ref, oac_ref, oal_ref, y_ref, u_ref, d_ref, wglu_ref, wout_ref, gt_ref, o_ref,
                   wglu_s, wa_s, wb_s):
    @pl.when(pl.program_id(0) == 0)
    def _():
        wglu_s[...] = wglu_ref[...].astype(BF16)
        wa_s[...] = wout_ref[0:A_WIDTH, :].astype(BF16)
        wb_s[...] = wout_ref[A_WIDTH:, :].astype(BF16)

    g = jax.nn.gelu(y_ref[...] + d_ref[...] * u_ref[...])
    ob = g * jax.nn.sigmoid(jnp.dot(g.astype(BF16), wglu_s[...], preferred_element_type=F32))
    out = (jnp.dot(_ctx_or_lat(oac_ref, oal_ref).astype(BF16), wa_s[...], preferred_element_type=F32)
           + jnp.dot(ob.astype(BF16), wb_s[...], preferred_element_type=F32))
    o_ref[...] = _ctx_or_lat(xc_ref, xl_ref) + gt_ref[0] * out


def _ab_out(x, oa, y, u, d_skip, w_glu, w_out, mod, layer):
    row = lambda n: pl.BlockSpec((ROW_BLK, n), lambda i: (i, 0))
    full = lambda a, b: pl.BlockSpec((a, b), lambda i: (0, 0), pipeline_mode=pl.Buffered(1))
    return pl.pallas_call(
        _ab_out_kernel,
        grid=(N_ROW_BLK,),
        in_specs=_split_specs(D_MODEL) + _split_specs(A_WIDTH) + [row(S5_WIDTH), row(S5_WIDTH), full(1, S5_WIDTH),
                  full(S5_WIDTH, S5_WIDTH), full(A_WIDTH + S5_WIDTH, D_MODEL), _mod_spec(layer, 2)],
        out_specs=row(D_MODEL),
        out_shape=jax.ShapeDtypeStruct((N_TOK, D_MODEL), F32),
        scratch_shapes=[pltpu.VMEM((S5_WIDTH, S5_WIDTH), BF16), pltpu.VMEM((A_WIDTH, D_MODEL), BF16),
                        pltpu.VMEM((S5_WIDTH, D_MODEL), BF16)],
        compiler_params=_cparams("arbitrary"),
        name="ab_out",
    )(x[0], x[1], oa[0], oa[1], y, u, d_skip.reshape(1, S5_WIDTH), w_glu, w_out, mod)


CD_HG0 = MLA_Q_RANK + MLA_KV_RANK + MLA_ROPE
KPE_LANES = 128


MLA_TILE = 128
MLA_QK = MLA_HEADS * MLA_TILE


def _mla_head_tiles(w, a):
    k, n = w.shape
    w = w.reshape(k, MLA_HEADS, n // MLA_HEADS)
    tiles = jnp.pad(w[:, :, :a], ((0, 0), (0, 0), (0, MLA_TILE - a)))
    return jnp.concatenate([tiles.reshape(k, -1), w[:, :, a:].reshape(k, -1)], axis=1)


def _mla_key_tiles(k_tiles, kpe):
    lane = lax.broadcasted_iota(jnp.int32, (1, MLA_TILE), 1)
    pe = jnp.where((lane >= MLA_NOPE) & (lane < MLA_NOPE + MLA_ROPE), kpe, 0.0)
    return k_tiles + jnp.concatenate([pe] * MLA_HEADS, axis=1)


def _cd_in_kernel(x_ref, y_ref, gp_ref, g_ref, sc_ref, sh_ref, w_ref, gq_ref, wqu_ref, gkv_ref, wkvu_ref,
                  cos_ref, sa_ref, sb_ref, cq_ref, sq_ref, tq_ref,
                  x1_ref, qf_ref, ckv_ref, kpe_ref, kf_ref, vm_ref, hq_ref, hff_ref, hfb_ref, hi_ref, hg_ref):
    dot = lambda a, b: jnp.dot(a, b, preferred_element_type=F32)
    c0, c1, c2 = MLA_Q_RANK, MLA_Q_RANK + MLA_KV_RANK, MLA_Q_RANK + MLA_KV_RANK + KPE_LANES
    x = x_ref[...] + gp_ref[0] * y_ref[...]
    x1_ref[...] = x
    h = _norm_mod(x, g_ref[...], sc_ref[0], sh_ref[0]).astype(BF16)
    cq = _rms(dot(h, w_ref[:, 0:c0])) * gq_ref[...]
    qq = dot(cq.astype(BF16), wqu_ref[...])
    qf_ref[...] = _apply_rope(qq, cq_ref[...], sq_ref[...], tq_ref[...], MLA_ROPE // 4).astype(qf_ref.dtype)
    ckv = _rms(dot(h, w_ref[:, c0:c1])) * gkv_ref[...]
    ckv_ref[...] = ckv
    kv = dot(ckv.astype(BF16), wkvu_ref[...])
    kpe = _apply_rope(dot(h, w_ref[:, c1:c2]), cos_ref[...], sa_ref[...], sb_ref[...], MLA_ROPE // 4)
    kpe_ref[...] = kpe
    kf_ref[...] = _mla_key_tiles(kv[:, 0:MLA_QK], kpe).astype(kf_ref.dtype)
    vm_ref[...] = kv[:, MLA_QK:].astype(vm_ref.dtype)
    for n, ref in enumerate((hq_ref, hff_ref, hfb_ref, hi_ref, hg_ref)):
        ref[...] = dot(h, w_ref[:, c2 + HG_KD * n:c2 + HG_KD * (n + 1)])


def _cd_in(x, y_prev, mod, g, w, g_q, w_q_up, g_kv, w_kv_up, layer):
    cos, sa, sb = _rope_tables(MLA_ROPE)
    cos_q, sa_q, sb_q = _rope_tables(MLA_ROPE, only=(MLA_NOPE, MLA_NOPE + MLA_ROPE))
    c1 = MLA_Q_RANK + MLA_KV_RANK
    w = jnp.concatenate([w[:, :c1], jnp.tile(w[:, c1:CD_HG0], (1, KPE_LANES // MLA_ROPE)), w[:, CD_HG0:]],
                        axis=1).astype(BF16)
    d_in = w.shape[1]
    row = lambda n: pl.BlockSpec((ROW_BLK, n), lambda i: (i, 0))
    full = lambda a, b: pl.BlockSpec((a, b), lambda i: (0, 0), pipeline_mode=pl.Buffered(1))
    rope = pl.BlockSpec((ROW_BLK, 128), lambda i: (_rope_blk(i), 0))
    outs = ([(D_MODEL, F32), (MLA_QK, BF16), (MLA_KV_RANK, F32), (KPE_LANES, F32), (MLA_QK, BF16), (MLA_WIDTH, BF16)]
            + [(HG_KD, F32)] * 5)
    nkv = MLA_QK + MLA_WIDTH
    return pl.pallas_call(
        _cd_in_kernel,
        grid=(N_ROW_BLK,),
        in_specs=[row(D_MODEL), row(D_MODEL), _mod_spec(layer - 1, 5),
                  full(1, D_MODEL), _mod_spec(layer, 1), _mod_spec(layer, 0), full(D_MODEL, d_in),
                  full(1, MLA_Q_RANK), full(MLA_Q_RANK, MLA_QK), full(1, MLA_KV_RANK), full(MLA_KV_RANK, nkv),
                  rope, rope, rope, rope, rope, rope],
        out_specs=[row(n) for n, _ in outs],
        out_shape=[jax.ShapeDtypeStruct((N_TOK, n), dt) for n, dt in outs],
        compiler_params=_cparams("arbitrary"),
        name="cd_in",
    )(x, y_prev, mod, g.reshape(1, D_MODEL), mod, mod, w, g_q.reshape(1, -1),
      _mla_head_tiles(w_q_up, MLA_NOPE + MLA_ROPE).astype(BF16), g_kv.reshape(1, -1),
      _mla_head_tiles(w_kv_up, MLA_NOPE).astype(BF16), cos, sa, sb, cos_q, sa_q, sb_q)


def _mla_cache_kernel(c_ref, kpe_ref, w_ref, kf_ref, v_ref):
    kv = _dot(c_ref[...], w_ref[...])
    kf_ref[...] = _mla_key_tiles(kv[:, 0:MLA_QK], kpe_ref[...]).astype(kf_ref.dtype)
    v_ref[...] = kv[:, MLA_QK:].astype(v_ref.dtype)


def _mla_cache_kv(cckv, ckpe, w_kv_up):
    rows = cckv.shape[0]
    blk = lambda w: pl.BlockSpec((PAST_LEN, w), lambda i: (i, 0))
    return pl.pallas_call(
        _mla_cache_kernel,
        grid=(rows // PAST_LEN,),
        in_specs=[blk(MLA_KV_RANK), blk(KPE_LANES), pl.BlockSpec((MLA_KV_RANK, MLA_QK + MLA_WIDTH), lambda i: (0, 0))],
        out_specs=[blk(MLA_QK), blk(MLA_WIDTH)],
        out_shape=[jax.ShapeDtypeStruct((rows, MLA_QK), BF16), jax.ShapeDtypeStruct((rows, MLA_WIDTH), BF16)],
        compiler_params=_cparams("arbitrary"),
        name="mla_cache_kv",
    )(cckv, jnp.tile(ckpe, (1, KPE_LANES // MLA_ROPE)), _mla_head_tiles(w_kv_up, MLA_NOPE))


def _mla_heads(qf, keys, o_ref):
    scale = (MLA_NOPE + MLA_ROPE) ** -0.5
    outs = []
    for h in range(MLA_HEADS):
        t = slice(MLA_TILE * h, MLA_TILE * (h + 1))
        a = slice(MLA_V * h, MLA_V * (h + 1))
        s_list = [_dot_nt(qf[:, t], kf[:, t]) * scale for kf, _ in keys]
        outs.append(_softmax_pv(s_list, [v[:, a] for _, v in keys], None))
    o_ref[...] = jnp.concatenate(outs, axis=1).astype(o_ref.dtype)


def _mla_ctx_kernel(qf_ref, kf_ref, v_ref, o_ref):
    _mla_heads(qf_ref[...], [(kf_ref[...], v_ref[...])], o_ref)


def _mla_lat_kernel(qf_ref, kf_ref, v_ref, kcf_ref, vc_ref, o_ref):
    _mla_heads(qf_ref[...], [(kcf_ref[...], vc_ref[...]), (kf_ref[...], v_ref[...])], o_ref)


def _attention_mla(qf, kf, vm, kcf, vc):
    blk = lambda w: pl.BlockSpec((SEQ, w), lambda b: (b, 0))
    o = pl.pallas_call(
        _mla_ctx_kernel,
        grid=(BATCH,),
        in_specs=[blk(MLA_QK), blk(MLA_QK), blk(MLA_WIDTH)],
        out_specs=blk(MLA_WIDTH),
        out_shape=jax.ShapeDtypeStruct((N_CTX_TOK, MLA_WIDTH), BF16),
        compiler_params=_cparams("arbitrary"),
        name="mla_ctx",
    )(qf, kf, vm)
    nq = DEC_SEQ // MLA_QBLK
    qblk = lambda w: pl.BlockSpec((MLA_QBLK, w), lambda b, i: (N_CTX_TOK // MLA_QBLK + b * nq + i, 0))
    seq = lambda w: pl.BlockSpec((DEC_SEQ, w), lambda b, i: (N_CTX_TOK // DEC_SEQ + b, 0))
    past = lambda w: pl.BlockSpec((PAST_LEN, w), lambda b, i: (b, 0))
    o_lat = pl.pallas_call(
        _mla_lat_kernel,
        grid=(DEC_BATCH, nq),
        in_specs=[qblk(MLA_QK), seq(MLA_QK), seq(MLA_WIDTH), past(MLA_QK), past(MLA_WIDTH)],
        out_specs=pl.BlockSpec((MLA_QBLK, MLA_WIDTH), lambda b, i: (b * nq + i, 0)),
        out_shape=jax.ShapeDtypeStruct((N_LAT_TOK, MLA_WIDTH), BF16),
        compiler_params=_cparams("arbitrary", "arbitrary"),
        name="mla_lat",
    )(qf, kf, vm, kcf, vc)
    return o, o_lat


HG_TILE = 128
HG_NC = HG_TILE // HG_CHUNK
HG_HALF = 256


HG_SLABS = HG_KD // 128


def _hg_token_plane(raw_s, p):
    return jnp.concatenate([raw_s[j, pl.ds(p, HG_NC, stride=HG_CHUNK), :] for j in range(HG_SLABS)], axis=1)


def _hg_put(ref, x):
    for j in range(HG_SLABS):
        ref[j] = x[:, 128 * j:128 * (j + 1)]


def _hg_get(ref, c, l):
    rows = pl.ds(c, HG_CHUNK, stride=HG_NC)
    return jnp.concatenate([ref[2 * l, rows, :], ref[2 * l + 1, rows, :]], axis=1)


def _hg_plane(ref, p):
    return jnp.concatenate([ref[j, HG_NC * p:HG_NC * (p + 1), :] for j in range(HG_SLABS)], axis=1)


def _hg_put_plane(ref, p, x):
    for j in range(HG_SLABS):
        ref[j, HG_NC * p:HG_NC * (p + 1), :] = x[:, 128 * j:128 * (j + 1)]


HG_PAIRS = HG_CHUNK * (HG_CHUNK + 1) // 2
HG_PAIR_ROWS = HG_PAIRS * HG_NC


def _hg_pair_rows(d, i):
    n = d * HG_CHUNK - d * (d - 1) // 2 + (i - d)
    return slice(HG_NC * n, HG_NC * (n + 1))


def _hg_direction(hq_ref, hf_ref, hi_ref, o_ref, lb, st_ref, raw_s, q_s, f_s, kk_s, v_s, qt_s, kt_s, o_s, p_s, e_s,
                  sign, ones_bd, head_mask):
    _hg_put(raw_s, hq_ref[...])
    for p in range(HG_CHUNK):
        hq = _hg_token_plane(raw_s, p)
        _hg_put_plane(q_s, p, hq * jax.nn.sigmoid(hq))
    _hg_put(raw_s, hf_ref[...])
    for p in range(HG_CHUNK):
        f = lb + (1.0 - lb) * jax.nn.sigmoid(_hg_token_plane(raw_s, p))
        _hg_put_plane(f_s, p, f)
        _hg_put_plane(kk_s, p, 1.0 - f)
    _hg_put(raw_s, hi_ref[...])
    for p in range(HG_CHUNK):
        _hg_put_plane(v_s, p, _hg_token_plane(raw_s, p))
    pos = (lambda i: i) if sign > 0 else (lambda i: HG_CHUNK - 1 - i)
    plane = lambda ref, i: _hg_plane(ref, pos(i))

    for i in range(HG_CHUNK):
        q = plane(q_s, i)
        p_s[_hg_pair_rows(0, i), :] = q * plane(kk_s, i)
        dec = None
        for d in range(1, i + 1):
            fd = plane(f_s, i - d + 1)
            dec = fd if dec is None else dec * fd
            p_s[_hg_pair_rows(d, i), :] = q * plane(kk_s, i - d) * dec
    step = HG_PAIR_ROWS // 4
    for c in range(0, HG_PAIR_ROWS, step):
        pb = p_s[c:c + step, :].astype(BF16)
        e_s[c:c + step, 0:HG_HALF] = jnp.dot(pb[:, 0:HG_HALF], ones_bd, preferred_element_type=F32)
        e_s[c:c + step, HG_HALF:] = jnp.dot(pb[:, HG_HALF:], ones_bd, preferred_element_type=F32)
    for i in range(HG_CHUNK):
        o = None
        for d in range(i + 1):
            t = e_s[_hg_pair_rows(d, i), :] * plane(v_s, i - d)
            o = t if o is None else o + t
        _hg_put_plane(o_s, pos(i), o)

    incl = None
    for i in range(HG_CHUNK):
        fi = plane(f_s, i)
        incl = fi if incl is None else incl * fi
        _hg_put_plane(qt_s, pos(i), plane(q_s, i) * incl)
    whole = incl
    excl = None
    for i in range(HG_CHUNK - 1, -1, -1):
        kt = plane(kk_s, i)
        if excl is not None:
            kt = kt * excl
        _hg_put_plane(kt_s, pos(i), kt)
        fi = plane(f_s, i)
        excl = fi if excl is None else excl * fi

    tile4 = lambda x: jnp.where(head_mask, jnp.concatenate([x] * 4, axis=0), 0.0).astype(BF16)
    for cc in range(HG_NC):
        c = cc if sign > 0 else HG_NC - 1 - cc
        r = slice(HG_CHUNK * c, HG_CHUNK * (c + 1))
        for g in range(2):
            l = slice(HG_HALF * g, HG_HALF * (g + 1))
            st = st_ref[g]
            oc = _dot_nt(tile4(_hg_get(qt_s, c, g)), st)
            o_ref[r, l] = _hg_get(o_s, c, g) + jnp.concatenate(
                [oc[HG_CHUNK * h:HG_CHUNK * (h + 1), :] for h in range(4)], axis=1)
            v = hi_ref[r, l]
            vs = jnp.concatenate([v[:, HG_VAL * h:HG_VAL * (h + 1)] for h in range(4)], axis=0)
            ds = lax.dot_general(vs.astype(BF16), tile4(_hg_get(kt_s, c, g)), (((0,), (0,)), ((), ())),
                                 preferred_element_type=F32)
            st_ref[g] = st * whole[c:c + 1, l] + ds


def _hg_lower_bound(lb_ref, d, layer):
    raw = [lb_ref[d * DEPTH + m:d * DEPTH + m + 1, :] for m in range(DEPTH)]
    mx = functools.reduce(jnp.maximum, raw)
    e = [jnp.exp(r - mx) for r in raw]
    tot = functools.reduce(jnp.add, e)
    return functools.reduce(jnp.add, e[1:layer + 1], jnp.zeros_like(tot)) / tot


def _hgrn_kernel(nt, layer, hqf_ref, hff_ref, hif_ref, hqb_ref, hfb_ref, hib_ref, lb_ref, s0_ref,
                 of_ref, ob_ref, sfin_ref, st_s, raw_s, q_s, f_s, kk_s, v_s, qt_s, kt_s, o_s, p_s, e_s):
    i = pl.program_id(1)

    @pl.when(i == 0)
    def _():
        st_s[...] = s0_ref[0]

    ri = lax.broadcasted_iota(jnp.int32, (HG_HALF, HG_HALF), 0) // HG_KEY
    ci = lax.broadcasted_iota(jnp.int32, (HG_HALF, HG_HALF), 1) // HG_KEY
    ones_bd = jnp.where(ri == ci, 1.0, 0.0).astype(BF16)
    head_mask = (lax.broadcasted_iota(jnp.int32, (4 * HG_CHUNK, HG_HALF), 0) // HG_CHUNK
                 == lax.broadcasted_iota(jnp.int32, (4 * HG_CHUNK, HG_HALF), 1) // HG_KEY)
    _hg_direction(hqf_ref, hff_ref, hif_ref, of_ref, _hg_lower_bound(lb_ref, 0, layer), st_s.at[0],
                  raw_s, q_s, f_s, kk_s, v_s, qt_s, kt_s, o_s, p_s, e_s, 1, ones_bd, head_mask)
    _hg_direction(hqb_ref, hfb_ref, hib_ref, ob_ref, _hg_lower_bound(lb_ref, 1, layer), st_s.at[1],
                  raw_s, q_s, f_s, kk_s, v_s, qt_s, kt_s, o_s, p_s, e_s, -1, ones_bd, head_mask)

    @pl.when(i == nt - 1)
    def _():
        sfin_ref[0] = st_s[...]


def _hg_state_to_blocks(s):
    b = s.shape[0]
    st = s.reshape(b, 2, 2, 4, HG_KEY, HG_VAL).transpose(0, 1, 2, 5, 3, 4)
    return st.reshape(b, 2, 2, HG_VAL, HG_HALF)


def _hg_state_from_blocks(st):
    b = st.shape[0]
    st = st.reshape(b, 2, 2, HG_VAL, 4, HG_KEY).transpose(0, 1, 2, 4, 5, 3)
    return st.reshape(b, 2, HG_HEADS, HG_KEY, HG_VAL)


def _hgrn_scan(hq, hff, hfb, hi, lb, s0, layer, row0, nseq, seqlen):
    nt = seqlen // HG_TILE
    base = row0 // HG_TILE
    fwd = pl.BlockSpec((HG_TILE, HG_KD), lambda b, i: (base + b * nt + i, 0))
    bwd = pl.BlockSpec((HG_TILE, HG_KD), lambda b, i: (base + b * nt + nt - 1 - i, 0))
    ofw = pl.BlockSpec((HG_TILE, HG_KD), lambda b, i: (b * nt + i, 0))
    obw = pl.BlockSpec((HG_TILE, HG_KD), lambda b, i: (b * nt + nt - 1 - i, 0))
    st = pl.BlockSpec((1, 2, 2, HG_VAL, HG_HALF), lambda b, i: (b, 0, 0, 0, 0))
    tile = lambda: pltpu.VMEM((HG_SLABS, HG_TILE, 128), F32)
    pairs = lambda: pltpu.VMEM((HG_PAIR_ROWS, HG_KD), F32)
    return pl.pallas_call(
        functools.partial(_hgrn_kernel, nt, layer),
        grid=(nseq, nt),
        in_specs=[fwd, fwd, fwd, bwd, bwd, bwd, pl.BlockSpec((2 * DEPTH, HG_KD), lambda b, i: (0, 0)), st],
        out_specs=[ofw, obw, st],
        out_shape=[jax.ShapeDtypeStruct((nseq * seqlen, HG_WIDTH), F32)] * 2
        + [jax.ShapeDtypeStruct((nseq, 2, 2, HG_VAL, HG_HALF), F32)],
        scratch_shapes=[pltpu.VMEM((2, 2, HG_VAL, HG_HALF), F32)] + [tile() for _ in range(8)] + [pairs(), pairs()],
        compiler_params=_cparams("arbitrary", "arbitrary"),
        name="hgrn_scan",
    )(hq, hff, hi, hq, hfb, hi, lb.reshape(2 * DEPTH, HG_KD), s0)


def _cd_out_kernel(x_ref, occ_ref, ocl_ref, ofc_ref, ofl_ref, obc_ref, obl_ref, hg_ref, go_ref, wout_ref, gt_ref,
                   o_ref, wa_s, wb_s):
    @pl.when(pl.program_id(0) == 0)
    def _():
        wa_s[...] = wout_ref[0:MLA_WIDTH, :].astype(BF16)
        wb_s[...] = wout_ref[MLA_WIDTH:, :].astype(BF16)

    ri = lax.broadcasted_iota(jnp.int32, (HG_HALF, HG_HALF), 0) // HG_VAL
    ci = lax.broadcasted_iota(jnp.int32, (HG_HALF, HG_HALF), 1) // HG_VAL
    ones_bd = jnp.where(ri == ci, 1.0, 0.0).astype(BF16)
    o = _ctx_or_lat(ofc_ref, ofl_ref) + _ctx_or_lat(obc_ref, obl_ref)
    sq = o * o
    hi = sq.astype(BF16)
    lo = (sq - hi.astype(F32)).astype(BF16)
    ms = jnp.concatenate(
        [jnp.dot(hi[:, l], ones_bd, preferred_element_type=F32) + jnp.dot(lo[:, l], ones_bd, preferred_element_type=F32)
         for l in (slice(0, HG_HALF), slice(HG_HALF, 2 * HG_HALF))], axis=1) * (1.0 / HG_VAL)
    hg = hg_ref[...]
    od = o * lax.rsqrt(ms + EPS) * go_ref[...] * (hg * jax.nn.sigmoid(hg))
    out = (jnp.dot(_ctx_or_lat(occ_ref, ocl_ref).astype(BF16), wa_s[...], preferred_element_type=F32)
           + jnp.dot(od.astype(BF16), wb_s[...], preferred_element_type=F32))
    o_ref[...] = x_ref[...] + gt_ref[0] * out


def _cd_out(x, oc, of, ob, hg, g_o, w_out, mod, layer):
    row = lambda n: pl.BlockSpec((ROW_BLK, n), lambda i: (i, 0))
    full = lambda a, b: pl.BlockSpec((a, b), lambda i: (0, 0), pipeline_mode=pl.Buffered(1))
    return pl.pallas_call(
        _cd_out_kernel,
        grid=(N_ROW_BLK,),
        in_specs=[row(D_MODEL)] + _split_specs(MLA_WIDTH) + _split_specs(HG_WIDTH) + _split_specs(HG_WIDTH)
        + [row(HG_WIDTH), full(1, HG_WIDTH), full(MLA_WIDTH + HG_WIDTH, D_MODEL), _mod_spec(layer, 2)],
        out_specs=row(D_MODEL),
        out_shape=jax.ShapeDtypeStruct((N_TOK, D_MODEL), F32),
        scratch_shapes=[pltpu.VMEM((MLA_WIDTH, D_MODEL), BF16), pltpu.VMEM((HG_WIDTH, D_MODEL), BF16)],
        compiler_params=_cparams("arbitrary"),
        name="cd_out",
    )(x, oc[0], oc[1], of[0], of[1], ob[0], ob[1], hg, jnp.tile(g_o, HG_HEADS).reshape(1, HG_WIDTH), w_out, mod)


N_PAIRS = 6
N_CLASSES = N_GROUPS * N_PAIRS
CLS_ROWS = 32
MOE_BM = 256
MOE_NBLK = N_TOK // MOE_BM + N_CLASSES
MOE_ROWS = MOE_NBLK * MOE_BM
PAIR_SLOTS = ((0, 1), (3, 1), (2, 1), (2, 0), (3, 0), (3, 2))


def _moe_route_kernel(x_ref, g_ref, sc_ref, sh_ref, rw_ref, rb_ref, h_ref, ti_ref, tw_ref, cnt_ref, base_s):
    @pl.when(pl.program_id(0) == 0)
    def _():
        base_s[...] = jnp.zeros_like(base_s)

    h = _norm_mod(x_ref[...], g_ref[...], sc_ref[0], sh_ref[0])
    logits = lax.dot_general(rw_ref[...], h, (((1,), (1,)), ((), ())), precision=HI,
                             preferred_element_type=F32)
    aff = jax.nn.sigmoid(logits)
    sel = aff + rb_ref[...]
    s = [sel[e:e + 1, :] for e in range(N_EXPERTS)]
    a = [aff[e:e + 1, :] for e in range(N_EXPERTS)]
    gs = []
    for g in range(N_GROUPS):
        m = s[4 * g:4 * g + 4]
        pairs = [m[i] + m[j] for i in range(4) for j in range(i + 1, 4)]
        gs.append(functools.reduce(jnp.maximum, pairs))
    gmax = functools.reduce(jnp.maximum, gs)
    taken = jnp.zeros_like(gmax) > 1.0
    gsel = []
    for g in range(N_GROUPS):
        hit = (gs[g] == gmax) & jnp.logical_not(taken)
        gsel.append(hit)
        taken = taken | hit
    e_lo = jnp.zeros(gmax.shape, jnp.int32)
    e_hi = jnp.zeros(gmax.shape, jnp.int32)
    a_lo = jnp.zeros_like(gmax)
    a_hi = jnp.zeros_like(gmax)
    nsel = jnp.zeros(gmax.shape, jnp.int32)
    for g in range(N_GROUPS):
        for i in range(4):
            e = 4 * g + i
            beat = jnp.zeros(gmax.shape, jnp.int32)
            for j in range(4):
                if j != i:
                    o = 4 * g + j
                    beat = beat + jnp.where((s[o] > s[e]) | ((s[o] == s[e]) & (j < i)), 1, 0)
            pick = gsel[g] & (beat < 2)
            is_first = pick & (nsel == 0)
            is_second = pick & (nsel == 1)
            e_lo = jnp.where(is_first, e, e_lo)
            a_lo = jnp.where(is_first, a[e], a_lo)
            e_hi = jnp.where(is_second, e, e_hi)
            a_hi = jnp.where(is_second, a[e], a_hi)
            nsel = nsel + jnp.where(pick, 1, 0)
    grp = e_lo // EXPERTS_PER_GROUP
    lo = e_lo - grp * EXPERTS_PER_GROUP
    hi = e_hi - grp * EXPERTS_PER_GROUP
    pair = ((lo * (7 - lo)) >> 1) + (hi - lo - 1)
    pair = jnp.where(pair == 1, 3, jnp.where(pair == 2, 4, jnp.where(pair == 3, 2, jnp.where(pair == 4, 1, pair))))
    cls = grp * N_PAIRS + pair
    wsum = a_lo + a_hi
    w_lo, w_hi = a_lo / wsum, a_hi / wsum
    onehot = (lax.broadcasted_iota(jnp.int32, (CLS_ROWS, ROW_BLK), 0) == cls).astype(F32)
    tt = lax.broadcasted_iota(jnp.int32, (ROW_BLK, ROW_BLK), 0) < lax.broadcasted_iota(jnp.int32, (ROW_BLK, ROW_BLK), 1)
    before = _dot(onehot, jnp.where(tt, 1.0, 0.0))
    base = base_s[...]
    rank = jnp.sum(onehot * (before + base[:, 0:1]), axis=0, keepdims=True).astype(jnp.int32)
    base = base + jnp.sum(onehot, axis=1, keepdims=True)
    base_s[...] = base
    cnt_ref[...] = base.astype(jnp.int32)
    ti_ref[0] = jnp.concatenate([cls, rank, e_lo, e_hi, jnp.zeros((4, ROW_BLK), jnp.int32)], axis=0)
    ident = lax.broadcasted_iota(jnp.int32, (ROW_BLK, ROW_BLK), 0) == lax.broadcasted_iota(jnp.int32, (ROW_BLK, ROW_BLK), 1)
    col = lambda r: jnp.sum(jnp.where(ident, r, 0.0), axis=1, keepdims=True)
    tw_ref[...] = jnp.where(lax.broadcasted_iota(jnp.int32, (ROW_BLK, 128), 1) < 64, col(w_lo), col(w_hi))
    h_ref[...] = h


def _moe_route(x, mod, g, router_w, router_b, layer):
    row = lambda n: pl.BlockSpec((ROW_BLK, n), lambda i: (i, 0))
    full = lambda a, b: pl.BlockSpec((a, b), lambda i: (0, 0), pipeline_mode=pl.Buffered(1))
    return pl.pallas_call(
        _moe_route_kernel,
        grid=(N_ROW_BLK,),
        in_specs=[row(D_MODEL), full(1, D_MODEL), _mod_spec(layer, 4), _mod_spec(layer, 3),
                  full(N_EXPERTS, D_MODEL), full(N_EXPERTS, 1)],
        out_specs=[row(D_MODEL), pl.BlockSpec((1, 8, ROW_BLK), lambda i: (i, 0, 0)), row(128),
                   pl.BlockSpec((CLS_ROWS, 128), lambda i: (0, 0))],
        out_shape=[jax.ShapeDtypeStruct((N_TOK, D_MODEL), F32),
                   jax.ShapeDtypeStruct((N_ROW_BLK, 8, ROW_BLK), jnp.int32),
                   jax.ShapeDtypeStruct((N_TOK, 128), F32),
                   jax.ShapeDtypeStruct((CLS_ROWS, 128), jnp.int32)],
        scratch_shapes=[pltpu.VMEM((CLS_ROWS, 128), F32)],
        compiler_params=_cparams("arbitrary"),
        name="moe_route",
    )(x, g.reshape(1, D_MODEL), mod, mod, router_w.T, router_b.reshape(N_EXPERTS, 1))


def _moe_sort_kernel(pos_ref, nblk_ref, pad0_ref, pad1_ref, h_ref, wt_ref, hs_ref, ws_ref, perm_s, stage_s):
    j = pl.program_id(0)

    @pl.when(j == 0)
    def _():
        def init(r, c):
            perm_s[r] = N_TOK - 1
            return c

        for c in range(N_CLASSES):
            lax.fori_loop(pad0_ref[c], pad1_ref[c], init, 0)

        def build(t, c):
            perm_s[pos_ref[t]] = t
            return c

        lax.fori_loop(0, N_TOK, build, 0, unroll=16)

    @pl.when(j < nblk_ref[0])
    def _():
        base = j * MOE_BM
        for r in range(MOE_BM):
            src = perm_s[base + r]
            stage_s[r:r + 1, :] = h_ref[pl.ds(src, 1), :]
            ws_ref[r:r + 1, :] = wt_ref[pl.ds(src, 1), :]
        hs_ref[...] = stage_s[...].astype(BF16)

    @pl.when(j >= nblk_ref[0])
    def _():
        hs_ref[...] = jnp.zeros_like(hs_ref)
        ws_ref[...] = jnp.zeros_like(ws_ref)


def _moe_sort(h, wtok, pos, nblk, pad0, pad1):
    res = lambda w: pl.BlockSpec((N_TOK, w), lambda j, *_: (0, 0), pipeline_mode=pl.Buffered(1))
    grid_spec = pltpu.PrefetchScalarGridSpec(
        num_scalar_prefetch=4,
        grid=(MOE_NBLK,),
        in_specs=[res(D_MODEL), res(128)],
        out_specs=[pl.BlockSpec((MOE_BM, D_MODEL), lambda j, *_: (j, 0)),
                   pl.BlockSpec((MOE_BM, 128), lambda j, *_: (j, 0))],
        scratch_shapes=[pltpu.SMEM((MOE_ROWS,), jnp.int32), pltpu.VMEM((MOE_BM, D_MODEL), F32)],
    )
    return pl.pallas_call(
        _moe_sort_kernel,
        grid_spec=grid_spec,
        out_shape=[jax.ShapeDtypeStruct((MOE_ROWS, D_MODEL), BF16), jax.ShapeDtypeStruct((MOE_ROWS, 128), F32)],
        compiler_params=_cparams("arbitrary"),
        name="moe_sort",
    )(pos, nblk, pad0, pad1, h, wtok)


def _moe_experts_kernel(elo_ref, ehi_ref, nblk_ref, swap_ref, hs_ref, ws_ref, wgl_ref, wul_ref, wdl_ref,
                        wgh_ref, wuh_ref, wdh_ref, y_ref):
    del elo_ref, ehi_ref
    j = pl.program_id(0)

    @pl.when(j < nblk_ref[0])
    def _():
        h = hs_ref[...].astype(BF16)
        swap = swap_ref[j] == 1
        gate_a = jnp.where(swap, ws_ref[:, 64:65], ws_ref[:, 0:1])
        gate_b = jnp.where(swap, ws_ref[:, 0:1], ws_ref[:, 64:65])
        acc = None
        for wg, wu, wd, gate in ((wgl_ref, wul_ref, wdl_ref, gate_a), (wgh_ref, wuh_ref, wdh_ref, gate_b)):
            g = _dot(h, wg[0, 0])
            u = _dot(h, wu[0, 0])
            hid = g * jax.nn.sigmoid(g) * u * gate
            y = _dot(hid, wd[0, 0])
            acc = y if acc is None else acc + y
        y_ref[...] = acc

    @pl.when(j >= nblk_ref[0])
    def _():
        y_ref[...] = jnp.zeros_like(y_ref)


def _moe_experts(hs, ws, blk_elo, blk_ehi, nblk, swap, w_gate, w_up, w_down, layer):
    last = lambda j, nb: jnp.minimum(j, nb[0] - 1)
    wspec = lambda a, b, which: pl.BlockSpec(
        (1, 1, a, b), lambda j, elo, ehi, nb, sw: (layer, (elo, ehi)[which][last(j, nb)], 0, 0))
    grid_spec = pltpu.PrefetchScalarGridSpec(
        num_scalar_prefetch=4,
        grid=(MOE_NBLK,),
        in_specs=[pl.BlockSpec((MOE_BM, D_MODEL), lambda j, elo, ehi, nb, sw: (last(j, nb), 0)),
                  pl.BlockSpec((MOE_BM, 128), lambda j, elo, ehi, nb, sw: (last(j, nb), 0)),
                  wspec(D_MODEL, D_FF, 0), wspec(D_MODEL, D_FF, 0), wspec(D_FF, D_MODEL, 0),
                  wspec(D_MODEL, D_FF, 1), wspec(D_MODEL, D_FF, 1), wspec(D_FF, D_MODEL, 1)],
        out_specs=pl.BlockSpec((MOE_BM, D_MODEL), lambda j, *_: (j, 0)),
    )
    return pl.pallas_call(
        _moe_experts_kernel,
        grid_spec=grid_spec,
        out_shape=jax.ShapeDtypeStruct((MOE_ROWS, D_MODEL), F32),
        compiler_params=_cparams("arbitrary"),
        name="moe_experts",
    )(blk_elo, blk_ehi, nblk, swap, hs, ws, w_gate, w_up, w_down, w_gate, w_up, w_down)


SC_ROWS = 64


def _sc_gather_rows(table, idx):
    info = plsc.get_sparse_core_info()
    nc, nw = info.num_cores, info.num_cores * info.num_subcores
    b, d = idx.shape[0], table.shape[1]
    per_w = b // nw
    assert per_w * nw == b and per_w % SC_ROWS == 0
    mesh = plsc.VectorSubcoreMesh(core_axis_name="c", subcore_axis_name="s")

    @functools.partial(
        pl.kernel, mesh=mesh, out_type=jax.ShapeDtypeStruct((b, d), table.dtype),
        scratch_types=[pltpu.VMEM((SC_ROWS,), jnp.int32), pltpu.VMEM((SC_ROWS, d), table.dtype),
                       pltpu.SemaphoreType.DMA])
    def gather(table_hbm, idx_hbm, out_hbm, idx_v, rows_v, sem):
        wid = lax.axis_index("s") * nc + lax.axis_index("c")
        for c in range(per_w // SC_ROWS):
            base = wid * per_w + c * SC_ROWS
            pltpu.sync_copy(idx_hbm.at[pl.ds(base, SC_ROWS)], idx_v)
            pltpu.async_copy(table_hbm.at[idx_v], rows_v, sem).wait()
            pltpu.sync_copy(rows_v, out_hbm.at[pl.ds(base, SC_ROWS)])

    return gather(table, idx)


def _moe_residual_kernel(x_ref, y_ref, gt_ref, o_ref):
    o_ref[...] = x_ref[...] + gt_ref[0] * y_ref[...]


def _moe_residual(x, y_tok, mod, layer):
    row = pl.BlockSpec((ROW_BLK, D_MODEL), lambda i: (i, 0))
    return pl.pallas_call(
        _moe_residual_kernel,
        grid=(N_ROW_BLK,),
        in_specs=[row, row, _mod_spec(layer, 5)],
        out_specs=row,
        out_shape=jax.ShapeDtypeStruct((N_TOK, D_MODEL), F32),
        compiler_params=_cparams("arbitrary"),
        name="moe_residual",
    )(x, y_tok, mod)


def _moe(x, mod, g, router_w, router_b, w_gate, w_up, w_down, layer):
    h, info, wtok, counts = _moe_route(x, mod, g, router_w, router_b, layer)
    cls = info[:, 0, :].reshape(N_TOK)
    rank = info[:, 1, :].reshape(N_TOK)
    cnt = counts[:N_CLASSES, 0]
    nb = (cnt + MOE_BM - 1) // MOE_BM
    ends = jnp.cumsum(nb)
    starts = ends - nb
    pos = ((starts * MOE_BM)[cls] + rank).astype(jnp.int32)
    blk = jnp.arange(MOE_NBLK, dtype=jnp.int32)
    blk_cls = jnp.minimum(jnp.sum((blk[:, None] >= ends[None, :]).astype(jnp.int32), axis=1), N_CLASSES - 1)
    slot_a = jnp.asarray([s[0] for s in PAIR_SLOTS], jnp.int32)
    slot_b = jnp.asarray([s[1] for s in PAIR_SLOTS], jnp.int32)
    grp = blk_cls // N_PAIRS
    blk_ea = (grp * EXPERTS_PER_GROUP + slot_a[blk_cls % N_PAIRS]).astype(jnp.int32)
    blk_eb = (grp * EXPERTS_PER_GROUP + slot_b[blk_cls % N_PAIRS]).astype(jnp.int32)
    swap = (blk_ea > blk_eb).astype(jnp.int32)
    nblk = ends[-1:].astype(jnp.int32)
    hs, ws = _moe_sort(h, wtok, pos, nblk, (starts * MOE_BM + cnt).astype(jnp.int32), (ends * MOE_BM).astype(jnp.int32))
    y_sorted = _moe_experts(hs, ws, blk_ea, blk_eb, nblk, swap, w_gate, w_up, w_down, layer)
    return _sc_gather_rows(y_sorted, pos)


def _final_norm_kernel(x_ref, y_ref, gt_ref, g_ref, o_ref):
    o_ref[...] = _rms(x_ref[...] + gt_ref[0] * y_ref[...]) * g_ref[...]


def _final_norm(x, y_tok, mod, layer, g, row0, rows):
    base = row0 // ROW_BLK
    row = pl.BlockSpec((ROW_BLK, D_MODEL), lambda i: (base + i, 0))
    gate = pl.BlockSpec((1, 1, D_MODEL), lambda i: ((layer * MOD_ROWS + _mod_group(base + i)) * N_MOD + 5, 0, 0))
    return pl.pallas_call(
        _final_norm_kernel,
        grid=(rows // ROW_BLK,),
        in_specs=[row, row, gate, pl.BlockSpec((1, D_MODEL), lambda i: (0, 0))],
        out_specs=pl.BlockSpec((ROW_BLK, D_MODEL), lambda i: (i, 0)),
        out_shape=jax.ShapeDtypeStruct((rows, D_MODEL), F32),
        compiler_params=_cparams("arbitrary"),
        name="final_norm",
    )(x, y_tok, mod, g.reshape(1, D_MODEL))


def kernel(x_prompt, x_sample, cache_attn_k, cache_attn_v, state_ssm_re, state_ssm_im, cache_mla_ckv, cache_mla_kpe,
           state_hgrn, c, c_ctx, w_mod, b_mod, g_mix, g_ffn, g_final, router_w, router_b, moe_w_gate, moe_w_up,
           moe_w_down, ab_w_in, ab_sink, s5_lam_re, s5_lam_im, s5_log_dt, s5_b_re, s5_b_im, s5_c_re, s5_c_im, s5_d,
           s5_w_glu, ab_w_out, cd_w_in, mla_g_q, mla_w_q_up, mla_g_kv, mla_w_kv_up, hg_lower_bounds, hg_g_o, cd_w_out):
    x = (x_prompt.reshape(N_CTX_TOK, D_MODEL), x_sample.reshape(N_LAT_TOK, D_MODEL))
    y_tok = None
    cond = jnp.zeros((MOD_ROWS, D_MODEL), F32).at[0].set(c_ctx).at[1:1 + DEC_BATCH].set(c)
    mod = _modulation(cond, w_mod, b_mod)
    keep = ([], [], [], [], [], [], [])
    for l in range(DEPTH):
        j = l // 2
        if l % 2 == 0:
            if y_tok is not None:
                x = _moe_residual(x, y_tok, mod, l - 1)
                x = (x[:N_CTX_TOK], x[N_CTX_TOK:])
            q, k, v, u = _ab_in(x, mod, g_mix[l], ab_w_in[j], l)
            o_a = _attention_a(q, k, v, cache_attn_k[:, j].reshape(DEC_BATCH, PAST_LEN, A_KV_WIDTH),
                               cache_attn_v[:, j].reshape(DEC_BATCH, PAST_LEN, A_KV_WIDTH), ab_sink[j])
            y, fin_re, fin_im = _s5_scan(u, state_ssm_re[:, j], state_ssm_im[:, j], s5_lam_re[j], s5_lam_im[j],
                                         s5_log_dt[j], s5_b_re[j], s5_b_im[j], s5_c_re[j], s5_c_im[j])
            x = _ab_out(x, o_a, y, u, s5_d[j], s5_w_glu[j], ab_w_out[j], mod, l)
            keep[0].append(k[:N_CTX_TOK].reshape(BATCH, SEQ, A_KV_HEADS, HEAD_DIM))
            keep[1].append(v[:N_CTX_TOK].reshape(BATCH, SEQ, A_KV_HEADS, HEAD_DIM))
            keep[2].append(fin_re)
            keep[3].append(fin_im)
        else:
            x, qf, ckv, kpe, kf, vm, hq, hff, hfb, hi, hg = _cd_in(
                x, y_tok, mod, g_mix[l], cd_w_in[j], mla_g_q[j], mla_w_q_up[j], mla_g_kv[j], mla_w_kv_up[j], l)
            kcf, vc = _mla_cache_kv(cache_mla_ckv[:, j].reshape(DEC_BATCH * PAST_LEN, MLA_KV_RANK),
                                    cache_mla_kpe[:, j].reshape(DEC_BATCH * PAST_LEN, MLA_ROPE), mla_w_kv_up[j])
            o_c = _attention_mla(qf, kf, vm, kcf, vc)
            s0_ctx = jnp.zeros((BATCH, 2, 2, HG_VAL, HG_HALF), F32)
            of_c, ob_c, s_fin = _hgrn_scan(hq, hff, hfb, hi, hg_lower_bounds, s0_ctx, l, 0, BATCH, SEQ)
            of_l, ob_l, _ = _hgrn_scan(hq, hff, hfb, hi, hg_lower_bounds, _hg_state_to_blocks(state_hgrn[:, j]), l,
                                       N_CTX_TOK, DEC_BATCH, DEC_SEQ)
            x = _cd_out(x, o_c, (of_c, of_l), (ob_c, ob_l), hg, hg_g_o[j], cd_w_out[j], mod, l)
            keep[4].append(ckv[:N_CTX_TOK].reshape(BATCH, SEQ, MLA_KV_RANK))
            keep[5].append(kpe[:N_CTX_TOK, :MLA_ROPE].reshape(BATCH, SEQ, MLA_ROPE))
            keep[6].append(_hg_state_from_blocks(s_fin))
        y_tok = _moe(x, mod, g_ffn[l], router_w, router_b, moe_w_gate, moe_w_up, moe_w_down, l)
    y_ctx = _final_norm(x, y_tok, mod, DEPTH - 1, g_final, 0, N_CTX_TOK)
    y_lat = _final_norm(x, y_tok, mod, DEPTH - 1, g_final, N_CTX_TOK, N_LAT_TOK)
    return (y_ctx.reshape(BATCH, SEQ, D_MODEL), y_lat.reshape(DEC_BATCH, DEC_SEQ, D_MODEL),
            jnp.stack(keep[0], 1), jnp.stack(keep[1], 1), jnp.stack(keep[2], 1), jnp.stack(keep[3], 1),
            jnp.stack(keep[4], 1), jnp.stack(keep[5], 1), jnp.stack(keep[6], 1))
```

```python
import functools
import math

import numpy as np
import jax
import jax.numpy as jnp
from jax import lax
from jax.experimental import pallas as pl
from jax.experimental.pallas import tpu as pltpu
from jax.experimental.pallas import tpu_sc as plsc

F32 = jnp.float32
BF16 = jnp.bfloat16

D_MODEL = 1024
BATCH = 16
SEQ = 256
DEPTH = 2
DEC_BATCH = 4
DEC_SEQ = 1024
PAST_LEN = 512
GRID_W = 64
N_MOD = 6
EPS = 1e-6
NEG_INF = -1e30
ROPE_BASE = 10000.0
HEAD_DIM = 64
A_HEADS = 8
A_KV_HEADS = 2
A_WINDOW = 128
A_WIDTH = A_HEADS * HEAD_DIM
A_KV_WIDTH = A_KV_HEADS * HEAD_DIM
S5_WIDTH = D_MODEL // 2
S5_GROUP = 16
S5_GROUPS = S5_WIDTH // S5_GROUP
S5_STATE = 64
MLA_HEADS = 8
MLA_Q_RANK = D_MODEL // 4
MLA_KV_RANK = D_MODEL // 8
MLA_NOPE = 64
MLA_ROPE = 32
MLA_V = 64
MLA_WIDTH = MLA_HEADS * MLA_V
HG_HEADS = 8
HG_KEY = 64
HG_VAL = 64
HG_KD = HG_HEADS * HG_KEY
HG_WIDTH = HG_HEADS * HG_VAL
HG_CHUNK = 16
N_EXPERTS = 16
N_GROUPS = 4
EXPERTS_PER_GROUP = N_EXPERTS // N_GROUPS
D_FF = D_MODEL // 2

N_CTX_TOK = BATCH * SEQ
N_LAT_TOK = DEC_BATCH * DEC_SEQ
N_TOK = N_CTX_TOK + N_LAT_TOK
ROW_BLK = 512
MLA_QBLK = 256
N_ROW_BLK = N_TOK // ROW_BLK
N_CTX_BLK = N_CTX_TOK // ROW_BLK
LAT_BLK_PER_SEQ = DEC_SEQ // ROW_BLK
MOD_ROWS = 8
VMEM_LIMIT = 56 * 1024 * 1024


def _cparams(*sem):
    return pltpu.CompilerParams(dimension_semantics=sem, vmem_limit_bytes=VMEM_LIMIT)


def _mod_group(i):
    return jnp.where(i < N_CTX_BLK, 0, 1 + (i - N_CTX_BLK) // LAT_BLK_PER_SEQ)


def _mod_spec(layer, which):
    return pl.BlockSpec((1, 1, D_MODEL), lambda i: ((layer * MOD_ROWS + _mod_group(i)) * N_MOD + which, 0, 0))


def _rope_blk(i):
    return jnp.where(i < N_CTX_BLK, 0, 1 + (i - N_CTX_BLK) % LAT_BLK_PER_SEQ)


def _rope_tables(rot_dim, only=None):
    n_freq = rot_dim // 4
    t = np.arange(DEC_SEQ)
    rows = (t // GRID_W).astype(np.float32)
    cols = (t % GRID_W).astype(np.float32)
    inv = (np.float32(ROPE_BASE) ** (-np.arange(n_freq, dtype=np.float32) / np.float32(n_freq))).astype(np.float32)
    ang_r = rows[:, None] * inv[None, :]
    ang_c = cols[:, None] * inv[None, :]
    ang = np.concatenate([ang_r, ang_r, ang_c, ang_c], axis=-1).astype(np.float32)
    reps = 128 // rot_dim
    cos = np.tile(np.cos(ang), (1, reps)).astype(np.float32)
    sin = np.tile(np.sin(ang), (1, reps)).astype(np.float32)
    lane = np.arange(128)
    first = (lane % (2 * n_freq)) < n_freq
    sin_a = np.where(first[None, :], -sin, 0.0).astype(np.float32)
    sin_b = np.where(first[None, :], 0.0, sin).astype(np.float32)
    if only is not None:
        keep = ((lane >= only[0]) & (lane < only[1]))[None, :]
        cos, sin_a, sin_b = np.where(keep, cos, 1.0), np.where(keep, sin_a, 0.0), np.where(keep, sin_b, 0.0)
        cos, sin_a, sin_b = cos.astype(np.float32), sin_a.astype(np.float32), sin_b.astype(np.float32)
    ident = np.zeros((ROW_BLK, 128), np.float32)
    cos = np.concatenate([ident + 1.0, cos], axis=0)
    sin_a = np.concatenate([ident, sin_a], axis=0)
    sin_b = np.concatenate([ident, sin_b], axis=0)
    return jnp.asarray(cos), jnp.asarray(sin_a), jnp.asarray(sin_b)


def _apply_rope(x, cos, sin_a, sin_b, quarter):
    outs = []
    for j in range(x.shape[1] // 128):
        xt = x[:, 128 * j:128 * (j + 1)]
        up = pltpu.roll(xt, 128 - quarter, axis=1)
        dn = pltpu.roll(xt, quarter, axis=1)
        outs.append(xt * cos + up * sin_a + dn * sin_b)
    return outs[0] if len(outs) == 1 else jnp.concatenate(outs, axis=1)


def _rms(x):
    return x * lax.rsqrt(jnp.mean(x * x, axis=-1, keepdims=True) + EPS)


def _norm_mod(x, g, sc, sh):
    return _rms(x) * g * (1.0 + sc) + sh


def _dot(a, b):
    return jnp.dot(a.astype(BF16), b.astype(BF16), preferred_element_type=F32)


def _dot_nt(a, b):
    return lax.dot_general(a.astype(BF16), b.astype(BF16), (((1,), (1,)), ((), ())), preferred_element_type=F32)


def _mod_kernel(cond_ref, w_ref, b_ref, o_ref):
    c = cond_ref[...]
    s = c * jax.nn.sigmoid(c)
    o_ref[0] = _dot(s, w_ref[0]) + b_ref[0]


def _modulation(cond, w_mod, b_mod):
    nb = 1024
    out = pl.pallas_call(
        _mod_kernel,
        grid=(DEPTH, N_MOD * D_MODEL // nb),
        in_specs=[pl.BlockSpec((MOD_ROWS, D_MODEL), lambda l, n: (0, 0)),
                  pl.BlockSpec((1, D_MODEL, nb), lambda l, n: (l, 0, n)),
                  pl.BlockSpec((1, 1, nb), lambda l, n: (l, 0, n))],
        out_specs=pl.BlockSpec((1, MOD_ROWS, nb), lambda l, n: (l, 0, n)),
        out_shape=jax.ShapeDtypeStruct((DEPTH, MOD_ROWS, N_MOD * D_MODEL), F32),
        compiler_params=_cparams("arbitrary", "arbitrary"),
        name="modulation",
    )(cond, w_mod, b_mod.reshape(DEPTH, 1, N_MOD * D_MODEL))
    return out.reshape(DEPTH * MOD_ROWS * N_MOD, 1, D_MODEL)


def _ab_in_kernel(xc_ref, xl_ref, g_ref, sc_ref, sh_ref, w_ref, cos_ref, sa_ref, sb_ref,
                  q_ref, k_ref, v_ref, u_ref, wq_s, wk_s, wv_s, wu_s):
    @pl.when(pl.program_id(0) == 0)
    def _():
        wq_s[...] = w_ref[:, 0:A_WIDTH].astype(BF16)
        wk_s[...] = w_ref[:, A_WIDTH:A_WIDTH + A_KV_WIDTH].astype(BF16)
        wv_s[...] = w_ref[:, A_WIDTH + A_KV_WIDTH:A_WIDTH + 2 * A_KV_WIDTH].astype(BF16)
        wu_s[...] = w_ref[:, A_WIDTH + 2 * A_KV_WIDTH:].astype(BF16)

    h = _norm_mod(_ctx_or_lat(xc_ref, xl_ref), g_ref[...], sc_ref[0], sh_ref[0]).astype(BF16)
    cos, sa, sb = cos_ref[...], sa_ref[...], sb_ref[...]
    q = jnp.dot(h, wq_s[...], preferred_element_type=F32)
    q_ref[...] = _apply_rope(q, cos, sa, sb, HEAD_DIM // 4).astype(q_ref.dtype)
    k = jnp.dot(h, wk_s[...], preferred_element_type=F32)
    k_ref[...] = _apply_rope(k, cos, sa, sb, HEAD_DIM // 4)
    v_ref[...] = jnp.dot(h, wv_s[...], preferred_element_type=F32)
    u_ref[...] = jnp.dot(h, wu_s[...], preferred_element_type=F32)


def _ab_in(x, mod, g, w, layer):
    cos, sa, sb = _rope_tables(HEAD_DIM)
    d_in = w.shape[1]
    row = lambda n: pl.BlockSpec((ROW_BLK, n), lambda i: (i, 0))
    rope = pl.BlockSpec((ROW_BLK, 128), lambda i: (_rope_blk(i), 0))
    return pl.pallas_call(
        _ab_in_kernel,
        grid=(N_ROW_BLK,),
        in_specs=_split_specs(D_MODEL) + [pl.BlockSpec((1, D_MODEL), lambda i: (0, 0)),
                  _mod_spec(layer, 1), _mod_spec(layer, 0),
                  pl.BlockSpec((D_MODEL, d_in), lambda i: (0, 0), pipeline_mode=pl.Buffered(1)), rope, rope, rope],
        out_specs=[row(A_WIDTH), row(A_KV_WIDTH), row(A_KV_WIDTH), row(S5_WIDTH)],
        out_shape=[jax.ShapeDtypeStruct((N_TOK, A_WIDTH), BF16), jax.ShapeDtypeStruct((N_TOK, A_KV_WIDTH), F32),
                   jax.ShapeDtypeStruct((N_TOK, A_KV_WIDTH), F32), jax.ShapeDtypeStruct((N_TOK, S5_WIDTH), F32)],
        scratch_shapes=[pltpu.VMEM((D_MODEL, A_WIDTH), BF16), pltpu.VMEM((D_MODEL, A_KV_WIDTH), BF16),
                        pltpu.VMEM((D_MODEL, A_KV_WIDTH), BF16), pltpu.VMEM((D_MODEL, S5_WIDTH), BF16)],
        compiler_params=_cparams("arbitrary"),
        name="ab_in",
    )(x[0], x[1], g.reshape(1, D_MODEL), mod, mod, w, cos, sa, sb)


def _softmax_pv(s_list, v_list, sink):
    m = functools.reduce(jnp.maximum, [jnp.max(s, axis=-1, keepdims=True) for s in s_list])
    if sink is not None:
        m = jnp.maximum(m, sink)
    ps = [jnp.exp(s - m) for s in s_list]
    l = functools.reduce(jnp.add, [jnp.sum(p, axis=-1, keepdims=True) for p in ps])
    if sink is not None:
        l = l + jnp.exp(sink - m)
    o = functools.reduce(jnp.add, [_dot(p, v) for p, v in zip(ps, v_list)])
    return o / l


def _gqa_with_sink(sink_ref, q, k_of, v_of, bias, o_ref):
    g = A_HEADS // A_KV_HEADS
    r = q.shape[0]
    outs = [None] * A_HEADS
    for kh in range(A_KV_HEADS):
        heads = range(g * kh, g * (kh + 1))
        qg = jnp.concatenate([q[:, HEAD_DIM * h:HEAD_DIM * (h + 1)] for h in heads], axis=0)
        sink = jnp.concatenate([jnp.full((r, 1), sink_ref[h], F32) for h in heads], axis=0)
        s = _dot_nt(qg, k_of(kh))
        if bias is not None:
            s = s + jnp.concatenate([bias] * g, axis=0)
        m = jnp.maximum(jnp.max(s, axis=-1, keepdims=True), sink)
        p = jnp.exp(s - m)
        l = jnp.sum(p, axis=-1, keepdims=True) + jnp.exp(sink - m)
        o = _dot(p, v_of(kh)) / l
        for n, h in enumerate(heads):
            outs[h] = o[r * n:r * (n + 1)]
    o_ref[...] = jnp.concatenate(outs, axis=1).astype(o_ref.dtype)


def _attn_ctx_kernel(sink_ref, q_ref, k_ref, v_ref, o_ref):
    scale = HEAD_DIM ** -0.5
    g = A_HEADS // A_KV_HEADS
    outs = []
    for h in range(A_HEADS):
        kh = h // g
        q = q_ref[:, HEAD_DIM * h:HEAD_DIM * (h + 1)]
        k = k_ref[:, HEAD_DIM * kh:HEAD_DIM * (kh + 1)]
        v = v_ref[:, HEAD_DIM * kh:HEAD_DIM * (kh + 1)]
        s = _dot_nt(q, k) * scale
        outs.append(_softmax_pv([s], [v], sink_ref[h]))
    o_ref[...] = jnp.concatenate(outs, axis=1).astype(o_ref.dtype)


def _attn_lat_kernel(sink_ref, q_ref, kp_ref, kc_ref, kn_ref, vp_ref, vc_ref, vn_ref, kx_ref, vx_ref, o_ref):
    n = pl.program_id(1)
    nb = DEC_SEQ // A_WINDOW
    i = lax.broadcasted_iota(jnp.int32, (A_WINDOW, A_WINDOW), 0)
    j = lax.broadcasted_iota(jnp.int32, (A_WINDOW, A_WINDOW), 1)
    zero = jnp.zeros((A_WINDOW, A_WINDOW), F32)
    bias = jnp.concatenate([jnp.where((j >= i) & (n > 0), 0.0, NEG_INF), zero,
                            jnp.where((j <= i) & (n < nb - 1), 0.0, NEG_INF),
                            jnp.zeros((A_WINDOW, PAST_LEN), F32)], axis=1)

    def rows(p_ref, c_ref, n_ref, x_ref):
        def of(kh):
            sl = slice(HEAD_DIM * kh, HEAD_DIM * (kh + 1))
            return jnp.concatenate([p_ref[:, sl], c_ref[:, sl], n_ref[:, sl], x_ref[0, :, sl]], axis=0)
        return of

    _gqa_with_sink(sink_ref, q_ref[...] * HEAD_DIM ** -0.5, rows(kp_ref, kc_ref, kn_ref, kx_ref),
                   rows(vp_ref, vc_ref, vn_ref, vx_ref), bias, o_ref)


def _attention_a(q, k, v, cache_k, cache_v, sink):
    smem = pl.BlockSpec(memory_space=pltpu.SMEM)
    o = pl.pallas_call(
        _attn_ctx_kernel,
        grid=(BATCH,),
        in_specs=[smem, pl.BlockSpec((SEQ, A_WIDTH), lambda b: (b, 0)),
                  pl.BlockSpec((SEQ, A_KV_WIDTH), lambda b: (b, 0)), pl.BlockSpec((SEQ, A_KV_WIDTH), lambda b: (b, 0))],
        out_specs=pl.BlockSpec((SEQ, A_WIDTH), lambda b: (b, 0)),
        out_shape=jax.ShapeDtypeStruct((N_CTX_TOK, A_WIDTH), BF16),
        compiler_params=_cparams("arbitrary"),
        name="attn_a_ctx",
    )(sink, q, k, v)
    nb = DEC_SEQ // A_WINDOW
    base = N_CTX_TOK // A_WINDOW
    cur = lambda b, n: (base + b * nb + n, 0)
    prev = lambda b, n: (base + b * nb + jnp.maximum(n - 1, 0), 0)
    nxt = lambda b, n: (base + b * nb + jnp.minimum(n + 1, nb - 1), 0)
    kv = lambda f: pl.BlockSpec((A_WINDOW, A_KV_WIDTH), f)
    cache = pl.BlockSpec((1, PAST_LEN, A_KV_WIDTH), lambda b, n: (b, 0, 0))
    o_lat = pl.pallas_call(
        _attn_lat_kernel,
        grid=(DEC_BATCH, nb),
        in_specs=[smem, pl.BlockSpec((A_WINDOW, A_WIDTH), cur), kv(prev), kv(cur), kv(nxt), kv(prev), kv(cur), kv(nxt),
                  cache, cache],
        out_specs=pl.BlockSpec((A_WINDOW, A_WIDTH), lambda b, n: (b * nb + n, 0)),
        out_shape=jax.ShapeDtypeStruct((N_LAT_TOK, A_WIDTH), BF16),
        compiler_params=_cparams("arbitrary", "arbitrary"),
        name="attn_a_lat",
    )(sink, q, k, k, k, v, v, v, cache_k, cache_v)
    return o, o_lat


S5_CHUNK = 16
S5_OCT = 128 // S5_GROUP
S5_NOCT = S5_GROUPS // S5_OCT
S5_K = S5_CHUNK * 128
S5_PART = S5_OCT * S5_STATE
S5_SW = 4 * S5_PART
S5_ROWS_CTX = BATCH * SEQ // S5_CHUNK
S5_ROWS_LAT = DEC_BATCH * DEC_SEQ // S5_CHUNK
S5_ROWS = S5_ROWS_CTX + S5_ROWS_LAT
S5_NB = 4
S5_TP = 4
HI = lax.Precision.HIGHEST


def _s5_disc_kernel(lr_ref, li_ref, ldt_ref, ar_ref, ai_ref, zr_ref, zi_ref):
    lr, li = lr_ref[...], li_ref[...]
    dt = jnp.exp(ldt_ref[...])
    mag = jnp.exp(lr * dt)
    ar, ai = mag * jnp.cos(li * dt), mag * jnp.sin(li * dt)
    den = lr * lr + li * li
    ar_ref[...] = ar
    ai_ref[...] = ai
    zr_ref[...] = ((ar - 1.0) * lr + ai * li) / den
    zi_ref[...] = (ai * lr - (ar - 1.0) * li) / den


def _cmul(xr, xi, yr, yi):
    return xr * yr - xi * yi, xr * yi + xi * yr


def _dot_nt_hi(a, b):
    return lax.dot_general(a, b, (((1,), (1,)), ((), ())), precision=HI, preferred_element_type=F32)


def _s5_prep_kernel(ar_ref, ai_ref, zr_ref, zi_ref, btr_ref, bti_ref, ctr_ref, cti_ref,
                    m_ref, win_ref, wout_ref, a16_ref, pw_s, w_s, k_s):
    npw = S5_CHUNK + 1
    blk = lambda j: pl.ds(pl.multiple_of(j * 128, 128), 128)

    @pl.when(pl.program_id(1) == 0)
    def _():
        kd = []
        for d in range(2):
            ar, ai = ar_ref[0, d], ai_ref[0, d]
            pr, pi = jnp.ones_like(ar), jnp.zeros_like(ar)
            bbr, bbi = _cmul(zr_ref[0, d], zi_ref[0, d], btr_ref[0, d], bti_ref[0, d])
            for j in range(npw):
                pw_s[d, 0, j:j + 1, :] = pr
                pw_s[d, 1, j:j + 1, :] = pi
                if j < S5_CHUNK:
                    wr, wi = _cmul(pr, pi, bbr, bbi)
                    w_s[d, 0, 128 * j:128 * (j + 1), :] = wr.astype(BF16)
                    w_s[d, 1, 128 * j:128 * (j + 1), :] = wi.astype(BF16)
                pr, pi = _cmul(pr, pi, ar, ai)
            kd.append(_dot_nt(w_s[d, 0], ctr_ref[0, d]) - _dot_nt(w_s[d, 1], cti_ref[0, d]))
        for jj in range(2 * S5_CHUNK - 1):
            j = jj - (S5_CHUNK - 1)
            if j > 0:
                k = kd[0][128 * j:128 * (j + 1)]
            elif j < 0:
                k = kd[1][128 * -j:128 * (1 - j)]
            else:
                k = kd[0][0:128] + kd[1][0:128]
            k_s[128 * jj:128 * (jj + 1), :] = k
        a16_ref[0] = jnp.concatenate([pw_s[0, 0, S5_CHUNK:npw, :], pw_s[1, 0, S5_CHUNK:npw, :],
                                      pw_s[0, 1, S5_CHUNK:npw, :], pw_s[1, 1, S5_CHUNK:npw, :]], axis=1)

    def power(d, e):
        return pw_s[d, 0, pl.ds(e, 1), :], pw_s[d, 1, pl.ds(e, 1), :]

    for tt in range(S5_TP):
        t = pl.program_id(1) * S5_TP + tt
        rows = slice(128 * tt, 128 * (tt + 1))
        for tp in range(S5_CHUNK):
            m_ref[0, rows, 128 * tp:128 * (tp + 1)] = k_s[blk(S5_CHUNK - 1 + tp - t), :].astype(BF16)
        for d in range(2):
            j = (S5_CHUNK - 1 - t) if d == 0 else t
            win_ref[0, rows, S5_PART * d:S5_PART * (d + 1)] = w_s[d, 0, blk(j), :]
            win_ref[0, rows, S5_PART * (2 + d):S5_PART * (3 + d)] = w_s[d, 1, blk(j), :]
            er, ei = _cmul(*power(d, (t + 1) if d == 0 else (S5_CHUNK - t)), ctr_ref[0, d], cti_ref[0, d])
            wout_ref[0, rows, S5_PART * d:S5_PART * (d + 1)] = er.astype(BF16)
            wout_ref[0, rows, S5_PART * (2 + d):S5_PART * (3 + d)] = (-ei).astype(BF16)


def _s5_main_kernel(u_ref, win_ref, m_ref, wout_ref, a16_ref, h0_ref, y_ref, hfin_ref, uo_s, x_s, hs_s):
    s = pl.program_id(1)
    nq = S5_PART // 128

    @pl.when(s == 0)
    def _():
        for t in range(S5_CHUNK):
            uo_s[:, 128 * t:128 * (t + 1)] = u_ref[pl.ds(t, S5_ROWS, stride=S5_CHUNK), :].astype(BF16)

    @pl.when(s < S5_NB)
    def _():
        x = jnp.dot(uo_s[...], win_ref[0], preferred_element_type=F32)
        for q in range(nq):
            x_s[s * nq + q] = x[:, 128 * q:128 * (q + 1)]

    @pl.when(s == S5_NB - 1)
    def _():
        def run(row0, nb, nc, h):
            for c in range(nc):
                for d in range(2):
                    cc = c if d == 0 else nc - 1 - c
                    rows = pl.ds(row0 + cc, nb, stride=nc)
                    for q in range(nq):
                        kr, ki = d * nq + q, (2 + d) * nq + q
                        hr, hi = h[d][0][q], h[d][1][q]
                        hs_s[kr, rows, :] = hr
                        hs_s[ki, rows, :] = hi
                        ar = a16_ref[0, :, 128 * kr:128 * (kr + 1)]
                        ai = a16_ref[0, :, 128 * ki:128 * (ki + 1)]
                        h[d][0][q] = ar * hr - ai * hi + x_s[kr, rows, :]
                        h[d][1][q] = ar * hi + ai * hr + x_s[ki, rows, :]
            return h

        zero = jnp.zeros((BATCH, 128), F32)
        fin = run(0, BATCH, SEQ // S5_CHUNK, [[[zero] * nq, [zero] * nq] for _ in range(2)])
        for d in range(2):
            for ri in range(2):
                for q in range(nq):
                    k = (2 * ri + d) * nq + q
                    hfin_ref[0, :, 128 * k:128 * (k + 1)] = fin[d][ri][q]
        h0 = [[[h0_ref[0, :, 128 * ((2 * ri + d) * nq + q):128 * ((2 * ri + d) * nq + q + 1)] for q in range(nq)]
               for ri in range(2)] for d in range(2)]
        run(S5_ROWS_CTX, DEC_BATCH, DEC_SEQ // S5_CHUNK, h0)

    @pl.when(s >= S5_NB)
    def _():
        hs = jnp.concatenate([hs_s[k] for k in range(4 * nq)], axis=1).astype(BF16)
        y = (jnp.dot(uo_s[...], m_ref[0], preferred_element_type=F32)
             + lax.dot_general(hs, wout_ref[0], (((1,), (1,)), ((), ())), preferred_element_type=F32))
        for q in range(nq):
            t = (s - S5_NB) * nq + q
            y_ref[pl.ds(t, S5_ROWS, stride=S5_CHUNK), :] = y[:, 128 * q:128 * (q + 1)]


def _s5_octets(t, lanes):
    return t.reshape(2, S5_NOCT, 1, S5_OCT * lanes).transpose(1, 0, 2, 3)


def _s5_blockdiag(t):
    a, n = t.shape[2], t.shape[3]
    t = t.reshape(2, S5_NOCT, S5_OCT, a, n)
    bd = jnp.einsum('dogan,gh->dogahn', t, jnp.eye(S5_OCT, dtype=t.dtype))
    return bd.reshape(2, S5_NOCT, S5_OCT * a, S5_OCT * n).transpose(1, 0, 2, 3)


def _s5_state_to_lanes(h_re, h_im):
    b = h_re.shape[0]
    parts = jnp.stack([h_re[:, 0], h_re[:, 1], h_im[:, 0], h_im[:, 1]], axis=1)
    parts = parts.reshape(b, 4, S5_NOCT, S5_PART).transpose(2, 0, 1, 3)
    return parts.reshape(S5_NOCT, b, S5_SW)


def _s5_state_from_lanes(h):
    b = h.shape[1]
    parts = h.reshape(S5_NOCT, b, 4, S5_OCT, S5_STATE).transpose(1, 2, 0, 3, 4).reshape(b, 4, S5_GROUPS, S5_STATE)
    return parts[:, 0:2], parts[:, 2:4]


def _s5_scan(u, h0_re, h0_im, lam_re, lam_im, log_dt, b_re, b_im, c_re, c_im):
    ng, n = S5_GROUPS, S5_STATE
    rows = 2 * ng
    disc = pl.pallas_call(
        _s5_disc_kernel,
        out_shape=[jax.ShapeDtypeStruct((rows, n), F32)] * 4,
        name="s5_disc",
    )(lam_re.reshape(rows, n), lam_im.reshape(rows, n), log_dt.reshape(rows, 1))
    ar, ai, zr, zi = [_s5_octets(t.reshape(2, ng, n), n) for t in disc]
    bt = lambda t: _s5_blockdiag(t.transpose(0, 1, 3, 2))
    vec = pl.BlockSpec((1, 2, 1, S5_PART), lambda o, t: (o, 0, 0, 0))
    mat = pl.BlockSpec((1, 2, 128, S5_PART), lambda o, t: (o, 0, 0, 0))
    rowblk = lambda w: pl.BlockSpec((1, 128 * S5_TP, w), lambda o, t: (o, t, 0))
    m, win, wout, a16 = pl.pallas_call(
        _s5_prep_kernel,
        grid=(S5_NOCT, S5_CHUNK // S5_TP),
        in_specs=[vec, vec, vec, vec, mat, mat, mat, mat],
        out_specs=[rowblk(S5_K), rowblk(S5_SW), rowblk(S5_SW), pl.BlockSpec((1, 1, S5_SW), lambda o, t: (o, 0, 0))],
        out_shape=[jax.ShapeDtypeStruct((S5_NOCT, S5_K, S5_K), BF16), jax.ShapeDtypeStruct((S5_NOCT, S5_K, S5_SW), BF16),
                   jax.ShapeDtypeStruct((S5_NOCT, S5_K, S5_SW), BF16), jax.ShapeDtypeStruct((S5_NOCT, 1, S5_SW), F32)],
        scratch_shapes=[pltpu.VMEM((2, 2, 24, S5_PART), F32), pltpu.VMEM((2, 2, S5_K, S5_PART), BF16),
                        pltpu.VMEM(((2 * S5_CHUNK - 1) * 128, 128), F32)],
        compiler_params=_cparams("arbitrary", "arbitrary"),
        name="s5_prep",
    )(ar, ai, zr, zi, bt(b_re), bt(b_im), _s5_blockdiag(c_re), _s5_blockdiag(c_im))
    nb = S5_NB
    y, hfin = pl.pallas_call(
        _s5_main_kernel,
        grid=(S5_NOCT, 2 * nb),
        in_specs=[pl.BlockSpec((N_TOK, 128), lambda o, s: (0, o)),
                  pl.BlockSpec((1, S5_K, S5_PART), lambda o, s: (o, 0, jnp.minimum(s, nb - 1))),
                  pl.BlockSpec((1, S5_K, S5_PART), lambda o, s: (o, 0, jnp.maximum(s - nb, 0))),
                  pl.BlockSpec((1, S5_PART, S5_SW), lambda o, s: (o, jnp.maximum(s - nb, 0), 0)),
                  pl.BlockSpec((1, 1, S5_SW), lambda o, s: (o, 0, 0)),
                  pl.BlockSpec((1, DEC_BATCH, S5_SW), lambda o, s: (o, 0, 0))],
        out_specs=[pl.BlockSpec((N_TOK, 128), lambda o, s: (0, o)),
                   pl.BlockSpec((1, BATCH, S5_SW), lambda o, s: (o, 0, 0))],
        out_shape=[jax.ShapeDtypeStruct((N_TOK, S5_WIDTH), F32), jax.ShapeDtypeStruct((S5_NOCT, BATCH, S5_SW), F32)],
        scratch_shapes=[pltpu.VMEM((S5_ROWS, S5_K), BF16), pltpu.VMEM((S5_SW // 128, S5_ROWS, 128), F32),
                        pltpu.VMEM((S5_SW // 128, S5_ROWS, 128), F32)],
        compiler_params=_cparams("arbitrary", "arbitrary"),
        name="s5_main",
    )(u, win, m, wout, a16, _s5_state_to_lanes(h0_re, h0_im))
    fin_re, fin_im = _s5_state_from_lanes(hfin)
    return y, fin_re, fin_im


def _ctx_or_lat(ctx_ref, lat_ref):
    return jnp.where(pl.program_id(0) < N_CTX_BLK, ctx_ref[...], lat_ref[...])


def _split_specs(width):
    return [pl.BlockSpec((ROW_BLK, width), lambda i: (jnp.minimum(i, N_CTX_BLK - 1), 0)),
            pl.BlockSpec((ROW_BLK, width), lambda i: (jnp.maximum(i - N_CTX_BLK, 0), 0))]


def _ab_out_kernel(xc_ref, xl_ref, oac_ref, oal_ref, y_ref, u_ref, d_ref, wglu_ref, wout_ref, gt_ref, o_ref,
                   wglu_s, wa_s, wb_s):
    @pl.when(pl.program_id(0) == 0)
    def _():
        wglu_s[...] = wglu_ref[...].astype(BF16)
        wa_s[...] = wout_ref[0:A_WIDTH, :].astype(BF16)
        wb_s[...] = wout_ref[A_WIDTH:, :].astype(BF16)

    g = jax.nn.gelu(y_ref[...] + d_ref[...] * u_ref[...])
    ob = g * jax.nn.sigmoid(jnp.dot(g.astype(BF16), wglu_s[...], preferred_element_type=F32))
    out = (jnp.dot(_ctx_or_lat(oac_ref, oal_ref).astype(BF16), wa_s[...], preferred_element_type=F32)
           + jnp.dot(ob.astype(BF16), wb_s[...], preferred_element_type=F32))
    o_ref[...] = _ctx_or_lat(xc_ref, xl_ref) + gt_ref[0] * out


def _ab_out(x, oa, y, u, d_skip, w_glu, w_out, mod, layer):
    row = lambda n: pl.BlockSpec((ROW_BLK, n), lambda i: (i, 0))
    full = lambda a, b: pl.BlockSpec((a, b), lambda i: (0, 0), pipeline_mode=pl.Buffered(1))
    return pl.pallas_call(
        _ab_out_kernel,
        grid=(N_ROW_BLK,),
        in_specs=_split_specs(D_MODEL) + _split_specs(A_WIDTH) + [row(S5_WIDTH), row(S5_WIDTH), full(1, S5_WIDTH),
                  full(S5_WIDTH, S5_WIDTH), full(A_WIDTH + S5_WIDTH, D_MODEL), _mod_spec(layer, 2)],
        out_specs=row(D_MODEL),
        out_shape=jax.ShapeDtypeStruct((N_TOK, D_MODEL), F32),
        scratch_shapes=[pltpu.VMEM((S5_WIDTH, S5_WIDTH), BF16), pltpu.VMEM((A_WIDTH, D_MODEL), BF16),
                        pltpu.VMEM((S5_WIDTH, D_MODEL), BF16)],
        compiler_params=_cparams("arbitrary"),
        name="ab_out",
    )(x[0], x[1], oa[0], oa[1], y, u, d_skip.reshape(1, S5_WIDTH), w_glu, w_out, mod)


CD_HG0 = MLA_Q_RANK + MLA_KV_RANK + MLA_ROPE
KPE_LANES = 128


MLA_TILE = 128
MLA_QK = MLA_HEADS * MLA_TILE


def _mla_head_tiles(w, a):
    k, n = w.shape
    w = w.reshape(k, MLA_HEADS, n // MLA_HEADS)
    tiles = jnp.pad(w[:, :, :a], ((0, 0), (0, 0), (0, MLA_TILE - a)))
    return jnp.concatenate([tiles.reshape(k, -1), w[:, :, a:].reshape(k, -1)], axis=1)


def _mla_key_tiles(k_tiles, kpe):
    lane = lax.broadcasted_iota(jnp.int32, (1, MLA_TILE), 1)
    pe = jnp.where((lane >= MLA_NOPE) & (lane < MLA_NOPE + MLA_ROPE), kpe, 0.0)
    return k_tiles + jnp.concatenate([pe] * MLA_HEADS, axis=1)


def _cd_in_kernel(x_ref, y_ref, gp_ref, g_ref, sc_ref, sh_ref, w_ref, gq_ref, wqu_ref, gkv_ref, wkvu_ref,
                  cos_ref, sa_ref, sb_ref, cq_ref, sq_ref, tq_ref,
                  x1_ref, qf_ref, ckv_ref, kpe_ref, kf_ref, vm_ref, hq_ref, hff_ref, hfb_ref, hi_ref, hg_ref):
    dot = lambda a, b: jnp.dot(a, b, preferred_element_type=F32)
    c0, c1, c2 = MLA_Q_RANK, MLA_Q_RANK + MLA_KV_RANK, MLA_Q_RANK + MLA_KV_RANK + KPE_LANES
    x = x_ref[...] + gp_ref[0] * y_ref[...]
    x1_ref[...] = x
    h = _norm_mod(x, g_ref[...], sc_ref[0], sh_ref[0]).astype(BF16)
    cq = _rms(dot(h, w_ref[:, 0:c0])) * gq_ref[...]
    qq = dot(cq.astype(BF16), wqu_ref[...])
    qf_ref[...] = _apply_rope(qq, cq_ref[...], sq_ref[...], tq_ref[...], MLA_ROPE // 4).astype(qf_ref.dtype)
    ckv = _rms(dot(h, w_ref[:, c0:c1])) * gkv_ref[...]
    ckv_ref[...] = ckv
    kv = dot(ckv.astype(BF16), wkvu_ref[...])
    kpe = _apply_rope(dot(h, w_ref[:, c1:c2]), cos_ref[...], sa_ref[...], sb_ref[...], MLA_ROPE // 4)
    kpe_ref[...] = kpe
    kf_ref[...] = _mla_key_tiles(kv[:, 0:MLA_QK], kpe).astype(kf_ref.dtype)
    vm_ref[...] = kv[:, MLA_QK:].astype(vm_ref.dtype)
    for n, ref in enumerate((hq_ref, hff_ref, hfb_ref, hi_ref, hg_ref)):
        ref[...] = dot(h, w_ref[:, c2 + HG_KD * n:c2 + HG_KD * (n + 1)])


def _cd_in(x, y_prev, mod, g, w, g_q, w_q_up, g_kv, w_kv_up, layer):
    cos, sa, sb = _rope_tables(MLA_ROPE)
    cos_q, sa_q, sb_q = _rope_tables(MLA_ROPE, only=(MLA_NOPE, MLA_NOPE + MLA_ROPE))
    c1 = MLA_Q_RANK + MLA_KV_RANK
    w = jnp.concatenate([w[:, :c1], jnp.tile(w[:, c1:CD_HG0], (1, KPE_LANES // MLA_ROPE)), w[:, CD_HG0:]],
                        axis=1).astype(BF16)
    d_in = w.shape[1]
    row = lambda n: pl.BlockSpec((ROW_BLK, n), lambda i: (i, 0))
    full = lambda a, b: pl.BlockSpec((a, b), lambda i: (0, 0), pipeline_mode=pl.Buffered(1))
    rope = pl.BlockSpec((ROW_BLK, 128), lambda i: (_rope_blk(i), 0))
    outs = ([(D_MODEL, F32), (MLA_QK, BF16), (MLA_KV_RANK, F32), (KPE_LANES, F32), (MLA_QK, BF16), (MLA_WIDTH, BF16)]
            + [(HG_KD, F32)] * 5)
    nkv = MLA_QK + MLA_WIDTH
    return pl.pallas_call(
        _cd_in_kernel,
        grid=(N_ROW_BLK,),
        in_specs=[row(D_MODEL), row(D_MODEL), _mod_spec(layer - 1, 5),
                  full(1, D_MODEL), _mod_spec(layer, 1), _mod_spec(layer, 0), full(D_MODEL, d_in),
                  full(1, MLA_Q_RANK), full(MLA_Q_RANK, MLA_QK), full(1, MLA_KV_RANK), full(MLA_KV_RANK, nkv),
                  rope, rope, rope, rope, rope, rope],
        out_specs=[row(n) for n, _ in outs],
        out_shape=[jax.ShapeDtypeStruct((N_TOK, n), dt) for n, dt in outs],
        compiler_params=_cparams("arbitrary"),
        name="cd_in",
    )(x, y_prev, mod, g.reshape(1, D_MODEL), mod, mod, w, g_q.reshape(1, -1),
      _mla_head_tiles(w_q_up, MLA_NOPE + MLA_ROPE).astype(BF16), g_kv.reshape(1, -1),
      _mla_head_tiles(w_kv_up, MLA_NOPE).astype(BF16), cos, sa, sb, cos_q, sa_q, sb_q)


def _mla_cache_kernel(c_ref, kpe_ref, w_ref, kf_ref, v_ref):
    kv = _dot(c_ref[...], w_ref[...])
    kf_ref[...] = _mla_key_tiles(kv[:, 0:MLA_QK], kpe_ref[...]).astype(kf_ref.dtype)
    v_ref[...] = kv[:, MLA_QK:].astype(v_ref.dtype)


def _mla_cache_kv(cckv, ckpe, w_kv_up):
    rows = cckv.shape[0]
    blk = lambda w: pl.BlockSpec((PAST_LEN, w), lambda i: (i, 0))
    return pl.pallas_call(
        _mla_cache_kernel,
        grid=(rows // PAST_LEN,),
        in_specs=[blk(MLA_KV_RANK), blk(KPE_LANES), pl.BlockSpec((MLA_KV_RANK, MLA_QK + MLA_WIDTH), lambda i: (0, 0))],
        out_specs=[blk(MLA_QK), blk(MLA_WIDTH)],
        out_shape=[jax.ShapeDtypeStruct((rows, MLA_QK), BF16), jax.ShapeDtypeStruct((rows, MLA_WIDTH), BF16)],
        compiler_params=_cparams("arbitrary"),
        name="mla_cache_kv",
    )(cckv, jnp.tile(ckpe, (1, KPE_LANES // MLA_ROPE)), _mla_head_tiles(w_kv_up, MLA_NOPE))


def _mla_heads(qf, keys, o_ref):
    c = (MLA_NOPE + MLA_ROPE) ** -0.5 * math.log2(math.e)
    outs = []
    for h in range(MLA_HEADS):
        t = slice(MLA_TILE * h, MLA_TILE * (h + 1))
        a = slice(MLA_V * h, MLA_V * (h + 1))
        s_list = [_dot_nt(qf[:, t], kf[:, t]) for kf, _ in keys]
        m = functools.reduce(jnp.maximum, [jnp.max(s, axis=-1, keepdims=True) for s in s_list])
        ps = [jnp.exp2((s - m) * c) for s in s_list]
        l = functools.reduce(jnp.add, [jnp.sum(p, axis=-1, keepdims=True) for p in ps])
        o = functools.reduce(jnp.add, [_dot(p, v[:, a]) for p, (_, v) in zip(ps, keys)])
        outs.append(o / l)
    o_ref[...] = jnp.concatenate(outs, axis=1).astype(o_ref.dtype)


def _mla_ctx_kernel(qf_ref, kf_ref, v_ref, o_ref):
    _mla_heads(qf_ref[...], [(kf_ref[...], v_ref[...])], o_ref)


def _mla_lat_kernel(qf_ref, kf_ref, v_ref, kcf_ref, vc_ref, o_ref):
    _mla_heads(qf_ref[...], [(kcf_ref[...], vc_ref[...]), (kf_ref[...], v_ref[...])], o_ref)


def _attention_mla(qf, kf, vm, kcf, vc):
    blk = lambda w: pl.BlockSpec((SEQ, w), lambda b: (b, 0))
    o = pl.pallas_call(
        _mla_ctx_kernel,
        grid=(BATCH,),
        in_specs=[blk(MLA_QK), blk(MLA_QK), blk(MLA_WIDTH)],
        out_specs=blk(MLA_WIDTH),
        out_shape=jax.ShapeDtypeStruct((N_CTX_TOK, MLA_WIDTH), BF16),
        compiler_params=_cparams("arbitrary"),
        name="mla_ctx",
    )(qf, kf, vm)
    nq = DEC_SEQ // MLA_QBLK
    qblk = lambda w: pl.BlockSpec((MLA_QBLK, w), lambda b, i: (N_CTX_TOK // MLA_QBLK + b * nq + i, 0))
    seq = lambda w: pl.BlockSpec((DEC_SEQ, w), lambda b, i: (N_CTX_TOK // DEC_SEQ + b, 0))
    past = lambda w: pl.BlockSpec((PAST_LEN, w), lambda b, i: (b, 0))
    o_lat = pl.pallas_call(
        _mla_lat_kernel,
        grid=(DEC_BATCH, nq),
        in_specs=[qblk(MLA_QK), seq(MLA_QK), seq(MLA_WIDTH), past(MLA_QK), past(MLA_WIDTH)],
        out_specs=pl.BlockSpec((MLA_QBLK, MLA_WIDTH), lambda b, i: (b * nq + i, 0)),
        out_shape=jax.ShapeDtypeStruct((N_LAT_TOK, MLA_WIDTH), BF16),
        compiler_params=_cparams("arbitrary", "arbitrary"),
        name="mla_lat",
    )(qf, kf, vm, kcf, vc)
    return o, o_lat


HG_TILE = 128
HG_NC = HG_TILE // HG_CHUNK
HG_HALF = 256


HG_SLABS = HG_KD // 128


def _hg_token_plane(raw_s, p):
    return jnp.concatenate([raw_s[j, pl.ds(p, HG_NC, stride=HG_CHUNK), :] for j in range(HG_SLABS)], axis=1)


def _hg_put(ref, x):
    for j in range(HG_SLABS):
        ref[j] = x[:, 128 * j:128 * (j + 1)]


def _hg_get(ref, c, l):
    rows = pl.ds(c, HG_CHUNK, stride=HG_NC)
    return jnp.concatenate([ref[2 * l, rows, :], ref[2 * l + 1, rows, :]], axis=1)


def _hg_plane(ref, p):
    return jnp.concatenate([ref[j, HG_NC * p:HG_NC * (p + 1), :] for j in range(HG_SLABS)], axis=1)


def _hg_put_plane(ref, p, x):
    for j in range(HG_SLABS):
        ref[j, HG_NC * p:HG_NC * (p + 1), :] = x[:, 128 * j:128 * (j + 1)]


HG_PAIRS = HG_CHUNK * (HG_CHUNK + 1) // 2
HG_PAIR_ROWS = HG_PAIRS * HG_NC


def _hg_pair_rows(d, i):
    n = d * HG_CHUNK - d * (d - 1) // 2 + (i - d)
    return slice(HG_NC * n, HG_NC * (n + 1))


def _hg_direction(hq_ref, hf_ref, hi_ref, o_ref, lb, st_ref, raw_s, q_s, f_s, kk_s, v_s, qt_s, kt_s, o_s, p_s, e_s,
                  sign, ones_bd, head_mask):
    _hg_put(raw_s, hq_ref[...])
    for p in range(HG_CHUNK):
        hq = _hg_token_plane(raw_s, p)
        _hg_put_plane(q_s, p, hq * jax.nn.sigmoid(hq))
    _hg_put(raw_s, hf_ref[...])
    for p in range(HG_CHUNK):
        f = lb + (1.0 - lb) * jax.nn.sigmoid(_hg_token_plane(raw_s, p))
        _hg_put_plane(f_s, p, f)
        _hg_put_plane(kk_s, p, 1.0 - f)
    _hg_put(raw_s, hi_ref[...])
    for p in range(HG_CHUNK):
        _hg_put_plane(v_s, p, _hg_token_plane(raw_s, p))
    pos = (lambda i: i) if sign > 0 else (lambda i: HG_CHUNK - 1 - i)
    plane = lambda ref, i: _hg_plane(ref, pos(i))

    for i in range(HG_CHUNK):
        q = plane(q_s, i)
        p_s[_hg_pair_rows(0, i), :] = q * plane(kk_s, i)
        dec = None
        for d in range(1, i + 1):
            fd = plane(f_s, i - d + 1)
            dec = fd if dec is None else dec * fd
            p_s[_hg_pair_rows(d, i), :] = q * plane(kk_s, i - d) * dec
    step = HG_PAIR_ROWS // 4
    for c in range(0, HG_PAIR_ROWS, step):
        pb = p_s[c:c + step, :].astype(BF16)
        e_s[c:c + step, 0:HG_HALF] = jnp.dot(pb[:, 0:HG_HALF], ones_bd, preferred_element_type=F32)
        e_s[c:c + step, HG_HALF:] = jnp.dot(pb[:, HG_HALF:], ones_bd, preferred_element_type=F32)
    for i in range(HG_CHUNK):
        o = None
        for d in range(i + 1):
            t = e_s[_hg_pair_rows(d, i), :] * plane(v_s, i - d)
            o = t if o is None else o + t
        _hg_put_plane(o_s, pos(i), o)

    incl = None
    for i in range(HG_CHUNK):
        fi = plane(f_s, i)
        incl = fi if incl is None else incl * fi
        _hg_put_plane(qt_s, pos(i), plane(q_s, i) * incl)
    whole = incl
    excl = None
    for i in range(HG_CHUNK - 1, -1, -1):
        kt = plane(kk_s, i)
        if excl is not None:
            kt = kt * excl
        _hg_put_plane(kt_s, pos(i), kt)
        fi = plane(f_s, i)
        excl = fi if excl is None else excl * fi

    tile4 = lambda x: jnp.where(head_mask, jnp.concatenate([x] * 4, axis=0), 0.0).astype(BF16)
    for cc in range(HG_NC):
        c = cc if sign > 0 else HG_NC - 1 - cc
        r = slice(HG_CHUNK * c, HG_CHUNK * (c + 1))
        for g in range(2):
            l = slice(HG_HALF * g, HG_HALF * (g + 1))
            st = st_ref[g]
            oc = _dot_nt(tile4(_hg_get(qt_s, c, g)), st)
            o_ref[r, l] = _hg_get(o_s, c, g) + jnp.concatenate(
                [oc[HG_CHUNK * h:HG_CHUNK * (h + 1), :] for h in range(4)], axis=1)
            v = hi_ref[r, l]
            vs = jnp.concatenate([v[:, HG_VAL * h:HG_VAL * (h + 1)] for h in range(4)], axis=0)
            ds = lax.dot_general(vs.astype(BF16), tile4(_hg_get(kt_s, c, g)), (((0,), (0,)), ((), ())),
                                 preferred_element_type=F32)
            st_ref[g] = st * whole[c:c + 1, l] + ds


def _hg_lower_bound(lb_ref, d, layer):
    raw = [lb_ref[d * DEPTH + m:d * DEPTH + m + 1, :] for m in range(DEPTH)]
    mx = functools.reduce(jnp.maximum, raw)
    e = [jnp.exp(r - mx) for r in raw]
    tot = functools.reduce(jnp.add, e)
    return functools.reduce(jnp.add, e[1:layer + 1], jnp.zeros_like(tot)) / tot


def _hgrn_kernel(nt, layer, hqf_ref, hff_ref, hif_ref, hqb_ref, hfb_ref, hib_ref, lb_ref, s0_ref,
                 of_ref, ob_ref, sfin_ref, st_s, raw_s, q_s, f_s, kk_s, v_s, qt_s, kt_s, o_s, p_s, e_s):
    i = pl.program_id(1)

    @pl.when(i == 0)
    def _():
        st_s[...] = s0_ref[0]

    ri = lax.broadcasted_iota(jnp.int32, (HG_HALF, HG_HALF), 0) // HG_KEY
    ci = lax.broadcasted_iota(jnp.int32, (HG_HALF, HG_HALF), 1) // HG_KEY
    ones_bd = jnp.where(ri == ci, 1.0, 0.0).astype(BF16)
    head_mask = (lax.broadcasted_iota(jnp.int32, (4 * HG_CHUNK, HG_HALF), 0) // HG_CHUNK
                 == lax.broadcasted_iota(jnp.int32, (4 * HG_CHUNK, HG_HALF), 1) // HG_KEY)
    _hg_direction(hqf_ref, hff_ref, hif_ref, of_ref, _hg_lower_bound(lb_ref, 0, layer), st_s.at[0],
                  raw_s, q_s, f_s, kk_s, v_s, qt_s, kt_s, o_s, p_s, e_s, 1, ones_bd, head_mask)
    _hg_direction(hqb_ref, hfb_ref, hib_ref, ob_ref, _hg_lower_bound(lb_ref, 1, layer), st_s.at[1],
                  raw_s, q_s, f_s, kk_s, v_s, qt_s, kt_s, o_s, p_s, e_s, -1, ones_bd, head_mask)

    @pl.when(i == nt - 1)
    def _():
        sfin_ref[0] = st_s[...]


def _hg_state_to_blocks(s):
    b = s.shape[0]
    st = s.reshape(b, 2, 2, 4, HG_KEY, HG_VAL).transpose(0, 1, 2, 5, 3, 4)
    return st.reshape(b, 2, 2, HG_VAL, HG_HALF)


def _hg_state_from_blocks(st):
    b = st.shape[0]
    st = st.reshape(b, 2, 2, HG_VAL, 4, HG_KEY).transpose(0, 1, 2, 4, 5, 3)
    return st.reshape(b, 2, HG_HEADS, HG_KEY, HG_VAL)


def _hgrn_scan(hq, hff, hfb, hi, lb, s0, layer, row0, nseq, seqlen):
    nt = seqlen // HG_TILE
    base = row0 // HG_TILE
    fwd = pl.BlockSpec((HG_TILE, HG_KD), lambda b, i: (base + b * nt + i, 0))
    bwd = pl.BlockSpec((HG_TILE, HG_KD), lambda b, i: (base + b * nt + nt - 1 - i, 0))
    ofw = pl.BlockSpec((HG_TILE, HG_KD), lambda b, i: (b * nt + i, 0))
    obw = pl.BlockSpec((HG_TILE, HG_KD), lambda b, i: (b * nt + nt - 1 - i, 0))
    st = pl.BlockSpec((1, 2, 2, HG_VAL, HG_HALF), lambda b, i: (b, 0, 0, 0, 0))
    tile = lambda: pltpu.VMEM((HG_SLABS, HG_TILE, 128), F32)
    pairs = lambda: pltpu.VMEM((HG_PAIR_ROWS, HG_KD), F32)
    return pl.pallas_call(
        functools.partial(_hgrn_kernel, nt, layer),
        grid=(nseq, nt),
        in_specs=[fwd, fwd, fwd, bwd, bwd, bwd, pl.BlockSpec((2 * DEPTH, HG_KD), lambda b, i: (0, 0)), st],
        out_specs=[ofw, obw, st],
        out_shape=[jax.ShapeDtypeStruct((nseq * seqlen, HG_WIDTH), F32)] * 2
        + [jax.ShapeDtypeStruct((nseq, 2, 2, HG_VAL, HG_HALF), F32)],
        scratch_shapes=[pltpu.VMEM((2, 2, HG_VAL, HG_HALF), F32)] + [tile() for _ in range(8)] + [pairs(), pairs()],
        compiler_params=_cparams("arbitrary", "arbitrary"),
        name="hgrn_scan",
    )(hq, hff, hi, hq, hfb, hi, lb.reshape(2 * DEPTH, HG_KD), s0)


def _cd_out_kernel(x_ref, occ_ref, ocl_ref, ofc_ref, ofl_ref, obc_ref, obl_ref, hg_ref, go_ref, wout_ref, gt_ref,
                   o_ref, wa_s, wb_s):
    @pl.when(pl.program_id(0) == 0)
    def _():
        wa_s[...] = wout_ref[0:MLA_WIDTH, :].astype(BF16)
        wb_s[...] = wout_ref[MLA_WIDTH:, :].astype(BF16)

    ri = lax.broadcasted_iota(jnp.int32, (HG_HALF, HG_HALF), 0) // HG_VAL
    ci = lax.broadcasted_iota(jnp.int32, (HG_HALF, HG_HALF), 1) // HG_VAL
    ones_bd = jnp.where(ri == ci, 1.0, 0.0).astype(BF16)
    o = _ctx_or_lat(ofc_ref, ofl_ref) + _ctx_or_lat(obc_ref, obl_ref)
    sq = o * o
    hi = sq.astype(BF16)
    lo = (sq - hi.astype(F32)).astype(BF16)
    ms = jnp.concatenate(
        [jnp.dot(hi[:, l], ones_bd, preferred_element_type=F32) + jnp.dot(lo[:, l], ones_bd, preferred_element_type=F32)
         for l in (slice(0, HG_HALF), slice(HG_HALF, 2 * HG_HALF))], axis=1) * (1.0 / HG_VAL)
    hg = hg_ref[...]
    od = o * lax.rsqrt(ms + EPS) * go_ref[...] * (hg * jax.nn.sigmoid(hg))
    out = (jnp.dot(_ctx_or_lat(occ_ref, ocl_ref).astype(BF16), wa_s[...], preferred_element_type=F32)
           + jnp.dot(od.astype(BF16), wb_s[...], preferred_element_type=F32))
    o_ref[...] = x_ref[...] + gt_ref[0] * out


def _cd_out(x, oc, of, ob, hg, g_o, w_out, mod, layer):
    row = lambda n: pl.BlockSpec((ROW_BLK, n), lambda i: (i, 0))
    full = lambda a, b: pl.BlockSpec((a, b), lambda i: (0, 0), pipeline_mode=pl.Buffered(1))
    return pl.pallas_call(
        _cd_out_kernel,
        grid=(N_ROW_BLK,),
        in_specs=[row(D_MODEL)] + _split_specs(MLA_WIDTH) + _split_specs(HG_WIDTH) + _split_specs(HG_WIDTH)
        + [row(HG_WIDTH), full(1, HG_WIDTH), full(MLA_WIDTH + HG_WIDTH, D_MODEL), _mod_spec(layer, 2)],
        out_specs=row(D_MODEL),
        out_shape=jax.ShapeDtypeStruct((N_TOK, D_MODEL), F32),
        scratch_shapes=[pltpu.VMEM((MLA_WIDTH, D_MODEL), BF16), pltpu.VMEM((HG_WIDTH, D_MODEL), BF16)],
        compiler_params=_cparams("arbitrary"),
        name="cd_out",
    )(x, oc[0], oc[1], of[0], of[1], ob[0], ob[1], hg, jnp.tile(g_o, HG_HEADS).reshape(1, HG_WIDTH), w_out, mod)


N_PAIRS = 6
N_CLASSES = N_GROUPS * N_PAIRS
CLS_ROWS = 32
MOE_BM = 256
MOE_NBLK = N_TOK // MOE_BM + N_CLASSES
MOE_ROWS = MOE_NBLK * MOE_BM
PAIR_SLOTS = ((0, 1), (3, 1), (2, 1), (2, 0), (3, 0), (3, 2))


def _moe_route_kernel(x_ref, g_ref, sc_ref, sh_ref, rw_ref, rb_ref, h_ref, ti_ref, tw_ref, cnt_ref, base_s):
    @pl.when(pl.program_id(0) == 0)
    def _():
        base_s[...] = jnp.zeros_like(base_s)

    h = _norm_mod(x_ref[...], g_ref[...], sc_ref[0], sh_ref[0])
    logits = lax.dot_general(rw_ref[...], h, (((1,), (1,)), ((), ())), precision=HI,
                             preferred_element_type=F32)
    aff = jax.nn.sigmoid(logits)
    sel = aff + rb_ref[...]
    s = [sel[e:e + 1, :] for e in range(N_EXPERTS)]
    a = [aff[e:e + 1, :] for e in range(N_EXPERTS)]
    gs = []
    for g in range(N_GROUPS):
        m = s[4 * g:4 * g + 4]
        pairs = [m[i] + m[j] for i in range(4) for j in range(i + 1, 4)]
        gs.append(functools.reduce(jnp.maximum, pairs))
    gmax = functools.reduce(jnp.maximum, gs)
    taken = jnp.zeros_like(gmax) > 1.0
    gsel = []
    for g in range(N_GROUPS):
        hit = (gs[g] == gmax) & jnp.logical_not(taken)
        gsel.append(hit)
        taken = taken | hit
    e_lo = jnp.zeros(gmax.shape, jnp.int32)
    e_hi = jnp.zeros(gmax.shape, jnp.int32)
    a_lo = jnp.zeros_like(gmax)
    a_hi = jnp.zeros_like(gmax)
    nsel = jnp.zeros(gmax.shape, jnp.int32)
    for g in range(N_GROUPS):
        for i in range(4):
            e = 4 * g + i
            beat = jnp.zeros(gmax.shape, jnp.int32)
            for j in range(4):
                if j != i:
                    o = 4 * g + j
                    beat = beat + jnp.where((s[o] > s[e]) | ((s[o] == s[e]) & (j < i)), 1, 0)
            pick = gsel[g] & (beat < 2)
            is_first = pick & (nsel == 0)
            is_second = pick & (nsel == 1)
            e_lo = jnp.where(is_first, e, e_lo)
            a_lo = jnp.where(is_first, a[e], a_lo)
            e_hi = jnp.where(is_second, e, e_hi)
            a_hi = jnp.where(is_second, a[e], a_hi)
            nsel = nsel + jnp.where(pick, 1, 0)
    grp = e_lo // EXPERTS_PER_GROUP
    lo = e_lo - grp * EXPERTS_PER_GROUP
    hi = e_hi - grp * EXPERTS_PER_GROUP
    pair = ((lo * (7 - lo)) >> 1) + (hi - lo - 1)
    pair = jnp.where(pair == 1, 3, jnp.where(pair == 2, 4, jnp.where(pair == 3, 2, jnp.where(pair == 4, 1, pair))))
    cls = grp * N_PAIRS + pair
    wsum = a_lo + a_hi
    w_lo, w_hi = a_lo / wsum, a_hi / wsum
    onehot = (lax.broadcasted_iota(jnp.int32, (CLS_ROWS, ROW_BLK), 0) == cls).astype(F32)
    tt = lax.broadcasted_iota(jnp.int32, (ROW_BLK, ROW_BLK), 0) < lax.broadcasted_iota(jnp.int32, (ROW_BLK, ROW_BLK), 1)
    before = _dot(onehot, jnp.where(tt, 1.0, 0.0))
    base = base_s[...]
    rank = jnp.sum(onehot * (before + base[:, 0:1]), axis=0, keepdims=True).astype(jnp.int32)
    base = base + jnp.sum(onehot, axis=1, keepdims=True)
    base_s[...] = base
    cnt_ref[...] = base.astype(jnp.int32)
    ti_ref[0] = jnp.concatenate([cls, rank, e_lo, e_hi, jnp.zeros((4, ROW_BLK), jnp.int32)], axis=0)
    ident = lax.broadcasted_iota(jnp.int32, (ROW_BLK, ROW_BLK), 0) == lax.broadcasted_iota(jnp.int32, (ROW_BLK, ROW_BLK), 1)
    col = lambda r: jnp.sum(jnp.where(ident, r, 0.0), axis=1, keepdims=True)
    tw_ref[...] = jnp.where(lax.broadcasted_iota(jnp.int32, (ROW_BLK, 128), 1) < 64, col(w_lo), col(w_hi))
    h_ref[...] = h


def _moe_route(x, mod, g, router_w, router_b, layer):
    row = lambda n: pl.BlockSpec((ROW_BLK, n), lambda i: (i, 0))
    full = lambda a, b: pl.BlockSpec((a, b), lambda i: (0, 0), pipeline_mode=pl.Buffered(1))
    return pl.pallas_call(
        _moe_route_kernel,
        grid=(N_ROW_BLK,),
        in_specs=[row(D_MODEL), full(1, D_MODEL), _mod_spec(layer, 4), _mod_spec(layer, 3),
                  full(N_EXPERTS, D_MODEL), full(N_EXPERTS, 1)],
        out_specs=[row(D_MODEL), pl.BlockSpec((1, 8, ROW_BLK), lambda i: (i, 0, 0)), row(128),
                   pl.BlockSpec((CLS_ROWS, 128), lambda i: (0, 0))],
        out_shape=[jax.ShapeDtypeStruct((N_TOK, D_MODEL), F32),
                   jax.ShapeDtypeStruct((N_ROW_BLK, 8, ROW_BLK), jnp.int32),
                   jax.ShapeDtypeStruct((N_TOK, 128), F32),
                   jax.ShapeDtypeStruct((CLS_ROWS, 128), jnp.int32)],
        scratch_shapes=[pltpu.VMEM((CLS_ROWS, 128), F32)],
        compiler_params=_cparams("arbitrary"),
        name="moe_route",
    )(x, g.reshape(1, D_MODEL), mod, mod, router_w.T, router_b.reshape(N_EXPERTS, 1))


def _moe_sort_kernel(cls_ref, rank_ref, start_ref, nblk_ref, h_ref, wt_ref, hs_ref, ws_ref, pos_ref, perm_s, stage_s):
    j = pl.program_id(0)

    @pl.when(j == 0)
    def _():
        def init(r, c):
            perm_s[r] = N_TOK - 1
            return c

        lax.fori_loop(0, MOE_ROWS, init, 0, unroll=16)

        def build(t, c):
            p = start_ref[cls_ref[t]] + rank_ref[t]
            pos_ref[t] = p
            perm_s[p] = t
            return c

        lax.fori_loop(0, N_TOK, build, 0, unroll=16)

    @pl.when(j < nblk_ref[0])
    def _():
        base = j * MOE_BM
        for r in range(MOE_BM):
            src = perm_s[base + r]
            stage_s[r:r + 1, :] = h_ref[pl.ds(src, 1), :]
            ws_ref[r:r + 1, :] = wt_ref[pl.ds(src, 1), :]
        hs_ref[...] = stage_s[...].astype(BF16)

    @pl.when(j >= nblk_ref[0])
    def _():
        hs_ref[...] = jnp.zeros_like(hs_ref)
        ws_ref[...] = jnp.zeros_like(ws_ref)


def _moe_sort(h, wtok, cls, rank, start_rows, nblk):
    res = lambda w: pl.BlockSpec((N_TOK, w), lambda j, *_: (0, 0), pipeline_mode=pl.Buffered(1))
    grid_spec = pltpu.PrefetchScalarGridSpec(
        num_scalar_prefetch=4,
        grid=(MOE_NBLK,),
        in_specs=[res(D_MODEL), res(128)],
        out_specs=[pl.BlockSpec((MOE_BM, D_MODEL), lambda j, *_: (j, 0)),
                   pl.BlockSpec((MOE_BM, 128), lambda j, *_: (j, 0)),
                   pl.BlockSpec(memory_space=pltpu.SMEM)],
        scratch_shapes=[pltpu.SMEM((MOE_ROWS,), jnp.int32), pltpu.VMEM((MOE_BM, D_MODEL), F32)],
    )
    return pl.pallas_call(
        _moe_sort_kernel,
        grid_spec=grid_spec,
        out_shape=[jax.ShapeDtypeStruct((MOE_ROWS, D_MODEL), BF16), jax.ShapeDtypeStruct((MOE_ROWS, 128), F32),
                   jax.ShapeDtypeStruct((N_TOK,), jnp.int32)],
        compiler_params=_cparams("arbitrary"),
        name="moe_sort",
    )(cls, rank, start_rows, nblk, h, wtok)


def _moe_experts_kernel(elo_ref, ehi_ref, nblk_ref, swap_ref, hs_ref, ws_ref, wgl_ref, wul_ref, wdl_ref,
                        wgh_ref, wuh_ref, wdh_ref, y_ref):
    del elo_ref, ehi_ref
    j = pl.program_id(0)

    @pl.when(j < nblk_ref[0])
    def _():
        h = hs_ref[...].astype(BF16)
        swap = swap_ref[j] == 1
        gate_a = jnp.where(swap, ws_ref[:, 64:65], ws_ref[:, 0:1])
        gate_b = jnp.where(swap, ws_ref[:, 0:1], ws_ref[:, 64:65])
        acc = None
        for wg, wu, wd, gate in ((wgl_ref, wul_ref, wdl_ref, gate_a), (wgh_ref, wuh_ref, wdh_ref, gate_b)):
            g = _dot(h, wg[0, 0])
            u = _dot(h, wu[0, 0])
            hid = g * jax.nn.sigmoid(g) * u * gate
            y = _dot(hid, wd[0, 0])
            acc = y if acc is None else acc + y
        y_ref[...] = acc

    @pl.when(j >= nblk_ref[0])
    def _():
        y_ref[...] = jnp.zeros_like(y_ref)


def _moe_experts(hs, ws, blk_elo, blk_ehi, nblk, swap, w_gate, w_up, w_down, layer):
    last = lambda j, nb: jnp.minimum(j, nb[0] - 1)
    wspec = lambda a, b, which: pl.BlockSpec(
        (1, 1, a, b), lambda j, elo, ehi, nb, sw: (layer, (elo, ehi)[which][last(j, nb)], 0, 0))
    grid_spec = pltpu.PrefetchScalarGridSpec(
        num_scalar_prefetch=4,
        grid=(MOE_NBLK,),
        in_specs=[pl.BlockSpec((MOE_BM, D_MODEL), lambda j, elo, ehi, nb, sw: (last(j, nb), 0)),
                  pl.BlockSpec((MOE_BM, 128), lambda j, elo, ehi, nb, sw: (last(j, nb), 0)),
                  wspec(D_MODEL, D_FF, 0), wspec(D_MODEL, D_FF, 0), wspec(D_FF, D_MODEL, 0),
                  wspec(D_MODEL, D_FF, 1), wspec(D_MODEL, D_FF, 1), wspec(D_FF, D_MODEL, 1)],
        out_specs=pl.BlockSpec((MOE_BM, D_MODEL), lambda j, *_: (j, 0)),
    )
    return pl.pallas_call(
        _moe_experts_kernel,
        grid_spec=grid_spec,
        out_shape=jax.ShapeDtypeStruct((MOE_ROWS, D_MODEL), F32),
        compiler_params=_cparams("arbitrary"),
        name="moe_experts",
    )(blk_elo, blk_ehi, nblk, swap, hs, ws, w_gate, w_up, w_down, w_gate, w_up, w_down)


SC_ROWS = 64


def _sc_gather_rows(table, idx):
    info = plsc.get_sparse_core_info()
    nc, nw = info.num_cores, info.num_cores * info.num_subcores
    b, d = idx.shape[0], table.shape[1]
    per_w = b // nw
    assert per_w * nw == b and per_w % SC_ROWS == 0
    mesh = plsc.VectorSubcoreMesh(core_axis_name="c", subcore_axis_name="s")

    @functools.partial(
        pl.kernel, mesh=mesh, out_type=jax.ShapeDtypeStruct((b, d), table.dtype),
        scratch_types=[pltpu.VMEM((SC_ROWS,), jnp.int32), pltpu.VMEM((SC_ROWS, d), table.dtype),
                       pltpu.SemaphoreType.DMA])
    def gather(table_hbm, idx_hbm, out_hbm, idx_v, rows_v, sem):
        wid = lax.axis_index("s") * nc + lax.axis_index("c")
        for c in range(per_w // SC_ROWS):
            base = wid * per_w + c * SC_ROWS
            pltpu.sync_copy(idx_hbm.at[pl.ds(base, SC_ROWS)], idx_v)
            pltpu.async_copy(table_hbm.at[idx_v], rows_v, sem).wait()
            pltpu.sync_copy(rows_v, out_hbm.at[pl.ds(base, SC_ROWS)])

    return gather(table, idx)


def _moe_residual_kernel(x_ref, y_ref, gt_ref, o_ref):
    o_ref[...] = x_ref[...] + gt_ref[0] * y_ref[...]


def _moe_residual(x, y_tok, mod, layer):
    row = pl.BlockSpec((ROW_BLK, D_MODEL), lambda i: (i, 0))
    return pl.pallas_call(
        _moe_residual_kernel,
        grid=(N_ROW_BLK,),
        in_specs=[row, row, _mod_spec(layer, 5)],
        out_specs=row,
        out_shape=jax.ShapeDtypeStruct((N_TOK, D_MODEL), F32),
        compiler_params=_cparams("arbitrary"),
        name="moe_residual",
    )(x, y_tok, mod)


def _moe(x, mod, g, router_w, router_b, w_gate, w_up, w_down, layer):
    h, info, wtok, counts = _moe_route(x, mod, g, router_w, router_b, layer)
    cls = info[:, 0, :].reshape(N_TOK)
    rank = info[:, 1, :].reshape(N_TOK)
    cnt = counts[:N_CLASSES, 0]
    nb = (cnt + MOE_BM - 1) // MOE_BM
    ends = jnp.cumsum(nb)
    starts = ends - nb
    blk = jnp.arange(MOE_NBLK, dtype=jnp.int32)
    blk_cls = jnp.minimum(jnp.sum((blk[:, None] >= ends[None, :]).astype(jnp.int32), axis=1), N_CLASSES - 1)
    slot_a = jnp.asarray([s[0] for s in PAIR_SLOTS], jnp.int32)
    slot_b = jnp.asarray([s[1] for s in PAIR_SLOTS], jnp.int32)
    grp = blk_cls // N_PAIRS
    blk_ea = (grp * EXPERTS_PER_GROUP + slot_a[blk_cls % N_PAIRS]).astype(jnp.int32)
    blk_eb = (grp * EXPERTS_PER_GROUP + slot_b[blk_cls % N_PAIRS]).astype(jnp.int32)
    swap = (blk_ea > blk_eb).astype(jnp.int32)
    nblk = ends[-1:].astype(jnp.int32)
    hs, ws, pos = _moe_sort(h, wtok, cls, rank, (starts * MOE_BM).astype(jnp.int32), nblk)
    y_sorted = _moe_experts(hs, ws, blk_ea, blk_eb, nblk, swap, w_gate, w_up, w_down, layer)
    return _sc_gather_rows(y_sorted, pos)


def _final_norm_kernel(x_ref, y_ref, gt_ref, g_ref, o_ref):
    o_ref[...] = _rms(x_ref[...] + gt_ref[0] * y_ref[...]) * g_ref[...]


def _final_norm(x, y_tok, mod, layer, g, row0, rows):
    base = row0 // ROW_BLK
    row = pl.BlockSpec((ROW_BLK, D_MODEL), lambda i: (base + i, 0))
    gate = pl.BlockSpec((1, 1, D_MODEL), lambda i: ((layer * MOD_ROWS + _mod_group(base + i)) * N_MOD + 5, 0, 0))
    return pl.pallas_call(
        _final_norm_kernel,
        grid=(rows // ROW_BLK,),
        in_specs=[row, row, gate, pl.BlockSpec((1, D_MODEL), lambda i: (0, 0))],
        out_specs=pl.BlockSpec((ROW_BLK, D_MODEL), lambda i: (i, 0)),
        out_shape=jax.ShapeDtypeStruct((rows, D_MODEL), F32),
        compiler_params=_cparams("arbitrary"),
        name="final_norm",
    )(x, y_tok, mod, g.reshape(1, D_MODEL))


def kernel(x_prompt, x_sample, cache_attn_k, cache_attn_v, state_ssm_re, state_ssm_im, cache_mla_ckv, cache_mla_kpe,
           state_hgrn, c, c_ctx, w_mod, b_mod, g_mix, g_ffn, g_final, router_w, router_b, moe_w_gate, moe_w_up,
           moe_w_down, ab_w_in, ab_sink, s5_lam_re, s5_lam_im, s5_log_dt, s5_b_re, s5_b_im, s5_c_re, s5_c_im, s5_d,
           s5_w_glu, ab_w_out, cd_w_in, mla_g_q, mla_w_q_up, mla_g_kv, mla_w_kv_up, hg_lower_bounds, hg_g_o, cd_w_out):
    x = (x_prompt.reshape(N_CTX_TOK, D_MODEL), x_sample.reshape(N_LAT_TOK, D_MODEL))
    y_tok = None
    cond = jnp.zeros((MOD_ROWS, D_MODEL), F32).at[0].set(c_ctx).at[1:1 + DEC_BATCH].set(c)
    mod = _modulation(cond, w_mod, b_mod)
    keep = ([], [], [], [], [], [], [])
    for l in range(DEPTH):
        j = l // 2
        if l % 2 == 0:
            if y_tok is not None:
                x = _moe_residual(x, y_tok, mod, l - 1)
                x = (x[:N_CTX_TOK], x[N_CTX_TOK:])
            q, k, v, u = _ab_in(x, mod, g_mix[l], ab_w_in[j], l)
            o_a = _attention_a(q, k, v, cache_attn_k[:, j].reshape(DEC_BATCH, PAST_LEN, A_KV_WIDTH),
                               cache_attn_v[:, j].reshape(DEC_BATCH, PAST_LEN, A_KV_WIDTH), ab_sink[j])
            y, fin_re, fin_im = _s5_scan(u, state_ssm_re[:, j], state_ssm_im[:, j], s5_lam_re[j], s5_lam_im[j],
                                         s5_log_dt[j], s5_b_re[j], s5_b_im[j], s5_c_re[j], s5_c_im[j])
            x = _ab_out(x, o_a, y, u, s5_d[j], s5_w_glu[j], ab_w_out[j], mod, l)
            keep[0].append(k[:N_CTX_TOK].reshape(BATCH, SEQ, A_KV_HEADS, HEAD_DIM))
            keep[1].append(v[:N_CTX_TOK].reshape(BATCH, SEQ, A_KV_HEADS, HEAD_DIM))
            keep[2].append(fin_re)
            keep[3].append(fin_im)
        else:
            x, qf, ckv, kpe, kf, vm, hq, hff, hfb, hi, hg = _cd_in(
                x, y_tok, mod, g_mix[l], cd_w_in[j], mla_g_q[j], mla_w_q_up[j], mla_g_kv[j], mla_w_kv_up[j], l)
            kcf, vc = _mla_cache_kv(cache_mla_ckv[:, j].reshape(DEC_BATCH * PAST_LEN, MLA_KV_RANK),
                                    cache_mla_kpe[:, j].reshape(DEC_BATCH * PAST_LEN, MLA_ROPE), mla_w_kv_up[j])
            o_c = _attention_mla(qf, kf, vm, kcf, vc)
            s0_ctx = jnp.zeros((BATCH, 2, 2, HG_VAL, HG_HALF), F32)
            of_c, ob_c, s_fin = _hgrn_scan(hq, hff, hfb, hi, hg_lower_bounds, s0_ctx, l, 0, BATCH, SEQ)
            of_l, ob_l, _ = _hgrn_scan(hq, hff, hfb, hi, hg_lower_bounds, _hg_state_to_blocks(state_hgrn[:, j]), l,
                                       N_CTX_TOK, DEC_BATCH, DEC_SEQ)
            x = _cd_out(x, o_c, (of_c, of_l), (ob_c, ob_l), hg, hg_g_o[j], cd_w_out[j], mod, l)
            keep[4].append(ckv[:N_CTX_TOK].reshape(BATCH, SEQ, MLA_KV_RANK))
            keep[5].append(kpe[:N_CTX_TOK, :MLA_ROPE].reshape(BATCH, SEQ, MLA_ROPE))
            keep[6].append(_hg_state_from_blocks(s_fin))
        y_tok = _moe(x, mod, g_ffn[l], router_w, router_b, moe_w_gate, moe_w_up, moe_w_down, l)
    y_ctx = _final_norm(x, y_tok, mod, DEPTH - 1, g_final, 0, N_CTX_TOK)
    y_lat = _final_norm(x, y_tok, mod, DEPTH - 1, g_final, N_CTX_TOK, N_LAT_TOK)
    return (y_ctx.reshape(BATCH, SEQ, D_MODEL), y_lat.reshape(DEC_BATCH, DEC_SEQ, D_MODEL),
            jnp.stack(keep[0], 1), jnp.stack(keep[1], 1), jnp.stack(keep[2], 1), jnp.stack(keep[3], 1),
            jnp.stack(keep[4], 1), jnp.stack(keep[5], 1), jnp.stack(keep[6], 1))
```
